```python
import jax, jax.numpy as jnp
from jax import lax
import numpy as np

D_MODEL = 1024
BATCH = 8
SEQ = 2048
DEPTH = 1

N_META = 16
CHUNK = 128
D_MIX = 2 * D_MODEL
RET_WIDTH = D_MIX // 2
RET_HEADS = 4
RET_DV = RET_WIDTH // RET_HEADS
RET_DK = RET_DV // 2
FOX_WIDTH = D_MIX - RET_WIDTH
FOX_HEAD_DIM = 64
FOX_HEADS = FOX_WIDTH // FOX_HEAD_DIM
ROPE_BASE = 10000.0
EPS = 1e-6
NEG_INF = -1e30
SPLIT_SIZES = (RET_HEADS * RET_DK, RET_HEADS * RET_DK, RET_WIDTH, RET_WIDTH,
               FOX_WIDTH, FOX_WIDTH, FOX_WIDTH, FOX_WIDTH, FOX_HEADS)
D_IN_PROJ = 2 * RET_HEADS * RET_DK + 2 * RET_WIDTH + 4 * FOX_WIDTH + FOX_HEADS

kernel_name = "hymba_retention_fox_hybrid"


def rmsnorm(x, g):
    xf = x.astype(jnp.float32)
    y = xf * lax.rsqrt(jnp.mean(xf * xf, axis=-1, keepdims=True) + EPS)
    return (y * g.astype(jnp.float32)).astype(x.dtype)


def head_rmsnorm(x):
    xf = x.astype(jnp.float32)
    y = xf * lax.rsqrt(jnp.mean(xf * xf, axis=-1, keepdims=True) + EPS)
    return y.astype(x.dtype)


def rotary(x, pos):
    d = x.shape[-1]
    inv = ROPE_BASE ** (-jnp.arange(0, d, 2, dtype=jnp.float32) / d)
    ang = pos[:, None] * inv[None, :]
    cos = jnp.cos(ang)[None, :, None, :]
    sin = jnp.sin(ang)[None, :, None, :]
    xf = x.astype(jnp.float32)
    x1, x2 = xf[..., : d // 2], xf[..., d // 2:]
    out = jnp.concatenate([x1 * cos - x2 * sin, x1 * sin + x2 * cos], axis=-1)
    return out.astype(x.dtype)


def retention_chunkwise(q, k, v):
    b, t, h, dk = q.shape
    dv = v.shape[-1]
    nc = t // CHUNK
    log_gamma = jnp.log1p(-jnp.exp2(-5.0 - jnp.arange(h, dtype=jnp.float32)))
    q = q.reshape(b, nc, CHUNK, h, dk)
    k = k.reshape(b, nc, CHUNK, h, dk)
    v = v.reshape(b, nc, CHUNK, h, dv)
    idx = jnp.arange(CHUNK, dtype=jnp.float32)
    diff = idx[:, None] - idx[None, :]
    dmask = jnp.where(diff[None] >= 0,
                      jnp.exp(log_gamma[:, None, None] * jnp.maximum(diff, 0.0)[None]),
                      0.0)
    scores = jnp.einsum('bnchk,bnshk->bnhcs', q, k) * dmask.astype(q.dtype)
    out_intra = jnp.einsum('bnhcs,bnshv->bnchv', scores, v)
    zeta = jnp.exp(log_gamma[:, None] * (CHUNK - 1.0 - idx)[None, :])
    kv = jnp.einsum('bnchk,bnchv,hc->nbhkv', k, v, zeta.astype(k.dtype)).astype(jnp.float32)
    chunk_decay = jnp.exp(log_gamma * CHUNK)[None, :, None, None]

    def step(state, kv_i):
        return state * chunk_decay + kv_i, state

    _, s_prev = lax.scan(step, jnp.zeros((b, h, dk, dv), jnp.float32), kv)
    xi = jnp.exp(log_gamma[:, None] * (idx + 1.0)[None, :])
    out_inter = jnp.einsum('bnchk,nbhkv,hc->bnchv', q, s_prev.astype(q.dtype), xi.astype(q.dtype))
    return (out_intra + out_inter).reshape(b, t, h, dv)


def forgetting_attention(q, k, v, log_f, valid):
    b, t, h, d = q.shape
    c = jnp.cumsum(log_f, axis=1).transpose(0, 2, 1)
    scale = d ** -0.5
    outs = []
    for i in range(t // CHUNK):
        lo, hi = i * CHUNK, (i + 1) * CHUNK
        s = jnp.einsum('bthd,bshd->bhts', q[:, lo:hi], k[:, :hi]).astype(jnp.float32) * scale
        s = s + (c[:, :, lo:hi, None] - c[:, :, None, :hi])
        tpos = jnp.arange(lo, hi)
        spos = jnp.arange(hi)
        mask = (spos[None, :] <= tpos[:, None]) & valid[None, :hi]
        s = jnp.where(mask[None, None], s, NEG_INF)
        p = jax.nn.softmax(s, axis=-1)
        outs.append(jnp.einsum('bhts,bshd->bthd', p.astype(v.dtype), v[:, :hi]))
    return jnp.concatenate(outs, axis=1)


def hybrid_layer(h_res, norm_g, w_in, b_f, w_out):
    b, l, _ = h_res.shape
    pad = CHUNK - N_META
    t = l + pad
    u = rmsnorm(h_res, norm_g)
    u = jnp.pad(u, ((0, 0), (pad, 0), (0, 0)))
    z = u @ w_in
    offsets = np.cumsum(SPLIT_SIZES)[:-1].tolist()
    rq, rk, rv, rg, fq, fk, fv, fg, ff = jnp.split(z, offsets, axis=-1)
    pos = jnp.arange(t, dtype=jnp.float32) - pad
    valid = jnp.arange(t) >= pad
    rq = rotary(rq.reshape(b, t, RET_HEADS, RET_DK), pos)
    rk = rotary(rk.reshape(b, t, RET_HEADS, RET_DK), pos) * (RET_DK ** -0.5)
    y_r = retention_chunkwise(rq, rk, rv.reshape(b, t, RET_HEADS, RET_DV))
    y_r = head_rmsnorm(y_r).reshape(b, t, RET_WIDTH) * jax.nn.silu(rg)
    log_f = jnp.where(valid[None, :, None],
                      jax.nn.log_sigmoid((ff + b_f).astype(jnp.float32)), 0.0)
    y_f = forgetting_attention(fq.reshape(b, t, FOX_HEADS, FOX_HEAD_DIM),
                               fk.reshape(b, t, FOX_HEADS, FOX_HEAD_DIM),
                               fv.reshape(b, t, FOX_HEADS, FOX_HEAD_DIM),
                               log_f, valid)
    y_f = y_f.reshape(b, t, FOX_WIDTH) * jax.nn.silu(fg)
    y = jnp.concatenate([y_r, y_f], axis=-1)[:, pad:]
    return h_res + y @ w_out


def _fwd_setup_inputs(seed: int = 0) -> dict:
    key = jax.random.key(seed)
    k = jax.random.split(key, 7)
    x = jax.random.normal(k[0], (BATCH, SEQ, D_MODEL), jnp.float32)
    meta_tokens = jax.random.normal(k[1], (N_META, D_MODEL), jnp.float32)
    norm_g = 1.0 + 0.02 * jax.random.normal(k[2], (DEPTH, D_MODEL), jnp.float32)
    w_in = jax.random.normal(k[3], (DEPTH, D_MODEL, D_IN_PROJ), jnp.float32) * D_MODEL ** -0.5
    b_f = jax.random.uniform(k[4], (DEPTH, FOX_HEADS), jnp.float32, 1.0, 4.0)
    w_out = jax.random.normal(k[5], (DEPTH, D_MIX, D_MODEL), jnp.float32) * D_MIX ** -0.5
    final_g = 1.0 + 0.02 * jax.random.normal(k[6], (D_MODEL,), jnp.float32)
    return {"x": x, "meta_tokens": meta_tokens, "norm_g": norm_g, "w_in": w_in,
            "b_f": b_f, "w_out": w_out, "final_g": final_g}


def _fwd_reference(x, meta_tokens, norm_g, w_in, b_f, w_out, final_g):
    b = x.shape[0]
    meta = jnp.broadcast_to(meta_tokens[None].astype(x.dtype), (b, N_META, D_MODEL))
    h = jnp.concatenate([meta, x], axis=1)
    for layer in range(DEPTH):
        h = hybrid_layer(h, norm_g[layer], w_in[layer], b_f[layer], w_out[layer])
    return rmsnorm(h[:, N_META:], final_g)


import jax as _jax
import jax.numpy as _jnp

TWIN_FORMAT = 'train_step'
FWD_PARAMS = ['x', 'meta_tokens', 'norm_g', 'w_in', 'b_f', 'w_out', 'final_g']
TWIN_WEIGHTS = ['meta_tokens', 'norm_g', 'w_in', 'b_f', 'w_out', 'final_g']
TWIN_DIFF_INPUT = 'x'
TWIN_INPUTS = ['x', 'meta_tokens', 'norm_g', 'w_in', 'b_f', 'w_out', 'final_g', 'loss_target', 'm_meta_tokens', 'm_norm_g', 'm_w_in', 'm_b_f', 'm_w_out', 'm_final_g', 'v_meta_tokens', 'v_norm_g', 'v_w_in', 'v_b_f', 'v_w_out', 'v_final_g']
TWIN_OUTPUTS = ['loss', 'grad_x', 'grad_meta_tokens', 'grad_norm_g', 'grad_w_in', 'grad_b_f', 'grad_w_out', 'grad_final_g', 'delta_meta_tokens', 'delta_norm_g', 'delta_w_in', 'delta_b_f', 'delta_w_out', 'delta_final_g', 'new_m_meta_tokens', 'new_m_norm_g', 'new_m_w_in', 'new_m_b_f', 'new_m_w_out', 'new_m_final_g', 'new_v_meta_tokens', 'new_v_norm_g', 'new_v_w_in', 'new_v_b_f', 'new_v_w_out', 'new_v_final_g']
TWIN_LEAF_KINDS = {'loss': 'loss', 'grad_x': 'grad_x', 'grad_meta_tokens': 'grad_w', 'grad_norm_g': 'grad_w', 'grad_w_in': 'grad_w', 'grad_b_f': 'grad_w', 'grad_w_out': 'grad_w', 'grad_final_g': 'grad_w', 'delta_meta_tokens': 'delta_w', 'delta_norm_g': 'delta_w', 'delta_w_in': 'delta_w', 'delta_b_f': 'delta_w', 'delta_w_out': 'delta_w', 'delta_final_g': 'delta_w', 'new_m_meta_tokens': 'new_m', 'new_m_norm_g': 'new_m', 'new_m_w_in': 'new_m', 'new_m_b_f': 'new_m', 'new_m_w_out': 'new_m', 'new_m_final_g': 'new_m', 'new_v_meta_tokens': 'new_v', 'new_v_norm_g': 'new_v', 'new_v_w_in': 'new_v', 'new_v_b_f': 'new_v', 'new_v_w_out': 'new_v', 'new_v_final_g': 'new_v'}


def _forward(args):
    return _fwd_reference(*[args[k] for k in FWD_PARAMS])


def _output_shape():
    out = _jax.eval_shape(lambda: _forward(_fwd_setup_inputs(0)))
    return out.shape, out.dtype

N_MICROBATCH = 1
ADAM_LR = 0.001
ADAM_B1 = 0.9
ADAM_B2 = 0.999
ADAM_EPS = 1e-08
ADAM_WD = 0.01
ADAM_STEP = 10
PER_EXAMPLE_BATCH_AXIS = {'x': 0, 'loss_target': 0}
SHARED_INPUTS = []
_WEIGHT_DTYPES = {'meta_tokens': _jnp.float32, 'norm_g': _jnp.float32, 'w_in': _jnp.float32, 'b_f': _jnp.float32, 'w_out': _jnp.float32, 'final_g': _jnp.float32}
MOMENT_SCALE = {'meta_tokens': 5.162122e-03, 'norm_g': 1.055557e-01, 'w_in': 3.900800e-02, 'b_f': 8.723227e-02, 'w_out': 5.149641e-02, 'final_g': 1.598136e+01}


def _to_microbatches(a, axis):
    t = _jnp.moveaxis(a, axis, 0)
    t = t.reshape((N_MICROBATCH, t.shape[0] // N_MICROBATCH) + t.shape[1:])
    return _jnp.moveaxis(t, 1, axis + 1)


def setup_inputs(seed: int = 0) -> dict:
    inp = _fwd_setup_inputs(seed)
    key = _jax.random.fold_in(_jax.random.key(seed), 7919)
    shape, _ = _output_shape()
    out = dict(inp)
    out["loss_target"] = _jax.random.normal(_jax.random.fold_in(key, 0), shape, _jnp.float32)
    for i, name in enumerate(TWIN_WEIGHTS):
        w = inp[name].astype(_jnp.float32)
        if MOMENT_SCALE is None:
            s = _jnp.sqrt(_jnp.mean(_jnp.square(w)) + 1e-30)
        else:
            s = MOMENT_SCALE[name]
        km, kv = _jax.random.split(_jax.random.fold_in(key, i + 1))
        out[name] = w
        out["m_" + name] = s * _jax.random.normal(km, w.shape, _jnp.float32)
        out["v_" + name] = (s * s) * _jax.random.uniform(kv, w.shape, _jnp.float32, 0.5, 1.5)
    if N_MICROBATCH > 1:
        for name, axis in PER_EXAMPLE_BATCH_AXIS.items():
            out[name] = _to_microbatches(out[name], axis)
    return {'x': out['x'], 'meta_tokens': out['meta_tokens'], 'norm_g': out['norm_g'], 'w_in': out['w_in'], 'b_f': out['b_f'], 'w_out': out['w_out'], 'final_g': out['final_g'], 'loss_target': out['loss_target'], 'm_meta_tokens': out['m_meta_tokens'], 'm_norm_g': out['m_norm_g'], 'm_w_in': out['m_w_in'], 'm_b_f': out['m_b_f'], 'm_w_out': out['m_w_out'], 'm_final_g': out['m_final_g'], 'v_meta_tokens': out['v_meta_tokens'], 'v_norm_g': out['v_norm_g'], 'v_w_in': out['v_w_in'], 'v_b_f': out['v_b_f'], 'v_w_out': out['v_w_out'], 'v_final_g': out['v_final_g']}


def _loss(weights, diff, rest, loss_target):
    with _jax.named_scope("forward"):
        args = {**rest, TWIN_DIFF_INPUT: diff, **{k: w.astype(_WEIGHT_DTYPES[k]) for k, w in weights.items()}}
        y = _forward(args)
    with _jax.named_scope("loss_head"):
        err = _jnp.square(y.astype(_jnp.float32) - loss_target)
        return 0.5 * _jnp.sum(_jnp.mean(err, axis=-1)) if err.ndim else 0.5 * err


def _adamw(w, g, m, v):
    m = ADAM_B1 * m + (1.0 - ADAM_B1) * g
    v = ADAM_B2 * v + (1.0 - ADAM_B2) * _jnp.square(g)
    m_hat = m / (1.0 - ADAM_B1 ** ADAM_STEP)
    v_hat = v / (1.0 - ADAM_B2 ** ADAM_STEP)
    delta = -ADAM_LR * (m_hat / (_jnp.sqrt(v_hat) + ADAM_EPS) + ADAM_WD * w)
    return delta, m, v


def reference(x, meta_tokens, norm_g, w_in, b_f, w_out, final_g, loss_target, m_meta_tokens, m_norm_g, m_w_in, m_b_f, m_w_out, m_final_g, v_meta_tokens, v_norm_g, v_w_in, v_b_f, v_w_out, v_final_g):
    given = dict(x=x, meta_tokens=meta_tokens, norm_g=norm_g, w_in=w_in, b_f=b_f, w_out=w_out, final_g=final_g, loss_target=loss_target, m_meta_tokens=m_meta_tokens, m_norm_g=m_norm_g, m_w_in=m_w_in, m_b_f=m_b_f, m_w_out=m_w_out, m_final_g=m_final_g, v_meta_tokens=v_meta_tokens, v_norm_g=v_norm_g, v_w_in=v_w_in, v_b_f=v_b_f, v_w_out=v_w_out, v_final_g=v_final_g)
    weights = {n: given[n] for n in TWIN_WEIGHTS}
    shared = {n: given[n] for n in SHARED_INPUTS}
    per_example = {n: given[n] for n in ['x']}
    grad_fn = _jax.value_and_grad(_loss, argnums=(0, 1))

    def one_microbatch(ex, loss_target):
        ex = dict(ex)
        diff = ex.pop(TWIN_DIFF_INPUT)
        return grad_fn(weights, diff, {**shared, **ex}, loss_target)

    if N_MICROBATCH == 1:
        loss, (grad_w, grad_x) = one_microbatch(per_example, given["loss_target"])
    else:
        def body(carry, xs):
            loss_sum, grad_sum = carry
            l_k, (gw_k, gx_k) = one_microbatch(xs[0], xs[1])
            with _jax.named_scope("update"):
                return (loss_sum + l_k, _jax.tree.map(_jnp.add, grad_sum, gw_k)), gx_k

        init = (_jnp.zeros((), _jnp.float32), _jax.tree.map(_jnp.zeros_like, weights))
        (loss, grad_w), grad_x = _jax.lax.scan(body, init, (per_example, given["loss_target"]))
    with _jax.named_scope("update"):
        delta_w, new_m, new_v = {}, {}, {}
        for n in TWIN_WEIGHTS:
            delta_w[n], new_m[n], new_v[n] = _adamw(weights[n], grad_w[n], given["m_" + n], given["v_" + n])
    return (loss, grad_x, *[grad_w[n] for n in TWIN_WEIGHTS], *[delta_w[n] for n in TWIN_WEIGHTS],
            *[new_m[n] for n in TWIN_WEIGHTS], *[new_v[n] for n in TWIN_WEIGHTS])
```

```python
import numpy as np
import jax
import jax.numpy as jnp
from jax import lax
from jax.experimental import pallas as pl
from jax.experimental.pallas import tpu as pltpu

F32 = jnp.float32
BF16 = jnp.bfloat16

N_DEV = 8
D_MODEL = 1024
SEQ = 2048
N_META = 16
CHUNK = 128
PAD = CHUNK - N_META
T = SEQ + CHUNK
NCHUNK = T // CHUNK
D_MIX = 2048
RET_HEADS = 4
RET_DK = 128
RET_DV = 256
FOX_HEADS = 16
FOX_D = 64
D_IN = 7184
D_IN_PAD = 7296
W_BLK = D_IN // N_DEV
WO_BLK = D_MIX // N_DEV
META_BLK = D_MODEL // N_DEV
EPS = 1e-6
NEG_INF = -1e30
ROPE_BASE = 10000.0

COL_RQ, COL_RK, COL_RV, COL_RG = 0, 4, 8, 16
COL_FQ, COL_FK, COL_FV, COL_FG, COL_FF = 24, 32, 40, 48, 56

ADAM_LR = 0.001
ADAM_B1 = 0.9
ADAM_B2 = 0.999
ADAM_EPS = 1e-08
ADAM_WD = 0.01
ADAM_STEP = 10

SMALL_ROWS = 24
VMEM_LIMIT = 56 * 1024 * 1024
MESH_AXES = ("x", "y", "c")
MESH = pl.DeviceIdType.MESH

_NT = (((1,), (1,)), ((), ()))
_TN = (((0,), (0,)), ((), ()))


def _dot(a, b):
    return jnp.dot(a, b, preferred_element_type=F32)


def _dot_nt(a, b):
    return lax.dot_general(a, b, _NT, preferred_element_type=F32)


def _dot_tn(a, b):
    return lax.dot_general(a, b, _TN, preferred_element_type=F32)


def _params(**kw):
    return pltpu.CompilerParams(vmem_limit_bytes=VMEM_LIMIT, **kw)


def _silu_parts(g):
    sig = jax.nn.sigmoid(g)
    return g * sig, sig * (1.0 + g * (1.0 - sig))


def _tables():
    pos = np.arange(T, dtype=np.float32) - PAD
    inv = (ROPE_BASE ** (-np.arange(0, RET_DK, 2, dtype=np.float32) / RET_DK)).astype(np.float32)
    ang = pos[:, None] * inv[None, :]
    cos, sin = np.cos(ang), np.sin(ang)
    cosf = np.concatenate([cos, cos], axis=1).astype(np.float32)
    sins = np.concatenate([-sin, sin], axis=1).astype(np.float32)
    h = np.arange(RET_HEADS, dtype=np.float32)
    log_gamma = np.log1p(-np.exp2(-5.0 - h)).astype(np.float32)
    idx = np.arange(CHUNK, dtype=np.float32)
    diff = idx[:, None] - idx[None, :]
    dmask = np.where(diff[None] >= 0,
                     np.exp(log_gamma[:, None, None] * np.maximum(diff, 0.0)[None]), 0.0)
    zeta = np.exp(log_gamma[:, None] * (CHUNK - 1.0 - idx)[None, :])
    xi = np.exp(log_gamma[:, None] * (idx + 1.0)[None, :])
    cdec = np.exp(log_gamma * CHUNK)
    return (jnp.asarray(cosf), jnp.asarray(sins), jnp.asarray(dmask, F32),
            jnp.asarray(zeta[:, :, None], F32), jnp.asarray(xi[:, :, None], F32),
            jnp.asarray(cdec[:, None, None], F32))


def _mm(a, b, *, nt, tm, tn, out_dtype, name, add=None):
    m, k = a.shape
    n = b.shape[0] if nt else b.shape[1]
    assert m % tm == 0 and n % tn == 0

    def body(a_ref, b_ref, *rest):
        o_ref = rest[-1]
        prod = _dot_nt(a_ref[...], b_ref[...]) if nt else _dot(a_ref[...], b_ref[...])
        if add is not None:
            prod = prod + rest[0][...]
        o_ref[...] = prod.astype(out_dtype)

    in_specs = [pl.BlockSpec((tm, k), lambda j, i: (i, 0)),
                pl.BlockSpec((tn, k), lambda j, i: (j, 0)) if nt
                else pl.BlockSpec((k, tn), lambda j, i: (0, j))]
    args = [a, b]
    if add is not None:
        in_specs.append(pl.BlockSpec((tm, tn), lambda j, i: (i, j)))
        args.append(add)
    return pl.pallas_call(
        body, name=name, grid=(n // tn, m // tm),
        in_specs=in_specs,
        out_specs=pl.BlockSpec((tm, tn), lambda j, i: (i, j)),
        out_shape=jax.ShapeDtypeStruct((m, n), out_dtype),
        compiler_params=_params(dimension_semantics=("arbitrary", "arbitrary")),
    )(*args)


def _rms_fwd(h_pad, g):
    def body(h_ref, g_ref, u_ref):
        h = h_ref[...]
        r = lax.rsqrt(jnp.mean(h * h, axis=-1, keepdims=True) + EPS)
        u_ref[...] = (h * r * g_ref[...]).astype(BF16)

    return pl.pallas_call(
        body, name="rms_fwd", grid=(NCHUNK,),
        in_specs=[pl.BlockSpec((CHUNK, D_MODEL), lambda i: (i, 0)),
                  pl.BlockSpec((1, D_MODEL), lambda i: (0, 0))],
        out_specs=pl.BlockSpec((CHUNK, D_MODEL), lambda i: (i, 0)),
        out_shape=jax.ShapeDtypeStruct((T, D_MODEL), BF16),
        compiler_params=_params(dimension_semantics=("arbitrary",)),
    )(h_pad, g)


def _final_loss(out_pad, target, g):
    def body(o_ref, t_ref, g_ref, d_ref, db_ref, loss_ref, dg_ref):
        i = pl.program_id(0)

        @pl.when(i == 0)
        def _():
            d_ref[...] = jnp.zeros_like(d_ref)
            db_ref[...] = jnp.zeros_like(db_ref)
            loss_ref[...] = jnp.zeros_like(loss_ref)
            dg_ref[...] = jnp.zeros_like(dg_ref)

        @pl.when(i > 0)
        def _():
            o = o_ref[...]
            g = g_ref[...]
            r = lax.rsqrt(jnp.mean(o * o, axis=-1, keepdims=True) + EPS)
            xn = o * r
            e = xn * g - t_ref[...]
            loss_ref[...] += jnp.full(loss_ref.shape, 0.5 / D_MODEL * jnp.sum(e * e), F32)
            do = e * (1.0 / D_MODEL)
            dg_ref[...] += jnp.sum(do * xn, axis=0, keepdims=True)
            dn = do * g
            d = r * (dn - xn * jnp.mean(dn * xn, axis=-1, keepdims=True))
            d_ref[...] = d
            db_ref[...] = d.astype(BF16)

    return pl.pallas_call(
        body, name="final_loss", grid=(NCHUNK,),
        in_specs=[pl.BlockSpec((CHUNK, D_MODEL), lambda i: (i, 0)),
                  pl.BlockSpec((CHUNK, D_MODEL), lambda i: (jnp.maximum(i - 1, 0), 0)),
                  pl.BlockSpec((1, D_MODEL), lambda i: (0, 0))],
        out_specs=[pl.BlockSpec((CHUNK, D_MODEL), lambda i: (i, 0)),
                   pl.BlockSpec((CHUNK, D_MODEL), lambda i: (i, 0)),
                   pl.BlockSpec((8, 128), lambda i: (0, 0)),
                   pl.BlockSpec((1, D_MODEL), lambda i: (0, 0))],
        out_shape=[jax.ShapeDtypeStruct((T, D_MODEL), F32),
                   jax.ShapeDtypeStruct((T, D_MODEL), BF16),
                   jax.ShapeDtypeStruct((8, 128), F32),
                   jax.ShapeDtypeStruct((1, D_MODEL), F32)],
        compiler_params=_params(dimension_semantics=("arbitrary",)),
    )(out_pad, target, g)


def _rms_bwd(du, h_pad, dout, g):
    def body(du_ref, h_ref, d_ref, g_ref, dh_ref, dg_ref):
        i = pl.program_id(0)

        @pl.when(i == 0)
        def _():
            dg_ref[...] = jnp.zeros_like(dg_ref)

        h = h_ref[...]
        du_ = du_ref[...]
        r = lax.rsqrt(jnp.mean(h * h, axis=-1, keepdims=True) + EPS)
        xn = h * r
        dg_ref[...] += jnp.sum(du_ * xn, axis=0, keepdims=True)
        dn = du_ * g_ref[...]
        dh_ref[...] = d_ref[...] + r * (dn - xn * jnp.mean(dn * xn, axis=-1, keepdims=True))

    return pl.pallas_call(
        body, name="rms_bwd", grid=(NCHUNK,),
        in_specs=[pl.BlockSpec((CHUNK, D_MODEL), lambda i: (i, 0)),
                  pl.BlockSpec((CHUNK, D_MODEL), lambda i: (i, 0)),
                  pl.BlockSpec((CHUNK, D_MODEL), lambda i: (i, 0)),
                  pl.BlockSpec((1, D_MODEL), lambda i: (0, 0))],
        out_specs=[pl.BlockSpec((CHUNK, D_MODEL), lambda i: (i, 0)),
                   pl.BlockSpec((1, D_MODEL), lambda i: (0, 0))],
        out_shape=[jax.ShapeDtypeStruct((T, D_MODEL), F32),
                   jax.ShapeDtypeStruct((1, D_MODEL), F32)],
        compiler_params=_params(dimension_semantics=("arbitrary",)),
    )(du, h_pad, dout, g)


def _tri(lower):
    r = lax.broadcasted_iota(jnp.int32, (CHUNK, CHUNK), 0)
    c = lax.broadcasted_iota(jnp.int32, (CHUNK, CHUNK), 1)
    return jnp.where((r >= c) if lower else (r <= c), 1.0, 0.0).astype(F32)


def _row_valid(n):
    r = lax.broadcasted_iota(jnp.int32, (CHUNK, 128), 0) + n * CHUNK
    return r >= PAD


def _forget_fwd(z, b_pad):
    def body(z_ref, b_ref, c_ref):
        tri = _tri(True)
        carry = jnp.zeros((1, 128), F32)
        for n in range(NCHUNK):
            rows = pl.ds(n * CHUNK, CHUNK)
            a = z_ref[rows, :] + b_ref[...]
            lf = -(jnp.maximum(-a, 0.0) + jnp.log(1.0 + jnp.exp(-jnp.abs(a))))
            lf = jnp.where(_row_valid(n), lf, 0.0)
            c = jnp.dot(tri, lf, precision=lax.Precision.HIGHEST,
                        preferred_element_type=F32) + carry
            c_ref[rows, :] = c
            carry = c[CHUNK - 1:CHUNK, :]

    return pl.pallas_call(
        body, name="forget_fwd", grid=(1,),
        in_specs=[pl.BlockSpec((T, 128), lambda i: (0, COL_FF)),
                  pl.BlockSpec((1, 128), lambda i: (0, 0))],
        out_specs=pl.BlockSpec((T, 128), lambda i: (0, 0)),
        out_shape=jax.ShapeDtypeStruct((T, 128), F32),
        compiler_params=_params(dimension_semantics=("arbitrary",)),
    )(z, b_pad)


def _forget_bwd(z, b_pad, dc):
    def body(z_ref, b_ref, dc_ref, dff_ref, db_ref):
        tri = _tri(False)
        carry = jnp.zeros((1, 128), F32)
        db = jnp.zeros((1, 128), F32)
        for n in reversed(range(NCHUNK)):
            rows = pl.ds(n * CHUNK, CHUNK)
            dlf = jnp.dot(tri, dc_ref[rows, :], precision=lax.Precision.HIGHEST,
                          preferred_element_type=F32) + carry
            carry = dlf[0:1, :]
            a = z_ref[rows, :] + b_ref[...]
            dff = jnp.where(_row_valid(n), dlf * jax.nn.sigmoid(-a), 0.0)
            dff_ref[rows, :] = dff.astype(BF16)
            db = db + jnp.sum(dff, axis=0, keepdims=True)
        db_ref[...] = db

    return pl.pallas_call(
        body, name="forget_bwd", grid=(1,),
        in_specs=[pl.BlockSpec((T, 128), lambda i: (0, COL_FF)),
                  pl.BlockSpec((1, 128), lambda i: (0, 0)),
                  pl.BlockSpec((T, 128), lambda i: (0, 0))],
        out_specs=[pl.BlockSpec((T, 128), lambda i: (0, 0)),
                   pl.BlockSpec((1, 128), lambda i: (0, 0))],
        out_shape=[jax.ShapeDtypeStruct((T, 128), BF16),
                   jax.ShapeDtypeStruct((1, 128), F32)],
        compiler_params=_params(dimension_semantics=("arbitrary",)),
    )(z, b_pad, dc)


def _fox_probs(q_blk, k_all, c_col, c_row, i):
    lo, hi = i * CHUNK, (i + 1) * CHUNK
    s = _dot_nt(q_blk, k_all[:hi]) * (FOX_D ** -0.5) + (c_col - c_row[:, :hi])
    tpos = lax.broadcasted_iota(jnp.int32, (CHUNK, hi), 0) + lo
    spos = lax.broadcasted_iota(jnp.int32, (CHUNK, hi), 1)
    s = jnp.where((spos <= tpos) & (spos >= PAD), s, NEG_INF)
    e = jnp.exp(s - jnp.max(s, axis=-1, keepdims=True))
    return e / jnp.sum(e, axis=-1, keepdims=True)


def _fox_fwd(z, c_col, c_row):
    def body(q_ref, k_ref, v_ref, g_ref, cc_ref, cr_ref, a_ref, y_ref):
        a_ref[pl.ds(0, CHUNK), :] = jnp.zeros((CHUNK, 128), F32)
        y_ref[pl.ds(0, CHUNK), :] = jnp.zeros((CHUNK, 128), BF16)
        for j in range(2):
            lanes = pl.ds(j * FOX_D, FOX_D)
            k_all = k_ref[:, lanes].astype(BF16)
            v_all = v_ref[:, lanes].astype(BF16)
            c_row = cr_ref[j]
            for i in range(1, NCHUNK):
                rows = pl.ds(i * CHUNK, CHUNK)
                p = _fox_probs(q_ref[rows, lanes].astype(BF16), k_all, cc_ref[j, rows, :], c_row, i)
                o = _dot(p.astype(BF16), v_all[:(i + 1) * CHUNK])
                a_ref[rows, lanes] = o
                y_ref[rows, lanes] = (o * _silu_parts(g_ref[rows, lanes])[0]).astype(BF16)

    col = lambda base: pl.BlockSpec((T, 128), lambda p: (0, base + p))
    return pl.pallas_call(
        body, name="fox_fwd", grid=(FOX_HEADS // 2,),
        in_specs=[col(COL_FQ), col(COL_FK), col(COL_FV), col(COL_FG),
                  pl.BlockSpec((2, T, 1), lambda p: (p, 0, 0)),
                  pl.BlockSpec((2, 1, T), lambda p: (p, 0, 0))],
        out_specs=[col(0), col(0)],
        out_shape=[jax.ShapeDtypeStruct((T, FOX_HEADS * FOX_D), F32),
                   jax.ShapeDtypeStruct((T, FOX_HEADS * FOX_D), BF16)],
        compiler_params=_params(dimension_semantics=("arbitrary",)),
    )(z, z, z, z, c_col, c_row)


def _fox_bwd(z, c_col, c_row, a_f, dy):
    scale = FOX_D ** -0.5

    def body(q_ref, k_ref, v_ref, g_ref, cc_ref, cr_ref, a_ref, dy_ref,
             dq_ref, dk_ref, dv_ref, dg_ref, dc_ref, dk_acc, dv_acc, dc_acc):
        dq_ref[pl.ds(0, CHUNK), :] = jnp.zeros((CHUNK, 128), BF16)
        dg_ref[pl.ds(0, CHUNK), :] = jnp.zeros((CHUNK, 128), BF16)
        for j in range(2):
            lanes = pl.ds(j * FOX_D, FOX_D)
            k_all = k_ref[:, lanes].astype(BF16)
            v_all = v_ref[:, lanes].astype(BF16)
            c_row = cr_ref[j]
            dk_acc[...] = jnp.zeros_like(dk_acc)
            dv_acc[...] = jnp.zeros_like(dv_acc)
            dc_acc[...] = jnp.zeros_like(dc_acc)
            for i in range(1, NCHUNK):
                rows = pl.ds(i * CHUNK, CHUNK)
                keys = pl.ds(0, (i + 1) * CHUNK)
                hi = (i + 1) * CHUNK
                q_blk = q_ref[rows, lanes].astype(BF16)
                p = _fox_probs(q_blk, k_all, cc_ref[j, rows, :], c_row, i)
                g = g_ref[rows, lanes]
                sg, dsg = _silu_parts(g)
                dyj = dy_ref[rows, lanes]
                a = a_ref[rows, lanes]
                dg_ref[rows, lanes] = (dyj * a * dsg).astype(BF16)
                do = dyj * sg
                do_b = do.astype(BF16)
                dp = _dot_nt(do_b, v_all[:hi])
                ds = p * (dp - jnp.sum(p * dp, axis=-1, keepdims=True))
                dc_acc[:, keys] -= jnp.sum(ds, axis=0, keepdims=True)
                ds_b = ds.astype(BF16)
                dq_ref[rows, lanes] = (_dot(ds_b, k_all[:hi]) * scale).astype(BF16)
                dk_acc[keys, :] += _dot_tn(ds_b, q_blk) * scale
                dv_acc[keys, :] += _dot_tn(p.astype(BF16), do_b)
            dk_ref[:, lanes] = dk_acc[...].astype(BF16)
            dv_ref[:, lanes] = dv_acc[...].astype(BF16)
            dc_ref[j] = dc_acc[...]

    col = lambda base: pl.BlockSpec((T, 128), lambda p: (0, base + p))
    w = FOX_HEADS * FOX_D
    return pl.pallas_call(
        body, name="fox_bwd", grid=(FOX_HEADS // 2,),
        in_specs=[col(COL_FQ), col(COL_FK), col(COL_FV), col(COL_FG),
                  pl.BlockSpec((2, T, 1), lambda p: (p, 0, 0)),
                  pl.BlockSpec((2, 1, T), lambda p: (p, 0, 0)),
                  col(0), col(8)],
        out_specs=[col(0), col(0), col(0), col(0),
                   pl.BlockSpec((2, 1, T), lambda p: (p, 0, 0))],
        out_shape=[jax.ShapeDtypeStruct((T, w), BF16)] * 4
        + [jax.ShapeDtypeStruct((FOX_HEADS, 1, T), F32)],
        scratch_shapes=[pltpu.VMEM((T, FOX_D), F32), pltpu.VMEM((T, FOX_D), F32),
                        pltpu.VMEM((1, T), F32)],
        compiler_params=_params(dimension_semantics=("arbitrary",)),
    )(z, z, z, z, c_col, c_row, a_f, dy)


def _rot(x, cosf, sins):
    return x * cosf + pltpu.roll(x, RET_DK // 2, 1) * sins


def _rot_t(d, cosf, sins):
    return d * cosf - pltpu.roll(d, RET_DK // 2, 1) * sins


_RET_TABLE_SPECS = [
    pl.BlockSpec((T, RET_DK), lambda h: (0, 0)),
    pl.BlockSpec((T, RET_DK), lambda h: (0, 0)),
    pl.BlockSpec((1, CHUNK, CHUNK), lambda h: (h, 0, 0)),
    pl.BlockSpec((1, CHUNK, 1), lambda h: (h, 0, 0)),
    pl.BlockSpec((1, CHUNK, 1), lambda h: (h, 0, 0)),
    pl.BlockSpec((1, 1, 1), lambda h: (h, 0, 0)),
]


def _ret_specs():
    return [pl.BlockSpec((T, RET_DK), lambda h: (0, COL_RQ + h)),
            pl.BlockSpec((T, RET_DK), lambda h: (0, COL_RK + h)),
            pl.BlockSpec((T, RET_DV), lambda h: (0, COL_RV // 2 + h)),
            pl.BlockSpec((T, RET_DV), lambda h: (0, COL_RG // 2 + h))]


def _ret_fwd(z, tables):
    kscale = RET_DK ** -0.5

    def body(q_ref, k_ref, v_ref, g_ref, cos_ref, sin_ref, dm_ref, zeta_ref, xi_ref, cd_ref,
             raw_ref, y_ref):
        dmask, zeta, xi, cdec = dm_ref[0], zeta_ref[0], xi_ref[0], cd_ref[0]
        state = jnp.zeros((RET_DK, RET_DV), F32)
        for n in range(NCHUNK):
            rows = pl.ds(n * CHUNK, CHUNK)
            cosf, sins = cos_ref[rows, :], sin_ref[rows, :]
            qr = _rot(q_ref[rows, :], cosf, sins)
            kr_b = (_rot(k_ref[rows, :], cosf, sins) * kscale).astype(BF16)
            v = v_ref[rows, :]
            a = _dot_nt(qr.astype(BF16), kr_b) * dmask
            out = _dot(a.astype(BF16), v.astype(BF16)) + _dot((qr * xi).astype(BF16), state.astype(BF16))
            state = state * cdec + _dot_tn(kr_b, (v * zeta).astype(BF16))
            raw_ref[rows, :] = out
            r = lax.rsqrt(jnp.mean(out * out, axis=-1, keepdims=True) + EPS)
            y_ref[rows, :] = (out * r * _silu_parts(g_ref[rows, :])[0]).astype(BF16)

    wide = pl.BlockSpec((T, RET_DV), lambda h: (0, h))
    return pl.pallas_call(
        body, name="ret_fwd", grid=(RET_HEADS,),
        in_specs=_ret_specs() + _RET_TABLE_SPECS,
        out_specs=[wide, wide],
        out_shape=[jax.ShapeDtypeStruct((T, RET_HEADS * RET_DV), F32),
                   jax.ShapeDtypeStruct((T, RET_HEADS * RET_DV), BF16)],
        compiler_params=_params(dimension_semantics=("arbitrary",)),
    )(z, z, z, z, *tables)


def _ret_bwd(z, tables, raw, dy):
    kscale = RET_DK ** -0.5

    def body(q_ref, k_ref, v_ref, g_ref, cos_ref, sin_ref, dm_ref, zeta_ref, xi_ref, cd_ref,
             raw_ref, dy_ref, dq_ref, dk_ref, dv_ref, dg_ref, st_ref):
        dmask, zeta, xi, cdec = dm_ref[0], zeta_ref[0], xi_ref[0], cd_ref[0]

        def rotated(n):
            rows = pl.ds(n * CHUNK, CHUNK)
            cosf, sins = cos_ref[rows, :], sin_ref[rows, :]
            qr = _rot(q_ref[rows, :], cosf, sins)
            kr_b = (_rot(k_ref[rows, :], cosf, sins) * kscale).astype(BF16)
            return rows, cosf, sins, qr, kr_b

        state = jnp.zeros((RET_DK, RET_DV), F32)
        for n in range(NCHUNK):
            st_ref[n] = state.astype(BF16)
            if n + 1 < NCHUNK:
                rows, _, _, _, kr_b = rotated(n)
                state = state * cdec + _dot_tn(kr_b, (v_ref[rows, :] * zeta).astype(BF16))

        grad_state = jnp.zeros((RET_DK, RET_DV), F32)
        for n in reversed(range(NCHUNK)):
            rows, cosf, sins, qr, kr_b = rotated(n)
            qr_b = qr.astype(BF16)
            v_b = v_ref[rows, :].astype(BF16)
            gs_b = grad_state.astype(BF16)
            o = raw_ref[rows, :]
            r = lax.rsqrt(jnp.mean(o * o, axis=-1, keepdims=True) + EPS)
            hn = o * r
            sg, dsg = _silu_parts(g_ref[rows, :])
            dyn = dy_ref[rows, :]
            dg_ref[rows, :] = (dyn * hn * dsg).astype(BF16)
            dhn = dyn * sg
            do_b = (r * (dhn - hn * jnp.mean(dhn * hn, axis=-1, keepdims=True))).astype(BF16)
            a_b = (_dot_nt(qr_b, kr_b) * dmask).astype(BF16)
            da_b = (_dot_nt(do_b, v_b) * dmask).astype(BF16)
            dqr = _dot(da_b, kr_b) + xi * _dot_nt(do_b, st_ref[n])
            dkr = _dot_tn(da_b, qr_b) + zeta * _dot_nt(v_b, gs_b)
            dv = _dot_tn(a_b, do_b) + zeta * _dot(kr_b, gs_b)
            grad_state = grad_state * cdec + _dot_tn((qr * xi).astype(BF16), do_b)
            dq_ref[rows, :] = _rot_t(dqr, cosf, sins).astype(BF16)
            dk_ref[rows, :] = (_rot_t(dkr, cosf, sins) * kscale).astype(BF16)
            dv_ref[rows, :] = dv.astype(BF16)

    narrow = pl.BlockSpec((T, RET_DK), lambda h: (0, h))
    wide = pl.BlockSpec((T, RET_DV), lambda h: (0, h))
    return pl.pallas_call(
        body, name="ret_bwd", grid=(RET_HEADS,),
        in_specs=_ret_specs() + _RET_TABLE_SPECS + [wide, wide],
        out_specs=[narrow, narrow, wide, wide],
        out_shape=[jax.ShapeDtypeStruct((T, RET_HEADS * RET_DK), BF16),
                   jax.ShapeDtypeStruct((T, RET_HEADS * RET_DK), BF16),
                   jax.ShapeDtypeStruct((T, RET_HEADS * RET_DV), BF16),
                   jax.ShapeDtypeStruct((T, RET_HEADS * RET_DV), BF16)],
        scratch_shapes=[pltpu.VMEM((NCHUNK, RET_DK, RET_DV), BF16)],
        compiler_params=_params(dimension_semantics=("arbitrary",)),
    )(z, z, z, z, *tables, raw, dy)


def _local_step(x, meta_full, norm_g, w_pad, b_f, w_out_b, final_g, target):
    tables = _tables()
    h_pad = jnp.concatenate([jnp.zeros((PAD, D_MODEL), F32), meta_full, x], axis=0)
    b_pad = jnp.pad(b_f, ((0, 0), (0, 128 - FOX_HEADS)))
    fg2 = final_g.reshape(1, D_MODEL)

    u = _rms_fwd(h_pad, norm_g)
    z = _mm(u, w_pad, nt=False, tm=T, tn=384, out_dtype=F32, name="mm_z")
    c = _forget_fwd(z, b_pad)
    c_heads = c[:, :FOX_HEADS].T
    c_col = c_heads.reshape(FOX_HEADS, T, 1)
    c_row = c_heads.reshape(FOX_HEADS, 1, T)
    raw, y_r = _ret_fwd(z, tables)
    a_f, y_f = _fox_fwd(z, c_col, c_row)
    y = jnp.concatenate([y_r, y_f], axis=1)
    out_pad = _mm(y, w_out_b, nt=False, tm=T, tn=256, out_dtype=F32, name="mm_out", add=h_pad)
    dout, dout_b, loss_blk, d_final_g = _final_loss(out_pad, target, fg2)

    dy = _mm(dout_b, w_out_b, nt=True, tm=T, tn=512, out_dtype=F32, name="mm_dy")
    d_w_out = _mm(y.T, dout_b, nt=False, tm=D_MIX, tn=256, out_dtype=F32, name="mm_dwout")
    drq, drk, drv, drg = _ret_bwd(z, tables, raw, dy)
    dfq, dfk, dfv, dfg, dc_row = _fox_bwd(z, c_col, c_row, a_f, dy)
    dc = jnp.pad(dc_row.reshape(FOX_HEADS, T).T, ((0, 0), (0, 128 - FOX_HEADS)))
    dff, db_f = _forget_bwd(z, b_pad, dc)
    dz = jnp.concatenate([drq, drk, drv, drg, dfq, dfk, dfv, dfg, dff], axis=1)
    d_w_pad = _mm(u.T, dz, nt=False, tm=D_MODEL, tn=384, out_dtype=F32, name="mm_dwin")
    du = _mm(dz, w_pad, nt=True, tm=272, tn=D_MODEL, out_dtype=F32, name="mm_du")
    dh, d_norm_g = _rms_bwd(du, h_pad, dout, norm_g)

    return dict(loss=loss_blk[0, 0], grad_x=dh[CHUNK:], d_meta=dh[PAD:CHUNK], d_norm_g=d_norm_g,
                d_w_in=d_w_pad[:, :D_IN], d_b_f=db_f[:, :FOX_HEADS], d_w_out=d_w_out,
                d_final_g=d_final_g)


def _coords():
    return lax.axis_index("x"), lax.axis_index("y"), lax.axis_index("c")


def _flip(v, bit):
    return 1 - v if bit else v


def _peer(x, y, c, r):
    return _flip(x, (r >> 2) & 1), _flip(y, (r >> 1) & 1), _flip(c, r & 1)


def _exchange(arrays, gather, name):
    n_arr = len(arrays)

    def body(*refs):
        ins, outs = refs[:n_arr], refs[n_arr:2 * n_arr]
        send_sems, recv_sems, local_sems = refs[2 * n_arr:]
        x, y, c = _coords()
        me = 4 * x + 2 * y + c

        def src(k, to_idx):
            return ins[k] if gather else ins[k].at[to_idx]

        local = [pltpu.make_async_copy(src(k, me), outs[k].at[me], local_sems.at[k])
                 for k in range(n_arr)]
        for cp in local:
            cp.start()
        sends = []
        for r in range(1, N_DEV):
            px, py, pc = _peer(x, y, c, r)
            for k in range(n_arr):
                sends.append(pltpu.make_async_remote_copy(
                    src_ref=src(k, 4 * px + 2 * py + pc), dst_ref=outs[k].at[me],
                    send_sem=send_sems.at[k, r - 1], recv_sem=recv_sems.at[k, r - 1],
                    device_id=(px, py, pc), device_id_type=MESH))
        for cp in sends:
            cp.start()
        for r in range(1, N_DEV):
            px, py, pc = _peer(x, y, c, r)
            frm = 4 * px + 2 * py + pc
            for k in range(n_arr):
                pltpu.make_async_remote_copy(
                    src_ref=src(k, frm), dst_ref=outs[k].at[frm],
                    send_sem=send_sems.at[k, r - 1], recv_sem=recv_sems.at[k, r - 1],
                    device_id=(px, py, pc), device_id_type=MESH).wait_recv()
        for cp in sends:
            cp.wait_send()
        for cp in local:
            cp.wait()

    def out_sds(a):
        return jax.ShapeDtypeStruct(((N_DEV,) + a.shape) if gather else a.shape, a.dtype)

    any_spec = pl.BlockSpec(memory_space=pl.ANY)
    return pl.pallas_call(
        body, name=name,
        in_specs=[any_spec] * n_arr, out_specs=[any_spec] * n_arr,
        out_shape=[out_sds(a) for a in arrays],
        scratch_shapes=[pltpu.SemaphoreType.DMA((n_arr, N_DEV - 1)),
                        pltpu.SemaphoreType.DMA((n_arr, N_DEV - 1)),
                        pltpu.SemaphoreType.DMA((n_arr,))],
    )(*arrays)


def _adamw(w, g, m, v):
    m = ADAM_B1 * m + (1.0 - ADAM_B1) * g
    v = ADAM_B2 * v + (1.0 - ADAM_B2) * (g * g)
    m_hat = m / (1.0 - ADAM_B1 ** ADAM_STEP)
    v_hat = v / (1.0 - ADAM_B2 ** ADAM_STEP)
    delta = -ADAM_LR * (m_hat / (jnp.sqrt(v_hat) + ADAM_EPS) + ADAM_WD * w)
    return delta, m, v


def _sum_adamw(parts, w, m, v, rows, name):
    _, r_tot, cols = parts.shape
    assert r_tot % rows == 0

    def body(p_ref, w_ref, m_ref, v_ref, g_ref, d_ref, nm_ref, nv_ref):
        g = p_ref[0].astype(F32)
        for d in range(1, N_DEV):
            g = g + p_ref[d].astype(F32)
        delta, nm, nv = _adamw(w_ref[...], g, m_ref[...], v_ref[...])
        g_ref[...] = g
        d_ref[...] = delta
        nm_ref[...] = nm
        nv_ref[...] = nv

    blk = pl.BlockSpec((rows, cols), lambda i: (i, 0))
    return pl.pallas_call(
        body, name=name, grid=(r_tot // rows,),
        in_specs=[pl.BlockSpec((N_DEV, rows, cols), lambda i: (0, i, 0)), blk, blk, blk],
        out_specs=[blk] * 4,
        out_shape=[jax.ShapeDtypeStruct((r_tot, cols), F32)] * 4,
        compiler_params=_params(dimension_semantics=("arbitrary",)),
    )(parts, w, m, v)


def kernel(x, meta_tokens, norm_g, w_in, b_f, w_out, final_g, loss_target, m_meta_tokens, m_norm_g, m_w_in, m_b_f, m_w_out, m_final_g, v_meta_tokens, v_norm_g, v_w_in, v_b_f, v_w_out, v_final_g):
    me = 4 * lax.axis_index("x") + 2 * lax.axis_index("y") + lax.axis_index("c")

    w_in_all, w_out_all, meta_all = _exchange(
        [w_in[0].astype(BF16), w_out[0].astype(BF16), meta_tokens], gather=True, name="gather_weights")
    w_full = jnp.transpose(w_in_all, (1, 0, 2)).reshape(D_MODEL, D_IN)
    w_pad = jnp.pad(w_full, ((0, 0), (0, D_IN_PAD - D_IN)))
    w_out_b = w_out_all.reshape(D_MIX, D_MODEL)
    meta_full = jnp.transpose(meta_all, (1, 0, 2)).reshape(N_META, D_MODEL)

    loc = _local_step(x[0], meta_full, norm_g, w_pad, b_f[0:1], w_out_b, final_g, loss_target[0])

    d_w_in_blocks = jnp.transpose(loc["d_w_in"].reshape(D_MODEL, N_DEV, W_BLK), (1, 0, 2)).astype(BF16)
    d_w_out_blocks = loc["d_w_out"].reshape(N_DEV, WO_BLK, D_MODEL).astype(BF16)
    small = jnp.concatenate([
        loc["d_meta"], loc["d_norm_g"], loc["d_final_g"],
        jnp.pad(loc["d_b_f"], ((0, 0), (0, D_MODEL - FOX_HEADS))),
        jnp.zeros((SMALL_ROWS - N_META - 3, D_MODEL), F32)], axis=0)
    got_w_in, got_w_out = _exchange([d_w_in_blocks, d_w_out_blocks], gather=False, name="scatter_grads")
    (got_small,) = _exchange([small], gather=True, name="gather_small")

    g_w_in, d_w_in, nm_w_in, nv_w_in = _sum_adamw(got_w_in, w_in[0], m_w_in[0], v_w_in[0], 128, "adamw_w_in")
    g_w_out, d_w_out, nm_w_out, nv_w_out = _sum_adamw(got_w_out, w_out[0], m_w_out[0], v_w_out[0], 128, "adamw_w_out")

    row = lambda a: a.reshape(1, -1)
    wide = lambda a: jnp.pad(row(a), ((0, 0), (0, D_MODEL - a.size)))

    def small_pack(meta, ng, fg, bf):
        return jnp.concatenate([meta, row(ng), row(fg), wide(bf),
                                jnp.zeros((SMALL_ROWS - N_META - 3, D_MODEL), F32)], axis=0)

    meta_at = lambda a: lax.dynamic_update_slice(jnp.zeros((N_META, D_MODEL), F32), a, (0, me * META_BLK))
    g_s, d_s, nm_s, nv_s = _sum_adamw(
        got_small,
        small_pack(meta_at(meta_tokens), norm_g, final_g, b_f),
        small_pack(meta_at(m_meta_tokens), m_norm_g, m_final_g, m_b_f),
        small_pack(meta_at(v_meta_tokens), v_norm_g, v_final_g, v_b_f),
        SMALL_ROWS, "adamw_small")

    def unpack(s):
        meta = lax.dynamic_slice(s[:N_META], (0, me * META_BLK), (N_META, META_BLK))
        return meta, s[N_META:N_META + 1], s[N_META + 1], s[N_META + 2:N_META + 3, :FOX_HEADS]

    g_meta, g_ng, g_fg, g_bf = unpack(g_s)
    d_meta, d_ng, d_fg, d_bf = unpack(d_s)
    nm_meta, nm_ng, nm_fg, nm_bf = unpack(nm_s)
    nv_meta, nv_ng, nv_fg, nv_bf = unpack(nv_s)

    loss = lax.psum(loc["loss"], MESH_AXES)
    return (loss, loc["grad_x"][None],
            g_meta, g_ng, g_w_in[None], g_bf, g_w_out[None], g_fg,
            d_meta, d_ng, d_w_in[None], d_bf, d_w_out[None], d_fg,
            nm_meta, nm_ng, nm_w_in[None], nm_bf, nm_w_out[None], nm_fg,
            nv_meta, nv_ng, nv_w_in[None], nv_bf, nv_w_out[None], nv_fg)
```

```python
import numpy as np
import jax
import jax.numpy as jnp
from jax import lax
from jax.experimental import pallas as pl
from jax.experimental.pallas import tpu as pltpu

F32 = jnp.float32
BF16 = jnp.bfloat16

N_DEV = 8
D_MODEL = 1024
SEQ = 2048
N_META = 16
CHUNK = 128
PAD = CHUNK - N_META
T = SEQ + CHUNK
NCHUNK = T // CHUNK
D_MIX = 2048
RET_HEADS = 4
RET_DK = 128
RET_DV = 256
RET_W = 2 * RET_DK + 2 * RET_DV
FOX_HEADS = 16
FOX_D = 64
FOX_PAIRS = FOX_HEADS // 2
FOX_W = 4 * 128
FOX_BASE = RET_HEADS * RET_W
FF_BASE = FOX_BASE + FOX_PAIRS * FOX_W
D_IN = 7184
D_IN_PAD = 7296
W_BLK = D_IN // N_DEV
WO_BLK = D_MIX // N_DEV
META_BLK = D_MODEL // N_DEV
EPS = 1e-6
NEG_INF = -1e30
ROPE_BASE = 10000.0

ADAM_LR = 0.001
ADAM_B1 = 0.9
ADAM_B2 = 0.999
ADAM_EPS = 1e-08
ADAM_WD = 0.01
ADAM_STEP = 10

SMALL_ROWS = 24
VMEM_LIMIT = 56 * 1024 * 1024
MESH_AXES = ("x", "y", "c")
MESH = pl.DeviceIdType.MESH
ANY = pl.BlockSpec(memory_space=pl.ANY)

_NT = (((1,), (1,)), ((), ()))
_TN = (((0,), (0,)), ((), ()))


def _dot(a, b):
    return jnp.dot(a, b, preferred_element_type=F32)


def _dot_nt(a, b):
    return lax.dot_general(a, b, _NT, preferred_element_type=F32)


def _dot_tn(a, b):
    return lax.dot_general(a, b, _TN, preferred_element_type=F32)


def _params(**kw):
    return pltpu.CompilerParams(vmem_limit_bytes=VMEM_LIMIT, **kw)


def _silu_parts(g):
    sig = jax.nn.sigmoid(g)
    return g * sig, sig * (1.0 + g * (1.0 - sig))


def _tables():
    pos = np.arange(T, dtype=np.float32) - PAD
    inv = (ROPE_BASE ** (-np.arange(0, RET_DK, 2, dtype=np.float32) / RET_DK)).astype(np.float32)
    ang = pos[:, None] * inv[None, :]
    cos, sin = np.cos(ang), np.sin(ang)
    cosf = np.concatenate([cos, cos], axis=1).astype(np.float32)
    sins = np.concatenate([-sin, sin], axis=1).astype(np.float32)
    h = np.arange(RET_HEADS, dtype=np.float32)
    log_gamma = np.log1p(-np.exp2(-5.0 - h)).astype(np.float32)
    idx = np.arange(CHUNK, dtype=np.float32)
    diff = idx[:, None] - idx[None, :]
    dmask = np.where(diff[None] >= 0,
                     np.exp(log_gamma[:, None, None] * np.maximum(diff, 0.0)[None]), 0.0)
    zeta = np.exp(log_gamma[:, None] * (CHUNK - 1.0 - idx)[None, :])
    xi = np.exp(log_gamma[:, None] * (idx + 1.0)[None, :])
    cdec = np.exp(log_gamma * CHUNK)
    return (jnp.asarray(cosf), jnp.asarray(sins), jnp.asarray(dmask, F32),
            jnp.asarray(zeta[:, :, None], F32), jnp.asarray(xi[:, :, None], F32),
            jnp.asarray(cdec[:, None, None], F32))


def _regroup(w):
    r = w.shape[0]
    ret = jnp.concatenate([w[:, 0:512].reshape(r, 4, 128), w[:, 512:1024].reshape(r, 4, 128),
                           w[:, 1024:2048].reshape(r, 4, 256), w[:, 2048:3072].reshape(r, 4, 256)],
                          axis=2).reshape(r, FOX_BASE)
    fox = jnp.concatenate([w[:, 3072 + 1024 * i:4096 + 1024 * i].reshape(r, 8, 128) for i in range(4)],
                          axis=2).reshape(r, FOX_PAIRS * FOX_W)
    ff = jnp.pad(w[:, FF_BASE:D_IN], ((0, 0), (0, D_IN_PAD - D_IN)))
    return jnp.concatenate([ret, fox, ff], axis=1)


def _ungroup(w):
    r = w.shape[0]
    ret = w[:, :FOX_BASE].reshape(r, 4, RET_W)
    fox = w[:, FOX_BASE:FF_BASE].reshape(r, 8, FOX_W)
    return jnp.concatenate(
        [ret[:, :, 0:128].reshape(r, 512), ret[:, :, 128:256].reshape(r, 512),
         ret[:, :, 256:512].reshape(r, 1024), ret[:, :, 512:768].reshape(r, 1024)]
        + [fox[:, :, 128 * i:128 * i + 128].reshape(r, 1024) for i in range(4)]
        + [w[:, FF_BASE:FF_BASE + FOX_HEADS]], axis=1)


def _mm(a, b, *, tm, tn, out_dtype, name, ta=False, nt=False, add=None):
    k, m = (a.shape if ta else a.shape[::-1])
    n = b.shape[0] if nt else b.shape[1]
    assert m % tm == 0 and n % tn == 0 and not (ta and nt)

    def body(a_ref, b_ref, *rest):
        o_ref = rest[-1]
        if ta:
            prod = _dot_tn(a_ref[...], b_ref[...])
        elif nt:
            prod = _dot_nt(a_ref[...], b_ref[...])
        else:
            prod = _dot(a_ref[...], b_ref[...])
        if add is not None:
            prod = prod + rest[0][...]
        o_ref[...] = prod.astype(out_dtype)

    in_specs = [pl.BlockSpec((k, tm), lambda j, i: (0, i)) if ta
                else pl.BlockSpec((tm, k), lambda j, i: (i, 0)),
                pl.BlockSpec((tn, k), lambda j, i: (j, 0)) if nt
                else pl.BlockSpec((k, tn), lambda j, i: (0, j))]
    args = [a, b]
    if add is not None:
        in_specs.append(pl.BlockSpec((tm, tn), lambda j, i: (i, j)))
        args.append(add)
    return pl.pallas_call(
        body, name=name, grid=(n // tn, m // tm),
        in_specs=in_specs,
        out_specs=pl.BlockSpec((tm, tn), lambda j, i: (i, j)),
        out_shape=jax.ShapeDtypeStruct((m, n), out_dtype),
        compiler_params=_params(dimension_semantics=("arbitrary", "arbitrary")),
    )(*args)


def _rms_fwd(h_pad, g):
    def body(h_ref, g_ref, u_ref):
        h = h_ref[...]
        r = lax.rsqrt(jnp.mean(h * h, axis=-1, keepdims=True) + EPS)
        u_ref[...] = (h * r * g_ref[...]).astype(BF16)

    return pl.pallas_call(
        body, name="rms_fwd", grid=(NCHUNK,),
        in_specs=[pl.BlockSpec((CHUNK, D_MODEL), lambda i: (i, 0)),
                  pl.BlockSpec((1, D_MODEL), lambda i: (0, 0))],
        out_specs=pl.BlockSpec((CHUNK, D_MODEL), lambda i: (i, 0)),
        out_shape=jax.ShapeDtypeStruct((T, D_MODEL), BF16),
        compiler_params=_params(dimension_semantics=("arbitrary",)),
    )(h_pad, g)


def _final_loss(out_pad, target, g):
    def body(o_ref, t_ref, g_ref, d_ref, db_ref, loss_ref, dg_ref):
        i = pl.program_id(0)

        @pl.when(i == 0)
        def _():
            d_ref[...] = jnp.zeros_like(d_ref)
            db_ref[...] = jnp.zeros_like(db_ref)
            loss_ref[...] = jnp.zeros_like(loss_ref)
            dg_ref[...] = jnp.zeros_like(dg_ref)

        @pl.when(i > 0)
        def _():
            o = o_ref[...]
            g = g_ref[...]
            r = lax.rsqrt(jnp.mean(o * o, axis=-1, keepdims=True) + EPS)
            xn = o * r
            e = xn * g - t_ref[...]
            loss_ref[...] += jnp.full(loss_ref.shape, 0.5 / D_MODEL * jnp.sum(e * e), F32)
            do = e * (1.0 / D_MODEL)
            dg_ref[...] += jnp.sum(do * xn, axis=0, keepdims=True)
            dn = do * g
            d = r * (dn - xn * jnp.mean(dn * xn, axis=-1, keepdims=True))
            d_ref[...] = d
            db_ref[...] = d.astype(BF16)

    return pl.pallas_call(
        body, name="final_loss", grid=(NCHUNK,),
        in_specs=[pl.BlockSpec((CHUNK, D_MODEL), lambda i: (i, 0)),
                  pl.BlockSpec((CHUNK, D_MODEL), lambda i: (jnp.maximum(i - 1, 0), 0)),
                  pl.BlockSpec((1, D_MODEL), lambda i: (0, 0))],
        out_specs=[pl.BlockSpec((CHUNK, D_MODEL), lambda i: (i, 0)),
                   pl.BlockSpec((CHUNK, D_MODEL), lambda i: (i, 0)),
                   pl.BlockSpec((8, 128), lambda i: (0, 0)),
                   pl.BlockSpec((1, D_MODEL), lambda i: (0, 0))],
        out_shape=[jax.ShapeDtypeStruct((T, D_MODEL), F32),
                   jax.ShapeDtypeStruct((T, D_MODEL), BF16),
                   jax.ShapeDtypeStruct((8, 128), F32),
                   jax.ShapeDtypeStruct((1, D_MODEL), F32)],
        compiler_params=_params(dimension_semantics=("arbitrary",)),
    )(out_pad, target, g)


def _rms_bwd(du, h_pad, dout, g):
    def body(du_ref, h_ref, d_ref, g_ref, dh_ref, dg_ref):
        i = pl.program_id(0)

        @pl.when(i == 0)
        def _():
            dg_ref[...] = jnp.zeros_like(dg_ref)

        h = h_ref[...]
        du_ = du_ref[...]
        r = lax.rsqrt(jnp.mean(h * h, axis=-1, keepdims=True) + EPS)
        xn = h * r
        dg_ref[...] += jnp.sum(du_ * xn, axis=0, keepdims=True)
        dn = du_ * g_ref[...]
        dh_ref[...] = d_ref[...] + r * (dn - xn * jnp.mean(dn * xn, axis=-1, keepdims=True))

    return pl.pallas_call(
        body, name="rms_bwd", grid=(NCHUNK,),
        in_specs=[pl.BlockSpec((CHUNK, D_MODEL), lambda i: (i, 0)),
                  pl.BlockSpec((CHUNK, D_MODEL), lambda i: (i, 0)),
                  pl.BlockSpec((CHUNK, D_MODEL), lambda i: (i, 0)),
                  pl.BlockSpec((1, D_MODEL), lambda i: (0, 0))],
        out_specs=[pl.BlockSpec((CHUNK, D_MODEL), lambda i: (i, 0)),
                   pl.BlockSpec((1, D_MODEL), lambda i: (0, 0))],
        out_shape=[jax.ShapeDtypeStruct((T, D_MODEL), F32),
                   jax.ShapeDtypeStruct((1, D_MODEL), F32)],
        compiler_params=_params(dimension_semantics=("arbitrary",)),
    )(du, h_pad, dout, g)


def _coords():
    return lax.axis_index("x"), lax.axis_index("y"), lax.axis_index("c")


def _flip(v, bit):
    return 1 - v if bit else v


def _peer(x, y, c, r):
    return _flip(x, (r >> 2) & 1), _flip(y, (r >> 1) & 1), _flip(c, r & 1)


def _direct_exchange(ins, outs, send_sems, recv_sems, local_sems, gather):
    x, y, c = _coords()
    me = 4 * x + 2 * y + c

    def src(k, to_idx):
        return ins[k] if gather else ins[k].at[to_idx]

    local = [pltpu.make_async_copy(src(k, me), outs[k].at[me], local_sems.at[k])
             for k in range(len(ins))]
    sends, recvs = [], []
    for r in range(1, N_DEV):
        px, py, pc = _peer(x, y, c, r)
        peer = 4 * px + 2 * py + pc
        for k in range(len(ins)):
            sems = dict(send_sem=send_sems.at[k, r - 1], recv_sem=recv_sems.at[k, r - 1],
                        device_id=(px, py, pc), device_id_type=MESH)
            sends.append(pltpu.make_async_remote_copy(src_ref=src(k, peer), dst_ref=outs[k].at[me], **sems))
            recvs.append(pltpu.make_async_remote_copy(src_ref=src(k, peer), dst_ref=outs[k].at[peer], **sems))

    def start():
        for cp in local + sends:
            cp.start()

    def wait():
        for cp in recvs:
            cp.wait_recv()
        for cp in sends:
            cp.wait_send()
        for cp in local:
            cp.wait()

    return start, wait


def _exchange_sems(n_arr):
    return [pltpu.SemaphoreType.DMA((n_arr, N_DEV - 1)), pltpu.SemaphoreType.DMA((n_arr, N_DEV - 1)),
            pltpu.SemaphoreType.DMA((n_arr,))]


def _exchange_shape(a, gather):
    return jax.ShapeDtypeStruct(((N_DEV,) + a.shape) if gather else a.shape, a.dtype)


def _exchange(arrays, gather, name):
    n_arr = len(arrays)

    def body(*refs):
        start, wait = _direct_exchange(refs[:n_arr], refs[n_arr:2 * n_arr], *refs[2 * n_arr:], gather)
        start()
        wait()

    return pl.pallas_call(
        body, name=name,
        in_specs=[ANY] * n_arr, out_specs=[ANY] * n_arr,
        out_shape=[_exchange_shape(a, gather) for a in arrays],
        scratch_shapes=_exchange_sems(n_arr),
    )(*arrays)


def _gather_two_level(arrays, name):
    n_arr = len(arrays)

    def body(*refs):
        ins, outs = refs[:n_arr], refs[n_arr:2 * n_arr]
        send_sems, recv_sems, local_sems = refs[2 * n_arr:]
        x, y, c = _coords()
        sibling = (x, y, 1 - c)
        chips = [(1 - x, y), (x, 1 - y), (1 - x, 1 - y)]

        def slot(k, px, py, pc):
            return outs[k].at[4 * px + 2 * py + pc]

        def copy(k, j, block, to, own):
            return pltpu.make_async_remote_copy(
                src_ref=ins[k] if own else slot(k, *block), dst_ref=slot(k, *block),
                send_sem=send_sems.at[k, j], recv_sem=recv_sems.at[k, j],
                device_id=to, device_id_type=MESH)

        local = [pltpu.make_async_copy(ins[k], slot(k, x, y, c), local_sems.at[k]) for k in range(n_arr)]
        first, passed = [], []
        for k in range(n_arr):
            first.append(copy(k, 0, (x, y, c), sibling, True))
            first += [copy(k, 1 + j, (x, y, c), (*chip, c), True) for j, chip in enumerate(chips)]
        for cp in local + first:
            cp.start()
        for j, chip in enumerate(chips):
            for k in range(n_arr):
                copy(k, 1 + j, (*chip, c), (x, y, c), False).wait_recv()
                fwd = copy(k, 4 + j, (*chip, c), sibling, False)
                fwd.start()
                passed.append(fwd)
        for k in range(n_arr):
            copy(k, 0, sibling, (x, y, c), False).wait_recv()
            for j, chip in enumerate(chips):
                copy(k, 4 + j, (*chip, 1 - c), (x, y, c), False).wait_recv()
        for cp in first + passed:
            cp.wait_send()
        for cp in local:
            cp.wait()

    return pl.pallas_call(
        body, name=name,
        in_specs=[ANY] * n_arr, out_specs=[ANY] * n_arr,
        out_shape=[_exchange_shape(a, True) for a in arrays],
        scratch_shapes=_exchange_sems(n_arr),
    )(*arrays)


def _tri(lower):
    r = lax.broadcasted_iota(jnp.int32, (CHUNK, CHUNK), 0)
    c = lax.broadcasted_iota(jnp.int32, (CHUNK, CHUNK), 1)
    return jnp.where((r >= c) if lower else (r <= c), 1.0, 0.0).astype(F32)


def _row_valid(n):
    r = lax.broadcasted_iota(jnp.int32, (CHUNK, 128), 0) + n * CHUNK
    return r >= PAD


_FF_SPEC = pl.BlockSpec((T, 128), lambda i: (0, FF_BASE // 128))


def _forget_fwd(z, b_pad):
    def body(z_ref, b_ref, o_ref):
        tri = _tri(True)
        carry = jnp.zeros((1, 128), F32)
        for n in range(NCHUNK):
            rows = pl.ds(n * CHUNK, CHUNK)
            a = z_ref[rows, :] + b_ref[...]
            lf = -(jnp.maximum(-a, 0.0) + jnp.log(1.0 + jnp.exp(-jnp.abs(a))))
            lf = jnp.where(_row_valid(n), lf, 0.0)
            c = jnp.dot(tri, lf, precision=lax.Precision.HIGHEST,
                        preferred_element_type=F32) + carry
            carry = c[CHUNK - 1:CHUNK, :]
            o_ref[:, rows] = jnp.where(_row_valid(n), -c, NEG_INF).T

    return pl.pallas_call(
        body, name="forget_fwd", grid=(1,),
        in_specs=[_FF_SPEC, pl.BlockSpec((1, 128), lambda i: (0, 0))],
        out_specs=pl.BlockSpec((128, T), lambda i: (0, 0)),
        out_shape=jax.ShapeDtypeStruct((128, T), F32),
        compiler_params=_params(dimension_semantics=("arbitrary",)),
    )(z, b_pad)


def _forget_bwd(z, b_pad, dc, dz):
    def body(z_ref, b_ref, dc_ref, dz_in, dff_ref, db_ref):
        tri = _tri(False)
        carry = jnp.zeros((1, 128), F32)
        db = jnp.zeros((1, 128), F32)
        for n in reversed(range(NCHUNK)):
            rows = pl.ds(n * CHUNK, CHUNK)
            dc_blk = jnp.concatenate([dc_ref[:, rows], jnp.zeros((128 - FOX_HEADS, CHUNK), F32)], axis=0).T
            dlf = jnp.dot(tri, dc_blk, precision=lax.Precision.HIGHEST,
                          preferred_element_type=F32) + carry
            carry = dlf[0:1, :]
            a = z_ref[rows, :] + b_ref[...]
            dff = jnp.where(_row_valid(n), dlf * jax.nn.sigmoid(-a), 0.0)
            dff_ref[rows, :] = dff.astype(BF16)
            db = db + jnp.sum(dff, axis=0, keepdims=True)
        db_ref[...] = db

    return pl.pallas_call(
        body, name="forget_bwd", grid=(1,),
        in_specs=[_FF_SPEC, pl.BlockSpec((1, 128), lambda i: (0, 0)),
                  pl.BlockSpec((FOX_HEADS, T), lambda i: (0, 0)), ANY],
        out_specs=[_FF_SPEC, pl.BlockSpec((1, 128), lambda i: (0, 0))],
        out_shape=[jax.ShapeDtypeStruct((T, D_IN_PAD), BF16),
                   jax.ShapeDtypeStruct((1, 128), F32)],
        input_output_aliases={3: 0},
        compiler_params=_params(dimension_semantics=("arbitrary",)),
    )(z, b_pad, dc, dz)


def _causal_bias():
    r = lax.broadcasted_iota(jnp.int32, (CHUNK, CHUNK), 0)
    c = lax.broadcasted_iota(jnp.int32, (CHUNK, CHUNK), 1)
    return jnp.where(c <= r, 0.0, NEG_INF).astype(F32)


def _fox_exp(q_blk, k_all, bias, causal, i):
    lo, hi = i * CHUNK, (i + 1) * CHUNK
    s_off = _dot_nt(q_blk, k_all[:lo]) + bias[:, :lo]
    s_dia = _dot_nt(q_blk, k_all[lo:hi]) + (bias[:, lo:hi] + causal)
    m = jnp.maximum(jnp.max(s_off, axis=-1, keepdims=True), jnp.max(s_dia, axis=-1, keepdims=True))
    e_off = jnp.exp(s_off - m)
    e_dia = jnp.exp(s_dia - m)
    inv = 1.0 / (jnp.sum(e_off, axis=-1, keepdims=True) + jnp.sum(e_dia, axis=-1, keepdims=True))
    return e_off, e_dia, inv


_FOX_Z_SPEC = pl.BlockSpec((T, FOX_W), lambda p: (0, FOX_BASE // FOX_W + p))
_FOX_BIAS_SPEC = pl.BlockSpec((2, 1, T), lambda p: (p, 0, 0))
_FOX_SCALE = FOX_D ** -0.5


def _fox_fwd(z, bias, y, w_out_blk):
    last = FOX_PAIRS - 1

    def body(z_ref, b_ref, y_in, w_ref, a_ref, y_ref, wall_ref, send_sems, recv_sems, local_sems):
        start, wait = _direct_exchange([w_ref], [wall_ref], send_sems, recv_sems, local_sems, True)
        pl.when(pl.program_id(0) == 0)(start)

        causal = _causal_bias()
        a_ref[pl.ds(0, CHUNK), :] = jnp.zeros((CHUNK, 128), F32)
        y_ref[pl.ds(0, CHUNK), :] = jnp.zeros((CHUNK, 128), BF16)
        for j in range(2):
            lanes = pl.ds(j * FOX_D, FOX_D)
            k_all = z_ref[:, pl.ds(128 + j * FOX_D, FOX_D)].astype(BF16)
            v_all = z_ref[:, pl.ds(256 + j * FOX_D, FOX_D)].astype(BF16)
            bias = b_ref[j]
            for i in range(1, NCHUNK):
                rows = pl.ds(i * CHUNK, CHUNK)
                lo, hi = i * CHUNK, (i + 1) * CHUNK
                q_blk = (z_ref[rows, lanes] * _FOX_SCALE).astype(BF16)
                e_off, e_dia, inv = _fox_exp(q_blk, k_all, bias, causal, i)
                o = (_dot(e_off.astype(BF16), v_all[:lo]) + _dot(e_dia.astype(BF16), v_all[lo:hi])) * inv
                a_ref[rows, lanes] = o
                gate = _silu_parts(z_ref[rows, pl.ds(384 + j * FOX_D, FOX_D)])[0]
                y_ref[rows, lanes] = (o * gate).astype(BF16)

        pl.when(pl.program_id(0) == last)(wait)

    return pl.pallas_call(
        body, name="fox_fwd", grid=(FOX_PAIRS,),
        in_specs=[_FOX_Z_SPEC, _FOX_BIAS_SPEC, ANY, ANY],
        out_specs=[pl.BlockSpec((T, 128), lambda p: (0, p)),
                   pl.BlockSpec((T, 128), lambda p: (0, 8 + p)), ANY],
        out_shape=[jax.ShapeDtypeStruct((T, FOX_HEADS * FOX_D), F32),
                   jax.ShapeDtypeStruct((T, D_MIX), BF16),
                   _exchange_shape(w_out_blk, True)],
        input_output_aliases={2: 1},
        scratch_shapes=_exchange_sems(1),
        compiler_params=_params(dimension_semantics=("arbitrary",)),
    )(z, bias, y, w_out_blk)


def _fox_bwd(z, bias, a_f, dy, dz, dwo_blocks):
    last = FOX_PAIRS - 1

    def body(z_ref, b_ref, a_ref, dy_ref, dz_in, dwo_ref, dz_ref, dc_ref, got_ref,
             dk_acc, dv_acc, dc_acc, send_sems, recv_sems, local_sems):
        start, wait = _direct_exchange([dwo_ref], [got_ref], send_sems, recv_sems, local_sems, False)
        pl.when(pl.program_id(0) == 0)(start)

        causal = _causal_bias()
        dz_ref[pl.ds(0, CHUNK), pl.ds(0, 128)] = jnp.zeros((CHUNK, 128), BF16)
        dz_ref[pl.ds(0, CHUNK), pl.ds(384, 128)] = jnp.zeros((CHUNK, 128), BF16)
        for j in range(2):
            lanes = pl.ds(j * FOX_D, FOX_D)
            k_all = z_ref[:, pl.ds(128 + j * FOX_D, FOX_D)].astype(BF16)
            v_all = z_ref[:, pl.ds(256 + j * FOX_D, FOX_D)].astype(BF16)
            bias = b_ref[j]
            dk_acc[...] = jnp.zeros_like(dk_acc)
            dv_acc[...] = jnp.zeros_like(dv_acc)
            dc_acc[...] = jnp.zeros_like(dc_acc)
            for i in range(1, NCHUNK):
                rows = pl.ds(i * CHUNK, CHUNK)
                lo, hi = i * CHUNK, (i + 1) * CHUNK
                off, dia = pl.ds(0, lo), pl.ds(lo, CHUNK)
                q_blk = (z_ref[rows, lanes] * _FOX_SCALE).astype(BF16)
                e_off, e_dia, inv = _fox_exp(q_blk, k_all, bias, causal, i)
                p_off, p_dia = e_off * inv, e_dia * inv
                sg, dsg = _silu_parts(z_ref[rows, pl.ds(384 + j * FOX_D, FOX_D)])
                dyj = dy_ref[rows, lanes]
                dz_ref[rows, pl.ds(384 + j * FOX_D, FOX_D)] = (dyj * a_ref[rows, lanes] * dsg).astype(BF16)
                do_b = (dyj * sg).astype(BF16)
                dp_off = _dot_nt(do_b, v_all[:lo])
                dp_dia = _dot_nt(do_b, v_all[lo:hi])
                d = (jnp.sum(p_off * dp_off, axis=-1, keepdims=True)
                     + jnp.sum(p_dia * dp_dia, axis=-1, keepdims=True))
                ds_off = p_off * (dp_off - d)
                ds_dia = p_dia * (dp_dia - d)
                dc_acc[:, off] -= jnp.sum(ds_off, axis=0, keepdims=True)
                dc_acc[:, dia] -= jnp.sum(ds_dia, axis=0, keepdims=True)
                ds_off_b, ds_dia_b = ds_off.astype(BF16), ds_dia.astype(BF16)
                dq = _dot(ds_off_b, k_all[:lo]) + _dot(ds_dia_b, k_all[lo:hi])
                dz_ref[rows, lanes] = (dq * _FOX_SCALE).astype(BF16)
                dk_acc[off, :] += _dot_tn(ds_off_b, q_blk)
                dk_acc[dia, :] += _dot_tn(ds_dia_b, q_blk)
                dv_acc[off, :] += _dot_tn(p_off.astype(BF16), do_b)
                dv_acc[dia, :] += _dot_tn(p_dia.astype(BF16), do_b)
            dz_ref[:, pl.ds(128 + j * FOX_D, FOX_D)] = dk_acc[...].astype(BF16)
            dz_ref[:, pl.ds(256 + j * FOX_D, FOX_D)] = dv_acc[...].astype(BF16)
            dc_ref[j] = dc_acc[...]

        pl.when(pl.program_id(0) == last)(wait)

    col = lambda base: pl.BlockSpec((T, 128), lambda p: (0, base + p))
    return pl.pallas_call(
        body, name="fox_bwd", grid=(FOX_PAIRS,),
        in_specs=[_FOX_Z_SPEC, _FOX_BIAS_SPEC, col(0), col(8), ANY, ANY],
        out_specs=[_FOX_Z_SPEC, _FOX_BIAS_SPEC, ANY],
        out_shape=[jax.ShapeDtypeStruct((T, D_IN_PAD), BF16),
                   jax.ShapeDtypeStruct((FOX_HEADS, 1, T), F32),
                   _exchange_shape(dwo_blocks, False)],
        input_output_aliases={4: 0},
        scratch_shapes=[pltpu.VMEM((T, FOX_D), F32), pltpu.VMEM((T, FOX_D), F32),
                        pltpu.VMEM((1, T), F32)] + _exchange_sems(1),
        compiler_params=_params(dimension_semantics=("arbitrary",)),
    )(z, bias, a_f, dy, dz, dwo_blocks)


def _rot(x, cosf, sins):
    return x * cosf + pltpu.roll(x, RET_DK // 2, 1) * sins


def _rot_t(d, cosf, sins):
    return d * cosf - pltpu.roll(d, RET_DK // 2, 1) * sins


_RET_Z_SPEC = pl.BlockSpec((T, RET_W), lambda h: (0, h))
_RET_TABLE_SPECS = [
    pl.BlockSpec((T, RET_DK), lambda h: (0, 0)),
    pl.BlockSpec((T, RET_DK), lambda h: (0, 0)),
    pl.BlockSpec((1, CHUNK, CHUNK), lambda h: (h, 0, 0)),
    pl.BlockSpec((1, CHUNK, 1), lambda h: (h, 0, 0)),
    pl.BlockSpec((1, CHUNK, 1), lambda h: (h, 0, 0)),
    pl.BlockSpec((1, 1, 1), lambda h: (h, 0, 0)),
]
_RQ, _RK = pl.ds(0, RET_DK), pl.ds(RET_DK, RET_DK)
_RV, _RG = pl.ds(2 * RET_DK, RET_DV), pl.ds(2 * RET_DK + RET_DV, RET_DV)
_RET_KSCALE = RET_DK ** -0.5


def _ret_fwd(z, tables):
    def body(z_ref, cos_ref, sin_ref, dm_ref, zeta_ref, xi_ref, cd_ref, raw_ref, y_ref):
        dmask, zeta, xi, cdec = dm_ref[0], zeta_ref[0], xi_ref[0], cd_ref[0]
        state = jnp.zeros((RET_DK, RET_DV), F32)
        for n in range(NCHUNK):
            rows = pl.ds(n * CHUNK, CHUNK)
            cosf, sins = cos_ref[rows, :], sin_ref[rows, :]
            qr = _rot(z_ref[rows, _RQ], cosf, sins)
            kr_b = (_rot(z_ref[rows, _RK], cosf, sins) * _RET_KSCALE).astype(BF16)
            v = z_ref[rows, _RV]
            a = _dot_nt(qr.astype(BF16), kr_b) * dmask
            out = _dot(a.astype(BF16), v.astype(BF16)) + _dot((qr * xi).astype(BF16), state.astype(BF16))
            state = state * cdec + _dot_tn(kr_b, (v * zeta).astype(BF16))
            raw_ref[rows, :] = out
            r = lax.rsqrt(jnp.mean(out * out, axis=-1, keepdims=True) + EPS)
            y_ref[rows, :] = (out * r * _silu_parts(z_ref[rows, _RG])[0]).astype(BF16)

    wide = pl.BlockSpec((T, RET_DV), lambda h: (0, h))
    return pl.pallas_call(
        body, name="ret_fwd", grid=(RET_HEADS,),
        in_specs=[_RET_Z_SPEC] + _RET_TABLE_SPECS,
        out_specs=[wide, wide],
        out_shape=[jax.ShapeDtypeStruct((T, RET_HEADS * RET_DV), F32),
                   jax.ShapeDtypeStruct((T, D_MIX), BF16)],
        compiler_params=_params(dimension_semantics=("arbitrary",)),
    )(z, *tables)


def _ret_bwd(z, tables, raw, dy):
    def body(z_ref, cos_ref, sin_ref, dm_ref, zeta_ref, xi_ref, cd_ref, raw_ref, dy_ref,
             dz_ref, st_ref):
        dmask, zeta, xi, cdec = dm_ref[0], zeta_ref[0], xi_ref[0], cd_ref[0]

        def rotated(n):
            rows = pl.ds(n * CHUNK, CHUNK)
            cosf, sins = cos_ref[rows, :], sin_ref[rows, :]
            qr = _rot(z_ref[rows, _RQ], cosf, sins)
            kr_b = (_rot(z_ref[rows, _RK], cosf, sins) * _RET_KSCALE).astype(BF16)
            return rows, cosf, sins, qr, kr_b

        state = jnp.zeros((RET_DK, RET_DV), F32)
        for n in range(NCHUNK):
            st_ref[n] = state.astype(BF16)
            if n + 1 < NCHUNK:
                rows, _, _, _, kr_b = rotated(n)
                state = state * cdec + _dot_tn(kr_b, (z_ref[rows, _RV] * zeta).astype(BF16))

        grad_state = jnp.zeros((RET_DK, RET_DV), F32)
        for n in reversed(range(NCHUNK)):
            rows, cosf, sins, qr, kr_b = rotated(n)
            qr_b = qr.astype(BF16)
            v_b = z_ref[rows, _RV].astype(BF16)
            gs_b = grad_state.astype(BF16)
            o = raw_ref[rows, :]
            r = lax.rsqrt(jnp.mean(o * o, axis=-1, keepdims=True) + EPS)
            hn = o * r
            sg, dsg = _silu_parts(z_ref[rows, _RG])
            dyn = dy_ref[rows, :]
            dz_ref[rows, _RG] = (dyn * hn * dsg).astype(BF16)
            dhn = dyn * sg
            do_b = (r * (dhn - hn * jnp.mean(dhn * hn, axis=-1, keepdims=True))).astype(BF16)
            a_b = (_dot_nt(qr_b, kr_b) * dmask).astype(BF16)
            da_b = (_dot_nt(do_b, v_b) * dmask).astype(BF16)
            dqr = _dot(da_b, kr_b) + xi * _dot_nt(do_b, st_ref[n])
            dkr = _dot_tn(da_b, qr_b) + zeta * _dot_nt(v_b, gs_b)
            dv = _dot_tn(a_b, do_b) + zeta * _dot(kr_b, gs_b)
            grad_state = grad_state * cdec + _dot_tn((qr * xi).astype(BF16), do_b)
            dz_ref[rows, _RQ] = _rot_t(dqr, cosf, sins).astype(BF16)
            dz_ref[rows, _RK] = (_rot_t(dkr, cosf, sins) * _RET_KSCALE).astype(BF16)
            dz_ref[rows, _RV] = dv.astype(BF16)

    wide = pl.BlockSpec((T, RET_DV), lambda h: (0, h))
    return pl.pallas_call(
        body, name="ret_bwd", grid=(RET_HEADS,),
        in_specs=[_RET_Z_SPEC] + _RET_TABLE_SPECS + [wide, wide],
        out_specs=_RET_Z_SPEC,
        out_shape=jax.ShapeDtypeStruct((T, D_IN_PAD), BF16),
        scratch_shapes=[pltpu.VMEM((NCHUNK, RET_DK, RET_DV), BF16)],
        compiler_params=_params(dimension_semantics=("arbitrary",)),
    )(z, *tables, raw, dy)


def _adamw(w, g, m, v):
    m = ADAM_B1 * m + (1.0 - ADAM_B1) * g
    v = ADAM_B2 * v + (1.0 - ADAM_B2) * (g * g)
    m_hat = m / (1.0 - ADAM_B1 ** ADAM_STEP)
    v_hat = v / (1.0 - ADAM_B2 ** ADAM_STEP)
    delta = -ADAM_LR * (m_hat / (jnp.sqrt(v_hat) + ADAM_EPS) + ADAM_WD * w)
    return delta, m, v


def _sum_adamw(parts, w, m, v, rows, name):
    _, r_tot, cols = parts.shape
    assert r_tot % rows == 0

    def body(p_ref, w_ref, m_ref, v_ref, g_ref, d_ref, nm_ref, nv_ref):
        g = p_ref[0].astype(F32)
        for d in range(1, N_DEV):
            g = g + p_ref[d].astype(F32)
        delta, nm, nv = _adamw(w_ref[...], g, m_ref[...], v_ref[...])
        g_ref[...] = g
        d_ref[...] = delta
        nm_ref[...] = nm
        nv_ref[...] = nv

    blk = pl.BlockSpec((rows, cols), lambda i: (i, 0))
    return pl.pallas_call(
        body, name=name, grid=(r_tot // rows,),
        in_specs=[pl.BlockSpec((N_DEV, rows, cols), lambda i: (0, i, 0)), blk, blk, blk],
        out_specs=[blk] * 4,
        out_shape=[jax.ShapeDtypeStruct((r_tot, cols), F32)] * 4,
        compiler_params=_params(dimension_semantics=("arbitrary",)),
    )(parts, w, m, v)


def _forward_backward(x, meta_full, norm_g, w_pad, b_f, w_out_blk, final_g, target):
    tables = _tables()
    h_pad = jnp.concatenate([jnp.zeros((PAD, D_MODEL), F32), meta_full, x], axis=0)
    b_pad = jnp.pad(b_f, ((0, 0), (0, 128 - FOX_HEADS)))

    u = _rms_fwd(h_pad, norm_g)
    z = _mm(u, w_pad, tm=T, tn=384, out_dtype=F32, name="mm_z")
    bias = _forget_fwd(z, b_pad)[:FOX_HEADS].reshape(FOX_HEADS, 1, T)
    raw, y = _ret_fwd(z, tables)
    a_f, y, w_out_all = _fox_fwd(z, bias, y, w_out_blk)
    w_out_b = w_out_all.reshape(D_MIX, D_MODEL)
    out_pad = _mm(y, w_out_b, tm=T, tn=256, out_dtype=F32, name="mm_out", add=h_pad)
    dout, dout_b, loss_blk, d_final_g = _final_loss(out_pad, target, final_g.reshape(1, D_MODEL))

    dy = _mm(dout_b, w_out_b, nt=True, tm=T, tn=512, out_dtype=F32, name="mm_dy")
    d_w_out = _mm(y, dout_b, ta=True, tm=D_MIX, tn=256, out_dtype=BF16, name="mm_dwout")
    dz = _ret_bwd(z, tables, raw, dy)
    dz, dc, got_w_out = _fox_bwd(z, bias, a_f, dy, dz, d_w_out.reshape(N_DEV, WO_BLK, D_MODEL))
    dz, db_f = _forget_bwd(z, b_pad, dc.reshape(FOX_HEADS, T), dz)
    d_w_pad = _mm(u, dz, ta=True, tm=D_MODEL, tn=384, out_dtype=F32, name="mm_dwin")
    du = _mm(dz, w_pad, nt=True, tm=272, tn=D_MODEL, out_dtype=F32, name="mm_du")
    dh, d_norm_g = _rms_bwd(du, h_pad, dout, norm_g)

    return dict(loss=loss_blk[0, 0], grad_x=dh[CHUNK:], d_meta=dh[PAD:CHUNK], d_norm_g=d_norm_g,
                d_w_in=_ungroup(d_w_pad), d_b_f=db_f[:, :FOX_HEADS], got_w_out=got_w_out,
                d_final_g=d_final_g)


def kernel(x, meta_tokens, norm_g, w_in, b_f, w_out, final_g, loss_target, m_meta_tokens, m_norm_g, m_w_in, m_b_f, m_w_out, m_final_g, v_meta_tokens, v_norm_g, v_w_in, v_b_f, v_w_out, v_final_g):
    me = 4 * lax.axis_index("x") + 2 * lax.axis_index("y") + lax.axis_index("c")

    w_in_all, meta_all = _gather_two_level([w_in[0].astype(BF16), meta_tokens], name="gather_w_in")
    w_pad = _regroup(jnp.transpose(w_in_all, (1, 0, 2)).reshape(D_MODEL, D_IN))
    meta_full = jnp.transpose(meta_all, (1, 0, 2)).reshape(N_META, D_MODEL)

    loc = _forward_backward(x[0], meta_full, norm_g, w_pad, b_f, w_out[0].astype(BF16), final_g,
                            loss_target[0])

    d_w_in_blocks = jnp.transpose(loc["d_w_in"].reshape(D_MODEL, N_DEV, W_BLK), (1, 0, 2)).astype(BF16)
    small = jnp.concatenate([
        loc["d_meta"], loc["d_norm_g"], loc["d_final_g"],
        jnp.pad(loc["d_b_f"], ((0, 0), (0, D_MODEL - FOX_HEADS))),
        jnp.zeros((SMALL_ROWS - N_META - 3, D_MODEL), F32)], axis=0)
    (got_w_in,) = _exchange([d_w_in_blocks], gather=False, name="scatter_w_in")
    (got_small,) = _exchange([small], gather=True, name="gather_small")

    g_w_in, d_w_in, nm_w_in, nv_w_in = _sum_adamw(got_w_in, w_in[0], m_w_in[0], v_w_in[0], 128, "adamw_w_in")
    g_w_out, d_w_out, nm_w_out, nv_w_out = _sum_adamw(loc["got_w_out"], w_out[0], m_w_out[0], v_w_out[0], 128, "adamw_w_out")

    row = lambda a: a.reshape(1, -1)
    wide = lambda a: jnp.pad(row(a), ((0, 0), (0, D_MODEL - a.size)))

    def small_pack(meta, ng, fg, bf):
        return jnp.concatenate([meta, row(ng), row(fg), wide(bf),
                                jnp.zeros((SMALL_ROWS - N_META - 3, D_MODEL), F32)], axis=0)

    meta_at = lambda a: lax.dynamic_update_slice(jnp.zeros((N_META, D_MODEL), F32), a, (0, me * META_BLK))
    g_s, d_s, nm_s, nv_s = _sum_adamw(
        got_small,
        small_pack(meta_at(meta_tokens), norm_g, final_g, b_f),
        small_pack(meta_at(m_meta_tokens), m_norm_g, m_final_g, m_b_f),
        small_pack(meta_at(v_meta_tokens), v_norm_g, v_final_g, v_b_f),
        SMALL_ROWS, "adamw_small")

    def unpack(s):
        meta = lax.dynamic_slice(s[:N_META], (0, me * META_BLK), (N_META, META_BLK))
        return meta, s[N_META:N_META + 1], s[N_META + 1], s[N_META + 2:N_META + 3, :FOX_HEADS]

    g_meta, g_ng, g_fg, g_bf = unpack(g_s)
    d_meta, d_ng, d_fg, d_bf = unpack(d_s)
    nm_meta, nm_ng, nm_fg, nm_bf = unpack(nm_s)
    nv_meta, nv_ng, nv_fg, nv_bf = unpack(nv_s)

    loss = lax.psum(loc["loss"], MESH_AXES)
    return (loss, loc["grad_x"][None],
            g_meta, g_ng, g_w_in[None], g_bf, g_w_out[None], g_fg,
            d_meta, d_ng, d_w_in[None], d_bf, d_w_out[None], d_fg,
            nm_meta, nm_ng, nm_w_in[None], nm_bf, nm_w_out[None], nm_fg,
            nv_meta, nv_ng, nv_w_in[None], nv_bf, nv_w_out[None], nv_fg)
```

```python
import numpy as np
import jax
import jax.numpy as jnp
from jax import lax
from jax.experimental import pallas as pl
from jax.experimental.pallas import tpu as pltpu

F32 = jnp.float32
BF16 = jnp.bfloat16

N_DEV = 8
N_CHIP = 4
D_MODEL = 1024
SEQ = 2048
N_META = 16
CHUNK = 128
PAD = CHUNK - N_META
T = SEQ + CHUNK
NCHUNK = T // CHUNK
D_MIX = 2048
RET_HEADS = 4
RET_DK = 128
RET_DV = 256
RET_W = 2 * RET_DK + 2 * RET_DV
FOX_HEADS = 16
FOX_D = 64
FOX_PAIRS = FOX_HEADS // 2
FOX_W = 4 * 128
FOX_BASE = RET_HEADS * RET_W
FF_BASE = FOX_BASE + FOX_PAIRS * FOX_W
D_IN = 7184
D_IN_PAD = 7296
W_BLK = D_IN // N_DEV
WO_BLK = D_MIX // N_DEV
META_BLK = D_MODEL // N_DEV
EPS = 1e-6
NEG_INF = -1e30
ROPE_BASE = 10000.0

ADAM_LR = 0.001
ADAM_B1 = 0.9
ADAM_B2 = 0.999
ADAM_EPS = 1e-08
ADAM_WD = 0.01
ADAM_STEP = 10

SMALL_ROWS = 24
VMEM_LIMIT = 56 * 1024 * 1024
MESH_AXES = ("x", "y", "c")
MESH = pl.DeviceIdType.MESH
ANY = pl.BlockSpec(memory_space=pl.ANY)

_NT = (((1,), (1,)), ((), ()))
_TN = (((0,), (0,)), ((), ()))


def _dot(a, b):
    return jnp.dot(a, b, preferred_element_type=F32)


def _dot_nt(a, b):
    return lax.dot_general(a, b, _NT, preferred_element_type=F32)


def _dot_tn(a, b):
    return lax.dot_general(a, b, _TN, preferred_element_type=F32)


def _params(**kw):
    return pltpu.CompilerParams(vmem_limit_bytes=VMEM_LIMIT, **kw)


def _silu_parts(g):
    sig = jax.nn.sigmoid(g)
    return g * sig, sig * (1.0 + g * (1.0 - sig))


def _tables():
    pos = np.arange(T, dtype=np.float32) - PAD
    inv = (ROPE_BASE ** (-np.arange(0, RET_DK, 2, dtype=np.float32) / RET_DK)).astype(np.float32)
    ang = pos[:, None] * inv[None, :]
    cos, sin = np.cos(ang), np.sin(ang)
    cosf = np.concatenate([cos, cos], axis=1).astype(np.float32)
    sins = np.concatenate([-sin, sin], axis=1).astype(np.float32)
    h = np.arange(RET_HEADS, dtype=np.float32)
    log_gamma = np.log1p(-np.exp2(-5.0 - h)).astype(np.float32)
    idx = np.arange(CHUNK, dtype=np.float32)
    diff = idx[:, None] - idx[None, :]
    dmask = np.where(diff[None] >= 0,
                     np.exp(log_gamma[:, None, None] * np.maximum(diff, 0.0)[None]), 0.0)
    zeta = np.exp(log_gamma[:, None] * (CHUNK - 1.0 - idx)[None, :])
    xi = np.exp(log_gamma[:, None] * (idx + 1.0)[None, :])
    cdec = np.exp(log_gamma * CHUNK)
    return (jnp.asarray(cosf), jnp.asarray(sins), jnp.asarray(dmask, F32),
            jnp.asarray(zeta[:, :, None], F32), jnp.asarray(xi[:, :, None], F32),
            jnp.asarray(cdec[:, None, None], F32))


def _regroup_rows(w):
    c = w.shape[1]
    ret = jnp.concatenate([w[0:512].reshape(4, 128, c), w[512:1024].reshape(4, 128, c),
                           w[1024:2048].reshape(4, 256, c), w[2048:3072].reshape(4, 256, c)],
                          axis=1).reshape(FOX_BASE, c)
    fox = jnp.concatenate([w[3072 + 1024 * i:4096 + 1024 * i].reshape(8, 128, c) for i in range(4)],
                          axis=1).reshape(FOX_PAIRS * FOX_W, c)
    ff = jnp.pad(w[FF_BASE:D_IN], ((0, D_IN_PAD - D_IN), (0, 0)))
    return jnp.concatenate([ret, fox, ff], axis=0)


def _ungroup_rows(w):
    c = w.shape[1]
    ret = w[:FOX_BASE].reshape(4, RET_W, c)
    fox = w[FOX_BASE:FF_BASE].reshape(8, FOX_W, c)
    return jnp.concatenate(
        [ret[:, 0:128].reshape(512, c), ret[:, 128:256].reshape(512, c),
         ret[:, 256:512].reshape(1024, c), ret[:, 512:768].reshape(1024, c)]
        + [fox[:, 128 * i:128 * i + 128].reshape(1024, c) for i in range(4)]
        + [w[FF_BASE:FF_BASE + FOX_HEADS]], axis=0)


def _mm(a, b, *, tm, tn, out_dtype, name, ta=False, nt=False, add=None):
    k, m = (a.shape if ta else a.shape[::-1])
    n = b.shape[0] if nt else b.shape[1]
    assert m % tm == 0 and n % tn == 0 and not (ta and nt)

    def body(a_ref, b_ref, *rest):
        o_ref = rest[-1]
        if ta:
            prod = _dot_tn(a_ref[...], b_ref[...])
        elif nt:
            prod = _dot_nt(a_ref[...], b_ref[...])
        else:
            prod = _dot(a_ref[...], b_ref[...])
        if add is not None:
            prod = prod + rest[0][...]
        o_ref[...] = prod.astype(out_dtype)

    in_specs = [pl.BlockSpec((k, tm), lambda j, i: (0, i)) if ta
                else pl.BlockSpec((tm, k), lambda j, i: (i, 0)),
                pl.BlockSpec((tn, k), lambda j, i: (j, 0)) if nt
                else pl.BlockSpec((k, tn), lambda j, i: (0, j))]
    args = [a, b]
    if add is not None:
        in_specs.append(pl.BlockSpec((tm, tn), lambda j, i: (i, j)))
        args.append(add)
    return pl.pallas_call(
        body, name=name, grid=(n // tn, m // tm),
        in_specs=in_specs,
        out_specs=pl.BlockSpec((tm, tn), lambda j, i: (i, j)),
        out_shape=jax.ShapeDtypeStruct((m, n), out_dtype),
        compiler_params=_params(dimension_semantics=("arbitrary", "arbitrary")),
    )(*args)


def _rms_fwd(h_pad, g):
    def body(h_ref, g_ref, u_ref):
        h = h_ref[...]
        r = lax.rsqrt(jnp.mean(h * h, axis=-1, keepdims=True) + EPS)
        u_ref[...] = (h * r * g_ref[...]).astype(BF16)

    return pl.pallas_call(
        body, name="rms_fwd", grid=(NCHUNK,),
        in_specs=[pl.BlockSpec((CHUNK, D_MODEL), lambda i: (i, 0)),
                  pl.BlockSpec((1, D_MODEL), lambda i: (0, 0))],
        out_specs=pl.BlockSpec((CHUNK, D_MODEL), lambda i: (i, 0)),
        out_shape=jax.ShapeDtypeStruct((T, D_MODEL), BF16),
        compiler_params=_params(dimension_semantics=("arbitrary",)),
    )(h_pad, g)


def _final_loss(out_pad, target, g):
    def body(o_ref, t_ref, g_ref, d_ref, db_ref, loss_ref, dg_ref):
        i = pl.program_id(0)

        @pl.when(i == 0)
        def _():
            d_ref[...] = jnp.zeros_like(d_ref)
            db_ref[...] = jnp.zeros_like(db_ref)
            loss_ref[...] = jnp.zeros_like(loss_ref)
            dg_ref[...] = jnp.zeros_like(dg_ref)

        @pl.when(i > 0)
        def _():
            o = o_ref[...]
            g = g_ref[...]
            r = lax.rsqrt(jnp.mean(o * o, axis=-1, keepdims=True) + EPS)
            xn = o * r
            e = xn * g - t_ref[...]
            loss_ref[...] += jnp.full(loss_ref.shape, 0.5 / D_MODEL * jnp.sum(e * e), F32)
            do = e * (1.0 / D_MODEL)
            dg_ref[...] += jnp.sum(do * xn, axis=0, keepdims=True)
            dn = do * g
            d = r * (dn - xn * jnp.mean(dn * xn, axis=-1, keepdims=True))
            d_ref[...] = d
            db_ref[...] = d.astype(BF16)

    return pl.pallas_call(
        body, name="final_loss", grid=(NCHUNK,),
        in_specs=[pl.BlockSpec((CHUNK, D_MODEL), lambda i: (i, 0)),
                  pl.BlockSpec((CHUNK, D_MODEL), lambda i: (jnp.maximum(i - 1, 0), 0)),
                  pl.BlockSpec((1, D_MODEL), lambda i: (0, 0))],
        out_specs=[pl.BlockSpec((CHUNK, D_MODEL), lambda i: (i, 0)),
                   pl.BlockSpec((CHUNK, D_MODEL), lambda i: (i, 0)),
                   pl.BlockSpec((8, 128), lambda i: (0, 0)),
                   pl.BlockSpec((1, D_MODEL), lambda i: (0, 0))],
        out_shape=[jax.ShapeDtypeStruct((T, D_MODEL), F32),
                   jax.ShapeDtypeStruct((T, D_MODEL), BF16),
                   jax.ShapeDtypeStruct((8, 128), F32),
                   jax.ShapeDtypeStruct((1, D_MODEL), F32)],
        compiler_params=_params(dimension_semantics=("arbitrary",)),
    )(out_pad, target, g)


def _coords():
    return lax.axis_index("x"), lax.axis_index("y"), lax.axis_index("c")


def _flip(v, bit):
    return 1 - v if bit else v


def _peer(x, y, c, r):
    return _flip(x, (r >> 2) & 1), _flip(y, (r >> 1) & 1), _flip(c, r & 1)


def _direct_exchange(ins, outs, send_sems, recv_sems, local_sems, gather, chips_only=False):
    x, y, c = _coords()
    me = 2 * x + y if chips_only else 4 * x + 2 * y + c

    def src(k, to_idx):
        return ins[k] if gather else ins[k].at[to_idx]

    local = [pltpu.make_async_copy(src(k, me), outs[k].at[me], local_sems.at[k])
             for k in range(len(ins))]
    sends, recvs = [], []
    for r in range(1, N_CHIP if chips_only else N_DEV):
        px, py, pc = _peer(x, y, c, 2 * r if chips_only else r)
        peer = 2 * px + py if chips_only else 4 * px + 2 * py + pc
        for k in range(len(ins)):
            sems = dict(send_sem=send_sems.at[k, r - 1], recv_sem=recv_sems.at[k, r - 1],
                        device_id=(px, py, pc), device_id_type=MESH)
            sends.append(pltpu.make_async_remote_copy(src_ref=src(k, peer), dst_ref=outs[k].at[me], **sems))
            recvs.append(pltpu.make_async_remote_copy(src_ref=src(k, peer), dst_ref=outs[k].at[peer], **sems))

    def start():
        for cp in local + sends:
            cp.start()

    def wait():
        for cp in recvs:
            cp.wait_recv()
        for cp in sends:
            cp.wait_send()
        for cp in local:
            cp.wait()

    return start, wait


def _exchange_sems(n_arr, n_peer=N_DEV - 1):
    return [pltpu.SemaphoreType.DMA((n_arr, n_peer)), pltpu.SemaphoreType.DMA((n_arr, n_peer)),
            pltpu.SemaphoreType.DMA((n_arr,))]


def _exchange_shape(a, gather):
    return jax.ShapeDtypeStruct(((N_DEV,) + a.shape) if gather else a.shape, a.dtype)


def _gather_two_level(arrays, name):
    n_arr = len(arrays)

    def body(*refs):
        ins, outs = refs[:n_arr], refs[n_arr:2 * n_arr]
        send_sems, recv_sems, local_sems = refs[2 * n_arr:]
        x, y, c = _coords()
        sibling = (x, y, 1 - c)
        chips = [(1 - x, y), (x, 1 - y), (1 - x, 1 - y)]

        def slot(k, px, py, pc):
            return outs[k].at[4 * px + 2 * py + pc]

        def copy(k, j, block, to, own):
            return pltpu.make_async_remote_copy(
                src_ref=ins[k] if own else slot(k, *block), dst_ref=slot(k, *block),
                send_sem=send_sems.at[k, j], recv_sem=recv_sems.at[k, j],
                device_id=to, device_id_type=MESH)

        local = [pltpu.make_async_copy(ins[k], slot(k, x, y, c), local_sems.at[k]) for k in range(n_arr)]
        first, passed = [], []
        for k in range(n_arr):
            first.append(copy(k, 0, (x, y, c), sibling, True))
            first += [copy(k, 1 + j, (x, y, c), (*chip, c), True) for j, chip in enumerate(chips)]
        for cp in local + first:
            cp.start()
        for j, chip in enumerate(chips):
            for k in range(n_arr):
                copy(k, 1 + j, (*chip, c), (x, y, c), False).wait_recv()
                fwd = copy(k, 4 + j, (*chip, c), sibling, False)
                fwd.start()
                passed.append(fwd)
        for k in range(n_arr):
            copy(k, 0, sibling, (x, y, c), False).wait_recv()
            for j, chip in enumerate(chips):
                copy(k, 4 + j, (*chip, 1 - c), (x, y, c), False).wait_recv()
        for cp in first + passed:
            cp.wait_send()
        for cp in local:
            cp.wait()

    return pl.pallas_call(
        body, name=name,
        in_specs=[ANY] * n_arr, out_specs=[ANY] * n_arr,
        out_shape=[_exchange_shape(a, True) for a in arrays],
        scratch_shapes=_exchange_sems(n_arr),
    )(*arrays)


def _sibling_swap(blocks):
    def body(b_ref, o_ref, send_sems, recv_sems):
        x, y, c = _coords()
        copies = [pltpu.make_async_remote_copy(
            src_ref=b_ref.at[2 * q + (1 - c)], dst_ref=o_ref.at[q],
            send_sem=send_sems.at[q], recv_sem=recv_sems.at[q],
            device_id=(x, y, 1 - c), device_id_type=MESH) for q in range(N_CHIP)]
        for cp in copies:
            cp.start()
        for cp in copies:
            cp.wait_recv()
        for cp in copies:
            cp.wait_send()

    return pl.pallas_call(
        body, name="sibling_swap", in_specs=[ANY], out_specs=ANY,
        out_shape=jax.ShapeDtypeStruct((N_CHIP,) + blocks.shape[1:], blocks.dtype),
        scratch_shapes=[pltpu.SemaphoreType.DMA((N_CHIP,)), pltpu.SemaphoreType.DMA((N_CHIP,))],
    )(blocks)


def _pair_sum(blocks, theirs, core):
    _, r, c = blocks.shape

    def body(core_ref, a_ref, b_ref, o_ref):
        o_ref[...] = (a_ref[...].astype(F32) + b_ref[...].astype(F32)).astype(BF16)

    return pl.pallas_call(
        body, name="pair_sum",
        grid_spec=pltpu.PrefetchScalarGridSpec(
            num_scalar_prefetch=1, grid=(N_CHIP,),
            in_specs=[pl.BlockSpec((1, r, c), lambda q, core_ref: (2 * q + core_ref[0], 0, 0)),
                      pl.BlockSpec((1, r, c), lambda q, core_ref: (q, 0, 0))],
            out_specs=pl.BlockSpec((1, r, c), lambda q, core_ref: (q, 0, 0))),
        out_shape=jax.ShapeDtypeStruct((N_CHIP, r, c), BF16),
        compiler_params=_params(dimension_semantics=("arbitrary",)),
    )(core, blocks, theirs)


def _du_rms(dz, wt_pad, h_pad, dout, g, pair_blocks):
    tm = 272
    steps = T // tm

    def body(dz_ref, w_ref, h_ref, d_ref, g_ref, p_ref, dh_ref, dg_ref, got_ref,
             send_sems, recv_sems, local_sems):
        start, wait = _direct_exchange([p_ref], [got_ref], send_sems, recv_sems, local_sems,
                                       False, chips_only=True)
        i = pl.program_id(0)
        pl.when(i == 0)(start)

        @pl.when(i == 0)
        def _():
            dg_ref[...] = jnp.zeros_like(dg_ref)

        du = _dot(dz_ref[...], w_ref[...])
        h = h_ref[...]
        r = lax.rsqrt(jnp.mean(h * h, axis=-1, keepdims=True) + EPS)
        xn = h * r
        dg_ref[...] += jnp.sum(du * xn, axis=0, keepdims=True)
        dn = du * g_ref[...]
        dh_ref[...] = d_ref[...] + r * (dn - xn * jnp.mean(dn * xn, axis=-1, keepdims=True))

        pl.when(i == steps - 1)(wait)

    tile = pl.BlockSpec((tm, D_MODEL), lambda i: (i, 0))
    return pl.pallas_call(
        body, name="du_rms", grid=(steps,),
        in_specs=[pl.BlockSpec((tm, D_IN_PAD), lambda i: (i, 0)),
                  pl.BlockSpec((D_IN_PAD, D_MODEL), lambda i: (0, 0)),
                  tile, tile, pl.BlockSpec((1, D_MODEL), lambda i: (0, 0)), ANY],
        out_specs=[tile, pl.BlockSpec((1, D_MODEL), lambda i: (0, 0)), ANY],
        out_shape=[jax.ShapeDtypeStruct((T, D_MODEL), F32),
                   jax.ShapeDtypeStruct((1, D_MODEL), F32),
                   jax.ShapeDtypeStruct(pair_blocks.shape, pair_blocks.dtype)],
        scratch_shapes=_exchange_sems(1, N_CHIP - 1),
        compiler_params=_params(dimension_semantics=("arbitrary",)),
    )(dz, wt_pad, h_pad, dout, g, pair_blocks)


def _tri(lower):
    r = lax.broadcasted_iota(jnp.int32, (CHUNK, CHUNK), 0)
    c = lax.broadcasted_iota(jnp.int32, (CHUNK, CHUNK), 1)
    return jnp.where((r >= c) if lower else (r <= c), 1.0, 0.0).astype(F32)


def _row_valid(n):
    r = lax.broadcasted_iota(jnp.int32, (CHUNK, 128), 0) + n * CHUNK
    return r >= PAD


_FF_SPEC = pl.BlockSpec((T, 128), lambda i: (0, FF_BASE // 128))


def _forget_fwd(z, b_pad):
    def body(z_ref, b_ref, o_ref):
        tri = _tri(True)
        carry = jnp.zeros((1, 128), F32)
        for n in range(NCHUNK):
            rows = pl.ds(n * CHUNK, CHUNK)
            a = z_ref[rows, :] + b_ref[...]
            lf = -(jnp.maximum(-a, 0.0) + jnp.log(1.0 + jnp.exp(-jnp.abs(a))))
            lf = jnp.where(_row_valid(n), lf, 0.0)
            c = jnp.dot(tri, lf, precision=lax.Precision.HIGHEST,
                        preferred_element_type=F32) + carry
            carry = c[CHUNK - 1:CHUNK, :]
            o_ref[:, rows] = jnp.where(_row_valid(n), -c, NEG_INF).T

    return pl.pallas_call(
        body, name="forget_fwd", grid=(1,),
        in_specs=[_FF_SPEC, pl.BlockSpec((1, 128), lambda i: (0, 0))],
        out_specs=pl.BlockSpec((128, T), lambda i: (0, 0)),
        out_shape=jax.ShapeDtypeStruct((128, T), F32),
        compiler_params=_params(dimension_semantics=("arbitrary",)),
    )(z, b_pad)


def _forget_bwd(z, b_pad, dc, dz):
    def body(z_ref, b_ref, dc_ref, dz_in, dff_ref, db_ref):
        tri = _tri(False)
        carry = jnp.zeros((1, 128), F32)
        db = jnp.zeros((1, 128), F32)
        for n in reversed(range(NCHUNK)):
            rows = pl.ds(n * CHUNK, CHUNK)
            dc_blk = jnp.concatenate([dc_ref[:, rows], jnp.zeros((128 - FOX_HEADS, CHUNK), F32)], axis=0).T
            dlf = jnp.dot(tri, dc_blk, precision=lax.Precision.HIGHEST,
                          preferred_element_type=F32) + carry
            carry = dlf[0:1, :]
            a = z_ref[rows, :] + b_ref[...]
            dff = jnp.where(_row_valid(n), dlf * jax.nn.sigmoid(-a), 0.0)
            dff_ref[rows, :] = dff.astype(BF16)
            db = db + jnp.sum(dff, axis=0, keepdims=True)
        db_ref[...] = db

    return pl.pallas_call(
        body, name="forget_bwd", grid=(1,),
        in_specs=[_FF_SPEC, pl.BlockSpec((1, 128), lambda i: (0, 0)),
                  pl.BlockSpec((FOX_HEADS, T), lambda i: (0, 0)), ANY],
        out_specs=[_FF_SPEC, pl.BlockSpec((1, 128), lambda i: (0, 0))],
        out_shape=[jax.ShapeDtypeStruct((T, D_IN_PAD), BF16),
                   jax.ShapeDtypeStruct((1, 128), F32)],
        input_output_aliases={3: 0},
        compiler_params=_params(dimension_semantics=("arbitrary",)),
    )(z, b_pad, dc, dz)


def _causal_bias():
    r = lax.broadcasted_iota(jnp.int32, (CHUNK, CHUNK), 0)
    c = lax.broadcasted_iota(jnp.int32, (CHUNK, CHUNK), 1)
    return jnp.where(c <= r, 0.0, NEG_INF).astype(F32)


def _fox_exp(q_blk, k_all, bias, causal, i):
    lo, hi = i * CHUNK, (i + 1) * CHUNK
    s_off = _dot_nt(q_blk, k_all[:lo]) + bias[:, :lo]
    s_dia = _dot_nt(q_blk, k_all[lo:hi]) + (bias[:, lo:hi] + causal)
    m = jnp.maximum(jnp.max(s_off, axis=-1, keepdims=True), jnp.max(s_dia, axis=-1, keepdims=True))
    e_off = jnp.exp(s_off - m)
    e_dia = jnp.exp(s_dia - m)
    inv = 1.0 / (jnp.sum(e_off, axis=-1, keepdims=True) + jnp.sum(e_dia, axis=-1, keepdims=True))
    return e_off, e_dia, inv


_FOX_Z_SPEC = pl.BlockSpec((T, FOX_W), lambda p: (0, FOX_BASE // FOX_W + p))
_FOX_BIAS_SPEC = pl.BlockSpec((2, 1, T), lambda p: (p, 0, 0))
_FOX_SCALE = FOX_D ** -0.5


def _fox_fwd(z, bias, y, w_out_blk):
    last = FOX_PAIRS - 1

    def body(z_ref, b_ref, y_in, w_ref, a_ref, y_ref, wall_ref, send_sems, recv_sems, local_sems):
        start, wait = _direct_exchange([w_ref], [wall_ref], send_sems, recv_sems, local_sems, True)
        pl.when(pl.program_id(0) == 0)(start)

        causal = _causal_bias()
        a_ref[pl.ds(0, CHUNK), :] = jnp.zeros((CHUNK, 128), F32)
        y_ref[pl.ds(0, CHUNK), :] = jnp.zeros((CHUNK, 128), BF16)
        for j in range(2):
            lanes = pl.ds(j * FOX_D, FOX_D)
            k_all = z_ref[:, pl.ds(128 + j * FOX_D, FOX_D)].astype(BF16)
            v_all = z_ref[:, pl.ds(256 + j * FOX_D, FOX_D)].astype(BF16)
            bias = b_ref[j]
            for i in range(1, NCHUNK):
                rows = pl.ds(i * CHUNK, CHUNK)
                lo, hi = i * CHUNK, (i + 1) * CHUNK
                q_blk = (z_ref[rows, lanes] * _FOX_SCALE).astype(BF16)
                e_off, e_dia, inv = _fox_exp(q_blk, k_all, bias, causal, i)
                o = (_dot(e_off.astype(BF16), v_all[:lo]) + _dot(e_dia.astype(BF16), v_all[lo:hi])) * inv
                a_ref[rows, lanes] = o
                gate = _silu_parts(z_ref[rows, pl.ds(384 + j * FOX_D, FOX_D)])[0]
                y_ref[rows, lanes] = (o * gate).astype(BF16)

        pl.when(pl.program_id(0) == last)(wait)

    return pl.pallas_call(
        body, name="fox_fwd", grid=(FOX_PAIRS,),
        in_specs=[_FOX_Z_SPEC, _FOX_BIAS_SPEC, ANY, ANY],
        out_specs=[pl.BlockSpec((T, 128), lambda p: (0, p)),
                   pl.BlockSpec((T, 128), lambda p: (0, 8 + p)), ANY],
        out_shape=[jax.ShapeDtypeStruct((T, FOX_HEADS * FOX_D), F32),
                   jax.ShapeDtypeStruct((T, D_MIX), BF16),
                   _exchange_shape(w_out_blk, True)],
        input_output_aliases={2: 1},
        scratch_shapes=_exchange_sems(1),
        compiler_params=_params(dimension_semantics=("arbitrary",)),
    )(z, bias, y, w_out_blk)


def _fox_bwd(z, bias, a_f, dy, dz, dwo_blocks):
    last = FOX_PAIRS - 1

    def body(z_ref, b_ref, a_ref, dy_ref, dz_in, dwo_ref, dz_ref, dc_ref, got_ref,
             dk_acc, dv_acc, dc_acc, send_sems, recv_sems, local_sems):
        start, wait = _direct_exchange([dwo_ref], [got_ref], send_sems, recv_sems, local_sems, False)
        pl.when(pl.program_id(0) == 0)(start)

        causal = _causal_bias()
        dz_ref[pl.ds(0, CHUNK), pl.ds(0, 128)] = jnp.zeros((CHUNK, 128), BF16)
        dz_ref[pl.ds(0, CHUNK), pl.ds(384, 128)] = jnp.zeros((CHUNK, 128), BF16)
        for j in range(2):
            lanes = pl.ds(j * FOX_D, FOX_D)
            k_all = z_ref[:, pl.ds(128 + j * FOX_D, FOX_D)].astype(BF16)
            v_all = z_ref[:, pl.ds(256 + j * FOX_D, FOX_D)].astype(BF16)
            bias = b_ref[j]
            dk_acc[...] = jnp.zeros_like(dk_acc)
            dv_acc[...] = jnp.zeros_like(dv_acc)
            dc_acc[...] = jnp.zeros_like(dc_acc)
            for i in range(1, NCHUNK):
                rows = pl.ds(i * CHUNK, CHUNK)
                lo, hi = i * CHUNK, (i + 1) * CHUNK
                off, dia = pl.ds(0, lo), pl.ds(lo, CHUNK)
                q_blk = (z_ref[rows, lanes] * _FOX_SCALE).astype(BF16)
                e_off, e_dia, inv = _fox_exp(q_blk, k_all, bias, causal, i)
                p_off, p_dia = e_off * inv, e_dia * inv
                sg, dsg = _silu_parts(z_ref[rows, pl.ds(384 + j * FOX_D, FOX_D)])
                dyj = dy_ref[rows, lanes]
                dz_ref[rows, pl.ds(384 + j * FOX_D, FOX_D)] = (dyj * a_ref[rows, lanes] * dsg).astype(BF16)
                do_b = (dyj * sg).astype(BF16)
                dp_off = _dot_nt(do_b, v_all[:lo])
                dp_dia = _dot_nt(do_b, v_all[lo:hi])
                d = (jnp.sum(p_off * dp_off, axis=-1, keepdims=True)
                     + jnp.sum(p_dia * dp_dia, axis=-1, keepdims=True))
                ds_off = p_off * (dp_off - d)
                ds_dia = p_dia * (dp_dia - d)
                dc_acc[:, off] -= jnp.sum(ds_off, axis=0, keepdims=True)
                dc_acc[:, dia] -= jnp.sum(ds_dia, axis=0, keepdims=True)
                ds_off_b, ds_dia_b = ds_off.astype(BF16), ds_dia.astype(BF16)
                dq = _dot(ds_off_b, k_all[:lo]) + _dot(ds_dia_b, k_all[lo:hi])
                dz_ref[rows, lanes] = (dq * _FOX_SCALE).astype(BF16)
                dk_acc[off, :] += _dot_tn(ds_off_b, q_blk)
                dk_acc[dia, :] += _dot_tn(ds_dia_b, q_blk)
                dv_acc[off, :] += _dot_tn(p_off.astype(BF16), do_b)
                dv_acc[dia, :] += _dot_tn(p_dia.astype(BF16), do_b)
            dz_ref[:, pl.ds(128 + j * FOX_D, FOX_D)] = dk_acc[...].astype(BF16)
            dz_ref[:, pl.ds(256 + j * FOX_D, FOX_D)] = dv_acc[...].astype(BF16)
            dc_ref[j] = dc_acc[...]

        pl.when(pl.program_id(0) == last)(wait)

    col = lambda base: pl.BlockSpec((T, 128), lambda p: (0, base + p))
    return pl.pallas_call(
        body, name="fox_bwd", grid=(FOX_PAIRS,),
        in_specs=[_FOX_Z_SPEC, _FOX_BIAS_SPEC, col(0), col(8), ANY, ANY],
        out_specs=[_FOX_Z_SPEC, _FOX_BIAS_SPEC, ANY],
        out_shape=[jax.ShapeDtypeStruct((T, D_IN_PAD), BF16),
                   jax.ShapeDtypeStruct((FOX_HEADS, 1, T), F32),
                   _exchange_shape(dwo_blocks, False)],
        input_output_aliases={4: 0},
        scratch_shapes=[pltpu.VMEM((T, FOX_D), F32), pltpu.VMEM((T, FOX_D), F32),
                        pltpu.VMEM((1, T), F32)] + _exchange_sems(1),
        compiler_params=_params(dimension_semantics=("arbitrary",)),
    )(z, bias, a_f, dy, dz, dwo_blocks)


def _rot(x, cosf, sins):
    return x * cosf + pltpu.roll(x, RET_DK // 2, 1) * sins


def _rot_t(d, cosf, sins):
    return d * cosf - pltpu.roll(d, RET_DK // 2, 1) * sins


_RET_Z_SPEC = pl.BlockSpec((T, RET_W), lambda h: (0, h))
_RET_TABLE_SPECS = [
    pl.BlockSpec((T, RET_DK), lambda h: (0, 0)),
    pl.BlockSpec((T, RET_DK), lambda h: (0, 0)),
    pl.BlockSpec((1, CHUNK, CHUNK), lambda h: (h, 0, 0)),
    pl.BlockSpec((1, CHUNK, 1), lambda h: (h, 0, 0)),
    pl.BlockSpec((1, CHUNK, 1), lambda h: (h, 0, 0)),
    pl.BlockSpec((1, 1, 1), lambda h: (h, 0, 0)),
]
_RQ, _RK = pl.ds(0, RET_DK), pl.ds(RET_DK, RET_DK)
_RV, _RG = pl.ds(2 * RET_DK, RET_DV), pl.ds(2 * RET_DK + RET_DV, RET_DV)
_RET_KSCALE = RET_DK ** -0.5


def _ret_fwd(z, tables):
    def body(z_ref, cos_ref, sin_ref, dm_ref, zeta_ref, xi_ref, cd_ref, raw_ref, y_ref):
        dmask, zeta, xi, cdec = dm_ref[0], zeta_ref[0], xi_ref[0], cd_ref[0]
        state = jnp.zeros((RET_DK, RET_DV), F32)
        for n in range(NCHUNK):
            rows = pl.ds(n * CHUNK, CHUNK)
            cosf, sins = cos_ref[rows, :], sin_ref[rows, :]
            qr = _rot(z_ref[rows, _RQ], cosf, sins)
            kr_b = (_rot(z_ref[rows, _RK], cosf, sins) * _RET_KSCALE).astype(BF16)
            v = z_ref[rows, _RV]
            a = _dot_nt(qr.astype(BF16), kr_b) * dmask
            out = _dot(a.astype(BF16), v.astype(BF16)) + _dot((qr * xi).astype(BF16), state.astype(BF16))
            state = state * cdec + _dot_tn(kr_b, (v * zeta).astype(BF16))
            raw_ref[rows, :] = out
            r = lax.rsqrt(jnp.mean(out * out, axis=-1, keepdims=True) + EPS)
            y_ref[rows, :] = (out * r * _silu_parts(z_ref[rows, _RG])[0]).astype(BF16)

    wide = pl.BlockSpec((T, RET_DV), lambda h: (0, h))
    return pl.pallas_call(
        body, name="ret_fwd", grid=(RET_HEADS,),
        in_specs=[_RET_Z_SPEC] + _RET_TABLE_SPECS,
        out_specs=[wide, wide],
        out_shape=[jax.ShapeDtypeStruct((T, RET_HEADS * RET_DV), F32),
                   jax.ShapeDtypeStruct((T, D_MIX), BF16)],
        compiler_params=_params(dimension_semantics=("arbitrary",)),
    )(z, *tables)


def _ret_bwd(z, tables, raw, dy):
    def body(z_ref, cos_ref, sin_ref, dm_ref, zeta_ref, xi_ref, cd_ref, raw_ref, dy_ref,
             dz_ref, st_ref):
        dmask, zeta, xi, cdec = dm_ref[0], zeta_ref[0], xi_ref[0], cd_ref[0]

        def rotated(n):
            rows = pl.ds(n * CHUNK, CHUNK)
            cosf, sins = cos_ref[rows, :], sin_ref[rows, :]
            qr = _rot(z_ref[rows, _RQ], cosf, sins)
            kr_b = (_rot(z_ref[rows, _RK], cosf, sins) * _RET_KSCALE).astype(BF16)
            return rows, cosf, sins, qr, kr_b

        state = jnp.zeros((RET_DK, RET_DV), F32)
        for n in range(NCHUNK):
            st_ref[n] = state.astype(BF16)
            if n + 1 < NCHUNK:
                rows, _, _, _, kr_b = rotated(n)
                state = state * cdec + _dot_tn(kr_b, (z_ref[rows, _RV] * zeta).astype(BF16))

        grad_state = jnp.zeros((RET_DK, RET_DV), F32)
        for n in reversed(range(NCHUNK)):
            rows, cosf, sins, qr, kr_b = rotated(n)
            qr_b = qr.astype(BF16)
            v_b = z_ref[rows, _RV].astype(BF16)
            gs_b = grad_state.astype(BF16)
            o = raw_ref[rows, :]
            r = lax.rsqrt(jnp.mean(o * o, axis=-1, keepdims=True) + EPS)
            hn = o * r
            sg, dsg = _silu_parts(z_ref[rows, _RG])
            dyn = dy_ref[rows, :]
            dz_ref[rows, _RG] = (dyn * hn * dsg).astype(BF16)
            dhn = dyn * sg
            do_b = (r * (dhn - hn * jnp.mean(dhn * hn, axis=-1, keepdims=True))).astype(BF16)
            a_b = (_dot_nt(qr_b, kr_b) * dmask).astype(BF16)
            da_b = (_dot_nt(do_b, v_b) * dmask).astype(BF16)
            dqr = _dot(da_b, kr_b) + xi * _dot_nt(do_b, st_ref[n])
            dkr = _dot_tn(da_b, qr_b) + zeta * _dot_nt(v_b, gs_b)
            dv = _dot_tn(a_b, do_b) + zeta * _dot(kr_b, gs_b)
            grad_state = grad_state * cdec + _dot_tn((qr * xi).astype(BF16), do_b)
            dz_ref[rows, _RQ] = _rot_t(dqr, cosf, sins).astype(BF16)
            dz_ref[rows, _RK] = (_rot_t(dkr, cosf, sins) * _RET_KSCALE).astype(BF16)
            dz_ref[rows, _RV] = dv.astype(BF16)

    wide = pl.BlockSpec((T, RET_DV), lambda h: (0, h))
    return pl.pallas_call(
        body, name="ret_bwd", grid=(RET_HEADS,),
        in_specs=[_RET_Z_SPEC] + _RET_TABLE_SPECS + [wide, wide],
        out_specs=_RET_Z_SPEC,
        out_shape=jax.ShapeDtypeStruct((T, D_IN_PAD), BF16),
        scratch_shapes=[pltpu.VMEM((NCHUNK, RET_DK, RET_DV), BF16)],
        compiler_params=_params(dimension_semantics=("arbitrary",)),
    )(z, *tables, raw, dy)


def _adamw(w, g, m, v):
    m = ADAM_B1 * m + (1.0 - ADAM_B1) * g
    v = ADAM_B2 * v + (1.0 - ADAM_B2) * (g * g)
    m_hat = m / (1.0 - ADAM_B1 ** ADAM_STEP)
    v_hat = v / (1.0 - ADAM_B2 ** ADAM_STEP)
    delta = -ADAM_LR * (m_hat / (jnp.sqrt(v_hat) + ADAM_EPS) + ADAM_WD * w)
    return delta, m, v


def _sum_adamw(parts, w, m, v, rows, name):
    _, r_tot, cols = parts.shape
    assert r_tot % rows == 0

    def body(p_ref, w_ref, m_ref, v_ref, g_ref, d_ref, nm_ref, nv_ref):
        g = p_ref[0].astype(F32)
        for d in range(1, N_DEV):
            g = g + p_ref[d].astype(F32)
        delta, nm, nv = _adamw(w_ref[...], g, m_ref[...], v_ref[...])
        g_ref[...] = g
        d_ref[...] = delta
        nm_ref[...] = nm
        nv_ref[...] = nv

    blk = pl.BlockSpec((rows, cols), lambda i: (i, 0))
    return pl.pallas_call(
        body, name=name, grid=(r_tot // rows,),
        in_specs=[pl.BlockSpec((N_DEV, rows, cols), lambda i: (0, i, 0)), blk, blk, blk],
        out_specs=[blk] * 4,
        out_shape=[jax.ShapeDtypeStruct((r_tot, cols), F32)] * 4,
        compiler_params=_params(dimension_semantics=("arbitrary",)),
    )(parts, w, m, v)


def _sum_adamw_w_in(parts, w, m, v, small):
    n_part, r, c = parts.shape
    steps = c // 128

    def body(p_ref, w_ref, m_ref, v_ref, s_ref, g_ref, d_ref, nm_ref, nv_ref, got_ref,
             send_sems, recv_sems, local_sems):
        start, wait = _direct_exchange([s_ref], [got_ref], send_sems, recv_sems, local_sems, True)
        pl.when(pl.program_id(0) == 0)(start)
        g = p_ref[0].astype(F32)
        for d in range(1, n_part):
            g = g + p_ref[d].astype(F32)
        delta, nm, nv = _adamw(w_ref[...], g, m_ref[...], v_ref[...])
        g_ref[...] = g
        d_ref[...] = delta
        nm_ref[...] = nm
        nv_ref[...] = nv
        pl.when(pl.program_id(0) == steps - 1)(wait)

    blk = pl.BlockSpec((r, 128), lambda i: (0, i))
    return pl.pallas_call(
        body, name="adamw_w_in", grid=(steps,),
        in_specs=[pl.BlockSpec((n_part, r, 128), lambda i: (0, 0, i)), blk, blk, blk, ANY],
        out_specs=[blk] * 4 + [ANY],
        out_shape=[jax.ShapeDtypeStruct((r, c), F32)] * 4 + [_exchange_shape(small, True)],
        scratch_shapes=_exchange_sems(1),
        compiler_params=_params(dimension_semantics=("arbitrary",)),
    )(parts, w, m, v, small)


def kernel(x, meta_tokens, norm_g, w_in, b_f, w_out, final_g, loss_target, m_meta_tokens, m_norm_g, m_w_in, m_b_f, m_w_out, m_final_g, v_meta_tokens, v_norm_g, v_w_in, v_b_f, v_w_out, v_final_g):
    core = lax.axis_index("c")
    me = 4 * lax.axis_index("x") + 2 * lax.axis_index("y") + core
    tables = _tables()

    wt_all, meta_all = _gather_two_level([w_in[0].T.astype(BF16), meta_tokens], name="gather_w_in")
    wt_pad = _regroup_rows(wt_all.reshape(D_IN, D_MODEL))
    meta_full = jnp.transpose(meta_all, (1, 0, 2)).reshape(N_META, D_MODEL)
    h_pad = jnp.concatenate([jnp.zeros((PAD, D_MODEL), F32), meta_full, x[0]], axis=0)
    b_pad = jnp.pad(b_f, ((0, 0), (0, 128 - FOX_HEADS)))

    u = _rms_fwd(h_pad, norm_g)
    z = _mm(u, wt_pad, nt=True, tm=T, tn=384, out_dtype=F32, name="mm_z")
    bias = _forget_fwd(z, b_pad)[:FOX_HEADS].reshape(FOX_HEADS, 1, T)
    raw, y = _ret_fwd(z, tables)
    a_f, y, w_out_all = _fox_fwd(z, bias, y, w_out[0].astype(BF16))
    w_out_b = w_out_all.reshape(D_MIX, D_MODEL)
    out_pad = _mm(y, w_out_b, tm=T, tn=256, out_dtype=F32, name="mm_out", add=h_pad)
    dout, dout_b, loss_blk, d_final_g = _final_loss(out_pad, loss_target[0], final_g.reshape(1, D_MODEL))

    dy = _mm(dout_b, w_out_b, nt=True, tm=T, tn=512, out_dtype=F32, name="mm_dy")
    d_w_out = _mm(y, dout_b, ta=True, tm=D_MIX, tn=256, out_dtype=BF16, name="mm_dwout")
    dz = _ret_bwd(z, tables, raw, dy)
    dz, dc, got_w_out = _fox_bwd(z, bias, a_f, dy, dz, d_w_out.reshape(N_DEV, WO_BLK, D_MODEL))
    dz, db_f = _forget_bwd(z, b_pad, dc.reshape(FOX_HEADS, T), dz)
    d_wt_pad = _mm(dz, u, ta=True, tm=384, tn=D_MODEL, out_dtype=BF16, name="mm_dwin")

    d_wt_blocks = _ungroup_rows(d_wt_pad).reshape(N_DEV, W_BLK, D_MODEL)
    theirs = _sibling_swap(d_wt_blocks)
    pair = _pair_sum(d_wt_blocks, theirs, core.astype(jnp.int32).reshape(1))
    dh, d_norm_g, got_w_in = _du_rms(dz, wt_pad, h_pad, dout, norm_g, pair)

    small = jnp.concatenate([
        dh[PAD:CHUNK], d_norm_g, d_final_g, jnp.pad(db_f[:, :FOX_HEADS], ((0, 0), (0, D_MODEL - FOX_HEADS))),
        jnp.zeros((SMALL_ROWS - N_META - 3, D_MODEL), F32)], axis=0)
    g_w_in, d_w_in, nm_w_in, nv_w_in, got_small = _sum_adamw_w_in(
        got_w_in, w_in[0].T, m_w_in[0].T, v_w_in[0].T, small)
    g_w_out, d_w_out, nm_w_out, nv_w_out = _sum_adamw(got_w_out, w_out[0], m_w_out[0], v_w_out[0], 128, "adamw_w_out")

    row = lambda a: a.reshape(1, -1)
    wide = lambda a: jnp.pad(row(a), ((0, 0), (0, D_MODEL - a.size)))

    def small_pack(meta, ng, fg, bf):
        return jnp.concatenate([meta, row(ng), row(fg), wide(bf),
                                jnp.zeros((SMALL_ROWS - N_META - 3, D_MODEL), F32)], axis=0)

    meta_at = lambda a: lax.dynamic_update_slice(jnp.zeros((N_META, D_MODEL), F32), a, (0, me * META_BLK))
    g_s, d_s, nm_s, nv_s = _sum_adamw(
        got_small,
        small_pack(meta_at(meta_tokens), norm_g, final_g, b_f),
        small_pack(meta_at(m_meta_tokens), m_norm_g, m_final_g, m_b_f),
        small_pack(meta_at(v_meta_tokens), v_norm_g, v_final_g, v_b_f),
        SMALL_ROWS, "adamw_small")

    def unpack(s):
        meta = lax.dynamic_slice(s[:N_META], (0, me * META_BLK), (N_META, META_BLK))
        return meta, s[N_META:N_META + 1], s[N_META + 1], s[N_META + 2:N_META + 3, :FOX_HEADS]

    g_meta, g_ng, g_fg, g_bf = unpack(g_s)
    d_meta, d_ng, d_fg, d_bf = unpack(d_s)
    nm_meta, nm_ng, nm_fg, nm_bf = unpack(nm_s)
    nv_meta, nv_ng, nv_fg, nv_bf = unpack(nv_s)

    back = lambda a: a.T[None]
    loss = lax.psum(loss_blk[0, 0], MESH_AXES)
    return (loss, dh[CHUNK:][None],
            g_meta, g_ng, back(g_w_in), g_bf, g_w_out[None], g_fg,
            d_meta, d_ng, back(d_w_in), d_bf, d_w_out[None], d_fg,
            nm_meta, nm_ng, back(nm_w_in), nm_bf, nm_w_out[None], nm_fg,
            nv_meta, nv_ng, back(nv_w_in), nv_bf, nv_w_out[None], nv_fg)
```

```python
import numpy as np
import jax
import jax.numpy as jnp
from jax import lax
from jax.experimental import pallas as pl
from jax.experimental.pallas import tpu as pltpu

F32 = jnp.float32
BF16 = jnp.bfloat16

N_DEV = 8
N_CHIP = 4
D_MODEL = 1024
SEQ = 2048
N_META = 16
CHUNK = 128
PAD = CHUNK - N_META
T = SEQ + CHUNK
NCHUNK = T // CHUNK
D_MIX = 2048
RET_HEADS = 4
RET_DK = 128
RET_DV = 256
RET_W = 2 * RET_DK + 2 * RET_DV
FOX_HEADS = 16
FOX_D = 64
FOX_PAIRS = FOX_HEADS // 2
FOX_W = 4 * 128
FOX_BASE = RET_HEADS * RET_W
FF_BASE = FOX_BASE + FOX_PAIRS * FOX_W
D_IN = 7184
D_IN_PAD = 7296
W_BLK = D_IN // N_DEV
WO_BLK = D_MIX // N_DEV
META_BLK = D_MODEL // N_DEV
EPS = 1e-6
NEG_INF = -1e30
ROPE_BASE = 10000.0
LOG2E = 1.4426950408889634
LN2 = 0.6931471805599453

ADAM_LR = 0.001
ADAM_B1 = 0.9
ADAM_B2 = 0.999
ADAM_EPS = 1e-08
ADAM_WD = 0.01
ADAM_STEP = 10

SMALL_ROWS = 24
VMEM_LIMIT = 56 * 1024 * 1024
MESH = pl.DeviceIdType.MESH
ANY = pl.BlockSpec(memory_space=pl.ANY)

_NT = (((1,), (1,)), ((), ()))
_TN = (((0,), (0,)), ((), ()))


def _dot(a, b):
    return jnp.dot(a, b, preferred_element_type=F32)


def _dot_nt(a, b):
    return lax.dot_general(a, b, _NT, preferred_element_type=F32)


def _dot_tn(a, b):
    return lax.dot_general(a, b, _TN, preferred_element_type=F32)


def _params(**kw):
    return pltpu.CompilerParams(vmem_limit_bytes=VMEM_LIMIT, **kw)


def _silu_parts(g):
    sig = jax.nn.sigmoid(g)
    return g * sig, sig * (1.0 + g * (1.0 - sig))


def _tables():
    pos = np.arange(T, dtype=np.float32) - PAD
    inv = (ROPE_BASE ** (-np.arange(0, RET_DK, 2, dtype=np.float32) / RET_DK)).astype(np.float32)
    ang = pos[:, None] * inv[None, :]
    cos, sin = np.cos(ang), np.sin(ang)
    cosf = np.concatenate([cos, cos], axis=1).astype(np.float32)
    sins = np.concatenate([-sin, sin], axis=1).astype(np.float32)
    h = np.arange(RET_HEADS, dtype=np.float32)
    log_gamma = np.log1p(-np.exp2(-5.0 - h)).astype(np.float32)
    idx = np.arange(CHUNK, dtype=np.float32)
    diff = idx[:, None] - idx[None, :]
    dmask = np.where(diff[None] >= 0,
                     np.exp(log_gamma[:, None, None] * np.maximum(diff, 0.0)[None]), 0.0)
    zeta = np.exp(log_gamma[:, None] * (CHUNK - 1.0 - idx)[None, :])
    xi = np.exp(log_gamma[:, None] * (idx + 1.0)[None, :])
    cdec = np.exp(log_gamma * CHUNK)
    return (jnp.asarray(cosf), jnp.asarray(sins), jnp.asarray(dmask, F32),
            jnp.asarray(zeta[:, :, None], F32), jnp.asarray(xi[:, :, None], F32),
            jnp.asarray(cdec[:, None, None], F32))


def _regroup_rows(w):
    c = w.shape[1]
    ret = jnp.concatenate([w[0:512].reshape(4, 128, c), w[512:1024].reshape(4, 128, c),
                           w[1024:2048].reshape(4, 256, c), w[2048:3072].reshape(4, 256, c)],
                          axis=1).reshape(FOX_BASE, c)
    fox = jnp.concatenate([w[3072 + 1024 * i:4096 + 1024 * i].reshape(8, 128, c) for i in range(4)],
                          axis=1).reshape(FOX_PAIRS * FOX_W, c)
    ff = jnp.pad(w[FF_BASE:D_IN], ((0, D_IN_PAD - D_IN), (0, 0)))
    return jnp.concatenate([ret, fox, ff], axis=0)


def _ungroup_rows(w):
    c = w.shape[1]
    ret = w[:FOX_BASE].reshape(4, RET_W, c)
    fox = w[FOX_BASE:FF_BASE].reshape(8, FOX_W, c)
    return jnp.concatenate(
        [ret[:, 0:128].reshape(512, c), ret[:, 128:256].reshape(512, c),
         ret[:, 256:512].reshape(1024, c), ret[:, 512:768].reshape(1024, c)]
        + [fox[:, 128 * i:128 * i + 128].reshape(1024, c) for i in range(4)]
        + [w[FF_BASE:FF_BASE + FOX_HEADS]], axis=0)


def _mm(a, b, *, tm, tn, out_dtype, name, ta=False, nt=False, add=None):
    k, m = (a.shape if ta else a.shape[::-1])
    n = b.shape[0] if nt else b.shape[1]
    assert m % tm == 0 and n % tn == 0 and not (ta and nt)

    def body(a_ref, b_ref, *rest):
        o_ref = rest[-1]
        if ta:
            prod = _dot_tn(a_ref[...], b_ref[...])
        elif nt:
            prod = _dot_nt(a_ref[...], b_ref[...])
        else:
            prod = _dot(a_ref[...], b_ref[...])
        if add is not None:
            prod = prod + rest[0][...]
        o_ref[...] = prod.astype(out_dtype)

    in_specs = [pl.BlockSpec((k, tm), lambda j, i: (0, i)) if ta
                else pl.BlockSpec((tm, k), lambda j, i: (i, 0)),
                pl.BlockSpec((tn, k), lambda j, i: (j, 0)) if nt
                else pl.BlockSpec((k, tn), lambda j, i: (0, j))]
    args = [a, b]
    if add is not None:
        in_specs.append(pl.BlockSpec((tm, tn), lambda j, i: (i, j)))
        args.append(add)
    return pl.pallas_call(
        body, name=name, grid=(n // tn, m // tm),
        in_specs=in_specs,
        out_specs=pl.BlockSpec((tm, tn), lambda j, i: (i, j)),
        out_shape=jax.ShapeDtypeStruct((m, n), out_dtype),
        compiler_params=_params(dimension_semantics=("arbitrary", "arbitrary")),
    )(*args)


def _rms_fwd(h_pad, g):
    def body(h_ref, g_ref, u_ref):
        h = h_ref[...]
        r = lax.rsqrt(jnp.mean(h * h, axis=-1, keepdims=True) + EPS)
        u_ref[...] = (h * r * g_ref[...]).astype(BF16)

    return pl.pallas_call(
        body, name="rms_fwd", grid=(NCHUNK,),
        in_specs=[pl.BlockSpec((CHUNK, D_MODEL), lambda i: (i, 0)),
                  pl.BlockSpec((1, D_MODEL), lambda i: (0, 0))],
        out_specs=pl.BlockSpec((CHUNK, D_MODEL), lambda i: (i, 0)),
        out_shape=jax.ShapeDtypeStruct((T, D_MODEL), BF16),
        compiler_params=_params(dimension_semantics=("arbitrary",)),
    )(h_pad, g)


def _final_loss(out_pad, target, g):
    def body(o_ref, t_ref, g_ref, d_ref, db_ref, loss_ref, dg_ref):
        i = pl.program_id(0)

        @pl.when(i == 0)
        def _():
            d_ref[...] = jnp.zeros_like(d_ref)
            db_ref[...] = jnp.zeros_like(db_ref)
            loss_ref[...] = jnp.zeros_like(loss_ref)
            dg_ref[...] = jnp.zeros_like(dg_ref)

        @pl.when(i > 0)
        def _():
            o = o_ref[...]
            g = g_ref[...]
            r = lax.rsqrt(jnp.mean(o * o, axis=-1, keepdims=True) + EPS)
            xn = o * r
            e = xn * g - t_ref[...]
            loss_ref[...] += jnp.full(loss_ref.shape, 0.5 / D_MODEL * jnp.sum(e * e), F32)
            do = e * (1.0 / D_MODEL)
            dg_ref[...] += jnp.sum(do * xn, axis=0, keepdims=True)
            dn = do * g
            d = r * (dn - xn * jnp.mean(dn * xn, axis=-1, keepdims=True))
            d_ref[...] = d
            db_ref[...] = d.astype(BF16)

    return pl.pallas_call(
        body, name="final_loss", grid=(NCHUNK,),
        in_specs=[pl.BlockSpec((CHUNK, D_MODEL), lambda i: (i, 0)),
                  pl.BlockSpec((CHUNK, D_MODEL), lambda i: (jnp.maximum(i - 1, 0), 0)),
                  pl.BlockSpec((1, D_MODEL), lambda i: (0, 0))],
        out_specs=[pl.BlockSpec((CHUNK, D_MODEL), lambda i: (i, 0)),
                   pl.BlockSpec((CHUNK, D_MODEL), lambda i: (i, 0)),
                   pl.BlockSpec((8, 128), lambda i: (0, 0)),
                   pl.BlockSpec((1, D_MODEL), lambda i: (0, 0))],
        out_shape=[jax.ShapeDtypeStruct((T, D_MODEL), F32),
                   jax.ShapeDtypeStruct((T, D_MODEL), BF16),
                   jax.ShapeDtypeStruct((8, 128), F32),
                   jax.ShapeDtypeStruct((1, D_MODEL), F32)],
        compiler_params=_params(dimension_semantics=("arbitrary",)),
    )(out_pad, target, g)


def _coords():
    return lax.axis_index("x"), lax.axis_index("y"), lax.axis_index("c")


def _flip(v, bit):
    return 1 - v if bit else v


def _peer(x, y, c, r):
    return _flip(x, (r >> 2) & 1), _flip(y, (r >> 1) & 1), _flip(c, r & 1)


def _direct_exchange(ins, outs, send_sems, recv_sems, local_sems, gather, chips_only=False):
    x, y, c = _coords()
    me = 2 * x + y if chips_only else 4 * x + 2 * y + c

    def src(k, to_idx):
        return ins[k] if gather else ins[k].at[to_idx]

    local = [pltpu.make_async_copy(src(k, me), outs[k].at[me], local_sems.at[k])
             for k in range(len(ins))]
    sends, recvs = [], []
    for r in range(1, N_CHIP if chips_only else N_DEV):
        px, py, pc = _peer(x, y, c, 2 * r if chips_only else r)
        peer = 2 * px + py if chips_only else 4 * px + 2 * py + pc
        for k in range(len(ins)):
            sems = dict(send_sem=send_sems.at[k, r - 1], recv_sem=recv_sems.at[k, r - 1],
                        device_id=(px, py, pc), device_id_type=MESH)
            sends.append(pltpu.make_async_remote_copy(src_ref=src(k, peer), dst_ref=outs[k].at[me], **sems))
            recvs.append(pltpu.make_async_remote_copy(src_ref=src(k, peer), dst_ref=outs[k].at[peer], **sems))

    def start():
        for cp in local + sends:
            cp.start()

    def wait():
        for cp in recvs:
            cp.wait_recv()
        for cp in sends:
            cp.wait_send()
        for cp in local:
            cp.wait()

    return start, wait


def _exchange_sems(n_arr, n_peer=N_DEV - 1):
    return [pltpu.SemaphoreType.DMA((n_arr, n_peer)), pltpu.SemaphoreType.DMA((n_arr, n_peer)),
            pltpu.SemaphoreType.DMA((n_arr,))]


def _exchange_shape(a, gather):
    return jax.ShapeDtypeStruct(((N_DEV,) + a.shape) if gather else a.shape, a.dtype)


def _gather_two_level(arrays, name):
    n_arr = len(arrays)

    def body(*refs):
        ins, outs = refs[:n_arr], refs[n_arr:2 * n_arr]
        send_sems, recv_sems, local_sems = refs[2 * n_arr:]
        x, y, c = _coords()
        sibling = (x, y, 1 - c)
        chips = [(1 - x, y), (x, 1 - y), (1 - x, 1 - y)]

        def slot(k, px, py, pc):
            return outs[k].at[4 * px + 2 * py + pc]

        def copy(k, j, block, to, own):
            return pltpu.make_async_remote_copy(
                src_ref=ins[k] if own else slot(k, *block), dst_ref=slot(k, *block),
                send_sem=send_sems.at[k, j], recv_sem=recv_sems.at[k, j],
                device_id=to, device_id_type=MESH)

        local = [pltpu.make_async_copy(ins[k], slot(k, x, y, c), local_sems.at[k]) for k in range(n_arr)]
        first, passed = [], []
        for k in range(n_arr):
            first.append(copy(k, 0, (x, y, c), sibling, True))
            first += [copy(k, 1 + j, (x, y, c), (*chip, c), True) for j, chip in enumerate(chips)]
        for cp in local + first:
            cp.start()
        for j, chip in enumerate(chips):
            for k in range(n_arr):
                copy(k, 1 + j, (*chip, c), (x, y, c), False).wait_recv()
                fwd = copy(k, 4 + j, (*chip, c), sibling, False)
                fwd.start()
                passed.append(fwd)
        for k in range(n_arr):
            copy(k, 0, sibling, (x, y, c), False).wait_recv()
            for j, chip in enumerate(chips):
                copy(k, 4 + j, (*chip, 1 - c), (x, y, c), False).wait_recv()
        for cp in first + passed:
            cp.wait_send()
        for cp in local:
            cp.wait()

    return pl.pallas_call(
        body, name=name,
        in_specs=[ANY] * n_arr, out_specs=[ANY] * n_arr,
        out_shape=[_exchange_shape(a, True) for a in arrays],
        scratch_shapes=_exchange_sems(n_arr),
    )(*arrays)


def _sibling_swap(blocks):
    def body(b_ref, o_ref, send_sems, recv_sems):
        x, y, c = _coords()
        copies = [pltpu.make_async_remote_copy(
            src_ref=b_ref.at[2 * q + (1 - c)], dst_ref=o_ref.at[q],
            send_sem=send_sems.at[q], recv_sem=recv_sems.at[q],
            device_id=(x, y, 1 - c), device_id_type=MESH) for q in range(N_CHIP)]
        for cp in copies:
            cp.start()
        for cp in copies:
            cp.wait_recv()
        for cp in copies:
            cp.wait_send()

    return pl.pallas_call(
        body, name="sibling_swap", in_specs=[ANY], out_specs=ANY,
        out_shape=jax.ShapeDtypeStruct((N_CHIP,) + blocks.shape[1:], blocks.dtype),
        scratch_shapes=[pltpu.SemaphoreType.DMA((N_CHIP,)), pltpu.SemaphoreType.DMA((N_CHIP,))],
    )(blocks)


def _pair_sum(blocks, theirs, core):
    _, r, c = blocks.shape

    def body(core_ref, a_ref, b_ref, o_ref):
        o_ref[...] = (a_ref[...].astype(F32) + b_ref[...].astype(F32)).astype(BF16)

    return pl.pallas_call(
        body, name="pair_sum",
        grid_spec=pltpu.PrefetchScalarGridSpec(
            num_scalar_prefetch=1, grid=(N_CHIP,),
            in_specs=[pl.BlockSpec((1, r, c), lambda q, core_ref: (2 * q + core_ref[0], 0, 0)),
                      pl.BlockSpec((1, r, c), lambda q, core_ref: (q, 0, 0))],
            out_specs=pl.BlockSpec((1, r, c), lambda q, core_ref: (q, 0, 0))),
        out_shape=jax.ShapeDtypeStruct((N_CHIP, r, c), BF16),
        compiler_params=_params(dimension_semantics=("arbitrary",)),
    )(core, blocks, theirs)


def _du_rms(dz, wt_pad, h_pad, dout, g, pair_blocks):
    tm = 272
    steps = T // tm

    def body(dz_ref, w_ref, h_ref, d_ref, g_ref, p_ref, dh_ref, dg_ref, got_ref,
             send_sems, recv_sems, local_sems):
        start, wait = _direct_exchange([p_ref], [got_ref], send_sems, recv_sems, local_sems,
                                       False, chips_only=True)
        i = pl.program_id(0)
        pl.when(i == 0)(start)

        @pl.when(i == 0)
        def _():
            dg_ref[...] = jnp.zeros_like(dg_ref)

        du = _dot(dz_ref[...], w_ref[...])
        h = h_ref[...]
        r = lax.rsqrt(jnp.mean(h * h, axis=-1, keepdims=True) + EPS)
        xn = h * r
        dg_ref[...] += jnp.sum(du * xn, axis=0, keepdims=True)
        dn = du * g_ref[...]
        dh_ref[...] = d_ref[...] + r * (dn - xn * jnp.mean(dn * xn, axis=-1, keepdims=True))

        pl.when(i == steps - 1)(wait)

    tile = pl.BlockSpec((tm, D_MODEL), lambda i: (i, 0))
    return pl.pallas_call(
        body, name="du_rms", grid=(steps,),
        in_specs=[pl.BlockSpec((tm, D_IN_PAD), lambda i: (i, 0)),
                  pl.BlockSpec((D_IN_PAD, D_MODEL), lambda i: (0, 0)),
                  tile, tile, pl.BlockSpec((1, D_MODEL), lambda i: (0, 0)), ANY],
        out_specs=[tile, pl.BlockSpec((1, D_MODEL), lambda i: (0, 0)), ANY],
        out_shape=[jax.ShapeDtypeStruct((T, D_MODEL), F32),
                   jax.ShapeDtypeStruct((1, D_MODEL), F32),
                   jax.ShapeDtypeStruct(pair_blocks.shape, pair_blocks.dtype)],
        scratch_shapes=_exchange_sems(1, N_CHIP - 1),
        compiler_params=_params(dimension_semantics=("arbitrary",)),
    )(dz, wt_pad, h_pad, dout, g, pair_blocks)


def _tri(lower):
    r = lax.broadcasted_iota(jnp.int32, (CHUNK, CHUNK), 0)
    c = lax.broadcasted_iota(jnp.int32, (CHUNK, CHUNK), 1)
    return jnp.where((r >= c) if lower else (r <= c), 1.0, 0.0).astype(F32)


def _row_valid(n):
    r = lax.broadcasted_iota(jnp.int32, (CHUNK, 128), 0) + n * CHUNK
    return r >= PAD


_FF_SPEC = pl.BlockSpec((T, 128), lambda i: (0, FF_BASE // 128))


def _forget_fwd(z, b_pad):
    def body(z_ref, b_ref, o_ref):
        tri = _tri(True)
        carry = jnp.zeros((1, 128), F32)
        for n in range(NCHUNK):
            rows = pl.ds(n * CHUNK, CHUNK)
            a = z_ref[rows, :] + b_ref[...]
            lf = -(jnp.maximum(-a, 0.0) + jnp.log(1.0 + jnp.exp(-jnp.abs(a))))
            lf = jnp.where(_row_valid(n), lf, 0.0)
            c = jnp.dot(tri, lf, precision=lax.Precision.HIGHEST,
                        preferred_element_type=F32) + carry
            carry = c[CHUNK - 1:CHUNK, :]
            o_ref[:, rows] = jnp.where(_row_valid(n), c * (-LOG2E), NEG_INF).T

    return pl.pallas_call(
        body, name="forget_fwd", grid=(1,),
        in_specs=[_FF_SPEC, pl.BlockSpec((1, 128), lambda i: (0, 0))],
        out_specs=pl.BlockSpec((128, T), lambda i: (0, 0)),
        out_shape=jax.ShapeDtypeStruct((128, T), F32),
        compiler_params=_params(dimension_semantics=("arbitrary",)),
    )(z, b_pad)


def _forget_bwd(z, b_pad, dc, dz):
    def body(z_ref, b_ref, dc_ref, dz_in, dff_ref, db_ref):
        tri = _tri(False)
        carry = jnp.zeros((1, 128), F32)
        db = jnp.zeros((1, 128), F32)
        for n in reversed(range(NCHUNK)):
            rows = pl.ds(n * CHUNK, CHUNK)
            dc_blk = jnp.concatenate([dc_ref[:, rows], jnp.zeros((128 - FOX_HEADS, CHUNK), F32)], axis=0).T
            dlf = jnp.dot(tri, dc_blk, precision=lax.Precision.HIGHEST,
                          preferred_element_type=F32) + carry
            carry = dlf[0:1, :]
            a = z_ref[rows, :] + b_ref[...]
            dff = jnp.where(_row_valid(n), dlf * jax.nn.sigmoid(-a), 0.0)
            dff_ref[rows, :] = dff.astype(BF16)
            db = db + jnp.sum(dff, axis=0, keepdims=True)
        db_ref[...] = db

    return pl.pallas_call(
        body, name="forget_bwd", grid=(1,),
        in_specs=[_FF_SPEC, pl.BlockSpec((1, 128), lambda i: (0, 0)),
                  pl.BlockSpec((FOX_HEADS, T), lambda i: (0, 0)), ANY],
        out_specs=[_FF_SPEC, pl.BlockSpec((1, 128), lambda i: (0, 0))],
        out_shape=[jax.ShapeDtypeStruct((T, D_IN_PAD), BF16),
                   jax.ShapeDtypeStruct((1, 128), F32)],
        input_output_aliases={3: 0},
        compiler_params=_params(dimension_semantics=("arbitrary",)),
    )(z, b_pad, dc, dz)


FOX_QB = 512
FOX_NQB = SEQ // FOX_QB


def _fox_block(b):
    lo = CHUNK + b * FOX_QB
    return pl.ds(lo, FOX_QB), lo, lo + FOX_QB


def _causal_bias():
    r = lax.broadcasted_iota(jnp.int32, (FOX_QB, FOX_QB), 0)
    c = lax.broadcasted_iota(jnp.int32, (FOX_QB, FOX_QB), 1)
    return jnp.where(c <= r, 0.0, NEG_INF).astype(F32)


def _fox_logits(q_blk, k_all, bias, causal, b):
    _, lo, hi = _fox_block(b)
    s_off = _dot_nt(q_blk, k_all[:lo]) + bias[:, :lo]
    s_dia = _dot_nt(q_blk, k_all[lo:hi]) + (bias[:, lo:hi] + causal)
    return s_off, s_dia


_FOX_Z_SPEC = pl.BlockSpec((T, FOX_W), lambda p: (0, FOX_BASE // FOX_W + p))
_FOX_BIAS_SPEC = pl.BlockSpec((2, 1, T), lambda p: (p, 0, 0))
_FOX_LSE_SPEC = pl.BlockSpec((2, T, 1), lambda p: (p, 0, 0))
_FOX_SCALE = FOX_D ** -0.5
_FOX_QSCALE = _FOX_SCALE * LOG2E


def _fox_fwd(z, bias, y, w_out_blk):
    last = FOX_PAIRS - 1

    def body(z_ref, b_ref, y_in, w_ref, a_ref, lse_ref, y_ref, wall_ref,
             send_sems, recv_sems, local_sems):
        start, wait = _direct_exchange([w_ref], [wall_ref], send_sems, recv_sems, local_sems, True)
        pl.when(pl.program_id(0) == 0)(start)

        causal = _causal_bias()
        a_ref[pl.ds(0, CHUNK), :] = jnp.zeros((CHUNK, 128), F32)
        y_ref[pl.ds(0, CHUNK), :] = jnp.zeros((CHUNK, 128), BF16)
        for j in range(2):
            lanes = pl.ds(j * FOX_D, FOX_D)
            k_all = z_ref[:, pl.ds(128 + j * FOX_D, FOX_D)].astype(BF16)
            v_all = z_ref[:, pl.ds(256 + j * FOX_D, FOX_D)].astype(BF16)
            bias = b_ref[j]
            lse_ref[j, pl.ds(0, CHUNK), :] = jnp.zeros((CHUNK, 1), F32)
            for b in range(FOX_NQB):
                rows, lo, hi = _fox_block(b)
                q_blk = (z_ref[rows, lanes] * _FOX_QSCALE).astype(BF16)
                s_off, s_dia = _fox_logits(q_blk, k_all, bias, causal, b)
                m = jnp.maximum(jnp.max(s_off, axis=-1, keepdims=True),
                                jnp.max(s_dia, axis=-1, keepdims=True))
                e_off = jnp.exp2(s_off - m)
                e_dia = jnp.exp2(s_dia - m)
                total = jnp.sum(e_off, axis=-1, keepdims=True) + jnp.sum(e_dia, axis=-1, keepdims=True)
                o = (_dot(e_off.astype(BF16), v_all[:lo]) + _dot(e_dia.astype(BF16), v_all[lo:hi])) / total
                a_ref[rows, lanes] = o
                lse_ref[j, rows, :] = m + jnp.log(total) * LOG2E
                gate = _silu_parts(z_ref[rows, pl.ds(384 + j * FOX_D, FOX_D)])[0]
                y_ref[rows, lanes] = (o * gate).astype(BF16)

        pl.when(pl.program_id(0) == last)(wait)

    return pl.pallas_call(
        body, name="fox_fwd", grid=(FOX_PAIRS,),
        in_specs=[_FOX_Z_SPEC, _FOX_BIAS_SPEC, ANY, ANY],
        out_specs=[pl.BlockSpec((T, 128), lambda p: (0, p)), _FOX_LSE_SPEC,
                   pl.BlockSpec((T, 128), lambda p: (0, 8 + p)), ANY],
        out_shape=[jax.ShapeDtypeStruct((T, FOX_HEADS * FOX_D), F32),
                   jax.ShapeDtypeStruct((FOX_HEADS, T, 1), F32),
                   jax.ShapeDtypeStruct((T, D_MIX), BF16),
                   _exchange_shape(w_out_blk, True)],
        input_output_aliases={2: 2},
        scratch_shapes=_exchange_sems(1),
        compiler_params=_params(dimension_semantics=("arbitrary",)),
    )(z, bias, y, w_out_blk)


def _fox_bwd(z, bias, a_f, lse, dy, dz, dwo_blocks):
    last = FOX_PAIRS - 1

    def body(z_ref, b_ref, a_ref, lse_ref, dy_ref, dz_in, dwo_ref, dz_ref, dc_ref, got_ref,
             kv_acc, dc_acc, send_sems, recv_sems, local_sems):
        start, wait = _direct_exchange([dwo_ref], [got_ref], send_sems, recv_sems, local_sems, False)
        pl.when(pl.program_id(0) == 0)(start)

        causal = _causal_bias()
        dz_ref[pl.ds(0, CHUNK), pl.ds(0, 128)] = jnp.zeros((CHUNK, 128), BF16)
        dz_ref[pl.ds(0, CHUNK), pl.ds(384, 128)] = jnp.zeros((CHUNK, 128), BF16)
        dk_rows, dv_rows = pl.ds(0, FOX_D), pl.ds(FOX_D, FOX_D)
        for j in range(2):
            lanes = pl.ds(j * FOX_D, FOX_D)
            k_all = z_ref[:, pl.ds(128 + j * FOX_D, FOX_D)].astype(BF16)
            v_all = z_ref[:, pl.ds(256 + j * FOX_D, FOX_D)].astype(BF16)
            bias = b_ref[j]
            kv_acc[...] = jnp.zeros_like(kv_acc)
            dc_acc[...] = jnp.zeros_like(dc_acc)
            for b in range(FOX_NQB):
                rows, lo, hi = _fox_block(b)
                off, dia = pl.ds(0, lo), pl.ds(lo, FOX_QB)
                q_blk = (z_ref[rows, lanes] * _FOX_QSCALE).astype(BF16)
                s_off, s_dia = _fox_logits(q_blk, k_all, bias, causal, b)
                lse_blk = lse_ref[j, rows, :]
                p_off, p_dia = jnp.exp2(s_off - lse_blk), jnp.exp2(s_dia - lse_blk)
                sg, dsg = _silu_parts(z_ref[rows, pl.ds(384 + j * FOX_D, FOX_D)])
                dyj = dy_ref[rows, lanes]
                dz_ref[rows, pl.ds(384 + j * FOX_D, FOX_D)] = (dyj * a_ref[rows, lanes] * dsg).astype(BF16)
                do_b = (dyj * sg).astype(BF16)
                dp_off = _dot_nt(do_b, v_all[:lo])
                dp_dia = _dot_nt(do_b, v_all[lo:hi])
                d = (jnp.sum(p_off * dp_off, axis=-1, keepdims=True)
                     + jnp.sum(p_dia * dp_dia, axis=-1, keepdims=True))
                ds_off = p_off * (dp_off - d)
                ds_dia = p_dia * (dp_dia - d)
                dc_acc[:, off] -= jnp.sum(ds_off, axis=0, keepdims=True)
                dc_acc[:, dia] -= jnp.sum(ds_dia, axis=0, keepdims=True)
                ds_off_b, ds_dia_b = ds_off.astype(BF16), ds_dia.astype(BF16)
                dq = _dot(ds_off_b, k_all[:lo]) + _dot(ds_dia_b, k_all[lo:hi])
                dz_ref[rows, lanes] = (dq * _FOX_SCALE).astype(BF16)
                kv_acc[dk_rows, off] += _dot_tn(q_blk, ds_off_b)
                kv_acc[dk_rows, dia] += _dot_tn(q_blk, ds_dia_b)
                kv_acc[dv_rows, off] += _dot_tn(do_b, p_off.astype(BF16))
                kv_acc[dv_rows, dia] += _dot_tn(do_b, p_dia.astype(BF16))
            for n in range(NCHUNK):
                rows = pl.ds(n * CHUNK, CHUNK)
                both = kv_acc[:, rows].T
                dz_ref[rows, pl.ds(128 + j * FOX_D, FOX_D)] = (both[:, :FOX_D] * LN2).astype(BF16)
                dz_ref[rows, pl.ds(256 + j * FOX_D, FOX_D)] = both[:, FOX_D:].astype(BF16)
            dc_ref[j] = dc_acc[...]

        pl.when(pl.program_id(0) == last)(wait)

    col = lambda base: pl.BlockSpec((T, 128), lambda p: (0, base + p))
    return pl.pallas_call(
        body, name="fox_bwd", grid=(FOX_PAIRS,),
        in_specs=[_FOX_Z_SPEC, _FOX_BIAS_SPEC, col(0), _FOX_LSE_SPEC, col(8), ANY, ANY],
        out_specs=[_FOX_Z_SPEC, _FOX_BIAS_SPEC, ANY],
        out_shape=[jax.ShapeDtypeStruct((T, D_IN_PAD), BF16),
                   jax.ShapeDtypeStruct((FOX_HEADS, 1, T), F32),
                   _exchange_shape(dwo_blocks, False)],
        input_output_aliases={5: 0},
        scratch_shapes=[pltpu.VMEM((2 * FOX_D, T), F32), pltpu.VMEM((1, T), F32)] + _exchange_sems(1),
        compiler_params=_params(dimension_semantics=("arbitrary",)),
    )(z, bias, a_f, lse, dy, dz, dwo_blocks)


def _rot(x, cosf, sins):
    return x * cosf + pltpu.roll(x, RET_DK // 2, 1) * sins


def _rot_t(d, cosf, sins):
    return d * cosf - pltpu.roll(d, RET_DK // 2, 1) * sins


_RET_Z_SPEC = pl.BlockSpec((T, RET_W), lambda h: (0, h))
_RET_TABLE_SPECS = [
    pl.BlockSpec((T, RET_DK), lambda h: (0, 0)),
    pl.BlockSpec((T, RET_DK), lambda h: (0, 0)),
    pl.BlockSpec((1, CHUNK, CHUNK), lambda h: (h, 0, 0)),
    pl.BlockSpec((1, CHUNK, 1), lambda h: (h, 0, 0)),
    pl.BlockSpec((1, CHUNK, 1), lambda h: (h, 0, 0)),
    pl.BlockSpec((1, 1, 1), lambda h: (h, 0, 0)),
]
_RQ, _RK = pl.ds(0, RET_DK), pl.ds(RET_DK, RET_DK)
_RV, _RG = pl.ds(2 * RET_DK, RET_DV), pl.ds(2 * RET_DK + RET_DV, RET_DV)
_RET_KSCALE = RET_DK ** -0.5


def _ret_fwd(z, tables):
    def body(z_ref, cos_ref, sin_ref, dm_ref, zeta_ref, xi_ref, cd_ref, raw_ref, y_ref):
        dmask, zeta, xi, cdec = dm_ref[0], zeta_ref[0], xi_ref[0], cd_ref[0]
        state = jnp.zeros((RET_DK, RET_DV), F32)
        for n in range(NCHUNK):
            rows = pl.ds(n * CHUNK, CHUNK)
            cosf, sins = cos_ref[rows, :], sin_ref[rows, :]
            qr = _rot(z_ref[rows, _RQ], cosf, sins)
            kr_b = (_rot(z_ref[rows, _RK], cosf, sins) * _RET_KSCALE).astype(BF16)
            v = z_ref[rows, _RV]
            a = _dot_nt(qr.astype(BF16), kr_b) * dmask
            out = _dot(a.astype(BF16), v.astype(BF16)) + _dot((qr * xi).astype(BF16), state.astype(BF16))
            state = state * cdec + _dot_tn(kr_b, (v * zeta).astype(BF16))
            raw_ref[rows, :] = out
            r = lax.rsqrt(jnp.mean(out * out, axis=-1, keepdims=True) + EPS)
            y_ref[rows, :] = (out * r * _silu_parts(z_ref[rows, _RG])[0]).astype(BF16)

    wide = pl.BlockSpec((T, RET_DV), lambda h: (0, h))
    return pl.pallas_call(
        body, name="ret_fwd", grid=(RET_HEADS,),
        in_specs=[_RET_Z_SPEC] + _RET_TABLE_SPECS,
        out_specs=[wide, wide],
        out_shape=[jax.ShapeDtypeStruct((T, RET_HEADS * RET_DV), F32),
                   jax.ShapeDtypeStruct((T, D_MIX), BF16)],
        compiler_params=_params(dimension_semantics=("arbitrary",)),
    )(z, *tables)


def _ret_bwd(z, tables, raw, dy):
    def body(z_ref, cos_ref, sin_ref, dm_ref, zeta_ref, xi_ref, cd_ref, raw_ref, dy_ref,
             dz_ref, st_ref):
        dmask, zeta, xi, cdec = dm_ref[0], zeta_ref[0], xi_ref[0], cd_ref[0]

        def rotated(n):
            rows = pl.ds(n * CHUNK, CHUNK)
            cosf, sins = cos_ref[rows, :], sin_ref[rows, :]
            qr = _rot(z_ref[rows, _RQ], cosf, sins)
            kr_b = (_rot(z_ref[rows, _RK], cosf, sins) * _RET_KSCALE).astype(BF16)
            return rows, cosf, sins, qr, kr_b

        state = jnp.zeros((RET_DK, RET_DV), F32)
        for n in range(NCHUNK):
            st_ref[n] = state.astype(BF16)
            if n + 1 < NCHUNK:
                rows, _, _, _, kr_b = rotated(n)
                state = state * cdec + _dot_tn(kr_b, (z_ref[rows, _RV] * zeta).astype(BF16))

        grad_state = jnp.zeros((RET_DK, RET_DV), F32)
        for n in reversed(range(NCHUNK)):
            rows, cosf, sins, qr, kr_b = rotated(n)
            qr_b = qr.astype(BF16)
            v_b = z_ref[rows, _RV].astype(BF16)
            gs_b = grad_state.astype(BF16)
            o = raw_ref[rows, :]
            r = lax.rsqrt(jnp.mean(o * o, axis=-1, keepdims=True) + EPS)
            hn = o * r
            sg, dsg = _silu_parts(z_ref[rows, _RG])
            dyn = dy_ref[rows, :]
            dz_ref[rows, _RG] = (dyn * hn * dsg).astype(BF16)
            dhn = dyn * sg
            do_b = (r * (dhn - hn * jnp.mean(dhn * hn, axis=-1, keepdims=True))).astype(BF16)
            a_b = (_dot_nt(qr_b, kr_b) * dmask).astype(BF16)
            da_b = (_dot_nt(do_b, v_b) * dmask).astype(BF16)
            dqr = _dot(da_b, kr_b) + xi * _dot_nt(do_b, st_ref[n])
            dkr = _dot_tn(da_b, qr_b) + zeta * _dot_nt(v_b, gs_b)
            dv = _dot_tn(a_b, do_b) + zeta * _dot(kr_b, gs_b)
            grad_state = grad_state * cdec + _dot_tn((qr * xi).astype(BF16), do_b)
            dz_ref[rows, _RQ] = _rot_t(dqr, cosf, sins).astype(BF16)
            dz_ref[rows, _RK] = (_rot_t(dkr, cosf, sins) * _RET_KSCALE).astype(BF16)
            dz_ref[rows, _RV] = dv.astype(BF16)

    wide = pl.BlockSpec((T, RET_DV), lambda h: (0, h))
    return pl.pallas_call(
        body, name="ret_bwd", grid=(RET_HEADS,),
        in_specs=[_RET_Z_SPEC] + _RET_TABLE_SPECS + [wide, wide],
        out_specs=_RET_Z_SPEC,
        out_shape=jax.ShapeDtypeStruct((T, D_IN_PAD), BF16),
        scratch_shapes=[pltpu.VMEM((NCHUNK, RET_DK, RET_DV), BF16)],
        compiler_params=_params(dimension_semantics=("arbitrary",)),
    )(z, *tables, raw, dy)


def _adamw(w, g, m, v):
    m = ADAM_B1 * m + (1.0 - ADAM_B1) * g
    v = ADAM_B2 * v + (1.0 - ADAM_B2) * (g * g)
    m_hat = m / (1.0 - ADAM_B1 ** ADAM_STEP)
    v_hat = v / (1.0 - ADAM_B2 ** ADAM_STEP)
    delta = -ADAM_LR * (m_hat / (jnp.sqrt(v_hat) + ADAM_EPS) + ADAM_WD * w)
    return delta, m, v


def _sum_adamw(parts, w, m, v, rows, name):
    _, r_tot, cols = parts.shape
    assert r_tot % rows == 0

    def body(p_ref, w_ref, m_ref, v_ref, g_ref, d_ref, nm_ref, nv_ref):
        g = p_ref[0].astype(F32)
        for d in range(1, N_DEV):
            g = g + p_ref[d].astype(F32)
        delta, nm, nv = _adamw(w_ref[...], g, m_ref[...], v_ref[...])
        g_ref[...] = g
        d_ref[...] = delta
        nm_ref[...] = nm
        nv_ref[...] = nv

    blk = pl.BlockSpec((rows, cols), lambda i: (i, 0))
    return pl.pallas_call(
        body, name=name, grid=(r_tot // rows,),
        in_specs=[pl.BlockSpec((N_DEV, rows, cols), lambda i: (0, i, 0)), blk, blk, blk],
        out_specs=[blk] * 4,
        out_shape=[jax.ShapeDtypeStruct((r_tot, cols), F32)] * 4,
        compiler_params=_params(dimension_semantics=("arbitrary",)),
    )(parts, w, m, v)


def _sum_adamw_w_in(parts, w, m, v, small):
    n_part, r, c = parts.shape
    steps = c // 128

    def body(p_ref, w_ref, m_ref, v_ref, s_ref, g_ref, d_ref, nm_ref, nv_ref, got_ref,
             send_sems, recv_sems, local_sems):
        start, wait = _direct_exchange([s_ref], [got_ref], send_sems, recv_sems, local_sems, True)
        pl.when(pl.program_id(0) == 0)(start)
        g = p_ref[0].astype(F32)
        for d in range(1, n_part):
            g = g + p_ref[d].astype(F32)
        delta, nm, nv = _adamw(w_ref[...], g, m_ref[...], v_ref[...])
        g_ref[...] = g
        d_ref[...] = delta
        nm_ref[...] = nm
        nv_ref[...] = nv
        pl.when(pl.program_id(0) == steps - 1)(wait)

    blk = pl.BlockSpec((r, 128), lambda i: (0, i))
    return pl.pallas_call(
        body, name="adamw_w_in", grid=(steps,),
        in_specs=[pl.BlockSpec((n_part, r, 128), lambda i: (0, 0, i)), blk, blk, blk, ANY],
        out_specs=[blk] * 4 + [ANY],
        out_shape=[jax.ShapeDtypeStruct((r, c), F32)] * 4 + [_exchange_shape(small, True)],
        scratch_shapes=_exchange_sems(1),
        compiler_params=_params(dimension_semantics=("arbitrary",)),
    )(parts, w, m, v, small)


def kernel(x, meta_tokens, norm_g, w_in, b_f, w_out, final_g, loss_target, m_meta_tokens, m_norm_g, m_w_in, m_b_f, m_w_out, m_final_g, v_meta_tokens, v_norm_g, v_w_in, v_b_f, v_w_out, v_final_g):
    core = lax.axis_index("c")
    me = 4 * lax.axis_index("x") + 2 * lax.axis_index("y") + core
    tables = _tables()

    wt_all, meta_all = _gather_two_level([w_in[0].T.astype(BF16), meta_tokens], name="gather_w_in")
    wt_pad = _regroup_rows(wt_all.reshape(D_IN, D_MODEL))
    meta_full = jnp.transpose(meta_all, (1, 0, 2)).reshape(N_META, D_MODEL)
    h_pad = jnp.concatenate([jnp.zeros((PAD, D_MODEL), F32), meta_full, x[0]], axis=0)
    b_pad = jnp.pad(b_f, ((0, 0), (0, 128 - FOX_HEADS)))

    u = _rms_fwd(h_pad, norm_g)
    z = _mm(u, wt_pad, nt=True, tm=T, tn=384, out_dtype=F32, name="mm_z")
    bias = _forget_fwd(z, b_pad)[:FOX_HEADS].reshape(FOX_HEADS, 1, T)
    raw, y = _ret_fwd(z, tables)
    a_f, lse, y, w_out_all = _fox_fwd(z, bias, y, w_out[0].astype(BF16))
    w_out_b = w_out_all.reshape(D_MIX, D_MODEL)
    out_pad = _mm(y, w_out_b, tm=T, tn=256, out_dtype=F32, name="mm_out", add=h_pad)
    dout, dout_b, loss_blk, d_final_g = _final_loss(out_pad, loss_target[0], final_g.reshape(1, D_MODEL))

    dy = _mm(dout_b, w_out_b, nt=True, tm=T, tn=512, out_dtype=F32, name="mm_dy")
    d_w_out = _mm(y, dout_b, ta=True, tm=D_MIX, tn=256, out_dtype=BF16, name="mm_dwout")
    dz = _ret_bwd(z, tables, raw, dy)
    dz, dc, got_w_out = _fox_bwd(z, bias, a_f, lse, dy, dz, d_w_out.reshape(N_DEV, WO_BLK, D_MODEL))
    dz, db_f = _forget_bwd(z, b_pad, dc.reshape(FOX_HEADS, T), dz)
    d_wt_pad = _mm(dz, u, ta=True, tm=384, tn=D_MODEL, out_dtype=BF16, name="mm_dwin")

    d_wt_blocks = _ungroup_rows(d_wt_pad).reshape(N_DEV, W_BLK, D_MODEL)
    theirs = _sibling_swap(d_wt_blocks)
    pair = _pair_sum(d_wt_blocks, theirs, core.astype(jnp.int32).reshape(1))
    dh, d_norm_g, got_w_in = _du_rms(dz, wt_pad, h_pad, dout, norm_g, pair)

    small = jnp.concatenate([
        dh[PAD:CHUNK], d_norm_g, d_final_g, jnp.pad(db_f[:, :FOX_HEADS], ((0, 0), (0, D_MODEL - FOX_HEADS))),
        jnp.pad(loss_blk[0:1], ((0, 0), (0, D_MODEL - 128))),
        jnp.zeros((SMALL_ROWS - N_META - 4, D_MODEL), F32)], axis=0)
    g_w_in, d_w_in, nm_w_in, nv_w_in, got_small = _sum_adamw_w_in(
        got_w_in, w_in[0].T, m_w_in[0].T, v_w_in[0].T, small)
    g_w_out, d_w_out, nm_w_out, nv_w_out = _sum_adamw(got_w_out, w_out[0], m_w_out[0], v_w_out[0], 128, "adamw_w_out")

    row = lambda a: a.reshape(1, -1)
    wide = lambda a: jnp.pad(row(a), ((0, 0), (0, D_MODEL - a.size)))

    def small_pack(meta, ng, fg, bf):
        return jnp.concatenate([meta, row(ng), row(fg), wide(bf),
                                jnp.zeros((SMALL_ROWS - N_META - 3, D_MODEL), F32)], axis=0)

    meta_at = lambda a: lax.dynamic_update_slice(jnp.zeros((N_META, D_MODEL), F32), a, (0, me * META_BLK))
    g_s, d_s, nm_s, nv_s = _sum_adamw(
        got_small,
        small_pack(meta_at(meta_tokens), norm_g, final_g, b_f),
        small_pack(meta_at(m_meta_tokens), m_norm_g, m_final_g, m_b_f),
        small_pack(meta_at(v_meta_tokens), v_norm_g, v_final_g, v_b_f),
        SMALL_ROWS, "adamw_small")

    def unpack(s):
        meta = lax.dynamic_slice(s[:N_META], (0, me * META_BLK), (N_META, META_BLK))
        return meta, s[N_META:N_META + 1], s[N_META + 1], s[N_META + 2:N_META + 3, :FOX_HEADS]

    g_meta, g_ng, g_fg, g_bf = unpack(g_s)
    d_meta, d_ng, d_fg, d_bf = unpack(d_s)
    nm_meta, nm_ng, nm_fg, nm_bf = unpack(nm_s)
    nv_meta, nv_ng, nv_fg, nv_bf = unpack(nv_s)

    back = lambda a: a.T[None]
    loss = g_s[N_META + 3, 0]
    return (loss, dh[CHUNK:][None],
            g_meta, g_ng, back(g_w_in), g_bf, g_w_out[None], g_fg,
            d_meta, d_ng, back(d_w_in), d_bf, d_w_out[None], d_fg,
            nm_meta, nm_ng, back(nm_w_in), nm_bf, nm_w_out[None], nm_fg,
            nv_meta, nv_ng, back(nv_w_in), nv_bf, nv_w_out[None], nv_fg)
```

```python
import numpy as np
import jax
import jax.numpy as jnp
from jax import lax
from jax.experimental import pallas as pl
from jax.experimental.pallas import tpu as pltpu

F32 = jnp.float32
BF16 = jnp.bfloat16

N_DEV = 8
N_CHIP = 4
D_MODEL = 1024
SEQ = 2048
N_META = 16
CHUNK = 128
PAD = CHUNK - N_META
T = SEQ + CHUNK
NCHUNK = T // CHUNK
D_MIX = 2048
RET_HEADS = 4
RET_DK = 128
RET_DV = 256
RET_W = 2 * RET_DK + 2 * RET_DV
FOX_HEADS = 16
FOX_D = 64
FOX_PAIRS = FOX_HEADS // 2
FOX_W = 4 * 128
FOX_BASE = RET_HEADS * RET_W
FF_BASE = FOX_BASE + FOX_PAIRS * FOX_W
D_IN = 7184
D_IN_PAD = 7296
W_BLK = D_IN // N_DEV
WO_BLK = D_MIX // N_DEV
META_BLK = D_MODEL // N_DEV
EPS = 1e-6
NEG_INF = -1e30
ROPE_BASE = 10000.0
LOG2E = 1.4426950408889634
LN2 = 0.6931471805599453

ADAM_LR = 0.001
ADAM_B1 = 0.9
ADAM_B2 = 0.999
ADAM_EPS = 1e-08
ADAM_WD = 0.01
ADAM_STEP = 10

SMALL_ROWS = 24
VMEM_LIMIT = 56 * 1024 * 1024
MESH = pl.DeviceIdType.MESH
ANY = pl.BlockSpec(memory_space=pl.ANY)

_NT = (((1,), (1,)), ((), ()))
_TN = (((0,), (0,)), ((), ()))


def _dot(a, b):
    return jnp.dot(a, b, preferred_element_type=F32)


def _dot_nt(a, b):
    return lax.dot_general(a, b, _NT, preferred_element_type=F32)


def _dot_tn(a, b):
    return lax.dot_general(a, b, _TN, preferred_element_type=F32)


def _params(**kw):
    return pltpu.CompilerParams(vmem_limit_bytes=VMEM_LIMIT, **kw)


def _silu_parts(g):
    sig = jax.nn.sigmoid(g)
    return g * sig, sig * (1.0 + g * (1.0 - sig))


def _tables():
    pos = np.arange(T, dtype=np.float32) - PAD
    inv = (ROPE_BASE ** (-np.arange(0, RET_DK, 2, dtype=np.float32) / RET_DK)).astype(np.float32)
    ang = pos[:, None] * inv[None, :]
    cos, sin = np.cos(ang), np.sin(ang)
    cosf = np.concatenate([cos, cos], axis=1).astype(np.float32)
    sins = np.concatenate([-sin, sin], axis=1).astype(np.float32)
    h = np.arange(RET_HEADS, dtype=np.float32)
    log_gamma = np.log1p(-np.exp2(-5.0 - h)).astype(np.float32)
    idx = np.arange(CHUNK, dtype=np.float32)
    diff = idx[:, None] - idx[None, :]
    dmask = np.where(diff[None] >= 0,
                     np.exp(log_gamma[:, None, None] * np.maximum(diff, 0.0)[None]), 0.0)
    zeta = np.exp(log_gamma[:, None] * (CHUNK - 1.0 - idx)[None, :])
    xi = np.exp(log_gamma[:, None] * (idx + 1.0)[None, :])
    cdec = np.exp(log_gamma * CHUNK)
    return (jnp.asarray(cosf), jnp.asarray(sins), jnp.asarray(dmask, F32),
            jnp.asarray(zeta[:, :, None], F32), jnp.asarray(xi[:, :, None], F32),
            jnp.asarray(cdec[:, None, None], F32))


def _regrouped_ranges():
    runs = []
    for h in range(RET_HEADS):
        runs += [(128 * h, 128), (512 + 128 * h, 128), (1024 + 256 * h, 256), (2048 + 256 * h, 256)]
    for p in range(FOX_PAIRS):
        runs += [(FOX_BASE + 1024 * i + 128 * p, 128) for i in range(4)]
    return runs + [(FF_BASE, D_IN - FF_BASE)]


def _regroup_rows(blocks):
    pieces = []
    for start, length in _regrouped_ranges():
        r = start
        while r < start + length:
            d = r // W_BLK
            end = min(start + length, (d + 1) * W_BLK)
            pieces.append(blocks[d, r - d * W_BLK:end - d * W_BLK])
            r = end
    pieces.append(jnp.zeros((D_IN_PAD - D_IN, blocks.shape[2]), blocks.dtype))
    return jnp.concatenate(pieces, axis=0)


def _ungroup_rows(w):
    where, g = {}, 0
    for start, length in _regrouped_ranges():
        for q in range(0, length, 128):
            where[start + q] = g + q
        g += length
    blocks = []
    for d in range(N_DEV):
        pieces, r = [], d * W_BLK
        while r < (d + 1) * W_BLK:
            base = r // 128 * 128
            end = min((d + 1) * W_BLK, base + 128)
            pieces.append(w[where[base] + r - base:where[base] + end - base])
            r = end
        blocks.append(jnp.concatenate(pieces, axis=0))
    return jnp.stack(blocks)


def _mm(a, b, *, tm, tn, out_dtype, name, ta=False, nt=False):
    k, m = (a.shape if ta else a.shape[::-1])
    n = b.shape[0] if nt else b.shape[1]
    assert m % tm == 0 and n % tn == 0 and not (ta and nt)

    def body(a_ref, b_ref, o_ref):
        if ta:
            prod = _dot_tn(a_ref[...], b_ref[...])
        elif nt:
            prod = _dot_nt(a_ref[...], b_ref[...])
        else:
            prod = _dot(a_ref[...], b_ref[...])
        o_ref[...] = prod.astype(out_dtype)

    return pl.pallas_call(
        body, name=name, grid=(n // tn, m // tm),
        in_specs=[pl.BlockSpec((k, tm), lambda j, i: (0, i)) if ta
                  else pl.BlockSpec((tm, k), lambda j, i: (i, 0)),
                  pl.BlockSpec((tn, k), lambda j, i: (j, 0)) if nt
                  else pl.BlockSpec((k, tn), lambda j, i: (0, j))],
        out_specs=pl.BlockSpec((tm, tn), lambda j, i: (i, j)),
        out_shape=jax.ShapeDtypeStruct((m, n), out_dtype),
        compiler_params=_params(dimension_semantics=("arbitrary", "arbitrary")),
    )(a, b)


def _rms_fwd(h_pad, g):
    def body(h_ref, g_ref, u_ref):
        h = h_ref[...]
        r = lax.rsqrt(jnp.mean(h * h, axis=-1, keepdims=True) + EPS)
        u_ref[...] = (h * r * g_ref[...]).astype(BF16)

    return pl.pallas_call(
        body, name="rms_fwd", grid=(NCHUNK,),
        in_specs=[pl.BlockSpec((CHUNK, D_MODEL), lambda i: (i, 0)),
                  pl.BlockSpec((1, D_MODEL), lambda i: (0, 0))],
        out_specs=pl.BlockSpec((CHUNK, D_MODEL), lambda i: (i, 0)),
        out_shape=jax.ShapeDtypeStruct((T, D_MODEL), BF16),
        compiler_params=_params(dimension_semantics=("arbitrary",)),
    )(h_pad, g)


def _out_loss_dy(y, w_out_b, h_pad, target_pad, g):
    tm = T // 4

    def body(y_ref, w_ref, h_ref, t_ref, g_ref, d_ref, db_ref, dy_ref, loss_ref, dg_ref):
        i = pl.program_id(0)

        @pl.when(i == 0)
        def _():
            loss_ref[...] = jnp.zeros_like(loss_ref)
            dg_ref[...] = jnp.zeros_like(dg_ref)

        w = w_ref[...]
        o = _dot(y_ref[...], w) + h_ref[...]
        token = lax.broadcasted_iota(jnp.int32, (tm, 1), 0) + i * tm >= CHUNK
        g = g_ref[...]
        r = lax.rsqrt(jnp.mean(o * o, axis=-1, keepdims=True) + EPS)
        xn = o * r
        e = jnp.where(token, xn * g - t_ref[...], 0.0)
        loss_ref[...] += jnp.full(loss_ref.shape, 0.5 / D_MODEL * jnp.sum(e * e), F32)
        do = e * (1.0 / D_MODEL)
        dg_ref[...] += jnp.sum(do * xn, axis=0, keepdims=True)
        dn = do * g
        d = r * (dn - xn * jnp.mean(dn * xn, axis=-1, keepdims=True))
        d_b = d.astype(BF16)
        d_ref[...] = d
        db_ref[...] = d_b
        dy_ref[...] = _dot_nt(d_b, w)

    tile = pl.BlockSpec((tm, D_MODEL), lambda i: (i, 0))
    wide = pl.BlockSpec((tm, D_MIX), lambda i: (i, 0))
    return pl.pallas_call(
        body, name="out_loss_dy", grid=(T // tm,),
        in_specs=[wide, pl.BlockSpec((D_MIX, D_MODEL), lambda i: (0, 0)), tile, tile,
                  pl.BlockSpec((1, D_MODEL), lambda i: (0, 0))],
        out_specs=[tile, tile, wide,
                   pl.BlockSpec((8, 128), lambda i: (0, 0)),
                   pl.BlockSpec((1, D_MODEL), lambda i: (0, 0))],
        out_shape=[jax.ShapeDtypeStruct((T, D_MODEL), F32),
                   jax.ShapeDtypeStruct((T, D_MODEL), BF16),
                   jax.ShapeDtypeStruct((T, D_MIX), F32),
                   jax.ShapeDtypeStruct((8, 128), F32),
                   jax.ShapeDtypeStruct((1, D_MODEL), F32)],
        compiler_params=_params(dimension_semantics=("arbitrary",)),
    )(y, w_out_b, h_pad, target_pad, g)


def _coords():
    return lax.axis_index("x"), lax.axis_index("y"), lax.axis_index("c")


def _flip(v, bit):
    return 1 - v if bit else v


def _peer(x, y, c, r):
    return _flip(x, (r >> 2) & 1), _flip(y, (r >> 1) & 1), _flip(c, r & 1)


def _direct_exchange(ins, outs, send_sems, recv_sems, local_sems, gather, chips_only=False):
    x, y, c = _coords()
    me = 2 * x + y if chips_only else 4 * x + 2 * y + c

    def src(k, to_idx):
        return ins[k] if gather else ins[k].at[to_idx]

    local = [pltpu.make_async_copy(src(k, me), outs[k].at[me], local_sems.at[k])
             for k in range(len(ins))]
    sends, recvs = [], []
    for r in range(1, N_CHIP if chips_only else N_DEV):
        px, py, pc = _peer(x, y, c, 2 * r if chips_only else r)
        peer = 2 * px + py if chips_only else 4 * px + 2 * py + pc
        for k in range(len(ins)):
            sems = dict(send_sem=send_sems.at[k, r - 1], recv_sem=recv_sems.at[k, r - 1],
                        device_id=(px, py, pc), device_id_type=MESH)
            sends.append(pltpu.make_async_remote_copy(src_ref=src(k, peer), dst_ref=outs[k].at[me], **sems))
            recvs.append(pltpu.make_async_remote_copy(src_ref=src(k, peer), dst_ref=outs[k].at[peer], **sems))

    def start():
        for cp in local + sends:
            cp.start()

    def wait():
        for cp in recvs:
            cp.wait_recv()
        for cp in sends:
            cp.wait_send()
        for cp in local:
            cp.wait()

    return start, wait


def _exchange_sems(n_arr, n_peer=N_DEV - 1):
    return [pltpu.SemaphoreType.DMA((n_arr, n_peer)), pltpu.SemaphoreType.DMA((n_arr, n_peer)),
            pltpu.SemaphoreType.DMA((n_arr,))]


def _exchange_shape(a, gather):
    return jax.ShapeDtypeStruct(((N_DEV,) + a.shape) if gather else a.shape, a.dtype)


def _gather_two_level(arrays, name):
    n_arr = len(arrays)

    def body(*refs):
        ins, outs = refs[:n_arr], refs[n_arr:2 * n_arr]
        send_sems, recv_sems, local_sems = refs[2 * n_arr:]
        x, y, c = _coords()

        def slot(k, px, py, pc):
            return outs[k].at[4 * px + 2 * py + pc]

        def routed(core):
            me, sibling = (x, y, core), (x, y, 1 - core)
            xn, yn, dg = (1 - x, y), (x, 1 - y), (1 - x, 1 - y)
            (first, s_first), (second, s_second) = ((xn, 1), (yn, 2)) if core == 0 else ((yn, 2), (xn, 1))

            def copy(k, j, block, to, own=False):
                return pltpu.make_async_remote_copy(
                    src_ref=ins[k] if own else slot(k, *block), dst_ref=slot(k, *block),
                    send_sem=send_sems.at[k, j], recv_sem=recv_sems.at[k, j],
                    device_id=to, device_id_type=MESH)

            local = [pltpu.make_async_copy(ins[k], slot(k, *me), local_sems.at[k]) for k in range(n_arr)]
            sent = []
            for k in range(n_arr):
                sent += [copy(k, 0, me, sibling, True), copy(k, 1, me, (*xn, core), True),
                         copy(k, 2, me, (*yn, core), True)]
            for cp in local + sent:
                cp.start()

            def pass_on(k, j_from, j_to, block, targets):
                copy(k, j_from, block, me).wait_recv()
                for j, to in zip(j_to, targets):
                    cp = copy(k, j, block, to)
                    cp.start()
                    sent.append(cp)

            for k in range(n_arr):
                pass_on(k, s_first, (3, 3 + s_first), (*first, core), ((*second, core), sibling))
            for k in range(n_arr):
                pass_on(k, s_second, (3 + s_second,), (*second, core), (sibling,))
            for k in range(n_arr):
                pass_on(k, 3, (6,), (*dg, core), (sibling,))
            for k in range(n_arr):
                copy(k, 0, sibling, me).wait_recv()
                for j, chip in ((4, xn), (5, yn), (6, dg)):
                    copy(k, j, (*chip, 1 - core), me).wait_recv()
            for cp in sent:
                cp.wait_send()
            for cp in local:
                cp.wait()

        for core in (0, 1):
            pl.when(c == core)(lambda core=core: routed(core))

    return pl.pallas_call(
        body, name=name,
        in_specs=[ANY] * n_arr, out_specs=[ANY] * n_arr,
        out_shape=[_exchange_shape(a, True) for a in arrays],
        scratch_shapes=_exchange_sems(n_arr),
    )(*arrays)


def _sibling_swap(blocks):
    def body(b_ref, o_ref, send_sems, recv_sems):
        x, y, c = _coords()
        copies = [pltpu.make_async_remote_copy(
            src_ref=b_ref.at[2 * q + (1 - c)], dst_ref=o_ref.at[q],
            send_sem=send_sems.at[q], recv_sem=recv_sems.at[q],
            device_id=(x, y, 1 - c), device_id_type=MESH) for q in range(N_CHIP)]
        for cp in copies:
            cp.start()
        for cp in copies:
            cp.wait_recv()
        for cp in copies:
            cp.wait_send()

    return pl.pallas_call(
        body, name="sibling_swap", in_specs=[ANY], out_specs=ANY,
        out_shape=jax.ShapeDtypeStruct((N_CHIP,) + blocks.shape[1:], blocks.dtype),
        scratch_shapes=[pltpu.SemaphoreType.DMA((N_CHIP,)), pltpu.SemaphoreType.DMA((N_CHIP,))],
    )(blocks)


def _pair_sum(blocks, theirs, core):
    _, r, c = blocks.shape

    def body(core_ref, a_ref, b_ref, o_ref):
        o_ref[...] = (a_ref[...].astype(F32) + b_ref[...].astype(F32)).astype(BF16)

    return pl.pallas_call(
        body, name="pair_sum",
        grid_spec=pltpu.PrefetchScalarGridSpec(
            num_scalar_prefetch=1, grid=(N_CHIP,),
            in_specs=[pl.BlockSpec((1, r, c), lambda q, core_ref: (2 * q + core_ref[0], 0, 0)),
                      pl.BlockSpec((1, r, c), lambda q, core_ref: (q, 0, 0))],
            out_specs=pl.BlockSpec((1, r, c), lambda q, core_ref: (q, 0, 0))),
        out_shape=jax.ShapeDtypeStruct((N_CHIP, r, c), BF16),
        compiler_params=_params(dimension_semantics=("arbitrary",)),
    )(core, blocks, theirs)


def _du_rms(dz, wt_pad, h_pad, dout, g, pair_blocks):
    tm = 272
    steps = T // tm

    def body(dz_ref, w_ref, h_ref, d_ref, g_ref, p_ref, dh_ref, dg_ref, got_ref,
             send_sems, recv_sems, local_sems):
        start, wait = _direct_exchange([p_ref], [got_ref], send_sems, recv_sems, local_sems,
                                       False, chips_only=True)
        i = pl.program_id(0)
        pl.when(i == 0)(start)

        @pl.when(i == 0)
        def _():
            dg_ref[...] = jnp.zeros_like(dg_ref)

        du = _dot(dz_ref[...], w_ref[...])
        h = h_ref[...]
        r = lax.rsqrt(jnp.mean(h * h, axis=-1, keepdims=True) + EPS)
        xn = h * r
        dg_ref[...] += jnp.sum(du * xn, axis=0, keepdims=True)
        dn = du * g_ref[...]
        dh_ref[...] = d_ref[...] + r * (dn - xn * jnp.mean(dn * xn, axis=-1, keepdims=True))

        pl.when(i == steps - 1)(wait)

    tile = pl.BlockSpec((tm, D_MODEL), lambda i: (i, 0))
    return pl.pallas_call(
        body, name="du_rms", grid=(steps,),
        in_specs=[pl.BlockSpec((tm, D_IN_PAD), lambda i: (i, 0)),
                  pl.BlockSpec((D_IN_PAD, D_MODEL), lambda i: (0, 0)),
                  tile, tile, pl.BlockSpec((1, D_MODEL), lambda i: (0, 0)), ANY],
        out_specs=[tile, pl.BlockSpec((1, D_MODEL), lambda i: (0, 0)), ANY],
        out_shape=[jax.ShapeDtypeStruct((T, D_MODEL), F32),
                   jax.ShapeDtypeStruct((1, D_MODEL), F32),
                   jax.ShapeDtypeStruct(pair_blocks.shape, pair_blocks.dtype)],
        scratch_shapes=_exchange_sems(1, N_CHIP - 1),
        compiler_params=_params(dimension_semantics=("arbitrary",)),
    )(dz, wt_pad, h_pad, dout, g, pair_blocks)


def _tri(lower):
    r = lax.broadcasted_iota(jnp.int32, (CHUNK, CHUNK), 0)
    c = lax.broadcasted_iota(jnp.int32, (CHUNK, CHUNK), 1)
    return jnp.where((r >= c) if lower else (r <= c), 1.0, 0.0).astype(F32)


def _row_valid(n):
    r = lax.broadcasted_iota(jnp.int32, (CHUNK, 128), 0) + n * CHUNK
    return r >= PAD


_FF_SPEC = pl.BlockSpec((T, 128), lambda i: (0, FF_BASE // 128))


def _forget_fwd(z, b_pad):
    def body(z_ref, b_ref, o_ref):
        tri = _tri(True)
        carry = jnp.zeros((1, 128), F32)
        for n in range(NCHUNK):
            rows = pl.ds(n * CHUNK, CHUNK)
            a = z_ref[rows, :] + b_ref[...]
            lf = -(jnp.maximum(-a, 0.0) + jnp.log(1.0 + jnp.exp(-jnp.abs(a))))
            lf = jnp.where(_row_valid(n), lf, 0.0)
            c = jnp.dot(tri, lf, precision=lax.Precision.HIGHEST,
                        preferred_element_type=F32) + carry
            carry = c[CHUNK - 1:CHUNK, :]
            o_ref[:, rows] = jnp.where(_row_valid(n), c * (-LOG2E), NEG_INF).T

    return pl.pallas_call(
        body, name="forget_fwd", grid=(1,),
        in_specs=[_FF_SPEC, pl.BlockSpec((1, 128), lambda i: (0, 0))],
        out_specs=pl.BlockSpec((128, T), lambda i: (0, 0)),
        out_shape=jax.ShapeDtypeStruct((128, T), F32),
        compiler_params=_params(dimension_semantics=("arbitrary",)),
    )(z, b_pad)


def _forget_bwd(z, b_pad, dc, dz):
    def body(z_ref, b_ref, dc_ref, dz_in, dff_ref, db_ref):
        tri = _tri(False)
        carry = jnp.zeros((1, 128), F32)
        db = jnp.zeros((1, 128), F32)
        for n in reversed(range(NCHUNK)):
            rows = pl.ds(n * CHUNK, CHUNK)
            dc_blk = jnp.concatenate([dc_ref[:, rows], jnp.zeros((128 - FOX_HEADS, CHUNK), F32)], axis=0).T
            dlf = jnp.dot(tri, dc_blk, precision=lax.Precision.HIGHEST,
                          preferred_element_type=F32) + carry
            carry = dlf[0:1, :]
            a = z_ref[rows, :] + b_ref[...]
            dff = jnp.where(_row_valid(n), dlf * jax.nn.sigmoid(-a), 0.0)
            dff_ref[rows, :] = dff.astype(BF16)
            db = db + jnp.sum(dff, axis=0, keepdims=True)
        db_ref[...] = db

    return pl.pallas_call(
        body, name="forget_bwd", grid=(1,),
        in_specs=[_FF_SPEC, pl.BlockSpec((1, 128), lambda i: (0, 0)),
                  pl.BlockSpec((FOX_HEADS, T), lambda i: (0, 0)), ANY],
        out_specs=[_FF_SPEC, pl.BlockSpec((1, 128), lambda i: (0, 0))],
        out_shape=[jax.ShapeDtypeStruct((T, D_IN_PAD), BF16),
                   jax.ShapeDtypeStruct((1, 128), F32)],
        input_output_aliases={3: 0},
        compiler_params=_params(dimension_semantics=("arbitrary",)),
    )(z, b_pad, dc, dz)


FOX_QB = 512
FOX_NQB = SEQ // FOX_QB


def _fox_block(b):
    lo = CHUNK + b * FOX_QB
    return pl.ds(lo, FOX_QB), lo, lo + FOX_QB


def _causal_bias():
    r = lax.broadcasted_iota(jnp.int32, (FOX_QB, FOX_QB), 0)
    c = lax.broadcasted_iota(jnp.int32, (FOX_QB, FOX_QB), 1)
    return jnp.where(c <= r, 0.0, NEG_INF).astype(F32)


def _fox_logits(q_blk, k_all, bias, causal, b):
    _, lo, hi = _fox_block(b)
    s_off = _dot_nt(q_blk, k_all[:lo]) + bias[:, :lo]
    s_dia = _dot_nt(q_blk, k_all[lo:hi]) + (bias[:, lo:hi] + causal)
    return s_off, s_dia


_FOX_Z_SPEC = pl.BlockSpec((T, FOX_W), lambda p: (0, FOX_BASE // FOX_W + p))
_FOX_BIAS_SPEC = pl.BlockSpec((2, 1, T), lambda p: (p, 0, 0))
_FOX_LSE_SPEC = pl.BlockSpec((2, T, 1), lambda p: (p, 0, 0))
_FOX_SCALE = FOX_D ** -0.5
_FOX_QSCALE = _FOX_SCALE * LOG2E


def _fox_fwd(z, bias, y, w_out_blk):
    last = FOX_PAIRS - 1

    def body(z_ref, b_ref, y_in, w_ref, a_ref, lse_ref, y_ref, wall_ref,
             send_sems, recv_sems, local_sems):
        start, wait = _direct_exchange([w_ref], [wall_ref], send_sems, recv_sems, local_sems, True)
        pl.when(pl.program_id(0) == 0)(start)

        causal = _causal_bias()
        a_ref[pl.ds(0, CHUNK), :] = jnp.zeros((CHUNK, 128), F32)
        y_ref[pl.ds(0, CHUNK), :] = jnp.zeros((CHUNK, 128), BF16)
        for j in range(2):
            lanes = pl.ds(j * FOX_D, FOX_D)
            k_all = z_ref[:, pl.ds(128 + j * FOX_D, FOX_D)].astype(BF16)
            v_all = z_ref[:, pl.ds(256 + j * FOX_D, FOX_D)].astype(BF16)
            bias = b_ref[j]
            lse_ref[j, pl.ds(0, CHUNK), :] = jnp.zeros((CHUNK, 1), F32)
            for b in range(FOX_NQB):
                rows, lo, hi = _fox_block(b)
                q_blk = (z_ref[rows, lanes] * _FOX_QSCALE).astype(BF16)
                s_off, s_dia = _fox_logits(q_blk, k_all, bias, causal, b)
                m = jnp.maximum(jnp.max(s_off, axis=-1, keepdims=True),
                                jnp.max(s_dia, axis=-1, keepdims=True))
                e_off = jnp.exp2(s_off - m)
                e_dia = jnp.exp2(s_dia - m)
                total = jnp.sum(e_off, axis=-1, keepdims=True) + jnp.sum(e_dia, axis=-1, keepdims=True)
                o = (_dot(e_off.astype(BF16), v_all[:lo]) + _dot(e_dia.astype(BF16), v_all[lo:hi])) / total
                a_ref[rows, lanes] = o
                lse_ref[j, rows, :] = m + jnp.log(total) * LOG2E
                gate = _silu_parts(z_ref[rows, pl.ds(384 + j * FOX_D, FOX_D)])[0]
                y_ref[rows, lanes] = (o * gate).astype(BF16)

        pl.when(pl.program_id(0) == last)(wait)

    return pl.pallas_call(
        body, name="fox_fwd", grid=(FOX_PAIRS,),
        in_specs=[_FOX_Z_SPEC, _FOX_BIAS_SPEC, ANY, ANY],
        out_specs=[pl.BlockSpec((T, 128), lambda p: (0, p)), _FOX_LSE_SPEC,
                   pl.BlockSpec((T, 128), lambda p: (0, 8 + p)), ANY],
        out_shape=[jax.ShapeDtypeStruct((T, FOX_HEADS * FOX_D), F32),
                   jax.ShapeDtypeStruct((FOX_HEADS, T, 1), F32),
                   jax.ShapeDtypeStruct((T, D_MIX), BF16),
                   _exchange_shape(w_out_blk, True)],
        input_output_aliases={2: 2},
        scratch_shapes=_exchange_sems(1),
        compiler_params=_params(dimension_semantics=("arbitrary",)),
    )(z, bias, y, w_out_blk)


def _fox_bwd(z, bias, a_f, lse, dy, dz, dwo_blocks):
    last = FOX_PAIRS - 1

    def body(z_ref, b_ref, a_ref, lse_ref, dy_ref, dz_in, dwo_ref, dz_ref, dc_ref, got_ref,
             kv_acc, dc_acc, send_sems, recv_sems, local_sems):
        start, wait = _direct_exchange([dwo_ref], [got_ref], send_sems, recv_sems, local_sems, False)
        pl.when(pl.program_id(0) == 0)(start)

        causal = _causal_bias()
        dz_ref[pl.ds(0, CHUNK), pl.ds(0, 128)] = jnp.zeros((CHUNK, 128), BF16)
        dz_ref[pl.ds(0, CHUNK), pl.ds(384, 128)] = jnp.zeros((CHUNK, 128), BF16)
        dk_rows, dv_rows = pl.ds(0, FOX_D), pl.ds(FOX_D, FOX_D)
        for j in range(2):
            lanes = pl.ds(j * FOX_D, FOX_D)
            k_all = z_ref[:, pl.ds(128 + j * FOX_D, FOX_D)].astype(BF16)
            v_all = z_ref[:, pl.ds(256 + j * FOX_D, FOX_D)].astype(BF16)
            bias = b_ref[j]
            kv_acc[...] = jnp.zeros_like(kv_acc)
            dc_acc[...] = jnp.zeros_like(dc_acc)
            for b in range(FOX_NQB):
                rows, lo, hi = _fox_block(b)
                off, dia = pl.ds(0, lo), pl.ds(lo, FOX_QB)
                q_blk = (z_ref[rows, lanes] * _FOX_QSCALE).astype(BF16)
                s_off, s_dia = _fox_logits(q_blk, k_all, bias, causal, b)
                lse_blk = lse_ref[j, rows, :]
                p_off, p_dia = jnp.exp2(s_off - lse_blk), jnp.exp2(s_dia - lse_blk)
                sg, dsg = _silu_parts(z_ref[rows, pl.ds(384 + j * FOX_D, FOX_D)])
                dyj = dy_ref[rows, lanes]
                dz_ref[rows, pl.ds(384 + j * FOX_D, FOX_D)] = (dyj * a_ref[rows, lanes] * dsg).astype(BF16)
                do_b = (dyj * sg).astype(BF16)
                dp_off = _dot_nt(do_b, v_all[:lo])
                dp_dia = _dot_nt(do_b, v_all[lo:hi])
                d = (jnp.sum(p_off * dp_off, axis=-1, keepdims=True)
                     + jnp.sum(p_dia * dp_dia, axis=-1, keepdims=True))
                ds_off = p_off * (dp_off - d)
                ds_dia = p_dia * (dp_dia - d)
                dc_acc[:, off] -= jnp.sum(ds_off, axis=0, keepdims=True)
                dc_acc[:, dia] -= jnp.sum(ds_dia, axis=0, keepdims=True)
                ds_off_b, ds_dia_b = ds_off.astype(BF16), ds_dia.astype(BF16)
                dq = _dot(ds_off_b, k_all[:lo]) + _dot(ds_dia_b, k_all[lo:hi])
                dz_ref[rows, lanes] = (dq * _FOX_SCALE).astype(BF16)
                kv_acc[dk_rows, off] += _dot_tn(q_blk, ds_off_b)
                kv_acc[dk_rows, dia] += _dot_tn(q_blk, ds_dia_b)
                kv_acc[dv_rows, off] += _dot_tn(do_b, p_off.astype(BF16))
                kv_acc[dv_rows, dia] += _dot_tn(do_b, p_dia.astype(BF16))
            for n in range(NCHUNK):
                rows = pl.ds(n * CHUNK, CHUNK)
                both = kv_acc[:, rows].T
                dz_ref[rows, pl.ds(128 + j * FOX_D, FOX_D)] = (both[:, :FOX_D] * LN2).astype(BF16)
                dz_ref[rows, pl.ds(256 + j * FOX_D, FOX_D)] = both[:, FOX_D:].astype(BF16)
            dc_ref[j] = dc_acc[...]

        pl.when(pl.program_id(0) == last)(wait)

    col = lambda base: pl.BlockSpec((T, 128), lambda p: (0, base + p))
    return pl.pallas_call(
        body, name="fox_bwd", grid=(FOX_PAIRS,),
        in_specs=[_FOX_Z_SPEC, _FOX_BIAS_SPEC, col(0), _FOX_LSE_SPEC, col(8), ANY, ANY],
        out_specs=[_FOX_Z_SPEC, _FOX_BIAS_SPEC, ANY],
        out_shape=[jax.ShapeDtypeStruct((T, D_IN_PAD), BF16),
                   jax.ShapeDtypeStruct((FOX_HEADS, 1, T), F32),
                   _exchange_shape(dwo_blocks, False)],
        input_output_aliases={5: 0},
        scratch_shapes=[pltpu.VMEM((2 * FOX_D, T), F32), pltpu.VMEM((1, T), F32)] + _exchange_sems(1),
        compiler_params=_params(dimension_semantics=("arbitrary",)),
    )(z, bias, a_f, lse, dy, dz, dwo_blocks)


def _rot(x, cosf, sins):
    return x * cosf + pltpu.roll(x, RET_DK // 2, 1) * sins


def _rot_t(d, cosf, sins):
    return d * cosf - pltpu.roll(d, RET_DK // 2, 1) * sins


_RET_Z_SPEC = pl.BlockSpec((T, RET_W), lambda h: (0, h))
_RET_TABLE_SPECS = [
    pl.BlockSpec((T, RET_DK), lambda h: (0, 0)),
    pl.BlockSpec((T, RET_DK), lambda h: (0, 0)),
    pl.BlockSpec((1, CHUNK, CHUNK), lambda h: (h, 0, 0)),
    pl.BlockSpec((1, CHUNK, 1), lambda h: (h, 0, 0)),
    pl.BlockSpec((1, CHUNK, 1), lambda h: (h, 0, 0)),
    pl.BlockSpec((1, 1, 1), lambda h: (h, 0, 0)),
]
_RQ, _RK = pl.ds(0, RET_DK), pl.ds(RET_DK, RET_DK)
_RV, _RG = pl.ds(2 * RET_DK, RET_DV), pl.ds(2 * RET_DK + RET_DV, RET_DV)
_RET_KSCALE = RET_DK ** -0.5


def _ret_fwd(z, tables):
    def body(z_ref, cos_ref, sin_ref, dm_ref, zeta_ref, xi_ref, cd_ref, raw_ref, y_ref):
        dmask, zeta, xi, cdec = dm_ref[0], zeta_ref[0], xi_ref[0], cd_ref[0]
        state = jnp.zeros((RET_DK, RET_DV), F32)
        for n in range(NCHUNK):
            rows = pl.ds(n * CHUNK, CHUNK)
            cosf, sins = cos_ref[rows, :], sin_ref[rows, :]
            qr = _rot(z_ref[rows, _RQ], cosf, sins)
            kr_b = (_rot(z_ref[rows, _RK], cosf, sins) * _RET_KSCALE).astype(BF16)
            v = z_ref[rows, _RV]
            a = _dot_nt(qr.astype(BF16), kr_b) * dmask
            out = _dot(a.astype(BF16), v.astype(BF16)) + _dot((qr * xi).astype(BF16), state.astype(BF16))
            state = state * cdec + _dot_tn(kr_b, (v * zeta).astype(BF16))
            raw_ref[rows, :] = out
            r = lax.rsqrt(jnp.mean(out * out, axis=-1, keepdims=True) + EPS)
            y_ref[rows, :] = (out * r * _silu_parts(z_ref[rows, _RG])[0]).astype(BF16)

    wide = pl.BlockSpec((T, RET_DV), lambda h: (0, h))
    return pl.pallas_call(
        body, name="ret_fwd", grid=(RET_HEADS,),
        in_specs=[_RET_Z_SPEC] + _RET_TABLE_SPECS,
        out_specs=[wide, wide],
        out_shape=[jax.ShapeDtypeStruct((T, RET_HEADS * RET_DV), F32),
                   jax.ShapeDtypeStruct((T, D_MIX), BF16)],
        compiler_params=_params(dimension_semantics=("arbitrary",)),
    )(z, *tables)


def _ret_bwd(z, tables, raw, dy):
    def body(z_ref, cos_ref, sin_ref, dm_ref, zeta_ref, xi_ref, cd_ref, raw_ref, dy_ref,
             dz_ref, st_ref):
        dmask, zeta, xi, cdec = dm_ref[0], zeta_ref[0], xi_ref[0], cd_ref[0]

        def rotated(n):
            rows = pl.ds(n * CHUNK, CHUNK)
            cosf, sins = cos_ref[rows, :], sin_ref[rows, :]
            qr = _rot(z_ref[rows, _RQ], cosf, sins)
            kr_b = (_rot(z_ref[rows, _RK], cosf, sins) * _RET_KSCALE).astype(BF16)
            return rows, cosf, sins, qr, kr_b

        state = jnp.zeros((RET_DK, RET_DV), F32)
        for n in range(NCHUNK):
            st_ref[n] = state.astype(BF16)
            if n + 1 < NCHUNK:
                rows, _, _, _, kr_b = rotated(n)
                state = state * cdec + _dot_tn(kr_b, (z_ref[rows, _RV] * zeta).astype(BF16))

        grad_state = jnp.zeros((RET_DK, RET_DV), F32)
        for n in reversed(range(NCHUNK)):
            rows, cosf, sins, qr, kr_b = rotated(n)
            qr_b = qr.astype(BF16)
            v_b = z_ref[rows, _RV].astype(BF16)
            gs_b = grad_state.astype(BF16)
            o = raw_ref[rows, :]
            r = lax.rsqrt(jnp.mean(o * o, axis=-1, keepdims=True) + EPS)
            hn = o * r
            sg, dsg = _silu_parts(z_ref[rows, _RG])
            dyn = dy_ref[rows, :]
            dz_ref[rows, _RG] = (dyn * hn * dsg).astype(BF16)
            dhn = dyn * sg
            do_b = (r * (dhn - hn * jnp.mean(dhn * hn, axis=-1, keepdims=True))).astype(BF16)
            a_b = (_dot_nt(qr_b, kr_b) * dmask).astype(BF16)
            da_b = (_dot_nt(do_b, v_b) * dmask).astype(BF16)
            dqr = _dot(da_b, kr_b) + xi * _dot_nt(do_b, st_ref[n])
            dkr = _dot_tn(da_b, qr_b) + zeta * _dot_nt(v_b, gs_b)
            dv = _dot_tn(a_b, do_b) + zeta * _dot(kr_b, gs_b)
            grad_state = grad_state * cdec + _dot_tn((qr * xi).astype(BF16), do_b)
            dz_ref[rows, _RQ] = _rot_t(dqr, cosf, sins).astype(BF16)
            dz_ref[rows, _RK] = (_rot_t(dkr, cosf, sins) * _RET_KSCALE).astype(BF16)
            dz_ref[rows, _RV] = dv.astype(BF16)

    wide = pl.BlockSpec((T, RET_DV), lambda h: (0, h))
    return pl.pallas_call(
        body, name="ret_bwd", grid=(RET_HEADS,),
        in_specs=[_RET_Z_SPEC] + _RET_TABLE_SPECS + [wide, wide],
        out_specs=_RET_Z_SPEC,
        out_shape=jax.ShapeDtypeStruct((T, D_IN_PAD), BF16),
        scratch_shapes=[pltpu.VMEM((NCHUNK, RET_DK, RET_DV), BF16)],
        compiler_params=_params(dimension_semantics=("arbitrary",)),
    )(z, *tables, raw, dy)


def _adamw(w, g, m, v):
    m = ADAM_B1 * m + (1.0 - ADAM_B1) * g
    v = ADAM_B2 * v + (1.0 - ADAM_B2) * (g * g)
    m_hat = m / (1.0 - ADAM_B1 ** ADAM_STEP)
    v_hat = v / (1.0 - ADAM_B2 ** ADAM_STEP)
    delta = -ADAM_LR * (m_hat / (jnp.sqrt(v_hat) + ADAM_EPS) + ADAM_WD * w)
    return delta, m, v


def _sum_adamw(parts, w, m, v, rows, name):
    _, r_tot, cols = parts.shape
    assert r_tot % rows == 0

    def body(p_ref, w_ref, m_ref, v_ref, g_ref, d_ref, nm_ref, nv_ref):
        g = p_ref[0].astype(F32)
        for d in range(1, N_DEV):
            g = g + p_ref[d].astype(F32)
        delta, nm, nv = _adamw(w_ref[...], g, m_ref[...], v_ref[...])
        g_ref[...] = g
        d_ref[...] = delta
        nm_ref[...] = nm
        nv_ref[...] = nv

    blk = pl.BlockSpec((rows, cols), lambda i: (i, 0))
    return pl.pallas_call(
        body, name=name, grid=(r_tot // rows,),
        in_specs=[pl.BlockSpec((N_DEV, rows, cols), lambda i: (0, i, 0)), blk, blk, blk],
        out_specs=[blk] * 4,
        out_shape=[jax.ShapeDtypeStruct((r_tot, cols), F32)] * 4,
        compiler_params=_params(dimension_semantics=("arbitrary",)),
    )(parts, w, m, v)


def _sum_adamw_w_in(parts, w, m, v, small):
    n_part, r, c = parts.shape
    steps = c // 128

    def body(p_ref, w_ref, m_ref, v_ref, s_ref, g_ref, d_ref, nm_ref, nv_ref, got_ref,
             send_sems, recv_sems, local_sems):
        start, wait = _direct_exchange([s_ref], [got_ref], send_sems, recv_sems, local_sems, True)
        pl.when(pl.program_id(0) == 0)(start)
        g = p_ref[0].astype(F32)
        for d in range(1, n_part):
            g = g + p_ref[d].astype(F32)
        delta, nm, nv = _adamw(w_ref[...], g, m_ref[...], v_ref[...])
        g_ref[...] = g
        d_ref[...] = delta
        nm_ref[...] = nm
        nv_ref[...] = nv
        pl.when(pl.program_id(0) == steps - 1)(wait)

    blk = pl.BlockSpec((r, 128), lambda i: (0, i))
    return pl.pallas_call(
        body, name="adamw_w_in", grid=(steps,),
        in_specs=[pl.BlockSpec((n_part, r, 128), lambda i: (0, 0, i)), blk, blk, blk, ANY],
        out_specs=[blk] * 4 + [ANY],
        out_shape=[jax.ShapeDtypeStruct((r, c), F32)] * 4 + [_exchange_shape(small, True)],
        scratch_shapes=_exchange_sems(1),
        compiler_params=_params(dimension_semantics=("arbitrary",)),
    )(parts, w, m, v, small)


def kernel(x, meta_tokens, norm_g, w_in, b_f, w_out, final_g, loss_target, m_meta_tokens, m_norm_g, m_w_in, m_b_f, m_w_out, m_final_g, v_meta_tokens, v_norm_g, v_w_in, v_b_f, v_w_out, v_final_g):
    core = lax.axis_index("c")
    me = 4 * lax.axis_index("x") + 2 * lax.axis_index("y") + core
    tables = _tables()

    wt_all, meta_all = _gather_two_level([w_in[0].T.astype(BF16), meta_tokens], name="gather_w_in")
    wt_pad = _regroup_rows(wt_all)
    meta_full = jnp.transpose(meta_all, (1, 0, 2)).reshape(N_META, D_MODEL)
    h_pad = jnp.concatenate([jnp.zeros((PAD, D_MODEL), F32), meta_full, x[0]], axis=0)
    b_pad = jnp.pad(b_f, ((0, 0), (0, 128 - FOX_HEADS)))

    u = _rms_fwd(h_pad, norm_g)
    z = _mm(u, wt_pad, nt=True, tm=T // 2, tn=D_IN_PAD // 3, out_dtype=F32, name="mm_z")
    bias = _forget_fwd(z, b_pad)[:FOX_HEADS].reshape(FOX_HEADS, 1, T)
    raw, y = _ret_fwd(z, tables)
    a_f, lse, y, w_out_all = _fox_fwd(z, bias, y, w_out[0].astype(BF16))
    w_out_b = w_out_all.reshape(D_MIX, D_MODEL)
    target_pad = jnp.pad(loss_target[0], ((CHUNK, 0), (0, 0)))
    dout, dout_b, dy, loss_blk, d_final_g = _out_loss_dy(y, w_out_b, h_pad, target_pad,
                                                         final_g.reshape(1, D_MODEL))

    d_w_out = _mm(y, dout_b, ta=True, tm=D_MIX, tn=256, out_dtype=BF16, name="mm_dwout")
    dz = _ret_bwd(z, tables, raw, dy)
    dz, dc, got_w_out = _fox_bwd(z, bias, a_f, lse, dy, dz, d_w_out.reshape(N_DEV, WO_BLK, D_MODEL))
    dz, db_f = _forget_bwd(z, b_pad, dc.reshape(FOX_HEADS, T), dz)
    d_wt_pad = _mm(dz, u, ta=True, tm=384, tn=D_MODEL, out_dtype=BF16, name="mm_dwin")

    d_wt_blocks = _ungroup_rows(d_wt_pad)
    theirs = _sibling_swap(d_wt_blocks)
    pair = _pair_sum(d_wt_blocks, theirs, core.astype(jnp.int32).reshape(1))
    dh, d_norm_g, got_w_in = _du_rms(dz, wt_pad, h_pad, dout, norm_g, pair)

    small = jnp.concatenate([
        dh[PAD:CHUNK], d_norm_g, d_final_g, jnp.pad(db_f[:, :FOX_HEADS], ((0, 0), (0, D_MODEL - FOX_HEADS))),
        jnp.pad(loss_blk[0:1], ((0, 0), (0, D_MODEL - 128))),
        jnp.zeros((SMALL_ROWS - N_META - 4, D_MODEL), F32)], axis=0)
    g_w_in, d_w_in, nm_w_in, nv_w_in, got_small = _sum_adamw_w_in(
        got_w_in, w_in[0].T, m_w_in[0].T, v_w_in[0].T, small)
    g_w_out, d_w_out, nm_w_out, nv_w_out = _sum_adamw(got_w_out, w_out[0], m_w_out[0], v_w_out[0], 128, "adamw_w_out")

    row = lambda a: a.reshape(1, -1)
    wide = lambda a: jnp.pad(row(a), ((0, 0), (0, D_MODEL - a.size)))

    def small_pack(meta, ng, fg, bf):
        return jnp.concatenate([meta, row(ng), row(fg), wide(bf),
                                jnp.zeros((SMALL_ROWS - N_META - 3, D_MODEL), F32)], axis=0)

    meta_at = lambda a: lax.dynamic_update_slice(jnp.zeros((N_META, D_MODEL), F32), a, (0, me * META_BLK))
    g_s, d_s, nm_s, nv_s = _sum_adamw(
        got_small,
        small_pack(meta_at(meta_tokens), norm_g, final_g, b_f),
        small_pack(meta_at(m_meta_tokens), m_norm_g, m_final_g, m_b_f),
        small_pack(meta_at(v_meta_tokens), v_norm_g, v_final_g, v_b_f),
        SMALL_ROWS, "adamw_small")

    def unpack(s):
        meta = lax.dynamic_slice(s[:N_META], (0, me * META_BLK), (N_META, META_BLK))
        return meta, s[N_META:N_META + 1], s[N_META + 1], s[N_META + 2:N_META + 3, :FOX_HEADS]

    g_meta, g_ng, g_fg, g_bf = unpack(g_s)
    d_meta, d_ng, d_fg, d_bf = unpack(d_s)
    nm_meta, nm_ng, nm_fg, nm_bf = unpack(nm_s)
    nv_meta, nv_ng, nv_fg, nv_bf = unpack(nv_s)

    back = lambda a: a.T[None]
    loss = g_s[N_META + 3, 0]
    return (loss, dh[CHUNK:][None],
            g_meta, g_ng, back(g_w_in), g_bf, g_w_out[None], g_fg,
            d_meta, d_ng, back(d_w_in), d_bf, d_w_out[None], d_fg,
            nm_meta, nm_ng, back(nm_w_in), nm_bf, nm_w_out[None], nm_fg,
            nv_meta, nv_ng, back(nv_w_in), nv_bf, nv_w_out[None], nv_fg)
```

```python
import numpy as np
import jax
import jax.numpy as jnp
from jax import lax
from jax.experimental import pallas as pl
from jax.experimental.pallas import tpu as pltpu

F32 = jnp.float32
BF16 = jnp.bfloat16

N_DEV = 8
N_CHIP = 4
D_MODEL = 1024
SEQ = 2048
N_META = 16
CHUNK = 128
PAD = CHUNK - N_META
T = SEQ + CHUNK
NCHUNK = T // CHUNK
D_MIX = 2048
RET_HEADS = 4
RET_DK = 128
RET_DV = 256
RET_W = 2 * RET_DK + 2 * RET_DV
FOX_HEADS = 16
FOX_D = 64
FOX_PAIRS = FOX_HEADS // 2
FOX_W = 4 * 128
FOX_BASE = RET_HEADS * RET_W
FF_BASE = FOX_BASE + FOX_PAIRS * FOX_W
D_IN = 7184
D_IN_PAD = 7296
W_BLK = D_IN // N_DEV
WO_BLK = D_MIX // N_DEV
META_BLK = D_MODEL // N_DEV
EPS = 1e-6
NEG_INF = -1e30
ROPE_BASE = 10000.0
LOG2E = 1.4426950408889634
LN2 = 0.6931471805599453

ADAM_LR = 0.001
ADAM_B1 = 0.9
ADAM_B2 = 0.999
ADAM_EPS = 1e-08
ADAM_WD = 0.01
ADAM_STEP = 10

SMALL_ROWS = 24
VMEM_LIMIT = 56 * 1024 * 1024
MESH = pl.DeviceIdType.MESH
ANY = pl.BlockSpec(memory_space=pl.ANY)

_NT = (((1,), (1,)), ((), ()))
_TN = (((0,), (0,)), ((), ()))


def _dot(a, b):
    return jnp.dot(a, b, preferred_element_type=F32)


def _dot_nt(a, b):
    return lax.dot_general(a, b, _NT, preferred_element_type=F32)


def _dot_tn(a, b):
    return lax.dot_general(a, b, _TN, preferred_element_type=F32)


def _params(**kw):
    return pltpu.CompilerParams(vmem_limit_bytes=VMEM_LIMIT, **kw)


def _silu_parts(g):
    sig = jax.nn.sigmoid(g)
    return g * sig, sig * (1.0 + g * (1.0 - sig))


def _tables():
    pos = np.arange(T, dtype=np.float32) - PAD
    inv = (ROPE_BASE ** (-np.arange(0, RET_DK, 2, dtype=np.float32) / RET_DK)).astype(np.float32)
    ang = pos[:, None] * inv[None, :]
    cos, sin = np.cos(ang), np.sin(ang)
    cosf = np.concatenate([cos, cos], axis=1).astype(np.float32)
    sins = np.concatenate([-sin, sin], axis=1).astype(np.float32)
    h = np.arange(RET_HEADS, dtype=np.float32)
    log_gamma = np.log1p(-np.exp2(-5.0 - h)).astype(np.float32)
    idx = np.arange(CHUNK, dtype=np.float32)
    diff = idx[:, None] - idx[None, :]
    dmask = np.where(diff[None] >= 0,
                     np.exp(log_gamma[:, None, None] * np.maximum(diff, 0.0)[None]), 0.0)
    zeta = np.exp(log_gamma[:, None] * (CHUNK - 1.0 - idx)[None, :])
    xi = np.exp(log_gamma[:, None] * (idx + 1.0)[None, :])
    cdec = np.exp(log_gamma * CHUNK)
    return (jnp.asarray(cosf), jnp.asarray(sins), jnp.asarray(dmask, F32),
            jnp.asarray(zeta[:, :, None], F32), jnp.asarray(xi[:, :, None], F32),
            jnp.asarray(cdec[:, None, None], F32))


def _regroup_rows(w):
    c = w.shape[1]
    ret = jnp.concatenate([w[0:512].reshape(4, 128, c), w[512:1024].reshape(4, 128, c),
                           w[1024:2048].reshape(4, 256, c), w[2048:3072].reshape(4, 256, c)],
                          axis=1).reshape(FOX_BASE, c)
    fox = jnp.concatenate([w[3072 + 1024 * i:4096 + 1024 * i].reshape(8, 128, c) for i in range(4)],
                          axis=1).reshape(FOX_PAIRS * FOX_W, c)
    ff = jnp.pad(w[FF_BASE:D_IN], ((0, D_IN_PAD - D_IN), (0, 0)))
    return jnp.concatenate([ret, fox, ff], axis=0)


def _ungroup_rows(w):
    c = w.shape[1]
    ret = w[:FOX_BASE].reshape(4, RET_W, c)
    fox = w[FOX_BASE:FF_BASE].reshape(8, FOX_W, c)
    return jnp.concatenate(
        [ret[:, 0:128].reshape(512, c), ret[:, 128:256].reshape(512, c),
         ret[:, 256:512].reshape(1024, c), ret[:, 512:768].reshape(1024, c)]
        + [fox[:, 128 * i:128 * i + 128].reshape(1024, c) for i in range(4)]
        + [w[FF_BASE:FF_BASE + FOX_HEADS]], axis=0)


def _mm(a, b, *, tm, tn, out_dtype, name, ta=False, nt=False):
    k, m = (a.shape if ta else a.shape[::-1])
    n = b.shape[0] if nt else b.shape[1]
    assert m % tm == 0 and n % tn == 0 and not (ta and nt)

    def body(a_ref, b_ref, o_ref):
        if ta:
            prod = _dot_tn(a_ref[...], b_ref[...])
        elif nt:
            prod = _dot_nt(a_ref[...], b_ref[...])
        else:
            prod = _dot(a_ref[...], b_ref[...])
        o_ref[...] = prod.astype(out_dtype)

    return pl.pallas_call(
        body, name=name, grid=(n // tn, m // tm),
        in_specs=[pl.BlockSpec((k, tm), lambda j, i: (0, i)) if ta
                  else pl.BlockSpec((tm, k), lambda j, i: (i, 0)),
                  pl.BlockSpec((tn, k), lambda j, i: (j, 0)) if nt
                  else pl.BlockSpec((k, tn), lambda j, i: (0, j))],
        out_specs=pl.BlockSpec((tm, tn), lambda j, i: (i, j)),
        out_shape=jax.ShapeDtypeStruct((m, n), out_dtype),
        compiler_params=_params(dimension_semantics=("arbitrary", "arbitrary")),
    )(a, b)


def _rms_fwd(h_pad, g):
    def body(h_ref, g_ref, u_ref):
        h = h_ref[...]
        r = lax.rsqrt(jnp.mean(h * h, axis=-1, keepdims=True) + EPS)
        u_ref[...] = (h * r * g_ref[...]).astype(BF16)

    return pl.pallas_call(
        body, name="rms_fwd", grid=(NCHUNK,),
        in_specs=[pl.BlockSpec((CHUNK, D_MODEL), lambda i: (i, 0)),
                  pl.BlockSpec((1, D_MODEL), lambda i: (0, 0))],
        out_specs=pl.BlockSpec((CHUNK, D_MODEL), lambda i: (i, 0)),
        out_shape=jax.ShapeDtypeStruct((T, D_MODEL), BF16),
        compiler_params=_params(dimension_semantics=("arbitrary",)),
    )(h_pad, g)


def _out_loss_dy(y, w_out_b, h_pad, target, g):
    tm = T // 4

    def body(y_ref, w_ref, h_ref, t_hbm, g_ref, d_ref, db_ref, dy_ref, loss_ref, dg_ref, t_buf, t_sem):
        i = pl.program_id(0)
        head = pltpu.make_async_copy(t_hbm.at[pl.ds(0, tm - CHUNK)], t_buf.at[pl.ds(CHUNK, tm - CHUNK)], t_sem)
        rest = pltpu.make_async_copy(t_hbm.at[pl.ds(pl.multiple_of(jnp.maximum(i, 1) * tm - CHUNK, 8), tm)],
                                     t_buf, t_sem)

        @pl.when(i == 0)
        def _():
            t_buf[pl.ds(0, CHUNK), :] = jnp.zeros((CHUNK, D_MODEL), F32)
            head.start()
            loss_ref[...] = jnp.zeros_like(loss_ref)
            dg_ref[...] = jnp.zeros_like(dg_ref)

        pl.when(i > 0)(rest.start)

        w = w_ref[...]
        o = _dot(y_ref[...], w) + h_ref[...]
        pl.when(i == 0)(head.wait)
        pl.when(i > 0)(rest.wait)
        token = lax.broadcasted_iota(jnp.int32, (tm, 1), 0) + i * tm >= CHUNK
        g = g_ref[...]
        r = lax.rsqrt(jnp.mean(o * o, axis=-1, keepdims=True) + EPS)
        xn = o * r
        e = jnp.where(token, xn * g - t_buf[...], 0.0)
        loss_ref[...] += jnp.full(loss_ref.shape, 0.5 / D_MODEL * jnp.sum(e * e), F32)
        do = e * (1.0 / D_MODEL)
        dg_ref[...] += jnp.sum(do * xn, axis=0, keepdims=True)
        dn = do * g
        d = r * (dn - xn * jnp.mean(dn * xn, axis=-1, keepdims=True))
        d_b = d.astype(BF16)
        d_ref[...] = d
        db_ref[...] = d_b
        dy_ref[...] = _dot_nt(d_b, w)

    tile = pl.BlockSpec((tm, D_MODEL), lambda i: (i, 0))
    wide = pl.BlockSpec((tm, D_MIX), lambda i: (i, 0))
    return pl.pallas_call(
        body, name="out_loss_dy", grid=(T // tm,),
        in_specs=[wide, pl.BlockSpec((D_MIX, D_MODEL), lambda i: (0, 0)), tile, ANY,
                  pl.BlockSpec((1, D_MODEL), lambda i: (0, 0))],
        out_specs=[tile, tile, wide,
                   pl.BlockSpec((8, 128), lambda i: (0, 0)),
                   pl.BlockSpec((1, D_MODEL), lambda i: (0, 0))],
        out_shape=[jax.ShapeDtypeStruct((T, D_MODEL), F32),
                   jax.ShapeDtypeStruct((T, D_MODEL), BF16),
                   jax.ShapeDtypeStruct((T, D_MIX), F32),
                   jax.ShapeDtypeStruct((8, 128), F32),
                   jax.ShapeDtypeStruct((1, D_MODEL), F32)],
        scratch_shapes=[pltpu.VMEM((tm, D_MODEL), F32), pltpu.SemaphoreType.DMA],
        compiler_params=_params(dimension_semantics=("arbitrary",)),
    )(y, w_out_b, h_pad, target, g)


def _coords():
    return lax.axis_index("x"), lax.axis_index("y"), lax.axis_index("c")


def _flip(v, bit):
    return 1 - v if bit else v


def _peer(x, y, c, r):
    return _flip(x, (r >> 2) & 1), _flip(y, (r >> 1) & 1), _flip(c, r & 1)


def _direct_exchange(ins, outs, send_sems, recv_sems, local_sems, gather, chips_only=False):
    x, y, c = _coords()
    me = 2 * x + y if chips_only else 4 * x + 2 * y + c

    def src(k, to_idx):
        return ins[k] if gather else ins[k].at[to_idx]

    local = [pltpu.make_async_copy(src(k, me), outs[k].at[me], local_sems.at[k])
             for k in range(len(ins))]
    sends, recvs = [], []
    for r in range(1, N_CHIP if chips_only else N_DEV):
        px, py, pc = _peer(x, y, c, 2 * r if chips_only else r)
        peer = 2 * px + py if chips_only else 4 * px + 2 * py + pc
        for k in range(len(ins)):
            sems = dict(send_sem=send_sems.at[k, r - 1], recv_sem=recv_sems.at[k, r - 1],
                        device_id=(px, py, pc), device_id_type=MESH)
            sends.append(pltpu.make_async_remote_copy(src_ref=src(k, peer), dst_ref=outs[k].at[me], **sems))
            recvs.append(pltpu.make_async_remote_copy(src_ref=src(k, peer), dst_ref=outs[k].at[peer], **sems))

    def start():
        for cp in local + sends:
            cp.start()

    def wait():
        for cp in recvs:
            cp.wait_recv()
        for cp in sends:
            cp.wait_send()
        for cp in local:
            cp.wait()

    return start, wait


def _exchange_sems(n_arr, n_peer=N_DEV - 1):
    return [pltpu.SemaphoreType.DMA((n_arr, n_peer)), pltpu.SemaphoreType.DMA((n_arr, n_peer)),
            pltpu.SemaphoreType.DMA((n_arr,))]


def _exchange_shape(a, gather):
    return jax.ShapeDtypeStruct(((N_DEV,) + a.shape) if gather else a.shape, a.dtype)


def _gather_two_level(arrays, name):
    n_arr = len(arrays)

    def body(*refs):
        ins, outs = refs[:n_arr], refs[n_arr:2 * n_arr]
        send_sems, recv_sems, local_sems = refs[2 * n_arr:]
        x, y, c = _coords()

        def slot(k, px, py, pc):
            return outs[k].at[4 * px + 2 * py + pc]

        def routed(core):
            me, sibling = (x, y, core), (x, y, 1 - core)
            xn, yn, dg = (1 - x, y), (x, 1 - y), (1 - x, 1 - y)
            (first, s_first), (second, s_second) = ((xn, 1), (yn, 2)) if core == 0 else ((yn, 2), (xn, 1))

            def copy(k, j, block, to, own=False):
                return pltpu.make_async_remote_copy(
                    src_ref=ins[k] if own else slot(k, *block), dst_ref=slot(k, *block),
                    send_sem=send_sems.at[k, j], recv_sem=recv_sems.at[k, j],
                    device_id=to, device_id_type=MESH)

            local = [pltpu.make_async_copy(ins[k], slot(k, *me), local_sems.at[k]) for k in range(n_arr)]
            sent = []
            for k in range(n_arr):
                sent += [copy(k, 0, me, sibling, True), copy(k, 1, me, (*xn, core), True),
                         copy(k, 2, me, (*yn, core), True)]
            for cp in local + sent:
                cp.start()

            def pass_on(k, j_from, j_to, block, targets):
                copy(k, j_from, block, me).wait_recv()
                for j, to in zip(j_to, targets):
                    cp = copy(k, j, block, to)
                    cp.start()
                    sent.append(cp)

            for k in range(n_arr):
                pass_on(k, s_first, (3, 3 + s_first), (*first, core), ((*second, core), sibling))
            for k in range(n_arr):
                pass_on(k, s_second, (3 + s_second,), (*second, core), (sibling,))
            for k in range(n_arr):
                pass_on(k, 3, (6,), (*dg, core), (sibling,))
            for k in range(n_arr):
                copy(k, 0, sibling, me).wait_recv()
                for j, chip in ((4, xn), (5, yn), (6, dg)):
                    copy(k, j, (*chip, 1 - core), me).wait_recv()
            for cp in sent:
                cp.wait_send()
            for cp in local:
                cp.wait()

        for core in (0, 1):
            pl.when(c == core)(lambda core=core: routed(core))

    return pl.pallas_call(
        body, name=name,
        in_specs=[ANY] * n_arr, out_specs=[ANY] * n_arr,
        out_shape=[_exchange_shape(a, True) for a in arrays],
        scratch_shapes=_exchange_sems(n_arr),
    )(*arrays)


def _sibling_swap(blocks):
    def body(b_ref, o_ref, send_sems, recv_sems):
        x, y, c = _coords()
        copies = [pltpu.make_async_remote_copy(
            src_ref=b_ref.at[2 * q + (1 - c)], dst_ref=o_ref.at[q],
            send_sem=send_sems.at[q], recv_sem=recv_sems.at[q],
            device_id=(x, y, 1 - c), device_id_type=MESH) for q in range(N_CHIP)]
        for cp in copies:
            cp.start()
        for cp in copies:
            cp.wait_recv()
        for cp in copies:
            cp.wait_send()

    return pl.pallas_call(
        body, name="sibling_swap", in_specs=[ANY], out_specs=ANY,
        out_shape=jax.ShapeDtypeStruct((N_CHIP,) + blocks.shape[1:], blocks.dtype),
        scratch_shapes=[pltpu.SemaphoreType.DMA((N_CHIP,)), pltpu.SemaphoreType.DMA((N_CHIP,))],
    )(blocks)


def _pair_sum(blocks, theirs, core):
    _, r, c = blocks.shape

    def body(core_ref, a_ref, b_ref, o_ref):
        o_ref[...] = (a_ref[...].astype(F32) + b_ref[...].astype(F32)).astype(BF16)

    return pl.pallas_call(
        body, name="pair_sum",
        grid_spec=pltpu.PrefetchScalarGridSpec(
            num_scalar_prefetch=1, grid=(N_CHIP,),
            in_specs=[pl.BlockSpec((1, r, c), lambda q, core_ref: (2 * q + core_ref[0], 0, 0)),
                      pl.BlockSpec((1, r, c), lambda q, core_ref: (q, 0, 0))],
            out_specs=pl.BlockSpec((1, r, c), lambda q, core_ref: (q, 0, 0))),
        out_shape=jax.ShapeDtypeStruct((N_CHIP, r, c), BF16),
        compiler_params=_params(dimension_semantics=("arbitrary",)),
    )(core, blocks, theirs)


def _du_rms(dz, wt_pad, h_pad, dout, g, pair_blocks):
    tm = 272
    steps = T // tm

    def body(dz_ref, w_ref, h_ref, d_ref, g_ref, p_ref, dh_ref, dg_ref, got_ref,
             send_sems, recv_sems, local_sems):
        start, wait = _direct_exchange([p_ref], [got_ref], send_sems, recv_sems, local_sems,
                                       False, chips_only=True)
        i = pl.program_id(0)
        pl.when(i == 0)(start)

        @pl.when(i == 0)
        def _():
            dg_ref[...] = jnp.zeros_like(dg_ref)

        du = _dot(dz_ref[...], w_ref[...])
        h = h_ref[...]
        r = lax.rsqrt(jnp.mean(h * h, axis=-1, keepdims=True) + EPS)
        xn = h * r
        dg_ref[...] += jnp.sum(du * xn, axis=0, keepdims=True)
        dn = du * g_ref[...]
        dh_ref[...] = d_ref[...] + r * (dn - xn * jnp.mean(dn * xn, axis=-1, keepdims=True))

        pl.when(i == steps - 1)(wait)

    tile = pl.BlockSpec((tm, D_MODEL), lambda i: (i, 0))
    return pl.pallas_call(
        body, name="du_rms", grid=(steps,),
        in_specs=[pl.BlockSpec((tm, D_IN_PAD), lambda i: (i, 0)),
                  pl.BlockSpec((D_IN_PAD, D_MODEL), lambda i: (0, 0)),
                  tile, tile, pl.BlockSpec((1, D_MODEL), lambda i: (0, 0)), ANY],
        out_specs=[tile, pl.BlockSpec((1, D_MODEL), lambda i: (0, 0)), ANY],
        out_shape=[jax.ShapeDtypeStruct((T, D_MODEL), F32),
                   jax.ShapeDtypeStruct((1, D_MODEL), F32),
                   jax.ShapeDtypeStruct(pair_blocks.shape, pair_blocks.dtype)],
        scratch_shapes=_exchange_sems(1, N_CHIP - 1),
        compiler_params=_params(dimension_semantics=("arbitrary",)),
    )(dz, wt_pad, h_pad, dout, g, pair_blocks)


def _tri(lower):
    r = lax.broadcasted_iota(jnp.int32, (CHUNK, CHUNK), 0)
    c = lax.broadcasted_iota(jnp.int32, (CHUNK, CHUNK), 1)
    return jnp.where((r >= c) if lower else (r <= c), 1.0, 0.0).astype(F32)


def _row_valid(n):
    r = lax.broadcasted_iota(jnp.int32, (CHUNK, 128), 0) + n * CHUNK
    return r >= PAD


_FF_SPEC = pl.BlockSpec((T, 128), lambda i: (0, FF_BASE // 128))


def _forget_fwd(z, b_pad):
    def body(z_ref, b_ref, o_ref):
        tri = _tri(True)
        carry = jnp.zeros((1, 128), F32)
        for n in range(NCHUNK):
            rows = pl.ds(n * CHUNK, CHUNK)
            a = z_ref[rows, :] + b_ref[...]
            lf = -(jnp.maximum(-a, 0.0) + jnp.log(1.0 + jnp.exp(-jnp.abs(a))))
            lf = jnp.where(_row_valid(n), lf, 0.0)
            c = jnp.dot(tri, lf, precision=lax.Precision.HIGHEST,
                        preferred_element_type=F32) + carry
            carry = c[CHUNK - 1:CHUNK, :]
            o_ref[:, rows] = jnp.where(_row_valid(n), c * (-LOG2E), NEG_INF).T

    return pl.pallas_call(
        body, name="forget_fwd", grid=(1,),
        in_specs=[_FF_SPEC, pl.BlockSpec((1, 128), lambda i: (0, 0))],
        out_specs=pl.BlockSpec((128, T), lambda i: (0, 0)),
        out_shape=jax.ShapeDtypeStruct((128, T), F32),
        compiler_params=_params(dimension_semantics=("arbitrary",)),
    )(z, b_pad)


def _forget_bwd(z, b_pad, dc, dz):
    def body(z_ref, b_ref, dc_ref, dz_in, dff_ref, db_ref):
        tri = _tri(False)
        carry = jnp.zeros((1, 128), F32)
        db = jnp.zeros((1, 128), F32)
        for n in reversed(range(NCHUNK)):
            rows = pl.ds(n * CHUNK, CHUNK)
            dc_blk = jnp.concatenate([dc_ref[:, rows], jnp.zeros((128 - FOX_HEADS, CHUNK), F32)], axis=0).T
            dlf = jnp.dot(tri, dc_blk, precision=lax.Precision.HIGHEST,
                          preferred_element_type=F32) + carry
            carry = dlf[0:1, :]
            a = z_ref[rows, :] + b_ref[...]
            dff = jnp.where(_row_valid(n), dlf * jax.nn.sigmoid(-a), 0.0)
            dff_ref[rows, :] = dff.astype(BF16)
            db = db + jnp.sum(dff, axis=0, keepdims=True)
        db_ref[...] = db

    return pl.pallas_call(
        body, name="forget_bwd", grid=(1,),
        in_specs=[_FF_SPEC, pl.BlockSpec((1, 128), lambda i: (0, 0)),
                  pl.BlockSpec((FOX_HEADS, T), lambda i: (0, 0)), ANY],
        out_specs=[_FF_SPEC, pl.BlockSpec((1, 128), lambda i: (0, 0))],
        out_shape=[jax.ShapeDtypeStruct((T, D_IN_PAD), BF16),
                   jax.ShapeDtypeStruct((1, 128), F32)],
        input_output_aliases={3: 0},
        compiler_params=_params(dimension_semantics=("arbitrary",)),
    )(z, b_pad, dc, dz)


FOX_QB = 512
FOX_NQB = SEQ // FOX_QB


def _fox_block(b):
    lo = CHUNK + b * FOX_QB
    return pl.ds(lo, FOX_QB), lo, lo + FOX_QB


def _causal_bias():
    r = lax.broadcasted_iota(jnp.int32, (FOX_QB, FOX_QB), 0)
    c = lax.broadcasted_iota(jnp.int32, (FOX_QB, FOX_QB), 1)
    return jnp.where(c <= r, 0.0, NEG_INF).astype(F32)


def _fox_logits(q_blk, k_all, bias, causal, b):
    _, lo, hi = _fox_block(b)
    s_off = _dot_nt(q_blk, k_all[:lo]) + bias[:, :lo]
    s_dia = _dot_nt(q_blk, k_all[lo:hi]) + (bias[:, lo:hi] + causal)
    return s_off, s_dia


_FOX_Z_SPEC = pl.BlockSpec((T, FOX_W), lambda p: (0, FOX_BASE // FOX_W + p))
_FOX_BIAS_SPEC = pl.BlockSpec((2, 1, T), lambda p: (p, 0, 0))
_FOX_LSE_SPEC = pl.BlockSpec((2, T, 1), lambda p: (p, 0, 0))
_FOX_SCALE = FOX_D ** -0.5
_FOX_QSCALE = _FOX_SCALE * LOG2E


def _fox_fwd(z, bias, y, w_out_blk):
    last = FOX_PAIRS - 1

    def body(z_ref, b_ref, y_in, w_ref, a_ref, lse_ref, y_ref, wall_ref,
             send_sems, recv_sems, local_sems):
        start, wait = _direct_exchange([w_ref], [wall_ref], send_sems, recv_sems, local_sems, True)
        pl.when(pl.program_id(0) == 0)(start)

        causal = _causal_bias()
        a_ref[pl.ds(0, CHUNK), :] = jnp.zeros((CHUNK, 128), F32)
        y_ref[pl.ds(0, CHUNK), :] = jnp.zeros((CHUNK, 128), BF16)
        for j in range(2):
            lanes = pl.ds(j * FOX_D, FOX_D)
            k_all = z_ref[:, pl.ds(128 + j * FOX_D, FOX_D)].astype(BF16)
            v_all = z_ref[:, pl.ds(256 + j * FOX_D, FOX_D)].astype(BF16)
            bias = b_ref[j]
            lse_ref[j, pl.ds(0, CHUNK), :] = jnp.zeros((CHUNK, 1), F32)
            for b in range(FOX_NQB):
                rows, lo, hi = _fox_block(b)
                q_blk = (z_ref[rows, lanes] * _FOX_QSCALE).astype(BF16)
                s_off, s_dia = _fox_logits(q_blk, k_all, bias, causal, b)
                m = jnp.maximum(jnp.max(s_off, axis=-1, keepdims=True),
                                jnp.max(s_dia, axis=-1, keepdims=True))
                e_off = jnp.exp2(s_off - m)
                e_dia = jnp.exp2(s_dia - m)
                total = jnp.sum(e_off, axis=-1, keepdims=True) + jnp.sum(e_dia, axis=-1, keepdims=True)
                o = (_dot(e_off.astype(BF16), v_all[:lo]) + _dot(e_dia.astype(BF16), v_all[lo:hi])) / total
                a_ref[rows, lanes] = o
                lse_ref[j, rows, :] = m + jnp.log(total) * LOG2E
                gate = _silu_parts(z_ref[rows, pl.ds(384 + j * FOX_D, FOX_D)])[0]
                y_ref[rows, lanes] = (o * gate).astype(BF16)

        pl.when(pl.program_id(0) == last)(wait)

    return pl.pallas_call(
        body, name="fox_fwd", grid=(FOX_PAIRS,),
        in_specs=[_FOX_Z_SPEC, _FOX_BIAS_SPEC, ANY, ANY],
        out_specs=[pl.BlockSpec((T, 128), lambda p: (0, p)), _FOX_LSE_SPEC,
                   pl.BlockSpec((T, 128), lambda p: (0, 8 + p)), ANY],
        out_shape=[jax.ShapeDtypeStruct((T, FOX_HEADS * FOX_D), F32),
                   jax.ShapeDtypeStruct((FOX_HEADS, T, 1), F32),
                   jax.ShapeDtypeStruct((T, D_MIX), BF16),
                   _exchange_shape(w_out_blk, True)],
        input_output_aliases={2: 2},
        scratch_shapes=_exchange_sems(1),
        compiler_params=_params(dimension_semantics=("arbitrary",)),
    )(z, bias, y, w_out_blk)


def _fox_bwd(z, bias, a_f, lse, dy, dz, dwo_blocks):
    last = FOX_PAIRS - 1

    def body(z_ref, b_ref, a_ref, lse_ref, dy_ref, dz_in, dwo_ref, dz_ref, dc_ref, got_ref,
             kv_acc, dc_acc, send_sems, recv_sems, local_sems):
        start, wait = _direct_exchange([dwo_ref], [got_ref], send_sems, recv_sems, local_sems, False)
        pl.when(pl.program_id(0) == 0)(start)

        causal = _causal_bias()
        dz_ref[pl.ds(0, CHUNK), pl.ds(0, 128)] = jnp.zeros((CHUNK, 128), BF16)
        dz_ref[pl.ds(0, CHUNK), pl.ds(384, 128)] = jnp.zeros((CHUNK, 128), BF16)
        dk_rows, dv_rows = pl.ds(0, FOX_D), pl.ds(FOX_D, FOX_D)
        for j in range(2):
            lanes = pl.ds(j * FOX_D, FOX_D)
            k_all = z_ref[:, pl.ds(128 + j * FOX_D, FOX_D)].astype(BF16)
            v_all = z_ref[:, pl.ds(256 + j * FOX_D, FOX_D)].astype(BF16)
            bias = b_ref[j]
            kv_acc[...] = jnp.zeros_like(kv_acc)
            dc_acc[...] = jnp.zeros_like(dc_acc)
            for b in range(FOX_NQB):
                rows, lo, hi = _fox_block(b)
                off, dia = pl.ds(0, lo), pl.ds(lo, FOX_QB)
                q_blk = (z_ref[rows, lanes] * _FOX_QSCALE).astype(BF16)
                s_off, s_dia = _fox_logits(q_blk, k_all, bias, causal, b)
                lse_blk = lse_ref[j, rows, :]
                p_off, p_dia = jnp.exp2(s_off - lse_blk), jnp.exp2(s_dia - lse_blk)
                sg, dsg = _silu_parts(z_ref[rows, pl.ds(384 + j * FOX_D, FOX_D)])
                dyj = dy_ref[rows, lanes]
                dz_ref[rows, pl.ds(384 + j * FOX_D, FOX_D)] = (dyj * a_ref[rows, lanes] * dsg).astype(BF16)
                do_b = (dyj * sg).astype(BF16)
                dp_off = _dot_nt(do_b, v_all[:lo])
                dp_dia = _dot_nt(do_b, v_all[lo:hi])
                d = (jnp.sum(p_off * dp_off, axis=-1, keepdims=True)
                     + jnp.sum(p_dia * dp_dia, axis=-1, keepdims=True))
                ds_off = p_off * (dp_off - d)
                ds_dia = p_dia * (dp_dia - d)
                dc_acc[:, off] -= jnp.sum(ds_off, axis=0, keepdims=True)
                dc_acc[:, dia] -= jnp.sum(ds_dia, axis=0, keepdims=True)
                ds_off_b, ds_dia_b = ds_off.astype(BF16), ds_dia.astype(BF16)
                dq = _dot(ds_off_b, k_all[:lo]) + _dot(ds_dia_b, k_all[lo:hi])
                dz_ref[rows, lanes] = (dq * _FOX_SCALE).astype(BF16)
                kv_acc[dk_rows, off] += _dot_tn(q_blk, ds_off_b)
                kv_acc[dk_rows, dia] += _dot_tn(q_blk, ds_dia_b)
                kv_acc[dv_rows, off] += _dot_tn(do_b, p_off.astype(BF16))
                kv_acc[dv_rows, dia] += _dot_tn(do_b, p_dia.astype(BF16))
            for n in range(NCHUNK):
                rows = pl.ds(n * CHUNK, CHUNK)
                both = kv_acc[:, rows].T
                dz_ref[rows, pl.ds(128 + j * FOX_D, FOX_D)] = (both[:, :FOX_D] * LN2).astype(BF16)
                dz_ref[rows, pl.ds(256 + j * FOX_D, FOX_D)] = both[:, FOX_D:].astype(BF16)
            dc_ref[j] = dc_acc[...]

        pl.when(pl.program_id(0) == last)(wait)

    col = lambda base: pl.BlockSpec((T, 128), lambda p: (0, base + p))
    return pl.pallas_call(
        body, name="fox_bwd", grid=(FOX_PAIRS,),
        in_specs=[_FOX_Z_SPEC, _FOX_BIAS_SPEC, col(0), _FOX_LSE_SPEC, col(8), ANY, ANY],
        out_specs=[_FOX_Z_SPEC, _FOX_BIAS_SPEC, ANY],
        out_shape=[jax.ShapeDtypeStruct((T, D_IN_PAD), BF16),
                   jax.ShapeDtypeStruct((FOX_HEADS, 1, T), F32),
                   _exchange_shape(dwo_blocks, False)],
        input_output_aliases={5: 0},
        scratch_shapes=[pltpu.VMEM((2 * FOX_D, T), F32), pltpu.VMEM((1, T), F32)] + _exchange_sems(1),
        compiler_params=_params(dimension_semantics=("arbitrary",)),
    )(z, bias, a_f, lse, dy, dz, dwo_blocks)


def _rot(x, cosf, sins):
    return x * cosf + pltpu.roll(x, RET_DK // 2, 1) * sins


def _rot_t(d, cosf, sins):
    return d * cosf - pltpu.roll(d, RET_DK // 2, 1) * sins


_RET_Z_SPEC = pl.BlockSpec((T, RET_W), lambda h: (0, h))
_RET_TABLE_SPECS = [
    pl.BlockSpec((T, RET_DK), lambda h: (0, 0)),
    pl.BlockSpec((T, RET_DK), lambda h: (0, 0)),
    pl.BlockSpec((1, CHUNK, CHUNK), lambda h: (h, 0, 0)),
    pl.BlockSpec((1, CHUNK, 1), lambda h: (h, 0, 0)),
    pl.BlockSpec((1, CHUNK, 1), lambda h: (h, 0, 0)),
    pl.BlockSpec((1, 1, 1), lambda h: (h, 0, 0)),
]
_RQ, _RK = pl.ds(0, RET_DK), pl.ds(RET_DK, RET_DK)
_RV, _RG = pl.ds(2 * RET_DK, RET_DV), pl.ds(2 * RET_DK + RET_DV, RET_DV)
_RET_KSCALE = RET_DK ** -0.5


def _ret_fwd(z, tables):
    def body(z_ref, cos_ref, sin_ref, dm_ref, zeta_ref, xi_ref, cd_ref, raw_ref, y_ref):
        dmask, zeta, xi, cdec = dm_ref[0], zeta_ref[0], xi_ref[0], cd_ref[0]
        state = jnp.zeros((RET_DK, RET_DV), F32)
        for n in range(NCHUNK):
            rows = pl.ds(n * CHUNK, CHUNK)
            cosf, sins = cos_ref[rows, :], sin_ref[rows, :]
            qr = _rot(z_ref[rows, _RQ], cosf, sins)
            kr_b = (_rot(z_ref[rows, _RK], cosf, sins) * _RET_KSCALE).astype(BF16)
            v = z_ref[rows, _RV]
            a = _dot_nt(qr.astype(BF16), kr_b) * dmask
            out = _dot(a.astype(BF16), v.astype(BF16)) + _dot((qr * xi).astype(BF16), state.astype(BF16))
            state = state * cdec + _dot_tn(kr_b, (v * zeta).astype(BF16))
            raw_ref[rows, :] = out
            r = lax.rsqrt(jnp.mean(out * out, axis=-1, keepdims=True) + EPS)
            y_ref[rows, :] = (out * r * _silu_parts(z_ref[rows, _RG])[0]).astype(BF16)

    wide = pl.BlockSpec((T, RET_DV), lambda h: (0, h))
    return pl.pallas_call(
        body, name="ret_fwd", grid=(RET_HEADS,),
        in_specs=[_RET_Z_SPEC] + _RET_TABLE_SPECS,
        out_specs=[wide, wide],
        out_shape=[jax.ShapeDtypeStruct((T, RET_HEADS * RET_DV), F32),
                   jax.ShapeDtypeStruct((T, D_MIX), BF16)],
        compiler_params=_params(dimension_semantics=("arbitrary",)),
    )(z, *tables)


def _ret_bwd(z, tables, raw, dy):
    def body(z_ref, cos_ref, sin_ref, dm_ref, zeta_ref, xi_ref, cd_ref, raw_ref, dy_ref,
             dz_ref, st_ref):
        dmask, zeta, xi, cdec = dm_ref[0], zeta_ref[0], xi_ref[0], cd_ref[0]

        def rotated(n):
            rows = pl.ds(n * CHUNK, CHUNK)
            cosf, sins = cos_ref[rows, :], sin_ref[rows, :]
            qr = _rot(z_ref[rows, _RQ], cosf, sins)
            kr_b = (_rot(z_ref[rows, _RK], cosf, sins) * _RET_KSCALE).astype(BF16)
            return rows, cosf, sins, qr, kr_b

        state = jnp.zeros((RET_DK, RET_DV), F32)
        for n in range(NCHUNK):
            st_ref[n] = state.astype(BF16)
            if n + 1 < NCHUNK:
                rows, _, _, _, kr_b = rotated(n)
                state = state * cdec + _dot_tn(kr_b, (z_ref[rows, _RV] * zeta).astype(BF16))

        grad_state = jnp.zeros((RET_DK, RET_DV), F32)
        for n in reversed(range(NCHUNK)):
            rows, cosf, sins, qr, kr_b = rotated(n)
            qr_b = qr.astype(BF16)
            v_b = z_ref[rows, _RV].astype(BF16)
            gs_b = grad_state.astype(BF16)
            o = raw_ref[rows, :]
            r = lax.rsqrt(jnp.mean(o * o, axis=-1, keepdims=True) + EPS)
            hn = o * r
            sg, dsg = _silu_parts(z_ref[rows, _RG])
            dyn = dy_ref[rows, :]
            dz_ref[rows, _RG] = (dyn * hn * dsg).astype(BF16)
            dhn = dyn * sg
            do_b = (r * (dhn - hn * jnp.mean(dhn * hn, axis=-1, keepdims=True))).astype(BF16)
            a_b = (_dot_nt(qr_b, kr_b) * dmask).astype(BF16)
            da_b = (_dot_nt(do_b, v_b) * dmask).astype(BF16)
            dqr = _dot(da_b, kr_b) + xi * _dot_nt(do_b, st_ref[n])
            dkr = _dot_tn(da_b, qr_b) + zeta * _dot_nt(v_b, gs_b)
            dv = _dot_tn(a_b, do_b) + zeta * _dot(kr_b, gs_b)
            grad_state = grad_state * cdec + _dot_tn((qr * xi).astype(BF16), do_b)
            dz_ref[rows, _RQ] = _rot_t(dqr, cosf, sins).astype(BF16)
            dz_ref[rows, _RK] = (_rot_t(dkr, cosf, sins) * _RET_KSCALE).astype(BF16)
            dz_ref[rows, _RV] = dv.astype(BF16)

    wide = pl.BlockSpec((T, RET_DV), lambda h: (0, h))
    return pl.pallas_call(
        body, name="ret_bwd", grid=(RET_HEADS,),
        in_specs=[_RET_Z_SPEC] + _RET_TABLE_SPECS + [wide, wide],
        out_specs=_RET_Z_SPEC,
        out_shape=jax.ShapeDtypeStruct((T, D_IN_PAD), BF16),
        scratch_shapes=[pltpu.VMEM((NCHUNK, RET_DK, RET_DV), BF16)],
        compiler_params=_params(dimension_semantics=("arbitrary",)),
    )(z, *tables, raw, dy)


def _adamw(w, g, m, v):
    m = ADAM_B1 * m + (1.0 - ADAM_B1) * g
    v = ADAM_B2 * v + (1.0 - ADAM_B2) * (g * g)
    m_hat = m / (1.0 - ADAM_B1 ** ADAM_STEP)
    v_hat = v / (1.0 - ADAM_B2 ** ADAM_STEP)
    delta = -ADAM_LR * (m_hat / (jnp.sqrt(v_hat) + ADAM_EPS) + ADAM_WD * w)
    return delta, m, v


def _sum_adamw(parts, w, m, v, rows, name):
    _, r_tot, cols = parts.shape
    assert r_tot % rows == 0

    def body(p_ref, w_ref, m_ref, v_ref, g_ref, d_ref, nm_ref, nv_ref):
        g = p_ref[0].astype(F32)
        for d in range(1, N_DEV):
            g = g + p_ref[d].astype(F32)
        delta, nm, nv = _adamw(w_ref[...], g, m_ref[...], v_ref[...])
        g_ref[...] = g
        d_ref[...] = delta
        nm_ref[...] = nm
        nv_ref[...] = nv

    blk = pl.BlockSpec((rows, cols), lambda i: (i, 0))
    return pl.pallas_call(
        body, name=name, grid=(r_tot // rows,),
        in_specs=[pl.BlockSpec((N_DEV, rows, cols), lambda i: (0, i, 0)), blk, blk, blk],
        out_specs=[blk] * 4,
        out_shape=[jax.ShapeDtypeStruct((r_tot, cols), F32)] * 4,
        compiler_params=_params(dimension_semantics=("arbitrary",)),
    )(parts, w, m, v)


def _sum_adamw_w_in(parts, w, m, v, small):
    n_part, r, c = parts.shape
    steps = c // 128

    def body(p_ref, w_ref, m_ref, v_ref, s_ref, g_ref, d_ref, nm_ref, nv_ref, got_ref,
             send_sems, recv_sems, local_sems):
        start, wait = _direct_exchange([s_ref], [got_ref], send_sems, recv_sems, local_sems, True)
        pl.when(pl.program_id(0) == 0)(start)
        g = p_ref[0].astype(F32)
        for d in range(1, n_part):
            g = g + p_ref[d].astype(F32)
        delta, nm, nv = _adamw(w_ref[...], g, m_ref[...], v_ref[...])
        g_ref[...] = g
        d_ref[...] = delta
        nm_ref[...] = nm
        nv_ref[...] = nv
        pl.when(pl.program_id(0) == steps - 1)(wait)

    blk = pl.BlockSpec((r, 128), lambda i: (0, i))
    return pl.pallas_call(
        body, name="adamw_w_in", grid=(steps,),
        in_specs=[pl.BlockSpec((n_part, r, 128), lambda i: (0, 0, i)), blk, blk, blk, ANY],
        out_specs=[blk] * 4 + [ANY],
        out_shape=[jax.ShapeDtypeStruct((r, c), F32)] * 4 + [_exchange_shape(small, True)],
        scratch_shapes=_exchange_sems(1),
        compiler_params=_params(dimension_semantics=("arbitrary",)),
    )(parts, w, m, v, small)


def _adamw_small(got, me, metas, norms, finals, biases):
    def body(me_ref, gm_ref, gr_ref, *refs):
        ins, outs = refs[:12], refs[12:]
        g_meta, g_rest = gm_ref[0], gr_ref[0]
        for d in range(1, N_DEV):
            g_meta, g_rest = g_meta + gm_ref[d], g_rest + gr_ref[d]
        grads = [g_meta, g_rest[0:1], g_rest[1:2], g_rest[2:3, :FOX_HEADS]]
        for k, g in enumerate(grads):
            w_ref, m_ref, v_ref = ins[3 * k:3 * k + 3]
            delta, new_m, new_v = _adamw(w_ref[...], g, m_ref[...], v_ref[...])
            for o_ref, val in zip(outs[4 * k:4 * k + 4], (g, delta, new_m, new_v)):
                o_ref[...] = val
        outs[16][...] = g_rest[3:4, :128]

    groups = (metas, norms, finals, biases)
    full = lambda a: pl.BlockSpec(a.shape, lambda i, me_ref: (0,) * a.ndim)
    flat = [a for grp in groups for a in grp]
    res = pl.pallas_call(
        body, name="adamw_small",
        grid_spec=pltpu.PrefetchScalarGridSpec(
            num_scalar_prefetch=1, grid=(1,),
            in_specs=[pl.BlockSpec((N_DEV, N_META, META_BLK), lambda i, me_ref: (0, 0, me_ref[0])),
                      pl.BlockSpec((N_DEV, 8, D_MODEL), lambda i, me_ref: (0, N_META // 8, 0))]
            + [full(a) for a in flat],
            out_specs=[full(grp[0]) for grp in groups for _ in range(4)]
            + [pl.BlockSpec((1, 128), lambda i, me_ref: (0, 0))]),
        out_shape=[jax.ShapeDtypeStruct(grp[0].shape, F32) for grp in groups for _ in range(4)]
        + [jax.ShapeDtypeStruct((1, 128), F32)],
        compiler_params=_params(dimension_semantics=("arbitrary",)),
    )(me, got, got, *flat)
    return [res[4 * k:4 * k + 4] for k in range(4)], res[16]


def kernel(x, meta_tokens, norm_g, w_in, b_f, w_out, final_g, loss_target, m_meta_tokens, m_norm_g, m_w_in, m_b_f, m_w_out, m_final_g, v_meta_tokens, v_norm_g, v_w_in, v_b_f, v_w_out, v_final_g):
    core = lax.axis_index("c")
    me = 4 * lax.axis_index("x") + 2 * lax.axis_index("y") + core
    tables = _tables()

    wt_all, meta_all = _gather_two_level([w_in[0].T.astype(BF16), meta_tokens], name="gather_w_in")
    wt_pad = _regroup_rows(wt_all.reshape(D_IN, D_MODEL))
    meta_full = jnp.transpose(meta_all, (1, 0, 2)).reshape(N_META, D_MODEL)
    h_pad = jnp.concatenate([jnp.zeros((PAD, D_MODEL), F32), meta_full, x[0]], axis=0)
    b_pad = jnp.pad(b_f, ((0, 0), (0, 128 - FOX_HEADS)))

    u = _rms_fwd(h_pad, norm_g)
    z = _mm(u, wt_pad, nt=True, tm=T // 2, tn=D_IN_PAD // 3, out_dtype=F32, name="mm_z")
    bias = _forget_fwd(z, b_pad)[:FOX_HEADS].reshape(FOX_HEADS, 1, T)
    raw, y = _ret_fwd(z, tables)
    a_f, lse, y, w_out_all = _fox_fwd(z, bias, y, w_out[0].astype(BF16))
    w_out_b = w_out_all.reshape(D_MIX, D_MODEL)
    dout, dout_b, dy, loss_blk, d_final_g = _out_loss_dy(y, w_out_b, h_pad, loss_target[0],
                                                         final_g.reshape(1, D_MODEL))

    d_w_out = _mm(y, dout_b, ta=True, tm=D_MIX, tn=256, out_dtype=BF16, name="mm_dwout")
    dz = _ret_bwd(z, tables, raw, dy)
    dz, dc, got_w_out = _fox_bwd(z, bias, a_f, lse, dy, dz, d_w_out.reshape(N_DEV, WO_BLK, D_MODEL))
    dz, db_f = _forget_bwd(z, b_pad, dc.reshape(FOX_HEADS, T), dz)
    d_wt_pad = _mm(dz, u, ta=True, tm=384, tn=D_MODEL, out_dtype=BF16, name="mm_dwin")

    d_wt_blocks = _ungroup_rows(d_wt_pad).reshape(N_DEV, W_BLK, D_MODEL)
    theirs = _sibling_swap(d_wt_blocks)
    pair = _pair_sum(d_wt_blocks, theirs, core.astype(jnp.int32).reshape(1))
    dh, d_norm_g, got_w_in = _du_rms(dz, wt_pad, h_pad, dout, norm_g, pair)

    small = jnp.concatenate([
        dh[PAD:CHUNK], d_norm_g, d_final_g, jnp.pad(db_f[:, :FOX_HEADS], ((0, 0), (0, D_MODEL - FOX_HEADS))),
        jnp.pad(loss_blk[0:1], ((0, 0), (0, D_MODEL - 128))),
        jnp.zeros((SMALL_ROWS - N_META - 4, D_MODEL), F32)], axis=0)
    g_w_in, d_w_in, nm_w_in, nv_w_in, got_small = _sum_adamw_w_in(
        got_w_in, w_in[0].T, m_w_in[0].T, v_w_in[0].T, small)
    g_w_out, d_w_out, nm_w_out, nv_w_out = _sum_adamw(got_w_out, w_out[0], m_w_out[0], v_w_out[0], 128, "adamw_w_out")

    row = lambda a: a.reshape(1, D_MODEL)
    (meta_o, norm_o, final_o, bias_o), loss_row = _adamw_small(
        got_small, me.astype(jnp.int32).reshape(1),
        (meta_tokens, m_meta_tokens, v_meta_tokens), (norm_g, m_norm_g, v_norm_g),
        (row(final_g), row(m_final_g), row(v_final_g)), (b_f, m_b_f, v_b_f))
    final_o = [a.reshape(D_MODEL) for a in final_o]

    back = lambda a: a.T[None]
    outs = [[meta_o[k], norm_o[k], back(wk), bias_o[k], ok[None], final_o[k]]
            for k, (wk, ok) in enumerate(zip((g_w_in, d_w_in, nm_w_in, nv_w_in),
                                             (g_w_out, d_w_out, nm_w_out, nv_w_out)))]
    return (loss_row[0, 0], dh[CHUNK:][None], *outs[0], *outs[1], *outs[2], *outs[3])
```

```python
import numpy as np
import jax
import jax.numpy as jnp
from jax import lax
from jax.experimental import pallas as pl
from jax.experimental.pallas import tpu as pltpu

F32 = jnp.float32
BF16 = jnp.bfloat16

N_DEV = 8
N_CHIP = 4
D_MODEL = 1024
SEQ = 2048
N_META = 16
CHUNK = 128
PAD = CHUNK - N_META
T = SEQ + CHUNK
NCHUNK = T // CHUNK
D_MIX = 2048
RET_HEADS = 4
RET_DK = 128
RET_DV = 256
RET_W = 2 * RET_DK + 2 * RET_DV
FOX_HEADS = 16
FOX_D = 64
FOX_PAIRS = FOX_HEADS // 2
FOX_W = 4 * 128
FOX_BASE = RET_HEADS * RET_W
FF_BASE = FOX_BASE + FOX_PAIRS * FOX_W
D_IN = 7184
D_IN_PAD = 7296
W_BLK = D_IN // N_DEV
WO_BLK = D_MIX // N_DEV
META_BLK = D_MODEL // N_DEV
EPS = 1e-6
NEG_INF = -1e30
ROPE_BASE = 10000.0
LOG2E = 1.4426950408889634
LN2 = 0.6931471805599453

ADAM_LR = 0.001
ADAM_B1 = 0.9
ADAM_B2 = 0.999
ADAM_EPS = 1e-08
ADAM_WD = 0.01
ADAM_STEP = 10

SMALL_ROWS = 24
VMEM_LIMIT = 56 * 1024 * 1024
MESH = pl.DeviceIdType.MESH
ANY = pl.BlockSpec(memory_space=pl.ANY)

_NT = (((1,), (1,)), ((), ()))
_TN = (((0,), (0,)), ((), ()))


def _dot(a, b):
    return jnp.dot(a, b, preferred_element_type=F32)


def _dot_nt(a, b):
    return lax.dot_general(a, b, _NT, preferred_element_type=F32)


def _dot_tn(a, b):
    return lax.dot_general(a, b, _TN, preferred_element_type=F32)


def _params(**kw):
    return pltpu.CompilerParams(vmem_limit_bytes=VMEM_LIMIT, **kw)


def _silu_parts(g):
    sig = jax.nn.sigmoid(g)
    return g * sig, sig * (1.0 + g * (1.0 - sig))


def _tables():
    pos = np.arange(T, dtype=np.float32) - PAD
    inv = (ROPE_BASE ** (-np.arange(0, RET_DK, 2, dtype=np.float32) / RET_DK)).astype(np.float32)
    ang = pos[:, None] * inv[None, :]
    cos, sin = np.cos(ang), np.sin(ang)
    cosf = np.concatenate([cos, cos], axis=1).astype(np.float32)
    sins = np.concatenate([-sin, sin], axis=1).astype(np.float32)
    h = np.arange(RET_HEADS, dtype=np.float32)
    log_gamma = np.log1p(-np.exp2(-5.0 - h)).astype(np.float32)
    idx = np.arange(CHUNK, dtype=np.float32)
    diff = idx[:, None] - idx[None, :]
    dmask = np.where(diff[None] >= 0,
                     np.exp(log_gamma[:, None, None] * np.maximum(diff, 0.0)[None]), 0.0)
    zeta = np.exp(log_gamma[:, None] * (CHUNK - 1.0 - idx)[None, :])
    xi = np.exp(log_gamma[:, None] * (idx + 1.0)[None, :])
    cdec = np.exp(log_gamma * CHUNK)
    return (jnp.asarray(cosf), jnp.asarray(sins), jnp.asarray(dmask, F32),
            jnp.asarray(zeta[:, :, None], F32), jnp.asarray(xi[:, :, None], F32),
            jnp.asarray(cdec[:, None, None], F32))


W_STRIDE = 896
W_SLAB = 912
W_EDGE = W_SLAB - W_STRIDE
N_PIECE = D_IN_PAD // 128


def _regrouped_pieces():
    order = []
    for h in range(RET_HEADS):
        order += [h, 4 + h, 8 + 2 * h, 9 + 2 * h, 16 + 2 * h, 17 + 2 * h]
    for p in range(FOX_PAIRS):
        order += [24 + 8 * i + p for i in range(4)]
    return order + [N_PIECE - 1]


def _slab(block, me):
    return lax.dynamic_update_slice(jnp.zeros((W_SLAB, block.shape[1]), block.dtype), block,
                                    ((W_BLK - W_STRIDE) * me, 0))


def _regroup_slabs(slabs):
    def edge(k):
        parts = ([slabs[k - 1, W_STRIDE:]] if k > 0 else []) + ([slabs[k, :W_EDGE]] if k < N_DEV else [])
        return parts[0] if len(parts) == 1 else parts[0] + parts[1]

    pieces = []
    for p in _regrouped_pieces():
        d, j = divmod(p, 7)
        if j:
            pieces.append(slabs[d, 128 * j:128 * j + 128])
        elif d < N_DEV:
            pieces += [edge(d), slabs[d, W_EDGE:128]]
        else:
            pieces += [edge(d), jnp.zeros((128 - W_EDGE, slabs.shape[2]), slabs.dtype)]
    return jnp.concatenate(pieces, axis=0)


def _ungroup_slabs(w):
    at = {p: 128 * i for i, p in enumerate(_regrouped_pieces())}
    edge = lambda k: w[at[7 * k]:at[7 * k] + W_EDGE]
    slabs = []
    for d in range(N_DEV):
        slabs.append(jnp.concatenate(
            [edge(d), w[at[7 * d] + W_EDGE:at[7 * d] + 128]]
            + [w[at[7 * d + j]:at[7 * d + j] + 128] for j in range(1, 7)] + [edge(d + 1)], axis=0))
    return jnp.stack(slabs)


def _mm_tn(a, b, *, tm, tn, name):
    k, m = a.shape
    n = b.shape[1]
    assert m % tm == 0 and n % tn == 0

    def body(a_ref, b_ref, o_ref):
        o_ref[...] = _dot_tn(a_ref[...], b_ref[...]).astype(BF16)

    return pl.pallas_call(
        body, name=name, grid=(n // tn, m // tm),
        in_specs=[pl.BlockSpec((k, tm), lambda j, i: (0, i)),
                  pl.BlockSpec((k, tn), lambda j, i: (0, j))],
        out_specs=pl.BlockSpec((tm, tn), lambda j, i: (i, j)),
        out_shape=jax.ShapeDtypeStruct((m, n), BF16),
        compiler_params=_params(dimension_semantics=("arbitrary", "arbitrary")),
    )(a, b)


def _rms_mm_z(h_pad, g, wt_pad):
    tm, tn = T // 2, D_IN_PAD // 3

    def body(h_ref, g_ref, w_ref, u_ref, z_ref):
        h = h_ref[...]
        r = lax.rsqrt(jnp.mean(h * h, axis=-1, keepdims=True) + EPS)
        u = (h * r * g_ref[...]).astype(BF16)
        u_ref[...] = u
        z_ref[...] = _dot_nt(u, w_ref[...])

    return pl.pallas_call(
        body, name="rms_mm_z", grid=(T // tm, D_IN_PAD // tn),
        in_specs=[pl.BlockSpec((tm, D_MODEL), lambda i, j: (i, 0)),
                  pl.BlockSpec((1, D_MODEL), lambda i, j: (0, 0)),
                  pl.BlockSpec((tn, D_MODEL), lambda i, j: (j, 0))],
        out_specs=[pl.BlockSpec((tm, D_MODEL), lambda i, j: (i, 0)),
                   pl.BlockSpec((tm, tn), lambda i, j: (i, j))],
        out_shape=[jax.ShapeDtypeStruct((T, D_MODEL), BF16),
                   jax.ShapeDtypeStruct((T, D_IN_PAD), F32)],
        compiler_params=_params(dimension_semantics=("arbitrary", "arbitrary")),
    )(h_pad, g, wt_pad)


def _out_loss_dy(y, w_out_b, h_pad, target, g):
    tm = T // 4

    def body(y_ref, w_ref, h_ref, t_hbm, g_ref, d_ref, db_ref, dy_ref, loss_ref, dg_ref, t_buf, t_sem):
        i = pl.program_id(0)
        head = pltpu.make_async_copy(t_hbm.at[pl.ds(0, tm - CHUNK)], t_buf.at[pl.ds(CHUNK, tm - CHUNK)], t_sem)
        rest = pltpu.make_async_copy(t_hbm.at[pl.ds(pl.multiple_of(jnp.maximum(i, 1) * tm - CHUNK, 8), tm)],
                                     t_buf, t_sem)

        @pl.when(i == 0)
        def _():
            t_buf[pl.ds(0, CHUNK), :] = jnp.zeros((CHUNK, D_MODEL), F32)
            head.start()
            loss_ref[...] = jnp.zeros_like(loss_ref)
            dg_ref[...] = jnp.zeros_like(dg_ref)

        pl.when(i > 0)(rest.start)

        w = w_ref[...]
        o = _dot(y_ref[...], w) + h_ref[...]
        pl.when(i == 0)(head.wait)
        pl.when(i > 0)(rest.wait)
        token = lax.broadcasted_iota(jnp.int32, (tm, 1), 0) + i * tm >= CHUNK
        g = g_ref[...]
        r = lax.rsqrt(jnp.mean(o * o, axis=-1, keepdims=True) + EPS)
        xn = o * r
        e = jnp.where(token, xn * g - t_buf[...], 0.0)
        loss_ref[...] += jnp.full(loss_ref.shape, 0.5 / D_MODEL * jnp.sum(e * e), F32)
        do = e * (1.0 / D_MODEL)
        dg_ref[...] += jnp.sum(do * xn, axis=0, keepdims=True)
        dn = do * g
        d = r * (dn - xn * jnp.mean(dn * xn, axis=-1, keepdims=True))
        d_b = d.astype(BF16)
        d_ref[...] = d
        db_ref[...] = d_b
        dy_ref[...] = _dot_nt(d_b, w)

    tile = pl.BlockSpec((tm, D_MODEL), lambda i: (i, 0))
    wide = pl.BlockSpec((tm, D_MIX), lambda i: (i, 0))
    return pl.pallas_call(
        body, name="out_loss_dy", grid=(T // tm,),
        in_specs=[wide, pl.BlockSpec((D_MIX, D_MODEL), lambda i: (0, 0)), tile, ANY,
                  pl.BlockSpec((1, D_MODEL), lambda i: (0, 0))],
        out_specs=[tile, tile, wide,
                   pl.BlockSpec((8, 128), lambda i: (0, 0)),
                   pl.BlockSpec((1, D_MODEL), lambda i: (0, 0))],
        out_shape=[jax.ShapeDtypeStruct((T, D_MODEL), F32),
                   jax.ShapeDtypeStruct((T, D_MODEL), BF16),
                   jax.ShapeDtypeStruct((T, D_MIX), F32),
                   jax.ShapeDtypeStruct((8, 128), F32),
                   jax.ShapeDtypeStruct((1, D_MODEL), F32)],
        scratch_shapes=[pltpu.VMEM((tm, D_MODEL), F32), pltpu.SemaphoreType.DMA],
        compiler_params=_params(dimension_semantics=("arbitrary",)),
    )(y, w_out_b, h_pad, target, g)


def _coords():
    return lax.axis_index("x"), lax.axis_index("y"), lax.axis_index("c")


def _flip(v, bit):
    return 1 - v if bit else v


def _peer(x, y, c, r):
    return _flip(x, (r >> 2) & 1), _flip(y, (r >> 1) & 1), _flip(c, r & 1)


def _direct_exchange(ins, outs, send_sems, recv_sems, local_sems, gather, chips_only=False):
    x, y, c = _coords()
    me = 2 * x + y if chips_only else 4 * x + 2 * y + c

    def src(k, to_idx):
        return ins[k] if gather else ins[k].at[to_idx]

    local = [pltpu.make_async_copy(src(k, me), outs[k].at[me], local_sems.at[k])
             for k in range(len(ins))]
    sends, recvs = [], []
    for r in range(1, N_CHIP if chips_only else N_DEV):
        px, py, pc = _peer(x, y, c, 2 * r if chips_only else r)
        peer = 2 * px + py if chips_only else 4 * px + 2 * py + pc
        for k in range(len(ins)):
            sems = dict(send_sem=send_sems.at[k, r - 1], recv_sem=recv_sems.at[k, r - 1],
                        device_id=(px, py, pc), device_id_type=MESH)
            sends.append(pltpu.make_async_remote_copy(src_ref=src(k, peer), dst_ref=outs[k].at[me], **sems))
            recvs.append(pltpu.make_async_remote_copy(src_ref=src(k, peer), dst_ref=outs[k].at[peer], **sems))

    def start():
        for cp in local + sends:
            cp.start()

    def wait():
        for cp in recvs:
            cp.wait_recv()
        for cp in sends:
            cp.wait_send()
        for cp in local:
            cp.wait()

    return start, wait


def _exchange_sems(n_arr, n_peer=N_DEV - 1):
    return [pltpu.SemaphoreType.DMA((n_arr, n_peer)), pltpu.SemaphoreType.DMA((n_arr, n_peer)),
            pltpu.SemaphoreType.DMA((n_arr,))]


def _exchange_shape(a, gather):
    return jax.ShapeDtypeStruct(((N_DEV,) + a.shape) if gather else a.shape, a.dtype)


def _gather_two_level(arrays, name):
    n_arr = len(arrays)

    def body(*refs):
        ins, outs = refs[:n_arr], refs[n_arr:2 * n_arr]
        send_sems, recv_sems, local_sems = refs[2 * n_arr:]
        x, y, c = _coords()

        def slot(k, px, py, pc):
            return outs[k].at[4 * px + 2 * py + pc]

        def routed(core):
            me, sibling = (x, y, core), (x, y, 1 - core)
            xn, yn, dg = (1 - x, y), (x, 1 - y), (1 - x, 1 - y)
            (first, s_first), (second, s_second) = ((xn, 1), (yn, 2)) if core == 0 else ((yn, 2), (xn, 1))

            def copy(k, j, block, to, own=False):
                return pltpu.make_async_remote_copy(
                    src_ref=ins[k] if own else slot(k, *block), dst_ref=slot(k, *block),
                    send_sem=send_sems.at[k, j], recv_sem=recv_sems.at[k, j],
                    device_id=to, device_id_type=MESH)

            local = [pltpu.make_async_copy(ins[k], slot(k, *me), local_sems.at[k]) for k in range(n_arr)]
            sent = []
            for k in range(n_arr):
                sent += [copy(k, 0, me, sibling, True), copy(k, 1, me, (*xn, core), True),
                         copy(k, 2, me, (*yn, core), True)]
            for cp in local + sent:
                cp.start()

            def pass_on(k, j_from, j_to, block, targets):
                copy(k, j_from, block, me).wait_recv()
                for j, to in zip(j_to, targets):
                    cp = copy(k, j, block, to)
                    cp.start()
                    sent.append(cp)

            for k in range(n_arr):
                pass_on(k, s_first, (3, 3 + s_first), (*first, core), ((*second, core), sibling))
            for k in range(n_arr):
                pass_on(k, s_second, (3 + s_second,), (*second, core), (sibling,))
            for k in range(n_arr):
                pass_on(k, 3, (6,), (*dg, core), (sibling,))
            for k in range(n_arr):
                copy(k, 0, sibling, me).wait_recv()
                for j, chip in ((4, xn), (5, yn), (6, dg)):
                    copy(k, j, (*chip, 1 - core), me).wait_recv()
            for cp in sent:
                cp.wait_send()
            for cp in local:
                cp.wait()

        for core in (0, 1):
            pl.when(c == core)(lambda core=core: routed(core))

    return pl.pallas_call(
        body, name=name,
        in_specs=[ANY] * n_arr, out_specs=[ANY] * n_arr,
        out_shape=[_exchange_shape(a, True) for a in arrays],
        scratch_shapes=_exchange_sems(n_arr),
    )(*arrays)


def _sibling_swap(blocks):
    def body(b_ref, o_ref, send_sems, recv_sems):
        x, y, c = _coords()
        copies = [pltpu.make_async_remote_copy(
            src_ref=b_ref.at[2 * q + (1 - c)], dst_ref=o_ref.at[q],
            send_sem=send_sems.at[q], recv_sem=recv_sems.at[q],
            device_id=(x, y, 1 - c), device_id_type=MESH) for q in range(N_CHIP)]
        for cp in copies:
            cp.start()
        for cp in copies:
            cp.wait_recv()
        for cp in copies:
            cp.wait_send()

    return pl.pallas_call(
        body, name="sibling_swap", in_specs=[ANY], out_specs=ANY,
        out_shape=jax.ShapeDtypeStruct((N_CHIP,) + blocks.shape[1:], blocks.dtype),
        scratch_shapes=[pltpu.SemaphoreType.DMA((N_CHIP,)), pltpu.SemaphoreType.DMA((N_CHIP,))],
    )(blocks)


def _pair_sum(blocks, theirs, core):
    _, r, c = blocks.shape

    def body(core_ref, a_ref, b_ref, o_ref):
        o_ref[...] = (a_ref[...].astype(F32) + b_ref[...].astype(F32)).astype(BF16)

    return pl.pallas_call(
        body, name="pair_sum",
        grid_spec=pltpu.PrefetchScalarGridSpec(
            num_scalar_prefetch=1, grid=(N_CHIP,),
            in_specs=[pl.BlockSpec((1, r, c), lambda q, core_ref: (2 * q + core_ref[0], 0, 0)),
                      pl.BlockSpec((1, r, c), lambda q, core_ref: (q, 0, 0))],
            out_specs=pl.BlockSpec((1, r, c), lambda q, core_ref: (q, 0, 0))),
        out_shape=jax.ShapeDtypeStruct((N_CHIP, r, c), BF16),
        compiler_params=_params(dimension_semantics=("arbitrary",)),
    )(core, blocks, theirs)


def _du_rms(dz, wt_pad, h_pad, dout, g, pair_blocks):
    tm = 272
    steps = T // tm

    def body(dz_ref, w_ref, h_ref, d_ref, g_ref, p_ref, dh_ref, dg_ref, got_ref,
             send_sems, recv_sems, local_sems):
        start, wait = _direct_exchange([p_ref], [got_ref], send_sems, recv_sems, local_sems,
                                       False, chips_only=True)
        i = pl.program_id(0)
        pl.when(i == 0)(start)

        @pl.when(i == 0)
        def _():
            dg_ref[...] = jnp.zeros_like(dg_ref)

        du = _dot(dz_ref[...], w_ref[...])
        h = h_ref[...]
        r = lax.rsqrt(jnp.mean(h * h, axis=-1, keepdims=True) + EPS)
        xn = h * r
        dg_ref[...] += jnp.sum(du * xn, axis=0, keepdims=True)
        dn = du * g_ref[...]
        dh_ref[...] = d_ref[...] + r * (dn - xn * jnp.mean(dn * xn, axis=-1, keepdims=True))

        pl.when(i == steps - 1)(wait)

    tile = pl.BlockSpec((tm, D_MODEL), lambda i: (i, 0))
    return pl.pallas_call(
        body, name="du_rms", grid=(steps,),
        in_specs=[pl.BlockSpec((tm, D_IN_PAD), lambda i: (i, 0)),
                  pl.BlockSpec((D_IN_PAD, D_MODEL), lambda i: (0, 0)),
                  tile, tile, pl.BlockSpec((1, D_MODEL), lambda i: (0, 0)), ANY],
        out_specs=[tile, pl.BlockSpec((1, D_MODEL), lambda i: (0, 0)), ANY],
        out_shape=[jax.ShapeDtypeStruct((T, D_MODEL), F32),
                   jax.ShapeDtypeStruct((1, D_MODEL), F32),
                   jax.ShapeDtypeStruct(pair_blocks.shape, pair_blocks.dtype)],
        scratch_shapes=_exchange_sems(1, N_CHIP - 1),
        compiler_params=_params(dimension_semantics=("arbitrary",)),
    )(dz, wt_pad, h_pad, dout, g, pair_blocks)


def _tri(lower):
    r = lax.broadcasted_iota(jnp.int32, (CHUNK, CHUNK), 0)
    c = lax.broadcasted_iota(jnp.int32, (CHUNK, CHUNK), 1)
    return jnp.where((r >= c) if lower else (r <= c), 1.0, 0.0).astype(F32)


def _row_valid(n):
    r = lax.broadcasted_iota(jnp.int32, (CHUNK, 128), 0) + n * CHUNK
    return r >= PAD


_FF_SPEC = pl.BlockSpec((T, 128), lambda i: (0, FF_BASE // 128))


def _forget_fwd(z, b_pad):
    def body(z_ref, b_ref, o_ref):
        tri = _tri(True)
        carry = jnp.zeros((1, 128), F32)
        for n in range(NCHUNK):
            rows = pl.ds(n * CHUNK, CHUNK)
            a = z_ref[rows, :] + b_ref[...]
            lf = -(jnp.maximum(-a, 0.0) + jnp.log(1.0 + jnp.exp(-jnp.abs(a))))
            lf = jnp.where(_row_valid(n), lf, 0.0)
            c = jnp.dot(tri, lf, precision=lax.Precision.HIGHEST,
                        preferred_element_type=F32) + carry
            carry = c[CHUNK - 1:CHUNK, :]
            o_ref[:, rows] = jnp.where(_row_valid(n), c * (-LOG2E), NEG_INF).T

    return pl.pallas_call(
        body, name="forget_fwd", grid=(1,),
        in_specs=[_FF_SPEC, pl.BlockSpec((1, 128), lambda i: (0, 0))],
        out_specs=pl.BlockSpec((128, T), lambda i: (0, 0)),
        out_shape=jax.ShapeDtypeStruct((128, T), F32),
        compiler_params=_params(dimension_semantics=("arbitrary",)),
    )(z, b_pad)


def _forget_bwd(z, b_pad, dc, dz):
    def body(z_ref, b_ref, dc_ref, dz_in, dff_ref, db_ref):
        tri = _tri(False)
        carry = jnp.zeros((1, 128), F32)
        db = jnp.zeros((1, 128), F32)
        for n in reversed(range(NCHUNK)):
            rows = pl.ds(n * CHUNK, CHUNK)
            dc_blk = jnp.concatenate([dc_ref[:, rows], jnp.zeros((128 - FOX_HEADS, CHUNK), F32)], axis=0).T
            dlf = jnp.dot(tri, dc_blk, precision=lax.Precision.HIGHEST,
                          preferred_element_type=F32) + carry
            carry = dlf[0:1, :]
            a = z_ref[rows, :] + b_ref[...]
            dff = jnp.where(_row_valid(n), dlf * jax.nn.sigmoid(-a), 0.0)
            dff_ref[rows, :] = dff.astype(BF16)
            db = db + jnp.sum(dff, axis=0, keepdims=True)
        db_ref[...] = db

    return pl.pallas_call(
        body, name="forget_bwd", grid=(1,),
        in_specs=[_FF_SPEC, pl.BlockSpec((1, 128), lambda i: (0, 0)),
                  pl.BlockSpec((FOX_HEADS, T), lambda i: (0, 0)), ANY],
        out_specs=[_FF_SPEC, pl.BlockSpec((1, 128), lambda i: (0, 0))],
        out_shape=[jax.ShapeDtypeStruct((T, D_IN_PAD), BF16),
                   jax.ShapeDtypeStruct((1, 128), F32)],
        input_output_aliases={3: 0},
        compiler_params=_params(dimension_semantics=("arbitrary",)),
    )(z, b_pad, dc, dz)


FOX_QB = 512
FOX_NQB = SEQ // FOX_QB


def _fox_block(b):
    lo = CHUNK + b * FOX_QB
    return pl.ds(lo, FOX_QB), lo, lo + FOX_QB


def _causal_bias():
    r = lax.broadcasted_iota(jnp.int32, (FOX_QB, FOX_QB), 0)
    c = lax.broadcasted_iota(jnp.int32, (FOX_QB, FOX_QB), 1)
    return jnp.where(c <= r, 0.0, NEG_INF).astype(F32)


def _fox_logits(q_blk, k_all, bias, causal, b):
    _, lo, hi = _fox_block(b)
    s_off = _dot_nt(q_blk, k_all[:lo]) + bias[:, :lo]
    s_dia = _dot_nt(q_blk, k_all[lo:hi]) + (bias[:, lo:hi] + causal)
    return s_off, s_dia


_FOX_Z_SPEC = pl.BlockSpec((T, FOX_W), lambda p: (0, FOX_BASE // FOX_W + p))
_FOX_BIAS_SPEC = pl.BlockSpec((2, 1, T), lambda p: (p, 0, 0))
_FOX_LSE_SPEC = pl.BlockSpec((2, T, 1), lambda p: (p, 0, 0))
_FOX_SCALE = FOX_D ** -0.5
_FOX_QSCALE = _FOX_SCALE * LOG2E


def _fox_fwd(z, bias, y, w_out_blk):
    last = FOX_PAIRS - 1

    def body(z_ref, b_ref, y_in, w_ref, a_ref, lse_ref, y_ref, wall_ref,
             send_sems, recv_sems, local_sems):
        start, wait = _direct_exchange([w_ref], [wall_ref], send_sems, recv_sems, local_sems, True)
        pl.when(pl.program_id(0) == 0)(start)

        causal = _causal_bias()
        a_ref[pl.ds(0, CHUNK), :] = jnp.zeros((CHUNK, 128), F32)
        y_ref[pl.ds(0, CHUNK), :] = jnp.zeros((CHUNK, 128), BF16)
        for j in range(2):
            lanes = pl.ds(j * FOX_D, FOX_D)
            k_all = z_ref[:, pl.ds(128 + j * FOX_D, FOX_D)].astype(BF16)
            v_all = z_ref[:, pl.ds(256 + j * FOX_D, FOX_D)].astype(BF16)
            bias = b_ref[j]
            lse_ref[j, pl.ds(0, CHUNK), :] = jnp.zeros((CHUNK, 1), F32)
            for b in range(FOX_NQB):
                rows, lo, hi = _fox_block(b)
                q_blk = (z_ref[rows, lanes] * _FOX_QSCALE).astype(BF16)
                s_off, s_dia = _fox_logits(q_blk, k_all, bias, causal, b)
                m = jnp.maximum(jnp.max(s_off, axis=-1, keepdims=True),
                                jnp.max(s_dia, axis=-1, keepdims=True))
                e_off = jnp.exp2(s_off - m)
                e_dia = jnp.exp2(s_dia - m)
                total = jnp.sum(e_off, axis=-1, keepdims=True) + jnp.sum(e_dia, axis=-1, keepdims=True)
                o = (_dot(e_off.astype(BF16), v_all[:lo]) + _dot(e_dia.astype(BF16), v_all[lo:hi])) / total
                a_ref[rows, lanes] = o
                lse_ref[j, rows, :] = m + jnp.log(total) * LOG2E
                gate = _silu_parts(z_ref[rows, pl.ds(384 + j * FOX_D, FOX_D)])[0]
                y_ref[rows, lanes] = (o * gate).astype(BF16)

        pl.when(pl.program_id(0) == last)(wait)

    return pl.pallas_call(
        body, name="fox_fwd", grid=(FOX_PAIRS,),
        in_specs=[_FOX_Z_SPEC, _FOX_BIAS_SPEC, ANY, ANY],
        out_specs=[pl.BlockSpec((T, 128), lambda p: (0, p)), _FOX_LSE_SPEC,
                   pl.BlockSpec((T, 128), lambda p: (0, 8 + p)), ANY],
        out_shape=[jax.ShapeDtypeStruct((T, FOX_HEADS * FOX_D), F32),
                   jax.ShapeDtypeStruct((FOX_HEADS, T, 1), F32),
                   jax.ShapeDtypeStruct((T, D_MIX), BF16),
                   _exchange_shape(w_out_blk, True)],
        input_output_aliases={2: 2},
        scratch_shapes=_exchange_sems(1),
        compiler_params=_params(dimension_semantics=("arbitrary",)),
    )(z, bias, y, w_out_blk)


def _fox_bwd(z, bias, a_f, lse, dy, dz, dwo_blocks):
    last = FOX_PAIRS - 1

    def body(z_ref, b_ref, a_ref, lse_ref, dy_ref, dz_in, dwo_ref, dz_ref, dc_ref, got_ref,
             kv_acc, dc_acc, send_sems, recv_sems, local_sems):
        start, wait = _direct_exchange([dwo_ref], [got_ref], send_sems, recv_sems, local_sems, False)
        pl.when(pl.program_id(0) == 0)(start)

        causal = _causal_bias()
        dz_ref[pl.ds(0, CHUNK), pl.ds(0, 128)] = jnp.zeros((CHUNK, 128), BF16)
        dz_ref[pl.ds(0, CHUNK), pl.ds(384, 128)] = jnp.zeros((CHUNK, 128), BF16)
        dk_rows, dv_rows = pl.ds(0, FOX_D), pl.ds(FOX_D, FOX_D)
        for j in range(2):
            lanes = pl.ds(j * FOX_D, FOX_D)
            k_all = z_ref[:, pl.ds(128 + j * FOX_D, FOX_D)].astype(BF16)
            v_all = z_ref[:, pl.ds(256 + j * FOX_D, FOX_D)].astype(BF16)
            bias = b_ref[j]
            kv_acc[...] = jnp.zeros_like(kv_acc)
            dc_acc[...] = jnp.zeros_like(dc_acc)
            for b in range(FOX_NQB):
                rows, lo, hi = _fox_block(b)
                off, dia = pl.ds(0, lo), pl.ds(lo, FOX_QB)
                q_blk = (z_ref[rows, lanes] * _FOX_QSCALE).astype(BF16)
                s_off, s_dia = _fox_logits(q_blk, k_all, bias, causal, b)
                lse_blk = lse_ref[j, rows, :]
                p_off, p_dia = jnp.exp2(s_off - lse_blk), jnp.exp2(s_dia - lse_blk)
                sg, dsg = _silu_parts(z_ref[rows, pl.ds(384 + j * FOX_D, FOX_D)])
                dyj = dy_ref[rows, lanes]
                dz_ref[rows, pl.ds(384 + j * FOX_D, FOX_D)] = (dyj * a_ref[rows, lanes] * dsg).astype(BF16)
                do_b = (dyj * sg).astype(BF16)
                dp_off = _dot_nt(do_b, v_all[:lo])
                dp_dia = _dot_nt(do_b, v_all[lo:hi])
                d = (jnp.sum(p_off * dp_off, axis=-1, keepdims=True)
                     + jnp.sum(p_dia * dp_dia, axis=-1, keepdims=True))
                ds_off = p_off * (dp_off - d)
                ds_dia = p_dia * (dp_dia - d)
                dc_acc[:, off] -= jnp.sum(ds_off, axis=0, keepdims=True)
                dc_acc[:, dia] -= jnp.sum(ds_dia, axis=0, keepdims=True)
                ds_off_b, ds_dia_b = ds_off.astype(BF16), ds_dia.astype(BF16)
                dq = _dot(ds_off_b, k_all[:lo]) + _dot(ds_dia_b, k_all[lo:hi])
                dz_ref[rows, lanes] = (dq * _FOX_SCALE).astype(BF16)
                kv_acc[dk_rows, off] += _dot_tn(q_blk, ds_off_b)
                kv_acc[dk_rows, dia] += _dot_tn(q_blk, ds_dia_b)
                kv_acc[dv_rows, off] += _dot_tn(do_b, p_off.astype(BF16))
                kv_acc[dv_rows, dia] += _dot_tn(do_b, p_dia.astype(BF16))
            for n in range(NCHUNK):
                rows = pl.ds(n * CHUNK, CHUNK)
                both = kv_acc[:, rows].T
                dz_ref[rows, pl.ds(128 + j * FOX_D, FOX_D)] = (both[:, :FOX_D] * LN2).astype(BF16)
                dz_ref[rows, pl.ds(256 + j * FOX_D, FOX_D)] = both[:, FOX_D:].astype(BF16)
            dc_ref[j] = dc_acc[...]

        pl.when(pl.program_id(0) == last)(wait)

    col = lambda base: pl.BlockSpec((T, 128), lambda p: (0, base + p))
    return pl.pallas_call(
        body, name="fox_bwd", grid=(FOX_PAIRS,),
        in_specs=[_FOX_Z_SPEC, _FOX_BIAS_SPEC, col(0), _FOX_LSE_SPEC, col(8), ANY, ANY],
        out_specs=[_FOX_Z_SPEC, _FOX_BIAS_SPEC, ANY],
        out_shape=[jax.ShapeDtypeStruct((T, D_IN_PAD), BF16),
                   jax.ShapeDtypeStruct((FOX_HEADS, 1, T), F32),
                   _exchange_shape(dwo_blocks, False)],
        input_output_aliases={5: 0},
        scratch_shapes=[pltpu.VMEM((2 * FOX_D, T), F32), pltpu.VMEM((1, T), F32)] + _exchange_sems(1),
        compiler_params=_params(dimension_semantics=("arbitrary",)),
    )(z, bias, a_f, lse, dy, dz, dwo_blocks)


def _rot(x, cosf, sins):
    return x * cosf + pltpu.roll(x, RET_DK // 2, 1) * sins


def _rot_t(d, cosf, sins):
    return d * cosf - pltpu.roll(d, RET_DK // 2, 1) * sins


_RET_Z_SPEC = pl.BlockSpec((T, RET_W), lambda h: (0, h))
_RET_TABLE_SPECS = [
    pl.BlockSpec((T, RET_DK), lambda h: (0, 0)),
    pl.BlockSpec((T, RET_DK), lambda h: (0, 0)),
    pl.BlockSpec((1, CHUNK, CHUNK), lambda h: (h, 0, 0)),
    pl.BlockSpec((1, CHUNK, 1), lambda h: (h, 0, 0)),
    pl.BlockSpec((1, CHUNK, 1), lambda h: (h, 0, 0)),
    pl.BlockSpec((1, 1, 1), lambda h: (h, 0, 0)),
]
_RQ, _RK = pl.ds(0, RET_DK), pl.ds(RET_DK, RET_DK)
_RV, _RG = pl.ds(2 * RET_DK, RET_DV), pl.ds(2 * RET_DK + RET_DV, RET_DV)
_RET_KSCALE = RET_DK ** -0.5


def _ret_fwd(z, tables):
    def body(z_ref, cos_ref, sin_ref, dm_ref, zeta_ref, xi_ref, cd_ref, raw_ref, y_ref):
        dmask, zeta, xi, cdec = dm_ref[0], zeta_ref[0], xi_ref[0], cd_ref[0]
        state = jnp.zeros((RET_DK, RET_DV), F32)
        for n in range(NCHUNK):
            rows = pl.ds(n * CHUNK, CHUNK)
            cosf, sins = cos_ref[rows, :], sin_ref[rows, :]
            qr = _rot(z_ref[rows, _RQ], cosf, sins)
            kr_b = (_rot(z_ref[rows, _RK], cosf, sins) * _RET_KSCALE).astype(BF16)
            v = z_ref[rows, _RV]
            a = _dot_nt(qr.astype(BF16), kr_b) * dmask
            out = _dot(a.astype(BF16), v.astype(BF16)) + _dot((qr * xi).astype(BF16), state.astype(BF16))
            state = state * cdec + _dot_tn(kr_b, (v * zeta).astype(BF16))
            raw_ref[rows, :] = out
            r = lax.rsqrt(jnp.mean(out * out, axis=-1, keepdims=True) + EPS)
            y_ref[rows, :] = (out * r * _silu_parts(z_ref[rows, _RG])[0]).astype(BF16)

    wide = pl.BlockSpec((T, RET_DV), lambda h: (0, h))
    return pl.pallas_call(
        body, name="ret_fwd", grid=(RET_HEADS,),
        in_specs=[_RET_Z_SPEC] + _RET_TABLE_SPECS,
        out_specs=[wide, wide],
        out_shape=[jax.ShapeDtypeStruct((T, RET_HEADS * RET_DV), F32),
                   jax.ShapeDtypeStruct((T, D_MIX), BF16)],
        compiler_params=_params(dimension_semantics=("arbitrary",)),
    )(z, *tables)


def _ret_bwd(z, tables, raw, dy):
    def body(z_ref, cos_ref, sin_ref, dm_ref, zeta_ref, xi_ref, cd_ref, raw_ref, dy_ref,
             dz_ref, st_ref):
        dmask, zeta, xi, cdec = dm_ref[0], zeta_ref[0], xi_ref[0], cd_ref[0]

        def rotated(n):
            rows = pl.ds(n * CHUNK, CHUNK)
            cosf, sins = cos_ref[rows, :], sin_ref[rows, :]
            qr = _rot(z_ref[rows, _RQ], cosf, sins)
            kr_b = (_rot(z_ref[rows, _RK], cosf, sins) * _RET_KSCALE).astype(BF16)
            return rows, cosf, sins, qr, kr_b

        state = jnp.zeros((RET_DK, RET_DV), F32)
        for n in range(NCHUNK):
            st_ref[n] = state.astype(BF16)
            if n + 1 < NCHUNK:
                rows, _, _, _, kr_b = rotated(n)
                state = state * cdec + _dot_tn(kr_b, (z_ref[rows, _RV] * zeta).astype(BF16))

        grad_state = jnp.zeros((RET_DK, RET_DV), F32)
        for n in reversed(range(NCHUNK)):
            rows, cosf, sins, qr, kr_b = rotated(n)
            qr_b = qr.astype(BF16)
            v_b = z_ref[rows, _RV].astype(BF16)
            gs_b = grad_state.astype(BF16)
            o = raw_ref[rows, :]
            r = lax.rsqrt(jnp.mean(o * o, axis=-1, keepdims=True) + EPS)
            hn = o * r
            sg, dsg = _silu_parts(z_ref[rows, _RG])
            dyn = dy_ref[rows, :]
            dz_ref[rows, _RG] = (dyn * hn * dsg).astype(BF16)
            dhn = dyn * sg
            do_b = (r * (dhn - hn * jnp.mean(dhn * hn, axis=-1, keepdims=True))).astype(BF16)
            a_b = (_dot_nt(qr_b, kr_b) * dmask).astype(BF16)
            da_b = (_dot_nt(do_b, v_b) * dmask).astype(BF16)
            dqr = _dot(da_b, kr_b) + xi * _dot_nt(do_b, st_ref[n])
            dkr = _dot_tn(da_b, qr_b) + zeta * _dot_nt(v_b, gs_b)
            dv = _dot_tn(a_b, do_b) + zeta * _dot(kr_b, gs_b)
            grad_state = grad_state * cdec + _dot_tn((qr * xi).astype(BF16), do_b)
            dz_ref[rows, _RQ] = _rot_t(dqr, cosf, sins).astype(BF16)
            dz_ref[rows, _RK] = (_rot_t(dkr, cosf, sins) * _RET_KSCALE).astype(BF16)
            dz_ref[rows, _RV] = dv.astype(BF16)

    wide = pl.BlockSpec((T, RET_DV), lambda h: (0, h))
    return pl.pallas_call(
        body, name="ret_bwd", grid=(RET_HEADS,),
        in_specs=[_RET_Z_SPEC] + _RET_TABLE_SPECS + [wide, wide],
        out_specs=_RET_Z_SPEC,
        out_shape=jax.ShapeDtypeStruct((T, D_IN_PAD), BF16),
        scratch_shapes=[pltpu.VMEM((NCHUNK, RET_DK, RET_DV), BF16)],
        compiler_params=_params(dimension_semantics=("arbitrary",)),
    )(z, *tables, raw, dy)


def _adamw(w, g, m, v):
    m = ADAM_B1 * m + (1.0 - ADAM_B1) * g
    v = ADAM_B2 * v + (1.0 - ADAM_B2) * (g * g)
    m_hat = m / (1.0 - ADAM_B1 ** ADAM_STEP)
    v_hat = v / (1.0 - ADAM_B2 ** ADAM_STEP)
    delta = -ADAM_LR * (m_hat / (jnp.sqrt(v_hat) + ADAM_EPS) + ADAM_WD * w)
    return delta, m, v


def _sum_adamw(parts, w, m, v, rows, name):
    _, r_tot, cols = parts.shape
    assert r_tot % rows == 0

    def body(p_ref, w_ref, m_ref, v_ref, g_ref, d_ref, nm_ref, nv_ref):
        g = p_ref[0].astype(F32)
        for d in range(1, N_DEV):
            g = g + p_ref[d].astype(F32)
        delta, nm, nv = _adamw(w_ref[...], g, m_ref[...], v_ref[...])
        g_ref[...] = g
        d_ref[...] = delta
        nm_ref[...] = nm
        nv_ref[...] = nv

    blk = pl.BlockSpec((rows, cols), lambda i: (i, 0))
    return pl.pallas_call(
        body, name=name, grid=(r_tot // rows,),
        in_specs=[pl.BlockSpec((N_DEV, rows, cols), lambda i: (0, i, 0)), blk, blk, blk],
        out_specs=[blk] * 4,
        out_shape=[jax.ShapeDtypeStruct((r_tot, cols), F32)] * 4,
        compiler_params=_params(dimension_semantics=("arbitrary",)),
    )(parts, w, m, v)


def _sum_adamw_w_in(parts, w, m, v, small):
    n_part, r, c = parts.shape
    steps = c // 128

    def body(p_ref, w_ref, m_ref, v_ref, s_ref, g_ref, d_ref, nm_ref, nv_ref, got_ref,
             send_sems, recv_sems, local_sems):
        start, wait = _direct_exchange([s_ref], [got_ref], send_sems, recv_sems, local_sems, True)
        pl.when(pl.program_id(0) == 0)(start)
        g = p_ref[0].astype(F32)
        for d in range(1, n_part):
            g = g + p_ref[d].astype(F32)
        delta, nm, nv = _adamw(w_ref[...], g, m_ref[...], v_ref[...])
        g_ref[...] = g
        d_ref[...] = delta
        nm_ref[...] = nm
        nv_ref[...] = nv
        pl.when(pl.program_id(0) == steps - 1)(wait)

    blk = pl.BlockSpec((r, 128), lambda i: (0, i))
    return pl.pallas_call(
        body, name="adamw_w_in", grid=(steps,),
        in_specs=[pl.BlockSpec((n_part, r, 128), lambda i: (0, 0, i)), blk, blk, blk, ANY],
        out_specs=[blk] * 4 + [ANY],
        out_shape=[jax.ShapeDtypeStruct((r, c), F32)] * 4 + [_exchange_shape(small, True)],
        scratch_shapes=_exchange_sems(1),
        compiler_params=_params(dimension_semantics=("arbitrary",)),
    )(parts, w, m, v, small)


def _adamw_small(got, me, metas, norms, finals, biases):
    def body(me_ref, gm_ref, gr_ref, *refs):
        ins, outs = refs[:12], refs[12:]
        g_meta, g_rest = gm_ref[0], gr_ref[0]
        for d in range(1, N_DEV):
            g_meta, g_rest = g_meta + gm_ref[d], g_rest + gr_ref[d]
        grads = [g_meta, g_rest[0:1], g_rest[1:2], g_rest[2:3, :FOX_HEADS]]
        for k, g in enumerate(grads):
            w_ref, m_ref, v_ref = ins[3 * k:3 * k + 3]
            delta, new_m, new_v = _adamw(w_ref[...], g, m_ref[...], v_ref[...])
            for o_ref, val in zip(outs[4 * k:4 * k + 4], (g, delta, new_m, new_v)):
                o_ref[...] = val
        outs[16][...] = g_rest[3:4, :128]

    groups = (metas, norms, finals, biases)
    full = lambda a: pl.BlockSpec(a.shape, lambda i, me_ref: (0,) * a.ndim)
    flat = [a for grp in groups for a in grp]
    res = pl.pallas_call(
        body, name="adamw_small",
        grid_spec=pltpu.PrefetchScalarGridSpec(
            num_scalar_prefetch=1, grid=(1,),
            in_specs=[pl.BlockSpec((N_DEV, N_META, META_BLK), lambda i, me_ref: (0, 0, me_ref[0])),
                      pl.BlockSpec((N_DEV, 8, D_MODEL), lambda i, me_ref: (0, N_META // 8, 0))]
            + [full(a) for a in flat],
            out_specs=[full(grp[0]) for grp in groups for _ in range(4)]
            + [pl.BlockSpec((1, 128), lambda i, me_ref: (0, 0))]),
        out_shape=[jax.ShapeDtypeStruct(grp[0].shape, F32) for grp in groups for _ in range(4)]
        + [jax.ShapeDtypeStruct((1, 128), F32)],
        compiler_params=_params(dimension_semantics=("arbitrary",)),
    )(me, got, got, *flat)
    return [res[4 * k:4 * k + 4] for k in range(4)], res[16]


def kernel(x, meta_tokens, norm_g, w_in, b_f, w_out, final_g, loss_target, m_meta_tokens, m_norm_g, m_w_in, m_b_f, m_w_out, m_final_g, v_meta_tokens, v_norm_g, v_w_in, v_b_f, v_w_out, v_final_g):
    core = lax.axis_index("c")
    me = 4 * lax.axis_index("x") + 2 * lax.axis_index("y") + core
    tables = _tables()

    wt_all, meta_all = _gather_two_level([_slab(w_in[0].T.astype(BF16), me), meta_tokens], name="gather_w_in")
    wt_pad = _regroup_slabs(wt_all)
    meta_full = jnp.transpose(meta_all, (1, 0, 2)).reshape(N_META, D_MODEL)
    h_pad = jnp.concatenate([jnp.zeros((PAD, D_MODEL), F32), meta_full, x[0]], axis=0)
    b_pad = jnp.pad(b_f, ((0, 0), (0, 128 - FOX_HEADS)))

    u, z = _rms_mm_z(h_pad, norm_g, wt_pad)
    bias = _forget_fwd(z, b_pad)[:FOX_HEADS].reshape(FOX_HEADS, 1, T)
    raw, y = _ret_fwd(z, tables)
    a_f, lse, y, w_out_all = _fox_fwd(z, bias, y, w_out[0].astype(BF16))
    w_out_b = w_out_all.reshape(D_MIX, D_MODEL)
    dout, dout_b, dy, loss_blk, d_final_g = _out_loss_dy(y, w_out_b, h_pad, loss_target[0],
                                                         final_g.reshape(1, D_MODEL))

    d_w_out = _mm_tn(y, dout_b, tm=D_MIX, tn=256, name="mm_dwout")
    dz = _ret_bwd(z, tables, raw, dy)
    dz, dc, got_w_out = _fox_bwd(z, bias, a_f, lse, dy, dz, d_w_out.reshape(N_DEV, WO_BLK, D_MODEL))
    dz, db_f = _forget_bwd(z, b_pad, dc.reshape(FOX_HEADS, T), dz)
    d_wt_pad = _mm_tn(dz, u, tm=384, tn=D_MODEL, name="mm_dwin")

    d_wt_slabs = _ungroup_slabs(d_wt_pad)
    theirs = _sibling_swap(d_wt_slabs)
    pair = _pair_sum(d_wt_slabs, theirs, core.astype(jnp.int32).reshape(1))
    dh, d_norm_g, got_slabs = _du_rms(dz, wt_pad, h_pad, dout, norm_g, pair)
    got_w_in = lax.dynamic_slice(got_slabs, (0, (W_BLK - W_STRIDE) * me, 0), (N_CHIP, W_BLK, D_MODEL))

    small = jnp.concatenate([
        dh[PAD:CHUNK], d_norm_g, d_final_g, jnp.pad(db_f[:, :FOX_HEADS], ((0, 0), (0, D_MODEL - FOX_HEADS))),
        jnp.pad(loss_blk[0:1], ((0, 0), (0, D_MODEL - 128))),
        jnp.zeros((SMALL_ROWS - N_META - 4, D_MODEL), F32)], axis=0)
    g_w_in, d_w_in, nm_w_in, nv_w_in, got_small = _sum_adamw_w_in(
        got_w_in, w_in[0].T, m_w_in[0].T, v_w_in[0].T, small)
    g_w_out, d_w_out, nm_w_out, nv_w_out = _sum_adamw(got_w_out, w_out[0], m_w_out[0], v_w_out[0], 128, "adamw_w_out")

    row = lambda a: a.reshape(1, D_MODEL)
    (meta_o, norm_o, final_o, bias_o), loss_row = _adamw_small(
        got_small, me.astype(jnp.int32).reshape(1),
        (meta_tokens, m_meta_tokens, v_meta_tokens), (norm_g, m_norm_g, v_norm_g),
        (row(final_g), row(m_final_g), row(v_final_g)), (b_f, m_b_f, v_b_f))
    final_o = [a.reshape(D_MODEL) for a in final_o]

    back = lambda a: a.T[None]
    outs = [[meta_o[k], norm_o[k], back(wk), bias_o[k], ok[None], final_o[k]]
            for k, (wk, ok) in enumerate(zip((g_w_in, d_w_in, nm_w_in, nv_w_in),
                                             (g_w_out, d_w_out, nm_w_out, nv_w_out)))]
    return (loss_row[0, 0], dh[CHUNK:][None], *outs[0], *outs[1], *outs[2], *outs[3])
```

```python
import numpy as np
import jax
import jax.numpy as jnp
from jax import lax
from jax.experimental import pallas as pl
from jax.experimental.pallas import tpu as pltpu

F32 = jnp.float32
BF16 = jnp.bfloat16

N_DEV = 8
N_CHIP = 4
D_MODEL = 1024
SEQ = 2048
N_META = 16
CHUNK = 128
PAD = CHUNK - N_META
T = SEQ + CHUNK
NCHUNK = T // CHUNK
D_MIX = 2048
RET_HEADS = 4
RET_DK = 128
RET_DV = 256
RET_W = 2 * RET_DK + 2 * RET_DV
FOX_HEADS = 16
FOX_D = 64
FOX_PAIRS = FOX_HEADS // 2
FOX_W = 4 * 128
FOX_BASE = RET_HEADS * RET_W
FF_BASE = FOX_BASE + FOX_PAIRS * FOX_W
D_IN = 7184
D_IN_PAD = 7296
W_BLK = D_IN // N_DEV
WO_BLK = D_MIX // N_DEV
META_BLK = D_MODEL // N_DEV
EPS = 1e-6
NEG_INF = -1e30
ROPE_BASE = 10000.0
LOG2E = 1.4426950408889634
LN2 = 0.6931471805599453

ADAM_LR = 0.001
ADAM_B1 = 0.9
ADAM_B2 = 0.999
ADAM_EPS = 1e-08
ADAM_WD = 0.01
ADAM_STEP = 10

SMALL_ROWS = 24
VMEM_LIMIT = 56 * 1024 * 1024
MESH = pl.DeviceIdType.MESH
ANY = pl.BlockSpec(memory_space=pl.ANY)

_NT = (((1,), (1,)), ((), ()))
_TN = (((0,), (0,)), ((), ()))


def _dot(a, b):
    return jnp.dot(a, b, preferred_element_type=F32)


def _dot_nt(a, b):
    return lax.dot_general(a, b, _NT, preferred_element_type=F32)


def _dot_tn(a, b):
    return lax.dot_general(a, b, _TN, preferred_element_type=F32)


def _params(**kw):
    return pltpu.CompilerParams(vmem_limit_bytes=VMEM_LIMIT, **kw)


def _silu_parts(g):
    sig = jax.nn.sigmoid(g)
    return g * sig, sig * (1.0 + g * (1.0 - sig))


def _tables():
    pos = np.arange(T, dtype=np.float32) - PAD
    inv = (ROPE_BASE ** (-np.arange(0, RET_DK, 2, dtype=np.float32) / RET_DK)).astype(np.float32)
    ang = pos[:, None] * inv[None, :]
    cos, sin = np.cos(ang), np.sin(ang)
    cosf = np.concatenate([cos, cos], axis=1).astype(np.float32)
    sins = np.concatenate([-sin, sin], axis=1).astype(np.float32)
    h = np.arange(RET_HEADS, dtype=np.float32)
    log_gamma = np.log1p(-np.exp2(-5.0 - h)).astype(np.float32)
    idx = np.arange(CHUNK, dtype=np.float32)
    diff = idx[:, None] - idx[None, :]
    dmask = np.where(diff[None] >= 0,
                     np.exp(log_gamma[:, None, None] * np.maximum(diff, 0.0)[None]), 0.0)
    zeta = np.exp(log_gamma[:, None] * (CHUNK - 1.0 - idx)[None, :])
    xi = np.exp(log_gamma[:, None] * (idx + 1.0)[None, :])
    cdec = np.exp(log_gamma * CHUNK)
    return (jnp.asarray(cosf), jnp.asarray(sins), jnp.asarray(dmask, F32),
            jnp.asarray(zeta[:, :, None], F32), jnp.asarray(xi[:, :, None], F32),
            jnp.asarray(cdec[:, None, None], F32))


W_STRIDE = 896
W_SLAB = 912
W_EDGE = W_SLAB - W_STRIDE
def _slab(block, me):
    return lax.dynamic_update_slice(jnp.zeros((W_SLAB, block.shape[1]), block.dtype), block,
                                    ((W_BLK - W_STRIDE) * me, 0))


def _regroup_slabs(slabs):
    c = slabs.shape[2]
    last = slabs[:, W_STRIDE:]
    mine = slabs[:, :W_EDGE] + jnp.concatenate([jnp.zeros_like(last[:1]), last[:-1]], axis=0)
    w = jnp.concatenate([mine, slabs[:, W_EDGE:W_STRIDE]], axis=1).reshape(FF_BASE, c)
    ret = jnp.concatenate([w[0:512].reshape(4, 128, c), w[512:1024].reshape(4, 128, c),
                           w[1024:2048].reshape(4, 256, c), w[2048:3072].reshape(4, 256, c)],
                          axis=1).reshape(FOX_BASE, c)
    fox = jnp.concatenate([w[3072 + 1024 * i:4096 + 1024 * i].reshape(8, 128, c) for i in range(4)],
                          axis=1).reshape(FOX_PAIRS * FOX_W, c)
    ff = jnp.pad(last[N_DEV - 1], ((0, D_IN_PAD - FF_BASE - W_EDGE), (0, 0)))
    return jnp.concatenate([ret, fox, ff], axis=0)


def _ungroup_slabs(w):
    c = w.shape[1]
    ret = w[:FOX_BASE].reshape(4, RET_W, c)
    fox = w[FOX_BASE:FF_BASE].reshape(8, FOX_W, c)
    main = jnp.concatenate(
        [ret[:, 0:128].reshape(512, c), ret[:, 128:256].reshape(512, c),
         ret[:, 256:512].reshape(1024, c), ret[:, 512:768].reshape(1024, c)]
        + [fox[:, 128 * i:128 * i + 128].reshape(1024, c) for i in range(4)], axis=0
    ).reshape(N_DEV, W_STRIDE, c)
    nxt = jnp.concatenate([main[1:, :W_EDGE], w[None, FF_BASE:FF_BASE + W_EDGE]], axis=0)
    return jnp.concatenate([main, nxt], axis=1)


def _mm_tn(a, b, *, tm, tn, name):
    k, m = a.shape
    n = b.shape[1]
    assert m % tm == 0 and n % tn == 0

    def body(a_ref, b_ref, o_ref):
        o_ref[...] = _dot_tn(a_ref[...], b_ref[...]).astype(BF16)

    return pl.pallas_call(
        body, name=name, grid=(n // tn, m // tm),
        in_specs=[pl.BlockSpec((k, tm), lambda j, i: (0, i)),
                  pl.BlockSpec((k, tn), lambda j, i: (0, j))],
        out_specs=pl.BlockSpec((tm, tn), lambda j, i: (i, j)),
        out_shape=jax.ShapeDtypeStruct((m, n), BF16),
        compiler_params=_params(dimension_semantics=("arbitrary", "arbitrary")),
    )(a, b)


def _rms_mm_z(h_pad, g, wt_pad):
    tm, tn = T // 2, D_IN_PAD // 3

    def body(h_ref, g_ref, w_ref, u_ref, z_ref):
        h = h_ref[...]
        r = lax.rsqrt(jnp.mean(h * h, axis=-1, keepdims=True) + EPS)
        u = (h * r * g_ref[...]).astype(BF16)
        u_ref[...] = u
        z_ref[...] = _dot_nt(u, w_ref[...])

    return pl.pallas_call(
        body, name="rms_mm_z", grid=(T // tm, D_IN_PAD // tn),
        in_specs=[pl.BlockSpec((tm, D_MODEL), lambda i, j: (i, 0)),
                  pl.BlockSpec((1, D_MODEL), lambda i, j: (0, 0)),
                  pl.BlockSpec((tn, D_MODEL), lambda i, j: (j, 0))],
        out_specs=[pl.BlockSpec((tm, D_MODEL), lambda i, j: (i, 0)),
                   pl.BlockSpec((tm, tn), lambda i, j: (i, j))],
        out_shape=[jax.ShapeDtypeStruct((T, D_MODEL), BF16),
                   jax.ShapeDtypeStruct((T, D_IN_PAD), F32)],
        compiler_params=_params(dimension_semantics=("arbitrary", "arbitrary")),
    )(h_pad, g, wt_pad)


def _out_loss_dy(y, w_out_b, h_pad, target, g):
    tm = T // 4

    def body(y_ref, w_ref, h_ref, t_hbm, g_ref, d_ref, db_ref, dy_ref, loss_ref, dg_ref, t_buf, t_sem):
        i = pl.program_id(0)
        head = pltpu.make_async_copy(t_hbm.at[pl.ds(0, tm - CHUNK)], t_buf.at[pl.ds(CHUNK, tm - CHUNK)], t_sem)
        rest = pltpu.make_async_copy(t_hbm.at[pl.ds(pl.multiple_of(jnp.maximum(i, 1) * tm - CHUNK, 8), tm)],
                                     t_buf, t_sem)

        @pl.when(i == 0)
        def _():
            t_buf[pl.ds(0, CHUNK), :] = jnp.zeros((CHUNK, D_MODEL), F32)
            head.start()
            loss_ref[...] = jnp.zeros_like(loss_ref)
            dg_ref[...] = jnp.zeros_like(dg_ref)

        pl.when(i > 0)(rest.start)

        w = w_ref[...]
        o = _dot(y_ref[...], w) + h_ref[...]
        pl.when(i == 0)(head.wait)
        pl.when(i > 0)(rest.wait)
        token = lax.broadcasted_iota(jnp.int32, (tm, 1), 0) + i * tm >= CHUNK
        g = g_ref[...]
        r = lax.rsqrt(jnp.mean(o * o, axis=-1, keepdims=True) + EPS)
        xn = o * r
        e = jnp.where(token, xn * g - t_buf[...], 0.0)
        loss_ref[...] += jnp.full(loss_ref.shape, 0.5 / D_MODEL * jnp.sum(e * e), F32)
        do = e * (1.0 / D_MODEL)
        dg_ref[...] += jnp.sum(do * xn, axis=0, keepdims=True)
        dn = do * g
        d = r * (dn - xn * jnp.mean(dn * xn, axis=-1, keepdims=True))
        d_b = d.astype(BF16)
        d_ref[...] = d
        db_ref[...] = d_b
        dy_ref[...] = _dot_nt(d_b, w)

    tile = pl.BlockSpec((tm, D_MODEL), lambda i: (i, 0))
    wide = pl.BlockSpec((tm, D_MIX), lambda i: (i, 0))
    return pl.pallas_call(
        body, name="out_loss_dy", grid=(T // tm,),
        in_specs=[wide, pl.BlockSpec((D_MIX, D_MODEL), lambda i: (0, 0)), tile, ANY,
                  pl.BlockSpec((1, D_MODEL), lambda i: (0, 0))],
        out_specs=[tile, tile, wide,
                   pl.BlockSpec((8, 128), lambda i: (0, 0)),
                   pl.BlockSpec((1, D_MODEL), lambda i: (0, 0))],
        out_shape=[jax.ShapeDtypeStruct((T, D_MODEL), F32),
                   jax.ShapeDtypeStruct((T, D_MODEL), BF16),
                   jax.ShapeDtypeStruct((T, D_MIX), F32),
                   jax.ShapeDtypeStruct((8, 128), F32),
                   jax.ShapeDtypeStruct((1, D_MODEL), F32)],
        scratch_shapes=[pltpu.VMEM((tm, D_MODEL), F32), pltpu.SemaphoreType.DMA],
        compiler_params=_params(dimension_semantics=("arbitrary",)),
    )(y, w_out_b, h_pad, target, g)


def _coords():
    return lax.axis_index("x"), lax.axis_index("y"), lax.axis_index("c")


def _flip(v, bit):
    return 1 - v if bit else v


def _peer(x, y, c, r):
    return _flip(x, (r >> 2) & 1), _flip(y, (r >> 1) & 1), _flip(c, r & 1)


def _direct_exchange(ins, outs, send_sems, recv_sems, local_sems, gather, chips_only=False):
    x, y, c = _coords()
    me = 2 * x + y if chips_only else 4 * x + 2 * y + c

    def src(k, to_idx):
        return ins[k] if gather else ins[k].at[to_idx]

    local = [pltpu.make_async_copy(src(k, me), outs[k].at[me], local_sems.at[k])
             for k in range(len(ins))]
    sends, recvs = [], []
    for r in range(1, N_CHIP if chips_only else N_DEV):
        px, py, pc = _peer(x, y, c, 2 * r if chips_only else r)
        peer = 2 * px + py if chips_only else 4 * px + 2 * py + pc
        for k in range(len(ins)):
            sems = dict(send_sem=send_sems.at[k, r - 1], recv_sem=recv_sems.at[k, r - 1],
                        device_id=(px, py, pc), device_id_type=MESH)
            sends.append(pltpu.make_async_remote_copy(src_ref=src(k, peer), dst_ref=outs[k].at[me], **sems))
            recvs.append(pltpu.make_async_remote_copy(src_ref=src(k, peer), dst_ref=outs[k].at[peer], **sems))

    def start():
        for cp in local + sends:
            cp.start()

    def wait():
        for cp in recvs:
            cp.wait_recv()
        for cp in sends:
            cp.wait_send()
        for cp in local:
            cp.wait()

    return start, wait


def _exchange_sems(n_arr, n_peer=N_DEV - 1):
    return [pltpu.SemaphoreType.DMA((n_arr, n_peer)), pltpu.SemaphoreType.DMA((n_arr, n_peer)),
            pltpu.SemaphoreType.DMA((n_arr,))]


def _exchange_shape(a, gather):
    return jax.ShapeDtypeStruct(((N_DEV,) + a.shape) if gather else a.shape, a.dtype)


def _gather_two_level(arrays, name):
    n_arr = len(arrays)

    def body(*refs):
        ins, outs = refs[:n_arr], refs[n_arr:2 * n_arr]
        send_sems, recv_sems, local_sems = refs[2 * n_arr:]
        x, y, c = _coords()

        def slot(k, px, py, pc):
            return outs[k].at[4 * px + 2 * py + pc]

        def routed(core):
            me, sibling = (x, y, core), (x, y, 1 - core)
            xn, yn, dg = (1 - x, y), (x, 1 - y), (1 - x, 1 - y)
            (first, s_first), (second, s_second) = ((xn, 1), (yn, 2)) if core == 0 else ((yn, 2), (xn, 1))

            def copy(k, j, block, to, own=False):
                return pltpu.make_async_remote_copy(
                    src_ref=ins[k] if own else slot(k, *block), dst_ref=slot(k, *block),
                    send_sem=send_sems.at[k, j], recv_sem=recv_sems.at[k, j],
                    device_id=to, device_id_type=MESH)

            local = [pltpu.make_async_copy(ins[k], slot(k, *me), local_sems.at[k]) for k in range(n_arr)]
            sent = []
            for k in range(n_arr):
                sent += [copy(k, 0, me, sibling, True), copy(k, 1, me, (*xn, core), True),
                         copy(k, 2, me, (*yn, core), True)]
            for cp in local + sent:
                cp.start()

            def pass_on(k, j_from, j_to, block, targets):
                copy(k, j_from, block, me).wait_recv()
                for j, to in zip(j_to, targets):
                    cp = copy(k, j, block, to)
                    cp.start()
                    sent.append(cp)

            for k in range(n_arr):
                pass_on(k, s_first, (3, 3 + s_first), (*first, core), ((*second, core), sibling))
            for k in range(n_arr):
                pass_on(k, s_second, (3 + s_second,), (*second, core), (sibling,))
            for k in range(n_arr):
                pass_on(k, 3, (6,), (*dg, core), (sibling,))
            for k in range(n_arr):
                copy(k, 0, sibling, me).wait_recv()
                for j, chip in ((4, xn), (5, yn), (6, dg)):
                    copy(k, j, (*chip, 1 - core), me).wait_recv()
            for cp in sent:
                cp.wait_send()
            for cp in local:
                cp.wait()

        for core in (0, 1):
            pl.when(c == core)(lambda core=core: routed(core))

    return pl.pallas_call(
        body, name=name,
        in_specs=[ANY] * n_arr, out_specs=[ANY] * n_arr,
        out_shape=[_exchange_shape(a, True) for a in arrays],
        scratch_shapes=_exchange_sems(n_arr),
    )(*arrays)


def _sibling_swap(blocks):
    def body(b_ref, o_ref, send_sems, recv_sems):
        x, y, c = _coords()
        copies = [pltpu.make_async_remote_copy(
            src_ref=b_ref.at[2 * q + (1 - c)], dst_ref=o_ref.at[q],
            send_sem=send_sems.at[q], recv_sem=recv_sems.at[q],
            device_id=(x, y, 1 - c), device_id_type=MESH) for q in range(N_CHIP)]
        for cp in copies:
            cp.start()
        for cp in copies:
            cp.wait_recv()
        for cp in copies:
            cp.wait_send()

    return pl.pallas_call(
        body, name="sibling_swap", in_specs=[ANY], out_specs=ANY,
        out_shape=jax.ShapeDtypeStruct((N_CHIP,) + blocks.shape[1:], blocks.dtype),
        scratch_shapes=[pltpu.SemaphoreType.DMA((N_CHIP,)), pltpu.SemaphoreType.DMA((N_CHIP,))],
    )(blocks)


def _pair_sum(blocks, theirs, core):
    _, r, c = blocks.shape

    def body(core_ref, a_ref, b_ref, o_ref):
        o_ref[...] = (a_ref[...].astype(F32) + b_ref[...].astype(F32)).astype(BF16)

    return pl.pallas_call(
        body, name="pair_sum",
        grid_spec=pltpu.PrefetchScalarGridSpec(
            num_scalar_prefetch=1, grid=(N_CHIP,),
            in_specs=[pl.BlockSpec((1, r, c), lambda q, core_ref: (2 * q + core_ref[0], 0, 0)),
                      pl.BlockSpec((1, r, c), lambda q, core_ref: (q, 0, 0))],
            out_specs=pl.BlockSpec((1, r, c), lambda q, core_ref: (q, 0, 0))),
        out_shape=jax.ShapeDtypeStruct((N_CHIP, r, c), BF16),
        compiler_params=_params(dimension_semantics=("arbitrary",)),
    )(core, blocks, theirs)


def _du_rms(dz, wt_pad, h_pad, dout, g, pair_blocks):
    tm = 272
    steps = T // tm

    def body(dz_ref, w_ref, h_ref, d_ref, g_ref, p_ref, dh_ref, dg_ref, got_ref,
             send_sems, recv_sems, local_sems):
        start, wait = _direct_exchange([p_ref], [got_ref], send_sems, recv_sems, local_sems,
                                       False, chips_only=True)
        i = pl.program_id(0)
        pl.when(i == 0)(start)

        @pl.when(i == 0)
        def _():
            dg_ref[...] = jnp.zeros_like(dg_ref)

        du = _dot(dz_ref[...], w_ref[...])
        h = h_ref[...]
        r = lax.rsqrt(jnp.mean(h * h, axis=-1, keepdims=True) + EPS)
        xn = h * r
        dg_ref[...] += jnp.sum(du * xn, axis=0, keepdims=True)
        dn = du * g_ref[...]
        dh_ref[...] = d_ref[...] + r * (dn - xn * jnp.mean(dn * xn, axis=-1, keepdims=True))

        pl.when(i == steps - 1)(wait)

    tile = pl.BlockSpec((tm, D_MODEL), lambda i: (i, 0))
    return pl.pallas_call(
        body, name="du_rms", grid=(steps,),
        in_specs=[pl.BlockSpec((tm, D_IN_PAD), lambda i: (i, 0)),
                  pl.BlockSpec((D_IN_PAD, D_MODEL), lambda i: (0, 0)),
                  tile, tile, pl.BlockSpec((1, D_MODEL), lambda i: (0, 0)), ANY],
        out_specs=[tile, pl.BlockSpec((1, D_MODEL), lambda i: (0, 0)), ANY],
        out_shape=[jax.ShapeDtypeStruct((T, D_MODEL), F32),
                   jax.ShapeDtypeStruct((1, D_MODEL), F32),
                   jax.ShapeDtypeStruct(pair_blocks.shape, pair_blocks.dtype)],
        scratch_shapes=_exchange_sems(1, N_CHIP - 1),
        compiler_params=_params(dimension_semantics=("arbitrary",)),
    )(dz, wt_pad, h_pad, dout, g, pair_blocks)


def _tri(lower):
    r = lax.broadcasted_iota(jnp.int32, (CHUNK, CHUNK), 0)
    c = lax.broadcasted_iota(jnp.int32, (CHUNK, CHUNK), 1)
    return jnp.where((r >= c) if lower else (r <= c), 1.0, 0.0).astype(F32)


def _row_valid(n):
    r = lax.broadcasted_iota(jnp.int32, (CHUNK, 128), 0) + n * CHUNK
    return r >= PAD


_FF_SPEC = pl.BlockSpec((T, 128), lambda i: (0, FF_BASE // 128))


def _forget_fwd(z, b_pad):
    def body(z_ref, b_ref, o_ref):
        tri = _tri(True)
        carry = jnp.zeros((1, 128), F32)
        for n in range(NCHUNK):
            rows = pl.ds(n * CHUNK, CHUNK)
            a = z_ref[rows, :] + b_ref[...]
            lf = -(jnp.maximum(-a, 0.0) + jnp.log(1.0 + jnp.exp(-jnp.abs(a))))
            lf = jnp.where(_row_valid(n), lf, 0.0)
            c = jnp.dot(tri, lf, precision=lax.Precision.HIGHEST,
                        preferred_element_type=F32) + carry
            carry = c[CHUNK - 1:CHUNK, :]
            o_ref[:, rows] = jnp.where(_row_valid(n), c * (-LOG2E), NEG_INF).T

    return pl.pallas_call(
        body, name="forget_fwd", grid=(1,),
        in_specs=[_FF_SPEC, pl.BlockSpec((1, 128), lambda i: (0, 0))],
        out_specs=pl.BlockSpec((128, T), lambda i: (0, 0)),
        out_shape=jax.ShapeDtypeStruct((128, T), F32),
        compiler_params=_params(dimension_semantics=("arbitrary",)),
    )(z, b_pad)


def _forget_bwd(z, b_pad, dc, dz):
    def body(z_ref, b_ref, dc_ref, dz_in, dff_ref, db_ref):
        tri = _tri(False)
        carry = jnp.zeros((1, 128), F32)
        db = jnp.zeros((1, 128), F32)
        for n in reversed(range(NCHUNK)):
            rows = pl.ds(n * CHUNK, CHUNK)
            dc_blk = jnp.concatenate([dc_ref[:, rows], jnp.zeros((128 - FOX_HEADS, CHUNK), F32)], axis=0).T
            dlf = jnp.dot(tri, dc_blk, precision=lax.Precision.HIGHEST,
                          preferred_element_type=F32) + carry
            carry = dlf[0:1, :]
            a = z_ref[rows, :] + b_ref[...]
            dff = jnp.where(_row_valid(n), dlf * jax.nn.sigmoid(-a), 0.0)
            dff_ref[rows, :] = dff.astype(BF16)
            db = db + jnp.sum(dff, axis=0, keepdims=True)
        db_ref[...] = db

    return pl.pallas_call(
        body, name="forget_bwd", grid=(1,),
        in_specs=[_FF_SPEC, pl.BlockSpec((1, 128), lambda i: (0, 0)),
                  pl.BlockSpec((FOX_HEADS, T), lambda i: (0, 0)), ANY],
        out_specs=[_FF_SPEC, pl.BlockSpec((1, 128), lambda i: (0, 0))],
        out_shape=[jax.ShapeDtypeStruct((T, D_IN_PAD), BF16),
                   jax.ShapeDtypeStruct((1, 128), F32)],
        input_output_aliases={3: 0},
        compiler_params=_params(dimension_semantics=("arbitrary",)),
    )(z, b_pad, dc, dz)


FOX_QB = 512
FOX_NQB = SEQ // FOX_QB


def _fox_block(b):
    lo = CHUNK + b * FOX_QB
    return pl.ds(lo, FOX_QB), lo, lo + FOX_QB


def _causal_bias():
    r = lax.broadcasted_iota(jnp.int32, (FOX_QB, FOX_QB), 0)
    c = lax.broadcasted_iota(jnp.int32, (FOX_QB, FOX_QB), 1)
    return jnp.where(c <= r, 0.0, NEG_INF).astype(F32)


def _fox_logits(q_blk, k_all, bias, causal, b):
    _, lo, hi = _fox_block(b)
    s_off = _dot_nt(q_blk, k_all[:lo]) + bias[:, :lo]
    s_dia = _dot_nt(q_blk, k_all[lo:hi]) + (bias[:, lo:hi] + causal)
    return s_off, s_dia


_FOX_Z_SPEC = pl.BlockSpec((T, FOX_W), lambda p: (0, FOX_BASE // FOX_W + p))
_FOX_BIAS_SPEC = pl.BlockSpec((2, 1, T), lambda p: (p, 0, 0))
_FOX_LSE_SPEC = pl.BlockSpec((2, T, 1), lambda p: (p, 0, 0))
_FOX_SCALE = FOX_D ** -0.5
_FOX_QSCALE = _FOX_SCALE * LOG2E


def _fox_fwd(z, bias, y, w_out_blk):
    last = FOX_PAIRS - 1

    def body(z_ref, b_ref, y_in, w_ref, a_ref, lse_ref, y_ref, wall_ref,
             send_sems, recv_sems, local_sems):
        start, wait = _direct_exchange([w_ref], [wall_ref], send_sems, recv_sems, local_sems, True)
        pl.when(pl.program_id(0) == 0)(start)

        causal = _causal_bias()
        a_ref[pl.ds(0, CHUNK), :] = jnp.zeros((CHUNK, 128), F32)
        y_ref[pl.ds(0, CHUNK), :] = jnp.zeros((CHUNK, 128), BF16)
        for j in range(2):
            lanes = pl.ds(j * FOX_D, FOX_D)
            k_all = z_ref[:, pl.ds(128 + j * FOX_D, FOX_D)].astype(BF16)
            v_all = z_ref[:, pl.ds(256 + j * FOX_D, FOX_D)].astype(BF16)
            bias = b_ref[j]
            lse_ref[j, pl.ds(0, CHUNK), :] = jnp.zeros((CHUNK, 1), F32)
            for b in range(FOX_NQB):
                rows, lo, hi = _fox_block(b)
                q_blk = (z_ref[rows, lanes] * _FOX_QSCALE).astype(BF16)
                s_off, s_dia = _fox_logits(q_blk, k_all, bias, causal, b)
                m = jnp.maximum(jnp.max(s_off, axis=-1, keepdims=True),
                                jnp.max(s_dia, axis=-1, keepdims=True))
                e_off = jnp.exp2(s_off - m)
                e_dia = jnp.exp2(s_dia - m)
                total = jnp.sum(e_off, axis=-1, keepdims=True) + jnp.sum(e_dia, axis=-1, keepdims=True)
                o = (_dot(e_off.astype(BF16), v_all[:lo]) + _dot(e_dia.astype(BF16), v_all[lo:hi])) / total
                a_ref[rows, lanes] = o
                lse_ref[j, rows, :] = m + jnp.log(total) * LOG2E
                gate = _silu_parts(z_ref[rows, pl.ds(384 + j * FOX_D, FOX_D)])[0]
                y_ref[rows, lanes] = (o * gate).astype(BF16)

        pl.when(pl.program_id(0) == last)(wait)

    return pl.pallas_call(
        body, name="fox_fwd", grid=(FOX_PAIRS,),
        in_specs=[_FOX_Z_SPEC, _FOX_BIAS_SPEC, ANY, ANY],
        out_specs=[pl.BlockSpec((T, 128), lambda p: (0, p)), _FOX_LSE_SPEC,
                   pl.BlockSpec((T, 128), lambda p: (0, 8 + p)), ANY],
        out_shape=[jax.ShapeDtypeStruct((T, FOX_HEADS * FOX_D), F32),
                   jax.ShapeDtypeStruct((FOX_HEADS, T, 1), F32),
                   jax.ShapeDtypeStruct((T, D_MIX), BF16),
                   _exchange_shape(w_out_blk, True)],
        input_output_aliases={2: 2},
        scratch_shapes=_exchange_sems(1),
        compiler_params=_params(dimension_semantics=("arbitrary",)),
    )(z, bias, y, w_out_blk)


def _fox_bwd(z, bias, a_f, lse, dy, dz, dwo_blocks):
    last = FOX_PAIRS - 1

    def body(z_ref, b_ref, a_ref, lse_ref, dy_ref, dz_in, dwo_ref, dz_ref, dc_ref, got_ref,
             kv_acc, dc_acc, send_sems, recv_sems, local_sems):
        start, wait = _direct_exchange([dwo_ref], [got_ref], send_sems, recv_sems, local_sems, False)
        pl.when(pl.program_id(0) == 0)(start)

        causal = _causal_bias()
        dz_ref[pl.ds(0, CHUNK), pl.ds(0, 128)] = jnp.zeros((CHUNK, 128), BF16)
        dz_ref[pl.ds(0, CHUNK), pl.ds(384, 128)] = jnp.zeros((CHUNK, 128), BF16)
        dk_rows, dv_rows = pl.ds(0, FOX_D), pl.ds(FOX_D, FOX_D)
        for j in range(2):
            lanes = pl.ds(j * FOX_D, FOX_D)
            k_all = z_ref[:, pl.ds(128 + j * FOX_D, FOX_D)].astype(BF16)
            v_all = z_ref[:, pl.ds(256 + j * FOX_D, FOX_D)].astype(BF16)
            bias = b_ref[j]
            kv_acc[...] = jnp.zeros_like(kv_acc)
            dc_acc[...] = jnp.zeros_like(dc_acc)
            for b in range(FOX_NQB):
                rows, lo, hi = _fox_block(b)
                off, dia = pl.ds(0, lo), pl.ds(lo, FOX_QB)
                q_blk = (z_ref[rows, lanes] * _FOX_QSCALE).astype(BF16)
                s_off, s_dia = _fox_logits(q_blk, k_all, bias, causal, b)
                lse_blk = lse_ref[j, rows, :]
                p_off, p_dia = jnp.exp2(s_off - lse_blk), jnp.exp2(s_dia - lse_blk)
                sg, dsg = _silu_parts(z_ref[rows, pl.ds(384 + j * FOX_D, FOX_D)])
                dyj = dy_ref[rows, lanes]
                dz_ref[rows, pl.ds(384 + j * FOX_D, FOX_D)] = (dyj * a_ref[rows, lanes] * dsg).astype(BF16)
                do_b = (dyj * sg).astype(BF16)
                dp_off = _dot_nt(do_b, v_all[:lo])
                dp_dia = _dot_nt(do_b, v_all[lo:hi])
                d = (jnp.sum(p_off * dp_off, axis=-1, keepdims=True)
                     + jnp.sum(p_dia * dp_dia, axis=-1, keepdims=True))
                ds_off = p_off * (dp_off - d)
                ds_dia = p_dia * (dp_dia - d)
                dc_acc[:, off] -= jnp.sum(ds_off, axis=0, keepdims=True)
                dc_acc[:, dia] -= jnp.sum(ds_dia, axis=0, keepdims=True)
                ds_off_b, ds_dia_b = ds_off.astype(BF16), ds_dia.astype(BF16)
                dq = _dot(ds_off_b, k_all[:lo]) + _dot(ds_dia_b, k_all[lo:hi])
                dz_ref[rows, lanes] = (dq * _FOX_SCALE).astype(BF16)
                kv_acc[dk_rows, off] += _dot_tn(q_blk, ds_off_b)
                kv_acc[dk_rows, dia] += _dot_tn(q_blk, ds_dia_b)
                kv_acc[dv_rows, off] += _dot_tn(do_b, p_off.astype(BF16))
                kv_acc[dv_rows, dia] += _dot_tn(do_b, p_dia.astype(BF16))
            for n in range(NCHUNK):
                rows = pl.ds(n * CHUNK, CHUNK)
                both = kv_acc[:, rows].T
                dz_ref[rows, pl.ds(128 + j * FOX_D, FOX_D)] = (both[:, :FOX_D] * LN2).astype(BF16)
                dz_ref[rows, pl.ds(256 + j * FOX_D, FOX_D)] = both[:, FOX_D:].astype(BF16)
            dc_ref[j] = dc_acc[...]

        pl.when(pl.program_id(0) == last)(wait)

    col = lambda base: pl.BlockSpec((T, 128), lambda p: (0, base + p))
    return pl.pallas_call(
        body, name="fox_bwd", grid=(FOX_PAIRS,),
        in_specs=[_FOX_Z_SPEC, _FOX_BIAS_SPEC, col(0), _FOX_LSE_SPEC, col(8), ANY, ANY],
        out_specs=[_FOX_Z_SPEC, _FOX_BIAS_SPEC, ANY],
        out_shape=[jax.ShapeDtypeStruct((T, D_IN_PAD), BF16),
                   jax.ShapeDtypeStruct((FOX_HEADS, 1, T), F32),
                   _exchange_shape(dwo_blocks, False)],
        input_output_aliases={5: 0},
        scratch_shapes=[pltpu.VMEM((2 * FOX_D, T), F32), pltpu.VMEM((1, T), F32)] + _exchange_sems(1),
        compiler_params=_params(dimension_semantics=("arbitrary",)),
    )(z, bias, a_f, lse, dy, dz, dwo_blocks)


def _rot(x, cosf, sins):
    return x * cosf + pltpu.roll(x, RET_DK // 2, 1) * sins


def _rot_t(d, cosf, sins):
    return d * cosf - pltpu.roll(d, RET_DK // 2, 1) * sins


_RET_Z_SPEC = pl.BlockSpec((T, RET_W), lambda h: (0, h))
_RET_TABLE_SPECS = [
    pl.BlockSpec((T, RET_DK), lambda h: (0, 0)),
    pl.BlockSpec((T, RET_DK), lambda h: (0, 0)),
    pl.BlockSpec((1, CHUNK, CHUNK), lambda h: (h, 0, 0)),
    pl.BlockSpec((1, CHUNK, 1), lambda h: (h, 0, 0)),
    pl.BlockSpec((1, CHUNK, 1), lambda h: (h, 0, 0)),
    pl.BlockSpec((1, 1, 1), lambda h: (h, 0, 0)),
]
_RQ, _RK = pl.ds(0, RET_DK), pl.ds(RET_DK, RET_DK)
_RV, _RG = pl.ds(2 * RET_DK, RET_DV), pl.ds(2 * RET_DK + RET_DV, RET_DV)
_RET_KSCALE = RET_DK ** -0.5


def _ret_fwd(z, tables):
    def body(z_ref, cos_ref, sin_ref, dm_ref, zeta_ref, xi_ref, cd_ref, raw_ref, y_ref):
        dmask, zeta, xi, cdec = dm_ref[0], zeta_ref[0], xi_ref[0], cd_ref[0]
        state = jnp.zeros((RET_DK, RET_DV), F32)
        for n in range(NCHUNK):
            rows = pl.ds(n * CHUNK, CHUNK)
            cosf, sins = cos_ref[rows, :], sin_ref[rows, :]
            qr = _rot(z_ref[rows, _RQ], cosf, sins)
            kr_b = (_rot(z_ref[rows, _RK], cosf, sins) * _RET_KSCALE).astype(BF16)
            v = z_ref[rows, _RV]
            a = _dot_nt(qr.astype(BF16), kr_b) * dmask
            out = _dot(a.astype(BF16), v.astype(BF16)) + _dot((qr * xi).astype(BF16), state.astype(BF16))
            state = state * cdec + _dot_tn(kr_b, (v * zeta).astype(BF16))
            raw_ref[rows, :] = out
            r = lax.rsqrt(jnp.mean(out * out, axis=-1, keepdims=True) + EPS)
            y_ref[rows, :] = (out * r * _silu_parts(z_ref[rows, _RG])[0]).astype(BF16)

    wide = pl.BlockSpec((T, RET_DV), lambda h: (0, h))
    return pl.pallas_call(
        body, name="ret_fwd", grid=(RET_HEADS,),
        in_specs=[_RET_Z_SPEC] + _RET_TABLE_SPECS,
        out_specs=[wide, wide],
        out_shape=[jax.ShapeDtypeStruct((T, RET_HEADS * RET_DV), F32),
                   jax.ShapeDtypeStruct((T, D_MIX), BF16)],
        compiler_params=_params(dimension_semantics=("arbitrary",)),
    )(z, *tables)


def _ret_bwd(z, tables, raw, dy):
    def body(z_ref, cos_ref, sin_ref, dm_ref, zeta_ref, xi_ref, cd_ref, raw_ref, dy_ref,
             dz_ref, st_ref):
        dmask, zeta, xi, cdec = dm_ref[0], zeta_ref[0], xi_ref[0], cd_ref[0]

        def rotated(n):
            rows = pl.ds(n * CHUNK, CHUNK)
            cosf, sins = cos_ref[rows, :], sin_ref[rows, :]
            qr = _rot(z_ref[rows, _RQ], cosf, sins)
            kr_b = (_rot(z_ref[rows, _RK], cosf, sins) * _RET_KSCALE).astype(BF16)
            return rows, cosf, sins, qr, kr_b

        state = jnp.zeros((RET_DK, RET_DV), F32)
        for n in range(NCHUNK):
            st_ref[n] = state.astype(BF16)
            if n + 1 < NCHUNK:
                rows, _, _, _, kr_b = rotated(n)
                state = state * cdec + _dot_tn(kr_b, (z_ref[rows, _RV] * zeta).astype(BF16))

        grad_state = jnp.zeros((RET_DK, RET_DV), F32)
        for n in reversed(range(NCHUNK)):
            rows, cosf, sins, qr, kr_b = rotated(n)
            qr_b = qr.astype(BF16)
            v_b = z_ref[rows, _RV].astype(BF16)
            gs_b = grad_state.astype(BF16)
            o = raw_ref[rows, :]
            r = lax.rsqrt(jnp.mean(o * o, axis=-1, keepdims=True) + EPS)
            hn = o * r
            sg, dsg = _silu_parts(z_ref[rows, _RG])
            dyn = dy_ref[rows, :]
            dz_ref[rows, _RG] = (dyn * hn * dsg).astype(BF16)
            dhn = dyn * sg
            do_b = (r * (dhn - hn * jnp.mean(dhn * hn, axis=-1, keepdims=True))).astype(BF16)
            a_b = (_dot_nt(qr_b, kr_b) * dmask).astype(BF16)
            da_b = (_dot_nt(do_b, v_b) * dmask).astype(BF16)
            dqr = _dot(da_b, kr_b) + xi * _dot_nt(do_b, st_ref[n])
            dkr = _dot_tn(da_b, qr_b) + zeta * _dot_nt(v_b, gs_b)
            dv = _dot_tn(a_b, do_b) + zeta * _dot(kr_b, gs_b)
            grad_state = grad_state * cdec + _dot_tn((qr * xi).astype(BF16), do_b)
            dz_ref[rows, _RQ] = _rot_t(dqr, cosf, sins).astype(BF16)
            dz_ref[rows, _RK] = (_rot_t(dkr, cosf, sins) * _RET_KSCALE).astype(BF16)
            dz_ref[rows, _RV] = dv.astype(BF16)

    wide = pl.BlockSpec((T, RET_DV), lambda h: (0, h))
    return pl.pallas_call(
        body, name="ret_bwd", grid=(RET_HEADS,),
        in_specs=[_RET_Z_SPEC] + _RET_TABLE_SPECS + [wide, wide],
        out_specs=_RET_Z_SPEC,
        out_shape=jax.ShapeDtypeStruct((T, D_IN_PAD), BF16),
        scratch_shapes=[pltpu.VMEM((NCHUNK, RET_DK, RET_DV), BF16)],
        compiler_params=_params(dimension_semantics=("arbitrary",)),
    )(z, *tables, raw, dy)


def _adamw(w, g, m, v):
    m = ADAM_B1 * m + (1.0 - ADAM_B1) * g
    v = ADAM_B2 * v + (1.0 - ADAM_B2) * (g * g)
    m_hat = m / (1.0 - ADAM_B1 ** ADAM_STEP)
    v_hat = v / (1.0 - ADAM_B2 ** ADAM_STEP)
    delta = -ADAM_LR * (m_hat / (jnp.sqrt(v_hat) + ADAM_EPS) + ADAM_WD * w)
    return delta, m, v


def _sum_adamw(parts, w, m, v, rows, name):
    _, r_tot, cols = parts.shape
    assert r_tot % rows == 0

    def body(p_ref, w_ref, m_ref, v_ref, g_ref, d_ref, nm_ref, nv_ref):
        g = p_ref[0].astype(F32)
        for d in range(1, N_DEV):
            g = g + p_ref[d].astype(F32)
        delta, nm, nv = _adamw(w_ref[...], g, m_ref[...], v_ref[...])
        g_ref[...] = g
        d_ref[...] = delta
        nm_ref[...] = nm
        nv_ref[...] = nv

    blk = pl.BlockSpec((rows, cols), lambda i: (i, 0))
    return pl.pallas_call(
        body, name=name, grid=(r_tot // rows,),
        in_specs=[pl.BlockSpec((N_DEV, rows, cols), lambda i: (0, i, 0)), blk, blk, blk],
        out_specs=[blk] * 4,
        out_shape=[jax.ShapeDtypeStruct((r_tot, cols), F32)] * 4,
        compiler_params=_params(dimension_semantics=("arbitrary",)),
    )(parts, w, m, v)


def _sum_adamw_w_in(parts, w, m, v, small):
    n_part, r, c = parts.shape
    steps = c // 128

    def body(p_ref, w_ref, m_ref, v_ref, s_ref, g_ref, d_ref, nm_ref, nv_ref, got_ref,
             send_sems, recv_sems, local_sems):
        start, wait = _direct_exchange([s_ref], [got_ref], send_sems, recv_sems, local_sems, True)
        pl.when(pl.program_id(0) == 0)(start)
        g = p_ref[0].astype(F32)
        for d in range(1, n_part):
            g = g + p_ref[d].astype(F32)
        delta, nm, nv = _adamw(w_ref[...], g, m_ref[...], v_ref[...])
        g_ref[...] = g
        d_ref[...] = delta
        nm_ref[...] = nm
        nv_ref[...] = nv
        pl.when(pl.program_id(0) == steps - 1)(wait)

    blk = pl.BlockSpec((r, 128), lambda i: (0, i))
    return pl.pallas_call(
        body, name="adamw_w_in", grid=(steps,),
        in_specs=[pl.BlockSpec((n_part, r, 128), lambda i: (0, 0, i)), blk, blk, blk, ANY],
        out_specs=[blk] * 4 + [ANY],
        out_shape=[jax.ShapeDtypeStruct((r, c), F32)] * 4 + [_exchange_shape(small, True)],
        scratch_shapes=_exchange_sems(1),
        compiler_params=_params(dimension_semantics=("arbitrary",)),
    )(parts, w, m, v, small)


def _adamw_small(got, me, metas, norms, finals, biases):
    def body(me_ref, gm_ref, gr_ref, *refs):
        ins, outs = refs[:12], refs[12:]
        g_meta, g_rest = gm_ref[0], gr_ref[0]
        for d in range(1, N_DEV):
            g_meta, g_rest = g_meta + gm_ref[d], g_rest + gr_ref[d]
        grads = [g_meta, g_rest[0:1], g_rest[1:2], g_rest[2:3, :FOX_HEADS]]
        for k, g in enumerate(grads):
            w_ref, m_ref, v_ref = ins[3 * k:3 * k + 3]
            delta, new_m, new_v = _adamw(w_ref[...], g, m_ref[...], v_ref[...])
            for o_ref, val in zip(outs[4 * k:4 * k + 4], (g, delta, new_m, new_v)):
                o_ref[...] = val
        outs[16][...] = g_rest[3:4, :128]

    groups = (metas, norms, finals, biases)
    full = lambda a: pl.BlockSpec(a.shape, lambda i, me_ref: (0,) * a.ndim)
    flat = [a for grp in groups for a in grp]
    res = pl.pallas_call(
        body, name="adamw_small",
        grid_spec=pltpu.PrefetchScalarGridSpec(
            num_scalar_prefetch=1, grid=(1,),
            in_specs=[pl.BlockSpec((N_DEV, N_META, META_BLK), lambda i, me_ref: (0, 0, me_ref[0])),
                      pl.BlockSpec((N_DEV, 8, D_MODEL), lambda i, me_ref: (0, N_META // 8, 0))]
            + [full(a) for a in flat],
            out_specs=[full(grp[0]) for grp in groups for _ in range(4)]
            + [pl.BlockSpec((1, 128), lambda i, me_ref: (0, 0))]),
        out_shape=[jax.ShapeDtypeStruct(grp[0].shape, F32) for grp in groups for _ in range(4)]
        + [jax.ShapeDtypeStruct((1, 128), F32)],
        compiler_params=_params(dimension_semantics=("arbitrary",)),
    )(me, got, got, *flat)
    return [res[4 * k:4 * k + 4] for k in range(4)], res[16]


def kernel(x, meta_tokens, norm_g, w_in, b_f, w_out, final_g, loss_target, m_meta_tokens, m_norm_g, m_w_in, m_b_f, m_w_out, m_final_g, v_meta_tokens, v_norm_g, v_w_in, v_b_f, v_w_out, v_final_g):
    core = lax.axis_index("c")
    me = 4 * lax.axis_index("x") + 2 * lax.axis_index("y") + core
    tables = _tables()

    wt_all, meta_all = _gather_two_level([_slab(w_in[0].T.astype(BF16), me), meta_tokens], name="gather_w_in")
    wt_pad = _regroup_slabs(wt_all)
    meta_full = jnp.transpose(meta_all, (1, 0, 2)).reshape(N_META, D_MODEL)
    h_pad = jnp.concatenate([jnp.zeros((PAD, D_MODEL), F32), meta_full, x[0]], axis=0)
    b_pad = jnp.pad(b_f, ((0, 0), (0, 128 - FOX_HEADS)))

    u, z = _rms_mm_z(h_pad, norm_g, wt_pad)
    bias = _forget_fwd(z, b_pad)[:FOX_HEADS].reshape(FOX_HEADS, 1, T)
    raw, y = _ret_fwd(z, tables)
    a_f, lse, y, w_out_all = _fox_fwd(z, bias, y, w_out[0].astype(BF16))
    w_out_b = w_out_all.reshape(D_MIX, D_MODEL)
    dout, dout_b, dy, loss_blk, d_final_g = _out_loss_dy(y, w_out_b, h_pad, loss_target[0],
                                                         final_g.reshape(1, D_MODEL))

    d_w_out = _mm_tn(y, dout_b, tm=D_MIX, tn=256, name="mm_dwout")
    dz = _ret_bwd(z, tables, raw, dy)
    dz, dc, got_w_out = _fox_bwd(z, bias, a_f, lse, dy, dz, d_w_out.reshape(N_DEV, WO_BLK, D_MODEL))
    dz, db_f = _forget_bwd(z, b_pad, dc.reshape(FOX_HEADS, T), dz)
    d_wt_pad = _mm_tn(dz, u, tm=384, tn=D_MODEL, name="mm_dwin")

    d_wt_slabs = _ungroup_slabs(d_wt_pad)
    theirs = _sibling_swap(d_wt_slabs)
    pair = _pair_sum(d_wt_slabs, theirs, core.astype(jnp.int32).reshape(1))
    dh, d_norm_g, got_slabs = _du_rms(dz, wt_pad, h_pad, dout, norm_g, pair)
    got_w_in = lax.dynamic_slice(got_slabs, (0, (W_BLK - W_STRIDE) * me, 0), (N_CHIP, W_BLK, D_MODEL))

    small = jnp.concatenate([
        dh[PAD:CHUNK], d_norm_g, d_final_g, jnp.pad(db_f[:, :FOX_HEADS], ((0, 0), (0, D_MODEL - FOX_HEADS))),
        jnp.pad(loss_blk[0:1], ((0, 0), (0, D_MODEL - 128))),
        jnp.zeros((SMALL_ROWS - N_META - 4, D_MODEL), F32)], axis=0)
    g_w_in, d_w_in, nm_w_in, nv_w_in, got_small = _sum_adamw_w_in(
        got_w_in, w_in[0].T, m_w_in[0].T, v_w_in[0].T, small)
    g_w_out, d_w_out, nm_w_out, nv_w_out = _sum_adamw(got_w_out, w_out[0], m_w_out[0], v_w_out[0], 128, "adamw_w_out")

    row = lambda a: a.reshape(1, D_MODEL)
    (meta_o, norm_o, final_o, bias_o), loss_row = _adamw_small(
        got_small, me.astype(jnp.int32).reshape(1),
        (meta_tokens, m_meta_tokens, v_meta_tokens), (norm_g, m_norm_g, v_norm_g),
        (row(final_g), row(m_final_g), row(v_final_g)), (b_f, m_b_f, v_b_f))
    final_o = [a.reshape(D_MODEL) for a in final_o]

    back = lambda a: a.T[None]
    outs = [[meta_o[k], norm_o[k], back(wk), bias_o[k], ok[None], final_o[k]]
            for k, (wk, ok) in enumerate(zip((g_w_in, d_w_in, nm_w_in, nv_w_in),
                                             (g_w_out, d_w_out, nm_w_out, nv_w_out)))]
    return (loss_row[0, 0], dh[CHUNK:][None], *outs[0], *outs[1], *outs[2], *outs[3])
```

```python
import numpy as np
import jax
import jax.numpy as jnp
from jax import lax
from jax.experimental import pallas as pl
from jax.experimental.pallas import tpu as pltpu

F32 = jnp.float32
BF16 = jnp.bfloat16

N_DEV = 8
N_CHIP = 4
D_MODEL = 1024
SEQ = 2048
N_META = 16
CHUNK = 128
PAD = CHUNK - N_META
T = SEQ + CHUNK
NCHUNK = T // CHUNK
D_MIX = 2048
RET_HEADS = 4
RET_DK = 128
RET_DV = 256
RET_W = 2 * RET_DK + 2 * RET_DV
FOX_HEADS = 16
FOX_D = 64
FOX_PAIRS = FOX_HEADS // 2
FOX_W = 4 * 128
FOX_BASE = RET_HEADS * RET_W
FF_BASE = FOX_BASE + FOX_PAIRS * FOX_W
D_IN = 7184
D_IN_PAD = 7296
W_BLK = D_IN // N_DEV
WO_BLK = D_MIX // N_DEV
META_BLK = D_MODEL // N_DEV
EPS = 1e-6
NEG_INF = -1e30
ROPE_BASE = 10000.0
LOG2E = 1.4426950408889634
LN2 = 0.6931471805599453

ADAM_LR = 0.001
ADAM_B1 = 0.9
ADAM_B2 = 0.999
ADAM_EPS = 1e-08
ADAM_WD = 0.01
ADAM_STEP = 10

SMALL_ROWS = 24
VMEM_LIMIT = 56 * 1024 * 1024
MESH = pl.DeviceIdType.MESH
ANY = pl.BlockSpec(memory_space=pl.ANY)

_NT = (((1,), (1,)), ((), ()))
_TN = (((0,), (0,)), ((), ()))


def _dot(a, b):
    return jnp.dot(a, b, preferred_element_type=F32)


def _dot_nt(a, b):
    return lax.dot_general(a, b, _NT, preferred_element_type=F32)


def _dot_tn(a, b):
    return lax.dot_general(a, b, _TN, preferred_element_type=F32)


def _params(**kw):
    return pltpu.CompilerParams(vmem_limit_bytes=VMEM_LIMIT, **kw)


def _silu_parts(g):
    sig = jax.nn.sigmoid(g)
    return g * sig, sig * (1.0 + g * (1.0 - sig))


def _tables():
    pos = np.arange(T, dtype=np.float32) - PAD
    inv = (ROPE_BASE ** (-np.arange(0, RET_DK, 2, dtype=np.float32) / RET_DK)).astype(np.float32)
    ang = pos[:, None] * inv[None, :]
    cos, sin = np.cos(ang), np.sin(ang)
    cosf = np.concatenate([cos, cos], axis=1).astype(np.float32)
    sins = np.concatenate([-sin, sin], axis=1).astype(np.float32)
    h = np.arange(RET_HEADS, dtype=np.float32)
    log_gamma = np.log1p(-np.exp2(-5.0 - h)).astype(np.float32)
    idx = np.arange(CHUNK, dtype=np.float32)
    diff = idx[:, None] - idx[None, :]
    dmask = np.where(diff[None] >= 0,
                     np.exp(log_gamma[:, None, None] * np.maximum(diff, 0.0)[None]), 0.0)
    zeta = np.exp(log_gamma[:, None] * (CHUNK - 1.0 - idx)[None, :])
    xi = np.exp(log_gamma[:, None] * (idx + 1.0)[None, :])
    cdec = np.exp(log_gamma * CHUNK)
    return (jnp.asarray(cosf), jnp.asarray(sins), jnp.asarray(dmask, F32),
            jnp.asarray(zeta[:, :, None], F32), jnp.asarray(xi[:, :, None], F32),
            jnp.asarray(cdec[:, None, None], F32))


W_STRIDE = 896
W_SLAB = 912
W_EDGE = W_SLAB - W_STRIDE
def _slab(block, me):
    return lax.dynamic_update_slice(jnp.zeros((W_SLAB, block.shape[1]), block.dtype), block,
                                    ((W_BLK - W_STRIDE) * me, 0))


def _regroup_slabs(slabs):
    c = slabs.shape[2]
    last = slabs[:, W_STRIDE:]
    mine = slabs[:, :W_EDGE] + jnp.concatenate([jnp.zeros_like(last[:1]), last[:-1]], axis=0)
    w = jnp.concatenate([mine, slabs[:, W_EDGE:W_STRIDE]], axis=1).reshape(FF_BASE, c)
    by_head = lambda a, kinds, width: jnp.transpose(a.reshape(kinds, -1, width, c), (1, 0, 2, 3))
    ret = jnp.concatenate([by_head(w[:1024], 2, 128).reshape(RET_HEADS, 256, c),
                           by_head(w[1024:FOX_BASE], 2, 256).reshape(RET_HEADS, 512, c)],
                          axis=1).reshape(FOX_BASE, c)
    fox = by_head(w[FOX_BASE:], 4, 128).reshape(FOX_PAIRS * FOX_W, c)
    ff = jnp.pad(last[N_DEV - 1], ((0, D_IN_PAD - FF_BASE - W_EDGE), (0, 0)))
    return jnp.concatenate([ret, fox, ff], axis=0)


def _mm_tn(a, b, *, tm, tn, name):
    k, m = a.shape
    n = b.shape[1]
    assert m % tm == 0 and n % tn == 0

    def body(a_ref, b_ref, o_ref):
        o_ref[...] = _dot_tn(a_ref[...], b_ref[...]).astype(BF16)

    return pl.pallas_call(
        body, name=name, grid=(n // tn, m // tm),
        in_specs=[pl.BlockSpec((k, tm), lambda j, i: (0, i)),
                  pl.BlockSpec((k, tn), lambda j, i: (0, j))],
        out_specs=pl.BlockSpec((tm, tn), lambda j, i: (i, j)),
        out_shape=jax.ShapeDtypeStruct((m, n), BF16),
        compiler_params=_params(dimension_semantics=("arbitrary", "arbitrary")),
    )(a, b)


def _rms_mm_z(h_pad, g, wt_pad):
    tm, tn = T // 2, D_IN_PAD // 3

    def body(h_ref, g_ref, w_ref, u_ref, z_ref):
        h = h_ref[...]
        r = lax.rsqrt(jnp.mean(h * h, axis=-1, keepdims=True) + EPS)
        u = (h * r * g_ref[...]).astype(BF16)
        u_ref[...] = u
        z_ref[...] = _dot_nt(u, w_ref[...])

    return pl.pallas_call(
        body, name="rms_mm_z", grid=(T // tm, D_IN_PAD // tn),
        in_specs=[pl.BlockSpec((tm, D_MODEL), lambda i, j: (i, 0)),
                  pl.BlockSpec((1, D_MODEL), lambda i, j: (0, 0)),
                  pl.BlockSpec((tn, D_MODEL), lambda i, j: (j, 0))],
        out_specs=[pl.BlockSpec((tm, D_MODEL), lambda i, j: (i, 0)),
                   pl.BlockSpec((tm, tn), lambda i, j: (i, j))],
        out_shape=[jax.ShapeDtypeStruct((T, D_MODEL), BF16),
                   jax.ShapeDtypeStruct((T, D_IN_PAD), F32)],
        compiler_params=_params(dimension_semantics=("arbitrary", "arbitrary")),
    )(h_pad, g, wt_pad)


def _out_loss_dy(y, w_out_b, h_pad, target, g):
    tm = T // 4

    def body(y_ref, w_ref, h_ref, t_hbm, g_ref, d_ref, db_ref, dy_ref, loss_ref, dg_ref, t_buf, t_sem):
        i = pl.program_id(0)
        head = pltpu.make_async_copy(t_hbm.at[pl.ds(0, tm - CHUNK)], t_buf.at[pl.ds(CHUNK, tm - CHUNK)], t_sem)
        rest = pltpu.make_async_copy(t_hbm.at[pl.ds(pl.multiple_of(jnp.maximum(i, 1) * tm - CHUNK, 8), tm)],
                                     t_buf, t_sem)

        @pl.when(i == 0)
        def _():
            t_buf[pl.ds(0, CHUNK), :] = jnp.zeros((CHUNK, D_MODEL), F32)
            head.start()
            loss_ref[...] = jnp.zeros_like(loss_ref)
            dg_ref[...] = jnp.zeros_like(dg_ref)

        pl.when(i > 0)(rest.start)

        w = w_ref[...]
        o = _dot(y_ref[...], w) + h_ref[...]
        pl.when(i == 0)(head.wait)
        pl.when(i > 0)(rest.wait)
        token = lax.broadcasted_iota(jnp.int32, (tm, 1), 0) + i * tm >= CHUNK
        g = g_ref[...]
        r = lax.rsqrt(jnp.mean(o * o, axis=-1, keepdims=True) + EPS)
        xn = o * r
        e = jnp.where(token, xn * g - t_buf[...], 0.0)
        loss_ref[...] += jnp.full(loss_ref.shape, 0.5 / D_MODEL * jnp.sum(e * e), F32)
        do = e * (1.0 / D_MODEL)
        dg_ref[...] += jnp.sum(do * xn, axis=0, keepdims=True)
        dn = do * g
        d = r * (dn - xn * jnp.mean(dn * xn, axis=-1, keepdims=True))
        d_b = d.astype(BF16)
        d_ref[...] = d
        db_ref[...] = d_b
        dy_ref[...] = _dot_nt(d_b, w)

    tile = pl.BlockSpec((tm, D_MODEL), lambda i: (i, 0))
    wide = pl.BlockSpec((tm, D_MIX), lambda i: (i, 0))
    return pl.pallas_call(
        body, name="out_loss_dy", grid=(T // tm,),
        in_specs=[wide, pl.BlockSpec((D_MIX, D_MODEL), lambda i: (0, 0)), tile, ANY,
                  pl.BlockSpec((1, D_MODEL), lambda i: (0, 0))],
        out_specs=[tile, tile, wide,
                   pl.BlockSpec((8, 128), lambda i: (0, 0)),
                   pl.BlockSpec((1, D_MODEL), lambda i: (0, 0))],
        out_shape=[jax.ShapeDtypeStruct((T, D_MODEL), F32),
                   jax.ShapeDtypeStruct((T, D_MODEL), BF16),
                   jax.ShapeDtypeStruct((T, D_MIX), F32),
                   jax.ShapeDtypeStruct((8, 128), F32),
                   jax.ShapeDtypeStruct((1, D_MODEL), F32)],
        scratch_shapes=[pltpu.VMEM((tm, D_MODEL), F32), pltpu.SemaphoreType.DMA],
        compiler_params=_params(dimension_semantics=("arbitrary",)),
    )(y, w_out_b, h_pad, target, g)


def _coords():
    return lax.axis_index("x"), lax.axis_index("y"), lax.axis_index("c")


def _flip(v, bit):
    return 1 - v if bit else v


def _peer(x, y, c, r):
    return _flip(x, (r >> 2) & 1), _flip(y, (r >> 1) & 1), _flip(c, r & 1)


def _direct_exchange(ins, outs, send_sems, recv_sems, local_sems, gather, chips_only=False):
    x, y, c = _coords()
    me = 2 * x + y if chips_only else 4 * x + 2 * y + c

    def src(k, to_idx):
        return ins[k] if gather else ins[k].at[to_idx]

    local = [pltpu.make_async_copy(src(k, me), outs[k].at[me], local_sems.at[k])
             for k in range(len(ins))]
    sends, recvs = [], []
    for r in range(1, N_CHIP if chips_only else N_DEV):
        px, py, pc = _peer(x, y, c, 2 * r if chips_only else r)
        peer = 2 * px + py if chips_only else 4 * px + 2 * py + pc
        for k in range(len(ins)):
            sems = dict(send_sem=send_sems.at[k, r - 1], recv_sem=recv_sems.at[k, r - 1],
                        device_id=(px, py, pc), device_id_type=MESH)
            sends.append(pltpu.make_async_remote_copy(src_ref=src(k, peer), dst_ref=outs[k].at[me], **sems))
            recvs.append(pltpu.make_async_remote_copy(src_ref=src(k, peer), dst_ref=outs[k].at[peer], **sems))

    def start():
        for cp in local + sends:
            cp.start()

    def wait():
        for cp in recvs:
            cp.wait_recv()
        for cp in sends:
            cp.wait_send()
        for cp in local:
            cp.wait()

    return start, wait


def _exchange_sems(n_arr, n_peer=N_DEV - 1):
    return [pltpu.SemaphoreType.DMA((n_arr, n_peer)), pltpu.SemaphoreType.DMA((n_arr, n_peer)),
            pltpu.SemaphoreType.DMA((n_arr,))]


def _exchange_shape(a, gather):
    return jax.ShapeDtypeStruct(((N_DEV,) + a.shape) if gather else a.shape, a.dtype)


def _gather_two_level(arrays, name):
    n_arr = len(arrays)

    def body(*refs):
        ins, outs = refs[:n_arr], refs[n_arr:2 * n_arr]
        send_sems, recv_sems, local_sems = refs[2 * n_arr:]
        x, y, c = _coords()

        def slot(k, px, py, pc):
            return outs[k].at[4 * px + 2 * py + pc]

        def routed(core):
            me, sibling = (x, y, core), (x, y, 1 - core)
            xn, yn, dg = (1 - x, y), (x, 1 - y), (1 - x, 1 - y)
            (first, s_first), (second, s_second) = ((xn, 1), (yn, 2)) if core == 0 else ((yn, 2), (xn, 1))

            def copy(k, j, block, to, own=False):
                return pltpu.make_async_remote_copy(
                    src_ref=ins[k] if own else slot(k, *block), dst_ref=slot(k, *block),
                    send_sem=send_sems.at[k, j], recv_sem=recv_sems.at[k, j],
                    device_id=to, device_id_type=MESH)

            local = [pltpu.make_async_copy(ins[k], slot(k, *me), local_sems.at[k]) for k in range(n_arr)]
            sent = []
            for k in range(n_arr):
                sent += [copy(k, 0, me, sibling, True), copy(k, 1, me, (*xn, core), True),
                         copy(k, 2, me, (*yn, core), True)]
            for cp in local + sent:
                cp.start()

            def pass_on(k, j_from, j_to, block, targets):
                copy(k, j_from, block, me).wait_recv()
                for j, to in zip(j_to, targets):
                    cp = copy(k, j, block, to)
                    cp.start()
                    sent.append(cp)

            for k in range(n_arr):
                pass_on(k, s_first, (3, 3 + s_first), (*first, core), ((*second, core), sibling))
            for k in range(n_arr):
                pass_on(k, s_second, (3 + s_second,), (*second, core), (sibling,))
            for k in range(n_arr):
                pass_on(k, 3, (6,), (*dg, core), (sibling,))
            for k in range(n_arr):
                copy(k, 0, sibling, me).wait_recv()
                for j, chip in ((4, xn), (5, yn), (6, dg)):
                    copy(k, j, (*chip, 1 - core), me).wait_recv()
            for cp in sent:
                cp.wait_send()
            for cp in local:
                cp.wait()

        for core in (0, 1):
            pl.when(c == core)(lambda core=core: routed(core))

    return pl.pallas_call(
        body, name=name,
        in_specs=[ANY] * n_arr, out_specs=[ANY] * n_arr,
        out_shape=[_exchange_shape(a, True) for a in arrays],
        scratch_shapes=_exchange_sems(n_arr),
    )(*arrays)


def _piece_columns():
    pos = {}
    for h in range(RET_HEADS):
        for k, p in enumerate((h, 4 + h, 8 + 2 * h, 9 + 2 * h, 16 + 2 * h, 17 + 2 * h)):
            pos[p] = 6 * h + k
    for p in range(FOX_PAIRS):
        for i in range(4):
            pos[24 + 8 * i + p] = 24 + 4 * p + i
    pos[D_IN_PAD // 128 - 1] = D_IN_PAD // 128 - 1
    return np.array([pos[7 * d + j] for d in range(N_DEV) for j in range(8)], np.int32)


def _dwin_pair_slabs(dz, u, core):
    def body(order_ref, cols_ref, *refs):
        pieces, u_ref, pair_ref, theirs_ref = refs[:8], refs[8], refs[9], refs[10]
        send_buf, got_buf, send_sems, recv_sems, load_sem = refs[11:]
        s = pl.program_id(0)
        x, y, c = _coords()
        cols = jnp.concatenate([p[...] for p in pieces], axis=1)
        slab = _dot_tn(cols, u_ref[...])[:W_SLAB]

        def push(q):
            return pltpu.make_async_remote_copy(
                src_ref=send_buf.at[q], dst_ref=theirs_ref.at[q],
                send_sem=send_sems.at[q], recv_sem=recv_sems.at[q],
                device_id=(x, y, 1 - c), device_id_type=MESH)

        for q in range(N_CHIP):
            @pl.when(s == q)
            def _(q=q):
                send_buf[q] = slab.astype(BF16)
                push(q).start()

            @pl.when(s == N_CHIP + q)
            def _(q=q):
                push(q).wait_recv()
                load = pltpu.make_async_copy(theirs_ref.at[q], got_buf, load_sem)
                load.start()
                load.wait()
                pair_ref[0] = (slab + got_buf[...].astype(F32)).astype(BF16)

        @pl.when(s == 2 * N_CHIP - 1)
        def _():
            for q in range(N_CHIP):
                push(q).wait_send()

    order = jnp.concatenate([2 * jnp.arange(N_CHIP) + (1 - core), 2 * jnp.arange(N_CHIP) + core]).astype(jnp.int32)
    piece = lambda j: pl.BlockSpec((T, 128), lambda s, order_ref, cols_ref: (0, cols_ref[order_ref[s] * 8 + j]))
    slabs = jax.ShapeDtypeStruct((N_CHIP, W_SLAB, D_MODEL), BF16)
    pair, _ = pl.pallas_call(
        body, name="dwin_pair_slabs",
        grid_spec=pltpu.PrefetchScalarGridSpec(
            num_scalar_prefetch=2, grid=(2 * N_CHIP,),
            in_specs=[piece(j) for j in range(8)]
            + [pl.BlockSpec((T, D_MODEL), lambda s, order_ref, cols_ref: (0, 0))],
            out_specs=[pl.BlockSpec((1, W_SLAB, D_MODEL),
                                    lambda s, order_ref, cols_ref: (jnp.maximum(s - N_CHIP, 0), 0, 0)), ANY],
            scratch_shapes=[pltpu.VMEM((N_CHIP, W_SLAB, D_MODEL), BF16), pltpu.VMEM((W_SLAB, D_MODEL), BF16),
                            pltpu.SemaphoreType.DMA((N_CHIP,)), pltpu.SemaphoreType.DMA((N_CHIP,)),
                            pltpu.SemaphoreType.DMA]),
        out_shape=[slabs, slabs],
        compiler_params=_params(dimension_semantics=("arbitrary",)),
    )(order, jnp.asarray(_piece_columns()), *([dz] * 8), u)
    return pair


def _du_rms(dz, wt_pad, h_pad, dout, g, pair_blocks):
    tm = 272
    steps = T // tm

    def body(dz_ref, w_ref, h_ref, d_ref, g_ref, p_ref, dh_ref, dg_ref, got_ref,
             send_sems, recv_sems, local_sems):
        start, wait = _direct_exchange([p_ref], [got_ref], send_sems, recv_sems, local_sems,
                                       False, chips_only=True)
        i = pl.program_id(0)
        pl.when(i == 0)(start)

        @pl.when(i == 0)
        def _():
            dg_ref[...] = jnp.zeros_like(dg_ref)

        du = _dot(dz_ref[...], w_ref[...])
        h = h_ref[...]
        r = lax.rsqrt(jnp.mean(h * h, axis=-1, keepdims=True) + EPS)
        xn = h * r
        dg_ref[...] += jnp.sum(du * xn, axis=0, keepdims=True)
        dn = du * g_ref[...]
        dh_ref[...] = d_ref[...] + r * (dn - xn * jnp.mean(dn * xn, axis=-1, keepdims=True))

        pl.when(i == steps - 1)(wait)

    tile = pl.BlockSpec((tm, D_MODEL), lambda i: (i, 0))
    return pl.pallas_call(
        body, name="du_rms", grid=(steps,),
        in_specs=[pl.BlockSpec((tm, D_IN_PAD), lambda i: (i, 0)),
                  pl.BlockSpec((D_IN_PAD, D_MODEL), lambda i: (0, 0)),
                  tile, tile, pl.BlockSpec((1, D_MODEL), lambda i: (0, 0)), ANY],
        out_specs=[tile, pl.BlockSpec((1, D_MODEL), lambda i: (0, 0)), ANY],
        out_shape=[jax.ShapeDtypeStruct((T, D_MODEL), F32),
                   jax.ShapeDtypeStruct((1, D_MODEL), F32),
                   jax.ShapeDtypeStruct(pair_blocks.shape, pair_blocks.dtype)],
        scratch_shapes=_exchange_sems(1, N_CHIP - 1),
        compiler_params=_params(dimension_semantics=("arbitrary",)),
    )(dz, wt_pad, h_pad, dout, g, pair_blocks)


def _tri(lower):
    r = lax.broadcasted_iota(jnp.int32, (CHUNK, CHUNK), 0)
    c = lax.broadcasted_iota(jnp.int32, (CHUNK, CHUNK), 1)
    return jnp.where((r >= c) if lower else (r <= c), 1.0, 0.0).astype(F32)


def _row_valid(n):
    r = lax.broadcasted_iota(jnp.int32, (CHUNK, 128), 0) + n * CHUNK
    return r >= PAD


_FF_SPEC = pl.BlockSpec((T, 128), lambda i: (0, FF_BASE // 128))


def _forget_fwd(z, b_pad):
    def body(z_ref, b_ref, o_ref):
        tri = _tri(True)
        carry = jnp.zeros((1, 128), F32)
        for n in range(NCHUNK):
            rows = pl.ds(n * CHUNK, CHUNK)
            a = z_ref[rows, :] + b_ref[...]
            lf = -(jnp.maximum(-a, 0.0) + jnp.log(1.0 + jnp.exp(-jnp.abs(a))))
            lf = jnp.where(_row_valid(n), lf, 0.0)
            c = jnp.dot(tri, lf, precision=lax.Precision.HIGHEST,
                        preferred_element_type=F32) + carry
            carry = c[CHUNK - 1:CHUNK, :]
            o_ref[:, rows] = jnp.where(_row_valid(n), c * (-LOG2E), NEG_INF).T

    return pl.pallas_call(
        body, name="forget_fwd", grid=(1,),
        in_specs=[_FF_SPEC, pl.BlockSpec((1, 128), lambda i: (0, 0))],
        out_specs=pl.BlockSpec((128, T), lambda i: (0, 0)),
        out_shape=jax.ShapeDtypeStruct((128, T), F32),
        compiler_params=_params(dimension_semantics=("arbitrary",)),
    )(z, b_pad)


def _forget_bwd(z, b_pad, dc, dz):
    def body(z_ref, b_ref, dc_ref, dz_in, dff_ref, db_ref):
        tri = _tri(False)
        carry = jnp.zeros((1, 128), F32)
        db = jnp.zeros((1, 128), F32)
        for n in reversed(range(NCHUNK)):
            rows = pl.ds(n * CHUNK, CHUNK)
            dc_blk = jnp.concatenate([dc_ref[:, rows], jnp.zeros((128 - FOX_HEADS, CHUNK), F32)], axis=0).T
            dlf = jnp.dot(tri, dc_blk, precision=lax.Precision.HIGHEST,
                          preferred_element_type=F32) + carry
            carry = dlf[0:1, :]
            a = z_ref[rows, :] + b_ref[...]
            dff = jnp.where(_row_valid(n), dlf * jax.nn.sigmoid(-a), 0.0)
            dff_ref[rows, :] = dff.astype(BF16)
            db = db + jnp.sum(dff, axis=0, keepdims=True)
        db_ref[...] = db

    return pl.pallas_call(
        body, name="forget_bwd", grid=(1,),
        in_specs=[_FF_SPEC, pl.BlockSpec((1, 128), lambda i: (0, 0)),
                  pl.BlockSpec((FOX_HEADS, T), lambda i: (0, 0)), ANY],
        out_specs=[_FF_SPEC, pl.BlockSpec((1, 128), lambda i: (0, 0))],
        out_shape=[jax.ShapeDtypeStruct((T, D_IN_PAD), BF16),
                   jax.ShapeDtypeStruct((1, 128), F32)],
        input_output_aliases={3: 0},
        compiler_params=_params(dimension_semantics=("arbitrary",)),
    )(z, b_pad, dc, dz)


FOX_QB = 512
FOX_NQB = SEQ // FOX_QB


def _fox_block(b):
    lo = CHUNK + b * FOX_QB
    return pl.ds(lo, FOX_QB), lo, lo + FOX_QB


def _causal_bias():
    r = lax.broadcasted_iota(jnp.int32, (FOX_QB, FOX_QB), 0)
    c = lax.broadcasted_iota(jnp.int32, (FOX_QB, FOX_QB), 1)
    return jnp.where(c <= r, 0.0, NEG_INF).astype(F32)


def _fox_logits(q_blk, k_all, bias, causal, b):
    _, lo, hi = _fox_block(b)
    s_off = _dot_nt(q_blk, k_all[:lo]) + bias[:, :lo]
    s_dia = _dot_nt(q_blk, k_all[lo:hi]) + (bias[:, lo:hi] + causal)
    return s_off, s_dia


_FOX_Z_SPEC = pl.BlockSpec((T, FOX_W), lambda p: (0, FOX_BASE // FOX_W + p))
_FOX_BIAS_SPEC = pl.BlockSpec((2, 1, T), lambda p: (p, 0, 0))
_FOX_LSE_SPEC = pl.BlockSpec((2, T, 1), lambda p: (p, 0, 0))
_FOX_SCALE = FOX_D ** -0.5
_FOX_QSCALE = _FOX_SCALE * LOG2E


def _fox_fwd(z, bias, y, w_out_blk):
    last = FOX_PAIRS - 1

    def body(z_ref, b_ref, y_in, w_ref, a_ref, lse_ref, y_ref, wall_ref,
             send_sems, recv_sems, local_sems):
        start, wait = _direct_exchange([w_ref], [wall_ref], send_sems, recv_sems, local_sems, True)
        pl.when(pl.program_id(0) == 0)(start)

        causal = _causal_bias()
        a_ref[pl.ds(0, CHUNK), :] = jnp.zeros((CHUNK, 128), F32)
        y_ref[pl.ds(0, CHUNK), :] = jnp.zeros((CHUNK, 128), BF16)
        for j in range(2):
            lanes = pl.ds(j * FOX_D, FOX_D)
            k_all = z_ref[:, pl.ds(128 + j * FOX_D, FOX_D)].astype(BF16)
            v_all = z_ref[:, pl.ds(256 + j * FOX_D, FOX_D)].astype(BF16)
            bias = b_ref[j]
            lse_ref[j, pl.ds(0, CHUNK), :] = jnp.zeros((CHUNK, 1), F32)
            for b in range(FOX_NQB):
                rows, lo, hi = _fox_block(b)
                q_blk = (z_ref[rows, lanes] * _FOX_QSCALE).astype(BF16)
                s_off, s_dia = _fox_logits(q_blk, k_all, bias, causal, b)
                m = jnp.maximum(jnp.max(s_off, axis=-1, keepdims=True),
                                jnp.max(s_dia, axis=-1, keepdims=True))
                e_off = jnp.exp2(s_off - m)
                e_dia = jnp.exp2(s_dia - m)
                total = jnp.sum(e_off, axis=-1, keepdims=True) + jnp.sum(e_dia, axis=-1, keepdims=True)
                o = (_dot(e_off.astype(BF16), v_all[:lo]) + _dot(e_dia.astype(BF16), v_all[lo:hi])) / total
                a_ref[rows, lanes] = o
                lse_ref[j, rows, :] = m + jnp.log(total) * LOG2E
                gate = _silu_parts(z_ref[rows, pl.ds(384 + j * FOX_D, FOX_D)])[0]
                y_ref[rows, lanes] = (o * gate).astype(BF16)

        pl.when(pl.program_id(0) == last)(wait)

    return pl.pallas_call(
        body, name="fox_fwd", grid=(FOX_PAIRS,),
        in_specs=[_FOX_Z_SPEC, _FOX_BIAS_SPEC, ANY, ANY],
        out_specs=[pl.BlockSpec((T, 128), lambda p: (0, p)), _FOX_LSE_SPEC,
                   pl.BlockSpec((T, 128), lambda p: (0, 8 + p)), ANY],
        out_shape=[jax.ShapeDtypeStruct((T, FOX_HEADS * FOX_D), F32),
                   jax.ShapeDtypeStruct((FOX_HEADS, T, 1), F32),
                   jax.ShapeDtypeStruct((T, D_MIX), BF16),
                   _exchange_shape(w_out_blk, True)],
        input_output_aliases={2: 2},
        scratch_shapes=_exchange_sems(1),
        compiler_params=_params(dimension_semantics=("arbitrary",)),
    )(z, bias, y, w_out_blk)


def _fox_bwd(z, bias, a_f, lse, dy, dz, dwo_blocks):
    last = FOX_PAIRS - 1

    def body(z_ref, b_ref, a_ref, lse_ref, dy_ref, dz_in, dwo_ref, dz_ref, dc_ref, got_ref,
             kv_acc, dc_acc, send_sems, recv_sems, local_sems):
        start, wait = _direct_exchange([dwo_ref], [got_ref], send_sems, recv_sems, local_sems, False)
        pl.when(pl.program_id(0) == 0)(start)

        causal = _causal_bias()
        dz_ref[pl.ds(0, CHUNK), pl.ds(0, 128)] = jnp.zeros((CHUNK, 128), BF16)
        dz_ref[pl.ds(0, CHUNK), pl.ds(384, 128)] = jnp.zeros((CHUNK, 128), BF16)
        dk_rows, dv_rows = pl.ds(0, FOX_D), pl.ds(FOX_D, FOX_D)
        for j in range(2):
            lanes = pl.ds(j * FOX_D, FOX_D)
            k_all = z_ref[:, pl.ds(128 + j * FOX_D, FOX_D)].astype(BF16)
            v_all = z_ref[:, pl.ds(256 + j * FOX_D, FOX_D)].astype(BF16)
            bias = b_ref[j]
            kv_acc[...] = jnp.zeros_like(kv_acc)
            dc_acc[...] = jnp.zeros_like(dc_acc)
            for b in range(FOX_NQB):
                rows, lo, hi = _fox_block(b)
                off, dia = pl.ds(0, lo), pl.ds(lo, FOX_QB)
                q_blk = (z_ref[rows, lanes] * _FOX_QSCALE).astype(BF16)
                s_off, s_dia = _fox_logits(q_blk, k_all, bias, causal, b)
                lse_blk = lse_ref[j, rows, :]
                p_off, p_dia = jnp.exp2(s_off - lse_blk), jnp.exp2(s_dia - lse_blk)
                sg, dsg = _silu_parts(z_ref[rows, pl.ds(384 + j * FOX_D, FOX_D)])
                dyj = dy_ref[rows, lanes]
                dz_ref[rows, pl.ds(384 + j * FOX_D, FOX_D)] = (dyj * a_ref[rows, lanes] * dsg).astype(BF16)
                do_b = (dyj * sg).astype(BF16)
                dp_off = _dot_nt(do_b, v_all[:lo])
                dp_dia = _dot_nt(do_b, v_all[lo:hi])
                d = (jnp.sum(p_off * dp_off, axis=-1, keepdims=True)
                     + jnp.sum(p_dia * dp_dia, axis=-1, keepdims=True))
                ds_off = p_off * (dp_off - d)
                ds_dia = p_dia * (dp_dia - d)
                dc_acc[:, off] -= jnp.sum(ds_off, axis=0, keepdims=True)
                dc_acc[:, dia] -= jnp.sum(ds_dia, axis=0, keepdims=True)
                ds_off_b, ds_dia_b = ds_off.astype(BF16), ds_dia.astype(BF16)
                dq = _dot(ds_off_b, k_all[:lo]) + _dot(ds_dia_b, k_all[lo:hi])
                dz_ref[rows, lanes] = (dq * _FOX_SCALE).astype(BF16)
                kv_acc[dk_rows, off] += _dot_tn(q_blk, ds_off_b)
                kv_acc[dk_rows, dia] += _dot_tn(q_blk, ds_dia_b)
                kv_acc[dv_rows, off] += _dot_tn(do_b, p_off.astype(BF16))
                kv_acc[dv_rows, dia] += _dot_tn(do_b, p_dia.astype(BF16))
            for n in range(NCHUNK):
                rows = pl.ds(n * CHUNK, CHUNK)
                both = kv_acc[:, rows].T
                dz_ref[rows, pl.ds(128 + j * FOX_D, FOX_D)] = (both[:, :FOX_D] * LN2).astype(BF16)
                dz_ref[rows, pl.ds(256 + j * FOX_D, FOX_D)] = both[:, FOX_D:].astype(BF16)
            dc_ref[j] = dc_acc[...]

        pl.when(pl.program_id(0) == last)(wait)

    col = lambda base: pl.BlockSpec((T, 128), lambda p: (0, base + p))
    return pl.pallas_call(
        body, name="fox_bwd", grid=(FOX_PAIRS,),
        in_specs=[_FOX_Z_SPEC, _FOX_BIAS_SPEC, col(0), _FOX_LSE_SPEC, col(8), ANY, ANY],
        out_specs=[_FOX_Z_SPEC, _FOX_BIAS_SPEC, ANY],
        out_shape=[jax.ShapeDtypeStruct((T, D_IN_PAD), BF16),
                   jax.ShapeDtypeStruct((FOX_HEADS, 1, T), F32),
                   _exchange_shape(dwo_blocks, False)],
        input_output_aliases={5: 0},
        scratch_shapes=[pltpu.VMEM((2 * FOX_D, T), F32), pltpu.VMEM((1, T), F32)] + _exchange_sems(1),
        compiler_params=_params(dimension_semantics=("arbitrary",)),
    )(z, bias, a_f, lse, dy, dz, dwo_blocks)


def _rot(x, cosf, sins):
    return x * cosf + pltpu.roll(x, RET_DK // 2, 1) * sins


def _rot_t(d, cosf, sins):
    return d * cosf - pltpu.roll(d, RET_DK // 2, 1) * sins


_RET_Z_SPEC = pl.BlockSpec((T, RET_W), lambda h: (0, h))
_RET_TABLE_SPECS = [
    pl.BlockSpec((T, RET_DK), lambda h: (0, 0)),
    pl.BlockSpec((T, RET_DK), lambda h: (0, 0)),
    pl.BlockSpec((1, CHUNK, CHUNK), lambda h: (h, 0, 0)),
    pl.BlockSpec((1, CHUNK, 1), lambda h: (h, 0, 0)),
    pl.BlockSpec((1, CHUNK, 1), lambda h: (h, 0, 0)),
    pl.BlockSpec((1, 1, 1), lambda h: (h, 0, 0)),
]
_RQ, _RK = pl.ds(0, RET_DK), pl.ds(RET_DK, RET_DK)
_RV, _RG = pl.ds(2 * RET_DK, RET_DV), pl.ds(2 * RET_DK + RET_DV, RET_DV)
_RET_KSCALE = RET_DK ** -0.5


def _ret_fwd(z, tables):
    def body(z_ref, cos_ref, sin_ref, dm_ref, zeta_ref, xi_ref, cd_ref, raw_ref, y_ref):
        dmask, zeta, xi, cdec = dm_ref[0], zeta_ref[0], xi_ref[0], cd_ref[0]
        state = jnp.zeros((RET_DK, RET_DV), F32)
        for n in range(NCHUNK):
            rows = pl.ds(n * CHUNK, CHUNK)
            cosf, sins = cos_ref[rows, :], sin_ref[rows, :]
            qr = _rot(z_ref[rows, _RQ], cosf, sins)
            kr_b = (_rot(z_ref[rows, _RK], cosf, sins) * _RET_KSCALE).astype(BF16)
            v = z_ref[rows, _RV]
            a = _dot_nt(qr.astype(BF16), kr_b) * dmask
            out = _dot(a.astype(BF16), v.astype(BF16)) + _dot((qr * xi).astype(BF16), state.astype(BF16))
            state = state * cdec + _dot_tn(kr_b, (v * zeta).astype(BF16))
            raw_ref[rows, :] = out
            r = lax.rsqrt(jnp.mean(out * out, axis=-1, keepdims=True) + EPS)
            y_ref[rows, :] = (out * r * _silu_parts(z_ref[rows, _RG])[0]).astype(BF16)

    wide = pl.BlockSpec((T, RET_DV), lambda h: (0, h))
    return pl.pallas_call(
        body, name="ret_fwd", grid=(RET_HEADS,),
        in_specs=[_RET_Z_SPEC] + _RET_TABLE_SPECS,
        out_specs=[wide, wide],
        out_shape=[jax.ShapeDtypeStruct((T, RET_HEADS * RET_DV), F32),
                   jax.ShapeDtypeStruct((T, D_MIX), BF16)],
        compiler_params=_params(dimension_semantics=("arbitrary",)),
    )(z, *tables)


def _ret_bwd(z, tables, raw, dy):
    def body(z_ref, cos_ref, sin_ref, dm_ref, zeta_ref, xi_ref, cd_ref, raw_ref, dy_ref,
             dz_ref, st_ref):
        dmask, zeta, xi, cdec = dm_ref[0], zeta_ref[0], xi_ref[0], cd_ref[0]

        def rotated(n):
            rows = pl.ds(n * CHUNK, CHUNK)
            cosf, sins = cos_ref[rows, :], sin_ref[rows, :]
            qr = _rot(z_ref[rows, _RQ], cosf, sins)
            kr_b = (_rot(z_ref[rows, _RK], cosf, sins) * _RET_KSCALE).astype(BF16)
            return rows, cosf, sins, qr, kr_b

        state = jnp.zeros((RET_DK, RET_DV), F32)
        for n in range(NCHUNK):
            st_ref[n] = state.astype(BF16)
            if n + 1 < NCHUNK:
                rows, _, _, _, kr_b = rotated(n)
                state = state * cdec + _dot_tn(kr_b, (z_ref[rows, _RV] * zeta).astype(BF16))

        grad_state = jnp.zeros((RET_DK, RET_DV), F32)
        for n in reversed(range(NCHUNK)):
            rows, cosf, sins, qr, kr_b = rotated(n)
            qr_b = qr.astype(BF16)
            v_b = z_ref[rows, _RV].astype(BF16)
            gs_b = grad_state.astype(BF16)
            o = raw_ref[rows, :]
            r = lax.rsqrt(jnp.mean(o * o, axis=-1, keepdims=True) + EPS)
            hn = o * r
            sg, dsg = _silu_parts(z_ref[rows, _RG])
            dyn = dy_ref[rows, :]
            dz_ref[rows, _RG] = (dyn * hn * dsg).astype(BF16)
            dhn = dyn * sg
            do_b = (r * (dhn - hn * jnp.mean(dhn * hn, axis=-1, keepdims=True))).astype(BF16)
            a_b = (_dot_nt(qr_b, kr_b) * dmask).astype(BF16)
            da_b = (_dot_nt(do_b, v_b) * dmask).astype(BF16)
            dqr = _dot(da_b, kr_b) + xi * _dot_nt(do_b, st_ref[n])
            dkr = _dot_tn(da_b, qr_b) + zeta * _dot_nt(v_b, gs_b)
            dv = _dot_tn(a_b, do_b) + zeta * _dot(kr_b, gs_b)
            grad_state = grad_state * cdec + _dot_tn((qr * xi).astype(BF16), do_b)
            dz_ref[rows, _RQ] = _rot_t(dqr, cosf, sins).astype(BF16)
            dz_ref[rows, _RK] = (_rot_t(dkr, cosf, sins) * _RET_KSCALE).astype(BF16)
            dz_ref[rows, _RV] = dv.astype(BF16)

    wide = pl.BlockSpec((T, RET_DV), lambda h: (0, h))
    return pl.pallas_call(
        body, name="ret_bwd", grid=(RET_HEADS,),
        in_specs=[_RET_Z_SPEC] + _RET_TABLE_SPECS + [wide, wide],
        out_specs=_RET_Z_SPEC,
        out_shape=jax.ShapeDtypeStruct((T, D_IN_PAD), BF16),
        scratch_shapes=[pltpu.VMEM((NCHUNK, RET_DK, RET_DV), BF16)],
        compiler_params=_params(dimension_semantics=("arbitrary",)),
    )(z, *tables, raw, dy)


def _adamw(w, g, m, v):
    m = ADAM_B1 * m + (1.0 - ADAM_B1) * g
    v = ADAM_B2 * v + (1.0 - ADAM_B2) * (g * g)
    m_hat = m / (1.0 - ADAM_B1 ** ADAM_STEP)
    v_hat = v / (1.0 - ADAM_B2 ** ADAM_STEP)
    delta = -ADAM_LR * (m_hat / (jnp.sqrt(v_hat) + ADAM_EPS) + ADAM_WD * w)
    return delta, m, v


def _sum_adamw(parts, w, m, v, rows, name):
    _, r_tot, cols = parts.shape
    assert r_tot % rows == 0

    def body(p_ref, w_ref, m_ref, v_ref, g_ref, d_ref, nm_ref, nv_ref):
        g = p_ref[0].astype(F32)
        for d in range(1, N_DEV):
            g = g + p_ref[d].astype(F32)
        delta, nm, nv = _adamw(w_ref[...], g, m_ref[...], v_ref[...])
        g_ref[...] = g
        d_ref[...] = delta
        nm_ref[...] = nm
        nv_ref[...] = nv

    blk = pl.BlockSpec((rows, cols), lambda i: (i, 0))
    return pl.pallas_call(
        body, name=name, grid=(r_tot // rows,),
        in_specs=[pl.BlockSpec((N_DEV, rows, cols), lambda i: (0, i, 0)), blk, blk, blk],
        out_specs=[blk] * 4,
        out_shape=[jax.ShapeDtypeStruct((r_tot, cols), F32)] * 4,
        compiler_params=_params(dimension_semantics=("arbitrary",)),
    )(parts, w, m, v)


def _sum_adamw_w_in(parts, w, m, v, small):
    n_part, r, c = parts.shape
    steps = c // 128

    def body(p_ref, w_ref, m_ref, v_ref, s_ref, g_ref, d_ref, nm_ref, nv_ref, got_ref,
             send_sems, recv_sems, local_sems):
        start, wait = _direct_exchange([s_ref], [got_ref], send_sems, recv_sems, local_sems, True)
        pl.when(pl.program_id(0) == 0)(start)
        g = p_ref[0].astype(F32)
        for d in range(1, n_part):
            g = g + p_ref[d].astype(F32)
        delta, nm, nv = _adamw(w_ref[...], g, m_ref[...], v_ref[...])
        g_ref[...] = g
        d_ref[...] = delta
        nm_ref[...] = nm
        nv_ref[...] = nv
        pl.when(pl.program_id(0) == steps - 1)(wait)

    blk = pl.BlockSpec((r, 128), lambda i: (0, i))
    return pl.pallas_call(
        body, name="adamw_w_in", grid=(steps,),
        in_specs=[pl.BlockSpec((n_part, r, 128), lambda i: (0, 0, i)), blk, blk, blk, ANY],
        out_specs=[blk] * 4 + [ANY],
        out_shape=[jax.ShapeDtypeStruct((r, c), F32)] * 4 + [_exchange_shape(small, True)],
        scratch_shapes=_exchange_sems(1),
        compiler_params=_params(dimension_semantics=("arbitrary",)),
    )(parts, w, m, v, small)


def _adamw_small(got, me, metas, norms, finals, biases):
    def body(me_ref, gm_ref, gr_ref, *refs):
        ins, outs = refs[:12], refs[12:]
        g_meta, g_rest = gm_ref[0], gr_ref[0]
        for d in range(1, N_DEV):
            g_meta, g_rest = g_meta + gm_ref[d], g_rest + gr_ref[d]
        grads = [g_meta, g_rest[0:1], g_rest[1:2], g_rest[2:3, :FOX_HEADS]]
        for k, g in enumerate(grads):
            w_ref, m_ref, v_ref = ins[3 * k:3 * k + 3]
            delta, new_m, new_v = _adamw(w_ref[...], g, m_ref[...], v_ref[...])
            for o_ref, val in zip(outs[4 * k:4 * k + 4], (g, delta, new_m, new_v)):
                o_ref[...] = val
        outs[16][...] = g_rest[3:4, :128]

    groups = (metas, norms, finals, biases)
    full = lambda a: pl.BlockSpec(a.shape, lambda i, me_ref: (0,) * a.ndim)
    flat = [a for grp in groups for a in grp]
    res = pl.pallas_call(
        body, name="adamw_small",
        grid_spec=pltpu.PrefetchScalarGridSpec(
            num_scalar_prefetch=1, grid=(1,),
            in_specs=[pl.BlockSpec((N_DEV, N_META, META_BLK), lambda i, me_ref: (0, 0, me_ref[0])),
                      pl.BlockSpec((N_DEV, 8, D_MODEL), lambda i, me_ref: (0, N_META // 8, 0))]
            + [full(a) for a in flat],
            out_specs=[full(grp[0]) for grp in groups for _ in range(4)]
            + [pl.BlockSpec((1, 128), lambda i, me_ref: (0, 0))]),
        out_shape=[jax.ShapeDtypeStruct(grp[0].shape, F32) for grp in groups for _ in range(4)]
        + [jax.ShapeDtypeStruct((1, 128), F32)],
        compiler_params=_params(dimension_semantics=("arbitrary",)),
    )(me, got, got, *flat)
    return [res[4 * k:4 * k + 4] for k in range(4)], res[16]


def kernel(x, meta_tokens, norm_g, w_in, b_f, w_out, final_g, loss_target, m_meta_tokens, m_norm_g, m_w_in, m_b_f, m_w_out, m_final_g, v_meta_tokens, v_norm_g, v_w_in, v_b_f, v_w_out, v_final_g):
    core = lax.axis_index("c")
    me = 4 * lax.axis_index("x") + 2 * lax.axis_index("y") + core
    tables = _tables()

    wt_all, meta_all = _gather_two_level([_slab(w_in[0].T.astype(BF16), me), meta_tokens], name="gather_w_in")
    wt_pad = _regroup_slabs(wt_all)
    meta_full = jnp.transpose(meta_all, (1, 0, 2)).reshape(N_META, D_MODEL)
    h_pad = jnp.concatenate([jnp.zeros((PAD, D_MODEL), F32), meta_full, x[0]], axis=0)
    b_pad = jnp.pad(b_f, ((0, 0), (0, 128 - FOX_HEADS)))

    u, z = _rms_mm_z(h_pad, norm_g, wt_pad)
    bias = _forget_fwd(z, b_pad)[:FOX_HEADS].reshape(FOX_HEADS, 1, T)
    raw, y = _ret_fwd(z, tables)
    a_f, lse, y, w_out_all = _fox_fwd(z, bias, y, w_out[0].astype(BF16))
    w_out_b = w_out_all.reshape(D_MIX, D_MODEL)
    dout, dout_b, dy, loss_blk, d_final_g = _out_loss_dy(y, w_out_b, h_pad, loss_target[0],
                                                         final_g.reshape(1, D_MODEL))

    d_w_out = _mm_tn(y, dout_b, tm=D_MIX, tn=256, name="mm_dwout")
    dz = _ret_bwd(z, tables, raw, dy)
    dz, dc, got_w_out = _fox_bwd(z, bias, a_f, lse, dy, dz, d_w_out.reshape(N_DEV, WO_BLK, D_MODEL))
    dz, db_f = _forget_bwd(z, b_pad, dc.reshape(FOX_HEADS, T), dz)

    pair = _dwin_pair_slabs(dz, u, core)
    dh, d_norm_g, got_slabs = _du_rms(dz, wt_pad, h_pad, dout, norm_g, pair)
    got_w_in = lax.dynamic_slice(got_slabs, (0, (W_BLK - W_STRIDE) * me, 0), (N_CHIP, W_BLK, D_MODEL))

    small = jnp.concatenate([
        dh[PAD:CHUNK], d_norm_g, d_final_g, jnp.pad(db_f[:, :FOX_HEADS], ((0, 0), (0, D_MODEL - FOX_HEADS))),
        jnp.pad(loss_blk[0:1], ((0, 0), (0, D_MODEL - 128))),
        jnp.zeros((SMALL_ROWS - N_META - 4, D_MODEL), F32)], axis=0)
    g_w_in, d_w_in, nm_w_in, nv_w_in, got_small = _sum_adamw_w_in(
        got_w_in, w_in[0].T, m_w_in[0].T, v_w_in[0].T, small)
    g_w_out, d_w_out, nm_w_out, nv_w_out = _sum_adamw(got_w_out, w_out[0], m_w_out[0], v_w_out[0], 128, "adamw_w_out")

    row = lambda a: a.reshape(1, D_MODEL)
    (meta_o, norm_o, final_o, bias_o), loss_row = _adamw_small(
        got_small, me.astype(jnp.int32).reshape(1),
        (meta_tokens, m_meta_tokens, v_meta_tokens), (norm_g, m_norm_g, v_norm_g),
        (row(final_g), row(m_final_g), row(v_final_g)), (b_f, m_b_f, v_b_f))
    final_o = [a.reshape(D_MODEL) for a in final_o]

    back = lambda a: a.T[None]
    outs = [[meta_o[k], norm_o[k], back(wk), bias_o[k], ok[None], final_o[k]]
            for k, (wk, ok) in enumerate(zip((g_w_in, d_w_in, nm_w_in, nv_w_in),
                                             (g_w_out, d_w_out, nm_w_out, nv_w_out)))]
    return (loss_row[0, 0], dh[CHUNK:][None], *outs[0], *outs[1], *outs[2], *outs[3])
```

```python
import numpy as np
import jax
import jax.numpy as jnp
from jax import lax
from jax.experimental import pallas as pl
from jax.experimental.pallas import tpu as pltpu

F32 = jnp.float32
BF16 = jnp.bfloat16

N_DEV = 8
N_CHIP = 4
D_MODEL = 1024
SEQ = 2048
N_META = 16
CHUNK = 128
PAD = CHUNK - N_META
T = SEQ + CHUNK
NCHUNK = T // CHUNK
D_MIX = 2048
RET_HEADS = 4
RET_DK = 128
RET_DV = 256
RET_W = 2 * RET_DK + 2 * RET_DV
FOX_HEADS = 16
FOX_D = 64
FOX_PAIRS = FOX_HEADS // 2
FOX_W = 4 * 128
FOX_BASE = RET_HEADS * RET_W
FF_BASE = FOX_BASE + FOX_PAIRS * FOX_W
D_IN = 7184
D_IN_PAD = 7296
W_BLK = D_IN // N_DEV
WO_BLK = D_MIX // N_DEV
META_BLK = D_MODEL // N_DEV
EPS = 1e-6
NEG_INF = -1e30
ROPE_BASE = 10000.0
LOG2E = 1.4426950408889634
LN2 = 0.6931471805599453

ADAM_LR = 0.001
ADAM_B1 = 0.9
ADAM_B2 = 0.999
ADAM_EPS = 1e-08
ADAM_WD = 0.01
ADAM_STEP = 10

SMALL_ROWS = 24
VMEM_LIMIT = 56 * 1024 * 1024
MESH = pl.DeviceIdType.MESH
ANY = pl.BlockSpec(memory_space=pl.ANY)

_NT = (((1,), (1,)), ((), ()))
_TN = (((0,), (0,)), ((), ()))


def _dot(a, b):
    return jnp.dot(a, b, preferred_element_type=F32)


def _dot_nt(a, b):
    return lax.dot_general(a, b, _NT, preferred_element_type=F32)


def _dot_tn(a, b):
    return lax.dot_general(a, b, _TN, preferred_element_type=F32)


def _params(**kw):
    return pltpu.CompilerParams(vmem_limit_bytes=VMEM_LIMIT, **kw)


def _silu_parts(g):
    sig = jax.nn.sigmoid(g)
    return g * sig, sig * (1.0 + g * (1.0 - sig))


def _tables():
    pos = np.arange(T, dtype=np.float32) - PAD
    inv = (ROPE_BASE ** (-np.arange(0, RET_DK, 2, dtype=np.float32) / RET_DK)).astype(np.float32)
    ang = pos[:, None] * inv[None, :]
    cos, sin = np.cos(ang), np.sin(ang)
    cosf = np.concatenate([cos, cos], axis=1).astype(np.float32)
    sins = np.concatenate([-sin, sin], axis=1).astype(np.float32)
    h = np.arange(RET_HEADS, dtype=np.float32)
    log_gamma = np.log1p(-np.exp2(-5.0 - h)).astype(np.float32)
    idx = np.arange(CHUNK, dtype=np.float32)
    diff = idx[:, None] - idx[None, :]
    dmask = np.where(diff[None] >= 0,
                     np.exp(log_gamma[:, None, None] * np.maximum(diff, 0.0)[None]), 0.0)
    zeta = np.exp(log_gamma[:, None] * (CHUNK - 1.0 - idx)[None, :])
    xi = np.exp(log_gamma[:, None] * (idx + 1.0)[None, :])
    cdec = np.exp(log_gamma * CHUNK)
    return (jnp.asarray(cosf), jnp.asarray(sins), jnp.asarray(dmask, F32),
            jnp.asarray(zeta[:, :, None], F32), jnp.asarray(xi[:, :, None], F32),
            jnp.asarray(cdec[:, None, None], F32))


W_STRIDE = 896
W_SLAB = 912
W_EDGE = W_SLAB - W_STRIDE
def _slab(block, me):
    return lax.dynamic_update_slice(jnp.zeros((W_SLAB, block.shape[1]), block.dtype), block,
                                    ((W_BLK - W_STRIDE) * me, 0))


def _regroup_slabs(slabs):
    c = slabs.shape[2]
    last = slabs[:, W_STRIDE:]
    mine = slabs[:, :W_EDGE] + jnp.concatenate([jnp.zeros_like(last[:1]), last[:-1]], axis=0)
    w = jnp.concatenate([mine, slabs[:, W_EDGE:W_STRIDE]], axis=1).reshape(FF_BASE, c)
    by_head = lambda a, kinds, width: jnp.transpose(a.reshape(kinds, -1, width, c), (1, 0, 2, 3))
    ret = jnp.concatenate([by_head(w[:1024], 2, 128).reshape(RET_HEADS, 256, c),
                           by_head(w[1024:FOX_BASE], 2, 256).reshape(RET_HEADS, 512, c)],
                          axis=1).reshape(FOX_BASE, c)
    fox = by_head(w[FOX_BASE:], 4, 128).reshape(FOX_PAIRS * FOX_W, c)
    ff = jnp.pad(last[N_DEV - 1], ((0, D_IN_PAD - FF_BASE - W_EDGE), (0, 0)))
    return jnp.concatenate([ret, fox, ff], axis=0)


def _mm_tn(a, b, *, tm, tn, name):
    k, m = a.shape
    n = b.shape[1]
    assert m % tm == 0 and n % tn == 0

    def body(a_ref, b_ref, o_ref):
        o_ref[...] = _dot_tn(a_ref[...], b_ref[...]).astype(BF16)

    return pl.pallas_call(
        body, name=name, grid=(n // tn, m // tm),
        in_specs=[pl.BlockSpec((k, tm), lambda j, i: (0, i)),
                  pl.BlockSpec((k, tn), lambda j, i: (0, j))],
        out_specs=pl.BlockSpec((tm, tn), lambda j, i: (i, j)),
        out_shape=jax.ShapeDtypeStruct((m, n), BF16),
        compiler_params=_params(dimension_semantics=("arbitrary", "arbitrary")),
    )(a, b)


def _rms_mm_z(h_pad, g, wt_pad):
    tm, tn = T // 2, D_IN_PAD // 3
    n_tiles = D_IN_PAD // tn

    def body(h_ref, g_ref, w_ref, u_ref, z_ref, zff_ref):
        h = h_ref[...]
        r = lax.rsqrt(jnp.mean(h * h, axis=-1, keepdims=True) + EPS)
        u = (h * r * g_ref[...]).astype(BF16)
        u_ref[...] = u
        z = _dot_nt(u, w_ref[...])
        z_ref[...] = z.astype(BF16)

        @pl.when(pl.program_id(1) == n_tiles - 1)
        def _():
            zff_ref[...] = z[:, tn - 128:]

    return pl.pallas_call(
        body, name="rms_mm_z", grid=(T // tm, D_IN_PAD // tn),
        in_specs=[pl.BlockSpec((tm, D_MODEL), lambda i, j: (i, 0)),
                  pl.BlockSpec((1, D_MODEL), lambda i, j: (0, 0)),
                  pl.BlockSpec((tn, D_MODEL), lambda i, j: (j, 0))],
        out_specs=[pl.BlockSpec((tm, D_MODEL), lambda i, j: (i, 0)),
                   pl.BlockSpec((tm, tn), lambda i, j: (i, j)),
                   pl.BlockSpec((tm, 128), lambda i, j: (i, 0))],
        out_shape=[jax.ShapeDtypeStruct((T, D_MODEL), BF16),
                   jax.ShapeDtypeStruct((T, D_IN_PAD), BF16),
                   jax.ShapeDtypeStruct((T, 128), F32)],
        compiler_params=_params(dimension_semantics=("arbitrary", "arbitrary")),
    )(h_pad, g, wt_pad)


def _out_loss_dy(y, w_out_b, h_pad, target, g):
    tm = T // 4

    def body(y_ref, w_ref, h_ref, t_hbm, g_ref, d_ref, db_ref, dy_ref, loss_ref, dg_ref, t_buf, t_sem):
        i = pl.program_id(0)
        head = pltpu.make_async_copy(t_hbm.at[pl.ds(0, tm - CHUNK)], t_buf.at[pl.ds(CHUNK, tm - CHUNK)], t_sem)
        rest = pltpu.make_async_copy(t_hbm.at[pl.ds(pl.multiple_of(jnp.maximum(i, 1) * tm - CHUNK, 8), tm)],
                                     t_buf, t_sem)

        @pl.when(i == 0)
        def _():
            t_buf[pl.ds(0, CHUNK), :] = jnp.zeros((CHUNK, D_MODEL), F32)
            head.start()
            loss_ref[...] = jnp.zeros_like(loss_ref)
            dg_ref[...] = jnp.zeros_like(dg_ref)

        pl.when(i > 0)(rest.start)

        w = w_ref[...]
        o = _dot(y_ref[...], w) + h_ref[...]
        pl.when(i == 0)(head.wait)
        pl.when(i > 0)(rest.wait)
        token = lax.broadcasted_iota(jnp.int32, (tm, 1), 0) + i * tm >= CHUNK
        g = g_ref[...]
        r = lax.rsqrt(jnp.mean(o * o, axis=-1, keepdims=True) + EPS)
        xn = o * r
        e = jnp.where(token, xn * g - t_buf[...], 0.0)
        loss_ref[...] += jnp.full(loss_ref.shape, 0.5 / D_MODEL * jnp.sum(e * e), F32)
        do = e * (1.0 / D_MODEL)
        dg_ref[...] += jnp.sum(do * xn, axis=0, keepdims=True)
        dn = do * g
        d = r * (dn - xn * jnp.mean(dn * xn, axis=-1, keepdims=True))
        d_b = d.astype(BF16)
        d_ref[...] = d
        db_ref[...] = d_b
        dy_ref[...] = _dot_nt(d_b, w).astype(BF16)

    tile = pl.BlockSpec((tm, D_MODEL), lambda i: (i, 0))
    wide = pl.BlockSpec((tm, D_MIX), lambda i: (i, 0))
    return pl.pallas_call(
        body, name="out_loss_dy", grid=(T // tm,),
        in_specs=[wide, pl.BlockSpec((D_MIX, D_MODEL), lambda i: (0, 0)), tile, ANY,
                  pl.BlockSpec((1, D_MODEL), lambda i: (0, 0))],
        out_specs=[tile, tile, wide,
                   pl.BlockSpec((8, 128), lambda i: (0, 0)),
                   pl.BlockSpec((1, D_MODEL), lambda i: (0, 0))],
        out_shape=[jax.ShapeDtypeStruct((T, D_MODEL), F32),
                   jax.ShapeDtypeStruct((T, D_MODEL), BF16),
                   jax.ShapeDtypeStruct((T, D_MIX), BF16),
                   jax.ShapeDtypeStruct((8, 128), F32),
                   jax.ShapeDtypeStruct((1, D_MODEL), F32)],
        scratch_shapes=[pltpu.VMEM((tm, D_MODEL), F32), pltpu.SemaphoreType.DMA],
        compiler_params=_params(dimension_semantics=("arbitrary",)),
    )(y, w_out_b, h_pad, target, g)


def _coords():
    return lax.axis_index("x"), lax.axis_index("y"), lax.axis_index("c")


def _flip(v, bit):
    return 1 - v if bit else v


def _peer(x, y, c, r):
    return _flip(x, (r >> 2) & 1), _flip(y, (r >> 1) & 1), _flip(c, r & 1)


def _direct_exchange(ins, outs, send_sems, recv_sems, local_sems, gather, chips_only=False):
    x, y, c = _coords()
    me = 2 * x + y if chips_only else 4 * x + 2 * y + c

    def src(k, to_idx):
        return ins[k] if gather else ins[k].at[to_idx]

    local = [pltpu.make_async_copy(src(k, me), outs[k].at[me], local_sems.at[k])
             for k in range(len(ins))]
    sends, recvs = [], []
    for r in range(1, N_CHIP if chips_only else N_DEV):
        px, py, pc = _peer(x, y, c, 2 * r if chips_only else r)
        peer = 2 * px + py if chips_only else 4 * px + 2 * py + pc
        for k in range(len(ins)):
            sems = dict(send_sem=send_sems.at[k, r - 1], recv_sem=recv_sems.at[k, r - 1],
                        device_id=(px, py, pc), device_id_type=MESH)
            sends.append(pltpu.make_async_remote_copy(src_ref=src(k, peer), dst_ref=outs[k].at[me], **sems))
            recvs.append(pltpu.make_async_remote_copy(src_ref=src(k, peer), dst_ref=outs[k].at[peer], **sems))

    def start():
        for cp in local + sends:
            cp.start()

    def wait():
        for cp in recvs:
            cp.wait_recv()
        for cp in sends:
            cp.wait_send()
        for cp in local:
            cp.wait()

    return start, wait


def _exchange_sems(n_arr, n_peer=N_DEV - 1):
    return [pltpu.SemaphoreType.DMA((n_arr, n_peer)), pltpu.SemaphoreType.DMA((n_arr, n_peer)),
            pltpu.SemaphoreType.DMA((n_arr,))]


def _exchange_shape(a, gather):
    return jax.ShapeDtypeStruct(((N_DEV,) + a.shape) if gather else a.shape, a.dtype)


def _gather_two_level(arrays, name):
    n_arr = len(arrays)

    def body(*refs):
        ins, outs = refs[:n_arr], refs[n_arr:2 * n_arr]
        send_sems, recv_sems, local_sems = refs[2 * n_arr:]
        x, y, c = _coords()

        def slot(k, px, py, pc):
            return outs[k].at[4 * px + 2 * py + pc]

        def routed(core):
            me, sibling = (x, y, core), (x, y, 1 - core)
            xn, yn, dg = (1 - x, y), (x, 1 - y), (1 - x, 1 - y)
            (first, s_first), (second, s_second) = ((xn, 1), (yn, 2)) if core == 0 else ((yn, 2), (xn, 1))

            def copy(k, j, block, to, own=False):
                return pltpu.make_async_remote_copy(
                    src_ref=ins[k] if own else slot(k, *block), dst_ref=slot(k, *block),
                    send_sem=send_sems.at[k, j], recv_sem=recv_sems.at[k, j],
                    device_id=to, device_id_type=MESH)

            local = [pltpu.make_async_copy(ins[k], slot(k, *me), local_sems.at[k]) for k in range(n_arr)]
            sent = []
            for k in range(n_arr):
                sent += [copy(k, 0, me, sibling, True), copy(k, 1, me, (*xn, core), True),
                         copy(k, 2, me, (*yn, core), True)]
            for cp in local + sent:
                cp.start()

            def pass_on(k, j_from, j_to, block, targets):
                copy(k, j_from, block, me).wait_recv()
                for j, to in zip(j_to, targets):
                    cp = copy(k, j, block, to)
                    cp.start()
                    sent.append(cp)

            for k in range(n_arr):
                pass_on(k, s_first, (3, 3 + s_first), (*first, core), ((*second, core), sibling))
            for k in range(n_arr):
                pass_on(k, s_second, (3 + s_second,), (*second, core), (sibling,))
            for k in range(n_arr):
                pass_on(k, 3, (6,), (*dg, core), (sibling,))
            for k in range(n_arr):
                copy(k, 0, sibling, me).wait_recv()
                for j, chip in ((4, xn), (5, yn), (6, dg)):
                    copy(k, j, (*chip, 1 - core), me).wait_recv()
            for cp in sent:
                cp.wait_send()
            for cp in local:
                cp.wait()

        for core in (0, 1):
            pl.when(c == core)(lambda core=core: routed(core))

    return pl.pallas_call(
        body, name=name,
        in_specs=[ANY] * n_arr, out_specs=[ANY] * n_arr,
        out_shape=[_exchange_shape(a, True) for a in arrays],
        scratch_shapes=_exchange_sems(n_arr),
    )(*arrays)


def _piece_columns():
    pos = {}
    for h in range(RET_HEADS):
        for k, p in enumerate((h, 4 + h, 8 + 2 * h, 9 + 2 * h, 16 + 2 * h, 17 + 2 * h)):
            pos[p] = 6 * h + k
    for p in range(FOX_PAIRS):
        for i in range(4):
            pos[24 + 8 * i + p] = 24 + 4 * p + i
    pos[D_IN_PAD // 128 - 1] = D_IN_PAD // 128 - 1
    return np.array([pos[7 * d + j] for d in range(N_DEV) for j in range(8)], np.int32)


def _dwin_pair_slabs(dz, u, core):
    def body(order_ref, cols_ref, *refs):
        pieces, u_ref, pair_ref, theirs_ref = refs[:8], refs[8], refs[9], refs[10]
        send_buf, got_buf, send_sems, recv_sems, load_sem = refs[11:]
        s = pl.program_id(0)
        x, y, c = _coords()
        cols = jnp.concatenate([p[...] for p in pieces], axis=1)
        slab = _dot_tn(cols, u_ref[...])[:W_SLAB]

        def push(q):
            return pltpu.make_async_remote_copy(
                src_ref=send_buf.at[q], dst_ref=theirs_ref.at[q],
                send_sem=send_sems.at[q], recv_sem=recv_sems.at[q],
                device_id=(x, y, 1 - c), device_id_type=MESH)

        for q in range(N_CHIP):
            @pl.when(s == q)
            def _(q=q):
                send_buf[q] = slab.astype(BF16)
                push(q).start()

            @pl.when(s == N_CHIP + q)
            def _(q=q):
                push(q).wait_recv()
                load = pltpu.make_async_copy(theirs_ref.at[q], got_buf, load_sem)
                load.start()
                load.wait()
                pair_ref[0] = (slab + got_buf[...].astype(F32)).astype(BF16)

        @pl.when(s == 2 * N_CHIP - 1)
        def _():
            for q in range(N_CHIP):
                push(q).wait_send()

    order = jnp.concatenate([2 * jnp.arange(N_CHIP) + (1 - core), 2 * jnp.arange(N_CHIP) + core]).astype(jnp.int32)
    piece = lambda j: pl.BlockSpec((T, 128), lambda s, order_ref, cols_ref: (0, cols_ref[order_ref[s] * 8 + j]))
    slabs = jax.ShapeDtypeStruct((N_CHIP, W_SLAB, D_MODEL), BF16)
    pair, _ = pl.pallas_call(
        body, name="dwin_pair_slabs",
        grid_spec=pltpu.PrefetchScalarGridSpec(
            num_scalar_prefetch=2, grid=(2 * N_CHIP,),
            in_specs=[piece(j) for j in range(8)]
            + [pl.BlockSpec((T, D_MODEL), lambda s, order_ref, cols_ref: (0, 0))],
            out_specs=[pl.BlockSpec((1, W_SLAB, D_MODEL),
                                    lambda s, order_ref, cols_ref: (jnp.maximum(s - N_CHIP, 0), 0, 0)), ANY],
            scratch_shapes=[pltpu.VMEM((N_CHIP, W_SLAB, D_MODEL), BF16), pltpu.VMEM((W_SLAB, D_MODEL), BF16),
                            pltpu.SemaphoreType.DMA((N_CHIP,)), pltpu.SemaphoreType.DMA((N_CHIP,)),
                            pltpu.SemaphoreType.DMA]),
        out_shape=[slabs, slabs],
        compiler_params=_params(dimension_semantics=("arbitrary",)),
    )(order, jnp.asarray(_piece_columns()), *([dz] * 8), u)
    return pair


def _du_rms(dz, wt_pad, h_pad, dout, g, pair_blocks):
    tm = 272
    steps = T // tm

    def body(dz_ref, w_ref, h_ref, d_ref, g_ref, p_ref, dh_ref, dg_ref, got_ref,
             send_sems, recv_sems, local_sems):
        start, wait = _direct_exchange([p_ref], [got_ref], send_sems, recv_sems, local_sems,
                                       False, chips_only=True)
        i = pl.program_id(0)
        pl.when(i == 0)(start)

        @pl.when(i == 0)
        def _():
            dg_ref[...] = jnp.zeros_like(dg_ref)

        du = _dot(dz_ref[...], w_ref[...])
        h = h_ref[...]
        r = lax.rsqrt(jnp.mean(h * h, axis=-1, keepdims=True) + EPS)
        xn = h * r
        dg_ref[...] += jnp.sum(du * xn, axis=0, keepdims=True)
        dn = du * g_ref[...]
        dh_ref[...] = d_ref[...] + r * (dn - xn * jnp.mean(dn * xn, axis=-1, keepdims=True))

        pl.when(i == steps - 1)(wait)

    tile = pl.BlockSpec((tm, D_MODEL), lambda i: (i, 0))
    return pl.pallas_call(
        body, name="du_rms", grid=(steps,),
        in_specs=[pl.BlockSpec((tm, D_IN_PAD), lambda i: (i, 0)),
                  pl.BlockSpec((D_IN_PAD, D_MODEL), lambda i: (0, 0)),
                  tile, tile, pl.BlockSpec((1, D_MODEL), lambda i: (0, 0)), ANY],
        out_specs=[tile, pl.BlockSpec((1, D_MODEL), lambda i: (0, 0)), ANY],
        out_shape=[jax.ShapeDtypeStruct((T, D_MODEL), F32),
                   jax.ShapeDtypeStruct((1, D_MODEL), F32),
                   jax.ShapeDtypeStruct(pair_blocks.shape, pair_blocks.dtype)],
        scratch_shapes=_exchange_sems(1, N_CHIP - 1),
        compiler_params=_params(dimension_semantics=("arbitrary",)),
    )(dz, wt_pad, h_pad, dout, g, pair_blocks)


def _tri(lower):
    r = lax.broadcasted_iota(jnp.int32, (CHUNK, CHUNK), 0)
    c = lax.broadcasted_iota(jnp.int32, (CHUNK, CHUNK), 1)
    return jnp.where((r >= c) if lower else (r <= c), 1.0, 0.0).astype(F32)


def _row_valid(n):
    r = lax.broadcasted_iota(jnp.int32, (CHUNK, 128), 0) + n * CHUNK
    return r >= PAD


_FF_SPEC = pl.BlockSpec((T, 128), lambda i: (0, FF_BASE // 128))
_ZFF_SPEC = pl.BlockSpec((T, 128), lambda i: (0, 0))


def _forget_fwd(z, b_pad):
    def body(z_ref, b_ref, o_ref):
        tri = _tri(True)
        carry = jnp.zeros((1, 128), F32)
        for n in range(NCHUNK):
            rows = pl.ds(n * CHUNK, CHUNK)
            a = z_ref[rows, :] + b_ref[...]
            lf = -(jnp.maximum(-a, 0.0) + jnp.log(1.0 + jnp.exp(-jnp.abs(a))))
            lf = jnp.where(_row_valid(n), lf, 0.0)
            c = jnp.dot(tri, lf, precision=lax.Precision.HIGHEST,
                        preferred_element_type=F32) + carry
            carry = c[CHUNK - 1:CHUNK, :]
            o_ref[:, rows] = jnp.where(_row_valid(n), c * (-LOG2E), NEG_INF).T

    return pl.pallas_call(
        body, name="forget_fwd", grid=(1,),
        in_specs=[_ZFF_SPEC, pl.BlockSpec((1, 128), lambda i: (0, 0))],
        out_specs=pl.BlockSpec((128, T), lambda i: (0, 0)),
        out_shape=jax.ShapeDtypeStruct((128, T), F32),
        compiler_params=_params(dimension_semantics=("arbitrary",)),
    )(z, b_pad)


def _forget_bwd(z, b_pad, dc, dz):
    def body(z_ref, b_ref, dc_ref, dz_in, dff_ref, db_ref):
        tri = _tri(False)
        carry = jnp.zeros((1, 128), F32)
        db = jnp.zeros((1, 128), F32)
        for n in reversed(range(NCHUNK)):
            rows = pl.ds(n * CHUNK, CHUNK)
            dc_blk = jnp.concatenate([dc_ref[:, rows], jnp.zeros((128 - FOX_HEADS, CHUNK), F32)], axis=0).T
            dlf = jnp.dot(tri, dc_blk, precision=lax.Precision.HIGHEST,
                          preferred_element_type=F32) + carry
            carry = dlf[0:1, :]
            a = z_ref[rows, :] + b_ref[...]
            dff = jnp.where(_row_valid(n), dlf * jax.nn.sigmoid(-a), 0.0)
            dff_ref[rows, :] = dff.astype(BF16)
            db = db + jnp.sum(dff, axis=0, keepdims=True)
        db_ref[...] = db

    return pl.pallas_call(
        body, name="forget_bwd", grid=(1,),
        in_specs=[_ZFF_SPEC, pl.BlockSpec((1, 128), lambda i: (0, 0)),
                  pl.BlockSpec((FOX_HEADS, T), lambda i: (0, 0)), ANY],
        out_specs=[_FF_SPEC, pl.BlockSpec((1, 128), lambda i: (0, 0))],
        out_shape=[jax.ShapeDtypeStruct((T, D_IN_PAD), BF16),
                   jax.ShapeDtypeStruct((1, 128), F32)],
        input_output_aliases={3: 0},
        compiler_params=_params(dimension_semantics=("arbitrary",)),
    )(z, b_pad, dc, dz)


FOX_QB = 512
FOX_NQB = SEQ // FOX_QB


def _fox_block(b):
    lo = CHUNK + b * FOX_QB
    return pl.ds(lo, FOX_QB), lo, lo + FOX_QB


def _causal_bias():
    r = lax.broadcasted_iota(jnp.int32, (FOX_QB, FOX_QB), 0)
    c = lax.broadcasted_iota(jnp.int32, (FOX_QB, FOX_QB), 1)
    return jnp.where(c <= r, 0.0, NEG_INF).astype(F32)


def _fox_logits(q_blk, k_all, bias, causal, b):
    _, lo, hi = _fox_block(b)
    s_off = _dot_nt(q_blk, k_all[:lo]) + bias[:, :lo]
    s_dia = _dot_nt(q_blk, k_all[lo:hi]) + (bias[:, lo:hi] + causal)
    return s_off, s_dia


_FOX_Z_SPEC = pl.BlockSpec((T, FOX_W), lambda p: (0, FOX_BASE // FOX_W + p))
_FOX_BIAS_SPEC = pl.BlockSpec((2, 1, T), lambda p: (p, 0, 0))
_FOX_LSE_SPEC = pl.BlockSpec((2, T, 1), lambda p: (p, 0, 0))
_FOX_SCALE = FOX_D ** -0.5
_FOX_QSCALE = _FOX_SCALE * LOG2E


def _fox_fwd(z, bias, y, w_out_blk):
    last = FOX_PAIRS - 1

    def body(z_ref, b_ref, y_in, w_ref, a_ref, lse_ref, y_ref, wall_ref,
             send_sems, recv_sems, local_sems):
        start, wait = _direct_exchange([w_ref], [wall_ref], send_sems, recv_sems, local_sems, True)
        pl.when(pl.program_id(0) == 0)(start)

        causal = _causal_bias()
        a_ref[pl.ds(0, CHUNK), :] = jnp.zeros((CHUNK, 128), F32)
        y_ref[pl.ds(0, CHUNK), :] = jnp.zeros((CHUNK, 128), BF16)
        for j in range(2):
            lanes = pl.ds(j * FOX_D, FOX_D)
            k_all = z_ref[:, pl.ds(128 + j * FOX_D, FOX_D)]
            v_all = z_ref[:, pl.ds(256 + j * FOX_D, FOX_D)]
            bias = b_ref[j]
            lse_ref[j, pl.ds(0, CHUNK), :] = jnp.zeros((CHUNK, 1), F32)
            for b in range(FOX_NQB):
                rows, lo, hi = _fox_block(b)
                q_blk = (z_ref[rows, lanes].astype(F32) * _FOX_QSCALE).astype(BF16)
                s_off, s_dia = _fox_logits(q_blk, k_all, bias, causal, b)
                m = jnp.maximum(jnp.max(s_off, axis=-1, keepdims=True),
                                jnp.max(s_dia, axis=-1, keepdims=True))
                e_off = jnp.exp2(s_off - m)
                e_dia = jnp.exp2(s_dia - m)
                total = jnp.sum(e_off, axis=-1, keepdims=True) + jnp.sum(e_dia, axis=-1, keepdims=True)
                o = (_dot(e_off.astype(BF16), v_all[:lo]) + _dot(e_dia.astype(BF16), v_all[lo:hi])) / total
                a_ref[rows, lanes] = o
                lse_ref[j, rows, :] = m + jnp.log(total) * LOG2E
                gate = _silu_parts(z_ref[rows, pl.ds(384 + j * FOX_D, FOX_D)].astype(F32))[0]
                y_ref[rows, lanes] = (o * gate).astype(BF16)

        pl.when(pl.program_id(0) == last)(wait)

    return pl.pallas_call(
        body, name="fox_fwd", grid=(FOX_PAIRS,),
        in_specs=[_FOX_Z_SPEC, _FOX_BIAS_SPEC, ANY, ANY],
        out_specs=[pl.BlockSpec((T, 128), lambda p: (0, p)), _FOX_LSE_SPEC,
                   pl.BlockSpec((T, 128), lambda p: (0, 8 + p)), ANY],
        out_shape=[jax.ShapeDtypeStruct((T, FOX_HEADS * FOX_D), F32),
                   jax.ShapeDtypeStruct((FOX_HEADS, T, 1), F32),
                   jax.ShapeDtypeStruct((T, D_MIX), BF16),
                   _exchange_shape(w_out_blk, True)],
        input_output_aliases={2: 2},
        scratch_shapes=_exchange_sems(1),
        compiler_params=_params(dimension_semantics=("arbitrary",)),
    )(z, bias, y, w_out_blk)


def _fox_bwd(z, bias, a_f, lse, dy, dz, dwo_blocks):
    last = FOX_PAIRS - 1

    def body(z_ref, b_ref, a_ref, lse_ref, dy_ref, dz_in, dwo_ref, dz_ref, dc_ref, got_ref,
             kv_acc, dc_acc, send_sems, recv_sems, local_sems):
        start, wait = _direct_exchange([dwo_ref], [got_ref], send_sems, recv_sems, local_sems, False)
        pl.when(pl.program_id(0) == 0)(start)

        causal = _causal_bias()
        dz_ref[pl.ds(0, CHUNK), pl.ds(0, 128)] = jnp.zeros((CHUNK, 128), BF16)
        dz_ref[pl.ds(0, CHUNK), pl.ds(384, 128)] = jnp.zeros((CHUNK, 128), BF16)
        dk_rows, dv_rows = pl.ds(0, FOX_D), pl.ds(FOX_D, FOX_D)
        for j in range(2):
            lanes = pl.ds(j * FOX_D, FOX_D)
            k_all = z_ref[:, pl.ds(128 + j * FOX_D, FOX_D)]
            v_all = z_ref[:, pl.ds(256 + j * FOX_D, FOX_D)]
            bias = b_ref[j]
            kv_acc[...] = jnp.zeros_like(kv_acc)
            dc_acc[...] = jnp.zeros_like(dc_acc)
            for b in range(FOX_NQB):
                rows, lo, hi = _fox_block(b)
                off, dia = pl.ds(0, lo), pl.ds(lo, FOX_QB)
                q_blk = (z_ref[rows, lanes].astype(F32) * _FOX_QSCALE).astype(BF16)
                s_off, s_dia = _fox_logits(q_blk, k_all, bias, causal, b)
                lse_blk = lse_ref[j, rows, :]
                p_off, p_dia = jnp.exp2(s_off - lse_blk), jnp.exp2(s_dia - lse_blk)
                sg, dsg = _silu_parts(z_ref[rows, pl.ds(384 + j * FOX_D, FOX_D)].astype(F32))
                dyj = dy_ref[rows, lanes].astype(F32)
                dz_ref[rows, pl.ds(384 + j * FOX_D, FOX_D)] = (dyj * a_ref[rows, lanes] * dsg).astype(BF16)
                do_b = (dyj * sg).astype(BF16)
                dp_off = _dot_nt(do_b, v_all[:lo])
                dp_dia = _dot_nt(do_b, v_all[lo:hi])
                d = (jnp.sum(p_off * dp_off, axis=-1, keepdims=True)
                     + jnp.sum(p_dia * dp_dia, axis=-1, keepdims=True))
                ds_off = p_off * (dp_off - d)
                ds_dia = p_dia * (dp_dia - d)
                dc_acc[:, off] -= jnp.sum(ds_off, axis=0, keepdims=True)
                dc_acc[:, dia] -= jnp.sum(ds_dia, axis=0, keepdims=True)
                ds_off_b, ds_dia_b = ds_off.astype(BF16), ds_dia.astype(BF16)
                dq = _dot(ds_off_b, k_all[:lo]) + _dot(ds_dia_b, k_all[lo:hi])
                dz_ref[rows, lanes] = (dq * _FOX_SCALE).astype(BF16)
                kv_acc[dk_rows, off] += _dot_tn(q_blk, ds_off_b)
                kv_acc[dk_rows, dia] += _dot_tn(q_blk, ds_dia_b)
                kv_acc[dv_rows, off] += _dot_tn(do_b, p_off.astype(BF16))
                kv_acc[dv_rows, dia] += _dot_tn(do_b, p_dia.astype(BF16))
            for n in range(NCHUNK):
                rows = pl.ds(n * CHUNK, CHUNK)
                both = kv_acc[:, rows].T
                dz_ref[rows, pl.ds(128 + j * FOX_D, FOX_D)] = (both[:, :FOX_D] * LN2).astype(BF16)
                dz_ref[rows, pl.ds(256 + j * FOX_D, FOX_D)] = both[:, FOX_D:].astype(BF16)
            dc_ref[j] = dc_acc[...]

        pl.when(pl.program_id(0) == last)(wait)

    col = lambda base: pl.BlockSpec((T, 128), lambda p: (0, base + p))
    return pl.pallas_call(
        body, name="fox_bwd", grid=(FOX_PAIRS,),
        in_specs=[_FOX_Z_SPEC, _FOX_BIAS_SPEC, col(0), _FOX_LSE_SPEC, col(8), ANY, ANY],
        out_specs=[_FOX_Z_SPEC, _FOX_BIAS_SPEC, ANY],
        out_shape=[jax.ShapeDtypeStruct((T, D_IN_PAD), BF16),
                   jax.ShapeDtypeStruct((FOX_HEADS, 1, T), F32),
                   _exchange_shape(dwo_blocks, False)],
        input_output_aliases={5: 0},
        scratch_shapes=[pltpu.VMEM((2 * FOX_D, T), F32), pltpu.VMEM((1, T), F32)] + _exchange_sems(1),
        compiler_params=_params(dimension_semantics=("arbitrary",)),
    )(z, bias, a_f, lse, dy, dz, dwo_blocks)


def _rot(x, cosf, sins):
    return x * cosf + pltpu.roll(x, RET_DK // 2, 1) * sins


def _rot_t(d, cosf, sins):
    return d * cosf - pltpu.roll(d, RET_DK // 2, 1) * sins


_RET_Z_SPEC = pl.BlockSpec((T, RET_W), lambda h: (0, h))
_RET_TABLE_SPECS = [
    pl.BlockSpec((T, RET_DK), lambda h: (0, 0)),
    pl.BlockSpec((T, RET_DK), lambda h: (0, 0)),
    pl.BlockSpec((1, CHUNK, CHUNK), lambda h: (h, 0, 0)),
    pl.BlockSpec((1, CHUNK, 1), lambda h: (h, 0, 0)),
    pl.BlockSpec((1, CHUNK, 1), lambda h: (h, 0, 0)),
    pl.BlockSpec((1, 1, 1), lambda h: (h, 0, 0)),
]
_RQ, _RK = pl.ds(0, RET_DK), pl.ds(RET_DK, RET_DK)
_RV, _RG = pl.ds(2 * RET_DK, RET_DV), pl.ds(2 * RET_DK + RET_DV, RET_DV)
_RET_KSCALE = RET_DK ** -0.5


def _ret_fwd(z, tables):
    def body(z_ref, cos_ref, sin_ref, dm_ref, zeta_ref, xi_ref, cd_ref, raw_ref, y_ref):
        dmask, zeta, xi, cdec = dm_ref[0], zeta_ref[0], xi_ref[0], cd_ref[0]
        state = jnp.zeros((RET_DK, RET_DV), F32)
        for n in range(NCHUNK):
            rows = pl.ds(n * CHUNK, CHUNK)
            cosf, sins = cos_ref[rows, :], sin_ref[rows, :]
            qr = _rot(z_ref[rows, _RQ].astype(F32), cosf, sins)
            kr_b = (_rot(z_ref[rows, _RK].astype(F32), cosf, sins) * _RET_KSCALE).astype(BF16)
            v_b = z_ref[rows, _RV]
            a = _dot_nt(qr.astype(BF16), kr_b) * dmask
            out = _dot(a.astype(BF16), v_b) + _dot((qr * xi).astype(BF16), state.astype(BF16))
            state = state * cdec + _dot_tn(kr_b, (v_b.astype(F32) * zeta).astype(BF16))
            raw_ref[rows, :] = out
            r = lax.rsqrt(jnp.mean(out * out, axis=-1, keepdims=True) + EPS)
            y_ref[rows, :] = (out * r * _silu_parts(z_ref[rows, _RG].astype(F32))[0]).astype(BF16)

    wide = pl.BlockSpec((T, RET_DV), lambda h: (0, h))
    return pl.pallas_call(
        body, name="ret_fwd", grid=(RET_HEADS,),
        in_specs=[_RET_Z_SPEC] + _RET_TABLE_SPECS,
        out_specs=[wide, wide],
        out_shape=[jax.ShapeDtypeStruct((T, RET_HEADS * RET_DV), F32),
                   jax.ShapeDtypeStruct((T, D_MIX), BF16)],
        compiler_params=_params(dimension_semantics=("arbitrary",)),
    )(z, *tables)


def _ret_bwd(z, tables, raw, dy):
    def body(z_ref, cos_ref, sin_ref, dm_ref, zeta_ref, xi_ref, cd_ref, raw_ref, dy_ref,
             dz_ref, st_ref):
        dmask, zeta, xi, cdec = dm_ref[0], zeta_ref[0], xi_ref[0], cd_ref[0]

        def rotated(n):
            rows = pl.ds(n * CHUNK, CHUNK)
            cosf, sins = cos_ref[rows, :], sin_ref[rows, :]
            qr = _rot(z_ref[rows, _RQ].astype(F32), cosf, sins)
            kr_b = (_rot(z_ref[rows, _RK].astype(F32), cosf, sins) * _RET_KSCALE).astype(BF16)
            return rows, cosf, sins, qr, kr_b

        state = jnp.zeros((RET_DK, RET_DV), F32)
        for n in range(NCHUNK):
            st_ref[n] = state.astype(BF16)
            if n + 1 < NCHUNK:
                rows, _, _, _, kr_b = rotated(n)
                state = state * cdec + _dot_tn(kr_b, (z_ref[rows, _RV].astype(F32) * zeta).astype(BF16))

        grad_state = jnp.zeros((RET_DK, RET_DV), F32)
        for n in reversed(range(NCHUNK)):
            rows, cosf, sins, qr, kr_b = rotated(n)
            qr_b = qr.astype(BF16)
            v_b = z_ref[rows, _RV]
            gs_b = grad_state.astype(BF16)
            o = raw_ref[rows, :]
            r = lax.rsqrt(jnp.mean(o * o, axis=-1, keepdims=True) + EPS)
            hn = o * r
            sg, dsg = _silu_parts(z_ref[rows, _RG].astype(F32))
            dyn = dy_ref[rows, :].astype(F32)
            dz_ref[rows, _RG] = (dyn * hn * dsg).astype(BF16)
            dhn = dyn * sg
            do_b = (r * (dhn - hn * jnp.mean(dhn * hn, axis=-1, keepdims=True))).astype(BF16)
            a_b = (_dot_nt(qr_b, kr_b) * dmask).astype(BF16)
            da_b = (_dot_nt(do_b, v_b) * dmask).astype(BF16)
            dqr = _dot(da_b, kr_b) + xi * _dot_nt(do_b, st_ref[n])
            dkr = _dot_tn(da_b, qr_b) + zeta * _dot_nt(v_b, gs_b)
            dv = _dot_tn(a_b, do_b) + zeta * _dot(kr_b, gs_b)
            grad_state = grad_state * cdec + _dot_tn((qr * xi).astype(BF16), do_b)
            dz_ref[rows, _RQ] = _rot_t(dqr, cosf, sins).astype(BF16)
            dz_ref[rows, _RK] = (_rot_t(dkr, cosf, sins) * _RET_KSCALE).astype(BF16)
            dz_ref[rows, _RV] = dv.astype(BF16)

    wide = pl.BlockSpec((T, RET_DV), lambda h: (0, h))
    return pl.pallas_call(
        body, name="ret_bwd", grid=(RET_HEADS,),
        in_specs=[_RET_Z_SPEC] + _RET_TABLE_SPECS + [wide, wide],
        out_specs=_RET_Z_SPEC,
        out_shape=jax.ShapeDtypeStruct((T, D_IN_PAD), BF16),
        scratch_shapes=[pltpu.VMEM((NCHUNK, RET_DK, RET_DV), BF16)],
        compiler_params=_params(dimension_semantics=("arbitrary",)),
    )(z, *tables, raw, dy)


def _adamw(w, g, m, v):
    m = ADAM_B1 * m + (1.0 - ADAM_B1) * g
    v = ADAM_B2 * v + (1.0 - ADAM_B2) * (g * g)
    m_hat = m / (1.0 - ADAM_B1 ** ADAM_STEP)
    v_hat = v / (1.0 - ADAM_B2 ** ADAM_STEP)
    delta = -ADAM_LR * (m_hat / (jnp.sqrt(v_hat) + ADAM_EPS) + ADAM_WD * w)
    return delta, m, v


def _sum_adamw(parts, w, m, v, rows, name):
    _, r_tot, cols = parts.shape
    assert r_tot % rows == 0

    def body(p_ref, w_ref, m_ref, v_ref, g_ref, d_ref, nm_ref, nv_ref):
        g = p_ref[0].astype(F32)
        for d in range(1, N_DEV):
            g = g + p_ref[d].astype(F32)
        delta, nm, nv = _adamw(w_ref[...], g, m_ref[...], v_ref[...])
        g_ref[...] = g
        d_ref[...] = delta
        nm_ref[...] = nm
        nv_ref[...] = nv

    blk = pl.BlockSpec((rows, cols), lambda i: (i, 0))
    return pl.pallas_call(
        body, name=name, grid=(r_tot // rows,),
        in_specs=[pl.BlockSpec((N_DEV, rows, cols), lambda i: (0, i, 0)), blk, blk, blk],
        out_specs=[blk] * 4,
        out_shape=[jax.ShapeDtypeStruct((r_tot, cols), F32)] * 4,
        compiler_params=_params(dimension_semantics=("arbitrary",)),
    )(parts, w, m, v)


def _sum_adamw_w_in(parts, w, m, v, small):
    n_part, r, c = parts.shape
    steps = c // 128

    def body(p_ref, w_ref, m_ref, v_ref, s_ref, g_ref, d_ref, nm_ref, nv_ref, got_ref,
             send_sems, recv_sems, local_sems):
        start, wait = _direct_exchange([s_ref], [got_ref], send_sems, recv_sems, local_sems, True)
        pl.when(pl.program_id(0) == 0)(start)
        g = p_ref[0].astype(F32)
        for d in range(1, n_part):
            g = g + p_ref[d].astype(F32)
        delta, nm, nv = _adamw(w_ref[...], g, m_ref[...], v_ref[...])
        g_ref[...] = g
        d_ref[...] = delta
        nm_ref[...] = nm
        nv_ref[...] = nv
        pl.when(pl.program_id(0) == steps - 1)(wait)

    blk = pl.BlockSpec((r, 128), lambda i: (0, i))
    return pl.pallas_call(
        body, name="adamw_w_in", grid=(steps,),
        in_specs=[pl.BlockSpec((n_part, r, 128), lambda i: (0, 0, i)), blk, blk, blk, ANY],
        out_specs=[blk] * 4 + [ANY],
        out_shape=[jax.ShapeDtypeStruct((r, c), F32)] * 4 + [_exchange_shape(small, True)],
        scratch_shapes=_exchange_sems(1),
        compiler_params=_params(dimension_semantics=("arbitrary",)),
    )(parts, w, m, v, small)


def _adamw_small(got, me, metas, norms, finals, biases):
    def body(me_ref, gm_ref, gr_ref, *refs):
        ins, outs = refs[:12], refs[12:]
        g_meta, g_rest = gm_ref[0], gr_ref[0]
        for d in range(1, N_DEV):
            g_meta, g_rest = g_meta + gm_ref[d], g_rest + gr_ref[d]
        grads = [g_meta, g_rest[0:1], g_rest[1:2], g_rest[2:3, :FOX_HEADS]]
        for k, g in enumerate(grads):
            w_ref, m_ref, v_ref = ins[3 * k:3 * k + 3]
            delta, new_m, new_v = _adamw(w_ref[...], g, m_ref[...], v_ref[...])
            for o_ref, val in zip(outs[4 * k:4 * k + 4], (g, delta, new_m, new_v)):
                o_ref[...] = val
        outs[16][...] = g_rest[3:4, :128]

    groups = (metas, norms, finals, biases)
    full = lambda a: pl.BlockSpec(a.shape, lambda i, me_ref: (0,) * a.ndim)
    flat = [a for grp in groups for a in grp]
    res = pl.pallas_call(
        body, name="adamw_small",
        grid_spec=pltpu.PrefetchScalarGridSpec(
            num_scalar_prefetch=1, grid=(1,),
            in_specs=[pl.BlockSpec((N_DEV, N_META, META_BLK), lambda i, me_ref: (0, 0, me_ref[0])),
                      pl.BlockSpec((N_DEV, 8, D_MODEL), lambda i, me_ref: (0, N_META // 8, 0))]
            + [full(a) for a in flat],
            out_specs=[full(grp[0]) for grp in groups for _ in range(4)]
            + [pl.BlockSpec((1, 128), lambda i, me_ref: (0, 0))]),
        out_shape=[jax.ShapeDtypeStruct(grp[0].shape, F32) for grp in groups for _ in range(4)]
        + [jax.ShapeDtypeStruct((1, 128), F32)],
        compiler_params=_params(dimension_semantics=("arbitrary",)),
    )(me, got, got, *flat)
    return [res[4 * k:4 * k + 4] for k in range(4)], res[16]


def kernel(x, meta_tokens, norm_g, w_in, b_f, w_out, final_g, loss_target, m_meta_tokens, m_norm_g, m_w_in, m_b_f, m_w_out, m_final_g, v_meta_tokens, v_norm_g, v_w_in, v_b_f, v_w_out, v_final_g):
    core = lax.axis_index("c")
    me = 4 * lax.axis_index("x") + 2 * lax.axis_index("y") + core
    tables = _tables()

    wt_all, meta_all = _gather_two_level([_slab(w_in[0].T.astype(BF16), me), meta_tokens], name="gather_w_in")
    wt_pad = _regroup_slabs(wt_all)
    meta_full = jnp.transpose(meta_all, (1, 0, 2)).reshape(N_META, D_MODEL)
    h_pad = jnp.concatenate([jnp.zeros((PAD, D_MODEL), F32), meta_full, x[0]], axis=0)
    b_pad = jnp.pad(b_f, ((0, 0), (0, 128 - FOX_HEADS)))

    u, z, zff = _rms_mm_z(h_pad, norm_g, wt_pad)
    bias = _forget_fwd(zff, b_pad)[:FOX_HEADS].reshape(FOX_HEADS, 1, T)
    raw, y = _ret_fwd(z, tables)
    a_f, lse, y, w_out_all = _fox_fwd(z, bias, y, w_out[0].astype(BF16))
    w_out_b = w_out_all.reshape(D_MIX, D_MODEL)
    dout, dout_b, dy, loss_blk, d_final_g = _out_loss_dy(y, w_out_b, h_pad, loss_target[0],
                                                         final_g.reshape(1, D_MODEL))

    d_w_out = _mm_tn(y, dout_b, tm=D_MIX, tn=256, name="mm_dwout")
    dz = _ret_bwd(z, tables, raw, dy)
    dz, dc, got_w_out = _fox_bwd(z, bias, a_f, lse, dy, dz, d_w_out.reshape(N_DEV, WO_BLK, D_MODEL))
    dz, db_f = _forget_bwd(zff, b_pad, dc.reshape(FOX_HEADS, T), dz)

    pair = _dwin_pair_slabs(dz, u, core)
    dh, d_norm_g, got_slabs = _du_rms(dz, wt_pad, h_pad, dout, norm_g, pair)
    got_w_in = lax.dynamic_slice(got_slabs, (0, (W_BLK - W_STRIDE) * me, 0), (N_CHIP, W_BLK, D_MODEL))

    small = jnp.concatenate([
        dh[PAD:CHUNK], d_norm_g, d_final_g, jnp.pad(db_f[:, :FOX_HEADS], ((0, 0), (0, D_MODEL - FOX_HEADS))),
        jnp.pad(loss_blk[0:1], ((0, 0), (0, D_MODEL - 128))),
        jnp.zeros((SMALL_ROWS - N_META - 4, D_MODEL), F32)], axis=0)
    g_w_in, d_w_in, nm_w_in, nv_w_in, got_small = _sum_adamw_w_in(
        got_w_in, w_in[0].T, m_w_in[0].T, v_w_in[0].T, small)
    g_w_out, d_w_out, nm_w_out, nv_w_out = _sum_adamw(got_w_out, w_out[0], m_w_out[0], v_w_out[0], 128, "adamw_w_out")

    row = lambda a: a.reshape(1, D_MODEL)
    (meta_o, norm_o, final_o, bias_o), loss_row = _adamw_small(
        got_small, me.astype(jnp.int32).reshape(1),
        (meta_tokens, m_meta_tokens, v_meta_tokens), (norm_g, m_norm_g, v_norm_g),
        (row(final_g), row(m_final_g), row(v_final_g)), (b_f, m_b_f, v_b_f))
    final_o = [a.reshape(D_MODEL) for a in final_o]

    back = lambda a: a.T[None]
    outs = [[meta_o[k], norm_o[k], back(wk), bias_o[k], ok[None], final_o[k]]
            for k, (wk, ok) in enumerate(zip((g_w_in, d_w_in, nm_w_in, nv_w_in),
                                             (g_w_out, d_w_out, nm_w_out, nv_w_out)))]
    return (loss_row[0, 0], dh[CHUNK:][None], *outs[0], *outs[1], *outs[2], *outs[3])
```

```python
import numpy as np
import jax
import jax.numpy as jnp
from jax import lax
from jax.experimental import pallas as pl
from jax.experimental.pallas import tpu as pltpu

F32 = jnp.float32
BF16 = jnp.bfloat16

N_DEV = 8
N_CHIP = 4
D_MODEL = 1024
SEQ = 2048
N_META = 16
CHUNK = 128
PAD = CHUNK - N_META
T = SEQ + CHUNK
NCHUNK = T // CHUNK
D_MIX = 2048
RET_HEADS = 4
RET_DK = 128
RET_DV = 256
RET_W = 2 * RET_DK + 2 * RET_DV
FOX_HEADS = 16
FOX_D = 64
FOX_PAIRS = FOX_HEADS // 2
FOX_W = 4 * 128
FOX_BASE = RET_HEADS * RET_W
FF_BASE = FOX_BASE + FOX_PAIRS * FOX_W
D_IN = 7184
D_IN_PAD = 7296
W_BLK = D_IN // N_DEV
WO_BLK = D_MIX // N_DEV
META_BLK = D_MODEL // N_DEV
EPS = 1e-6
NEG_INF = -1e30
ROPE_BASE = 10000.0
LOG2E = 1.4426950408889634
LN2 = 0.6931471805599453

ADAM_LR = 0.001
ADAM_B1 = 0.9
ADAM_B2 = 0.999
ADAM_EPS = 1e-08
ADAM_WD = 0.01
ADAM_STEP = 10

SMALL_ROWS = 24
VMEM_LIMIT = 56 * 1024 * 1024
MESH = pl.DeviceIdType.MESH
ANY = pl.BlockSpec(memory_space=pl.ANY)

_NT = (((1,), (1,)), ((), ()))
_TN = (((0,), (0,)), ((), ()))


def _dot(a, b):
    return jnp.dot(a, b, preferred_element_type=F32)


def _dot_nt(a, b):
    return lax.dot_general(a, b, _NT, preferred_element_type=F32)


def _dot_tn(a, b):
    return lax.dot_general(a, b, _TN, preferred_element_type=F32)


def _params(**kw):
    return pltpu.CompilerParams(vmem_limit_bytes=VMEM_LIMIT, **kw)


def _silu_parts(g):
    sig = jax.nn.sigmoid(g)
    return g * sig, sig * (1.0 + g * (1.0 - sig))


def _tables():
    pos = np.arange(T, dtype=np.float32) - PAD
    inv = (ROPE_BASE ** (-np.arange(0, RET_DK, 2, dtype=np.float32) / RET_DK)).astype(np.float32)
    ang = pos[:, None] * inv[None, :]
    cos, sin = np.cos(ang), np.sin(ang)
    cosf = np.concatenate([cos, cos], axis=1).astype(np.float32)
    sins = np.concatenate([-sin, sin], axis=1).astype(np.float32)
    h = np.arange(RET_HEADS, dtype=np.float32)
    log_gamma = np.log1p(-np.exp2(-5.0 - h)).astype(np.float32)
    idx = np.arange(CHUNK, dtype=np.float32)
    diff = idx[:, None] - idx[None, :]
    dmask = np.where(diff[None] >= 0,
                     np.exp(log_gamma[:, None, None] * np.maximum(diff, 0.0)[None]), 0.0)
    zeta = np.exp(log_gamma[:, None] * (CHUNK - 1.0 - idx)[None, :])
    xi = np.exp(log_gamma[:, None] * (idx + 1.0)[None, :])
    cdec = np.exp(log_gamma * CHUNK)
    return (jnp.asarray(cosf), jnp.asarray(sins), jnp.asarray(dmask, F32),
            jnp.asarray(zeta[:, :, None], F32), jnp.asarray(xi[:, :, None], F32),
            jnp.asarray(cdec[:, None, None], F32))


W_STRIDE = 896
W_SLAB = 912
W_EDGE = W_SLAB - W_STRIDE
def _slab(block, me):
    return lax.dynamic_update_slice(jnp.zeros((W_SLAB, block.shape[1]), block.dtype), block,
                                    ((W_BLK - W_STRIDE) * me, 0))


def _join_edges(slabs):
    last = slabs[:, W_STRIDE:]
    first = slabs[:, :W_EDGE] + jnp.concatenate([jnp.zeros_like(last[:1]), last[:-1]], axis=0)
    tail = jnp.pad(last[N_DEV - 1], ((0, 128 - W_EDGE), (0, 0)))
    return lax.dynamic_update_slice(slabs, first, (0, 0, 0)), tail


def _mm_tn(a, b, *, tm, tn, name):
    k, m = a.shape
    n = b.shape[1]
    assert m % tm == 0 and n % tn == 0

    def body(a_ref, b_ref, o_ref):
        o_ref[...] = _dot_tn(a_ref[...], b_ref[...]).astype(BF16)

    return pl.pallas_call(
        body, name=name, grid=(n // tn, m // tm),
        in_specs=[pl.BlockSpec((k, tm), lambda j, i: (0, i)),
                  pl.BlockSpec((k, tn), lambda j, i: (0, j))],
        out_specs=pl.BlockSpec((tm, tn), lambda j, i: (i, j)),
        out_shape=jax.ShapeDtypeStruct((m, n), BF16),
        compiler_params=_params(dimension_semantics=("arbitrary", "arbitrary")),
    )(a, b)


def _piece_spec(base, mult):
    def index(i):
        p = base + mult * i
        return p // 7, p % 7, 0
    return pl.BlockSpec((1, 128, D_MODEL), index)


_RET_PIECES = ((0, 1), (4, 1), (8, 2), (9, 2), (16, 2), (17, 2))
_FOX_PIECES = ((24, 1), (32, 1), (40, 1), (48, 1))


def _rms_z_ret(h_pad, g, slabs):
    def body(h_ref, g_ref, *refs):
        pieces, u_ref, z_ref = refs[:6], refs[6], refs[7]

        @pl.when(pl.program_id(0) == 0)
        def _():
            h = h_ref[...]
            r = lax.rsqrt(jnp.mean(h * h, axis=-1, keepdims=True) + EPS)
            u_ref[...] = (h * r * g_ref[...]).astype(BF16)

        w = jnp.concatenate([p[0] for p in pieces], axis=0)
        z_ref[...] = _dot_nt(u_ref[...], w).astype(BF16)

    whole = pl.BlockSpec((T, D_MODEL), lambda i: (0, 0))
    return pl.pallas_call(
        body, name="rms_z_ret", grid=(RET_HEADS,),
        in_specs=[whole, pl.BlockSpec((1, D_MODEL), lambda i: (0, 0))]
        + [_piece_spec(*bm) for bm in _RET_PIECES],
        out_specs=[whole, pl.BlockSpec((T, RET_W), lambda i: (0, i))],
        out_shape=[jax.ShapeDtypeStruct((T, D_MODEL), BF16),
                   jax.ShapeDtypeStruct((T, D_IN_PAD), BF16)],
        compiler_params=_params(dimension_semantics=("arbitrary",)),
    )(h_pad, g, *([slabs] * 6))


def _z_fox(u, slabs, tail, z):
    def body(u_ref, *refs):
        pieces, t_ref, z_ref, zff_ref = refs[:4], refs[4], refs[6], refs[7]
        u = u_ref[...]
        w = jnp.concatenate([p[0] for p in pieces], axis=0)
        z_ref[...] = _dot_nt(u, w).astype(BF16)

        @pl.when(pl.program_id(0) == 0)
        def _():
            zff_ref[...] = _dot_nt(u, t_ref[...])

    return pl.pallas_call(
        body, name="z_fox", grid=(FOX_PAIRS,),
        in_specs=[pl.BlockSpec((T, D_MODEL), lambda i: (0, 0))] + [_piece_spec(*bm) for bm in _FOX_PIECES]
        + [pl.BlockSpec((128, D_MODEL), lambda i: (0, 0)), ANY],
        out_specs=[_FOX_Z_SPEC, pl.BlockSpec((T, 128), lambda i: (0, 0))],
        out_shape=[jax.ShapeDtypeStruct((T, D_IN_PAD), BF16),
                   jax.ShapeDtypeStruct((T, 128), F32)],
        input_output_aliases={6: 0},
        compiler_params=_params(dimension_semantics=("arbitrary",)),
    )(u, *([slabs] * 4), tail, z)


def _out_loss_dy(y, w_out_b, h_pad, target, g):
    tm = T // 4

    def body(y_ref, w_ref, h_ref, t_hbm, g_ref, d_ref, db_ref, dy_ref, loss_ref, dg_ref, t_buf, t_sem):
        i = pl.program_id(0)
        head = pltpu.make_async_copy(t_hbm.at[pl.ds(0, tm - CHUNK)], t_buf.at[pl.ds(CHUNK, tm - CHUNK)], t_sem)
        rest = pltpu.make_async_copy(t_hbm.at[pl.ds(pl.multiple_of(jnp.maximum(i, 1) * tm - CHUNK, 8), tm)],
                                     t_buf, t_sem)

        @pl.when(i == 0)
        def _():
            t_buf[pl.ds(0, CHUNK), :] = jnp.zeros((CHUNK, D_MODEL), F32)
            head.start()
            loss_ref[...] = jnp.zeros_like(loss_ref)
            dg_ref[...] = jnp.zeros_like(dg_ref)

        pl.when(i > 0)(rest.start)

        w = w_ref[...]
        o = _dot(y_ref[...], w) + h_ref[...]
        pl.when(i == 0)(head.wait)
        pl.when(i > 0)(rest.wait)
        token = lax.broadcasted_iota(jnp.int32, (tm, 1), 0) + i * tm >= CHUNK
        g = g_ref[...]
        r = lax.rsqrt(jnp.mean(o * o, axis=-1, keepdims=True) + EPS)
        xn = o * r
        e = jnp.where(token, xn * g - t_buf[...], 0.0)
        loss_ref[...] += jnp.full(loss_ref.shape, 0.5 / D_MODEL * jnp.sum(e * e), F32)
        do = e * (1.0 / D_MODEL)
        dg_ref[...] += jnp.sum(do * xn, axis=0, keepdims=True)
        dn = do * g
        d = r * (dn - xn * jnp.mean(dn * xn, axis=-1, keepdims=True))
        d_b = d.astype(BF16)
        d_ref[...] = d
        db_ref[...] = d_b
        dy_ref[...] = _dot_nt(d_b, w).astype(BF16)

    tile = pl.BlockSpec((tm, D_MODEL), lambda i: (i, 0))
    wide = pl.BlockSpec((tm, D_MIX), lambda i: (i, 0))
    return pl.pallas_call(
        body, name="out_loss_dy", grid=(T // tm,),
        in_specs=[wide, pl.BlockSpec((D_MIX, D_MODEL), lambda i: (0, 0)), tile, ANY,
                  pl.BlockSpec((1, D_MODEL), lambda i: (0, 0))],
        out_specs=[tile, tile, wide,
                   pl.BlockSpec((8, 128), lambda i: (0, 0)),
                   pl.BlockSpec((1, D_MODEL), lambda i: (0, 0))],
        out_shape=[jax.ShapeDtypeStruct((T, D_MODEL), F32),
                   jax.ShapeDtypeStruct((T, D_MODEL), BF16),
                   jax.ShapeDtypeStruct((T, D_MIX), BF16),
                   jax.ShapeDtypeStruct((8, 128), F32),
                   jax.ShapeDtypeStruct((1, D_MODEL), F32)],
        scratch_shapes=[pltpu.VMEM((tm, D_MODEL), F32), pltpu.SemaphoreType.DMA],
        compiler_params=_params(dimension_semantics=("arbitrary",)),
    )(y, w_out_b, h_pad, target, g)


def _coords():
    return lax.axis_index("x"), lax.axis_index("y"), lax.axis_index("c")


def _flip(v, bit):
    return 1 - v if bit else v


def _peer(x, y, c, r):
    return _flip(x, (r >> 2) & 1), _flip(y, (r >> 1) & 1), _flip(c, r & 1)


def _direct_exchange(ins, outs, send_sems, recv_sems, local_sems, gather, chips_only=False):
    x, y, c = _coords()
    me = 2 * x + y if chips_only else 4 * x + 2 * y + c

    def src(k, to_idx):
        return ins[k] if gather else ins[k].at[to_idx]

    local = [pltpu.make_async_copy(src(k, me), outs[k].at[me], local_sems.at[k])
             for k in range(len(ins))]
    sends, recvs = [], []
    for r in range(1, N_CHIP if chips_only else N_DEV):
        px, py, pc = _peer(x, y, c, 2 * r if chips_only else r)
        peer = 2 * px + py if chips_only else 4 * px + 2 * py + pc
        for k in range(len(ins)):
            sems = dict(send_sem=send_sems.at[k, r - 1], recv_sem=recv_sems.at[k, r - 1],
                        device_id=(px, py, pc), device_id_type=MESH)
            sends.append(pltpu.make_async_remote_copy(src_ref=src(k, peer), dst_ref=outs[k].at[me], **sems))
            recvs.append(pltpu.make_async_remote_copy(src_ref=src(k, peer), dst_ref=outs[k].at[peer], **sems))

    def start():
        for cp in local + sends:
            cp.start()

    def wait():
        for cp in recvs:
            cp.wait_recv()
        for cp in sends:
            cp.wait_send()
        for cp in local:
            cp.wait()

    return start, wait


def _exchange_sems(n_arr, n_peer=N_DEV - 1):
    return [pltpu.SemaphoreType.DMA((n_arr, n_peer)), pltpu.SemaphoreType.DMA((n_arr, n_peer)),
            pltpu.SemaphoreType.DMA((n_arr,))]


def _exchange_shape(a, gather):
    return jax.ShapeDtypeStruct(((N_DEV,) + a.shape) if gather else a.shape, a.dtype)


def _gather_two_level(arrays, name):
    n_arr = len(arrays)

    def body(*refs):
        ins, outs = refs[:n_arr], refs[n_arr:2 * n_arr]
        send_sems, recv_sems, local_sems = refs[2 * n_arr:]
        x, y, c = _coords()

        def slot(k, px, py, pc):
            return outs[k].at[4 * px + 2 * py + pc]

        def routed(core):
            me, sibling = (x, y, core), (x, y, 1 - core)
            xn, yn, dg = (1 - x, y), (x, 1 - y), (1 - x, 1 - y)
            (first, s_first), (second, s_second) = ((xn, 1), (yn, 2)) if core == 0 else ((yn, 2), (xn, 1))

            def copy(k, j, block, to, own=False):
                return pltpu.make_async_remote_copy(
                    src_ref=ins[k] if own else slot(k, *block), dst_ref=slot(k, *block),
                    send_sem=send_sems.at[k, j], recv_sem=recv_sems.at[k, j],
                    device_id=to, device_id_type=MESH)

            local = [pltpu.make_async_copy(ins[k], slot(k, *me), local_sems.at[k]) for k in range(n_arr)]
            sent = []
            for k in range(n_arr):
                sent += [copy(k, 0, me, sibling, True), copy(k, 1, me, (*xn, core), True),
                         copy(k, 2, me, (*yn, core), True)]
            for cp in local + sent:
                cp.start()

            def pass_on(k, j_from, j_to, block, targets):
                copy(k, j_from, block, me).wait_recv()
                for j, to in zip(j_to, targets):
                    cp = copy(k, j, block, to)
                    cp.start()
                    sent.append(cp)

            for k in range(n_arr):
                pass_on(k, s_first, (3, 3 + s_first), (*first, core), ((*second, core), sibling))
            for k in range(n_arr):
                pass_on(k, s_second, (3 + s_second,), (*second, core), (sibling,))
            for k in range(n_arr):
                pass_on(k, 3, (6,), (*dg, core), (sibling,))
            for k in range(n_arr):
                copy(k, 0, sibling, me).wait_recv()
                for j, chip in ((4, xn), (5, yn), (6, dg)):
                    copy(k, j, (*chip, 1 - core), me).wait_recv()
            for cp in sent:
                cp.wait_send()
            for cp in local:
                cp.wait()

        for core in (0, 1):
            pl.when(c == core)(lambda core=core: routed(core))

    return pl.pallas_call(
        body, name=name,
        in_specs=[ANY] * n_arr, out_specs=[ANY] * n_arr,
        out_shape=[_exchange_shape(a, True) for a in arrays],
        scratch_shapes=_exchange_sems(n_arr),
    )(*arrays)


def _piece_columns():
    pos = {}
    for h in range(RET_HEADS):
        for k, p in enumerate((h, 4 + h, 8 + 2 * h, 9 + 2 * h, 16 + 2 * h, 17 + 2 * h)):
            pos[p] = 6 * h + k
    for p in range(FOX_PAIRS):
        for i in range(4):
            pos[24 + 8 * i + p] = 24 + 4 * p + i
    pos[D_IN_PAD // 128 - 1] = D_IN_PAD // 128 - 1
    return np.array([pos[7 * d + j] for d in range(N_DEV) for j in range(8)], np.int32)


def _dwin_pair_slabs(dz, u, core):
    def body(order_ref, cols_ref, *refs):
        pieces, u_ref, pair_ref, theirs_ref = refs[:8], refs[8], refs[9], refs[10]
        send_buf, got_buf, send_sems, recv_sems, load_sem = refs[11:]
        s = pl.program_id(0)
        x, y, c = _coords()
        cols = jnp.concatenate([p[...] for p in pieces], axis=1)
        slab = _dot_tn(cols, u_ref[...])[:W_SLAB]

        def push(q):
            return pltpu.make_async_remote_copy(
                src_ref=send_buf.at[q], dst_ref=theirs_ref.at[q],
                send_sem=send_sems.at[q], recv_sem=recv_sems.at[q],
                device_id=(x, y, 1 - c), device_id_type=MESH)

        for q in range(N_CHIP):
            @pl.when(s == q)
            def _(q=q):
                send_buf[q] = slab.astype(BF16)
                push(q).start()

            @pl.when(s == N_CHIP + q)
            def _(q=q):
                push(q).wait_recv()
                load = pltpu.make_async_copy(theirs_ref.at[q], got_buf, load_sem)
                load.start()
                load.wait()
                pair_ref[0] = (slab + got_buf[...].astype(F32)).astype(BF16)

        @pl.when(s == 2 * N_CHIP - 1)
        def _():
            for q in range(N_CHIP):
                push(q).wait_send()

    order = jnp.concatenate([2 * jnp.arange(N_CHIP) + (1 - core), 2 * jnp.arange(N_CHIP) + core]).astype(jnp.int32)
    piece = lambda j: pl.BlockSpec((T, 128), lambda s, order_ref, cols_ref: (0, cols_ref[order_ref[s] * 8 + j]))
    slabs = jax.ShapeDtypeStruct((N_CHIP, W_SLAB, D_MODEL), BF16)
    pair, _ = pl.pallas_call(
        body, name="dwin_pair_slabs",
        grid_spec=pltpu.PrefetchScalarGridSpec(
            num_scalar_prefetch=2, grid=(2 * N_CHIP,),
            in_specs=[piece(j) for j in range(8)]
            + [pl.BlockSpec((T, D_MODEL), lambda s, order_ref, cols_ref: (0, 0))],
            out_specs=[pl.BlockSpec((1, W_SLAB, D_MODEL),
                                    lambda s, order_ref, cols_ref: (jnp.maximum(s - N_CHIP, 0), 0, 0)), ANY],
            scratch_shapes=[pltpu.VMEM((N_CHIP, W_SLAB, D_MODEL), BF16), pltpu.VMEM((W_SLAB, D_MODEL), BF16),
                            pltpu.SemaphoreType.DMA((N_CHIP,)), pltpu.SemaphoreType.DMA((N_CHIP,)),
                            pltpu.SemaphoreType.DMA]),
        out_shape=[slabs, slabs],
        compiler_params=_params(dimension_semantics=("arbitrary",)),
    )(order, jnp.asarray(_piece_columns()), *([dz] * 8), u)
    return pair


def _du_rms(dz, slabs, tail, h_pad, dout, g, pair_blocks):
    tm = 272
    steps = T // tm
    columns = _piece_columns().reshape(N_DEV, 8)

    def body(dz_ref, w_ref, t_ref, h_ref, d_ref, g_ref, p_ref, dh_ref, dg_ref, got_ref,
             send_sems, recv_sems, local_sems):
        start, wait = _direct_exchange([p_ref], [got_ref], send_sems, recv_sems, local_sems,
                                       False, chips_only=True)
        i = pl.program_id(0)
        pl.when(i == 0)(start)

        @pl.when(i == 0)
        def _():
            dg_ref[...] = jnp.zeros_like(dg_ref)

        du = _dot(dz_ref[:, pl.ds(FF_BASE, 128)], t_ref[...])
        for d in range(N_DEV):
            cols = jnp.concatenate([dz_ref[:, pl.ds(128 * int(columns[d, j]), 128)] for j in range(7)], axis=1)
            du = du + _dot(cols, w_ref[d, pl.ds(0, W_STRIDE), :])
        h = h_ref[...]
        r = lax.rsqrt(jnp.mean(h * h, axis=-1, keepdims=True) + EPS)
        xn = h * r
        dg_ref[...] += jnp.sum(du * xn, axis=0, keepdims=True)
        dn = du * g_ref[...]
        dh_ref[...] = d_ref[...] + r * (dn - xn * jnp.mean(dn * xn, axis=-1, keepdims=True))

        pl.when(i == steps - 1)(wait)

    tile = pl.BlockSpec((tm, D_MODEL), lambda i: (i, 0))
    return pl.pallas_call(
        body, name="du_rms", grid=(steps,),
        in_specs=[pl.BlockSpec((tm, D_IN_PAD), lambda i: (i, 0)),
                  pl.BlockSpec((N_DEV, W_SLAB, D_MODEL), lambda i: (0, 0, 0)),
                  pl.BlockSpec((128, D_MODEL), lambda i: (0, 0)),
                  tile, tile, pl.BlockSpec((1, D_MODEL), lambda i: (0, 0)), ANY],
        out_specs=[tile, pl.BlockSpec((1, D_MODEL), lambda i: (0, 0)), ANY],
        out_shape=[jax.ShapeDtypeStruct((T, D_MODEL), F32),
                   jax.ShapeDtypeStruct((1, D_MODEL), F32),
                   jax.ShapeDtypeStruct(pair_blocks.shape, pair_blocks.dtype)],
        scratch_shapes=_exchange_sems(1, N_CHIP - 1),
        compiler_params=_params(dimension_semantics=("arbitrary",)),
    )(dz, slabs, tail, h_pad, dout, g, pair_blocks)


def _tri(lower):
    r = lax.broadcasted_iota(jnp.int32, (CHUNK, CHUNK), 0)
    c = lax.broadcasted_iota(jnp.int32, (CHUNK, CHUNK), 1)
    return jnp.where((r >= c) if lower else (r <= c), 1.0, 0.0).astype(F32)


def _row_valid(n):
    r = lax.broadcasted_iota(jnp.int32, (CHUNK, 128), 0) + n * CHUNK
    return r >= PAD


_FF_SPEC = pl.BlockSpec((T, 128), lambda i: (0, FF_BASE // 128))
_ZFF_SPEC = pl.BlockSpec((T, 128), lambda i: (0, 0))


def _forget_fwd(z, b_pad):
    def body(z_ref, b_ref, o_ref):
        tri = _tri(True)
        carry = jnp.zeros((1, 128), F32)
        for n in range(NCHUNK):
            rows = pl.ds(n * CHUNK, CHUNK)
            a = z_ref[rows, :] + b_ref[...]
            lf = -(jnp.maximum(-a, 0.0) + jnp.log(1.0 + jnp.exp(-jnp.abs(a))))
            lf = jnp.where(_row_valid(n), lf, 0.0)
            c = jnp.dot(tri, lf, precision=lax.Precision.HIGHEST,
                        preferred_element_type=F32) + carry
            carry = c[CHUNK - 1:CHUNK, :]
            o_ref[:, rows] = jnp.where(_row_valid(n), c * (-LOG2E), NEG_INF).T

    return pl.pallas_call(
        body, name="forget_fwd", grid=(1,),
        in_specs=[_ZFF_SPEC, pl.BlockSpec((1, 128), lambda i: (0, 0))],
        out_specs=pl.BlockSpec((128, T), lambda i: (0, 0)),
        out_shape=jax.ShapeDtypeStruct((128, T), F32),
        compiler_params=_params(dimension_semantics=("arbitrary",)),
    )(z, b_pad)


def _forget_bwd(z, b_pad, dc, dz):
    def body(z_ref, b_ref, dc_ref, dz_in, dff_ref, db_ref):
        tri = _tri(False)
        carry = jnp.zeros((1, 128), F32)
        db = jnp.zeros((1, 128), F32)
        for n in reversed(range(NCHUNK)):
            rows = pl.ds(n * CHUNK, CHUNK)
            dc_blk = jnp.concatenate([dc_ref[:, rows], jnp.zeros((128 - FOX_HEADS, CHUNK), F32)], axis=0).T
            dlf = jnp.dot(tri, dc_blk, precision=lax.Precision.HIGHEST,
                          preferred_element_type=F32) + carry
            carry = dlf[0:1, :]
            a = z_ref[rows, :] + b_ref[...]
            dff = jnp.where(_row_valid(n), dlf * jax.nn.sigmoid(-a), 0.0)
            dff_ref[rows, :] = dff.astype(BF16)
            db = db + jnp.sum(dff, axis=0, keepdims=True)
        db_ref[...] = db

    return pl.pallas_call(
        body, name="forget_bwd", grid=(1,),
        in_specs=[_ZFF_SPEC, pl.BlockSpec((1, 128), lambda i: (0, 0)),
                  pl.BlockSpec((FOX_HEADS, T), lambda i: (0, 0)), ANY],
        out_specs=[_FF_SPEC, pl.BlockSpec((1, 128), lambda i: (0, 0))],
        out_shape=[jax.ShapeDtypeStruct((T, D_IN_PAD), BF16),
                   jax.ShapeDtypeStruct((1, 128), F32)],
        input_output_aliases={3: 0},
        compiler_params=_params(dimension_semantics=("arbitrary",)),
    )(z, b_pad, dc, dz)


FOX_QB = 512
FOX_NQB = SEQ // FOX_QB


def _fox_block(b):
    lo = CHUNK + b * FOX_QB
    return pl.ds(lo, FOX_QB), lo, lo + FOX_QB


def _causal_bias():
    r = lax.broadcasted_iota(jnp.int32, (FOX_QB, FOX_QB), 0)
    c = lax.broadcasted_iota(jnp.int32, (FOX_QB, FOX_QB), 1)
    return jnp.where(c <= r, 0.0, NEG_INF).astype(F32)


def _fox_logits(q_blk, k_all, bias, causal, b):
    _, lo, hi = _fox_block(b)
    s_off = _dot_nt(q_blk, k_all[:lo]) + bias[:, :lo]
    s_dia = _dot_nt(q_blk, k_all[lo:hi]) + (bias[:, lo:hi] + causal)
    return s_off, s_dia


_FOX_Z_SPEC = pl.BlockSpec((T, FOX_W), lambda p: (0, FOX_BASE // FOX_W + p))
_FOX_BIAS_SPEC = pl.BlockSpec((2, 1, T), lambda p: (p, 0, 0))
_FOX_LSE_SPEC = pl.BlockSpec((2, T, 1), lambda p: (p, 0, 0))
_FOX_SCALE = FOX_D ** -0.5
_FOX_QSCALE = _FOX_SCALE * LOG2E


def _fox_fwd(z, bias, y, w_out_blk):
    last = FOX_PAIRS - 1

    def body(z_ref, b_ref, y_in, w_ref, a_ref, lse_ref, y_ref, wall_ref,
             send_sems, recv_sems, local_sems):
        start, wait = _direct_exchange([w_ref], [wall_ref], send_sems, recv_sems, local_sems, True)
        pl.when(pl.program_id(0) == 0)(start)

        causal = _causal_bias()
        a_ref[pl.ds(0, CHUNK), :] = jnp.zeros((CHUNK, 128), F32)
        y_ref[pl.ds(0, CHUNK), :] = jnp.zeros((CHUNK, 128), BF16)
        for j in range(2):
            lanes = pl.ds(j * FOX_D, FOX_D)
            k_all = z_ref[:, pl.ds(128 + j * FOX_D, FOX_D)]
            v_all = z_ref[:, pl.ds(256 + j * FOX_D, FOX_D)]
            bias = b_ref[j]
            lse_ref[j, pl.ds(0, CHUNK), :] = jnp.zeros((CHUNK, 1), F32)
            for b in range(FOX_NQB):
                rows, lo, hi = _fox_block(b)
                q_blk = (z_ref[rows, lanes].astype(F32) * _FOX_QSCALE).astype(BF16)
                s_off, s_dia = _fox_logits(q_blk, k_all, bias, causal, b)
                m = jnp.maximum(jnp.max(s_off, axis=-1, keepdims=True),
                                jnp.max(s_dia, axis=-1, keepdims=True))
                e_off = jnp.exp2(s_off - m)
                e_dia = jnp.exp2(s_dia - m)
                total = jnp.sum(e_off, axis=-1, keepdims=True) + jnp.sum(e_dia, axis=-1, keepdims=True)
                o = (_dot(e_off.astype(BF16), v_all[:lo]) + _dot(e_dia.astype(BF16), v_all[lo:hi])) / total
                a_ref[rows, lanes] = o
                lse_ref[j, rows, :] = m + jnp.log(total) * LOG2E
                gate = _silu_parts(z_ref[rows, pl.ds(384 + j * FOX_D, FOX_D)].astype(F32))[0]
                y_ref[rows, lanes] = (o * gate).astype(BF16)

        pl.when(pl.program_id(0) == last)(wait)

    return pl.pallas_call(
        body, name="fox_fwd", grid=(FOX_PAIRS,),
        in_specs=[_FOX_Z_SPEC, _FOX_BIAS_SPEC, ANY, ANY],
        out_specs=[pl.BlockSpec((T, 128), lambda p: (0, p)), _FOX_LSE_SPEC,
                   pl.BlockSpec((T, 128), lambda p: (0, 8 + p)), ANY],
        out_shape=[jax.ShapeDtypeStruct((T, FOX_HEADS * FOX_D), F32),
                   jax.ShapeDtypeStruct((FOX_HEADS, T, 1), F32),
                   jax.ShapeDtypeStruct((T, D_MIX), BF16),
                   _exchange_shape(w_out_blk, True)],
        input_output_aliases={2: 2},
        scratch_shapes=_exchange_sems(1),
        compiler_params=_params(dimension_semantics=("arbitrary",)),
    )(z, bias, y, w_out_blk)


def _fox_bwd(z, bias, a_f, lse, dy, dz, dwo_blocks):
    last = FOX_PAIRS - 1

    def body(z_ref, b_ref, a_ref, lse_ref, dy_ref, dz_in, dwo_ref, dz_ref, dc_ref, got_ref,
             kv_acc, dc_acc, send_sems, recv_sems, local_sems):
        start, wait = _direct_exchange([dwo_ref], [got_ref], send_sems, recv_sems, local_sems, False)
        pl.when(pl.program_id(0) == 0)(start)

        causal = _causal_bias()
        dz_ref[pl.ds(0, CHUNK), pl.ds(0, 128)] = jnp.zeros((CHUNK, 128), BF16)
        dz_ref[pl.ds(0, CHUNK), pl.ds(384, 128)] = jnp.zeros((CHUNK, 128), BF16)
        dk_rows, dv_rows = pl.ds(0, FOX_D), pl.ds(FOX_D, FOX_D)
        for j in range(2):
            lanes = pl.ds(j * FOX_D, FOX_D)
            k_all = z_ref[:, pl.ds(128 + j * FOX_D, FOX_D)]
            v_all = z_ref[:, pl.ds(256 + j * FOX_D, FOX_D)]
            bias = b_ref[j]
            kv_acc[...] = jnp.zeros_like(kv_acc)
            dc_acc[...] = jnp.zeros_like(dc_acc)
            for b in range(FOX_NQB):
                rows, lo, hi = _fox_block(b)
                off, dia = pl.ds(0, lo), pl.ds(lo, FOX_QB)
                q_blk = (z_ref[rows, lanes].astype(F32) * _FOX_QSCALE).astype(BF16)
                s_off, s_dia = _fox_logits(q_blk, k_all, bias, causal, b)
                lse_blk = lse_ref[j, rows, :]
                p_off, p_dia = jnp.exp2(s_off - lse_blk), jnp.exp2(s_dia - lse_blk)
                sg, dsg = _silu_parts(z_ref[rows, pl.ds(384 + j * FOX_D, FOX_D)].astype(F32))
                dyj = dy_ref[rows, lanes].astype(F32)
                dz_ref[rows, pl.ds(384 + j * FOX_D, FOX_D)] = (dyj * a_ref[rows, lanes] * dsg).astype(BF16)
                do_b = (dyj * sg).astype(BF16)
                dp_off = _dot_nt(do_b, v_all[:lo])
                dp_dia = _dot_nt(do_b, v_all[lo:hi])
                d = (jnp.sum(p_off * dp_off, axis=-1, keepdims=True)
                     + jnp.sum(p_dia * dp_dia, axis=-1, keepdims=True))
                ds_off = p_off * (dp_off - d)
                ds_dia = p_dia * (dp_dia - d)
                dc_acc[:, off] -= jnp.sum(ds_off, axis=0, keepdims=True)
                dc_acc[:, dia] -= jnp.sum(ds_dia, axis=0, keepdims=True)
                ds_off_b, ds_dia_b = ds_off.astype(BF16), ds_dia.astype(BF16)
                dq = _dot(ds_off_b, k_all[:lo]) + _dot(ds_dia_b, k_all[lo:hi])
                dz_ref[rows, lanes] = (dq * _FOX_SCALE).astype(BF16)
                kv_acc[dk_rows, off] += _dot_tn(q_blk, ds_off_b)
                kv_acc[dk_rows, dia] += _dot_tn(q_blk, ds_dia_b)
                kv_acc[dv_rows, off] += _dot_tn(do_b, p_off.astype(BF16))
                kv_acc[dv_rows, dia] += _dot_tn(do_b, p_dia.astype(BF16))
            for n in range(NCHUNK):
                rows = pl.ds(n * CHUNK, CHUNK)
                both = kv_acc[:, rows].T
                dz_ref[rows, pl.ds(128 + j * FOX_D, FOX_D)] = (both[:, :FOX_D] * LN2).astype(BF16)
                dz_ref[rows, pl.ds(256 + j * FOX_D, FOX_D)] = both[:, FOX_D:].astype(BF16)
            dc_ref[j] = dc_acc[...]

        pl.when(pl.program_id(0) == last)(wait)

    col = lambda base: pl.BlockSpec((T, 128), lambda p: (0, base + p))
    return pl.pallas_call(
        body, name="fox_bwd", grid=(FOX_PAIRS,),
        in_specs=[_FOX_Z_SPEC, _FOX_BIAS_SPEC, col(0), _FOX_LSE_SPEC, col(8), ANY, ANY],
        out_specs=[_FOX_Z_SPEC, _FOX_BIAS_SPEC, ANY],
        out_shape=[jax.ShapeDtypeStruct((T, D_IN_PAD), BF16),
                   jax.ShapeDtypeStruct((FOX_HEADS, 1, T), F32),
                   _exchange_shape(dwo_blocks, False)],
        input_output_aliases={5: 0},
        scratch_shapes=[pltpu.VMEM((2 * FOX_D, T), F32), pltpu.VMEM((1, T), F32)] + _exchange_sems(1),
        compiler_params=_params(dimension_semantics=("arbitrary",)),
    )(z, bias, a_f, lse, dy, dz, dwo_blocks)


def _rot(x, cosf, sins):
    return x * cosf + pltpu.roll(x, RET_DK // 2, 1) * sins


def _rot_t(d, cosf, sins):
    return d * cosf - pltpu.roll(d, RET_DK // 2, 1) * sins


_RET_Z_SPEC = pl.BlockSpec((T, RET_W), lambda h: (0, h))
_RET_TABLE_SPECS = [
    pl.BlockSpec((T, RET_DK), lambda h: (0, 0)),
    pl.BlockSpec((T, RET_DK), lambda h: (0, 0)),
    pl.BlockSpec((1, CHUNK, CHUNK), lambda h: (h, 0, 0)),
    pl.BlockSpec((1, CHUNK, 1), lambda h: (h, 0, 0)),
    pl.BlockSpec((1, CHUNK, 1), lambda h: (h, 0, 0)),
    pl.BlockSpec((1, 1, 1), lambda h: (h, 0, 0)),
]
_RQ, _RK = pl.ds(0, RET_DK), pl.ds(RET_DK, RET_DK)
_RV, _RG = pl.ds(2 * RET_DK, RET_DV), pl.ds(2 * RET_DK + RET_DV, RET_DV)
_RET_KSCALE = RET_DK ** -0.5


def _ret_fwd(z, tables):
    def body(z_ref, cos_ref, sin_ref, dm_ref, zeta_ref, xi_ref, cd_ref, raw_ref, y_ref):
        dmask, zeta, xi, cdec = dm_ref[0], zeta_ref[0], xi_ref[0], cd_ref[0]
        state = jnp.zeros((RET_DK, RET_DV), F32)
        for n in range(NCHUNK):
            rows = pl.ds(n * CHUNK, CHUNK)
            cosf, sins = cos_ref[rows, :], sin_ref[rows, :]
            qr = _rot(z_ref[rows, _RQ].astype(F32), cosf, sins)
            kr_b = (_rot(z_ref[rows, _RK].astype(F32), cosf, sins) * _RET_KSCALE).astype(BF16)
            v_b = z_ref[rows, _RV]
            a = _dot_nt(qr.astype(BF16), kr_b) * dmask
            out = _dot(a.astype(BF16), v_b) + _dot((qr * xi).astype(BF16), state.astype(BF16))
            state = state * cdec + _dot_tn(kr_b, (v_b.astype(F32) * zeta).astype(BF16))
            raw_ref[rows, :] = out
            r = lax.rsqrt(jnp.mean(out * out, axis=-1, keepdims=True) + EPS)
            y_ref[rows, :] = (out * r * _silu_parts(z_ref[rows, _RG].astype(F32))[0]).astype(BF16)

    wide = pl.BlockSpec((T, RET_DV), lambda h: (0, h))
    return pl.pallas_call(
        body, name="ret_fwd", grid=(RET_HEADS,),
        in_specs=[_RET_Z_SPEC] + _RET_TABLE_SPECS,
        out_specs=[wide, wide],
        out_shape=[jax.ShapeDtypeStruct((T, RET_HEADS * RET_DV), F32),
                   jax.ShapeDtypeStruct((T, D_MIX), BF16)],
        compiler_params=_params(dimension_semantics=("arbitrary",)),
    )(z, *tables)


def _ret_bwd(z, tables, raw, dy):
    def body(z_ref, cos_ref, sin_ref, dm_ref, zeta_ref, xi_ref, cd_ref, raw_ref, dy_ref,
             dz_ref, st_ref):
        dmask, zeta, xi, cdec = dm_ref[0], zeta_ref[0], xi_ref[0], cd_ref[0]

        def rotated(n):
            rows = pl.ds(n * CHUNK, CHUNK)
            cosf, sins = cos_ref[rows, :], sin_ref[rows, :]
            qr = _rot(z_ref[rows, _RQ].astype(F32), cosf, sins)
            kr_b = (_rot(z_ref[rows, _RK].astype(F32), cosf, sins) * _RET_KSCALE).astype(BF16)
            return rows, cosf, sins, qr, kr_b

        state = jnp.zeros((RET_DK, RET_DV), F32)
        for n in range(NCHUNK):
            st_ref[n] = state.astype(BF16)
            if n + 1 < NCHUNK:
                rows, _, _, _, kr_b = rotated(n)
                state = state * cdec + _dot_tn(kr_b, (z_ref[rows, _RV].astype(F32) * zeta).astype(BF16))

        grad_state = jnp.zeros((RET_DK, RET_DV), F32)
        for n in reversed(range(NCHUNK)):
            rows, cosf, sins, qr, kr_b = rotated(n)
            qr_b = qr.astype(BF16)
            v_b = z_ref[rows, _RV]
            gs_b = grad_state.astype(BF16)
            o = raw_ref[rows, :]
            r = lax.rsqrt(jnp.mean(o * o, axis=-1, keepdims=True) + EPS)
            hn = o * r
            sg, dsg = _silu_parts(z_ref[rows, _RG].astype(F32))
            dyn = dy_ref[rows, :].astype(F32)
            dz_ref[rows, _RG] = (dyn * hn * dsg).astype(BF16)
            dhn = dyn * sg
            do_b = (r * (dhn - hn * jnp.mean(dhn * hn, axis=-1, keepdims=True))).astype(BF16)
            a_b = (_dot_nt(qr_b, kr_b) * dmask).astype(BF16)
            da_b = (_dot_nt(do_b, v_b) * dmask).astype(BF16)
            dqr = _dot(da_b, kr_b) + xi * _dot_nt(do_b, st_ref[n])
            dkr = _dot_tn(da_b, qr_b) + zeta * _dot_nt(v_b, gs_b)
            dv = _dot_tn(a_b, do_b) + zeta * _dot(kr_b, gs_b)
            grad_state = grad_state * cdec + _dot_tn((qr * xi).astype(BF16), do_b)
            dz_ref[rows, _RQ] = _rot_t(dqr, cosf, sins).astype(BF16)
            dz_ref[rows, _RK] = (_rot_t(dkr, cosf, sins) * _RET_KSCALE).astype(BF16)
            dz_ref[rows, _RV] = dv.astype(BF16)

    wide = pl.BlockSpec((T, RET_DV), lambda h: (0, h))
    return pl.pallas_call(
        body, name="ret_bwd", grid=(RET_HEADS,),
        in_specs=[_RET_Z_SPEC] + _RET_TABLE_SPECS + [wide, wide],
        out_specs=_RET_Z_SPEC,
        out_shape=jax.ShapeDtypeStruct((T, D_IN_PAD), BF16),
        scratch_shapes=[pltpu.VMEM((NCHUNK, RET_DK, RET_DV), BF16)],
        compiler_params=_params(dimension_semantics=("arbitrary",)),
    )(z, *tables, raw, dy)


def _adamw(w, g, m, v):
    m = ADAM_B1 * m + (1.0 - ADAM_B1) * g
    v = ADAM_B2 * v + (1.0 - ADAM_B2) * (g * g)
    m_hat = m / (1.0 - ADAM_B1 ** ADAM_STEP)
    v_hat = v / (1.0 - ADAM_B2 ** ADAM_STEP)
    delta = -ADAM_LR * (m_hat / (jnp.sqrt(v_hat) + ADAM_EPS) + ADAM_WD * w)
    return delta, m, v


def _sum_adamw(parts, w, m, v, rows, name):
    _, r_tot, cols = parts.shape
    assert r_tot % rows == 0

    def body(p_ref, w_ref, m_ref, v_ref, g_ref, d_ref, nm_ref, nv_ref):
        g = p_ref[0].astype(F32)
        for d in range(1, N_DEV):
            g = g + p_ref[d].astype(F32)
        delta, nm, nv = _adamw(w_ref[...], g, m_ref[...], v_ref[...])
        g_ref[...] = g
        d_ref[...] = delta
        nm_ref[...] = nm
        nv_ref[...] = nv

    blk = pl.BlockSpec((rows, cols), lambda i: (i, 0))
    return pl.pallas_call(
        body, name=name, grid=(r_tot // rows,),
        in_specs=[pl.BlockSpec((N_DEV, rows, cols), lambda i: (0, i, 0)), blk, blk, blk],
        out_specs=[blk] * 4,
        out_shape=[jax.ShapeDtypeStruct((r_tot, cols), F32)] * 4,
        compiler_params=_params(dimension_semantics=("arbitrary",)),
    )(parts, w, m, v)


def _sum_adamw_w_in(parts, w, m, v, small):
    n_part, r, c = parts.shape
    steps = c // 128

    def body(p_ref, w_ref, m_ref, v_ref, s_ref, g_ref, d_ref, nm_ref, nv_ref, got_ref,
             send_sems, recv_sems, local_sems):
        start, wait = _direct_exchange([s_ref], [got_ref], send_sems, recv_sems, local_sems, True)
        pl.when(pl.program_id(0) == 0)(start)
        g = p_ref[0].astype(F32)
        for d in range(1, n_part):
            g = g + p_ref[d].astype(F32)
        delta, nm, nv = _adamw(w_ref[...], g, m_ref[...], v_ref[...])
        g_ref[...] = g
        d_ref[...] = delta
        nm_ref[...] = nm
        nv_ref[...] = nv
        pl.when(pl.program_id(0) == steps - 1)(wait)

    blk = pl.BlockSpec((r, 128), lambda i: (0, i))
    return pl.pallas_call(
        body, name="adamw_w_in", grid=(steps,),
        in_specs=[pl.BlockSpec((n_part, r, 128), lambda i: (0, 0, i)), blk, blk, blk, ANY],
        out_specs=[blk] * 4 + [ANY],
        out_shape=[jax.ShapeDtypeStruct((r, c), F32)] * 4 + [_exchange_shape(small, True)],
        scratch_shapes=_exchange_sems(1),
        compiler_params=_params(dimension_semantics=("arbitrary",)),
    )(parts, w, m, v, small)


def _adamw_small(got, me, metas, norms, finals, biases):
    def body(me_ref, gm_ref, gr_ref, *refs):
        ins, outs = refs[:12], refs[12:]
        g_meta, g_rest = gm_ref[0], gr_ref[0]
        for d in range(1, N_DEV):
            g_meta, g_rest = g_meta + gm_ref[d], g_rest + gr_ref[d]
        grads = [g_meta, g_rest[0:1], g_rest[1:2], g_rest[2:3, :FOX_HEADS]]
        for k, g in enumerate(grads):
            w_ref, m_ref, v_ref = ins[3 * k:3 * k + 3]
            delta, new_m, new_v = _adamw(w_ref[...], g, m_ref[...], v_ref[...])
            for o_ref, val in zip(outs[4 * k:4 * k + 4], (g, delta, new_m, new_v)):
                o_ref[...] = val
        outs[16][...] = g_rest[3:4, :128]

    groups = (metas, norms, finals, biases)
    full = lambda a: pl.BlockSpec(a.shape, lambda i, me_ref: (0,) * a.ndim)
    flat = [a for grp in groups for a in grp]
    res = pl.pallas_call(
        body, name="adamw_small",
        grid_spec=pltpu.PrefetchScalarGridSpec(
            num_scalar_prefetch=1, grid=(1,),
            in_specs=[pl.BlockSpec((N_DEV, N_META, META_BLK), lambda i, me_ref: (0, 0, me_ref[0])),
                      pl.BlockSpec((N_DEV, 8, D_MODEL), lambda i, me_ref: (0, N_META // 8, 0))]
            + [full(a) for a in flat],
            out_specs=[full(grp[0]) for grp in groups for _ in range(4)]
            + [pl.BlockSpec((1, 128), lambda i, me_ref: (0, 0))]),
        out_shape=[jax.ShapeDtypeStruct(grp[0].shape, F32) for grp in groups for _ in range(4)]
        + [jax.ShapeDtypeStruct((1, 128), F32)],
        compiler_params=_params(dimension_semantics=("arbitrary",)),
    )(me, got, got, *flat)
    return [res[4 * k:4 * k + 4] for k in range(4)], res[16]


def kernel(x, meta_tokens, norm_g, w_in, b_f, w_out, final_g, loss_target, m_meta_tokens, m_norm_g, m_w_in, m_b_f, m_w_out, m_final_g, v_meta_tokens, v_norm_g, v_w_in, v_b_f, v_w_out, v_final_g):
    core = lax.axis_index("c")
    me = 4 * lax.axis_index("x") + 2 * lax.axis_index("y") + core
    tables = _tables()

    wt_all, meta_all = _gather_two_level([_slab(w_in[0].T.astype(BF16), me), meta_tokens], name="gather_w_in")
    slabs, tail = _join_edges(wt_all)
    meta_full = jnp.transpose(meta_all, (1, 0, 2)).reshape(N_META, D_MODEL)
    h_pad = jnp.concatenate([jnp.zeros((PAD, D_MODEL), F32), meta_full, x[0]], axis=0)
    b_pad = jnp.pad(b_f, ((0, 0), (0, 128 - FOX_HEADS)))

    u, z = _rms_z_ret(h_pad, norm_g, slabs)
    z, zff = _z_fox(u, slabs, tail, z)
    bias = _forget_fwd(zff, b_pad)[:FOX_HEADS].reshape(FOX_HEADS, 1, T)
    raw, y = _ret_fwd(z, tables)
    a_f, lse, y, w_out_all = _fox_fwd(z, bias, y, w_out[0].astype(BF16))
    w_out_b = w_out_all.reshape(D_MIX, D_MODEL)
    dout, dout_b, dy, loss_blk, d_final_g = _out_loss_dy(y, w_out_b, h_pad, loss_target[0],
                                                         final_g.reshape(1, D_MODEL))

    d_w_out = _mm_tn(y, dout_b, tm=D_MIX, tn=256, name="mm_dwout")
    dz = _ret_bwd(z, tables, raw, dy)
    dz, dc, got_w_out = _fox_bwd(z, bias, a_f, lse, dy, dz, d_w_out.reshape(N_DEV, WO_BLK, D_MODEL))
    dz, db_f = _forget_bwd(zff, b_pad, dc.reshape(FOX_HEADS, T), dz)

    pair = _dwin_pair_slabs(dz, u, core)
    dh, d_norm_g, got_slabs = _du_rms(dz, slabs, tail, h_pad, dout, norm_g, pair)
    got_w_in = lax.dynamic_slice(got_slabs, (0, (W_BLK - W_STRIDE) * me, 0), (N_CHIP, W_BLK, D_MODEL))

    small = jnp.concatenate([
        dh[PAD:CHUNK], d_norm_g, d_final_g, jnp.pad(db_f[:, :FOX_HEADS], ((0, 0), (0, D_MODEL - FOX_HEADS))),
        jnp.pad(loss_blk[0:1], ((0, 0), (0, D_MODEL - 128))),
        jnp.zeros((SMALL_ROWS - N_META - 4, D_MODEL), F32)], axis=0)
    g_w_in, d_w_in, nm_w_in, nv_w_in, got_small = _sum_adamw_w_in(
        got_w_in, w_in[0].T, m_w_in[0].T, v_w_in[0].T, small)
    g_w_out, d_w_out, nm_w_out, nv_w_out = _sum_adamw(got_w_out, w_out[0], m_w_out[0], v_w_out[0], 128, "adamw_w_out")

    row = lambda a: a.reshape(1, D_MODEL)
    (meta_o, norm_o, final_o, bias_o), loss_row = _adamw_small(
        got_small, me.astype(jnp.int32).reshape(1),
        (meta_tokens, m_meta_tokens, v_meta_tokens), (norm_g, m_norm_g, v_norm_g),
        (row(final_g), row(m_final_g), row(v_final_g)), (b_f, m_b_f, v_b_f))
    final_o = [a.reshape(D_MODEL) for a in final_o]

    back = lambda a: a.T[None]
    outs = [[meta_o[k], norm_o[k], back(wk), bias_o[k], ok[None], final_o[k]]
            for k, (wk, ok) in enumerate(zip((g_w_in, d_w_in, nm_w_in, nv_w_in),
                                             (g_w_out, d_w_out, nm_w_out, nv_w_out)))]
    return (loss_row[0, 0], dh[CHUNK:][None], *outs[0], *outs[1], *outs[2], *outs[3])
```

```python
import numpy as np
import jax
import jax.numpy as jnp
from jax import lax
from jax.experimental import pallas as pl
from jax.experimental.pallas import tpu as pltpu

F32 = jnp.float32
BF16 = jnp.bfloat16

N_DEV = 8
N_CHIP = 4
D_MODEL = 1024
SEQ = 2048
N_META = 16
CHUNK = 128
PAD = CHUNK - N_META
T = SEQ + CHUNK
NCHUNK = T // CHUNK
D_MIX = 2048
RET_HEADS = 4
RET_DK = 128
RET_DV = 256
RET_W = 2 * RET_DK + 2 * RET_DV
FOX_HEADS = 16
FOX_D = 64
FOX_PAIRS = FOX_HEADS // 2
FOX_W = 4 * 128
FOX_BASE = RET_HEADS * RET_W
FF_BASE = FOX_BASE + FOX_PAIRS * FOX_W
D_IN = 7184
D_IN_PAD = 7296
W_BLK = D_IN // N_DEV
WO_BLK = D_MIX // N_DEV
META_BLK = D_MODEL // N_DEV
EPS = 1e-6
NEG_INF = -1e30
ROPE_BASE = 10000.0
LOG2E = 1.4426950408889634
LN2 = 0.6931471805599453

ADAM_LR = 0.001
ADAM_B1 = 0.9
ADAM_B2 = 0.999
ADAM_EPS = 1e-08
ADAM_WD = 0.01
ADAM_STEP = 10

SMALL_ROWS = 24
VMEM_LIMIT = 56 * 1024 * 1024
MESH = pl.DeviceIdType.MESH
ANY = pl.BlockSpec(memory_space=pl.ANY)

_NT = (((1,), (1,)), ((), ()))
_TN = (((0,), (0,)), ((), ()))


def _dot(a, b):
    return jnp.dot(a, b, preferred_element_type=F32)


def _dot_nt(a, b):
    return lax.dot_general(a, b, _NT, preferred_element_type=F32)


def _dot_tn(a, b):
    return lax.dot_general(a, b, _TN, preferred_element_type=F32)


def _params(**kw):
    return pltpu.CompilerParams(vmem_limit_bytes=VMEM_LIMIT, **kw)


def _silu_parts(g):
    sig = jax.nn.sigmoid(g)
    return g * sig, sig * (1.0 + g * (1.0 - sig))


def _tables():
    pos = np.arange(T, dtype=np.float32) - PAD
    inv = (ROPE_BASE ** (-np.arange(0, RET_DK, 2, dtype=np.float32) / RET_DK)).astype(np.float32)
    ang = pos[:, None] * inv[None, :]
    cos, sin = np.cos(ang), np.sin(ang)
    cosf = np.concatenate([cos, cos], axis=1).astype(np.float32)
    sins = np.concatenate([-sin, sin], axis=1).astype(np.float32)
    h = np.arange(RET_HEADS, dtype=np.float32)
    log_gamma = np.log1p(-np.exp2(-5.0 - h)).astype(np.float32)
    idx = np.arange(CHUNK, dtype=np.float32)
    diff = idx[:, None] - idx[None, :]
    dmask = np.where(diff[None] >= 0,
                     np.exp(log_gamma[:, None, None] * np.maximum(diff, 0.0)[None]), 0.0)
    zeta = np.exp(log_gamma[:, None] * (CHUNK - 1.0 - idx)[None, :])
    xi = np.exp(log_gamma[:, None] * (idx + 1.0)[None, :])
    cdec = np.exp(log_gamma * CHUNK)
    return (jnp.asarray(cosf), jnp.asarray(sins), jnp.asarray(dmask, F32),
            jnp.asarray(zeta[:, :, None], F32), jnp.asarray(xi[:, :, None], F32),
            jnp.asarray(cdec[:, None, None], F32))


W_STRIDE = 896
W_SLAB = 912
W_EDGE = W_SLAB - W_STRIDE
def _slab(block, me):
    return lax.dynamic_update_slice(jnp.zeros((W_SLAB, block.shape[1]), block.dtype), block,
                                    ((W_BLK - W_STRIDE) * me, 0))


def _join_edges(slabs):
    last = slabs[:, W_STRIDE:]
    first = slabs[:, :W_EDGE] + jnp.concatenate([jnp.zeros_like(last[:1]), last[:-1]], axis=0)
    tail = jnp.pad(last[N_DEV - 1], ((0, 128 - W_EDGE), (0, 0)))
    return lax.dynamic_update_slice(slabs, first, (0, 0, 0)), tail


def _mm_tn(a, b, *, tm, tn, name):
    k, m = a.shape
    n = b.shape[1]
    assert m % tm == 0 and n % tn == 0

    def body(a_ref, b_ref, o_ref):
        o_ref[...] = _dot_tn(a_ref[...], b_ref[...]).astype(BF16)

    return pl.pallas_call(
        body, name=name, grid=(n // tn, m // tm),
        in_specs=[pl.BlockSpec((k, tm), lambda j, i: (0, i)),
                  pl.BlockSpec((k, tn), lambda j, i: (0, j))],
        out_specs=pl.BlockSpec((tm, tn), lambda j, i: (i, j)),
        out_shape=jax.ShapeDtypeStruct((m, n), BF16),
        compiler_params=_params(dimension_semantics=("arbitrary", "arbitrary")),
    )(a, b)


def _piece_spec(base, mult):
    def index(i):
        p = base + mult * i
        return p // 7, p % 7, 0
    return pl.BlockSpec((1, 128, D_MODEL), index)


_RET_PIECES = ((0, 1), (4, 1), (8, 2), (9, 2), (16, 2), (17, 2))
_FOX_PIECES = ((24, 1), (32, 1), (40, 1), (48, 1))


def _rms_z_ret(h_pad, g, slabs):
    def body(h_ref, g_ref, *refs):
        pieces, u_ref, z_ref = refs[:6], refs[6], refs[7]

        @pl.when(pl.program_id(0) == 0)
        def _():
            h = h_ref[...]
            r = lax.rsqrt(jnp.mean(h * h, axis=-1, keepdims=True) + EPS)
            u_ref[...] = (h * r * g_ref[...]).astype(BF16)

        w = jnp.concatenate([p[0] for p in pieces], axis=0)
        z_ref[...] = _dot_nt(u_ref[...], w).astype(BF16)

    whole = pl.BlockSpec((T, D_MODEL), lambda i: (0, 0))
    return pl.pallas_call(
        body, name="rms_z_ret", grid=(RET_HEADS,),
        in_specs=[whole, pl.BlockSpec((1, D_MODEL), lambda i: (0, 0))]
        + [_piece_spec(*bm) for bm in _RET_PIECES],
        out_specs=[whole, pl.BlockSpec((T, RET_W), lambda i: (0, i))],
        out_shape=[jax.ShapeDtypeStruct((T, D_MODEL), BF16),
                   jax.ShapeDtypeStruct((T, D_IN_PAD), BF16)],
        compiler_params=_params(dimension_semantics=("arbitrary",)),
    )(h_pad, g, *([slabs] * 6))


def _z_fox(u, slabs, tail, z):
    def body(u_ref, *refs):
        pieces, t_ref, z_ref, zff_ref = refs[:4], refs[4], refs[6], refs[7]
        u = u_ref[...]
        w = jnp.concatenate([p[0] for p in pieces], axis=0)
        z_ref[...] = _dot_nt(u, w).astype(BF16)

        @pl.when(pl.program_id(0) == 0)
        def _():
            zff_ref[...] = _dot_nt(u, t_ref[...])

    return pl.pallas_call(
        body, name="z_fox", grid=(FOX_PAIRS,),
        in_specs=[pl.BlockSpec((T, D_MODEL), lambda i: (0, 0))] + [_piece_spec(*bm) for bm in _FOX_PIECES]
        + [pl.BlockSpec((128, D_MODEL), lambda i: (0, 0)), ANY],
        out_specs=[_FOX_Z_SPEC, pl.BlockSpec((T, 128), lambda i: (0, 0))],
        out_shape=[jax.ShapeDtypeStruct((T, D_IN_PAD), BF16),
                   jax.ShapeDtypeStruct((T, 128), F32)],
        input_output_aliases={6: 0},
        compiler_params=_params(dimension_semantics=("arbitrary",)),
    )(u, *([slabs] * 4), tail, z)


def _out_loss_dy(y, w_out_b, h_pad, target, g):
    tm = T // 4

    def body(y_ref, w_ref, h_ref, t_hbm, g_ref, d_ref, db_ref, dy_ref, loss_ref, dg_ref, t_buf, t_sem):
        i = pl.program_id(0)
        head = pltpu.make_async_copy(t_hbm.at[pl.ds(0, tm - CHUNK)], t_buf.at[pl.ds(CHUNK, tm - CHUNK)], t_sem)
        rest = pltpu.make_async_copy(t_hbm.at[pl.ds(pl.multiple_of(jnp.maximum(i, 1) * tm - CHUNK, 8), tm)],
                                     t_buf, t_sem)

        @pl.when(i == 0)
        def _():
            t_buf[pl.ds(0, CHUNK), :] = jnp.zeros((CHUNK, D_MODEL), F32)
            head.start()
            loss_ref[...] = jnp.zeros_like(loss_ref)
            dg_ref[...] = jnp.zeros_like(dg_ref)

        pl.when(i > 0)(rest.start)

        w = w_ref[...]
        o = _dot(y_ref[...], w) + h_ref[...]
        pl.when(i == 0)(head.wait)
        pl.when(i > 0)(rest.wait)
        token = lax.broadcasted_iota(jnp.int32, (tm, 1), 0) + i * tm >= CHUNK
        g = g_ref[...]
        r = lax.rsqrt(jnp.mean(o * o, axis=-1, keepdims=True) + EPS)
        xn = o * r
        e = jnp.where(token, xn * g - t_buf[...], 0.0)
        loss_ref[...] += jnp.full(loss_ref.shape, 0.5 / D_MODEL * jnp.sum(e * e), F32)
        do = e * (1.0 / D_MODEL)
        dg_ref[...] += jnp.sum(do * xn, axis=0, keepdims=True)
        dn = do * g
        d = r * (dn - xn * jnp.mean(dn * xn, axis=-1, keepdims=True))
        d_b = d.astype(BF16)
        d_ref[...] = d
        db_ref[...] = d_b
        dy_ref[...] = _dot_nt(d_b, w).astype(BF16)

    tile = pl.BlockSpec((tm, D_MODEL), lambda i: (i, 0))
    wide = pl.BlockSpec((tm, D_MIX), lambda i: (i, 0))
    return pl.pallas_call(
        body, name="out_loss_dy", grid=(T // tm,),
        in_specs=[wide, pl.BlockSpec((D_MIX, D_MODEL), lambda i: (0, 0)), tile, ANY,
                  pl.BlockSpec((1, D_MODEL), lambda i: (0, 0))],
        out_specs=[tile, tile, wide,
                   pl.BlockSpec((8, 128), lambda i: (0, 0)),
                   pl.BlockSpec((1, D_MODEL), lambda i: (0, 0))],
        out_shape=[jax.ShapeDtypeStruct((T, D_MODEL), F32),
                   jax.ShapeDtypeStruct((T, D_MODEL), BF16),
                   jax.ShapeDtypeStruct((T, D_MIX), BF16),
                   jax.ShapeDtypeStruct((8, 128), F32),
                   jax.ShapeDtypeStruct((1, D_MODEL), F32)],
        scratch_shapes=[pltpu.VMEM((tm, D_MODEL), F32), pltpu.SemaphoreType.DMA],
        compiler_params=_params(dimension_semantics=("arbitrary",)),
    )(y, w_out_b, h_pad, target, g)


def _coords():
    return lax.axis_index("x"), lax.axis_index("y"), lax.axis_index("c")


def _flip(v, bit):
    return 1 - v if bit else v


def _peer(x, y, c, r):
    return _flip(x, (r >> 2) & 1), _flip(y, (r >> 1) & 1), _flip(c, r & 1)


def _direct_exchange(ins, outs, send_sems, recv_sems, local_sems, gather, chips_only=False):
    x, y, c = _coords()
    me = 2 * x + y if chips_only else 4 * x + 2 * y + c

    def src(k, to_idx):
        return ins[k] if gather else ins[k].at[to_idx]

    local = [pltpu.make_async_copy(src(k, me), outs[k].at[me], local_sems.at[k])
             for k in range(len(ins))]
    sends, recvs = [], []
    for r in range(1, N_CHIP if chips_only else N_DEV):
        px, py, pc = _peer(x, y, c, 2 * r if chips_only else r)
        peer = 2 * px + py if chips_only else 4 * px + 2 * py + pc
        for k in range(len(ins)):
            sems = dict(send_sem=send_sems.at[k, r - 1], recv_sem=recv_sems.at[k, r - 1],
                        device_id=(px, py, pc), device_id_type=MESH)
            sends.append(pltpu.make_async_remote_copy(src_ref=src(k, peer), dst_ref=outs[k].at[me], **sems))
            recvs.append(pltpu.make_async_remote_copy(src_ref=src(k, peer), dst_ref=outs[k].at[peer], **sems))

    def start():
        for cp in local + sends:
            cp.start()

    def wait():
        for cp in recvs:
            cp.wait_recv()
        for cp in sends:
            cp.wait_send()
        for cp in local:
            cp.wait()

    return start, wait


def _exchange_sems(n_arr, n_peer=N_DEV - 1):
    return [pltpu.SemaphoreType.DMA((n_arr, n_peer)), pltpu.SemaphoreType.DMA((n_arr, n_peer)),
            pltpu.SemaphoreType.DMA((n_arr,))]


def _exchange_shape(a, gather):
    return jax.ShapeDtypeStruct(((N_DEV,) + a.shape) if gather else a.shape, a.dtype)


def _gather_two_level(arrays, name):
    n_arr = len(arrays)

    def body(*refs):
        ins, outs = refs[:n_arr], refs[n_arr:2 * n_arr]
        send_sems, recv_sems, local_sems = refs[2 * n_arr:]
        x, y, c = _coords()

        def slot(k, px, py, pc):
            return outs[k].at[4 * px + 2 * py + pc]

        def routed(core):
            me, sibling = (x, y, core), (x, y, 1 - core)
            xn, yn, dg = (1 - x, y), (x, 1 - y), (1 - x, 1 - y)
            (first, s_first), (second, s_second) = ((xn, 1), (yn, 2)) if core == 0 else ((yn, 2), (xn, 1))

            def copy(k, j, block, to, own=False):
                return pltpu.make_async_remote_copy(
                    src_ref=ins[k] if own else slot(k, *block), dst_ref=slot(k, *block),
                    send_sem=send_sems.at[k, j], recv_sem=recv_sems.at[k, j],
                    device_id=to, device_id_type=MESH)

            local = [pltpu.make_async_copy(ins[k], slot(k, *me), local_sems.at[k]) for k in range(n_arr)]
            sent = []
            for k in range(n_arr):
                sent += [copy(k, 0, me, sibling, True), copy(k, 1, me, (*xn, core), True),
                         copy(k, 2, me, (*yn, core), True)]
            for cp in local + sent:
                cp.start()

            def pass_on(k, j_from, j_to, block, targets):
                copy(k, j_from, block, me).wait_recv()
                for j, to in zip(j_to, targets):
                    cp = copy(k, j, block, to)
                    cp.start()
                    sent.append(cp)

            for k in range(n_arr):
                pass_on(k, s_first, (3, 3 + s_first), (*first, core), ((*second, core), sibling))
            for k in range(n_arr):
                pass_on(k, s_second, (3 + s_second,), (*second, core), (sibling,))
            for k in range(n_arr):
                pass_on(k, 3, (6,), (*dg, core), (sibling,))
            for k in range(n_arr):
                copy(k, 0, sibling, me).wait_recv()
                for j, chip in ((4, xn), (5, yn), (6, dg)):
                    copy(k, j, (*chip, 1 - core), me).wait_recv()
            for cp in sent:
                cp.wait_send()
            for cp in local:
                cp.wait()

        for core in (0, 1):
            pl.when(c == core)(lambda core=core: routed(core))

    return pl.pallas_call(
        body, name=name,
        in_specs=[ANY] * n_arr, out_specs=[ANY] * n_arr,
        out_shape=[_exchange_shape(a, True) for a in arrays],
        scratch_shapes=_exchange_sems(n_arr),
    )(*arrays)


def _piece_columns():
    pos = {}
    for h in range(RET_HEADS):
        for k, p in enumerate((h, 4 + h, 8 + 2 * h, 9 + 2 * h, 16 + 2 * h, 17 + 2 * h)):
            pos[p] = 6 * h + k
    for p in range(FOX_PAIRS):
        for i in range(4):
            pos[24 + 8 * i + p] = 24 + 4 * p + i
    pos[D_IN_PAD // 128 - 1] = D_IN_PAD // 128 - 1
    return np.array([pos[7 * d + j] for d in range(N_DEV) for j in range(8)], np.int32)


def _dwin_pair_slabs(dz, u, core):
    def body(order_ref, cols_ref, *refs):
        pieces, u_ref, pair_ref, theirs_ref = refs[:8], refs[8], refs[9], refs[10]
        send_buf, got_buf, send_sems, recv_sems, load_sem = refs[11:]
        s = pl.program_id(0)
        x, y, c = _coords()
        cols = jnp.concatenate([p[...] for p in pieces], axis=1)
        slab = _dot_tn(cols, u_ref[...])[:W_SLAB]

        def push(q):
            return pltpu.make_async_remote_copy(
                src_ref=send_buf.at[q], dst_ref=theirs_ref.at[q],
                send_sem=send_sems.at[q], recv_sem=recv_sems.at[q],
                device_id=(x, y, 1 - c), device_id_type=MESH)

        for q in range(N_CHIP):
            @pl.when(s == q)
            def _(q=q):
                send_buf[q] = slab.astype(BF16)
                push(q).start()

            @pl.when(s == N_CHIP + q)
            def _(q=q):
                push(q).wait_recv()
                load = pltpu.make_async_copy(theirs_ref.at[q], got_buf, load_sem)
                load.start()
                load.wait()
                pair_ref[0] = (slab + got_buf[...].astype(F32)).astype(BF16)

        @pl.when(s == 2 * N_CHIP - 1)
        def _():
            for q in range(N_CHIP):
                push(q).wait_send()

    order = jnp.concatenate([2 * jnp.arange(N_CHIP) + (1 - core), 2 * jnp.arange(N_CHIP) + core]).astype(jnp.int32)
    piece = lambda j: pl.BlockSpec((T, 128), lambda s, order_ref, cols_ref: (0, cols_ref[order_ref[s] * 8 + j]))
    slabs = jax.ShapeDtypeStruct((N_CHIP, W_SLAB, D_MODEL), BF16)
    pair, _ = pl.pallas_call(
        body, name="dwin_pair_slabs",
        grid_spec=pltpu.PrefetchScalarGridSpec(
            num_scalar_prefetch=2, grid=(2 * N_CHIP,),
            in_specs=[piece(j) for j in range(8)]
            + [pl.BlockSpec((T, D_MODEL), lambda s, order_ref, cols_ref: (0, 0))],
            out_specs=[pl.BlockSpec((1, W_SLAB, D_MODEL),
                                    lambda s, order_ref, cols_ref: (jnp.maximum(s - N_CHIP, 0), 0, 0)), ANY],
            scratch_shapes=[pltpu.VMEM((N_CHIP, W_SLAB, D_MODEL), BF16), pltpu.VMEM((W_SLAB, D_MODEL), BF16),
                            pltpu.SemaphoreType.DMA((N_CHIP,)), pltpu.SemaphoreType.DMA((N_CHIP,)),
                            pltpu.SemaphoreType.DMA]),
        out_shape=[slabs, slabs],
        compiler_params=_params(dimension_semantics=("arbitrary",)),
    )(order, jnp.asarray(_piece_columns()), *([dz] * 8), u)
    return pair


def _du_rms(dz, slabs, tail, h_pad, dout, g, pair_blocks):
    tm = 272
    steps = T // tm
    merge_step = steps // 2
    columns = _piece_columns().reshape(N_DEV, 8)

    def body(dz_ref, w_ref, t_ref, h_ref, d_ref, g_ref, p_ref, dh_ref, dg_ref, got_ref,
             via_buf, mine_buf, send_sems, recv_sems, local_sems):
        i = pl.program_id(0)
        x, y, c = _coords()

        def routed(core, step):
            first, second = ((1 - x, y), (x, 1 - y)) if core == 0 else ((x, 1 - y), (1 - x, y))
            chip = lambda px, py: 2 * px + py

            def send(j, src, dst, to):
                return pltpu.make_async_remote_copy(
                    src_ref=src, dst_ref=dst, send_sem=send_sems.at[j], recv_sem=recv_sems.at[j],
                    device_id=(*to, core), device_id_type=MESH)

            onward = send(1, p_ref.at[chip(1 - x, 1 - y)], via_buf, first)
            direct = send(0, p_ref.at[chip(*first)], got_ref.at[1], first)
            summed = send(2, mine_buf, got_ref.at[2], second)
            own = pltpu.make_async_copy(p_ref.at[chip(x, y)], got_ref.at[0], local_sems.at[0])
            load = pltpu.make_async_copy(p_ref.at[chip(*second)], mine_buf, local_sems.at[1])
            if step == "start":
                for cp in (onward, direct, own, load):
                    cp.start()
            elif step == "merge":
                onward.wait_recv()
                load.wait()
                mine_buf[...] = (mine_buf[...].astype(F32) + via_buf[...].astype(F32)).astype(BF16)
                summed.start()
            else:
                direct.wait_recv()
                summed.wait_recv()
                for cp in (onward, direct, summed):
                    cp.wait_send()
                own.wait()

        for core in (0, 1):
            pl.when(jnp.logical_and(c == core, i == 0))(lambda core=core: routed(core, "start"))

        @pl.when(i == 0)
        def _():
            dg_ref[...] = jnp.zeros_like(dg_ref)

        du = _dot(dz_ref[:, pl.ds(FF_BASE, 128)], t_ref[...])
        for d in range(N_DEV):
            cols = jnp.concatenate([dz_ref[:, pl.ds(128 * int(columns[d, j]), 128)] for j in range(7)], axis=1)
            du = du + _dot(cols, w_ref[d, pl.ds(0, W_STRIDE), :])
        h = h_ref[...]
        r = lax.rsqrt(jnp.mean(h * h, axis=-1, keepdims=True) + EPS)
        xn = h * r
        dg_ref[...] += jnp.sum(du * xn, axis=0, keepdims=True)
        dn = du * g_ref[...]
        dh_ref[...] = d_ref[...] + r * (dn - xn * jnp.mean(dn * xn, axis=-1, keepdims=True))

        for core in (0, 1):
            pl.when(jnp.logical_and(c == core, i == merge_step))(lambda core=core: routed(core, "merge"))
            pl.when(jnp.logical_and(c == core, i == steps - 1))(lambda core=core: routed(core, "finish"))

    tile = pl.BlockSpec((tm, D_MODEL), lambda i: (i, 0))
    slab = pair_blocks.shape[1:]
    return pl.pallas_call(
        body, name="du_rms", grid=(steps,),
        in_specs=[pl.BlockSpec((tm, D_IN_PAD), lambda i: (i, 0)),
                  pl.BlockSpec((N_DEV, W_SLAB, D_MODEL), lambda i: (0, 0, 0)),
                  pl.BlockSpec((128, D_MODEL), lambda i: (0, 0)),
                  tile, tile, pl.BlockSpec((1, D_MODEL), lambda i: (0, 0)), ANY],
        out_specs=[tile, pl.BlockSpec((1, D_MODEL), lambda i: (0, 0)), ANY],
        out_shape=[jax.ShapeDtypeStruct((T, D_MODEL), F32),
                   jax.ShapeDtypeStruct((1, D_MODEL), F32),
                   jax.ShapeDtypeStruct((3,) + slab, pair_blocks.dtype)],
        scratch_shapes=[pltpu.VMEM(slab, pair_blocks.dtype), pltpu.VMEM(slab, pair_blocks.dtype),
                        pltpu.SemaphoreType.DMA((3,)), pltpu.SemaphoreType.DMA((3,)),
                        pltpu.SemaphoreType.DMA((2,))],
        compiler_params=_params(dimension_semantics=("arbitrary",)),
    )(dz, slabs, tail, h_pad, dout, g, pair_blocks)


def _tri(lower):
    r = lax.broadcasted_iota(jnp.int32, (CHUNK, CHUNK), 0)
    c = lax.broadcasted_iota(jnp.int32, (CHUNK, CHUNK), 1)
    return jnp.where((r >= c) if lower else (r <= c), 1.0, 0.0).astype(F32)


def _row_valid(n):
    r = lax.broadcasted_iota(jnp.int32, (CHUNK, 128), 0) + n * CHUNK
    return r >= PAD


_FF_SPEC = pl.BlockSpec((T, 128), lambda i: (0, FF_BASE // 128))
_ZFF_SPEC = pl.BlockSpec((T, 128), lambda i: (0, 0))


def _forget_fwd(z, b_pad):
    def body(z_ref, b_ref, o_ref):
        tri = _tri(True)
        carry = jnp.zeros((1, 128), F32)
        for n in range(NCHUNK):
            rows = pl.ds(n * CHUNK, CHUNK)
            a = z_ref[rows, :] + b_ref[...]
            lf = -(jnp.maximum(-a, 0.0) + jnp.log(1.0 + jnp.exp(-jnp.abs(a))))
            lf = jnp.where(_row_valid(n), lf, 0.0)
            c = jnp.dot(tri, lf, precision=lax.Precision.HIGHEST,
                        preferred_element_type=F32) + carry
            carry = c[CHUNK - 1:CHUNK, :]
            o_ref[:, rows] = jnp.where(_row_valid(n), c * (-LOG2E), NEG_INF).T

    return pl.pallas_call(
        body, name="forget_fwd", grid=(1,),
        in_specs=[_ZFF_SPEC, pl.BlockSpec((1, 128), lambda i: (0, 0))],
        out_specs=pl.BlockSpec((128, T), lambda i: (0, 0)),
        out_shape=jax.ShapeDtypeStruct((128, T), F32),
        compiler_params=_params(dimension_semantics=("arbitrary",)),
    )(z, b_pad)


def _forget_bwd(z, b_pad, dc, dz):
    def body(z_ref, b_ref, dc_ref, dz_in, dff_ref, db_ref):
        tri = _tri(False)
        carry = jnp.zeros((1, 128), F32)
        db = jnp.zeros((1, 128), F32)
        for n in reversed(range(NCHUNK)):
            rows = pl.ds(n * CHUNK, CHUNK)
            dc_blk = jnp.concatenate([dc_ref[:, rows], jnp.zeros((128 - FOX_HEADS, CHUNK), F32)], axis=0).T
            dlf = jnp.dot(tri, dc_blk, precision=lax.Precision.HIGHEST,
                          preferred_element_type=F32) + carry
            carry = dlf[0:1, :]
            a = z_ref[rows, :] + b_ref[...]
            dff = jnp.where(_row_valid(n), dlf * jax.nn.sigmoid(-a), 0.0)
            dff_ref[rows, :] = dff.astype(BF16)
            db = db + jnp.sum(dff, axis=0, keepdims=True)
        db_ref[...] = db

    return pl.pallas_call(
        body, name="forget_bwd", grid=(1,),
        in_specs=[_ZFF_SPEC, pl.BlockSpec((1, 128), lambda i: (0, 0)),
                  pl.BlockSpec((FOX_HEADS, T), lambda i: (0, 0)), ANY],
        out_specs=[_FF_SPEC, pl.BlockSpec((1, 128), lambda i: (0, 0))],
        out_shape=[jax.ShapeDtypeStruct((T, D_IN_PAD), BF16),
                   jax.ShapeDtypeStruct((1, 128), F32)],
        input_output_aliases={3: 0},
        compiler_params=_params(dimension_semantics=("arbitrary",)),
    )(z, b_pad, dc, dz)


FOX_QB = 512
FOX_NQB = SEQ // FOX_QB


def _fox_block(b):
    lo = CHUNK + b * FOX_QB
    return pl.ds(lo, FOX_QB), lo, lo + FOX_QB


def _causal_bias():
    r = lax.broadcasted_iota(jnp.int32, (FOX_QB, FOX_QB), 0)
    c = lax.broadcasted_iota(jnp.int32, (FOX_QB, FOX_QB), 1)
    return jnp.where(c <= r, 0.0, NEG_INF).astype(F32)


def _fox_logits(q_blk, k_all, bias, causal, b):
    _, lo, hi = _fox_block(b)
    s_off = _dot_nt(q_blk, k_all[:lo]) + bias[:, :lo]
    s_dia = _dot_nt(q_blk, k_all[lo:hi]) + (bias[:, lo:hi] + causal)
    return s_off, s_dia


_FOX_Z_SPEC = pl.BlockSpec((T, FOX_W), lambda p: (0, FOX_BASE // FOX_W + p))
_FOX_BIAS_SPEC = pl.BlockSpec((2, 1, T), lambda p: (p, 0, 0))
_FOX_LSE_SPEC = pl.BlockSpec((2, T, 1), lambda p: (p, 0, 0))
_FOX_SCALE = FOX_D ** -0.5
_FOX_QSCALE = _FOX_SCALE * LOG2E


def _fox_fwd(z, bias, y, w_out_blk):
    last = FOX_PAIRS - 1

    def body(z_ref, b_ref, y_in, w_ref, a_ref, lse_ref, y_ref, wall_ref,
             send_sems, recv_sems, local_sems):
        start, wait = _direct_exchange([w_ref], [wall_ref], send_sems, recv_sems, local_sems, True)
        pl.when(pl.program_id(0) == 0)(start)

        causal = _causal_bias()
        a_ref[pl.ds(0, CHUNK), :] = jnp.zeros((CHUNK, 128), F32)
        y_ref[pl.ds(0, CHUNK), :] = jnp.zeros((CHUNK, 128), BF16)
        for j in range(2):
            lanes = pl.ds(j * FOX_D, FOX_D)
            k_all = z_ref[:, pl.ds(128 + j * FOX_D, FOX_D)]
            v_all = z_ref[:, pl.ds(256 + j * FOX_D, FOX_D)]
            bias = b_ref[j]
            lse_ref[j, pl.ds(0, CHUNK), :] = jnp.zeros((CHUNK, 1), F32)
            for b in range(FOX_NQB):
                rows, lo, hi = _fox_block(b)
                q_blk = (z_ref[rows, lanes].astype(F32) * _FOX_QSCALE).astype(BF16)
                s_off, s_dia = _fox_logits(q_blk, k_all, bias, causal, b)
                m = jnp.maximum(jnp.max(s_off, axis=-1, keepdims=True),
                                jnp.max(s_dia, axis=-1, keepdims=True))
                e_off = jnp.exp2(s_off - m)
                e_dia = jnp.exp2(s_dia - m)
                total = jnp.sum(e_off, axis=-1, keepdims=True) + jnp.sum(e_dia, axis=-1, keepdims=True)
                o = (_dot(e_off.astype(BF16), v_all[:lo]) + _dot(e_dia.astype(BF16), v_all[lo:hi])) / total
                a_ref[rows, lanes] = o
                lse_ref[j, rows, :] = m + jnp.log(total) * LOG2E
                gate = _silu_parts(z_ref[rows, pl.ds(384 + j * FOX_D, FOX_D)].astype(F32))[0]
                y_ref[rows, lanes] = (o * gate).astype(BF16)

        pl.when(pl.program_id(0) == last)(wait)

    return pl.pallas_call(
        body, name="fox_fwd", grid=(FOX_PAIRS,),
        in_specs=[_FOX_Z_SPEC, _FOX_BIAS_SPEC, ANY, ANY],
        out_specs=[pl.BlockSpec((T, 128), lambda p: (0, p)), _FOX_LSE_SPEC,
                   pl.BlockSpec((T, 128), lambda p: (0, 8 + p)), ANY],
        out_shape=[jax.ShapeDtypeStruct((T, FOX_HEADS * FOX_D), F32),
                   jax.ShapeDtypeStruct((FOX_HEADS, T, 1), F32),
                   jax.ShapeDtypeStruct((T, D_MIX), BF16),
                   _exchange_shape(w_out_blk, True)],
        input_output_aliases={2: 2},
        scratch_shapes=_exchange_sems(1),
        compiler_params=_params(dimension_semantics=("arbitrary",)),
    )(z, bias, y, w_out_blk)


def _fox_bwd(z, bias, a_f, lse, dy, dz, dwo_blocks):
    last = FOX_PAIRS - 1

    def body(z_ref, b_ref, a_ref, lse_ref, dy_ref, dz_in, dwo_ref, dz_ref, dc_ref, got_ref,
             kv_acc, dc_acc, send_sems, recv_sems, local_sems):
        start, wait = _direct_exchange([dwo_ref], [got_ref], send_sems, recv_sems, local_sems, False)
        pl.when(pl.program_id(0) == 0)(start)

        causal = _causal_bias()
        dz_ref[pl.ds(0, CHUNK), pl.ds(0, 128)] = jnp.zeros((CHUNK, 128), BF16)
        dz_ref[pl.ds(0, CHUNK), pl.ds(384, 128)] = jnp.zeros((CHUNK, 128), BF16)
        dk_rows, dv_rows = pl.ds(0, FOX_D), pl.ds(FOX_D, FOX_D)
        for j in range(2):
            lanes = pl.ds(j * FOX_D, FOX_D)
            k_all = z_ref[:, pl.ds(128 + j * FOX_D, FOX_D)]
            v_all = z_ref[:, pl.ds(256 + j * FOX_D, FOX_D)]
            bias = b_ref[j]
            kv_acc[...] = jnp.zeros_like(kv_acc)
            dc_acc[...] = jnp.zeros_like(dc_acc)
            for b in range(FOX_NQB):
                rows, lo, hi = _fox_block(b)
                off, dia = pl.ds(0, lo), pl.ds(lo, FOX_QB)
                q_blk = (z_ref[rows, lanes].astype(F32) * _FOX_QSCALE).astype(BF16)
                s_off, s_dia = _fox_logits(q_blk, k_all, bias, causal, b)
                lse_blk = lse_ref[j, rows, :]
                p_off, p_dia = jnp.exp2(s_off - lse_blk), jnp.exp2(s_dia - lse_blk)
                sg, dsg = _silu_parts(z_ref[rows, pl.ds(384 + j * FOX_D, FOX_D)].astype(F32))
                dyj = dy_ref[rows, lanes].astype(F32)
                dz_ref[rows, pl.ds(384 + j * FOX_D, FOX_D)] = (dyj * a_ref[rows, lanes] * dsg).astype(BF16)
                do_b = (dyj * sg).astype(BF16)
                dp_off = _dot_nt(do_b, v_all[:lo])
                dp_dia = _dot_nt(do_b, v_all[lo:hi])
                d = (jnp.sum(p_off * dp_off, axis=-1, keepdims=True)
                     + jnp.sum(p_dia * dp_dia, axis=-1, keepdims=True))
                ds_off = p_off * (dp_off - d)
                ds_dia = p_dia * (dp_dia - d)
                dc_acc[:, off] -= jnp.sum(ds_off, axis=0, keepdims=True)
                dc_acc[:, dia] -= jnp.sum(ds_dia, axis=0, keepdims=True)
                ds_off_b, ds_dia_b = ds_off.astype(BF16), ds_dia.astype(BF16)
                dq = _dot(ds_off_b, k_all[:lo]) + _dot(ds_dia_b, k_all[lo:hi])
                dz_ref[rows, lanes] = (dq * _FOX_SCALE).astype(BF16)
                kv_acc[dk_rows, off] += _dot_tn(q_blk, ds_off_b)
                kv_acc[dk_rows, dia] += _dot_tn(q_blk, ds_dia_b)
                kv_acc[dv_rows, off] += _dot_tn(do_b, p_off.astype(BF16))
                kv_acc[dv_rows, dia] += _dot_tn(do_b, p_dia.astype(BF16))
            for n in range(NCHUNK):
                rows = pl.ds(n * CHUNK, CHUNK)
                both = kv_acc[:, rows].T
                dz_ref[rows, pl.ds(128 + j * FOX_D, FOX_D)] = (both[:, :FOX_D] * LN2).astype(BF16)
                dz_ref[rows, pl.ds(256 + j * FOX_D, FOX_D)] = both[:, FOX_D:].astype(BF16)
            dc_ref[j] = dc_acc[...]

        pl.when(pl.program_id(0) == last)(wait)

    col = lambda base: pl.BlockSpec((T, 128), lambda p: (0, base + p))
    return pl.pallas_call(
        body, name="fox_bwd", grid=(FOX_PAIRS,),
        in_specs=[_FOX_Z_SPEC, _FOX_BIAS_SPEC, col(0), _FOX_LSE_SPEC, col(8), ANY, ANY],
        out_specs=[_FOX_Z_SPEC, _FOX_BIAS_SPEC, ANY],
        out_shape=[jax.ShapeDtypeStruct((T, D_IN_PAD), BF16),
                   jax.ShapeDtypeStruct((FOX_HEADS, 1, T), F32),
                   _exchange_shape(dwo_blocks, False)],
        input_output_aliases={5: 0},
        scratch_shapes=[pltpu.VMEM((2 * FOX_D, T), F32), pltpu.VMEM((1, T), F32)] + _exchange_sems(1),
        compiler_params=_params(dimension_semantics=("arbitrary",)),
    )(z, bias, a_f, lse, dy, dz, dwo_blocks)


def _rot(x, cosf, sins):
    return x * cosf + pltpu.roll(x, RET_DK // 2, 1) * sins


def _rot_t(d, cosf, sins):
    return d * cosf - pltpu.roll(d, RET_DK // 2, 1) * sins


_RET_Z_SPEC = pl.BlockSpec((T, RET_W), lambda h: (0, h))
_RET_TABLE_SPECS = [
    pl.BlockSpec((T, RET_DK), lambda h: (0, 0)),
    pl.BlockSpec((T, RET_DK), lambda h: (0, 0)),
    pl.BlockSpec((1, CHUNK, CHUNK), lambda h: (h, 0, 0)),
    pl.BlockSpec((1, CHUNK, 1), lambda h: (h, 0, 0)),
    pl.BlockSpec((1, CHUNK, 1), lambda h: (h, 0, 0)),
    pl.BlockSpec((1, 1, 1), lambda h: (h, 0, 0)),
]
_RQ, _RK = pl.ds(0, RET_DK), pl.ds(RET_DK, RET_DK)
_RV, _RG = pl.ds(2 * RET_DK, RET_DV), pl.ds(2 * RET_DK + RET_DV, RET_DV)
_RET_KSCALE = RET_DK ** -0.5


def _ret_fwd(z, tables):
    def body(z_ref, cos_ref, sin_ref, dm_ref, zeta_ref, xi_ref, cd_ref, raw_ref, y_ref):
        dmask, zeta, xi, cdec = dm_ref[0], zeta_ref[0], xi_ref[0], cd_ref[0]
        state = jnp.zeros((RET_DK, RET_DV), F32)
        for n in range(NCHUNK):
            rows = pl.ds(n * CHUNK, CHUNK)
            cosf, sins = cos_ref[rows, :], sin_ref[rows, :]
            qr = _rot(z_ref[rows, _RQ].astype(F32), cosf, sins)
            kr_b = (_rot(z_ref[rows, _RK].astype(F32), cosf, sins) * _RET_KSCALE).astype(BF16)
            v_b = z_ref[rows, _RV]
            a = _dot_nt(qr.astype(BF16), kr_b) * dmask
            out = _dot(a.astype(BF16), v_b) + _dot((qr * xi).astype(BF16), state.astype(BF16))
            state = state * cdec + _dot_tn(kr_b, (v_b.astype(F32) * zeta).astype(BF16))
            raw_ref[rows, :] = out
            r = lax.rsqrt(jnp.mean(out * out, axis=-1, keepdims=True) + EPS)
            y_ref[rows, :] = (out * r * _silu_parts(z_ref[rows, _RG].astype(F32))[0]).astype(BF16)

    wide = pl.BlockSpec((T, RET_DV), lambda h: (0, h))
    return pl.pallas_call(
        body, name="ret_fwd", grid=(RET_HEADS,),
        in_specs=[_RET_Z_SPEC] + _RET_TABLE_SPECS,
        out_specs=[wide, wide],
        out_shape=[jax.ShapeDtypeStruct((T, RET_HEADS * RET_DV), F32),
                   jax.ShapeDtypeStruct((T, D_MIX), BF16)],
        compiler_params=_params(dimension_semantics=("arbitrary",)),
    )(z, *tables)


def _ret_bwd(z, tables, raw, dy):
    def body(z_ref, cos_ref, sin_ref, dm_ref, zeta_ref, xi_ref, cd_ref, raw_ref, dy_ref,
             dz_ref, st_ref):
        dmask, zeta, xi, cdec = dm_ref[0], zeta_ref[0], xi_ref[0], cd_ref[0]

        def rotated(n):
            rows = pl.ds(n * CHUNK, CHUNK)
            cosf, sins = cos_ref[rows, :], sin_ref[rows, :]
            qr = _rot(z_ref[rows, _RQ].astype(F32), cosf, sins)
            kr_b = (_rot(z_ref[rows, _RK].astype(F32), cosf, sins) * _RET_KSCALE).astype(BF16)
            return rows, cosf, sins, qr, kr_b

        state = jnp.zeros((RET_DK, RET_DV), F32)
        for n in range(NCHUNK):
            st_ref[n] = state.astype(BF16)
            if n + 1 < NCHUNK:
                rows, _, _, _, kr_b = rotated(n)
                state = state * cdec + _dot_tn(kr_b, (z_ref[rows, _RV].astype(F32) * zeta).astype(BF16))

        grad_state = jnp.zeros((RET_DK, RET_DV), F32)
        for n in reversed(range(NCHUNK)):
            rows, cosf, sins, qr, kr_b = rotated(n)
            qr_b = qr.astype(BF16)
            v_b = z_ref[rows, _RV]
            gs_b = grad_state.astype(BF16)
            o = raw_ref[rows, :]
            r = lax.rsqrt(jnp.mean(o * o, axis=-1, keepdims=True) + EPS)
            hn = o * r
            sg, dsg = _silu_parts(z_ref[rows, _RG].astype(F32))
            dyn = dy_ref[rows, :].astype(F32)
            dz_ref[rows, _RG] = (dyn * hn * dsg).astype(BF16)
            dhn = dyn * sg
            do_b = (r * (dhn - hn * jnp.mean(dhn * hn, axis=-1, keepdims=True))).astype(BF16)
            a_b = (_dot_nt(qr_b, kr_b) * dmask).astype(BF16)
            da_b = (_dot_nt(do_b, v_b) * dmask).astype(BF16)
            dqr = _dot(da_b, kr_b) + xi * _dot_nt(do_b, st_ref[n])
            dkr = _dot_tn(da_b, qr_b) + zeta * _dot_nt(v_b, gs_b)
            dv = _dot_tn(a_b, do_b) + zeta * _dot(kr_b, gs_b)
            grad_state = grad_state * cdec + _dot_tn((qr * xi).astype(BF16), do_b)
            dz_ref[rows, _RQ] = _rot_t(dqr, cosf, sins).astype(BF16)
            dz_ref[rows, _RK] = (_rot_t(dkr, cosf, sins) * _RET_KSCALE).astype(BF16)
            dz_ref[rows, _RV] = dv.astype(BF16)

    wide = pl.BlockSpec((T, RET_DV), lambda h: (0, h))
    return pl.pallas_call(
        body, name="ret_bwd", grid=(RET_HEADS,),
        in_specs=[_RET_Z_SPEC] + _RET_TABLE_SPECS + [wide, wide],
        out_specs=_RET_Z_SPEC,
        out_shape=jax.ShapeDtypeStruct((T, D_IN_PAD), BF16),
        scratch_shapes=[pltpu.VMEM((NCHUNK, RET_DK, RET_DV), BF16)],
        compiler_params=_params(dimension_semantics=("arbitrary",)),
    )(z, *tables, raw, dy)


def _adamw(w, g, m, v):
    m = ADAM_B1 * m + (1.0 - ADAM_B1) * g
    v = ADAM_B2 * v + (1.0 - ADAM_B2) * (g * g)
    m_hat = m / (1.0 - ADAM_B1 ** ADAM_STEP)
    v_hat = v / (1.0 - ADAM_B2 ** ADAM_STEP)
    delta = -ADAM_LR * (m_hat / (jnp.sqrt(v_hat) + ADAM_EPS) + ADAM_WD * w)
    return delta, m, v


def _sum_adamw(parts, w, m, v, rows, name):
    _, r_tot, cols = parts.shape
    assert r_tot % rows == 0

    def body(p_ref, w_ref, m_ref, v_ref, g_ref, d_ref, nm_ref, nv_ref):
        g = p_ref[0].astype(F32)
        for d in range(1, N_DEV):
            g = g + p_ref[d].astype(F32)
        delta, nm, nv = _adamw(w_ref[...], g, m_ref[...], v_ref[...])
        g_ref[...] = g
        d_ref[...] = delta
        nm_ref[...] = nm
        nv_ref[...] = nv

    blk = pl.BlockSpec((rows, cols), lambda i: (i, 0))
    return pl.pallas_call(
        body, name=name, grid=(r_tot // rows,),
        in_specs=[pl.BlockSpec((N_DEV, rows, cols), lambda i: (0, i, 0)), blk, blk, blk],
        out_specs=[blk] * 4,
        out_shape=[jax.ShapeDtypeStruct((r_tot, cols), F32)] * 4,
        compiler_params=_params(dimension_semantics=("arbitrary",)),
    )(parts, w, m, v)


def _sum_adamw_w_in(parts, w, m, v, small):
    n_part, r, c = parts.shape
    steps = c // 128

    def body(p_ref, w_ref, m_ref, v_ref, s_ref, g_ref, d_ref, nm_ref, nv_ref, got_ref,
             send_sems, recv_sems, local_sems):
        start, wait = _direct_exchange([s_ref], [got_ref], send_sems, recv_sems, local_sems, True)
        pl.when(pl.program_id(0) == 0)(start)
        g = p_ref[0].astype(F32)
        for d in range(1, n_part):
            g = g + p_ref[d].astype(F32)
        delta, nm, nv = _adamw(w_ref[...], g, m_ref[...], v_ref[...])
        g_ref[...] = g
        d_ref[...] = delta
        nm_ref[...] = nm
        nv_ref[...] = nv
        pl.when(pl.program_id(0) == steps - 1)(wait)

    blk = pl.BlockSpec((r, 128), lambda i: (0, i))
    return pl.pallas_call(
        body, name="adamw_w_in", grid=(steps,),
        in_specs=[pl.BlockSpec((n_part, r, 128), lambda i: (0, 0, i)), blk, blk, blk, ANY],
        out_specs=[blk] * 4 + [ANY],
        out_shape=[jax.ShapeDtypeStruct((r, c), F32)] * 4 + [_exchange_shape(small, True)],
        scratch_shapes=_exchange_sems(1),
        compiler_params=_params(dimension_semantics=("arbitrary",)),
    )(parts, w, m, v, small)


def _adamw_small(got, me, metas, norms, finals, biases):
    def body(me_ref, gm_ref, gr_ref, *refs):
        ins, outs = refs[:12], refs[12:]
        g_meta, g_rest = gm_ref[0], gr_ref[0]
        for d in range(1, N_DEV):
            g_meta, g_rest = g_meta + gm_ref[d], g_rest + gr_ref[d]
        grads = [g_meta, g_rest[0:1], g_rest[1:2], g_rest[2:3, :FOX_HEADS]]
        for k, g in enumerate(grads):
            w_ref, m_ref, v_ref = ins[3 * k:3 * k + 3]
            delta, new_m, new_v = _adamw(w_ref[...], g, m_ref[...], v_ref[...])
            for o_ref, val in zip(outs[4 * k:4 * k + 4], (g, delta, new_m, new_v)):
                o_ref[...] = val
        outs[16][...] = g_rest[3:4, :128]

    groups = (metas, norms, finals, biases)
    full = lambda a: pl.BlockSpec(a.shape, lambda i, me_ref: (0,) * a.ndim)
    flat = [a for grp in groups for a in grp]
    res = pl.pallas_call(
        body, name="adamw_small",
        grid_spec=pltpu.PrefetchScalarGridSpec(
            num_scalar_prefetch=1, grid=(1,),
            in_specs=[pl.BlockSpec((N_DEV, N_META, META_BLK), lambda i, me_ref: (0, 0, me_ref[0])),
                      pl.BlockSpec((N_DEV, 8, D_MODEL), lambda i, me_ref: (0, N_META // 8, 0))]
            + [full(a) for a in flat],
            out_specs=[full(grp[0]) for grp in groups for _ in range(4)]
            + [pl.BlockSpec((1, 128), lambda i, me_ref: (0, 0))]),
        out_shape=[jax.ShapeDtypeStruct(grp[0].shape, F32) for grp in groups for _ in range(4)]
        + [jax.ShapeDtypeStruct((1, 128), F32)],
        compiler_params=_params(dimension_semantics=("arbitrary",)),
    )(me, got, got, *flat)
    return [res[4 * k:4 * k + 4] for k in range(4)], res[16]


def kernel(x, meta_tokens, norm_g, w_in, b_f, w_out, final_g, loss_target, m_meta_tokens, m_norm_g, m_w_in, m_b_f, m_w_out, m_final_g, v_meta_tokens, v_norm_g, v_w_in, v_b_f, v_w_out, v_final_g):
    core = lax.axis_index("c")
    me = 4 * lax.axis_index("x") + 2 * lax.axis_index("y") + core
    tables = _tables()

    wt_all, meta_all = _gather_two_level([_slab(w_in[0].T.astype(BF16), me), meta_tokens], name="gather_w_in")
    slabs, tail = _join_edges(wt_all)
    meta_full = jnp.transpose(meta_all, (1, 0, 2)).reshape(N_META, D_MODEL)
    h_pad = jnp.concatenate([jnp.zeros((PAD, D_MODEL), F32), meta_full, x[0]], axis=0)
    b_pad = jnp.pad(b_f, ((0, 0), (0, 128 - FOX_HEADS)))

    u, z = _rms_z_ret(h_pad, norm_g, slabs)
    z, zff = _z_fox(u, slabs, tail, z)
    bias = _forget_fwd(zff, b_pad)[:FOX_HEADS].reshape(FOX_HEADS, 1, T)
    raw, y = _ret_fwd(z, tables)
    a_f, lse, y, w_out_all = _fox_fwd(z, bias, y, w_out[0].astype(BF16))
    w_out_b = w_out_all.reshape(D_MIX, D_MODEL)
    dout, dout_b, dy, loss_blk, d_final_g = _out_loss_dy(y, w_out_b, h_pad, loss_target[0],
                                                         final_g.reshape(1, D_MODEL))

    d_w_out = _mm_tn(y, dout_b, tm=D_MIX, tn=256, name="mm_dwout")
    dz = _ret_bwd(z, tables, raw, dy)
    dz, dc, got_w_out = _fox_bwd(z, bias, a_f, lse, dy, dz, d_w_out.reshape(N_DEV, WO_BLK, D_MODEL))
    dz, db_f = _forget_bwd(zff, b_pad, dc.reshape(FOX_HEADS, T), dz)

    pair = _dwin_pair_slabs(dz, u, core)
    dh, d_norm_g, got_slabs = _du_rms(dz, slabs, tail, h_pad, dout, norm_g, pair)
    got_w_in = lax.dynamic_slice(got_slabs, (0, (W_BLK - W_STRIDE) * me, 0), (3, W_BLK, D_MODEL))

    small = jnp.concatenate([
        dh[PAD:CHUNK], d_norm_g, d_final_g, jnp.pad(db_f[:, :FOX_HEADS], ((0, 0), (0, D_MODEL - FOX_HEADS))),
        jnp.pad(loss_blk[0:1], ((0, 0), (0, D_MODEL - 128))),
        jnp.zeros((SMALL_ROWS - N_META - 4, D_MODEL), F32)], axis=0)
    g_w_in, d_w_in, nm_w_in, nv_w_in, got_small = _sum_adamw_w_in(
        got_w_in, w_in[0].T, m_w_in[0].T, v_w_in[0].T, small)
    g_w_out, d_w_out, nm_w_out, nv_w_out = _sum_adamw(got_w_out, w_out[0], m_w_out[0], v_w_out[0], 128, "adamw_w_out")

    row = lambda a: a.reshape(1, D_MODEL)
    (meta_o, norm_o, final_o, bias_o), loss_row = _adamw_small(
        got_small, me.astype(jnp.int32).reshape(1),
        (meta_tokens, m_meta_tokens, v_meta_tokens), (norm_g, m_norm_g, v_norm_g),
        (row(final_g), row(m_final_g), row(v_final_g)), (b_f, m_b_f, v_b_f))
    final_o = [a.reshape(D_MODEL) for a in final_o]

    back = lambda a: a.T[None]
    outs = [[meta_o[k], norm_o[k], back(wk), bias_o[k], ok[None], final_o[k]]
            for k, (wk, ok) in enumerate(zip((g_w_in, d_w_in, nm_w_in, nv_w_in),
                                             (g_w_out, d_w_out, nm_w_out, nv_w_out)))]
    return (loss_row[0, 0], dh[CHUNK:][None], *outs[0], *outs[1], *outs[2], *outs[3])
```

```python
import numpy as np
import jax
import jax.numpy as jnp
from jax import lax
from jax.experimental import pallas as pl
from jax.experimental.pallas import tpu as pltpu

F32 = jnp.float32
BF16 = jnp.bfloat16

N_DEV = 8
N_CHIP = 4
D_MODEL = 1024
SEQ = 2048
N_META = 16
CHUNK = 128
PAD = CHUNK - N_META
T = SEQ + CHUNK
NCHUNK = T // CHUNK
D_MIX = 2048
RET_HEADS = 4
RET_DK = 128
RET_DV = 256
RET_W = 2 * RET_DK + 2 * RET_DV
FOX_HEADS = 16
FOX_D = 64
FOX_PAIRS = FOX_HEADS // 2
FOX_W = 4 * 128
FOX_BASE = RET_HEADS * RET_W
FF_BASE = FOX_BASE + FOX_PAIRS * FOX_W
D_IN = 7184
D_IN_PAD = 7296
W_BLK = D_IN // N_DEV
WO_BLK = D_MIX // N_DEV
META_BLK = D_MODEL // N_DEV
EPS = 1e-6
NEG_INF = -1e30
ROPE_BASE = 10000.0
LOG2E = 1.4426950408889634
LN2 = 0.6931471805599453

ADAM_LR = 0.001
ADAM_B1 = 0.9
ADAM_B2 = 0.999
ADAM_EPS = 1e-08
ADAM_WD = 0.01
ADAM_STEP = 10

SMALL_ROWS = 24
VMEM_LIMIT = 56 * 1024 * 1024
MESH = pl.DeviceIdType.MESH
ANY = pl.BlockSpec(memory_space=pl.ANY)

_NT = (((1,), (1,)), ((), ()))
_TN = (((0,), (0,)), ((), ()))


def _dot(a, b):
    return jnp.dot(a, b, preferred_element_type=F32)


def _dot_nt(a, b):
    return lax.dot_general(a, b, _NT, preferred_element_type=F32)


def _dot_tn(a, b):
    return lax.dot_general(a, b, _TN, preferred_element_type=F32)


def _params(**kw):
    return pltpu.CompilerParams(vmem_limit_bytes=VMEM_LIMIT, **kw)


def _silu_parts(g):
    sig = jax.nn.sigmoid(g)
    return g * sig, sig * (1.0 + g * (1.0 - sig))


def _tables():
    pos = np.arange(T, dtype=np.float32) - PAD
    inv = (ROPE_BASE ** (-np.arange(0, RET_DK, 2, dtype=np.float32) / RET_DK)).astype(np.float32)
    ang = pos[:, None] * inv[None, :]
    cos, sin = np.cos(ang), np.sin(ang)
    cosf = np.concatenate([cos, cos], axis=1).astype(np.float32)
    sins = np.concatenate([-sin, sin], axis=1).astype(np.float32)
    h = np.arange(RET_HEADS, dtype=np.float32)
    log_gamma = np.log1p(-np.exp2(-5.0 - h)).astype(np.float32)
    idx = np.arange(CHUNK, dtype=np.float32)
    diff = idx[:, None] - idx[None, :]
    dmask = np.where(diff[None] >= 0,
                     np.exp(log_gamma[:, None, None] * np.maximum(diff, 0.0)[None]), 0.0)
    zeta = np.exp(log_gamma[:, None] * (CHUNK - 1.0 - idx)[None, :])
    xi = np.exp(log_gamma[:, None] * (idx + 1.0)[None, :])
    cdec = np.exp(log_gamma * CHUNK)
    return (jnp.asarray(cosf), jnp.asarray(sins), jnp.asarray(dmask, F32),
            jnp.asarray(zeta[:, :, None], F32), jnp.asarray(xi[:, :, None], F32),
            jnp.asarray(cdec[:, None, None], F32))


W_STRIDE = 896
W_SLAB = 912
W_EDGE = W_SLAB - W_STRIDE
def _slab(block, me):
    return lax.dynamic_update_slice(jnp.zeros((W_SLAB, block.shape[1]), block.dtype), block,
                                    ((W_BLK - W_STRIDE) * me, 0))


def _join_edges(slabs):
    last = slabs[:, W_STRIDE:]
    first = slabs[:, :W_EDGE] + jnp.concatenate([jnp.zeros_like(last[:1]), last[:-1]], axis=0)
    tail = jnp.pad(last[N_DEV - 1], ((0, 128 - W_EDGE), (0, 0)))
    return lax.dynamic_update_slice(slabs, first, (0, 0, 0)), tail


def _mm_tn(a, b, *, tm, tn, name):
    k, m = a.shape
    n = b.shape[1]
    assert m % tm == 0 and n % tn == 0

    def body(a_ref, b_ref, o_ref):
        o_ref[...] = _dot_tn(a_ref[...], b_ref[...]).astype(BF16)

    return pl.pallas_call(
        body, name=name, grid=(n // tn, m // tm),
        in_specs=[pl.BlockSpec((k, tm), lambda j, i: (0, i)),
                  pl.BlockSpec((k, tn), lambda j, i: (0, j))],
        out_specs=pl.BlockSpec((tm, tn), lambda j, i: (i, j)),
        out_shape=jax.ShapeDtypeStruct((m, n), BF16),
        compiler_params=_params(dimension_semantics=("arbitrary", "arbitrary")),
    )(a, b)


def _piece_spec(base, mult):
    def index(i):
        p = base + mult * i
        return p // 7, p % 7, 0
    return pl.BlockSpec((1, 128, D_MODEL), index)


_RET_PIECES = ((0, 1), (4, 1), (8, 2), (9, 2), (16, 2), (17, 2))
_FOX_PIECES = ((24, 1), (32, 1), (40, 1), (48, 1))


def _rms_z_ret(h_pad, g, slabs):
    def body(h_ref, g_ref, *refs):
        pieces, u_ref, z_ref = refs[:6], refs[6], refs[7]

        @pl.when(pl.program_id(0) == 0)
        def _():
            h = h_ref[...]
            r = lax.rsqrt(jnp.mean(h * h, axis=-1, keepdims=True) + EPS)
            u_ref[...] = (h * r * g_ref[...]).astype(BF16)

        w = jnp.concatenate([p[0] for p in pieces], axis=0)
        z_ref[...] = _dot_nt(u_ref[...], w).astype(BF16)

    whole = pl.BlockSpec((T, D_MODEL), lambda i: (0, 0))
    return pl.pallas_call(
        body, name="rms_z_ret", grid=(RET_HEADS,),
        in_specs=[whole, pl.BlockSpec((1, D_MODEL), lambda i: (0, 0))]
        + [_piece_spec(*bm) for bm in _RET_PIECES],
        out_specs=[whole, pl.BlockSpec((T, RET_W), lambda i: (0, i))],
        out_shape=[jax.ShapeDtypeStruct((T, D_MODEL), BF16),
                   jax.ShapeDtypeStruct((T, D_IN_PAD), BF16)],
        compiler_params=_params(dimension_semantics=("arbitrary",)),
    )(h_pad, g, *([slabs] * 6))


def _z_fox(u, slabs, tail, z):
    def body(u_ref, *refs):
        pieces, t_ref, z_ref, zff_ref = refs[:4], refs[4], refs[6], refs[7]
        u = u_ref[...]
        w = jnp.concatenate([p[0] for p in pieces], axis=0)
        z_ref[...] = _dot_nt(u, w).astype(BF16)

        @pl.when(pl.program_id(0) == 0)
        def _():
            zff_ref[...] = _dot_nt(u, t_ref[...])

    return pl.pallas_call(
        body, name="z_fox", grid=(FOX_PAIRS,),
        in_specs=[pl.BlockSpec((T, D_MODEL), lambda i: (0, 0))] + [_piece_spec(*bm) for bm in _FOX_PIECES]
        + [pl.BlockSpec((128, D_MODEL), lambda i: (0, 0)), ANY],
        out_specs=[_FOX_Z_SPEC, pl.BlockSpec((T, 128), lambda i: (0, 0))],
        out_shape=[jax.ShapeDtypeStruct((T, D_IN_PAD), BF16),
                   jax.ShapeDtypeStruct((T, 128), F32)],
        input_output_aliases={6: 0},
        compiler_params=_params(dimension_semantics=("arbitrary",)),
    )(u, *([slabs] * 4), tail, z)


def _out_loss_dy(y, w_out_b, h_pad, target, g):
    tm = T // 4

    def body(y_ref, w_ref, h_ref, t_hbm, g_ref, d_ref, db_ref, dy_ref, loss_ref, dg_ref, t_buf, t_sem):
        i = pl.program_id(0)
        head = pltpu.make_async_copy(t_hbm.at[pl.ds(0, tm - CHUNK)], t_buf.at[pl.ds(CHUNK, tm - CHUNK)], t_sem)
        rest = pltpu.make_async_copy(t_hbm.at[pl.ds(pl.multiple_of(jnp.maximum(i, 1) * tm - CHUNK, 8), tm)],
                                     t_buf, t_sem)

        @pl.when(i == 0)
        def _():
            t_buf[pl.ds(0, CHUNK), :] = jnp.zeros((CHUNK, D_MODEL), F32)
            head.start()
            loss_ref[...] = jnp.zeros_like(loss_ref)
            dg_ref[...] = jnp.zeros_like(dg_ref)

        pl.when(i > 0)(rest.start)

        w = w_ref[...]
        o = _dot(y_ref[...], w) + h_ref[...]
        pl.when(i == 0)(head.wait)
        pl.when(i > 0)(rest.wait)
        token = lax.broadcasted_iota(jnp.int32, (tm, 1), 0) + i * tm >= CHUNK
        g = g_ref[...]
        r = lax.rsqrt(jnp.mean(o * o, axis=-1, keepdims=True) + EPS)
        xn = o * r
        e = jnp.where(token, xn * g - t_buf[...], 0.0)
        loss_ref[...] += jnp.full(loss_ref.shape, 0.5 / D_MODEL * jnp.sum(e * e), F32)
        do = e * (1.0 / D_MODEL)
        dg_ref[...] += jnp.sum(do * xn, axis=0, keepdims=True)
        dn = do * g
        d = r * (dn - xn * jnp.mean(dn * xn, axis=-1, keepdims=True))
        d_b = d.astype(BF16)
        d_ref[...] = d
        db_ref[...] = d_b
        dy_ref[...] = _dot_nt(d_b, w).astype(BF16)

    tile = pl.BlockSpec((tm, D_MODEL), lambda i: (i, 0))
    wide = pl.BlockSpec((tm, D_MIX), lambda i: (i, 0))
    return pl.pallas_call(
        body, name="out_loss_dy", grid=(T // tm,),
        in_specs=[wide, pl.BlockSpec((D_MIX, D_MODEL), lambda i: (0, 0)), tile, ANY,
                  pl.BlockSpec((1, D_MODEL), lambda i: (0, 0))],
        out_specs=[tile, tile, wide,
                   pl.BlockSpec((8, 128), lambda i: (0, 0)),
                   pl.BlockSpec((1, D_MODEL), lambda i: (0, 0))],
        out_shape=[jax.ShapeDtypeStruct((T, D_MODEL), F32),
                   jax.ShapeDtypeStruct((T, D_MODEL), BF16),
                   jax.ShapeDtypeStruct((T, D_MIX), BF16),
                   jax.ShapeDtypeStruct((8, 128), F32),
                   jax.ShapeDtypeStruct((1, D_MODEL), F32)],
        scratch_shapes=[pltpu.VMEM((tm, D_MODEL), F32), pltpu.SemaphoreType.DMA],
        compiler_params=_params(dimension_semantics=("arbitrary",)),
    )(y, w_out_b, h_pad, target, g)


def _coords():
    return lax.axis_index("x"), lax.axis_index("y"), lax.axis_index("c")


def _flip(v, bit):
    return 1 - v if bit else v


def _peer(x, y, c, r):
    return _flip(x, (r >> 2) & 1), _flip(y, (r >> 1) & 1), _flip(c, r & 1)


def _direct_exchange(ins, outs, send_sems, recv_sems, local_sems, gather):
    x, y, c = _coords()
    me = 4 * x + 2 * y + c

    def src(k, to_idx):
        return ins[k] if gather else ins[k].at[to_idx]

    local = [pltpu.make_async_copy(src(k, me), outs[k].at[me], local_sems.at[k])
             for k in range(len(ins))]
    sends, recvs = [], []
    for r in range(1, N_DEV):
        px, py, pc = _peer(x, y, c, r)
        peer = 4 * px + 2 * py + pc
        for k in range(len(ins)):
            sems = dict(send_sem=send_sems.at[k, r - 1], recv_sem=recv_sems.at[k, r - 1],
                        device_id=(px, py, pc), device_id_type=MESH)
            sends.append(pltpu.make_async_remote_copy(src_ref=src(k, peer), dst_ref=outs[k].at[me], **sems))
            recvs.append(pltpu.make_async_remote_copy(src_ref=src(k, peer), dst_ref=outs[k].at[peer], **sems))

    def start():
        for cp in local + sends:
            cp.start()

    def wait():
        for cp in recvs:
            cp.wait_recv()
        for cp in sends:
            cp.wait_send()
        for cp in local:
            cp.wait()

    return start, wait


def _exchange_sems(n_arr):
    return [pltpu.SemaphoreType.DMA((n_arr, N_DEV - 1)), pltpu.SemaphoreType.DMA((n_arr, N_DEV - 1)),
            pltpu.SemaphoreType.DMA((n_arr,))]


def _exchange_shape(a, gather):
    return jax.ShapeDtypeStruct(((N_DEV,) + a.shape) if gather else a.shape, a.dtype)


def _gather_two_level(arrays, name):
    n_arr = len(arrays)

    def body(*refs):
        ins, outs = refs[:n_arr], refs[n_arr:2 * n_arr]
        send_sems, recv_sems, local_sems = refs[2 * n_arr:]
        x, y, c = _coords()

        def slot(k, px, py, pc):
            return outs[k].at[4 * px + 2 * py + pc]

        def routed(core):
            me, sibling = (x, y, core), (x, y, 1 - core)
            xn, yn, dg = (1 - x, y), (x, 1 - y), (1 - x, 1 - y)
            (first, s_first), (second, s_second) = ((xn, 1), (yn, 2)) if core == 0 else ((yn, 2), (xn, 1))

            def copy(k, j, block, to, own=False):
                return pltpu.make_async_remote_copy(
                    src_ref=ins[k] if own else slot(k, *block), dst_ref=slot(k, *block),
                    send_sem=send_sems.at[k, j], recv_sem=recv_sems.at[k, j],
                    device_id=to, device_id_type=MESH)

            local = [pltpu.make_async_copy(ins[k], slot(k, *me), local_sems.at[k]) for k in range(n_arr)]
            sent = []
            for k in range(n_arr):
                sent += [copy(k, 0, me, sibling, True), copy(k, 1, me, (*xn, core), True),
                         copy(k, 2, me, (*yn, core), True)]
            for cp in local + sent:
                cp.start()

            def pass_on(k, j_from, j_to, block, targets):
                copy(k, j_from, block, me).wait_recv()
                for j, to in zip(j_to, targets):
                    cp = copy(k, j, block, to)
                    cp.start()
                    sent.append(cp)

            for k in range(n_arr):
                pass_on(k, s_first, (3, 3 + s_first), (*first, core), ((*second, core), sibling))
            for k in range(n_arr):
                pass_on(k, s_second, (3 + s_second,), (*second, core), (sibling,))
            for k in range(n_arr):
                pass_on(k, 3, (6,), (*dg, core), (sibling,))
            for k in range(n_arr):
                copy(k, 0, sibling, me).wait_recv()
                for j, chip in ((4, xn), (5, yn), (6, dg)):
                    copy(k, j, (*chip, 1 - core), me).wait_recv()
            for cp in sent:
                cp.wait_send()
            for cp in local:
                cp.wait()

        for core in (0, 1):
            pl.when(c == core)(lambda core=core: routed(core))

    return pl.pallas_call(
        body, name=name,
        in_specs=[ANY] * n_arr, out_specs=[ANY] * n_arr,
        out_shape=[_exchange_shape(a, True) for a in arrays],
        scratch_shapes=_exchange_sems(n_arr),
    )(*arrays)


def _piece_columns():
    pos = {}
    for h in range(RET_HEADS):
        for k, p in enumerate((h, 4 + h, 8 + 2 * h, 9 + 2 * h, 16 + 2 * h, 17 + 2 * h)):
            pos[p] = 6 * h + k
    for p in range(FOX_PAIRS):
        for i in range(4):
            pos[24 + 8 * i + p] = 24 + 4 * p + i
    pos[D_IN_PAD // 128 - 1] = D_IN_PAD // 128 - 1
    return np.array([pos[7 * d + j] for d in range(N_DEV) for j in range(8)], np.int32)


def _dwin_pair_slabs(dz, u, core):
    def body(order_ref, cols_ref, *refs):
        pieces, u_ref, pair_ref, theirs_ref = refs[:8], refs[8], refs[9], refs[10]
        send_buf, got_buf, send_sems, recv_sems, load_sem = refs[11:]
        s = pl.program_id(0)
        x, y, c = _coords()
        cols = jnp.concatenate([p[...] for p in pieces], axis=1)
        slab = _dot_tn(cols, u_ref[...])[:W_SLAB]

        def push(q):
            return pltpu.make_async_remote_copy(
                src_ref=send_buf.at[q], dst_ref=theirs_ref.at[q],
                send_sem=send_sems.at[q], recv_sem=recv_sems.at[q],
                device_id=(x, y, 1 - c), device_id_type=MESH)

        for q in range(N_CHIP):
            @pl.when(s == q)
            def _(q=q):
                send_buf[q] = slab.astype(BF16)
                push(q).start()

            @pl.when(s == N_CHIP + q)
            def _(q=q):
                push(q).wait_recv()
                load = pltpu.make_async_copy(theirs_ref.at[q], got_buf, load_sem)
                load.start()
                load.wait()
                pair_ref[0] = (slab + got_buf[...].astype(F32)).astype(BF16)

        @pl.when(s == 2 * N_CHIP - 1)
        def _():
            for q in range(N_CHIP):
                push(q).wait_send()

    order = jnp.concatenate([2 * jnp.arange(N_CHIP) + (1 - core), 2 * jnp.arange(N_CHIP) + core]).astype(jnp.int32)
    piece = lambda j: pl.BlockSpec((T, 128), lambda s, order_ref, cols_ref: (0, cols_ref[order_ref[s] * 8 + j]))
    slabs = jax.ShapeDtypeStruct((N_CHIP, W_SLAB, D_MODEL), BF16)
    pair, _ = pl.pallas_call(
        body, name="dwin_pair_slabs",
        grid_spec=pltpu.PrefetchScalarGridSpec(
            num_scalar_prefetch=2, grid=(2 * N_CHIP,),
            in_specs=[piece(j) for j in range(8)]
            + [pl.BlockSpec((T, D_MODEL), lambda s, order_ref, cols_ref: (0, 0))],
            out_specs=[pl.BlockSpec((1, W_SLAB, D_MODEL),
                                    lambda s, order_ref, cols_ref: (jnp.maximum(s - N_CHIP, 0), 0, 0)), ANY],
            scratch_shapes=[pltpu.VMEM((N_CHIP, W_SLAB, D_MODEL), BF16), pltpu.VMEM((W_SLAB, D_MODEL), BF16),
                            pltpu.SemaphoreType.DMA((N_CHIP,)), pltpu.SemaphoreType.DMA((N_CHIP,)),
                            pltpu.SemaphoreType.DMA]),
        out_shape=[slabs, slabs],
        compiler_params=_params(dimension_semantics=("arbitrary",)),
    )(order, jnp.asarray(_piece_columns()), *([dz] * 8), u)
    return pair


def _du_rms(dz, slabs, tail, h_pad, dout, g, pair_blocks):
    tm = 272
    steps = T // tm
    merge_step = steps // 2
    columns = _piece_columns().reshape(N_DEV, 8)

    def body(dz_ref, w_ref, t_ref, h_ref, d_ref, g_ref, p_ref, dh_ref, dg_ref, got_ref,
             via_buf, mine_buf, send_sems, recv_sems, local_sems):
        i = pl.program_id(0)
        x, y, c = _coords()

        def routed(core, step):
            first, second = ((1 - x, y), (x, 1 - y)) if core == 0 else ((x, 1 - y), (1 - x, y))
            chip = lambda px, py: 2 * px + py

            def send(j, src, dst, to):
                return pltpu.make_async_remote_copy(
                    src_ref=src, dst_ref=dst, send_sem=send_sems.at[j], recv_sem=recv_sems.at[j],
                    device_id=(*to, core), device_id_type=MESH)

            onward = send(1, p_ref.at[chip(1 - x, 1 - y)], via_buf, first)
            direct = send(0, p_ref.at[chip(*first)], got_ref.at[1], first)
            summed = send(2, mine_buf, got_ref.at[2], second)
            own = pltpu.make_async_copy(p_ref.at[chip(x, y)], got_ref.at[0], local_sems.at[0])
            load = pltpu.make_async_copy(p_ref.at[chip(*second)], mine_buf, local_sems.at[1])
            if step == "start":
                for cp in (onward, own, load):
                    cp.start()
            elif step == "merge":
                onward.wait_recv()
                load.wait()
                mine_buf[...] = (mine_buf[...].astype(F32) + via_buf[...].astype(F32)).astype(BF16)
                summed.start()
                direct.start()
            else:
                direct.wait_recv()
                summed.wait_recv()
                for cp in (onward, direct, summed):
                    cp.wait_send()
                own.wait()

        for core in (0, 1):
            pl.when(jnp.logical_and(c == core, i == 0))(lambda core=core: routed(core, "start"))

        @pl.when(i == 0)
        def _():
            dg_ref[...] = jnp.zeros_like(dg_ref)

        du = _dot(dz_ref[:, pl.ds(FF_BASE, 128)], t_ref[...])
        for d in range(N_DEV):
            cols = jnp.concatenate([dz_ref[:, pl.ds(128 * int(columns[d, j]), 128)] for j in range(7)], axis=1)
            du = du + _dot(cols, w_ref[d, pl.ds(0, W_STRIDE), :])
        h = h_ref[...]
        r = lax.rsqrt(jnp.mean(h * h, axis=-1, keepdims=True) + EPS)
        xn = h * r
        dg_ref[...] += jnp.sum(du * xn, axis=0, keepdims=True)
        dn = du * g_ref[...]
        dh_ref[...] = d_ref[...] + r * (dn - xn * jnp.mean(dn * xn, axis=-1, keepdims=True))

        for core in (0, 1):
            pl.when(jnp.logical_and(c == core, i == merge_step))(lambda core=core: routed(core, "merge"))
            pl.when(jnp.logical_and(c == core, i == steps - 1))(lambda core=core: routed(core, "finish"))

    tile = pl.BlockSpec((tm, D_MODEL), lambda i: (i, 0))
    slab = pair_blocks.shape[1:]
    return pl.pallas_call(
        body, name="du_rms", grid=(steps,),
        in_specs=[pl.BlockSpec((tm, D_IN_PAD), lambda i: (i, 0)),
                  pl.BlockSpec((N_DEV, W_SLAB, D_MODEL), lambda i: (0, 0, 0)),
                  pl.BlockSpec((128, D_MODEL), lambda i: (0, 0)),
                  tile, tile, pl.BlockSpec((1, D_MODEL), lambda i: (0, 0)), ANY],
        out_specs=[tile, pl.BlockSpec((1, D_MODEL), lambda i: (0, 0)), ANY],
        out_shape=[jax.ShapeDtypeStruct((T, D_MODEL), F32),
                   jax.ShapeDtypeStruct((1, D_MODEL), F32),
                   jax.ShapeDtypeStruct((3,) + slab, pair_blocks.dtype)],
        scratch_shapes=[pltpu.VMEM(slab, pair_blocks.dtype), pltpu.VMEM(slab, pair_blocks.dtype),
                        pltpu.SemaphoreType.DMA((3,)), pltpu.SemaphoreType.DMA((3,)),
                        pltpu.SemaphoreType.DMA((2,))],
        compiler_params=_params(dimension_semantics=("arbitrary",)),
    )(dz, slabs, tail, h_pad, dout, g, pair_blocks)


def _tri(lower):
    r = lax.broadcasted_iota(jnp.int32, (CHUNK, CHUNK), 0)
    c = lax.broadcasted_iota(jnp.int32, (CHUNK, CHUNK), 1)
    return jnp.where((r >= c) if lower else (r <= c), 1.0, 0.0).astype(F32)


def _row_valid(n):
    r = lax.broadcasted_iota(jnp.int32, (CHUNK, 128), 0) + n * CHUNK
    return r >= PAD


_FF_SPEC = pl.BlockSpec((T, 128), lambda i: (0, FF_BASE // 128))
_ZFF_SPEC = pl.BlockSpec((T, 128), lambda i: (0, 0))


def _forget_fwd(z, b_pad):
    def body(z_ref, b_ref, o_ref):
        tri = _tri(True)
        carry = jnp.zeros((1, 128), F32)
        for n in range(NCHUNK):
            rows = pl.ds(n * CHUNK, CHUNK)
            a = z_ref[rows, :] + b_ref[...]
            lf = -(jnp.maximum(-a, 0.0) + jnp.log(1.0 + jnp.exp(-jnp.abs(a))))
            lf = jnp.where(_row_valid(n), lf, 0.0)
            c = jnp.dot(tri, lf, precision=lax.Precision.HIGHEST,
                        preferred_element_type=F32) + carry
            carry = c[CHUNK - 1:CHUNK, :]
            o_ref[:, rows] = jnp.where(_row_valid(n), c * (-LOG2E), NEG_INF).T

    return pl.pallas_call(
        body, name="forget_fwd", grid=(1,),
        in_specs=[_ZFF_SPEC, pl.BlockSpec((1, 128), lambda i: (0, 0))],
        out_specs=pl.BlockSpec((128, T), lambda i: (0, 0)),
        out_shape=jax.ShapeDtypeStruct((128, T), F32),
        compiler_params=_params(dimension_semantics=("arbitrary",)),
    )(z, b_pad)


def _forget_bwd(z, b_pad, dc, dz):
    def body(z_ref, b_ref, dc_ref, dz_in, dff_ref, db_ref):
        tri = _tri(False)
        carry = jnp.zeros((1, 128), F32)
        db = jnp.zeros((1, 128), F32)
        for n in reversed(range(NCHUNK)):
            rows = pl.ds(n * CHUNK, CHUNK)
            dc_blk = jnp.concatenate([dc_ref[:, rows], jnp.zeros((128 - FOX_HEADS, CHUNK), F32)], axis=0).T
            dlf = jnp.dot(tri, dc_blk, precision=lax.Precision.HIGHEST,
                          preferred_element_type=F32) + carry
            carry = dlf[0:1, :]
            a = z_ref[rows, :] + b_ref[...]
            dff = jnp.where(_row_valid(n), dlf * jax.nn.sigmoid(-a), 0.0)
            dff_ref[rows, :] = dff.astype(BF16)
            db = db + jnp.sum(dff, axis=0, keepdims=True)
        db_ref[...] = db

    return pl.pallas_call(
        body, name="forget_bwd", grid=(1,),
        in_specs=[_ZFF_SPEC, pl.BlockSpec((1, 128), lambda i: (0, 0)),
                  pl.BlockSpec((FOX_HEADS, T), lambda i: (0, 0)), ANY],
        out_specs=[_FF_SPEC, pl.BlockSpec((1, 128), lambda i: (0, 0))],
        out_shape=[jax.ShapeDtypeStruct((T, D_IN_PAD), BF16),
                   jax.ShapeDtypeStruct((1, 128), F32)],
        input_output_aliases={3: 0},
        compiler_params=_params(dimension_semantics=("arbitrary",)),
    )(z, b_pad, dc, dz)


FOX_QB = 512
FOX_NQB = SEQ // FOX_QB


def _fox_block(b):
    lo = CHUNK + b * FOX_QB
    return pl.ds(lo, FOX_QB), lo, lo + FOX_QB


def _causal_bias():
    r = lax.broadcasted_iota(jnp.int32, (FOX_QB, FOX_QB), 0)
    c = lax.broadcasted_iota(jnp.int32, (FOX_QB, FOX_QB), 1)
    return jnp.where(c <= r, 0.0, NEG_INF).astype(F32)


def _fox_logits(q_blk, k_all, bias, causal, b):
    _, lo, hi = _fox_block(b)
    s_off = _dot_nt(q_blk, k_all[:lo]) + bias[:, :lo]
    s_dia = _dot_nt(q_blk, k_all[lo:hi]) + (bias[:, lo:hi] + causal)
    return s_off, s_dia


_FOX_Z_SPEC = pl.BlockSpec((T, FOX_W), lambda p: (0, FOX_BASE // FOX_W + p))
_FOX_BIAS_SPEC = pl.BlockSpec((2, 1, T), lambda p: (p, 0, 0))
_FOX_LSE_SPEC = pl.BlockSpec((2, T, 1), lambda p: (p, 0, 0))
_FOX_SCALE = FOX_D ** -0.5
_FOX_QSCALE = _FOX_SCALE * LOG2E


def _fox_fwd(z, bias, y, w_out_blk):
    last = FOX_PAIRS - 1

    def body(z_ref, b_ref, y_in, w_ref, a_ref, lse_ref, y_ref, wall_ref,
             send_sems, recv_sems, local_sems):
        start, wait = _direct_exchange([w_ref], [wall_ref], send_sems, recv_sems, local_sems, True)
        pl.when(pl.program_id(0) == 0)(start)

        causal = _causal_bias()
        a_ref[pl.ds(0, CHUNK), :] = jnp.zeros((CHUNK, 128), F32)
        y_ref[pl.ds(0, CHUNK), :] = jnp.zeros((CHUNK, 128), BF16)
        for j in range(2):
            lanes = pl.ds(j * FOX_D, FOX_D)
            k_all = z_ref[:, pl.ds(128 + j * FOX_D, FOX_D)]
            v_all = z_ref[:, pl.ds(256 + j * FOX_D, FOX_D)]
            bias = b_ref[j]
            lse_ref[j, pl.ds(0, CHUNK), :] = jnp.zeros((CHUNK, 1), F32)
            for b in range(FOX_NQB):
                rows, lo, hi = _fox_block(b)
                q_blk = (z_ref[rows, lanes].astype(F32) * _FOX_QSCALE).astype(BF16)
                s_off, s_dia = _fox_logits(q_blk, k_all, bias, causal, b)
                m = jnp.maximum(jnp.max(s_off, axis=-1, keepdims=True),
                                jnp.max(s_dia, axis=-1, keepdims=True))
                e_off = jnp.exp2(s_off - m)
                e_dia = jnp.exp2(s_dia - m)
                total = jnp.sum(e_off, axis=-1, keepdims=True) + jnp.sum(e_dia, axis=-1, keepdims=True)
                o = (_dot(e_off.astype(BF16), v_all[:lo]) + _dot(e_dia.astype(BF16), v_all[lo:hi])) / total
                a_ref[rows, lanes] = o
                lse_ref[j, rows, :] = m + jnp.log(total) * LOG2E
                gate = _silu_parts(z_ref[rows, pl.ds(384 + j * FOX_D, FOX_D)].astype(F32))[0]
                y_ref[rows, lanes] = (o * gate).astype(BF16)

        pl.when(pl.program_id(0) == last)(wait)

    return pl.pallas_call(
        body, name="fox_fwd", grid=(FOX_PAIRS,),
        in_specs=[_FOX_Z_SPEC, _FOX_BIAS_SPEC, ANY, ANY],
        out_specs=[pl.BlockSpec((T, 128), lambda p: (0, p)), _FOX_LSE_SPEC,
                   pl.BlockSpec((T, 128), lambda p: (0, 8 + p)), ANY],
        out_shape=[jax.ShapeDtypeStruct((T, FOX_HEADS * FOX_D), F32),
                   jax.ShapeDtypeStruct((FOX_HEADS, T, 1), F32),
                   jax.ShapeDtypeStruct((T, D_MIX), BF16),
                   _exchange_shape(w_out_blk, True)],
        input_output_aliases={2: 2},
        scratch_shapes=_exchange_sems(1),
        compiler_params=_params(dimension_semantics=("arbitrary",)),
    )(z, bias, y, w_out_blk)


def _fox_bwd(z, bias, a_f, lse, dy, dz, dwo_blocks):
    last = FOX_PAIRS - 1

    def body(z_ref, b_ref, a_ref, lse_ref, dy_ref, dz_in, dwo_ref, dz_ref, dc_ref, got_ref,
             kv_acc, dc_acc, send_sems, recv_sems, local_sems):
        start, wait = _direct_exchange([dwo_ref], [got_ref], send_sems, recv_sems, local_sems, False)
        pl.when(pl.program_id(0) == 0)(start)

        causal = _causal_bias()
        dz_ref[pl.ds(0, CHUNK), pl.ds(0, 128)] = jnp.zeros((CHUNK, 128), BF16)
        dz_ref[pl.ds(0, CHUNK), pl.ds(384, 128)] = jnp.zeros((CHUNK, 128), BF16)
        dk_rows, dv_rows = pl.ds(0, FOX_D), pl.ds(FOX_D, FOX_D)
        for j in range(2):
            lanes = pl.ds(j * FOX_D, FOX_D)
            k_all = z_ref[:, pl.ds(128 + j * FOX_D, FOX_D)]
            v_all = z_ref[:, pl.ds(256 + j * FOX_D, FOX_D)]
            bias = b_ref[j]
            kv_acc[...] = jnp.zeros_like(kv_acc)
            dc_acc[...] = jnp.zeros_like(dc_acc)
            for b in range(FOX_NQB):
                rows, lo, hi = _fox_block(b)
                off, dia = pl.ds(0, lo), pl.ds(lo, FOX_QB)
                q_blk = (z_ref[rows, lanes].astype(F32) * _FOX_QSCALE).astype(BF16)
                s_off, s_dia = _fox_logits(q_blk, k_all, bias, causal, b)
                lse_blk = lse_ref[j, rows, :]
                p_off, p_dia = jnp.exp2(s_off - lse_blk), jnp.exp2(s_dia - lse_blk)
                sg, dsg = _silu_parts(z_ref[rows, pl.ds(384 + j * FOX_D, FOX_D)].astype(F32))
                dyj = dy_ref[rows, lanes].astype(F32)
                dz_ref[rows, pl.ds(384 + j * FOX_D, FOX_D)] = (dyj * a_ref[rows, lanes] * dsg).astype(BF16)
                do_b = (dyj * sg).astype(BF16)
                dp_off = _dot_nt(do_b, v_all[:lo])
                dp_dia = _dot_nt(do_b, v_all[lo:hi])
                d = (jnp.sum(p_off * dp_off, axis=-1, keepdims=True)
                     + jnp.sum(p_dia * dp_dia, axis=-1, keepdims=True))
                ds_off = p_off * (dp_off - d)
                ds_dia = p_dia * (dp_dia - d)
                dc_acc[:, off] -= jnp.sum(ds_off, axis=0, keepdims=True)
                dc_acc[:, dia] -= jnp.sum(ds_dia, axis=0, keepdims=True)
                ds_off_b, ds_dia_b = ds_off.astype(BF16), ds_dia.astype(BF16)
                dq = _dot(ds_off_b, k_all[:lo]) + _dot(ds_dia_b, k_all[lo:hi])
                dz_ref[rows, lanes] = (dq * _FOX_SCALE).astype(BF16)
                kv_acc[dk_rows, off] += _dot_tn(q_blk, ds_off_b)
                kv_acc[dk_rows, dia] += _dot_tn(q_blk, ds_dia_b)
                kv_acc[dv_rows, off] += _dot_tn(do_b, p_off.astype(BF16))
                kv_acc[dv_rows, dia] += _dot_tn(do_b, p_dia.astype(BF16))
            for n in range(NCHUNK):
                rows = pl.ds(n * CHUNK, CHUNK)
                both = kv_acc[:, rows].T
                dz_ref[rows, pl.ds(128 + j * FOX_D, FOX_D)] = (both[:, :FOX_D] * LN2).astype(BF16)
                dz_ref[rows, pl.ds(256 + j * FOX_D, FOX_D)] = both[:, FOX_D:].astype(BF16)
            dc_ref[j] = dc_acc[...]

        pl.when(pl.program_id(0) == last)(wait)

    col = lambda base: pl.BlockSpec((T, 128), lambda p: (0, base + p))
    return pl.pallas_call(
        body, name="fox_bwd", grid=(FOX_PAIRS,),
        in_specs=[_FOX_Z_SPEC, _FOX_BIAS_SPEC, col(0), _FOX_LSE_SPEC, col(8), ANY, ANY],
        out_specs=[_FOX_Z_SPEC, _FOX_BIAS_SPEC, ANY],
        out_shape=[jax.ShapeDtypeStruct((T, D_IN_PAD), BF16),
                   jax.ShapeDtypeStruct((FOX_HEADS, 1, T), F32),
                   _exchange_shape(dwo_blocks, False)],
        input_output_aliases={5: 0},
        scratch_shapes=[pltpu.VMEM((2 * FOX_D, T), F32), pltpu.VMEM((1, T), F32)] + _exchange_sems(1),
        compiler_params=_params(dimension_semantics=("arbitrary",)),
    )(z, bias, a_f, lse, dy, dz, dwo_blocks)


def _rot(x, cosf, sins):
    return x * cosf + pltpu.roll(x, RET_DK // 2, 1) * sins


def _rot_t(d, cosf, sins):
    return d * cosf - pltpu.roll(d, RET_DK // 2, 1) * sins


_RET_Z_SPEC = pl.BlockSpec((T, RET_W), lambda h: (0, h))
_RET_TABLE_SPECS = [
    pl.BlockSpec((T, RET_DK), lambda h: (0, 0)),
    pl.BlockSpec((T, RET_DK), lambda h: (0, 0)),
    pl.BlockSpec((1, CHUNK, CHUNK), lambda h: (h, 0, 0)),
    pl.BlockSpec((1, CHUNK, 1), lambda h: (h, 0, 0)),
    pl.BlockSpec((1, CHUNK, 1), lambda h: (h, 0, 0)),
    pl.BlockSpec((1, 1, 1), lambda h: (h, 0, 0)),
]
_RQ, _RK = pl.ds(0, RET_DK), pl.ds(RET_DK, RET_DK)
_RV, _RG = pl.ds(2 * RET_DK, RET_DV), pl.ds(2 * RET_DK + RET_DV, RET_DV)
_RET_KSCALE = RET_DK ** -0.5


def _ret_fwd(z, tables):
    def body(z_ref, cos_ref, sin_ref, dm_ref, zeta_ref, xi_ref, cd_ref, raw_ref, y_ref):
        dmask, zeta, xi, cdec = dm_ref[0], zeta_ref[0], xi_ref[0], cd_ref[0]
        state = jnp.zeros((RET_DK, RET_DV), F32)
        for n in range(NCHUNK):
            rows = pl.ds(n * CHUNK, CHUNK)
            cosf, sins = cos_ref[rows, :], sin_ref[rows, :]
            qr = _rot(z_ref[rows, _RQ].astype(F32), cosf, sins)
            kr_b = (_rot(z_ref[rows, _RK].astype(F32), cosf, sins) * _RET_KSCALE).astype(BF16)
            v_b = z_ref[rows, _RV]
            a = _dot_nt(qr.astype(BF16), kr_b) * dmask
            out = _dot(a.astype(BF16), v_b) + _dot((qr * xi).astype(BF16), state.astype(BF16))
            state = state * cdec + _dot_tn(kr_b, (v_b.astype(F32) * zeta).astype(BF16))
            raw_ref[rows, :] = out
            r = lax.rsqrt(jnp.mean(out * out, axis=-1, keepdims=True) + EPS)
            y_ref[rows, :] = (out * r * _silu_parts(z_ref[rows, _RG].astype(F32))[0]).astype(BF16)

    wide = pl.BlockSpec((T, RET_DV), lambda h: (0, h))
    return pl.pallas_call(
        body, name="ret_fwd", grid=(RET_HEADS,),
        in_specs=[_RET_Z_SPEC] + _RET_TABLE_SPECS,
        out_specs=[wide, wide],
        out_shape=[jax.ShapeDtypeStruct((T, RET_HEADS * RET_DV), F32),
                   jax.ShapeDtypeStruct((T, D_MIX), BF16)],
        compiler_params=_params(dimension_semantics=("arbitrary",)),
    )(z, *tables)


def _ret_bwd(z, tables, raw, dy):
    def body(z_ref, cos_ref, sin_ref, dm_ref, zeta_ref, xi_ref, cd_ref, raw_ref, dy_ref,
             dz_ref, st_ref):
        dmask, zeta, xi, cdec = dm_ref[0], zeta_ref[0], xi_ref[0], cd_ref[0]

        def rotated(n):
            rows = pl.ds(n * CHUNK, CHUNK)
            cosf, sins = cos_ref[rows, :], sin_ref[rows, :]
            qr = _rot(z_ref[rows, _RQ].astype(F32), cosf, sins)
            kr_b = (_rot(z_ref[rows, _RK].astype(F32), cosf, sins) * _RET_KSCALE).astype(BF16)
            return rows, cosf, sins, qr, kr_b

        state = jnp.zeros((RET_DK, RET_DV), F32)
        for n in range(NCHUNK):
            st_ref[n] = state.astype(BF16)
            if n + 1 < NCHUNK:
                rows, _, _, _, kr_b = rotated(n)
                state = state * cdec + _dot_tn(kr_b, (z_ref[rows, _RV].astype(F32) * zeta).astype(BF16))

        grad_state = jnp.zeros((RET_DK, RET_DV), F32)
        for n in reversed(range(NCHUNK)):
            rows, cosf, sins, qr, kr_b = rotated(n)
            qr_b = qr.astype(BF16)
            v_b = z_ref[rows, _RV]
            gs_b = grad_state.astype(BF16)
            o = raw_ref[rows, :]
            r = lax.rsqrt(jnp.mean(o * o, axis=-1, keepdims=True) + EPS)
            hn = o * r
            sg, dsg = _silu_parts(z_ref[rows, _RG].astype(F32))
            dyn = dy_ref[rows, :].astype(F32)
            dz_ref[rows, _RG] = (dyn * hn * dsg).astype(BF16)
            dhn = dyn * sg
            do_b = (r * (dhn - hn * jnp.mean(dhn * hn, axis=-1, keepdims=True))).astype(BF16)
            a_b = (_dot_nt(qr_b, kr_b) * dmask).astype(BF16)
            da_b = (_dot_nt(do_b, v_b) * dmask).astype(BF16)
            dqr = _dot(da_b, kr_b) + xi * _dot_nt(do_b, st_ref[n])
            dkr = _dot_tn(da_b, qr_b) + zeta * _dot_nt(v_b, gs_b)
            dv = _dot_tn(a_b, do_b) + zeta * _dot(kr_b, gs_b)
            grad_state = grad_state * cdec + _dot_tn((qr * xi).astype(BF16), do_b)
            dz_ref[rows, _RQ] = _rot_t(dqr, cosf, sins).astype(BF16)
            dz_ref[rows, _RK] = (_rot_t(dkr, cosf, sins) * _RET_KSCALE).astype(BF16)
            dz_ref[rows, _RV] = dv.astype(BF16)

    wide = pl.BlockSpec((T, RET_DV), lambda h: (0, h))
    return pl.pallas_call(
        body, name="ret_bwd", grid=(RET_HEADS,),
        in_specs=[_RET_Z_SPEC] + _RET_TABLE_SPECS + [wide, wide],
        out_specs=_RET_Z_SPEC,
        out_shape=jax.ShapeDtypeStruct((T, D_IN_PAD), BF16),
        scratch_shapes=[pltpu.VMEM((NCHUNK, RET_DK, RET_DV), BF16)],
        compiler_params=_params(dimension_semantics=("arbitrary",)),
    )(z, *tables, raw, dy)


def _adamw(w, g, m, v):
    m = ADAM_B1 * m + (1.0 - ADAM_B1) * g
    v = ADAM_B2 * v + (1.0 - ADAM_B2) * (g * g)
    m_hat = m / (1.0 - ADAM_B1 ** ADAM_STEP)
    v_hat = v / (1.0 - ADAM_B2 ** ADAM_STEP)
    delta = -ADAM_LR * (m_hat / (jnp.sqrt(v_hat) + ADAM_EPS) + ADAM_WD * w)
    return delta, m, v


def _sum_adamw(parts, w, m, v, rows, name):
    _, r_tot, cols = parts.shape
    assert r_tot % rows == 0

    def body(p_ref, w_ref, m_ref, v_ref, g_ref, d_ref, nm_ref, nv_ref):
        g = p_ref[0].astype(F32)
        for d in range(1, N_DEV):
            g = g + p_ref[d].astype(F32)
        delta, nm, nv = _adamw(w_ref[...], g, m_ref[...], v_ref[...])
        g_ref[...] = g
        d_ref[...] = delta
        nm_ref[...] = nm
        nv_ref[...] = nv

    blk = pl.BlockSpec((rows, cols), lambda i: (i, 0))
    return pl.pallas_call(
        body, name=name, grid=(r_tot // rows,),
        in_specs=[pl.BlockSpec((N_DEV, rows, cols), lambda i: (0, i, 0)), blk, blk, blk],
        out_specs=[blk] * 4,
        out_shape=[jax.ShapeDtypeStruct((r_tot, cols), F32)] * 4,
        compiler_params=_params(dimension_semantics=("arbitrary",)),
    )(parts, w, m, v)


def _sum_adamw_w_in(parts, w, m, v, small):
    n_part, r, c = parts.shape
    steps = c // 128

    def body(p_ref, w_ref, m_ref, v_ref, s_ref, g_ref, d_ref, nm_ref, nv_ref, got_ref,
             send_sems, recv_sems, local_sems):
        start, wait = _direct_exchange([s_ref], [got_ref], send_sems, recv_sems, local_sems, True)
        pl.when(pl.program_id(0) == 0)(start)
        g = p_ref[0].astype(F32)
        for d in range(1, n_part):
            g = g + p_ref[d].astype(F32)
        delta, nm, nv = _adamw(w_ref[...], g, m_ref[...], v_ref[...])
        g_ref[...] = g
        d_ref[...] = delta
        nm_ref[...] = nm
        nv_ref[...] = nv
        pl.when(pl.program_id(0) == steps - 1)(wait)

    blk = pl.BlockSpec((r, 128), lambda i: (0, i))
    return pl.pallas_call(
        body, name="adamw_w_in", grid=(steps,),
        in_specs=[pl.BlockSpec((n_part, r, 128), lambda i: (0, 0, i)), blk, blk, blk, ANY],
        out_specs=[blk] * 4 + [ANY],
        out_shape=[jax.ShapeDtypeStruct((r, c), F32)] * 4 + [_exchange_shape(small, True)],
        scratch_shapes=_exchange_sems(1),
        compiler_params=_params(dimension_semantics=("arbitrary",)),
    )(parts, w, m, v, small)


def _adamw_small(got, me, metas, norms, finals, biases):
    def body(me_ref, gm_ref, gr_ref, *refs):
        ins, outs = refs[:12], refs[12:]
        g_meta, g_rest = gm_ref[0], gr_ref[0]
        for d in range(1, N_DEV):
            g_meta, g_rest = g_meta + gm_ref[d], g_rest + gr_ref[d]
        grads = [g_meta, g_rest[0:1], g_rest[1:2], g_rest[2:3, :FOX_HEADS]]
        for k, g in enumerate(grads):
            w_ref, m_ref, v_ref = ins[3 * k:3 * k + 3]
            delta, new_m, new_v = _adamw(w_ref[...], g, m_ref[...], v_ref[...])
            for o_ref, val in zip(outs[4 * k:4 * k + 4], (g, delta, new_m, new_v)):
                o_ref[...] = val
        outs[16][...] = g_rest[3:4, :128]

    groups = (metas, norms, finals, biases)
    full = lambda a: pl.BlockSpec(a.shape, lambda i, me_ref: (0,) * a.ndim)
    flat = [a for grp in groups for a in grp]
    res = pl.pallas_call(
        body, name="adamw_small",
        grid_spec=pltpu.PrefetchScalarGridSpec(
            num_scalar_prefetch=1, grid=(1,),
            in_specs=[pl.BlockSpec((N_DEV, N_META, META_BLK), lambda i, me_ref: (0, 0, me_ref[0])),
                      pl.BlockSpec((N_DEV, 8, D_MODEL), lambda i, me_ref: (0, N_META // 8, 0))]
            + [full(a) for a in flat],
            out_specs=[full(grp[0]) for grp in groups for _ in range(4)]
            + [pl.BlockSpec((1, 128), lambda i, me_ref: (0, 0))]),
        out_shape=[jax.ShapeDtypeStruct(grp[0].shape, F32) for grp in groups for _ in range(4)]
        + [jax.ShapeDtypeStruct((1, 128), F32)],
        compiler_params=_params(dimension_semantics=("arbitrary",)),
    )(me, got, got, *flat)
    return [res[4 * k:4 * k + 4] for k in range(4)], res[16]


def kernel(x, meta_tokens, norm_g, w_in, b_f, w_out, final_g, loss_target, m_meta_tokens, m_norm_g, m_w_in, m_b_f, m_w_out, m_final_g, v_meta_tokens, v_norm_g, v_w_in, v_b_f, v_w_out, v_final_g):
    core = lax.axis_index("c")
    me = 4 * lax.axis_index("x") + 2 * lax.axis_index("y") + core
    tables = _tables()

    wt_all, meta_all = _gather_two_level([_slab(w_in[0].T.astype(BF16), me), meta_tokens], name="gather_w_in")
    slabs, tail = _join_edges(wt_all)
    meta_full = jnp.transpose(meta_all, (1, 0, 2)).reshape(N_META, D_MODEL)
    h_pad = jnp.concatenate([jnp.zeros((PAD, D_MODEL), F32), meta_full, x[0]], axis=0)
    b_pad = jnp.pad(b_f, ((0, 0), (0, 128 - FOX_HEADS)))

    u, z = _rms_z_ret(h_pad, norm_g, slabs)
    z, zff = _z_fox(u, slabs, tail, z)
    bias = _forget_fwd(zff, b_pad)[:FOX_HEADS].reshape(FOX_HEADS, 1, T)
    raw, y = _ret_fwd(z, tables)
    a_f, lse, y, w_out_all = _fox_fwd(z, bias, y, w_out[0].astype(BF16))
    w_out_b = w_out_all.reshape(D_MIX, D_MODEL)
    dout, dout_b, dy, loss_blk, d_final_g = _out_loss_dy(y, w_out_b, h_pad, loss_target[0],
                                                         final_g.reshape(1, D_MODEL))

    d_w_out = _mm_tn(y, dout_b, tm=D_MIX, tn=256, name="mm_dwout")
    dz = _ret_bwd(z, tables, raw, dy)
    dz, dc, got_w_out = _fox_bwd(z, bias, a_f, lse, dy, dz, d_w_out.reshape(N_DEV, WO_BLK, D_MODEL))
    dz, db_f = _forget_bwd(zff, b_pad, dc.reshape(FOX_HEADS, T), dz)

    pair = _dwin_pair_slabs(dz, u, core)
    dh, d_norm_g, got_slabs = _du_rms(dz, slabs, tail, h_pad, dout, norm_g, pair)
    got_w_in = lax.dynamic_slice(got_slabs, (0, (W_BLK - W_STRIDE) * me, 0), (3, W_BLK, D_MODEL))

    small = jnp.concatenate([
        dh[PAD:CHUNK], d_norm_g, d_final_g, jnp.pad(db_f[:, :FOX_HEADS], ((0, 0), (0, D_MODEL - FOX_HEADS))),
        jnp.pad(loss_blk[0:1], ((0, 0), (0, D_MODEL - 128))),
        jnp.zeros((SMALL_ROWS - N_META - 4, D_MODEL), F32)], axis=0)
    g_w_in, d_w_in, nm_w_in, nv_w_in, got_small = _sum_adamw_w_in(
        got_w_in, w_in[0].T, m_w_in[0].T, v_w_in[0].T, small)
    g_w_out, d_w_out, nm_w_out, nv_w_out = _sum_adamw(got_w_out, w_out[0], m_w_out[0], v_w_out[0], 128, "adamw_w_out")

    row = lambda a: a.reshape(1, D_MODEL)
    (meta_o, norm_o, final_o, bias_o), loss_row = _adamw_small(
        got_small, me.astype(jnp.int32).reshape(1),
        (meta_tokens, m_meta_tokens, v_meta_tokens), (norm_g, m_norm_g, v_norm_g),
        (row(final_g), row(m_final_g), row(v_final_g)), (b_f, m_b_f, v_b_f))
    final_o = [a.reshape(D_MODEL) for a in final_o]

    back = lambda a: a.T[None]
    outs = [[meta_o[k], norm_o[k], back(wk), bias_o[k], ok[None], final_o[k]]
            for k, (wk, ok) in enumerate(zip((g_w_in, d_w_in, nm_w_in, nv_w_in),
                                             (g_w_out, d_w_out, nm_w_out, nv_w_out)))]
    return (loss_row[0, 0], dh[CHUNK:][None], *outs[0], *outs[1], *outs[2], *outs[3])
```

```python
import numpy as np
import jax
import jax.numpy as jnp
from jax import lax
from jax.experimental import pallas as pl
from jax.experimental.pallas import tpu as pltpu

F32 = jnp.float32
BF16 = jnp.bfloat16

N_DEV = 8
N_CHIP = 4
D_MODEL = 1024
SEQ = 2048
N_META = 16
CHUNK = 128
PAD = CHUNK - N_META
T = SEQ + CHUNK
NCHUNK = T // CHUNK
D_MIX = 2048
RET_HEADS = 4
RET_DK = 128
RET_DV = 256
RET_W = 2 * RET_DK + 2 * RET_DV
FOX_HEADS = 16
FOX_D = 64
FOX_PAIRS = FOX_HEADS // 2
FOX_W = 4 * 128
FOX_BASE = RET_HEADS * RET_W
FF_BASE = FOX_BASE + FOX_PAIRS * FOX_W
D_IN = 7184
D_IN_PAD = 7296
W_BLK = D_IN // N_DEV
WO_BLK = D_MIX // N_DEV
META_BLK = D_MODEL // N_DEV
EPS = 1e-6
NEG_INF = -1e30
ROPE_BASE = 10000.0
LOG2E = 1.4426950408889634
LN2 = 0.6931471805599453

ADAM_LR = 0.001
ADAM_B1 = 0.9
ADAM_B2 = 0.999
ADAM_EPS = 1e-08
ADAM_WD = 0.01
ADAM_STEP = 10

SMALL_ROWS = 24
VMEM_LIMIT = 56 * 1024 * 1024
MESH = pl.DeviceIdType.MESH
ANY = pl.BlockSpec(memory_space=pl.ANY)

_NT = (((1,), (1,)), ((), ()))
_TN = (((0,), (0,)), ((), ()))


def _dot(a, b):
    return jnp.dot(a, b, preferred_element_type=F32)


def _dot_nt(a, b):
    return lax.dot_general(a, b, _NT, preferred_element_type=F32)


def _dot_tn(a, b):
    return lax.dot_general(a, b, _TN, preferred_element_type=F32)


def _params(**kw):
    return pltpu.CompilerParams(vmem_limit_bytes=VMEM_LIMIT, **kw)


def _silu_parts(g):
    sig = jax.nn.sigmoid(g)
    return g * sig, sig * (1.0 + g * (1.0 - sig))


def _tables():
    pos = np.arange(T, dtype=np.float32) - PAD
    inv = (ROPE_BASE ** (-np.arange(0, RET_DK, 2, dtype=np.float32) / RET_DK)).astype(np.float32)
    ang = pos[:, None] * inv[None, :]
    cos, sin = np.cos(ang), np.sin(ang)
    cosf = np.concatenate([cos, cos], axis=1).astype(np.float32)
    sins = np.concatenate([-sin, sin], axis=1).astype(np.float32)
    h = np.arange(RET_HEADS, dtype=np.float32)
    log_gamma = np.log1p(-np.exp2(-5.0 - h)).astype(np.float32)
    idx = np.arange(CHUNK, dtype=np.float32)
    diff = idx[:, None] - idx[None, :]
    dmask = np.where(diff[None] >= 0,
                     np.exp(log_gamma[:, None, None] * np.maximum(diff, 0.0)[None]), 0.0)
    zeta = np.exp(log_gamma[:, None] * (CHUNK - 1.0 - idx)[None, :])
    xi = np.exp(log_gamma[:, None] * (idx + 1.0)[None, :])
    cdec = np.exp(log_gamma * CHUNK)
    return (jnp.asarray(cosf), jnp.asarray(sins), jnp.asarray(dmask, F32),
            jnp.asarray(zeta[:, :, None], F32), jnp.asarray(xi[:, :, None], F32),
            jnp.asarray(cdec[:, None, None], F32))


W_STRIDE = 896
W_SLAB = 912
W_EDGE = W_SLAB - W_STRIDE
def _slab(block, me):
    return lax.dynamic_update_slice(jnp.zeros((W_SLAB, block.shape[1]), block.dtype), block,
                                    ((W_BLK - W_STRIDE) * me, 0))


def _join_edges(slabs):
    last = slabs[:, W_STRIDE:]
    first = slabs[:, :W_EDGE] + jnp.concatenate([jnp.zeros_like(last[:1]), last[:-1]], axis=0)
    tail = jnp.pad(last[N_DEV - 1], ((0, 128 - W_EDGE), (0, 0)))
    return lax.dynamic_update_slice(slabs, first, (0, 0, 0)), tail


def _mm_tn(a, b, *, tm, tn, name):
    k, m = a.shape
    n = b.shape[1]
    assert m % tm == 0 and n % tn == 0

    def body(a_ref, b_ref, o_ref):
        o_ref[...] = _dot_tn(a_ref[...], b_ref[...]).astype(BF16)

    return pl.pallas_call(
        body, name=name, grid=(n // tn, m // tm),
        in_specs=[pl.BlockSpec((k, tm), lambda j, i: (0, i)),
                  pl.BlockSpec((k, tn), lambda j, i: (0, j))],
        out_specs=pl.BlockSpec((tm, tn), lambda j, i: (i, j)),
        out_shape=jax.ShapeDtypeStruct((m, n), BF16),
        compiler_params=_params(dimension_semantics=("arbitrary", "arbitrary")),
    )(a, b)


def _piece_spec(base, mult):
    def index(i):
        p = base + mult * i
        return p // 7, p % 7, 0
    return pl.BlockSpec((1, 128, D_MODEL), index)


_RET_PIECES = ((0, 1), (4, 1), (8, 2), (9, 2), (16, 2), (17, 2))
_FOX_PIECES = ((24, 1), (32, 1), (40, 1), (48, 1))


def _rms_z_ret(h_pad, g, slabs):
    def body(h_ref, g_ref, *refs):
        pieces, u_ref, z_ref = refs[:6], refs[6], refs[7]

        @pl.when(pl.program_id(0) == 0)
        def _():
            h = h_ref[...]
            r = lax.rsqrt(jnp.mean(h * h, axis=-1, keepdims=True) + EPS)
            u_ref[...] = (h * r * g_ref[...]).astype(BF16)

        w = jnp.concatenate([p[0] for p in pieces], axis=0)
        z_ref[...] = _dot_nt(u_ref[...], w).astype(BF16)

    whole = pl.BlockSpec((T, D_MODEL), lambda i: (0, 0))
    return pl.pallas_call(
        body, name="rms_z_ret", grid=(RET_HEADS,),
        in_specs=[whole, pl.BlockSpec((1, D_MODEL), lambda i: (0, 0))]
        + [_piece_spec(*bm) for bm in _RET_PIECES],
        out_specs=[whole, pl.BlockSpec((T, RET_W), lambda i: (0, i))],
        out_shape=[jax.ShapeDtypeStruct((T, D_MODEL), BF16),
                   jax.ShapeDtypeStruct((T, D_IN_PAD), BF16)],
        compiler_params=_params(dimension_semantics=("arbitrary",)),
    )(h_pad, g, *([slabs] * 6))


def _z_fox(u, slabs, tail, z):
    def body(u_ref, *refs):
        pieces, t_ref, z_ref, zff_ref = refs[:4], refs[4], refs[6], refs[7]
        u = u_ref[...]
        w = jnp.concatenate([p[0] for p in pieces], axis=0)
        z_ref[...] = _dot_nt(u, w).astype(BF16)

        @pl.when(pl.program_id(0) == 0)
        def _():
            zff_ref[...] = _dot_nt(u, t_ref[...])

    return pl.pallas_call(
        body, name="z_fox", grid=(FOX_PAIRS,),
        in_specs=[pl.BlockSpec((T, D_MODEL), lambda i: (0, 0))] + [_piece_spec(*bm) for bm in _FOX_PIECES]
        + [pl.BlockSpec((128, D_MODEL), lambda i: (0, 0)), ANY],
        out_specs=[_FOX_Z_SPEC, pl.BlockSpec((T, 128), lambda i: (0, 0))],
        out_shape=[jax.ShapeDtypeStruct((T, D_IN_PAD), BF16),
                   jax.ShapeDtypeStruct((T, 128), F32)],
        input_output_aliases={6: 0},
        compiler_params=_params(dimension_semantics=("arbitrary",)),
    )(u, *([slabs] * 4), tail, z)


def _out_loss_dy(y, w_out_b, h_pad, target, g):
    tm = T // 4

    def body(y_ref, w_ref, h_ref, t_hbm, g_ref, d_ref, db_ref, dy_ref, loss_ref, dg_ref, t_buf, t_sem):
        i = pl.program_id(0)
        head = pltpu.make_async_copy(t_hbm.at[pl.ds(0, tm - CHUNK)], t_buf.at[pl.ds(CHUNK, tm - CHUNK)], t_sem)
        rest = pltpu.make_async_copy(t_hbm.at[pl.ds(pl.multiple_of(jnp.maximum(i, 1) * tm - CHUNK, 8), tm)],
                                     t_buf, t_sem)

        @pl.when(i == 0)
        def _():
            t_buf[pl.ds(0, CHUNK), :] = jnp.zeros((CHUNK, D_MODEL), F32)
            head.start()
            loss_ref[...] = jnp.zeros_like(loss_ref)
            dg_ref[...] = jnp.zeros_like(dg_ref)

        pl.when(i > 0)(rest.start)

        w = w_ref[...]
        o = _dot(y_ref[...], w) + h_ref[...]
        pl.when(i == 0)(head.wait)
        pl.when(i > 0)(rest.wait)
        token = lax.broadcasted_iota(jnp.int32, (tm, 1), 0) + i * tm >= CHUNK
        g = g_ref[...]
        r = lax.rsqrt(jnp.mean(o * o, axis=-1, keepdims=True) + EPS)
        xn = o * r
        e = jnp.where(token, xn * g - t_buf[...], 0.0)
        loss_ref[...] += jnp.full(loss_ref.shape, 0.5 / D_MODEL * jnp.sum(e * e), F32)
        do = e * (1.0 / D_MODEL)
        dg_ref[...] += jnp.sum(do * xn, axis=0, keepdims=True)
        dn = do * g
        d = r * (dn - xn * jnp.mean(dn * xn, axis=-1, keepdims=True))
        d_b = d.astype(BF16)
        d_ref[...] = d
        db_ref[...] = d_b
        dy_ref[...] = _dot_nt(d_b, w).astype(BF16)

    tile = pl.BlockSpec((tm, D_MODEL), lambda i: (i, 0))
    wide = pl.BlockSpec((tm, D_MIX), lambda i: (i, 0))
    return pl.pallas_call(
        body, name="out_loss_dy", grid=(T // tm,),
        in_specs=[wide, pl.BlockSpec((D_MIX, D_MODEL), lambda i: (0, 0)), tile, ANY,
                  pl.BlockSpec((1, D_MODEL), lambda i: (0, 0))],
        out_specs=[tile, tile, wide,
                   pl.BlockSpec((8, 128), lambda i: (0, 0)),
                   pl.BlockSpec((1, D_MODEL), lambda i: (0, 0))],
        out_shape=[jax.ShapeDtypeStruct((T, D_MODEL), F32),
                   jax.ShapeDtypeStruct((T, D_MODEL), BF16),
                   jax.ShapeDtypeStruct((T, D_MIX), BF16),
                   jax.ShapeDtypeStruct((8, 128), F32),
                   jax.ShapeDtypeStruct((1, D_MODEL), F32)],
        scratch_shapes=[pltpu.VMEM((tm, D_MODEL), F32), pltpu.SemaphoreType.DMA],
        compiler_params=_params(dimension_semantics=("arbitrary",)),
    )(y, w_out_b, h_pad, target, g)


def _coords():
    return lax.axis_index("x"), lax.axis_index("y"), lax.axis_index("c")


def _flip(v, bit):
    return 1 - v if bit else v


def _peer(x, y, c, r):
    return _flip(x, (r >> 2) & 1), _flip(y, (r >> 1) & 1), _flip(c, r & 1)


def _direct_exchange(ins, outs, send_sems, recv_sems, local_sems, gather):
    x, y, c = _coords()
    me = 4 * x + 2 * y + c

    def src(k, to_idx):
        return ins[k] if gather else ins[k].at[to_idx]

    local = [pltpu.make_async_copy(src(k, me), outs[k].at[me], local_sems.at[k])
             for k in range(len(ins))]
    sends, recvs = [], []
    for r in range(1, N_DEV):
        px, py, pc = _peer(x, y, c, r)
        peer = 4 * px + 2 * py + pc
        for k in range(len(ins)):
            sems = dict(send_sem=send_sems.at[k, r - 1], recv_sem=recv_sems.at[k, r - 1],
                        device_id=(px, py, pc), device_id_type=MESH)
            sends.append(pltpu.make_async_remote_copy(src_ref=src(k, peer), dst_ref=outs[k].at[me], **sems))
            recvs.append(pltpu.make_async_remote_copy(src_ref=src(k, peer), dst_ref=outs[k].at[peer], **sems))

    def start():
        for cp in local + sends:
            cp.start()

    def wait():
        for cp in recvs:
            cp.wait_recv()
        for cp in sends:
            cp.wait_send()
        for cp in local:
            cp.wait()

    return start, wait


def _exchange_sems(n_arr):
    return [pltpu.SemaphoreType.DMA((n_arr, N_DEV - 1)), pltpu.SemaphoreType.DMA((n_arr, N_DEV - 1)),
            pltpu.SemaphoreType.DMA((n_arr,))]


def _exchange_shape(a, gather):
    return jax.ShapeDtypeStruct(((N_DEV,) + a.shape) if gather else a.shape, a.dtype)


def _gather_two_level(arrays, name):
    n_arr = len(arrays)

    def body(*refs):
        ins, outs = refs[:n_arr], refs[n_arr:2 * n_arr]
        send_sems, recv_sems, local_sems = refs[2 * n_arr:]
        x, y, c = _coords()

        def slot(k, px, py, pc):
            return outs[k].at[4 * px + 2 * py + pc]

        def routed(core):
            me, sibling = (x, y, core), (x, y, 1 - core)
            xn, yn, dg = (1 - x, y), (x, 1 - y), (1 - x, 1 - y)
            (first, s_first), (second, s_second) = ((xn, 1), (yn, 2)) if core == 0 else ((yn, 2), (xn, 1))

            def copy(k, j, block, to, own=False):
                return pltpu.make_async_remote_copy(
                    src_ref=ins[k] if own else slot(k, *block), dst_ref=slot(k, *block),
                    send_sem=send_sems.at[k, j], recv_sem=recv_sems.at[k, j],
                    device_id=to, device_id_type=MESH)

            local = [pltpu.make_async_copy(ins[k], slot(k, *me), local_sems.at[k]) for k in range(n_arr)]
            sent = []
            for k in range(n_arr):
                sent += [copy(k, 0, me, sibling, True), copy(k, 1, me, (*xn, core), True),
                         copy(k, 2, me, (*yn, core), True)]
            for cp in local + sent:
                cp.start()

            def pass_on(k, j_from, j_to, block, targets):
                copy(k, j_from, block, me).wait_recv()
                for j, to in zip(j_to, targets):
                    cp = copy(k, j, block, to)
                    cp.start()
                    sent.append(cp)

            for k in range(n_arr):
                pass_on(k, s_first, (3, 3 + s_first), (*first, core), ((*second, core), sibling))
            for k in range(n_arr):
                pass_on(k, s_second, (3 + s_second,), (*second, core), (sibling,))
            for k in range(n_arr):
                pass_on(k, 3, (6,), (*dg, core), (sibling,))
            for k in range(n_arr):
                copy(k, 0, sibling, me).wait_recv()
                for j, chip in ((4, xn), (5, yn), (6, dg)):
                    copy(k, j, (*chip, 1 - core), me).wait_recv()
            for cp in sent:
                cp.wait_send()
            for cp in local:
                cp.wait()

        for core in (0, 1):
            pl.when(c == core)(lambda core=core: routed(core))

    return pl.pallas_call(
        body, name=name,
        in_specs=[ANY] * n_arr, out_specs=[ANY] * n_arr,
        out_shape=[_exchange_shape(a, True) for a in arrays],
        scratch_shapes=_exchange_sems(n_arr),
    )(*arrays)


def _piece_columns():
    pos = {}
    for h in range(RET_HEADS):
        for k, p in enumerate((h, 4 + h, 8 + 2 * h, 9 + 2 * h, 16 + 2 * h, 17 + 2 * h)):
            pos[p] = 6 * h + k
    for p in range(FOX_PAIRS):
        for i in range(4):
            pos[24 + 8 * i + p] = 24 + 4 * p + i
    pos[D_IN_PAD // 128 - 1] = D_IN_PAD // 128 - 1
    return np.array([pos[7 * d + j] for d in range(N_DEV) for j in range(8)], np.int32)


def _dwin_pair_slabs(dz, u, core):
    def body(order_ref, cols_ref, *refs):
        pieces, u_ref, pair_ref, theirs_ref = refs[:8], refs[8], refs[9], refs[10]
        send_buf, got_buf, send_sems, recv_sems, load_sem = refs[11:]
        s = pl.program_id(0)
        x, y, c = _coords()
        cols = jnp.concatenate([p[...] for p in pieces], axis=1)
        slab = _dot_tn(cols, u_ref[...])[:W_SLAB]

        def push(q):
            return pltpu.make_async_remote_copy(
                src_ref=send_buf.at[q], dst_ref=theirs_ref.at[q],
                send_sem=send_sems.at[q], recv_sem=recv_sems.at[q],
                device_id=(x, y, 1 - c), device_id_type=MESH)

        for q in range(N_CHIP):
            @pl.when(s == q)
            def _(q=q):
                send_buf[q] = slab.astype(BF16)
                push(q).start()

            @pl.when(s == N_CHIP + q)
            def _(q=q):
                push(q).wait_recv()
                load = pltpu.make_async_copy(theirs_ref.at[q], got_buf, load_sem)
                load.start()
                load.wait()
                pair_ref[0] = (slab + got_buf[...].astype(F32)).astype(BF16)

        @pl.when(s == 2 * N_CHIP - 1)
        def _():
            for q in range(N_CHIP):
                push(q).wait_send()

    order = jnp.concatenate([2 * jnp.arange(N_CHIP) + (1 - core), 2 * jnp.arange(N_CHIP) + core]).astype(jnp.int32)
    piece = lambda j: pl.BlockSpec((T, 128), lambda s, order_ref, cols_ref: (0, cols_ref[order_ref[s] * 8 + j]))
    slabs = jax.ShapeDtypeStruct((N_CHIP, W_SLAB, D_MODEL), BF16)
    pair, _ = pl.pallas_call(
        body, name="dwin_pair_slabs",
        grid_spec=pltpu.PrefetchScalarGridSpec(
            num_scalar_prefetch=2, grid=(2 * N_CHIP,),
            in_specs=[piece(j) for j in range(8)]
            + [pl.BlockSpec((T, D_MODEL), lambda s, order_ref, cols_ref: (0, 0))],
            out_specs=[pl.BlockSpec((1, W_SLAB, D_MODEL),
                                    lambda s, order_ref, cols_ref: (jnp.maximum(s - N_CHIP, 0), 0, 0)), ANY],
            scratch_shapes=[pltpu.VMEM((N_CHIP, W_SLAB, D_MODEL), BF16), pltpu.VMEM((W_SLAB, D_MODEL), BF16),
                            pltpu.SemaphoreType.DMA((N_CHIP,)), pltpu.SemaphoreType.DMA((N_CHIP,)),
                            pltpu.SemaphoreType.DMA]),
        out_shape=[slabs, slabs],
        compiler_params=_params(dimension_semantics=("arbitrary",)),
    )(order, jnp.asarray(_piece_columns()), *([dz] * 8), u)
    return pair


def _du_rms(dz, slabs, tail, h_pad, dout, g, pair_blocks):
    tm = 272
    steps = T // tm
    merge_step = steps // 2
    columns = _piece_columns().reshape(N_DEV, 8)

    def body(dz_ref, w_ref, t_ref, h_ref, d_ref, g_ref, p_ref, dh_ref, dg_ref, got_ref,
             via_buf, mine_buf, send_sems, recv_sems, local_sems):
        i = pl.program_id(0)
        x, y, c = _coords()

        def routed(core, step):
            first, second = ((1 - x, y), (x, 1 - y)) if core == 0 else ((x, 1 - y), (1 - x, y))
            chip = lambda px, py: 2 * px + py

            def send(j, src, dst, to):
                return pltpu.make_async_remote_copy(
                    src_ref=src, dst_ref=dst, send_sem=send_sems.at[j], recv_sem=recv_sems.at[j],
                    device_id=(*to, core), device_id_type=MESH)

            onward = send(1, p_ref.at[chip(1 - x, 1 - y)], via_buf, first)
            direct = send(0, p_ref.at[chip(*first)], got_ref.at[1], first)
            summed = send(2, mine_buf, got_ref.at[2], second)
            own = pltpu.make_async_copy(p_ref.at[chip(x, y)], got_ref.at[0], local_sems.at[0])
            load = pltpu.make_async_copy(p_ref.at[chip(*second)], mine_buf, local_sems.at[1])
            if step == "start":
                for cp in (onward, direct, own, load):
                    cp.start()
            elif step == "merge":
                onward.wait_recv()
                load.wait()
                mine_buf[...] = (mine_buf[...].astype(F32) + via_buf[...].astype(F32)).astype(BF16)
                summed.start()
            else:
                direct.wait_recv()
                summed.wait_recv()
                for cp in (onward, direct, summed):
                    cp.wait_send()
                own.wait()

        for core in (0, 1):
            pl.when(jnp.logical_and(c == core, i == 0))(lambda core=core: routed(core, "start"))

        @pl.when(i == 0)
        def _():
            dg_ref[...] = jnp.zeros_like(dg_ref)

        du = _dot(dz_ref[:, pl.ds(FF_BASE, 128)], t_ref[...])
        for d in range(N_DEV):
            cols = jnp.concatenate([dz_ref[:, pl.ds(128 * int(columns[d, j]), 128)] for j in range(7)], axis=1)
            du = du + _dot(cols, w_ref[d, pl.ds(0, W_STRIDE), :])
        h = h_ref[...]
        r = lax.rsqrt(jnp.mean(h * h, axis=-1, keepdims=True) + EPS)
        xn = h * r
        dg_ref[...] += jnp.sum(du * xn, axis=0, keepdims=True)
        dn = du * g_ref[...]
        dh_ref[...] = d_ref[...] + r * (dn - xn * jnp.mean(dn * xn, axis=-1, keepdims=True))

        for core in (0, 1):
            pl.when(jnp.logical_and(c == core, i == merge_step))(lambda core=core: routed(core, "merge"))
            pl.when(jnp.logical_and(c == core, i == steps - 1))(lambda core=core: routed(core, "finish"))

    tile = pl.BlockSpec((tm, D_MODEL), lambda i: (i, 0))
    slab = pair_blocks.shape[1:]
    return pl.pallas_call(
        body, name="du_rms", grid=(steps,),
        in_specs=[pl.BlockSpec((tm, D_IN_PAD), lambda i: (i, 0)),
                  pl.BlockSpec((N_DEV, W_SLAB, D_MODEL), lambda i: (0, 0, 0)),
                  pl.BlockSpec((128, D_MODEL), lambda i: (0, 0)),
                  tile, tile, pl.BlockSpec((1, D_MODEL), lambda i: (0, 0)), ANY],
        out_specs=[tile, pl.BlockSpec((1, D_MODEL), lambda i: (0, 0)), ANY],
        out_shape=[jax.ShapeDtypeStruct((T, D_MODEL), F32),
                   jax.ShapeDtypeStruct((1, D_MODEL), F32),
                   jax.ShapeDtypeStruct((3,) + slab, pair_blocks.dtype)],
        scratch_shapes=[pltpu.VMEM(slab, pair_blocks.dtype), pltpu.VMEM(slab, pair_blocks.dtype),
                        pltpu.SemaphoreType.DMA((3,)), pltpu.SemaphoreType.DMA((3,)),
                        pltpu.SemaphoreType.DMA((2,))],
        compiler_params=_params(dimension_semantics=("arbitrary",)),
    )(dz, slabs, tail, h_pad, dout, g, pair_blocks)


def _tri(lower):
    r = lax.broadcasted_iota(jnp.int32, (CHUNK, CHUNK), 0)
    c = lax.broadcasted_iota(jnp.int32, (CHUNK, CHUNK), 1)
    return jnp.where((r >= c) if lower else (r <= c), 1.0, 0.0).astype(F32)


def _row_valid(n):
    r = lax.broadcasted_iota(jnp.int32, (CHUNK, 128), 0) + n * CHUNK
    return r >= PAD


_FF_SPEC = pl.BlockSpec((T, 128), lambda i: (0, FF_BASE // 128))
_ZFF_SPEC = pl.BlockSpec((T, 128), lambda i: (0, 0))


def _forget_fwd(z, b_pad):
    def body(z_ref, b_ref, o_ref):
        tri = _tri(True)
        carry = jnp.zeros((1, 128), F32)
        for n in range(NCHUNK):
            rows = pl.ds(n * CHUNK, CHUNK)
            a = z_ref[rows, :] + b_ref[...]
            lf = -(jnp.maximum(-a, 0.0) + jnp.log(1.0 + jnp.exp(-jnp.abs(a))))
            lf = jnp.where(_row_valid(n), lf, 0.0)
            c = jnp.dot(tri, lf, precision=lax.Precision.HIGHEST,
                        preferred_element_type=F32) + carry
            carry = c[CHUNK - 1:CHUNK, :]
            o_ref[:, rows] = jnp.where(_row_valid(n), c * (-LOG2E), NEG_INF).T

    return pl.pallas_call(
        body, name="forget_fwd", grid=(1,),
        in_specs=[_ZFF_SPEC, pl.BlockSpec((1, 128), lambda i: (0, 0))],
        out_specs=pl.BlockSpec((128, T), lambda i: (0, 0)),
        out_shape=jax.ShapeDtypeStruct((128, T), F32),
        compiler_params=_params(dimension_semantics=("arbitrary",)),
    )(z, b_pad)


def _forget_bwd(z, b_pad, dc, dz):
    def body(z_ref, b_ref, dc_ref, dz_in, dff_ref, db_ref):
        tri = _tri(False)
        carry = jnp.zeros((1, 128), F32)
        db = jnp.zeros((1, 128), F32)
        for n in reversed(range(NCHUNK)):
            rows = pl.ds(n * CHUNK, CHUNK)
            dc_blk = jnp.concatenate([dc_ref[:, rows], jnp.zeros((128 - FOX_HEADS, CHUNK), F32)], axis=0).T
            dlf = jnp.dot(tri, dc_blk, precision=lax.Precision.HIGHEST,
                          preferred_element_type=F32) + carry
            carry = dlf[0:1, :]
            a = z_ref[rows, :] + b_ref[...]
            dff = jnp.where(_row_valid(n), dlf * jax.nn.sigmoid(-a), 0.0)
            dff_ref[rows, :] = dff.astype(BF16)
            db = db + jnp.sum(dff, axis=0, keepdims=True)
        db_ref[...] = db

    return pl.pallas_call(
        body, name="forget_bwd", grid=(1,),
        in_specs=[_ZFF_SPEC, pl.BlockSpec((1, 128), lambda i: (0, 0)),
                  pl.BlockSpec((FOX_HEADS, T), lambda i: (0, 0)), ANY],
        out_specs=[_FF_SPEC, pl.BlockSpec((1, 128), lambda i: (0, 0))],
        out_shape=[jax.ShapeDtypeStruct((T, D_IN_PAD), BF16),
                   jax.ShapeDtypeStruct((1, 128), F32)],
        input_output_aliases={3: 0},
        compiler_params=_params(dimension_semantics=("arbitrary",)),
    )(z, b_pad, dc, dz)


FOX_QB = 512
FOX_NQB = SEQ // FOX_QB


def _fox_block(b):
    lo = CHUNK + b * FOX_QB
    return pl.ds(lo, FOX_QB), lo, lo + FOX_QB


def _causal_bias():
    r = lax.broadcasted_iota(jnp.int32, (FOX_QB, FOX_QB), 0)
    c = lax.broadcasted_iota(jnp.int32, (FOX_QB, FOX_QB), 1)
    return jnp.where(c <= r, 0.0, NEG_INF).astype(F32)


def _fox_logits(q_blk, k_all, bias, causal, b):
    _, lo, hi = _fox_block(b)
    s_off = _dot_nt(q_blk, k_all[:lo]) + bias[:, :lo]
    s_dia = _dot_nt(q_blk, k_all[lo:hi]) + (bias[:, lo:hi] + causal)
    return s_off, s_dia


_FOX_Z_SPEC = pl.BlockSpec((T, FOX_W), lambda p: (0, FOX_BASE // FOX_W + p))
_FOX_BIAS_SPEC = pl.BlockSpec((2, 1, T), lambda p: (p, 0, 0))
_FOX_LSE_SPEC = pl.BlockSpec((2, T, 1), lambda p: (p, 0, 0))
_FOX_SCALE = FOX_D ** -0.5
_FOX_QSCALE = _FOX_SCALE * LOG2E


def _fox_fwd(z, bias, y, w_out_blk):
    last = FOX_PAIRS - 1

    def body(z_ref, b_ref, y_in, w_ref, a_ref, lse_ref, y_ref, wall_ref,
             send_sems, recv_sems, local_sems):
        start, wait = _direct_exchange([w_ref], [wall_ref], send_sems, recv_sems, local_sems, True)
        pl.when(pl.program_id(0) == 0)(start)

        causal = _causal_bias()
        a_ref[pl.ds(0, CHUNK), :] = jnp.zeros((CHUNK, 128), F32)
        y_ref[pl.ds(0, CHUNK), :] = jnp.zeros((CHUNK, 128), BF16)
        for j in range(2):
            lanes = pl.ds(j * FOX_D, FOX_D)
            k_all = z_ref[:, pl.ds(128 + j * FOX_D, FOX_D)]
            v_all = z_ref[:, pl.ds(256 + j * FOX_D, FOX_D)]
            bias = b_ref[j]
            lse_ref[j, pl.ds(0, CHUNK), :] = jnp.zeros((CHUNK, 1), F32)
            for b in range(FOX_NQB):
                rows, lo, hi = _fox_block(b)
                q_blk = (z_ref[rows, lanes].astype(F32) * _FOX_QSCALE).astype(BF16)
                s_off, s_dia = _fox_logits(q_blk, k_all, bias, causal, b)
                m = jnp.maximum(jnp.max(s_off, axis=-1, keepdims=True),
                                jnp.max(s_dia, axis=-1, keepdims=True))
                e_off = jnp.exp2(s_off - m)
                e_dia = jnp.exp2(s_dia - m)
                total = jnp.sum(e_off, axis=-1, keepdims=True) + jnp.sum(e_dia, axis=-1, keepdims=True)
                o = (_dot(e_off.astype(BF16), v_all[:lo]) + _dot(e_dia.astype(BF16), v_all[lo:hi])) / total
                a_ref[rows, lanes] = o
                lse_ref[j, rows, :] = m + jnp.log(total) * LOG2E
                gate = _silu_parts(z_ref[rows, pl.ds(384 + j * FOX_D, FOX_D)].astype(F32))[0]
                y_ref[rows, lanes] = (o * gate).astype(BF16)

        pl.when(pl.program_id(0) == last)(wait)

    return pl.pallas_call(
        body, name="fox_fwd", grid=(FOX_PAIRS,),
        in_specs=[_FOX_Z_SPEC, _FOX_BIAS_SPEC, ANY, ANY],
        out_specs=[pl.BlockSpec((T, 128), lambda p: (0, p)), _FOX_LSE_SPEC,
                   pl.BlockSpec((T, 128), lambda p: (0, 8 + p)), ANY],
        out_shape=[jax.ShapeDtypeStruct((T, FOX_HEADS * FOX_D), F32),
                   jax.ShapeDtypeStruct((FOX_HEADS, T, 1), F32),
                   jax.ShapeDtypeStruct((T, D_MIX), BF16),
                   _exchange_shape(w_out_blk, True)],
        input_output_aliases={2: 2},
        scratch_shapes=_exchange_sems(1),
        compiler_params=_params(dimension_semantics=("arbitrary",)),
    )(z, bias, y, w_out_blk)


def _fox_bwd(z, bias, a_f, lse, dy, dz, dwo_blocks):
    last = FOX_PAIRS - 1

    def body(z_ref, b_ref, a_ref, lse_ref, dy_ref, dz_in, dwo_ref, dz_ref, dc_ref, got_ref,
             kv_acc, dc_acc, send_sems, recv_sems, local_sems):
        start, wait = _direct_exchange([dwo_ref], [got_ref], send_sems, recv_sems, local_sems, False)
        pl.when(pl.program_id(0) == 0)(start)

        causal = _causal_bias()
        dz_ref[pl.ds(0, CHUNK), pl.ds(0, 128)] = jnp.zeros((CHUNK, 128), BF16)
        dz_ref[pl.ds(0, CHUNK), pl.ds(384, 128)] = jnp.zeros((CHUNK, 128), BF16)
        dk_rows, dv_rows = pl.ds(0, FOX_D), pl.ds(FOX_D, FOX_D)
        for j in range(2):
            lanes = pl.ds(j * FOX_D, FOX_D)
            k_all = z_ref[:, pl.ds(128 + j * FOX_D, FOX_D)]
            v_all = z_ref[:, pl.ds(256 + j * FOX_D, FOX_D)]
            bias = b_ref[j]
            kv_acc[...] = jnp.zeros_like(kv_acc)
            dc_acc[...] = jnp.zeros_like(dc_acc)
            for b in range(FOX_NQB):
                rows, lo, hi = _fox_block(b)
                off, dia = pl.ds(0, lo), pl.ds(lo, FOX_QB)
                q_blk = (z_ref[rows, lanes].astype(F32) * _FOX_QSCALE).astype(BF16)
                s_off, s_dia = _fox_logits(q_blk, k_all, bias, causal, b)
                lse_blk = lse_ref[j, rows, :]
                p_off, p_dia = jnp.exp2(s_off - lse_blk), jnp.exp2(s_dia - lse_blk)
                sg, dsg = _silu_parts(z_ref[rows, pl.ds(384 + j * FOX_D, FOX_D)].astype(F32))
                dyj = dy_ref[rows, lanes].astype(F32)
                dz_ref[rows, pl.ds(384 + j * FOX_D, FOX_D)] = (dyj * a_ref[rows, lanes] * dsg).astype(BF16)
                do_b = (dyj * sg).astype(BF16)
                dp_off = _dot_nt(do_b, v_all[:lo])
                dp_dia = _dot_nt(do_b, v_all[lo:hi])
                d = (jnp.sum(p_off * dp_off, axis=-1, keepdims=True)
                     + jnp.sum(p_dia * dp_dia, axis=-1, keepdims=True))
                ds_off = p_off * (dp_off - d)
                ds_dia = p_dia * (dp_dia - d)
                dc_acc[:, off] -= jnp.sum(ds_off, axis=0, keepdims=True)
                dc_acc[:, dia] -= jnp.sum(ds_dia, axis=0, keepdims=True)
                ds_off_b, ds_dia_b = ds_off.astype(BF16), ds_dia.astype(BF16)
                dq = _dot(ds_off_b, k_all[:lo]) + _dot(ds_dia_b, k_all[lo:hi])
                dz_ref[rows, lanes] = (dq * _FOX_SCALE).astype(BF16)
                kv_acc[dk_rows, off] += _dot_tn(q_blk, ds_off_b)
                kv_acc[dk_rows, dia] += _dot_tn(q_blk, ds_dia_b)
                kv_acc[dv_rows, off] += _dot_tn(do_b, p_off.astype(BF16))
                kv_acc[dv_rows, dia] += _dot_tn(do_b, p_dia.astype(BF16))
            for n in range(NCHUNK):
                rows = pl.ds(n * CHUNK, CHUNK)
                both = kv_acc[:, rows].T
                dz_ref[rows, pl.ds(128 + j * FOX_D, FOX_D)] = (both[:, :FOX_D] * LN2).astype(BF16)
                dz_ref[rows, pl.ds(256 + j * FOX_D, FOX_D)] = both[:, FOX_D:].astype(BF16)
            dc_ref[j] = dc_acc[...]

        pl.when(pl.program_id(0) == last)(wait)

    col = lambda base: pl.BlockSpec((T, 128), lambda p: (0, base + p))
    return pl.pallas_call(
        body, name="fox_bwd", grid=(FOX_PAIRS,),
        in_specs=[_FOX_Z_SPEC, _FOX_BIAS_SPEC, col(0), _FOX_LSE_SPEC, col(8), ANY, ANY],
        out_specs=[_FOX_Z_SPEC, _FOX_BIAS_SPEC, ANY],
        out_shape=[jax.ShapeDtypeStruct((T, D_IN_PAD), BF16),
                   jax.ShapeDtypeStruct((FOX_HEADS, 1, T), F32),
                   _exchange_shape(dwo_blocks, False)],
        input_output_aliases={5: 0},
        scratch_shapes=[pltpu.VMEM((2 * FOX_D, T), F32), pltpu.VMEM((1, T), F32)] + _exchange_sems(1),
        compiler_params=_params(dimension_semantics=("arbitrary",)),
    )(z, bias, a_f, lse, dy, dz, dwo_blocks)


def _rot(x, cosf, sins):
    return x * cosf + pltpu.roll(x, RET_DK // 2, 1) * sins


def _rot_t(d, cosf, sins):
    return d * cosf - pltpu.roll(d, RET_DK // 2, 1) * sins


_RET_Z_SPEC = pl.BlockSpec((T, RET_W), lambda h: (0, h))
_RET_TABLE_SPECS = [
    pl.BlockSpec((T, RET_DK), lambda h: (0, 0)),
    pl.BlockSpec((T, RET_DK), lambda h: (0, 0)),
    pl.BlockSpec((1, CHUNK, CHUNK), lambda h: (h, 0, 0)),
    pl.BlockSpec((1, CHUNK, 1), lambda h: (h, 0, 0)),
    pl.BlockSpec((1, CHUNK, 1), lambda h: (h, 0, 0)),
    pl.BlockSpec((1, 1, 1), lambda h: (h, 0, 0)),
]
_RQ, _RK = pl.ds(0, RET_DK), pl.ds(RET_DK, RET_DK)
_RV, _RG = pl.ds(2 * RET_DK, RET_DV), pl.ds(2 * RET_DK + RET_DV, RET_DV)
_RET_KSCALE = RET_DK ** -0.5


def _ret_fwd(z, tables):
    def body(z_ref, cos_ref, sin_ref, dm_ref, zeta_ref, xi_ref, cd_ref, raw_ref, y_ref):
        dmask, zeta, xi, cdec = dm_ref[0], zeta_ref[0], xi_ref[0], cd_ref[0]
        state = jnp.zeros((RET_DK, RET_DV), F32)
        for n in range(NCHUNK):
            rows = pl.ds(n * CHUNK, CHUNK)
            cosf, sins = cos_ref[rows, :], sin_ref[rows, :]
            qr = _rot(z_ref[rows, _RQ].astype(F32), cosf, sins)
            kr_b = (_rot(z_ref[rows, _RK].astype(F32), cosf, sins) * _RET_KSCALE).astype(BF16)
            v_b = z_ref[rows, _RV]
            a = _dot_nt(qr.astype(BF16), kr_b) * dmask
            out = _dot(a.astype(BF16), v_b) + _dot((qr * xi).astype(BF16), state.astype(BF16))
            state = state * cdec + _dot_tn(kr_b, (v_b.astype(F32) * zeta).astype(BF16))
            raw_ref[rows, :] = out
            r = lax.rsqrt(jnp.mean(out * out, axis=-1, keepdims=True) + EPS)
            y_ref[rows, :] = (out * r * _silu_parts(z_ref[rows, _RG].astype(F32))[0]).astype(BF16)

    wide = pl.BlockSpec((T, RET_DV), lambda h: (0, h))
    return pl.pallas_call(
        body, name="ret_fwd", grid=(RET_HEADS,),
        in_specs=[_RET_Z_SPEC] + _RET_TABLE_SPECS,
        out_specs=[wide, wide],
        out_shape=[jax.ShapeDtypeStruct((T, RET_HEADS * RET_DV), F32),
                   jax.ShapeDtypeStruct((T, D_MIX), BF16)],
        compiler_params=_params(dimension_semantics=("arbitrary",)),
    )(z, *tables)


def _ret_bwd(z, tables, raw, dy):
    def body(z_ref, cos_ref, sin_ref, dm_ref, zeta_ref, xi_ref, cd_ref, raw_ref, dy_ref,
             dz_ref, st_ref):
        dmask, zeta, xi, cdec = dm_ref[0], zeta_ref[0], xi_ref[0], cd_ref[0]

        def rotated(n):
            rows = pl.ds(n * CHUNK, CHUNK)
            cosf, sins = cos_ref[rows, :], sin_ref[rows, :]
            qr = _rot(z_ref[rows, _RQ].astype(F32), cosf, sins)
            kr_b = (_rot(z_ref[rows, _RK].astype(F32), cosf, sins) * _RET_KSCALE).astype(BF16)
            return rows, cosf, sins, qr, kr_b

        state = jnp.zeros((RET_DK, RET_DV), F32)
        for n in range(NCHUNK):
            st_ref[n] = state.astype(BF16)
            if n + 1 < NCHUNK:
                rows, _, _, _, kr_b = rotated(n)
                state = state * cdec + _dot_tn(kr_b, (z_ref[rows, _RV].astype(F32) * zeta).astype(BF16))

        grad_state = jnp.zeros((RET_DK, RET_DV), F32)
        for n in reversed(range(NCHUNK)):
            rows, cosf, sins, qr, kr_b = rotated(n)
            qr_b = qr.astype(BF16)
            v_b = z_ref[rows, _RV]
            gs_b = grad_state.astype(BF16)
            o = raw_ref[rows, :]
            r = lax.rsqrt(jnp.mean(o * o, axis=-1, keepdims=True) + EPS)
            hn = o * r
            sg, dsg = _silu_parts(z_ref[rows, _RG].astype(F32))
            dyn = dy_ref[rows, :].astype(F32)
            dz_ref[rows, _RG] = (dyn * hn * dsg).astype(BF16)
            dhn = dyn * sg
            do_b = (r * (dhn - hn * jnp.mean(dhn * hn, axis=-1, keepdims=True))).astype(BF16)
            a_b = (_dot_nt(qr_b, kr_b) * dmask).astype(BF16)
            da_b = (_dot_nt(do_b, v_b) * dmask).astype(BF16)
            dqr = _dot(da_b, kr_b) + xi * _dot_nt(do_b, st_ref[n])
            dkr = _dot_tn(da_b, qr_b) + zeta * _dot_nt(v_b, gs_b)
            dv = _dot_tn(a_b, do_b) + zeta * _dot(kr_b, gs_b)
            grad_state = grad_state * cdec + _dot_tn((qr * xi).astype(BF16), do_b)
            dz_ref[rows, _RQ] = _rot_t(dqr, cosf, sins).astype(BF16)
            dz_ref[rows, _RK] = (_rot_t(dkr, cosf, sins) * _RET_KSCALE).astype(BF16)
            dz_ref[rows, _RV] = dv.astype(BF16)

    wide = pl.BlockSpec((T, RET_DV), lambda h: (0, h))
    return pl.pallas_call(
        body, name="ret_bwd", grid=(RET_HEADS,),
        in_specs=[_RET_Z_SPEC] + _RET_TABLE_SPECS + [wide, wide],
        out_specs=_RET_Z_SPEC,
        out_shape=jax.ShapeDtypeStruct((T, D_IN_PAD), BF16),
        scratch_shapes=[pltpu.VMEM((NCHUNK, RET_DK, RET_DV), BF16)],
        compiler_params=_params(dimension_semantics=("arbitrary",)),
    )(z, *tables, raw, dy)


def _adamw(w, g, m, v):
    m = ADAM_B1 * m + (1.0 - ADAM_B1) * g
    v = ADAM_B2 * v + (1.0 - ADAM_B2) * (g * g)
    m_hat = m / (1.0 - ADAM_B1 ** ADAM_STEP)
    v_hat = v / (1.0 - ADAM_B2 ** ADAM_STEP)
    delta = -ADAM_LR * (m_hat / (jnp.sqrt(v_hat) + ADAM_EPS) + ADAM_WD * w)
    return delta, m, v


def _sum_adamw(parts, w, m, v, rows, name):
    _, r_tot, cols = parts.shape
    assert r_tot % rows == 0

    def body(p_ref, w_ref, m_ref, v_ref, g_ref, d_ref, nm_ref, nv_ref):
        g = p_ref[0].astype(F32)
        for d in range(1, N_DEV):
            g = g + p_ref[d].astype(F32)
        delta, nm, nv = _adamw(w_ref[...], g, m_ref[...], v_ref[...])
        g_ref[...] = g
        d_ref[...] = delta
        nm_ref[...] = nm
        nv_ref[...] = nv

    blk = pl.BlockSpec((rows, cols), lambda i: (i, 0))
    return pl.pallas_call(
        body, name=name, grid=(r_tot // rows,),
        in_specs=[pl.BlockSpec((N_DEV, rows, cols), lambda i: (0, i, 0)), blk, blk, blk],
        out_specs=[blk] * 4,
        out_shape=[jax.ShapeDtypeStruct((r_tot, cols), F32)] * 4,
        compiler_params=_params(dimension_semantics=("arbitrary",)),
    )(parts, w, m, v)


def _sum_adamw_w_in(parts, w, m, v, small):
    n_part, r, c = parts.shape
    steps = c // 128

    def body(p_ref, w_ref, m_ref, v_ref, s_ref, g_ref, d_ref, nm_ref, nv_ref, got_ref,
             send_sems, recv_sems, local_sems):
        start, wait = _direct_exchange([s_ref], [got_ref], send_sems, recv_sems, local_sems, True)
        pl.when(pl.program_id(0) == 0)(start)
        g = p_ref[0].astype(F32)
        for d in range(1, n_part):
            g = g + p_ref[d].astype(F32)
        delta, nm, nv = _adamw(w_ref[:, 0, :], g, m_ref[:, 0, :], v_ref[:, 0, :])
        g_ref[:, 0, :] = g
        d_ref[:, 0, :] = delta
        nm_ref[:, 0, :] = nm
        nv_ref[:, 0, :] = nv
        pl.when(pl.program_id(0) == steps - 1)(wait)

    blk3 = pl.BlockSpec((r, 1, 128), lambda i: (0, 0, i))
    return pl.pallas_call(
        body, name="adamw_w_in", grid=(steps,),
        in_specs=[pl.BlockSpec((n_part, r, 128), lambda i: (0, 0, i)), blk3, blk3, blk3, ANY],
        out_specs=[blk3] * 4 + [ANY],
        out_shape=[jax.ShapeDtypeStruct((r, 1, c), F32)] * 4 + [_exchange_shape(small, True)],
        scratch_shapes=_exchange_sems(1),
        compiler_params=_params(dimension_semantics=("arbitrary",)),
    )(parts, w, m, v, small)


def _adamw_small(got, me, metas, norms, finals, biases):
    def body(me_ref, gm_ref, gr_ref, *refs):
        ins, outs = refs[:12], refs[12:]
        g_meta, g_rest = gm_ref[0], gr_ref[0]
        for d in range(1, N_DEV):
            g_meta, g_rest = g_meta + gm_ref[d], g_rest + gr_ref[d]
        grads = [g_meta, g_rest[0:1], g_rest[1:2], g_rest[2:3, :FOX_HEADS]]
        for k, g in enumerate(grads):
            w_ref, m_ref, v_ref = ins[3 * k:3 * k + 3]
            delta, new_m, new_v = _adamw(w_ref[...], g, m_ref[...], v_ref[...])
            for o_ref, val in zip(outs[4 * k:4 * k + 4], (g, delta, new_m, new_v)):
                o_ref[...] = val
        outs[16][...] = g_rest[3:4, :128]

    groups = (metas, norms, finals, biases)
    full = lambda a: pl.BlockSpec(a.shape, lambda i, me_ref: (0,) * a.ndim)
    flat = [a for grp in groups for a in grp]
    res = pl.pallas_call(
        body, name="adamw_small",
        grid_spec=pltpu.PrefetchScalarGridSpec(
            num_scalar_prefetch=1, grid=(1,),
            in_specs=[pl.BlockSpec((N_DEV, N_META, META_BLK), lambda i, me_ref: (0, 0, me_ref[0])),
                      pl.BlockSpec((N_DEV, 8, D_MODEL), lambda i, me_ref: (0, N_META // 8, 0))]
            + [full(a) for a in flat],
            out_specs=[full(grp[0]) for grp in groups for _ in range(4)]
            + [pl.BlockSpec((1, 128), lambda i, me_ref: (0, 0))]),
        out_shape=[jax.ShapeDtypeStruct(grp[0].shape, F32) for grp in groups for _ in range(4)]
        + [jax.ShapeDtypeStruct((1, 128), F32)],
        compiler_params=_params(dimension_semantics=("arbitrary",)),
    )(me, got, got, *flat)
    return [res[4 * k:4 * k + 4] for k in range(4)], res[16]


def kernel(x, meta_tokens, norm_g, w_in, b_f, w_out, final_g, loss_target, m_meta_tokens, m_norm_g, m_w_in, m_b_f, m_w_out, m_final_g, v_meta_tokens, v_norm_g, v_w_in, v_b_f, v_w_out, v_final_g):
    core = lax.axis_index("c")
    me = 4 * lax.axis_index("x") + 2 * lax.axis_index("y") + core
    tables = _tables()

    wt_all, meta_all = _gather_two_level([_slab(w_in[0].T.astype(BF16), me), meta_tokens], name="gather_w_in")
    slabs, tail = _join_edges(wt_all)
    meta_full = jnp.transpose(meta_all, (1, 0, 2)).reshape(N_META, D_MODEL)
    h_pad = jnp.concatenate([jnp.zeros((PAD, D_MODEL), F32), meta_full, x[0]], axis=0)
    b_pad = jnp.pad(b_f, ((0, 0), (0, 128 - FOX_HEADS)))

    u, z = _rms_z_ret(h_pad, norm_g, slabs)
    z, zff = _z_fox(u, slabs, tail, z)
    bias = _forget_fwd(zff, b_pad)[:FOX_HEADS].reshape(FOX_HEADS, 1, T)
    raw, y = _ret_fwd(z, tables)
    a_f, lse, y, w_out_all = _fox_fwd(z, bias, y, w_out[0].astype(BF16))
    w_out_b = w_out_all.reshape(D_MIX, D_MODEL)
    dout, dout_b, dy, loss_blk, d_final_g = _out_loss_dy(y, w_out_b, h_pad, loss_target[0],
                                                         final_g.reshape(1, D_MODEL))

    d_w_out = _mm_tn(y, dout_b, tm=D_MIX, tn=256, name="mm_dwout")
    dz = _ret_bwd(z, tables, raw, dy)
    dz, dc, got_w_out = _fox_bwd(z, bias, a_f, lse, dy, dz, d_w_out.reshape(N_DEV, WO_BLK, D_MODEL))
    dz, db_f = _forget_bwd(zff, b_pad, dc.reshape(FOX_HEADS, T), dz)

    pair = _dwin_pair_slabs(dz, u, core)
    dh, d_norm_g, got_slabs = _du_rms(dz, slabs, tail, h_pad, dout, norm_g, pair)
    got_w_in = lax.dynamic_slice(got_slabs, (0, (W_BLK - W_STRIDE) * me, 0), (3, W_BLK, D_MODEL))

    small = jnp.concatenate([
        dh[PAD:CHUNK], d_norm_g, d_final_g, jnp.pad(db_f[:, :FOX_HEADS], ((0, 0), (0, D_MODEL - FOX_HEADS))),
        jnp.pad(loss_blk[0:1], ((0, 0), (0, D_MODEL - 128))),
        jnp.zeros((SMALL_ROWS - N_META - 4, D_MODEL), F32)], axis=0)
    fore = lambda a: jnp.transpose(a, (2, 0, 1))
    g_w_in, d_w_in, nm_w_in, nv_w_in, got_small = _sum_adamw_w_in(
        got_w_in, fore(w_in), fore(m_w_in), fore(v_w_in), small)
    g_w_out, d_w_out, nm_w_out, nv_w_out = _sum_adamw(got_w_out, w_out[0], m_w_out[0], v_w_out[0], 128, "adamw_w_out")

    row = lambda a: a.reshape(1, D_MODEL)
    (meta_o, norm_o, final_o, bias_o), loss_row = _adamw_small(
        got_small, me.astype(jnp.int32).reshape(1),
        (meta_tokens, m_meta_tokens, v_meta_tokens), (norm_g, m_norm_g, v_norm_g),
        (row(final_g), row(m_final_g), row(v_final_g)), (b_f, m_b_f, v_b_f))
    final_o = [a.reshape(D_MODEL) for a in final_o]

    back = lambda a: jnp.transpose(a, (1, 2, 0))
    outs = [[meta_o[k], norm_o[k], back(wk), bias_o[k], ok[None], final_o[k]]
            for k, (wk, ok) in enumerate(zip((g_w_in, d_w_in, nm_w_in, nv_w_in),
                                             (g_w_out, d_w_out, nm_w_out, nv_w_out)))]
    return (loss_row[0, 0], dh[CHUNK:][None], *outs[0], *outs[1], *outs[2], *outs[3])
```

```python
import numpy as np
import jax
import jax.numpy as jnp
from jax import lax
from jax.experimental import pallas as pl
from jax.experimental.pallas import tpu as pltpu

F32 = jnp.float32
BF16 = jnp.bfloat16

N_DEV = 8
N_CHIP = 4
D_MODEL = 1024
SEQ = 2048
N_META = 16
CHUNK = 128
PAD = CHUNK - N_META
T = SEQ + CHUNK
NCHUNK = T // CHUNK
D_MIX = 2048
RET_HEADS = 4
RET_DK = 128
RET_DV = 256
RET_W = 2 * RET_DK + 2 * RET_DV
FOX_HEADS = 16
FOX_D = 64
FOX_PAIRS = FOX_HEADS // 2
FOX_W = 4 * 128
FOX_BASE = RET_HEADS * RET_W
FF_BASE = FOX_BASE + FOX_PAIRS * FOX_W
D_IN = 7184
D_IN_PAD = 7296
W_BLK = D_IN // N_DEV
WO_BLK = D_MIX // N_DEV
META_BLK = D_MODEL // N_DEV
EPS = 1e-6
NEG_INF = -1e30
ROPE_BASE = 10000.0
LOG2E = 1.4426950408889634
LN2 = 0.6931471805599453

ADAM_LR = 0.001
ADAM_B1 = 0.9
ADAM_B2 = 0.999
ADAM_EPS = 1e-08
ADAM_WD = 0.01
ADAM_STEP = 10

SMALL_ROWS = 24
VMEM_LIMIT = 56 * 1024 * 1024
MESH = pl.DeviceIdType.MESH
ANY = pl.BlockSpec(memory_space=pl.ANY)

_NT = (((1,), (1,)), ((), ()))
_TN = (((0,), (0,)), ((), ()))


def _dot(a, b):
    return jnp.dot(a, b, preferred_element_type=F32)


def _dot_nt(a, b):
    return lax.dot_general(a, b, _NT, preferred_element_type=F32)


def _dot_tn(a, b):
    return lax.dot_general(a, b, _TN, preferred_element_type=F32)


def _params(**kw):
    return pltpu.CompilerParams(vmem_limit_bytes=VMEM_LIMIT, **kw)


def _silu_parts(g):
    sig = jax.nn.sigmoid(g)
    return g * sig, sig * (1.0 + g * (1.0 - sig))


def _tables():
    pos = np.arange(T, dtype=np.float32) - PAD
    inv = (ROPE_BASE ** (-np.arange(0, RET_DK, 2, dtype=np.float32) / RET_DK)).astype(np.float32)
    ang = pos[:, None] * inv[None, :]
    cos, sin = np.cos(ang), np.sin(ang)
    cosf = np.concatenate([cos, cos], axis=1).astype(np.float32)
    sins = np.concatenate([-sin, sin], axis=1).astype(np.float32)
    h = np.arange(RET_HEADS, dtype=np.float32)
    log_gamma = np.log1p(-np.exp2(-5.0 - h)).astype(np.float32)
    idx = np.arange(CHUNK, dtype=np.float32)
    diff = idx[:, None] - idx[None, :]
    dmask = np.where(diff[None] >= 0,
                     np.exp(log_gamma[:, None, None] * np.maximum(diff, 0.0)[None]), 0.0)
    zeta = np.exp(log_gamma[:, None] * (CHUNK - 1.0 - idx)[None, :])
    xi = np.exp(log_gamma[:, None] * (idx + 1.0)[None, :])
    cdec = np.exp(log_gamma * CHUNK)
    return (jnp.asarray(cosf), jnp.asarray(sins), jnp.asarray(dmask, F32),
            jnp.asarray(zeta[:, :, None], F32), jnp.asarray(xi[:, :, None], F32),
            jnp.asarray(cdec[:, None, None], F32))


W_STRIDE = 896
W_SLAB = 912
W_EDGE = W_SLAB - W_STRIDE
def _slab(block, me):
    return lax.dynamic_update_slice(jnp.zeros((W_SLAB, block.shape[1]), block.dtype), block,
                                    ((W_BLK - W_STRIDE) * me, 0))


def _join_edges(slabs):
    last = slabs[:, W_STRIDE:]
    first = slabs[:, :W_EDGE] + jnp.concatenate([jnp.zeros_like(last[:1]), last[:-1]], axis=0)
    tail = jnp.pad(last[N_DEV - 1], ((0, 128 - W_EDGE), (0, 0)))
    return lax.dynamic_update_slice(slabs, first, (0, 0, 0)), tail


def _mm_tn(a, b, *, tm, tn, name):
    k, m = a.shape
    n = b.shape[1]
    assert m % tm == 0 and n % tn == 0

    def body(a_ref, b_ref, o_ref):
        o_ref[...] = _dot_tn(a_ref[...], b_ref[...]).astype(BF16)

    return pl.pallas_call(
        body, name=name, grid=(n // tn, m // tm),
        in_specs=[pl.BlockSpec((k, tm), lambda j, i: (0, i)),
                  pl.BlockSpec((k, tn), lambda j, i: (0, j))],
        out_specs=pl.BlockSpec((tm, tn), lambda j, i: (i, j)),
        out_shape=jax.ShapeDtypeStruct((m, n), BF16),
        compiler_params=_params(dimension_semantics=("arbitrary", "arbitrary")),
    )(a, b)


def _piece_spec(base, mult):
    def index(i):
        p = base + mult * i
        return p // 7, p % 7, 0
    return pl.BlockSpec((1, 128, D_MODEL), index)


_RET_PIECES = ((0, 1), (4, 1), (8, 2), (9, 2), (16, 2), (17, 2))
_FOX_PIECES = ((24, 1), (32, 1), (40, 1), (48, 1))


def _rms_z_ret(h_pad, g, slabs):
    def body(h_ref, g_ref, *refs):
        pieces, u_ref, z_ref = refs[:6], refs[6], refs[7]

        @pl.when(pl.program_id(0) == 0)
        def _():
            h = h_ref[...]
            r = lax.rsqrt(jnp.mean(h * h, axis=-1, keepdims=True) + EPS)
            u_ref[...] = (h * r * g_ref[...]).astype(BF16)

        w = jnp.concatenate([p[0] for p in pieces], axis=0)
        z_ref[...] = _dot_nt(u_ref[...], w).astype(BF16)

    whole = pl.BlockSpec((T, D_MODEL), lambda i: (0, 0))
    return pl.pallas_call(
        body, name="rms_z_ret", grid=(RET_HEADS,),
        in_specs=[whole, pl.BlockSpec((1, D_MODEL), lambda i: (0, 0))]
        + [_piece_spec(*bm) for bm in _RET_PIECES],
        out_specs=[whole, pl.BlockSpec((T, RET_W), lambda i: (0, i))],
        out_shape=[jax.ShapeDtypeStruct((T, D_MODEL), BF16),
                   jax.ShapeDtypeStruct((T, D_IN_PAD), BF16)],
        compiler_params=_params(dimension_semantics=("arbitrary",)),
    )(h_pad, g, *([slabs] * 6))


def _z_fox(u, slabs, tail, z):
    def body(u_ref, *refs):
        pieces, t_ref, z_ref, zff_ref = refs[:4], refs[4], refs[6], refs[7]
        u = u_ref[...]
        w = jnp.concatenate([p[0] for p in pieces], axis=0)
        z_ref[...] = _dot_nt(u, w).astype(BF16)

        @pl.when(pl.program_id(0) == 0)
        def _():
            zff_ref[...] = _dot_nt(u, t_ref[...])

    return pl.pallas_call(
        body, name="z_fox", grid=(FOX_PAIRS,),
        in_specs=[pl.BlockSpec((T, D_MODEL), lambda i: (0, 0))] + [_piece_spec(*bm) for bm in _FOX_PIECES]
        + [pl.BlockSpec((128, D_MODEL), lambda i: (0, 0)), ANY],
        out_specs=[_FOX_Z_SPEC, pl.BlockSpec((T, 128), lambda i: (0, 0))],
        out_shape=[jax.ShapeDtypeStruct((T, D_IN_PAD), BF16),
                   jax.ShapeDtypeStruct((T, 128), F32)],
        input_output_aliases={6: 0},
        compiler_params=_params(dimension_semantics=("arbitrary",)),
    )(u, *([slabs] * 4), tail, z)


def _out_loss_dy(y, w_out_b, h_pad, target, g):
    tm = T // 4

    def body(y_ref, w_ref, h_ref, t_hbm, g_ref, d_ref, db_ref, dy_ref, loss_ref, dg_ref, t_buf, t_sem):
        i = pl.program_id(0)
        head = pltpu.make_async_copy(t_hbm.at[pl.ds(0, tm - CHUNK)], t_buf.at[pl.ds(CHUNK, tm - CHUNK)], t_sem)
        rest = pltpu.make_async_copy(t_hbm.at[pl.ds(pl.multiple_of(jnp.maximum(i, 1) * tm - CHUNK, 8), tm)],
                                     t_buf, t_sem)

        @pl.when(i == 0)
        def _():
            t_buf[pl.ds(0, CHUNK), :] = jnp.zeros((CHUNK, D_MODEL), F32)
            head.start()
            loss_ref[...] = jnp.zeros_like(loss_ref)
            dg_ref[...] = jnp.zeros_like(dg_ref)

        pl.when(i > 0)(rest.start)

        w = w_ref[...]
        o = _dot(y_ref[...], w) + h_ref[...]
        pl.when(i == 0)(head.wait)
        pl.when(i > 0)(rest.wait)
        token = lax.broadcasted_iota(jnp.int32, (tm, 1), 0) + i * tm >= CHUNK
        g = g_ref[...]
        r = lax.rsqrt(jnp.mean(o * o, axis=-1, keepdims=True) + EPS)
        xn = o * r
        e = jnp.where(token, xn * g - t_buf[...], 0.0)
        loss_ref[...] += jnp.full(loss_ref.shape, 0.5 / D_MODEL * jnp.sum(e * e), F32)
        do = e * (1.0 / D_MODEL)
        dg_ref[...] += jnp.sum(do * xn, axis=0, keepdims=True)
        dn = do * g
        d = r * (dn - xn * jnp.mean(dn * xn, axis=-1, keepdims=True))
        d_b = d.astype(BF16)
        d_ref[...] = d
        db_ref[...] = d_b
        dy_ref[...] = _dot_nt(d_b, w).astype(BF16)

    tile = pl.BlockSpec((tm, D_MODEL), lambda i: (i, 0))
    wide = pl.BlockSpec((tm, D_MIX), lambda i: (i, 0))
    return pl.pallas_call(
        body, name="out_loss_dy", grid=(T // tm,),
        in_specs=[wide, pl.BlockSpec((D_MIX, D_MODEL), lambda i: (0, 0)), tile, ANY,
                  pl.BlockSpec((1, D_MODEL), lambda i: (0, 0))],
        out_specs=[tile, tile, wide,
                   pl.BlockSpec((8, 128), lambda i: (0, 0)),
                   pl.BlockSpec((1, D_MODEL), lambda i: (0, 0))],
        out_shape=[jax.ShapeDtypeStruct((T, D_MODEL), F32),
                   jax.ShapeDtypeStruct((T, D_MODEL), BF16),
                   jax.ShapeDtypeStruct((T, D_MIX), BF16),
                   jax.ShapeDtypeStruct((8, 128), F32),
                   jax.ShapeDtypeStruct((1, D_MODEL), F32)],
        scratch_shapes=[pltpu.VMEM((tm, D_MODEL), F32), pltpu.SemaphoreType.DMA],
        compiler_params=_params(dimension_semantics=("arbitrary",)),
    )(y, w_out_b, h_pad, target, g)


def _coords():
    return lax.axis_index("x"), lax.axis_index("y"), lax.axis_index("c")


def _flip(v, bit):
    return 1 - v if bit else v


def _peer(x, y, c, r):
    return _flip(x, (r >> 2) & 1), _flip(y, (r >> 1) & 1), _flip(c, r & 1)


def _direct_exchange(ins, outs, send_sems, recv_sems, local_sems, gather):
    x, y, c = _coords()
    me = 4 * x + 2 * y + c

    def src(k, to_idx):
        return ins[k] if gather else ins[k].at[to_idx]

    local = [pltpu.make_async_copy(src(k, me), outs[k].at[me], local_sems.at[k])
             for k in range(len(ins))]
    sends, recvs = [], []
    for r in range(1, N_DEV):
        px, py, pc = _peer(x, y, c, r)
        peer = 4 * px + 2 * py + pc
        for k in range(len(ins)):
            sems = dict(send_sem=send_sems.at[k, r - 1], recv_sem=recv_sems.at[k, r - 1],
                        device_id=(px, py, pc), device_id_type=MESH)
            sends.append(pltpu.make_async_remote_copy(src_ref=src(k, peer), dst_ref=outs[k].at[me], **sems))
            recvs.append(pltpu.make_async_remote_copy(src_ref=src(k, peer), dst_ref=outs[k].at[peer], **sems))

    def start():
        for cp in local + sends:
            cp.start()

    def wait():
        for cp in recvs:
            cp.wait_recv()
        for cp in sends:
            cp.wait_send()
        for cp in local:
            cp.wait()

    return start, wait


def _exchange_sems(n_arr):
    return [pltpu.SemaphoreType.DMA((n_arr, N_DEV - 1)), pltpu.SemaphoreType.DMA((n_arr, N_DEV - 1)),
            pltpu.SemaphoreType.DMA((n_arr,))]


def _exchange_shape(a, gather):
    return jax.ShapeDtypeStruct(((N_DEV,) + a.shape) if gather else a.shape, a.dtype)


def _gather_two_level(arrays, name):
    n_arr = len(arrays)

    def body(*refs):
        ins, outs = refs[:n_arr], refs[n_arr:2 * n_arr]
        send_sems, recv_sems, local_sems = refs[2 * n_arr:]
        x, y, c = _coords()

        def slot(k, px, py, pc):
            return outs[k].at[4 * px + 2 * py + pc]

        def routed(core):
            me, sibling = (x, y, core), (x, y, 1 - core)
            xn, yn, dg = (1 - x, y), (x, 1 - y), (1 - x, 1 - y)
            (first, s_first), (second, s_second) = ((xn, 1), (yn, 2)) if core == 0 else ((yn, 2), (xn, 1))

            def copy(k, j, block, to, own=False):
                return pltpu.make_async_remote_copy(
                    src_ref=ins[k] if own else slot(k, *block), dst_ref=slot(k, *block),
                    send_sem=send_sems.at[k, j], recv_sem=recv_sems.at[k, j],
                    device_id=to, device_id_type=MESH)

            local = [pltpu.make_async_copy(ins[k], slot(k, *me), local_sems.at[k]) for k in range(n_arr)]
            sent = []
            for k in range(n_arr):
                sent += [copy(k, 0, me, sibling, True), copy(k, 1, me, (*xn, core), True),
                         copy(k, 2, me, (*yn, core), True)]
            for cp in local + sent:
                cp.start()

            def pass_on(k, j_from, j_to, block, targets):
                copy(k, j_from, block, me).wait_recv()
                for j, to in zip(j_to, targets):
                    cp = copy(k, j, block, to)
                    cp.start()
                    sent.append(cp)

            for k in range(n_arr):
                pass_on(k, s_first, (3, 3 + s_first), (*first, core), ((*second, core), sibling))
            for k in range(n_arr):
                pass_on(k, s_second, (3 + s_second,), (*second, core), (sibling,))
            for k in range(n_arr):
                pass_on(k, 3, (6,), (*dg, core), (sibling,))
            for k in range(n_arr):
                copy(k, 0, sibling, me).wait_recv()
                for j, chip in ((4, xn), (5, yn), (6, dg)):
                    copy(k, j, (*chip, 1 - core), me).wait_recv()
            for cp in sent:
                cp.wait_send()
            for cp in local:
                cp.wait()

        for core in (0, 1):
            pl.when(c == core)(lambda core=core: routed(core))

    return pl.pallas_call(
        body, name=name,
        in_specs=[ANY] * n_arr, out_specs=[ANY] * n_arr,
        out_shape=[_exchange_shape(a, True) for a in arrays],
        scratch_shapes=_exchange_sems(n_arr),
    )(*arrays)


def _piece_columns():
    pos = {}
    for h in range(RET_HEADS):
        for k, p in enumerate((h, 4 + h, 8 + 2 * h, 9 + 2 * h, 16 + 2 * h, 17 + 2 * h)):
            pos[p] = 6 * h + k
    for p in range(FOX_PAIRS):
        for i in range(4):
            pos[24 + 8 * i + p] = 24 + 4 * p + i
    pos[D_IN_PAD // 128 - 1] = D_IN_PAD // 128 - 1
    return np.array([pos[7 * d + j] for d in range(N_DEV) for j in range(8)], np.int32)


def _dwin_pair_slabs(dz, u, core):
    def body(order_ref, cols_ref, *refs):
        pieces, u_ref, pair_ref, theirs_ref = refs[:8], refs[8], refs[9], refs[10]
        send_buf, got_buf, send_sems, recv_sems, load_sem = refs[11:]
        s = pl.program_id(0)
        x, y, c = _coords()
        cols = jnp.concatenate([p[...] for p in pieces], axis=1)
        slab = _dot_tn(cols, u_ref[...])[:W_SLAB]

        def push(q):
            return pltpu.make_async_remote_copy(
                src_ref=send_buf.at[q], dst_ref=theirs_ref.at[q],
                send_sem=send_sems.at[q], recv_sem=recv_sems.at[q],
                device_id=(x, y, 1 - c), device_id_type=MESH)

        for q in range(N_CHIP):
            @pl.when(s == q)
            def _(q=q):
                send_buf[q] = slab.astype(BF16)
                push(q).start()

            @pl.when(s == N_CHIP + q)
            def _(q=q):
                push(q).wait_recv()
                load = pltpu.make_async_copy(theirs_ref.at[q], got_buf, load_sem)
                load.start()
                load.wait()
                pair_ref[0] = (slab + got_buf[...].astype(F32)).astype(BF16)

        @pl.when(s == 2 * N_CHIP - 1)
        def _():
            for q in range(N_CHIP):
                push(q).wait_send()

    order = jnp.concatenate([2 * jnp.arange(N_CHIP) + (1 - core), 2 * jnp.arange(N_CHIP) + core]).astype(jnp.int32)
    piece = lambda j: pl.BlockSpec((T, 128), lambda s, order_ref, cols_ref: (0, cols_ref[order_ref[s] * 8 + j]))
    slabs = jax.ShapeDtypeStruct((N_CHIP, W_SLAB, D_MODEL), BF16)
    pair, _ = pl.pallas_call(
        body, name="dwin_pair_slabs",
        grid_spec=pltpu.PrefetchScalarGridSpec(
            num_scalar_prefetch=2, grid=(2 * N_CHIP,),
            in_specs=[piece(j) for j in range(8)]
            + [pl.BlockSpec((T, D_MODEL), lambda s, order_ref, cols_ref: (0, 0))],
            out_specs=[pl.BlockSpec((1, W_SLAB, D_MODEL),
                                    lambda s, order_ref, cols_ref: (jnp.maximum(s - N_CHIP, 0), 0, 0)), ANY],
            scratch_shapes=[pltpu.VMEM((N_CHIP, W_SLAB, D_MODEL), BF16), pltpu.VMEM((W_SLAB, D_MODEL), BF16),
                            pltpu.SemaphoreType.DMA((N_CHIP,)), pltpu.SemaphoreType.DMA((N_CHIP,)),
                            pltpu.SemaphoreType.DMA]),
        out_shape=[slabs, slabs],
        compiler_params=_params(dimension_semantics=("arbitrary",)),
    )(order, jnp.asarray(_piece_columns()), *([dz] * 8), u)
    return pair


def _du_rms(dz, slabs, tail, h_pad, dout, g, pair_blocks):
    tm = 272
    steps = T // tm
    merge_step = steps // 2
    columns = _piece_columns().reshape(N_DEV, 8)

    def body(dz_ref, w_ref, t_ref, h_ref, d_ref, g_ref, p_ref, dh_ref, dg_ref, got_ref,
             via_buf, mine_buf, send_sems, recv_sems, local_sems):
        i = pl.program_id(0)
        x, y, c = _coords()

        def routed(core, step):
            first, second = ((1 - x, y), (x, 1 - y)) if core == 0 else ((x, 1 - y), (1 - x, y))
            chip = lambda px, py: 2 * px + py

            def send(j, src, dst, to):
                return pltpu.make_async_remote_copy(
                    src_ref=src, dst_ref=dst, send_sem=send_sems.at[j], recv_sem=recv_sems.at[j],
                    device_id=(*to, core), device_id_type=MESH)

            onward = send(1, p_ref.at[chip(1 - x, 1 - y)], via_buf, first)
            direct = send(0, p_ref.at[chip(*first)], got_ref.at[1], first)
            summed = send(2, mine_buf, got_ref.at[2], second)
            own = pltpu.make_async_copy(p_ref.at[chip(x, y)], got_ref.at[0], local_sems.at[0])
            load = pltpu.make_async_copy(p_ref.at[chip(*second)], mine_buf, local_sems.at[1])
            if step == "start":
                for cp in (onward, direct, own, load):
                    cp.start()
            elif step == "merge":
                onward.wait_recv()
                load.wait()
                mine_buf[...] = (mine_buf[...].astype(F32) + via_buf[...].astype(F32)).astype(BF16)
                summed.start()
            else:
                direct.wait_recv()
                summed.wait_recv()
                for cp in (onward, direct, summed):
                    cp.wait_send()
                own.wait()

        for core in (0, 1):
            pl.when(jnp.logical_and(c == core, i == 0))(lambda core=core: routed(core, "start"))

        @pl.when(i == 0)
        def _():
            dg_ref[...] = jnp.zeros_like(dg_ref)

        du = _dot(dz_ref[:, pl.ds(FF_BASE, 128)], t_ref[...])
        for d in range(N_DEV):
            cols = jnp.concatenate([dz_ref[:, pl.ds(128 * int(columns[d, j]), 128)] for j in range(7)], axis=1)
            du = du + _dot(cols, w_ref[d, pl.ds(0, W_STRIDE), :])
        h = h_ref[...]
        r = lax.rsqrt(jnp.mean(h * h, axis=-1, keepdims=True) + EPS)
        xn = h * r
        dg_ref[...] += jnp.sum(du * xn, axis=0, keepdims=True)
        dn = du * g_ref[...]
        dh_ref[...] = d_ref[...] + r * (dn - xn * jnp.mean(dn * xn, axis=-1, keepdims=True))

        for core in (0, 1):
            pl.when(jnp.logical_and(c == core, i == merge_step))(lambda core=core: routed(core, "merge"))
            pl.when(jnp.logical_and(c == core, i == steps - 1))(lambda core=core: routed(core, "finish"))

    tile = pl.BlockSpec((tm, D_MODEL), lambda i: (i, 0))
    slab = pair_blocks.shape[1:]
    return pl.pallas_call(
        body, name="du_rms", grid=(steps,),
        in_specs=[pl.BlockSpec((tm, D_IN_PAD), lambda i: (i, 0)),
                  pl.BlockSpec((N_DEV, W_SLAB, D_MODEL), lambda i: (0, 0, 0)),
                  pl.BlockSpec((128, D_MODEL), lambda i: (0, 0)),
                  tile, tile, pl.BlockSpec((1, D_MODEL), lambda i: (0, 0)), ANY],
        out_specs=[tile, pl.BlockSpec((1, D_MODEL), lambda i: (0, 0)), ANY],
        out_shape=[jax.ShapeDtypeStruct((T, D_MODEL), F32),
                   jax.ShapeDtypeStruct((1, D_MODEL), F32),
                   jax.ShapeDtypeStruct((3,) + slab, pair_blocks.dtype)],
        scratch_shapes=[pltpu.VMEM(slab, pair_blocks.dtype), pltpu.VMEM(slab, pair_blocks.dtype),
                        pltpu.SemaphoreType.DMA((3,)), pltpu.SemaphoreType.DMA((3,)),
                        pltpu.SemaphoreType.DMA((2,))],
        compiler_params=_params(dimension_semantics=("arbitrary",)),
    )(dz, slabs, tail, h_pad, dout, g, pair_blocks)


def _tri(lower):
    r = lax.broadcasted_iota(jnp.int32, (CHUNK, CHUNK), 0)
    c = lax.broadcasted_iota(jnp.int32, (CHUNK, CHUNK), 1)
    return jnp.where((r >= c) if lower else (r <= c), 1.0, 0.0).astype(F32)


def _row_valid(n):
    r = lax.broadcasted_iota(jnp.int32, (CHUNK, 128), 0) + n * CHUNK
    return r >= PAD


_FF_SPEC = pl.BlockSpec((T, 128), lambda i: (0, FF_BASE // 128))
_ZFF_SPEC = pl.BlockSpec((T, 128), lambda i: (0, 0))


def _forget_fwd(z, b_pad):
    def body(z_ref, b_ref, o_ref):
        tri = _tri(True)
        carry = jnp.zeros((1, 128), F32)
        for n in range(NCHUNK):
            rows = pl.ds(n * CHUNK, CHUNK)
            a = z_ref[rows, :] + b_ref[...]
            lf = -(jnp.maximum(-a, 0.0) + jnp.log(1.0 + jnp.exp(-jnp.abs(a))))
            lf = jnp.where(_row_valid(n), lf, 0.0)
            c = jnp.dot(tri, lf, precision=lax.Precision.HIGHEST,
                        preferred_element_type=F32) + carry
            carry = c[CHUNK - 1:CHUNK, :]
            o_ref[:, rows] = jnp.where(_row_valid(n), c * (-LOG2E), NEG_INF).T

    return pl.pallas_call(
        body, name="forget_fwd", grid=(1,),
        in_specs=[_ZFF_SPEC, pl.BlockSpec((1, 128), lambda i: (0, 0))],
        out_specs=pl.BlockSpec((128, T), lambda i: (0, 0)),
        out_shape=jax.ShapeDtypeStruct((128, T), F32),
        compiler_params=_params(dimension_semantics=("arbitrary",)),
    )(z, b_pad)


def _forget_bwd(z, b_pad, dc, dz):
    def body(z_ref, b_ref, dc_ref, dz_in, dff_ref, db_ref):
        tri = _tri(False)
        carry = jnp.zeros((1, 128), F32)
        db = jnp.zeros((1, 128), F32)
        for n in reversed(range(NCHUNK)):
            rows = pl.ds(n * CHUNK, CHUNK)
            dc_blk = jnp.concatenate([dc_ref[:, rows], jnp.zeros((128 - FOX_HEADS, CHUNK), F32)], axis=0).T
            dlf = jnp.dot(tri, dc_blk, precision=lax.Precision.HIGHEST,
                          preferred_element_type=F32) + carry
            carry = dlf[0:1, :]
            a = z_ref[rows, :] + b_ref[...]
            dff = jnp.where(_row_valid(n), dlf * jax.nn.sigmoid(-a), 0.0)
            dff_ref[rows, :] = dff.astype(BF16)
            db = db + jnp.sum(dff, axis=0, keepdims=True)
        db_ref[...] = db

    return pl.pallas_call(
        body, name="forget_bwd", grid=(1,),
        in_specs=[_ZFF_SPEC, pl.BlockSpec((1, 128), lambda i: (0, 0)),
                  pl.BlockSpec((FOX_HEADS, T), lambda i: (0, 0)), ANY],
        out_specs=[_FF_SPEC, pl.BlockSpec((1, 128), lambda i: (0, 0))],
        out_shape=[jax.ShapeDtypeStruct((T, D_IN_PAD), BF16),
                   jax.ShapeDtypeStruct((1, 128), F32)],
        input_output_aliases={3: 0},
        compiler_params=_params(dimension_semantics=("arbitrary",)),
    )(z, b_pad, dc, dz)


FOX_QB = 512
FOX_NQB = SEQ // FOX_QB


def _fox_block(b):
    lo = CHUNK + b * FOX_QB
    return pl.ds(lo, FOX_QB), lo, lo + FOX_QB


def _causal_bias():
    r = lax.broadcasted_iota(jnp.int32, (FOX_QB, FOX_QB), 0)
    c = lax.broadcasted_iota(jnp.int32, (FOX_QB, FOX_QB), 1)
    return jnp.where(c <= r, 0.0, NEG_INF).astype(F32)


def _fox_logits(q_blk, k_all, bias, causal, b):
    _, lo, hi = _fox_block(b)
    s_off = _dot_nt(q_blk, k_all[:lo]) + bias[:, :lo]
    s_dia = _dot_nt(q_blk, k_all[lo:hi]) + (bias[:, lo:hi] + causal)
    return s_off, s_dia


_FOX_Z_SPEC = pl.BlockSpec((T, FOX_W), lambda p: (0, FOX_BASE // FOX_W + p))
_FOX_BIAS_SPEC = pl.BlockSpec((2, 1, T), lambda p: (p, 0, 0))
_FOX_LSE_SPEC = pl.BlockSpec((2, T, 1), lambda p: (p, 0, 0))
_FOX_SCALE = FOX_D ** -0.5
_FOX_QSCALE = _FOX_SCALE * LOG2E


def _fox_fwd(z, bias, y, w_out_blk):
    last = FOX_PAIRS - 1

    def body(z_ref, b_ref, y_in, w_ref, a_ref, lse_ref, y_ref, wall_ref,
             send_sems, recv_sems, local_sems):
        start, wait = _direct_exchange([w_ref], [wall_ref], send_sems, recv_sems, local_sems, True)
        pl.when(pl.program_id(0) == 0)(start)

        causal = _causal_bias()
        a_ref[pl.ds(0, CHUNK), :] = jnp.zeros((CHUNK, 128), F32)
        y_ref[pl.ds(0, CHUNK), :] = jnp.zeros((CHUNK, 128), BF16)
        for j in range(2):
            lanes = pl.ds(j * FOX_D, FOX_D)
            k_all = z_ref[:, pl.ds(128 + j * FOX_D, FOX_D)]
            v_all = z_ref[:, pl.ds(256 + j * FOX_D, FOX_D)]
            bias = b_ref[j]
            lse_ref[j, pl.ds(0, CHUNK), :] = jnp.zeros((CHUNK, 1), F32)
            for b in range(FOX_NQB):
                rows, lo, hi = _fox_block(b)
                q_blk = (z_ref[rows, lanes].astype(F32) * _FOX_QSCALE).astype(BF16)
                s_off, s_dia = _fox_logits(q_blk, k_all, bias, causal, b)
                m = jnp.maximum(jnp.max(s_off, axis=-1, keepdims=True),
                                jnp.max(s_dia, axis=-1, keepdims=True))
                e_off = jnp.exp2(s_off - m)
                e_dia = jnp.exp2(s_dia - m)
                total = jnp.sum(e_off, axis=-1, keepdims=True) + jnp.sum(e_dia, axis=-1, keepdims=True)
                o = (_dot(e_off.astype(BF16), v_all[:lo]) + _dot(e_dia.astype(BF16), v_all[lo:hi])) / total
                a_ref[rows, lanes] = o
                lse_ref[j, rows, :] = m + jnp.log(total) * LOG2E
                gate = _silu_parts(z_ref[rows, pl.ds(384 + j * FOX_D, FOX_D)].astype(F32))[0]
                y_ref[rows, lanes] = (o * gate).astype(BF16)

        pl.when(pl.program_id(0) == last)(wait)

    return pl.pallas_call(
        body, name="fox_fwd", grid=(FOX_PAIRS,),
        in_specs=[_FOX_Z_SPEC, _FOX_BIAS_SPEC, ANY, ANY],
        out_specs=[pl.BlockSpec((T, 128), lambda p: (0, p)), _FOX_LSE_SPEC,
                   pl.BlockSpec((T, 128), lambda p: (0, 8 + p)), ANY],
        out_shape=[jax.ShapeDtypeStruct((T, FOX_HEADS * FOX_D), F32),
                   jax.ShapeDtypeStruct((FOX_HEADS, T, 1), F32),
                   jax.ShapeDtypeStruct((T, D_MIX), BF16),
                   _exchange_shape(w_out_blk, True)],
        input_output_aliases={2: 2},
        scratch_shapes=_exchange_sems(1),
        compiler_params=_params(dimension_semantics=("arbitrary",)),
    )(z, bias, y, w_out_blk)


def _fox_bwd(z, bias, a_f, lse, dy, dz, dwo_blocks):
    last = FOX_PAIRS - 1

    def body(z_ref, b_ref, a_ref, lse_ref, dy_ref, dz_in, dwo_ref, dz_ref, dc_ref, got_ref,
             kv_acc, dc_acc, send_sems, recv_sems, local_sems):
        start, wait = _direct_exchange([dwo_ref], [got_ref], send_sems, recv_sems, local_sems, False)
        pl.when(pl.program_id(0) == 0)(start)

        causal = _causal_bias()
        dz_ref[pl.ds(0, CHUNK), pl.ds(0, 128)] = jnp.zeros((CHUNK, 128), BF16)
        dz_ref[pl.ds(0, CHUNK), pl.ds(384, 128)] = jnp.zeros((CHUNK, 128), BF16)
        dk_rows, dv_rows = pl.ds(0, FOX_D), pl.ds(FOX_D, FOX_D)
        for j in range(2):
            lanes = pl.ds(j * FOX_D, FOX_D)
            k_all = z_ref[:, pl.ds(128 + j * FOX_D, FOX_D)]
            v_all = z_ref[:, pl.ds(256 + j * FOX_D, FOX_D)]
            bias = b_ref[j]
            kv_acc[...] = jnp.zeros_like(kv_acc)
            dc_acc[...] = jnp.zeros_like(dc_acc)
            for b in range(FOX_NQB):
                rows, lo, hi = _fox_block(b)
                off, dia = pl.ds(0, lo), pl.ds(lo, FOX_QB)
                q_blk = (z_ref[rows, lanes].astype(F32) * _FOX_QSCALE).astype(BF16)
                s_off, s_dia = _fox_logits(q_blk, k_all, bias, causal, b)
                lse_blk = lse_ref[j, rows, :]
                p_off, p_dia = jnp.exp2(s_off - lse_blk), jnp.exp2(s_dia - lse_blk)
                sg, dsg = _silu_parts(z_ref[rows, pl.ds(384 + j * FOX_D, FOX_D)].astype(F32))
                dyj = dy_ref[rows, lanes].astype(F32)
                dz_ref[rows, pl.ds(384 + j * FOX_D, FOX_D)] = (dyj * a_ref[rows, lanes] * dsg).astype(BF16)
                do_b = (dyj * sg).astype(BF16)
                dp_off = _dot_nt(do_b, v_all[:lo])
                dp_dia = _dot_nt(do_b, v_all[lo:hi])
                d = (jnp.sum(p_off * dp_off, axis=-1, keepdims=True)
                     + jnp.sum(p_dia * dp_dia, axis=-1, keepdims=True))
                ds_off = p_off * (dp_off - d)
                ds_dia = p_dia * (dp_dia - d)
                dc_acc[:, off] -= jnp.sum(ds_off, axis=0, keepdims=True)
                dc_acc[:, dia] -= jnp.sum(ds_dia, axis=0, keepdims=True)
                ds_off_b, ds_dia_b = ds_off.astype(BF16), ds_dia.astype(BF16)
                dq = _dot(ds_off_b, k_all[:lo]) + _dot(ds_dia_b, k_all[lo:hi])
                dz_ref[rows, lanes] = (dq * _FOX_SCALE).astype(BF16)
                kv_acc[dk_rows, off] += _dot_tn(q_blk, ds_off_b)
                kv_acc[dk_rows, dia] += _dot_tn(q_blk, ds_dia_b)
                kv_acc[dv_rows, off] += _dot_tn(do_b, p_off.astype(BF16))
                kv_acc[dv_rows, dia] += _dot_tn(do_b, p_dia.astype(BF16))
            for n in range(NCHUNK):
                rows = pl.ds(n * CHUNK, CHUNK)
                both = kv_acc[:, rows].T
                dz_ref[rows, pl.ds(128 + j * FOX_D, FOX_D)] = (both[:, :FOX_D] * LN2).astype(BF16)
                dz_ref[rows, pl.ds(256 + j * FOX_D, FOX_D)] = both[:, FOX_D:].astype(BF16)
            dc_ref[j] = dc_acc[...]

        pl.when(pl.program_id(0) == last)(wait)

    col = lambda base: pl.BlockSpec((T, 128), lambda p: (0, base + p))
    return pl.pallas_call(
        body, name="fox_bwd", grid=(FOX_PAIRS,),
        in_specs=[_FOX_Z_SPEC, _FOX_BIAS_SPEC, col(0), _FOX_LSE_SPEC, col(8), ANY, ANY],
        out_specs=[_FOX_Z_SPEC, _FOX_BIAS_SPEC, ANY],
        out_shape=[jax.ShapeDtypeStruct((T, D_IN_PAD), BF16),
                   jax.ShapeDtypeStruct((FOX_HEADS, 1, T), F32),
                   _exchange_shape(dwo_blocks, False)],
        input_output_aliases={5: 0},
        scratch_shapes=[pltpu.VMEM((2 * FOX_D, T), F32), pltpu.VMEM((1, T), F32)] + _exchange_sems(1),
        compiler_params=_params(dimension_semantics=("arbitrary",)),
    )(z, bias, a_f, lse, dy, dz, dwo_blocks)


def _rot(x, cosf, sins):
    return x * cosf + pltpu.roll(x, RET_DK // 2, 1) * sins


def _rot_t(d, cosf, sins):
    return d * cosf - pltpu.roll(d, RET_DK // 2, 1) * sins


_RET_Z_SPEC = pl.BlockSpec((T, RET_W), lambda h: (0, h))
_RET_TABLE_SPECS = [
    pl.BlockSpec((T, RET_DK), lambda h: (0, 0)),
    pl.BlockSpec((T, RET_DK), lambda h: (0, 0)),
    pl.BlockSpec((1, CHUNK, CHUNK), lambda h: (h, 0, 0)),
    pl.BlockSpec((1, CHUNK, 1), lambda h: (h, 0, 0)),
    pl.BlockSpec((1, CHUNK, 1), lambda h: (h, 0, 0)),
    pl.BlockSpec((1, 1, 1), lambda h: (h, 0, 0)),
]
_RQ, _RK = pl.ds(0, RET_DK), pl.ds(RET_DK, RET_DK)
_RV, _RG = pl.ds(2 * RET_DK, RET_DV), pl.ds(2 * RET_DK + RET_DV, RET_DV)
_RET_KSCALE = RET_DK ** -0.5


def _ret_fwd(z, tables):
    def body(z_ref, cos_ref, sin_ref, dm_ref, zeta_ref, xi_ref, cd_ref, raw_ref, y_ref):
        dmask, zeta, xi, cdec = dm_ref[0], zeta_ref[0], xi_ref[0], cd_ref[0]
        state = jnp.zeros((RET_DK, RET_DV), F32)
        for n in range(NCHUNK):
            rows = pl.ds(n * CHUNK, CHUNK)
            cosf, sins = cos_ref[rows, :], sin_ref[rows, :]
            qr = _rot(z_ref[rows, _RQ].astype(F32), cosf, sins)
            kr_b = (_rot(z_ref[rows, _RK].astype(F32), cosf, sins) * _RET_KSCALE).astype(BF16)
            v_b = z_ref[rows, _RV]
            a = _dot_nt(qr.astype(BF16), kr_b) * dmask
            out = _dot(a.astype(BF16), v_b) + _dot((qr * xi).astype(BF16), state.astype(BF16))
            state = state * cdec + _dot_tn(kr_b, (v_b.astype(F32) * zeta).astype(BF16))
            raw_ref[rows, :] = out
            r = lax.rsqrt(jnp.mean(out * out, axis=-1, keepdims=True) + EPS)
            y_ref[rows, :] = (out * r * _silu_parts(z_ref[rows, _RG].astype(F32))[0]).astype(BF16)

    wide = pl.BlockSpec((T, RET_DV), lambda h: (0, h))
    return pl.pallas_call(
        body, name="ret_fwd", grid=(RET_HEADS,),
        in_specs=[_RET_Z_SPEC] + _RET_TABLE_SPECS,
        out_specs=[wide, wide],
        out_shape=[jax.ShapeDtypeStruct((T, RET_HEADS * RET_DV), F32),
                   jax.ShapeDtypeStruct((T, D_MIX), BF16)],
        compiler_params=_params(dimension_semantics=("arbitrary",)),
    )(z, *tables)


def _ret_bwd(z, tables, raw, dy):
    def body(z_ref, cos_ref, sin_ref, dm_ref, zeta_ref, xi_ref, cd_ref, raw_ref, dy_ref,
             dz_ref, st_ref):
        dmask, zeta, xi, cdec = dm_ref[0], zeta_ref[0], xi_ref[0], cd_ref[0]

        def rotated(n):
            rows = pl.ds(n * CHUNK, CHUNK)
            cosf, sins = cos_ref[rows, :], sin_ref[rows, :]
            qr = _rot(z_ref[rows, _RQ].astype(F32), cosf, sins)
            kr_b = (_rot(z_ref[rows, _RK].astype(F32), cosf, sins) * _RET_KSCALE).astype(BF16)
            return rows, cosf, sins, qr, kr_b

        state = jnp.zeros((RET_DK, RET_DV), F32)
        for n in range(NCHUNK):
            st_ref[n] = state.astype(BF16)
            if n + 1 < NCHUNK:
                rows, _, _, _, kr_b = rotated(n)
                state = state * cdec + _dot_tn(kr_b, (z_ref[rows, _RV].astype(F32) * zeta).astype(BF16))

        grad_state = jnp.zeros((RET_DK, RET_DV), F32)
        for n in reversed(range(NCHUNK)):
            rows, cosf, sins, qr, kr_b = rotated(n)
            qr_b = qr.astype(BF16)
            v_b = z_ref[rows, _RV]
            gs_b = grad_state.astype(BF16)
            o = raw_ref[rows, :]
            r = lax.rsqrt(jnp.mean(o * o, axis=-1, keepdims=True) + EPS)
            hn = o * r
            sg, dsg = _silu_parts(z_ref[rows, _RG].astype(F32))
            dyn = dy_ref[rows, :].astype(F32)
            dz_ref[rows, _RG] = (dyn * hn * dsg).astype(BF16)
            dhn = dyn * sg
            do_b = (r * (dhn - hn * jnp.mean(dhn * hn, axis=-1, keepdims=True))).astype(BF16)
            a_b = (_dot_nt(qr_b, kr_b) * dmask).astype(BF16)
            da_b = (_dot_nt(do_b, v_b) * dmask).astype(BF16)
            dqr = _dot(da_b, kr_b) + xi * _dot_nt(do_b, st_ref[n])
            dkr = _dot_tn(da_b, qr_b) + zeta * _dot_nt(v_b, gs_b)
            dv = _dot_tn(a_b, do_b) + zeta * _dot(kr_b, gs_b)
            grad_state = grad_state * cdec + _dot_tn((qr * xi).astype(BF16), do_b)
            dz_ref[rows, _RQ] = _rot_t(dqr, cosf, sins).astype(BF16)
            dz_ref[rows, _RK] = (_rot_t(dkr, cosf, sins) * _RET_KSCALE).astype(BF16)
            dz_ref[rows, _RV] = dv.astype(BF16)

    wide = pl.BlockSpec((T, RET_DV), lambda h: (0, h))
    return pl.pallas_call(
        body, name="ret_bwd", grid=(RET_HEADS,),
        in_specs=[_RET_Z_SPEC] + _RET_TABLE_SPECS + [wide, wide],
        out_specs=_RET_Z_SPEC,
        out_shape=jax.ShapeDtypeStruct((T, D_IN_PAD), BF16),
        scratch_shapes=[pltpu.VMEM((NCHUNK, RET_DK, RET_DV), BF16)],
        compiler_params=_params(dimension_semantics=("arbitrary",)),
    )(z, *tables, raw, dy)


def _adamw(w, g, m, v):
    m = ADAM_B1 * m + (1.0 - ADAM_B1) * g
    v = ADAM_B2 * v + (1.0 - ADAM_B2) * (g * g)
    m_hat = m / (1.0 - ADAM_B1 ** ADAM_STEP)
    v_hat = v / (1.0 - ADAM_B2 ** ADAM_STEP)
    delta = -ADAM_LR * (m_hat / (jnp.sqrt(v_hat) + ADAM_EPS) + ADAM_WD * w)
    return delta, m, v


def _sum_adamw(parts, w, m, v, rows, name):
    _, r_tot, cols = parts.shape
    assert r_tot % rows == 0

    def body(p_ref, w_ref, m_ref, v_ref, g_ref, d_ref, nm_ref, nv_ref):
        g = p_ref[0].astype(F32)
        for d in range(1, N_DEV):
            g = g + p_ref[d].astype(F32)
        delta, nm, nv = _adamw(w_ref[...], g, m_ref[...], v_ref[...])
        g_ref[...] = g
        d_ref[...] = delta
        nm_ref[...] = nm
        nv_ref[...] = nv

    blk = pl.BlockSpec((rows, cols), lambda i: (i, 0))
    return pl.pallas_call(
        body, name=name, grid=(r_tot // rows,),
        in_specs=[pl.BlockSpec((N_DEV, rows, cols), lambda i: (0, i, 0)), blk, blk, blk],
        out_specs=[blk] * 4,
        out_shape=[jax.ShapeDtypeStruct((r_tot, cols), F32)] * 4,
        compiler_params=_params(dimension_semantics=("arbitrary",)),
    )(parts, w, m, v)


def _sum_adamw_w_in(parts, w, m, v, small):
    n_part, r, c = parts.shape
    steps = c // 128

    def body(p_ref, w_hbm, m_hbm, v_hbm, s_ref, g_hbm, d_hbm, nm_hbm, nv_hbm, got_ref,
             in_buf, out_buf, in_sems, out_sems, send_sems, recv_sems, local_sems):
        start, wait = _direct_exchange([s_ref], [got_ref], send_sems, recv_sems, local_sems, True)
        i = pl.program_id(0)
        pl.when(i == 0)(start)
        cols = pl.ds(pl.multiple_of(i * 128, 128), 128)
        loads = [pltpu.make_async_copy(h.at[:, 0, cols], in_buf.at[k], in_sems.at[k])
                 for k, h in enumerate((w_hbm, m_hbm, v_hbm))]
        stores = [pltpu.make_async_copy(out_buf.at[k], h.at[:, 0, cols], out_sems.at[k])
                  for k, h in enumerate((g_hbm, d_hbm, nm_hbm, nv_hbm))]
        for cp in loads:
            cp.start()
        g = p_ref[0].astype(F32)
        for d in range(1, n_part):
            g = g + p_ref[d].astype(F32)
        for cp in loads:
            cp.wait()
        delta, nm, nv = _adamw(in_buf[0], g, in_buf[1], in_buf[2])

        @pl.when(i > 0)
        def _():
            for cp in stores:
                cp.wait()

        for k, val in enumerate((g, delta, nm, nv)):
            out_buf[k] = val
        for cp in stores:
            cp.start()

        @pl.when(i == steps - 1)
        def _():
            for cp in stores:
                cp.wait()
            wait()

    return pl.pallas_call(
        body, name="adamw_w_in", grid=(steps,),
        in_specs=[pl.BlockSpec((n_part, r, 128), lambda i: (0, 0, i)), ANY, ANY, ANY, ANY],
        out_specs=[ANY] * 5,
        out_shape=[jax.ShapeDtypeStruct((r, 1, c), F32)] * 4 + [_exchange_shape(small, True)],
        scratch_shapes=[pltpu.VMEM((3, r, 128), F32), pltpu.VMEM((4, r, 128), F32),
                        pltpu.SemaphoreType.DMA((3,)), pltpu.SemaphoreType.DMA((4,))] + _exchange_sems(1),
        compiler_params=_params(dimension_semantics=("arbitrary",)),
    )(parts, w, m, v, small)


def _adamw_small(got, me, metas, norms, finals, biases):
    def body(me_ref, gm_ref, gr_ref, *refs):
        ins, outs = refs[:12], refs[12:]
        g_meta, g_rest = gm_ref[0], gr_ref[0]
        for d in range(1, N_DEV):
            g_meta, g_rest = g_meta + gm_ref[d], g_rest + gr_ref[d]
        grads = [g_meta, g_rest[0:1], g_rest[1:2], g_rest[2:3, :FOX_HEADS]]
        for k, g in enumerate(grads):
            w_ref, m_ref, v_ref = ins[3 * k:3 * k + 3]
            delta, new_m, new_v = _adamw(w_ref[...], g, m_ref[...], v_ref[...])
            for o_ref, val in zip(outs[4 * k:4 * k + 4], (g, delta, new_m, new_v)):
                o_ref[...] = val
        outs[16][...] = g_rest[3:4, :128]

    groups = (metas, norms, finals, biases)
    full = lambda a: pl.BlockSpec(a.shape, lambda i, me_ref: (0,) * a.ndim)
    flat = [a for grp in groups for a in grp]
    res = pl.pallas_call(
        body, name="adamw_small",
        grid_spec=pltpu.PrefetchScalarGridSpec(
            num_scalar_prefetch=1, grid=(1,),
            in_specs=[pl.BlockSpec((N_DEV, N_META, META_BLK), lambda i, me_ref: (0, 0, me_ref[0])),
                      pl.BlockSpec((N_DEV, 8, D_MODEL), lambda i, me_ref: (0, N_META // 8, 0))]
            + [full(a) for a in flat],
            out_specs=[full(grp[0]) for grp in groups for _ in range(4)]
            + [pl.BlockSpec((1, 128), lambda i, me_ref: (0, 0))]),
        out_shape=[jax.ShapeDtypeStruct(grp[0].shape, F32) for grp in groups for _ in range(4)]
        + [jax.ShapeDtypeStruct((1, 128), F32)],
        compiler_params=_params(dimension_semantics=("arbitrary",)),
    )(me, got, got, *flat)
    return [res[4 * k:4 * k + 4] for k in range(4)], res[16]


def kernel(x, meta_tokens, norm_g, w_in, b_f, w_out, final_g, loss_target, m_meta_tokens, m_norm_g, m_w_in, m_b_f, m_w_out, m_final_g, v_meta_tokens, v_norm_g, v_w_in, v_b_f, v_w_out, v_final_g):
    core = lax.axis_index("c")
    me = 4 * lax.axis_index("x") + 2 * lax.axis_index("y") + core
    tables = _tables()

    wt_all, meta_all = _gather_two_level([_slab(w_in[0].T.astype(BF16), me), meta_tokens], name="gather_w_in")
    slabs, tail = _join_edges(wt_all)
    meta_full = jnp.transpose(meta_all, (1, 0, 2)).reshape(N_META, D_MODEL)
    h_pad = jnp.concatenate([jnp.zeros((PAD, D_MODEL), F32), meta_full, x[0]], axis=0)
    b_pad = jnp.pad(b_f, ((0, 0), (0, 128 - FOX_HEADS)))

    u, z = _rms_z_ret(h_pad, norm_g, slabs)
    z, zff = _z_fox(u, slabs, tail, z)
    bias = _forget_fwd(zff, b_pad)[:FOX_HEADS].reshape(FOX_HEADS, 1, T)
    raw, y = _ret_fwd(z, tables)
    a_f, lse, y, w_out_all = _fox_fwd(z, bias, y, w_out[0].astype(BF16))
    w_out_b = w_out_all.reshape(D_MIX, D_MODEL)
    dout, dout_b, dy, loss_blk, d_final_g = _out_loss_dy(y, w_out_b, h_pad, loss_target[0],
                                                         final_g.reshape(1, D_MODEL))

    d_w_out = _mm_tn(y, dout_b, tm=D_MIX, tn=256, name="mm_dwout")
    dz = _ret_bwd(z, tables, raw, dy)
    dz, dc, got_w_out = _fox_bwd(z, bias, a_f, lse, dy, dz, d_w_out.reshape(N_DEV, WO_BLK, D_MODEL))
    dz, db_f = _forget_bwd(zff, b_pad, dc.reshape(FOX_HEADS, T), dz)

    pair = _dwin_pair_slabs(dz, u, core)
    dh, d_norm_g, got_slabs = _du_rms(dz, slabs, tail, h_pad, dout, norm_g, pair)
    got_w_in = lax.dynamic_slice(got_slabs, (0, (W_BLK - W_STRIDE) * me, 0), (3, W_BLK, D_MODEL))

    small = jnp.concatenate([
        dh[PAD:CHUNK], d_norm_g, d_final_g, jnp.pad(db_f[:, :FOX_HEADS], ((0, 0), (0, D_MODEL - FOX_HEADS))),
        jnp.pad(loss_blk[0:1], ((0, 0), (0, D_MODEL - 128))),
        jnp.zeros((SMALL_ROWS - N_META - 4, D_MODEL), F32)], axis=0)
    fore = lambda a: jnp.transpose(a, (2, 0, 1))
    g_w_in, d_w_in, nm_w_in, nv_w_in, got_small = _sum_adamw_w_in(
        got_w_in, fore(w_in), fore(m_w_in), fore(v_w_in), small)
    g_w_out, d_w_out, nm_w_out, nv_w_out = _sum_adamw(got_w_out, w_out[0], m_w_out[0], v_w_out[0], 128, "adamw_w_out")

    row = lambda a: a.reshape(1, D_MODEL)
    (meta_o, norm_o, final_o, bias_o), loss_row = _adamw_small(
        got_small, me.astype(jnp.int32).reshape(1),
        (meta_tokens, m_meta_tokens, v_meta_tokens), (norm_g, m_norm_g, v_norm_g),
        (row(final_g), row(m_final_g), row(v_final_g)), (b_f, m_b_f, v_b_f))
    final_o = [a.reshape(D_MODEL) for a in final_o]

    back = lambda a: jnp.transpose(a, (1, 2, 0))
    outs = [[meta_o[k], norm_o[k], back(wk), bias_o[k], ok[None], final_o[k]]
            for k, (wk, ok) in enumerate(zip((g_w_in, d_w_in, nm_w_in, nv_w_in),
                                             (g_w_out, d_w_out, nm_w_out, nv_w_out)))]
    return (loss_row[0, 0], dh[CHUNK:][None], *outs[0], *outs[1], *outs[2], *outs[3])
```

```python
import numpy as np
import jax
import jax.numpy as jnp
from jax import lax
from jax.experimental import pallas as pl
from jax.experimental.pallas import tpu as pltpu

F32 = jnp.float32
BF16 = jnp.bfloat16

N_DEV = 8
N_CHIP = 4
D_MODEL = 1024
SEQ = 2048
N_META = 16
CHUNK = 128
PAD = CHUNK - N_META
T = SEQ + CHUNK
NCHUNK = T // CHUNK
D_MIX = 2048
RET_HEADS = 4
RET_DK = 128
RET_DV = 256
RET_W = 2 * RET_DK + 2 * RET_DV
FOX_HEADS = 16
FOX_D = 64
FOX_PAIRS = FOX_HEADS // 2
FOX_W = 4 * 128
FOX_BASE = RET_HEADS * RET_W
FF_BASE = FOX_BASE + FOX_PAIRS * FOX_W
D_IN = 7184
D_IN_PAD = 7296
W_BLK = D_IN // N_DEV
WO_BLK = D_MIX // N_DEV
META_BLK = D_MODEL // N_DEV
EPS = 1e-6
NEG_INF = -1e30
ROPE_BASE = 10000.0
LOG2E = 1.4426950408889634
LN2 = 0.6931471805599453

ADAM_LR = 0.001
ADAM_B1 = 0.9
ADAM_B2 = 0.999
ADAM_EPS = 1e-08
ADAM_WD = 0.01
ADAM_STEP = 10

SMALL_ROWS = 24
VMEM_LIMIT = 56 * 1024 * 1024
MESH = pl.DeviceIdType.MESH
ANY = pl.BlockSpec(memory_space=pl.ANY)

_NT = (((1,), (1,)), ((), ()))
_TN = (((0,), (0,)), ((), ()))


def _dot(a, b):
    return jnp.dot(a, b, preferred_element_type=F32)


def _dot_nt(a, b):
    return lax.dot_general(a, b, _NT, preferred_element_type=F32)


def _dot_tn(a, b):
    return lax.dot_general(a, b, _TN, preferred_element_type=F32)


def _params(**kw):
    return pltpu.CompilerParams(vmem_limit_bytes=VMEM_LIMIT, **kw)


def _silu_parts(g):
    sig = jax.nn.sigmoid(g)
    return g * sig, sig * (1.0 + g * (1.0 - sig))


def _tables():
    pos = np.arange(T, dtype=np.float32) - PAD
    inv = (ROPE_BASE ** (-np.arange(0, RET_DK, 2, dtype=np.float32) / RET_DK)).astype(np.float32)
    ang = pos[:, None] * inv[None, :]
    cos, sin = np.cos(ang), np.sin(ang)
    cosf = np.concatenate([cos, cos], axis=1).astype(np.float32)
    sins = np.concatenate([-sin, sin], axis=1).astype(np.float32)
    h = np.arange(RET_HEADS, dtype=np.float32)
    log_gamma = np.log1p(-np.exp2(-5.0 - h)).astype(np.float32)
    idx = np.arange(CHUNK, dtype=np.float32)
    diff = idx[:, None] - idx[None, :]
    dmask = np.where(diff[None] >= 0,
                     np.exp(log_gamma[:, None, None] * np.maximum(diff, 0.0)[None]), 0.0)
    zeta = np.exp(log_gamma[:, None] * (CHUNK - 1.0 - idx)[None, :])
    xi = np.exp(log_gamma[:, None] * (idx + 1.0)[None, :])
    cdec = np.exp(log_gamma * CHUNK)
    return (jnp.asarray(cosf), jnp.asarray(sins), jnp.asarray(dmask, F32),
            jnp.asarray(zeta[:, :, None], F32), jnp.asarray(xi[:, :, None], F32),
            jnp.asarray(cdec[:, None, None], F32))


W_STRIDE = 896
W_SLAB = 912
W_EDGE = W_SLAB - W_STRIDE
def _slab(block, me):
    shift = W_BLK - W_STRIDE
    return lax.switch(me, [lambda b, d=d: jnp.pad(b, ((shift * d, W_SLAB - W_BLK - shift * d), (0, 0)))
                           for d in range(N_DEV)], block)


def _unslab(slabs, me):
    shift = W_BLK - W_STRIDE
    return lax.switch(me, [lambda s, d=d: s[:, shift * d:shift * d + W_BLK] for d in range(N_DEV)], slabs)


def _join_edges(slabs):
    last = slabs[:, W_STRIDE:]
    first = slabs[:, :W_EDGE] + jnp.concatenate([jnp.zeros_like(last[:1]), last[:-1]], axis=0)
    tail = jnp.pad(last[N_DEV - 1], ((0, 128 - W_EDGE), (0, 0)))
    return lax.dynamic_update_slice(slabs, first, (0, 0, 0)), tail


def _mm_tn(a, b, *, tm, tn, name):
    k, m = a.shape
    n = b.shape[1]
    assert m % tm == 0 and n % tn == 0

    def body(a_ref, b_ref, o_ref):
        o_ref[...] = _dot_tn(a_ref[...], b_ref[...]).astype(BF16)

    return pl.pallas_call(
        body, name=name, grid=(n // tn, m // tm),
        in_specs=[pl.BlockSpec((k, tm), lambda j, i: (0, i)),
                  pl.BlockSpec((k, tn), lambda j, i: (0, j))],
        out_specs=pl.BlockSpec((tm, tn), lambda j, i: (i, j)),
        out_shape=jax.ShapeDtypeStruct((m, n), BF16),
        compiler_params=_params(dimension_semantics=("arbitrary", "arbitrary")),
    )(a, b)


def _piece_spec(base, mult):
    def index(i):
        p = base + mult * i
        return p // 7, p % 7, 0
    return pl.BlockSpec((1, 128, D_MODEL), index)


_RET_PIECES = ((0, 1), (4, 1), (8, 2), (9, 2), (16, 2), (17, 2))
_FOX_PIECES = ((24, 1), (32, 1), (40, 1), (48, 1))


def _rms_z_ret(h_pad, g, slabs):
    def body(h_ref, g_ref, *refs):
        pieces, u_ref, z_ref = refs[:6], refs[6], refs[7]

        @pl.when(pl.program_id(0) == 0)
        def _():
            h = h_ref[...]
            r = lax.rsqrt(jnp.mean(h * h, axis=-1, keepdims=True) + EPS)
            u_ref[...] = (h * r * g_ref[...]).astype(BF16)

        w = jnp.concatenate([p[0] for p in pieces], axis=0)
        z_ref[...] = _dot_nt(u_ref[...], w).astype(BF16)

    whole = pl.BlockSpec((T, D_MODEL), lambda i: (0, 0))
    return pl.pallas_call(
        body, name="rms_z_ret", grid=(RET_HEADS,),
        in_specs=[whole, pl.BlockSpec((1, D_MODEL), lambda i: (0, 0))]
        + [_piece_spec(*bm) for bm in _RET_PIECES],
        out_specs=[whole, pl.BlockSpec((T, RET_W), lambda i: (0, i))],
        out_shape=[jax.ShapeDtypeStruct((T, D_MODEL), BF16),
                   jax.ShapeDtypeStruct((T, D_IN_PAD), BF16)],
        compiler_params=_params(dimension_semantics=("arbitrary",)),
    )(h_pad, g, *([slabs] * 6))


def _z_fox(u, slabs, tail, z):
    def body(u_ref, *refs):
        pieces, t_ref, z_ref, zff_ref = refs[:4], refs[4], refs[6], refs[7]
        u = u_ref[...]
        w = jnp.concatenate([p[0] for p in pieces], axis=0)
        z_ref[...] = _dot_nt(u, w).astype(BF16)

        @pl.when(pl.program_id(0) == 0)
        def _():
            zff_ref[...] = _dot_nt(u, t_ref[...])

    return pl.pallas_call(
        body, name="z_fox", grid=(FOX_PAIRS,),
        in_specs=[pl.BlockSpec((T, D_MODEL), lambda i: (0, 0))] + [_piece_spec(*bm) for bm in _FOX_PIECES]
        + [pl.BlockSpec((128, D_MODEL), lambda i: (0, 0)), ANY],
        out_specs=[_FOX_Z_SPEC, pl.BlockSpec((T, 128), lambda i: (0, 0))],
        out_shape=[jax.ShapeDtypeStruct((T, D_IN_PAD), BF16),
                   jax.ShapeDtypeStruct((T, 128), F32)],
        input_output_aliases={6: 0},
        compiler_params=_params(dimension_semantics=("arbitrary",)),
    )(u, *([slabs] * 4), tail, z)


def _out_loss_dy(y, w_out_b, h_pad, target, g):
    tm = T // 4

    def body(y_ref, w_ref, h_ref, t_hbm, g_ref, d_ref, db_ref, dy_ref, loss_ref, dg_ref, t_buf, t_sem):
        i = pl.program_id(0)
        head = pltpu.make_async_copy(t_hbm.at[pl.ds(0, tm - CHUNK)], t_buf.at[pl.ds(CHUNK, tm - CHUNK)], t_sem)
        rest = pltpu.make_async_copy(t_hbm.at[pl.ds(pl.multiple_of(jnp.maximum(i, 1) * tm - CHUNK, 8), tm)],
                                     t_buf, t_sem)

        @pl.when(i == 0)
        def _():
            t_buf[pl.ds(0, CHUNK), :] = jnp.zeros((CHUNK, D_MODEL), F32)
            head.start()
            loss_ref[...] = jnp.zeros_like(loss_ref)
            dg_ref[...] = jnp.zeros_like(dg_ref)

        pl.when(i > 0)(rest.start)

        w = w_ref[...]
        o = _dot(y_ref[...], w) + h_ref[...]
        pl.when(i == 0)(head.wait)
        pl.when(i > 0)(rest.wait)
        token = lax.broadcasted_iota(jnp.int32, (tm, 1), 0) + i * tm >= CHUNK
        g = g_ref[...]
        r = lax.rsqrt(jnp.mean(o * o, axis=-1, keepdims=True) + EPS)
        xn = o * r
        e = jnp.where(token, xn * g - t_buf[...], 0.0)
        loss_ref[...] += jnp.full(loss_ref.shape, 0.5 / D_MODEL * jnp.sum(e * e), F32)
        do = e * (1.0 / D_MODEL)
        dg_ref[...] += jnp.sum(do * xn, axis=0, keepdims=True)
        dn = do * g
        d = r * (dn - xn * jnp.mean(dn * xn, axis=-1, keepdims=True))
        d_b = d.astype(BF16)
        d_ref[...] = d
        db_ref[...] = d_b
        dy_ref[...] = _dot_nt(d_b, w).astype(BF16)

    tile = pl.BlockSpec((tm, D_MODEL), lambda i: (i, 0))
    wide = pl.BlockSpec((tm, D_MIX), lambda i: (i, 0))
    return pl.pallas_call(
        body, name="out_loss_dy", grid=(T // tm,),
        in_specs=[wide, pl.BlockSpec((D_MIX, D_MODEL), lambda i: (0, 0)), tile, ANY,
                  pl.BlockSpec((1, D_MODEL), lambda i: (0, 0))],
        out_specs=[tile, tile, wide,
                   pl.BlockSpec((8, 128), lambda i: (0, 0)),
                   pl.BlockSpec((1, D_MODEL), lambda i: (0, 0))],
        out_shape=[jax.ShapeDtypeStruct((T, D_MODEL), F32),
                   jax.ShapeDtypeStruct((T, D_MODEL), BF16),
                   jax.ShapeDtypeStruct((T, D_MIX), BF16),
                   jax.ShapeDtypeStruct((8, 128), F32),
                   jax.ShapeDtypeStruct((1, D_MODEL), F32)],
        scratch_shapes=[pltpu.VMEM((tm, D_MODEL), F32), pltpu.SemaphoreType.DMA],
        compiler_params=_params(dimension_semantics=("arbitrary",)),
    )(y, w_out_b, h_pad, target, g)


def _coords():
    return lax.axis_index("x"), lax.axis_index("y"), lax.axis_index("c")


def _flip(v, bit):
    return 1 - v if bit else v


def _peer(x, y, c, r):
    return _flip(x, (r >> 2) & 1), _flip(y, (r >> 1) & 1), _flip(c, r & 1)


def _direct_exchange(ins, outs, send_sems, recv_sems, local_sems, gather):
    x, y, c = _coords()
    me = 4 * x + 2 * y + c

    def src(k, to_idx):
        return ins[k] if gather else ins[k].at[to_idx]

    local = [pltpu.make_async_copy(src(k, me), outs[k].at[me], local_sems.at[k])
             for k in range(len(ins))]
    sends, recvs = [], []
    for r in range(1, N_DEV):
        px, py, pc = _peer(x, y, c, r)
        peer = 4 * px + 2 * py + pc
        for k in range(len(ins)):
            sems = dict(send_sem=send_sems.at[k, r - 1], recv_sem=recv_sems.at[k, r - 1],
                        device_id=(px, py, pc), device_id_type=MESH)
            sends.append(pltpu.make_async_remote_copy(src_ref=src(k, peer), dst_ref=outs[k].at[me], **sems))
            recvs.append(pltpu.make_async_remote_copy(src_ref=src(k, peer), dst_ref=outs[k].at[peer], **sems))

    def start():
        for cp in local + sends:
            cp.start()

    def wait():
        for cp in recvs:
            cp.wait_recv()
        for cp in sends:
            cp.wait_send()
        for cp in local:
            cp.wait()

    return start, wait


def _exchange_sems(n_arr):
    return [pltpu.SemaphoreType.DMA((n_arr, N_DEV - 1)), pltpu.SemaphoreType.DMA((n_arr, N_DEV - 1)),
            pltpu.SemaphoreType.DMA((n_arr,))]


def _exchange_shape(a, gather):
    return jax.ShapeDtypeStruct(((N_DEV,) + a.shape) if gather else a.shape, a.dtype)


def _gather_two_level(arrays, name):
    n_arr = len(arrays)

    def body(*refs):
        ins, outs = refs[:n_arr], refs[n_arr:2 * n_arr]
        send_sems, recv_sems, local_sems = refs[2 * n_arr:]
        x, y, c = _coords()

        def slot(k, px, py, pc):
            return outs[k].at[4 * px + 2 * py + pc]

        def routed(core):
            me, sibling = (x, y, core), (x, y, 1 - core)
            xn, yn, dg = (1 - x, y), (x, 1 - y), (1 - x, 1 - y)
            (first, s_first), (second, s_second) = ((xn, 1), (yn, 2)) if core == 0 else ((yn, 2), (xn, 1))

            def copy(k, j, block, to, own=False):
                return pltpu.make_async_remote_copy(
                    src_ref=ins[k] if own else slot(k, *block), dst_ref=slot(k, *block),
                    send_sem=send_sems.at[k, j], recv_sem=recv_sems.at[k, j],
                    device_id=to, device_id_type=MESH)

            local = [pltpu.make_async_copy(ins[k], slot(k, *me), local_sems.at[k]) for k in range(n_arr)]
            sent = []
            for k in range(n_arr):
                sent += [copy(k, 0, me, sibling, True), copy(k, 1, me, (*xn, core), True),
                         copy(k, 2, me, (*yn, core), True)]
            for cp in local + sent:
                cp.start()

            def pass_on(k, j_from, j_to, block, targets):
                copy(k, j_from, block, me).wait_recv()
                for j, to in zip(j_to, targets):
                    cp = copy(k, j, block, to)
                    cp.start()
                    sent.append(cp)

            for k in range(n_arr):
                pass_on(k, s_first, (3, 3 + s_first), (*first, core), ((*second, core), sibling))
            for k in range(n_arr):
                pass_on(k, s_second, (3 + s_second,), (*second, core), (sibling,))
            for k in range(n_arr):
                pass_on(k, 3, (6,), (*dg, core), (sibling,))
            for k in range(n_arr):
                copy(k, 0, sibling, me).wait_recv()
                for j, chip in ((4, xn), (5, yn), (6, dg)):
                    copy(k, j, (*chip, 1 - core), me).wait_recv()
            for cp in sent:
                cp.wait_send()
            for cp in local:
                cp.wait()

        for core in (0, 1):
            pl.when(c == core)(lambda core=core: routed(core))

    return pl.pallas_call(
        body, name=name,
        in_specs=[ANY] * n_arr, out_specs=[ANY] * n_arr,
        out_shape=[_exchange_shape(a, True) for a in arrays],
        scratch_shapes=_exchange_sems(n_arr),
    )(*arrays)


def _piece_columns():
    pos = {}
    for h in range(RET_HEADS):
        for k, p in enumerate((h, 4 + h, 8 + 2 * h, 9 + 2 * h, 16 + 2 * h, 17 + 2 * h)):
            pos[p] = 6 * h + k
    for p in range(FOX_PAIRS):
        for i in range(4):
            pos[24 + 8 * i + p] = 24 + 4 * p + i
    pos[D_IN_PAD // 128 - 1] = D_IN_PAD // 128 - 1
    return np.array([pos[7 * d + j] for d in range(N_DEV) for j in range(8)], np.int32)


def _dwin_pair_slabs(dz, u, core):
    def body(order_ref, cols_ref, *refs):
        pieces, u_ref, pair_ref, theirs_ref = refs[:8], refs[8], refs[9], refs[10]
        send_buf, got_buf, send_sems, recv_sems, load_sem = refs[11:]
        s = pl.program_id(0)
        x, y, c = _coords()
        cols = jnp.concatenate([p[...] for p in pieces], axis=1)
        slab = _dot_tn(cols, u_ref[...])[:W_SLAB]

        def push(q):
            return pltpu.make_async_remote_copy(
                src_ref=send_buf.at[q], dst_ref=theirs_ref.at[q],
                send_sem=send_sems.at[q], recv_sem=recv_sems.at[q],
                device_id=(x, y, 1 - c), device_id_type=MESH)

        for q in range(N_CHIP):
            @pl.when(s == q)
            def _(q=q):
                send_buf[q] = slab.astype(BF16)
                push(q).start()

            @pl.when(s == N_CHIP + q)
            def _(q=q):
                push(q).wait_recv()
                load = pltpu.make_async_copy(theirs_ref.at[q], got_buf, load_sem)
                load.start()
                load.wait()
                pair_ref[0] = (slab + got_buf[...].astype(F32)).astype(BF16)

        @pl.when(s == 2 * N_CHIP - 1)
        def _():
            for q in range(N_CHIP):
                push(q).wait_send()

    order = jnp.concatenate([2 * jnp.arange(N_CHIP) + (1 - core), 2 * jnp.arange(N_CHIP) + core]).astype(jnp.int32)
    piece = lambda j: pl.BlockSpec((T, 128), lambda s, order_ref, cols_ref: (0, cols_ref[order_ref[s] * 8 + j]))
    slabs = jax.ShapeDtypeStruct((N_CHIP, W_SLAB, D_MODEL), BF16)
    pair, _ = pl.pallas_call(
        body, name="dwin_pair_slabs",
        grid_spec=pltpu.PrefetchScalarGridSpec(
            num_scalar_prefetch=2, grid=(2 * N_CHIP,),
            in_specs=[piece(j) for j in range(8)]
            + [pl.BlockSpec((T, D_MODEL), lambda s, order_ref, cols_ref: (0, 0))],
            out_specs=[pl.BlockSpec((1, W_SLAB, D_MODEL),
                                    lambda s, order_ref, cols_ref: (jnp.maximum(s - N_CHIP, 0), 0, 0)), ANY],
            scratch_shapes=[pltpu.VMEM((N_CHIP, W_SLAB, D_MODEL), BF16), pltpu.VMEM((W_SLAB, D_MODEL), BF16),
                            pltpu.SemaphoreType.DMA((N_CHIP,)), pltpu.SemaphoreType.DMA((N_CHIP,)),
                            pltpu.SemaphoreType.DMA]),
        out_shape=[slabs, slabs],
        compiler_params=_params(dimension_semantics=("arbitrary",)),
    )(order, jnp.asarray(_piece_columns()), *([dz] * 8), u)
    return pair


def _du_rms(dz, slabs, tail, h_pad, dout, g, pair_blocks):
    tm = 272
    steps = T // tm
    merge_step = 3
    columns = _piece_columns().reshape(N_DEV, 8)

    def body(dz_ref, w_ref, t_ref, h_ref, d_ref, g_ref, p_ref, dh_ref, dg_ref, got_ref,
             via_buf, mine_buf, send_sems, recv_sems, local_sems):
        i = pl.program_id(0)
        x, y, c = _coords()

        def routed(core, step):
            first, second = ((1 - x, y), (x, 1 - y)) if core == 0 else ((x, 1 - y), (1 - x, y))
            chip = lambda px, py: 2 * px + py

            def send(j, src, dst, to):
                return pltpu.make_async_remote_copy(
                    src_ref=src, dst_ref=dst, send_sem=send_sems.at[j], recv_sem=recv_sems.at[j],
                    device_id=(*to, core), device_id_type=MESH)

            onward = send(1, p_ref.at[chip(1 - x, 1 - y)], via_buf, first)
            direct = send(0, p_ref.at[chip(*first)], got_ref.at[1], first)
            summed = send(2, mine_buf, got_ref.at[2], second)
            own = pltpu.make_async_copy(p_ref.at[chip(x, y)], got_ref.at[0], local_sems.at[0])
            load = pltpu.make_async_copy(p_ref.at[chip(*second)], mine_buf, local_sems.at[1])
            if step == "start":
                for cp in (onward, direct, own, load):
                    cp.start()
            elif step == "merge":
                onward.wait_recv()
                load.wait()
                mine_buf[...] = (mine_buf[...].astype(F32) + via_buf[...].astype(F32)).astype(BF16)
                summed.start()
            else:
                direct.wait_recv()
                summed.wait_recv()
                for cp in (onward, direct, summed):
                    cp.wait_send()
                own.wait()

        for core in (0, 1):
            pl.when(jnp.logical_and(c == core, i == 0))(lambda core=core: routed(core, "start"))

        @pl.when(i == 0)
        def _():
            dg_ref[...] = jnp.zeros_like(dg_ref)

        du = _dot(dz_ref[:, pl.ds(FF_BASE, 128)], t_ref[...])
        for d in range(N_DEV):
            cols = jnp.concatenate([dz_ref[:, pl.ds(128 * int(columns[d, j]), 128)] for j in range(7)], axis=1)
            du = du + _dot(cols, w_ref[d, pl.ds(0, W_STRIDE), :])
        h = h_ref[...]
        r = lax.rsqrt(jnp.mean(h * h, axis=-1, keepdims=True) + EPS)
        xn = h * r
        dg_ref[...] += jnp.sum(du * xn, axis=0, keepdims=True)
        dn = du * g_ref[...]
        dh_ref[...] = d_ref[...] + r * (dn - xn * jnp.mean(dn * xn, axis=-1, keepdims=True))

        for core in (0, 1):
            pl.when(jnp.logical_and(c == core, i == merge_step))(lambda core=core: routed(core, "merge"))
            pl.when(jnp.logical_and(c == core, i == steps - 1))(lambda core=core: routed(core, "finish"))

    tile = pl.BlockSpec((tm, D_MODEL), lambda i: (i, 0))
    slab = pair_blocks.shape[1:]
    return pl.pallas_call(
        body, name="du_rms", grid=(steps,),
        in_specs=[pl.BlockSpec((tm, D_IN_PAD), lambda i: (i, 0)),
                  pl.BlockSpec((N_DEV, W_SLAB, D_MODEL), lambda i: (0, 0, 0)),
                  pl.BlockSpec((128, D_MODEL), lambda i: (0, 0)),
                  tile, tile, pl.BlockSpec((1, D_MODEL), lambda i: (0, 0)), ANY],
        out_specs=[tile, pl.BlockSpec((1, D_MODEL), lambda i: (0, 0)), ANY],
        out_shape=[jax.ShapeDtypeStruct((T, D_MODEL), F32),
                   jax.ShapeDtypeStruct((1, D_MODEL), F32),
                   jax.ShapeDtypeStruct((3,) + slab, pair_blocks.dtype)],
        scratch_shapes=[pltpu.VMEM(slab, pair_blocks.dtype), pltpu.VMEM(slab, pair_blocks.dtype),
                        pltpu.SemaphoreType.DMA((3,)), pltpu.SemaphoreType.DMA((3,)),
                        pltpu.SemaphoreType.DMA((2,))],
        compiler_params=_params(dimension_semantics=("arbitrary",)),
    )(dz, slabs, tail, h_pad, dout, g, pair_blocks)


def _tri(lower):
    r = lax.broadcasted_iota(jnp.int32, (CHUNK, CHUNK), 0)
    c = lax.broadcasted_iota(jnp.int32, (CHUNK, CHUNK), 1)
    return jnp.where((r >= c) if lower else (r <= c), 1.0, 0.0).astype(F32)


def _row_valid(n):
    r = lax.broadcasted_iota(jnp.int32, (CHUNK, 128), 0) + n * CHUNK
    return r >= PAD


_FF_SPEC = pl.BlockSpec((T, 128), lambda i: (0, FF_BASE // 128))
_ZFF_SPEC = pl.BlockSpec((T, 128), lambda i: (0, 0))


def _forget_fwd(z, b_pad):
    def body(z_ref, b_ref, o_ref):
        tri = _tri(True)
        carry = jnp.zeros((1, 128), F32)
        for n in range(NCHUNK):
            rows = pl.ds(n * CHUNK, CHUNK)
            a = z_ref[rows, :] + b_ref[...]
            lf = -(jnp.maximum(-a, 0.0) + jnp.log(1.0 + jnp.exp(-jnp.abs(a))))
            lf = jnp.where(_row_valid(n), lf, 0.0)
            c = jnp.dot(tri, lf, precision=lax.Precision.HIGHEST,
                        preferred_element_type=F32) + carry
            carry = c[CHUNK - 1:CHUNK, :]
            o_ref[:, rows] = jnp.where(_row_valid(n), c * (-LOG2E), NEG_INF).T

    return pl.pallas_call(
        body, name="forget_fwd", grid=(1,),
        in_specs=[_ZFF_SPEC, pl.BlockSpec((1, 128), lambda i: (0, 0))],
        out_specs=pl.BlockSpec((128, T), lambda i: (0, 0)),
        out_shape=jax.ShapeDtypeStruct((128, T), F32),
        compiler_params=_params(dimension_semantics=("arbitrary",)),
    )(z, b_pad)


def _forget_bwd(z, b_pad, dc, dz):
    def body(z_ref, b_ref, dc_ref, dz_in, dff_ref, db_ref):
        tri = _tri(False)
        carry = jnp.zeros((1, 128), F32)
        db = jnp.zeros((1, 128), F32)
        for n in reversed(range(NCHUNK)):
            rows = pl.ds(n * CHUNK, CHUNK)
            dc_blk = jnp.concatenate([dc_ref[:, rows], jnp.zeros((128 - FOX_HEADS, CHUNK), F32)], axis=0).T
            dlf = jnp.dot(tri, dc_blk, precision=lax.Precision.HIGHEST,
                          preferred_element_type=F32) + carry
            carry = dlf[0:1, :]
            a = z_ref[rows, :] + b_ref[...]
            dff = jnp.where(_row_valid(n), dlf * jax.nn.sigmoid(-a), 0.0)
            dff_ref[rows, :] = dff.astype(BF16)
            db = db + jnp.sum(dff, axis=0, keepdims=True)
        db_ref[...] = db

    return pl.pallas_call(
        body, name="forget_bwd", grid=(1,),
        in_specs=[_ZFF_SPEC, pl.BlockSpec((1, 128), lambda i: (0, 0)),
                  pl.BlockSpec((FOX_HEADS, T), lambda i: (0, 0)), ANY],
        out_specs=[_FF_SPEC, pl.BlockSpec((1, 128), lambda i: (0, 0))],
        out_shape=[jax.ShapeDtypeStruct((T, D_IN_PAD), BF16),
                   jax.ShapeDtypeStruct((1, 128), F32)],
        input_output_aliases={3: 0},
        compiler_params=_params(dimension_semantics=("arbitrary",)),
    )(z, b_pad, dc, dz)


FOX_QB = 512
FOX_NQB = SEQ // FOX_QB


def _fox_block(b):
    lo = CHUNK + b * FOX_QB
    return pl.ds(lo, FOX_QB), lo, lo + FOX_QB


def _causal_bias():
    r = lax.broadcasted_iota(jnp.int32, (FOX_QB, FOX_QB), 0)
    c = lax.broadcasted_iota(jnp.int32, (FOX_QB, FOX_QB), 1)
    return jnp.where(c <= r, 0.0, NEG_INF).astype(F32)


def _fox_logits(q_blk, k_all, bias, causal, b):
    _, lo, hi = _fox_block(b)
    s_off = _dot_nt(q_blk, k_all[:lo]) + bias[:, :lo]
    s_dia = _dot_nt(q_blk, k_all[lo:hi]) + (bias[:, lo:hi] + causal)
    return s_off, s_dia


_FOX_Z_SPEC = pl.BlockSpec((T, FOX_W), lambda p: (0, FOX_BASE // FOX_W + p))
_FOX_BIAS_SPEC = pl.BlockSpec((2, 1, T), lambda p: (p, 0, 0))
_FOX_LSE_SPEC = pl.BlockSpec((2, T, 1), lambda p: (p, 0, 0))
_FOX_SCALE = FOX_D ** -0.5
_FOX_QSCALE = _FOX_SCALE * LOG2E


def _fox_fwd(z, bias, y, w_out_blk):
    last = FOX_PAIRS - 1

    def body(z_ref, b_ref, y_in, w_ref, a_ref, lse_ref, y_ref, wall_ref,
             send_sems, recv_sems, local_sems):
        start, wait = _direct_exchange([w_ref], [wall_ref], send_sems, recv_sems, local_sems, True)
        pl.when(pl.program_id(0) == 0)(start)

        causal = _causal_bias()
        a_ref[pl.ds(0, CHUNK), :] = jnp.zeros((CHUNK, 128), F32)
        y_ref[pl.ds(0, CHUNK), :] = jnp.zeros((CHUNK, 128), BF16)
        for j in range(2):
            lanes = pl.ds(j * FOX_D, FOX_D)
            k_all = z_ref[:, pl.ds(128 + j * FOX_D, FOX_D)]
            v_all = z_ref[:, pl.ds(256 + j * FOX_D, FOX_D)]
            bias = b_ref[j]
            lse_ref[j, pl.ds(0, CHUNK), :] = jnp.zeros((CHUNK, 1), F32)
            for b in range(FOX_NQB):
                rows, lo, hi = _fox_block(b)
                q_blk = (z_ref[rows, lanes].astype(F32) * _FOX_QSCALE).astype(BF16)
                s_off, s_dia = _fox_logits(q_blk, k_all, bias, causal, b)
                m = jnp.maximum(jnp.max(s_off, axis=-1, keepdims=True),
                                jnp.max(s_dia, axis=-1, keepdims=True))
                e_off = jnp.exp2(s_off - m)
                e_dia = jnp.exp2(s_dia - m)
                total = jnp.sum(e_off, axis=-1, keepdims=True) + jnp.sum(e_dia, axis=-1, keepdims=True)
                o = (_dot(e_off.astype(BF16), v_all[:lo]) + _dot(e_dia.astype(BF16), v_all[lo:hi])) / total
                a_ref[rows, lanes] = o
                lse_ref[j, rows, :] = m + jnp.log(total) * LOG2E
                gate = _silu_parts(z_ref[rows, pl.ds(384 + j * FOX_D, FOX_D)].astype(F32))[0]
                y_ref[rows, lanes] = (o * gate).astype(BF16)

        pl.when(pl.program_id(0) == last)(wait)

    return pl.pallas_call(
        body, name="fox_fwd", grid=(FOX_PAIRS,),
        in_specs=[_FOX_Z_SPEC, _FOX_BIAS_SPEC, ANY, ANY],
        out_specs=[pl.BlockSpec((T, 128), lambda p: (0, p)), _FOX_LSE_SPEC,
                   pl.BlockSpec((T, 128), lambda p: (0, 8 + p)), ANY],
        out_shape=[jax.ShapeDtypeStruct((T, FOX_HEADS * FOX_D), F32),
                   jax.ShapeDtypeStruct((FOX_HEADS, T, 1), F32),
                   jax.ShapeDtypeStruct((T, D_MIX), BF16),
                   _exchange_shape(w_out_blk, True)],
        input_output_aliases={2: 2},
        scratch_shapes=_exchange_sems(1),
        compiler_params=_params(dimension_semantics=("arbitrary",)),
    )(z, bias, y, w_out_blk)


def _fox_bwd(z, bias, a_f, lse, dy, dz, dwo_blocks):
    last = FOX_PAIRS - 1

    def body(z_ref, b_ref, a_ref, lse_ref, dy_ref, dz_in, dwo_ref, dz_ref, dc_ref, got_ref,
             kv_acc, dc_acc, send_sems, recv_sems, local_sems):
        start, wait = _direct_exchange([dwo_ref], [got_ref], send_sems, recv_sems, local_sems, False)
        pl.when(pl.program_id(0) == 0)(start)

        causal = _causal_bias()
        dz_ref[pl.ds(0, CHUNK), pl.ds(0, 128)] = jnp.zeros((CHUNK, 128), BF16)
        dz_ref[pl.ds(0, CHUNK), pl.ds(384, 128)] = jnp.zeros((CHUNK, 128), BF16)
        dk_rows, dv_rows = pl.ds(0, FOX_D), pl.ds(FOX_D, FOX_D)
        for j in range(2):
            lanes = pl.ds(j * FOX_D, FOX_D)
            k_all = z_ref[:, pl.ds(128 + j * FOX_D, FOX_D)]
            v_all = z_ref[:, pl.ds(256 + j * FOX_D, FOX_D)]
            bias = b_ref[j]
            kv_acc[...] = jnp.zeros_like(kv_acc)
            dc_acc[...] = jnp.zeros_like(dc_acc)
            for b in range(FOX_NQB):
                rows, lo, hi = _fox_block(b)
                off, dia = pl.ds(0, lo), pl.ds(lo, FOX_QB)
                q_blk = (z_ref[rows, lanes].astype(F32) * _FOX_QSCALE).astype(BF16)
                s_off, s_dia = _fox_logits(q_blk, k_all, bias, causal, b)
                lse_blk = lse_ref[j, rows, :]
                p_off, p_dia = jnp.exp2(s_off - lse_blk), jnp.exp2(s_dia - lse_blk)
                sg, dsg = _silu_parts(z_ref[rows, pl.ds(384 + j * FOX_D, FOX_D)].astype(F32))
                dyj = dy_ref[rows, lanes].astype(F32)
                dz_ref[rows, pl.ds(384 + j * FOX_D, FOX_D)] = (dyj * a_ref[rows, lanes] * dsg).astype(BF16)
                do_b = (dyj * sg).astype(BF16)
                dp_off = _dot_nt(do_b, v_all[:lo])
                dp_dia = _dot_nt(do_b, v_all[lo:hi])
                d = (jnp.sum(p_off * dp_off, axis=-1, keepdims=True)
                     + jnp.sum(p_dia * dp_dia, axis=-1, keepdims=True))
                ds_off = p_off * (dp_off - d)
                ds_dia = p_dia * (dp_dia - d)
                dc_acc[:, off] -= jnp.sum(ds_off, axis=0, keepdims=True)
                dc_acc[:, dia] -= jnp.sum(ds_dia, axis=0, keepdims=True)
                ds_off_b, ds_dia_b = ds_off.astype(BF16), ds_dia.astype(BF16)
                dq = _dot(ds_off_b, k_all[:lo]) + _dot(ds_dia_b, k_all[lo:hi])
                dz_ref[rows, lanes] = (dq * _FOX_SCALE).astype(BF16)
                kv_acc[dk_rows, off] += _dot_tn(q_blk, ds_off_b)
                kv_acc[dk_rows, dia] += _dot_tn(q_blk, ds_dia_b)
                kv_acc[dv_rows, off] += _dot_tn(do_b, p_off.astype(BF16))
                kv_acc[dv_rows, dia] += _dot_tn(do_b, p_dia.astype(BF16))
            for n in range(NCHUNK):
                rows = pl.ds(n * CHUNK, CHUNK)
                both = kv_acc[:, rows].T
                dz_ref[rows, pl.ds(128 + j * FOX_D, FOX_D)] = (both[:, :FOX_D] * LN2).astype(BF16)
                dz_ref[rows, pl.ds(256 + j * FOX_D, FOX_D)] = both[:, FOX_D:].astype(BF16)
            dc_ref[j] = dc_acc[...]

        pl.when(pl.program_id(0) == last)(wait)

    col = lambda base: pl.BlockSpec((T, 128), lambda p: (0, base + p))
    return pl.pallas_call(
        body, name="fox_bwd", grid=(FOX_PAIRS,),
        in_specs=[_FOX_Z_SPEC, _FOX_BIAS_SPEC, col(0), _FOX_LSE_SPEC, col(8), ANY, ANY],
        out_specs=[_FOX_Z_SPEC, _FOX_BIAS_SPEC, ANY],
        out_shape=[jax.ShapeDtypeStruct((T, D_IN_PAD), BF16),
                   jax.ShapeDtypeStruct((FOX_HEADS, 1, T), F32),
                   _exchange_shape(dwo_blocks, False)],
        input_output_aliases={5: 0},
        scratch_shapes=[pltpu.VMEM((2 * FOX_D, T), F32), pltpu.VMEM((1, T), F32)] + _exchange_sems(1),
        compiler_params=_params(dimension_semantics=("arbitrary",)),
    )(z, bias, a_f, lse, dy, dz, dwo_blocks)


def _rot(x, cosf, sins):
    return x * cosf + pltpu.roll(x, RET_DK // 2, 1) * sins


def _rot_t(d, cosf, sins):
    return d * cosf - pltpu.roll(d, RET_DK // 2, 1) * sins


_RET_Z_SPEC = pl.BlockSpec((T, RET_W), lambda h: (0, h))
_RET_TABLE_SPECS = [
    pl.BlockSpec((T, RET_DK), lambda h: (0, 0)),
    pl.BlockSpec((T, RET_DK), lambda h: (0, 0)),
    pl.BlockSpec((1, CHUNK, CHUNK), lambda h: (h, 0, 0)),
    pl.BlockSpec((1, CHUNK, 1), lambda h: (h, 0, 0)),
    pl.BlockSpec((1, CHUNK, 1), lambda h: (h, 0, 0)),
    pl.BlockSpec((1, 1, 1), lambda h: (h, 0, 0)),
]
_RQ, _RK = pl.ds(0, RET_DK), pl.ds(RET_DK, RET_DK)
_RV, _RG = pl.ds(2 * RET_DK, RET_DV), pl.ds(2 * RET_DK + RET_DV, RET_DV)
_RET_KSCALE = RET_DK ** -0.5


def _ret_fwd(z, tables):
    def body(z_ref, cos_ref, sin_ref, dm_ref, zeta_ref, xi_ref, cd_ref, raw_ref, y_ref):
        dmask, zeta, xi, cdec = dm_ref[0], zeta_ref[0], xi_ref[0], cd_ref[0]
        state = jnp.zeros((RET_DK, RET_DV), F32)
        for n in range(NCHUNK):
            rows = pl.ds(n * CHUNK, CHUNK)
            cosf, sins = cos_ref[rows, :], sin_ref[rows, :]
            qr = _rot(z_ref[rows, _RQ].astype(F32), cosf, sins)
            kr_b = (_rot(z_ref[rows, _RK].astype(F32), cosf, sins) * _RET_KSCALE).astype(BF16)
            v_b = z_ref[rows, _RV]
            a = _dot_nt(qr.astype(BF16), kr_b) * dmask
            out = _dot(a.astype(BF16), v_b) + _dot((qr * xi).astype(BF16), state.astype(BF16))
            state = state * cdec + _dot_tn(kr_b, (v_b.astype(F32) * zeta).astype(BF16))
            raw_ref[rows, :] = out
            r = lax.rsqrt(jnp.mean(out * out, axis=-1, keepdims=True) + EPS)
            y_ref[rows, :] = (out * r * _silu_parts(z_ref[rows, _RG].astype(F32))[0]).astype(BF16)

    wide = pl.BlockSpec((T, RET_DV), lambda h: (0, h))
    return pl.pallas_call(
        body, name="ret_fwd", grid=(RET_HEADS,),
        in_specs=[_RET_Z_SPEC] + _RET_TABLE_SPECS,
        out_specs=[wide, wide],
        out_shape=[jax.ShapeDtypeStruct((T, RET_HEADS * RET_DV), F32),
                   jax.ShapeDtypeStruct((T, D_MIX), BF16)],
        compiler_params=_params(dimension_semantics=("arbitrary",)),
    )(z, *tables)


def _ret_bwd(z, tables, raw, dy):
    def body(z_ref, cos_ref, sin_ref, dm_ref, zeta_ref, xi_ref, cd_ref, raw_ref, dy_ref,
             dz_ref, st_ref):
        dmask, zeta, xi, cdec = dm_ref[0], zeta_ref[0], xi_ref[0], cd_ref[0]

        def rotated(n):
            rows = pl.ds(n * CHUNK, CHUNK)
            cosf, sins = cos_ref[rows, :], sin_ref[rows, :]
            qr = _rot(z_ref[rows, _RQ].astype(F32), cosf, sins)
            kr_b = (_rot(z_ref[rows, _RK].astype(F32), cosf, sins) * _RET_KSCALE).astype(BF16)
            return rows, cosf, sins, qr, kr_b

        state = jnp.zeros((RET_DK, RET_DV), F32)
        for n in range(NCHUNK):
            st_ref[n] = state.astype(BF16)
            if n + 1 < NCHUNK:
                rows, _, _, _, kr_b = rotated(n)
                state = state * cdec + _dot_tn(kr_b, (z_ref[rows, _RV].astype(F32) * zeta).astype(BF16))

        grad_state = jnp.zeros((RET_DK, RET_DV), F32)
        for n in reversed(range(NCHUNK)):
            rows, cosf, sins, qr, kr_b = rotated(n)
            qr_b = qr.astype(BF16)
            v_b = z_ref[rows, _RV]
            gs_b = grad_state.astype(BF16)
            o = raw_ref[rows, :]
            r = lax.rsqrt(jnp.mean(o * o, axis=-1, keepdims=True) + EPS)
            hn = o * r
            sg, dsg = _silu_parts(z_ref[rows, _RG].astype(F32))
            dyn = dy_ref[rows, :].astype(F32)
            dz_ref[rows, _RG] = (dyn * hn * dsg).astype(BF16)
            dhn = dyn * sg
            do_b = (r * (dhn - hn * jnp.mean(dhn * hn, axis=-1, keepdims=True))).astype(BF16)
            a_b = (_dot_nt(qr_b, kr_b) * dmask).astype(BF16)
            da_b = (_dot_nt(do_b, v_b) * dmask).astype(BF16)
            dqr = _dot(da_b, kr_b) + xi * _dot_nt(do_b, st_ref[n])
            dkr = _dot_tn(da_b, qr_b) + zeta * _dot_nt(v_b, gs_b)
            dv = _dot_tn(a_b, do_b) + zeta * _dot(kr_b, gs_b)
            grad_state = grad_state * cdec + _dot_tn((qr * xi).astype(BF16), do_b)
            dz_ref[rows, _RQ] = _rot_t(dqr, cosf, sins).astype(BF16)
            dz_ref[rows, _RK] = (_rot_t(dkr, cosf, sins) * _RET_KSCALE).astype(BF16)
            dz_ref[rows, _RV] = dv.astype(BF16)

    wide = pl.BlockSpec((T, RET_DV), lambda h: (0, h))
    return pl.pallas_call(
        body, name="ret_bwd", grid=(RET_HEADS,),
        in_specs=[_RET_Z_SPEC] + _RET_TABLE_SPECS + [wide, wide],
        out_specs=_RET_Z_SPEC,
        out_shape=jax.ShapeDtypeStruct((T, D_IN_PAD), BF16),
        scratch_shapes=[pltpu.VMEM((NCHUNK, RET_DK, RET_DV), BF16)],
        compiler_params=_params(dimension_semantics=("arbitrary",)),
    )(z, *tables, raw, dy)


def _adamw(w, g, m, v):
    m = ADAM_B1 * m + (1.0 - ADAM_B1) * g
    v = ADAM_B2 * v + (1.0 - ADAM_B2) * (g * g)
    m_hat = m / (1.0 - ADAM_B1 ** ADAM_STEP)
    v_hat = v / (1.0 - ADAM_B2 ** ADAM_STEP)
    delta = -ADAM_LR * (m_hat / (jnp.sqrt(v_hat) + ADAM_EPS) + ADAM_WD * w)
    return delta, m, v


def _sum_adamw(parts, w, m, v, rows, name):
    _, r_tot, cols = parts.shape
    assert r_tot % rows == 0

    def body(p_ref, w_ref, m_ref, v_ref, g_ref, d_ref, nm_ref, nv_ref):
        g = p_ref[0].astype(F32)
        for d in range(1, N_DEV):
            g = g + p_ref[d].astype(F32)
        delta, nm, nv = _adamw(w_ref[...], g, m_ref[...], v_ref[...])
        g_ref[...] = g
        d_ref[...] = delta
        nm_ref[...] = nm
        nv_ref[...] = nv

    blk = pl.BlockSpec((rows, cols), lambda i: (i, 0))
    return pl.pallas_call(
        body, name=name, grid=(r_tot // rows,),
        in_specs=[pl.BlockSpec((N_DEV, rows, cols), lambda i: (0, i, 0)), blk, blk, blk],
        out_specs=[blk] * 4,
        out_shape=[jax.ShapeDtypeStruct((r_tot, cols), F32)] * 4,
        compiler_params=_params(dimension_semantics=("arbitrary",)),
    )(parts, w, m, v)


def _sum_adamw_w_in(parts, w, m, v, small):
    n_part, r, c = parts.shape
    steps = c // 128

    def body(p_ref, w_hbm, m_hbm, v_hbm, s_ref, g_hbm, d_hbm, nm_hbm, nv_hbm, got_ref,
             in_buf, out_buf, in_sems, out_sems, send_sems, recv_sems, local_sems):
        start, wait = _direct_exchange([s_ref], [got_ref], send_sems, recv_sems, local_sems, True)
        i = pl.program_id(0)
        pl.when(i == 0)(start)
        cols = pl.ds(pl.multiple_of(i * 128, 128), 128)
        loads = [pltpu.make_async_copy(h.at[:, 0, cols], in_buf.at[k], in_sems.at[k])
                 for k, h in enumerate((w_hbm, m_hbm, v_hbm))]
        stores = [pltpu.make_async_copy(out_buf.at[k], h.at[:, 0, cols], out_sems.at[k])
                  for k, h in enumerate((g_hbm, d_hbm, nm_hbm, nv_hbm))]
        for cp in loads:
            cp.start()
        g = p_ref[0].astype(F32)
        for d in range(1, n_part):
            g = g + p_ref[d].astype(F32)
        for cp in loads:
            cp.wait()
        delta, nm, nv = _adamw(in_buf[0], g, in_buf[1], in_buf[2])

        @pl.when(i > 0)
        def _():
            for cp in stores:
                cp.wait()

        for k, val in enumerate((g, delta, nm, nv)):
            out_buf[k] = val
        for cp in stores:
            cp.start()

        @pl.when(i == steps - 1)
        def _():
            for cp in stores:
                cp.wait()
            wait()

    return pl.pallas_call(
        body, name="adamw_w_in", grid=(steps,),
        in_specs=[pl.BlockSpec((n_part, r, 128), lambda i: (0, 0, i)), ANY, ANY, ANY, ANY],
        out_specs=[ANY] * 5,
        out_shape=[jax.ShapeDtypeStruct((r, 1, c), F32)] * 4 + [_exchange_shape(small, True)],
        scratch_shapes=[pltpu.VMEM((3, r, 128), F32), pltpu.VMEM((4, r, 128), F32),
                        pltpu.SemaphoreType.DMA((3,)), pltpu.SemaphoreType.DMA((4,))] + _exchange_sems(1),
        compiler_params=_params(dimension_semantics=("arbitrary",)),
    )(parts, w, m, v, small)


def _adamw_small(got, me, metas, norms, finals, biases):
    def body(me_ref, gm_ref, gr_ref, *refs):
        ins, outs = refs[:12], refs[12:]
        g_meta, g_rest = gm_ref[0], gr_ref[0]
        for d in range(1, N_DEV):
            g_meta, g_rest = g_meta + gm_ref[d], g_rest + gr_ref[d]
        grads = [g_meta, g_rest[0:1], g_rest[1:2], g_rest[2:3, :FOX_HEADS]]
        for k, g in enumerate(grads):
            w_ref, m_ref, v_ref = ins[3 * k:3 * k + 3]
            delta, new_m, new_v = _adamw(w_ref[...], g, m_ref[...], v_ref[...])
            for o_ref, val in zip(outs[4 * k:4 * k + 4], (g, delta, new_m, new_v)):
                o_ref[...] = val
        outs[16][...] = g_rest[3:4, :128]

    groups = (metas, norms, finals, biases)
    full = lambda a: pl.BlockSpec(a.shape, lambda i, me_ref: (0,) * a.ndim)
    flat = [a for grp in groups for a in grp]
    res = pl.pallas_call(
        body, name="adamw_small",
        grid_spec=pltpu.PrefetchScalarGridSpec(
            num_scalar_prefetch=1, grid=(1,),
            in_specs=[pl.BlockSpec((N_DEV, N_META, META_BLK), lambda i, me_ref: (0, 0, me_ref[0])),
                      pl.BlockSpec((N_DEV, 8, D_MODEL), lambda i, me_ref: (0, N_META // 8, 0))]
            + [full(a) for a in flat],
            out_specs=[full(grp[0]) for grp in groups for _ in range(4)]
            + [pl.BlockSpec((1, 128), lambda i, me_ref: (0, 0))]),
        out_shape=[jax.ShapeDtypeStruct(grp[0].shape, F32) for grp in groups for _ in range(4)]
        + [jax.ShapeDtypeStruct((1, 128), F32)],
        compiler_params=_params(dimension_semantics=("arbitrary",)),
    )(me, got, got, *flat)
    return [res[4 * k:4 * k + 4] for k in range(4)], res[16]


def kernel(x, meta_tokens, norm_g, w_in, b_f, w_out, final_g, loss_target, m_meta_tokens, m_norm_g, m_w_in, m_b_f, m_w_out, m_final_g, v_meta_tokens, v_norm_g, v_w_in, v_b_f, v_w_out, v_final_g):
    core = lax.axis_index("c")
    me = 4 * lax.axis_index("x") + 2 * lax.axis_index("y") + core
    tables = _tables()

    wt_all, meta_all = _gather_two_level([_slab(w_in[0].T.astype(BF16), me), meta_tokens], name="gather_w_in")
    slabs, tail = _join_edges(wt_all)
    meta_full = jnp.transpose(meta_all, (1, 0, 2)).reshape(N_META, D_MODEL)
    h_pad = jnp.concatenate([jnp.zeros((PAD, D_MODEL), F32), meta_full, x[0]], axis=0)
    b_pad = jnp.pad(b_f, ((0, 0), (0, 128 - FOX_HEADS)))

    u, z = _rms_z_ret(h_pad, norm_g, slabs)
    z, zff = _z_fox(u, slabs, tail, z)
    bias = _forget_fwd(zff, b_pad)[:FOX_HEADS].reshape(FOX_HEADS, 1, T)
    raw, y = _ret_fwd(z, tables)
    a_f, lse, y, w_out_all = _fox_fwd(z, bias, y, w_out[0].astype(BF16))
    w_out_b = w_out_all.reshape(D_MIX, D_MODEL)
    dout, dout_b, dy, loss_blk, d_final_g = _out_loss_dy(y, w_out_b, h_pad, loss_target[0],
                                                         final_g.reshape(1, D_MODEL))

    d_w_out = _mm_tn(y, dout_b, tm=D_MIX, tn=256, name="mm_dwout")
    dz = _ret_bwd(z, tables, raw, dy)
    dz, dc, got_w_out = _fox_bwd(z, bias, a_f, lse, dy, dz, d_w_out.reshape(N_DEV, WO_BLK, D_MODEL))
    dz, db_f = _forget_bwd(zff, b_pad, dc.reshape(FOX_HEADS, T), dz)

    pair = _dwin_pair_slabs(dz, u, core)
    dh, d_norm_g, got_slabs = _du_rms(dz, slabs, tail, h_pad, dout, norm_g, pair)
    got_w_in = _unslab(got_slabs, me)

    small = jnp.concatenate([
        dh[PAD:CHUNK], d_norm_g, d_final_g, jnp.pad(db_f[:, :FOX_HEADS], ((0, 0), (0, D_MODEL - FOX_HEADS))),
        jnp.pad(loss_blk[0:1], ((0, 0), (0, D_MODEL - 128))),
        jnp.zeros((SMALL_ROWS - N_META - 4, D_MODEL), F32)], axis=0)
    fore = lambda a: jnp.transpose(a, (2, 0, 1))
    g_w_in, d_w_in, nm_w_in, nv_w_in, got_small = _sum_adamw_w_in(
        got_w_in, fore(w_in), fore(m_w_in), fore(v_w_in), small)
    g_w_out, d_w_out, nm_w_out, nv_w_out = _sum_adamw(got_w_out, w_out[0], m_w_out[0], v_w_out[0], 128, "adamw_w_out")

    row = lambda a: a.reshape(1, D_MODEL)
    (meta_o, norm_o, final_o, bias_o), loss_row = _adamw_small(
        got_small, me.astype(jnp.int32).reshape(1),
        (meta_tokens, m_meta_tokens, v_meta_tokens), (norm_g, m_norm_g, v_norm_g),
        (row(final_g), row(m_final_g), row(v_final_g)), (b_f, m_b_f, v_b_f))
    final_o = [a.reshape(D_MODEL) for a in final_o]

    back = lambda a: jnp.transpose(a, (1, 2, 0))
    outs = [[meta_o[k], norm_o[k], back(wk), bias_o[k], ok[None], final_o[k]]
            for k, (wk, ok) in enumerate(zip((g_w_in, d_w_in, nm_w_in, nv_w_in),
                                             (g_w_out, d_w_out, nm_w_out, nv_w_out)))]
    return (loss_row[0, 0], dh[CHUNK:][None], *outs[0], *outs[1], *outs[2], *outs[3])
```

```python
import numpy as np
import jax
import jax.numpy as jnp
from jax import lax
from jax.experimental import pallas as pl
from jax.experimental.pallas import tpu as pltpu

F32 = jnp.float32
BF16 = jnp.bfloat16

N_DEV = 8
N_CHIP = 4
D_MODEL = 1024
SEQ = 2048
N_META = 16
CHUNK = 128
PAD = CHUNK - N_META
T = SEQ + CHUNK
NCHUNK = T // CHUNK
D_MIX = 2048
RET_HEADS = 4
RET_DK = 128
RET_DV = 256
RET_W = 2 * RET_DK + 2 * RET_DV
FOX_HEADS = 16
FOX_D = 64
FOX_PAIRS = FOX_HEADS // 2
FOX_W = 4 * 128
FOX_BASE = RET_HEADS * RET_W
FF_BASE = FOX_BASE + FOX_PAIRS * FOX_W
D_IN = 7184
D_IN_PAD = 7296
W_BLK = D_IN // N_DEV
WO_BLK = D_MIX // N_DEV
META_BLK = D_MODEL // N_DEV
EPS = 1e-6
NEG_INF = -1e30
ROPE_BASE = 10000.0
LOG2E = 1.4426950408889634
LN2 = 0.6931471805599453

ADAM_LR = 0.001
ADAM_B1 = 0.9
ADAM_B2 = 0.999
ADAM_EPS = 1e-08
ADAM_WD = 0.01
ADAM_STEP = 10

SMALL_ROWS = 24
VMEM_LIMIT = 56 * 1024 * 1024
MESH = pl.DeviceIdType.MESH
ANY = pl.BlockSpec(memory_space=pl.ANY)

_NT = (((1,), (1,)), ((), ()))
_TN = (((0,), (0,)), ((), ()))


def _dot(a, b):
    return jnp.dot(a, b, preferred_element_type=F32)


def _dot_nt(a, b):
    return lax.dot_general(a, b, _NT, preferred_element_type=F32)


def _dot_tn(a, b):
    return lax.dot_general(a, b, _TN, preferred_element_type=F32)


def _params(**kw):
    return pltpu.CompilerParams(vmem_limit_bytes=VMEM_LIMIT, **kw)


def _silu_parts(g):
    sig = jax.nn.sigmoid(g)
    return g * sig, sig * (1.0 + g * (1.0 - sig))


def _tables():
    pos = np.arange(T, dtype=np.float32) - PAD
    inv = (ROPE_BASE ** (-np.arange(0, RET_DK, 2, dtype=np.float32) / RET_DK)).astype(np.float32)
    ang = pos[:, None] * inv[None, :]
    cos, sin = np.cos(ang), np.sin(ang)
    cosf = np.concatenate([cos, cos], axis=1).astype(np.float32)
    sins = np.concatenate([-sin, sin], axis=1).astype(np.float32)
    h = np.arange(RET_HEADS, dtype=np.float32)
    log_gamma = np.log1p(-np.exp2(-5.0 - h)).astype(np.float32)
    idx = np.arange(CHUNK, dtype=np.float32)
    diff = idx[:, None] - idx[None, :]
    dmask = np.where(diff[None] >= 0,
                     np.exp(log_gamma[:, None, None] * np.maximum(diff, 0.0)[None]), 0.0)
    zeta = np.exp(log_gamma[:, None] * (CHUNK - 1.0 - idx)[None, :])
    xi = np.exp(log_gamma[:, None] * (idx + 1.0)[None, :])
    cdec = np.exp(log_gamma * CHUNK)
    return (jnp.asarray(cosf), jnp.asarray(sins), jnp.asarray(dmask, F32),
            jnp.asarray(zeta[:, :, None], F32), jnp.asarray(xi[:, :, None], F32),
            jnp.asarray(cdec[:, None, None], F32))


W_STRIDE = 896
W_SLAB = 912
W_EDGE = W_SLAB - W_STRIDE
def _slab(block, me):
    shift = W_BLK - W_STRIDE
    return lax.switch(me, [lambda b, d=d: jnp.pad(b, ((shift * d, W_SLAB - W_BLK - shift * d), (0, 0)))
                           for d in range(N_DEV)], block)


def _unslab(slabs, me):
    shift = W_BLK - W_STRIDE
    return lax.switch(me, [lambda s, d=d: s[:, shift * d:shift * d + W_BLK] for d in range(N_DEV)], slabs)


def _join_edges(slabs):
    last = slabs[:, W_STRIDE:]
    first = slabs[:, :W_EDGE] + jnp.concatenate([jnp.zeros_like(last[:1]), last[:-1]], axis=0)
    tail = jnp.pad(last[N_DEV - 1], ((0, 128 - W_EDGE), (0, 0)))
    return lax.dynamic_update_slice(slabs, first, (0, 0, 0)), tail


def _mm_tn(a, b, *, tm, tn, name):
    k, m = a.shape
    n = b.shape[1]
    assert m % tm == 0 and n % tn == 0

    def body(a_ref, b_ref, o_ref):
        o_ref[...] = _dot_tn(a_ref[...], b_ref[...]).astype(BF16)

    return pl.pallas_call(
        body, name=name, grid=(n // tn, m // tm),
        in_specs=[pl.BlockSpec((k, tm), lambda j, i: (0, i)),
                  pl.BlockSpec((k, tn), lambda j, i: (0, j))],
        out_specs=pl.BlockSpec((tm, tn), lambda j, i: (i, j)),
        out_shape=jax.ShapeDtypeStruct((m, n), BF16),
        compiler_params=_params(dimension_semantics=("arbitrary", "arbitrary")),
    )(a, b)


def _piece_spec(base, mult):
    def index(i):
        p = base + mult * i
        return p // 7, p % 7, 0
    return pl.BlockSpec((1, 128, D_MODEL), index)


_RET_PIECES = ((0, 1), (4, 1), (8, 2), (9, 2), (16, 2), (17, 2))
_FOX_PIECES = ((24, 1), (32, 1), (40, 1), (48, 1))


def _rms_z_ret(h_pad, g, slabs):
    def body(h_ref, g_ref, *refs):
        pieces, u_ref, z_ref = refs[:6], refs[6], refs[7]

        @pl.when(pl.program_id(0) == 0)
        def _():
            h = h_ref[...]
            r = lax.rsqrt(jnp.mean(h * h, axis=-1, keepdims=True) + EPS)
            u_ref[...] = (h * r * g_ref[...]).astype(BF16)

        w = jnp.concatenate([p[0] for p in pieces], axis=0)
        z_ref[...] = _dot_nt(u_ref[...], w).astype(BF16)

    whole = pl.BlockSpec((T, D_MODEL), lambda i: (0, 0))
    return pl.pallas_call(
        body, name="rms_z_ret", grid=(RET_HEADS,),
        in_specs=[whole, pl.BlockSpec((1, D_MODEL), lambda i: (0, 0))]
        + [_piece_spec(*bm) for bm in _RET_PIECES],
        out_specs=[whole, pl.BlockSpec((T, RET_W), lambda i: (0, i))],
        out_shape=[jax.ShapeDtypeStruct((T, D_MODEL), BF16),
                   jax.ShapeDtypeStruct((T, D_IN_PAD), BF16)],
        compiler_params=_params(dimension_semantics=("arbitrary",)),
    )(h_pad, g, *([slabs] * 6))


def _z_fox(u, slabs, tail, z):
    def body(u_ref, *refs):
        pieces, t_ref, z_ref, zff_ref = refs[:4], refs[4], refs[6], refs[7]
        u = u_ref[...]
        w = jnp.concatenate([p[0] for p in pieces], axis=0)
        z_ref[...] = _dot_nt(u, w).astype(BF16)

        @pl.when(pl.program_id(0) == 0)
        def _():
            zff_ref[...] = _dot_nt(u, t_ref[...])

    return pl.pallas_call(
        body, name="z_fox", grid=(FOX_PAIRS,),
        in_specs=[pl.BlockSpec((T, D_MODEL), lambda i: (0, 0))] + [_piece_spec(*bm) for bm in _FOX_PIECES]
        + [pl.BlockSpec((128, D_MODEL), lambda i: (0, 0)), ANY],
        out_specs=[_FOX_Z_SPEC, pl.BlockSpec((T, 128), lambda i: (0, 0))],
        out_shape=[jax.ShapeDtypeStruct((T, D_IN_PAD), BF16),
                   jax.ShapeDtypeStruct((T, 128), F32)],
        input_output_aliases={6: 0},
        compiler_params=_params(dimension_semantics=("arbitrary",)),
    )(u, *([slabs] * 4), tail, z)


def _out_loss_dy(y, w_out_b, h_pad, target, g):
    tm = T // 4

    def body(y_ref, w_ref, h_ref, t_hbm, g_ref, d_ref, db_ref, dy_ref, loss_ref, dg_ref, t_buf, t_sem):
        i = pl.program_id(0)
        head = pltpu.make_async_copy(t_hbm.at[pl.ds(0, tm - CHUNK)], t_buf.at[pl.ds(CHUNK, tm - CHUNK)], t_sem)
        rest = pltpu.make_async_copy(t_hbm.at[pl.ds(pl.multiple_of(jnp.maximum(i, 1) * tm - CHUNK, 8), tm)],
                                     t_buf, t_sem)

        @pl.when(i == 0)
        def _():
            t_buf[pl.ds(0, CHUNK), :] = jnp.zeros((CHUNK, D_MODEL), F32)
            head.start()
            loss_ref[...] = jnp.zeros_like(loss_ref)
            dg_ref[...] = jnp.zeros_like(dg_ref)

        pl.when(i > 0)(rest.start)

        w = w_ref[...]
        o = _dot(y_ref[...], w) + h_ref[...]
        pl.when(i == 0)(head.wait)
        pl.when(i > 0)(rest.wait)
        token = lax.broadcasted_iota(jnp.int32, (tm, 1), 0) + i * tm >= CHUNK
        g = g_ref[...]
        r = lax.rsqrt(jnp.mean(o * o, axis=-1, keepdims=True) + EPS)
        xn = o * r
        e = jnp.where(token, xn * g - t_buf[...], 0.0)
        loss_ref[...] += jnp.full(loss_ref.shape, 0.5 / D_MODEL * jnp.sum(e * e), F32)
        do = e * (1.0 / D_MODEL)
        dg_ref[...] += jnp.sum(do * xn, axis=0, keepdims=True)
        dn = do * g
        d = r * (dn - xn * jnp.mean(dn * xn, axis=-1, keepdims=True))
        d_b = d.astype(BF16)
        d_ref[...] = d
        db_ref[...] = d_b
        dy_ref[...] = _dot_nt(d_b, w).astype(BF16)

    tile = pl.BlockSpec((tm, D_MODEL), lambda i: (i, 0))
    wide = pl.BlockSpec((tm, D_MIX), lambda i: (i, 0))
    return pl.pallas_call(
        body, name="out_loss_dy", grid=(T // tm,),
        in_specs=[wide, pl.BlockSpec((D_MIX, D_MODEL), lambda i: (0, 0)), tile, ANY,
                  pl.BlockSpec((1, D_MODEL), lambda i: (0, 0))],
        out_specs=[tile, tile, wide,
                   pl.BlockSpec((8, 128), lambda i: (0, 0)),
                   pl.BlockSpec((1, D_MODEL), lambda i: (0, 0))],
        out_shape=[jax.ShapeDtypeStruct((T, D_MODEL), F32),
                   jax.ShapeDtypeStruct((T, D_MODEL), BF16),
                   jax.ShapeDtypeStruct((T, D_MIX), BF16),
                   jax.ShapeDtypeStruct((8, 128), F32),
                   jax.ShapeDtypeStruct((1, D_MODEL), F32)],
        scratch_shapes=[pltpu.VMEM((tm, D_MODEL), F32), pltpu.SemaphoreType.DMA],
        compiler_params=_params(dimension_semantics=("arbitrary",)),
    )(y, w_out_b, h_pad, target, g)


def _coords():
    return lax.axis_index("x"), lax.axis_index("y"), lax.axis_index("c")


def _flip(v, bit):
    return 1 - v if bit else v


def _peer(x, y, c, r):
    return _flip(x, (r >> 2) & 1), _flip(y, (r >> 1) & 1), _flip(c, r & 1)


def _direct_exchange(ins, outs, send_sems, recv_sems, local_sems, gather):
    x, y, c = _coords()
    me = 4 * x + 2 * y + c

    def src(k, to_idx):
        return ins[k] if gather else ins[k].at[to_idx]

    local = [pltpu.make_async_copy(src(k, me), outs[k].at[me], local_sems.at[k])
             for k in range(len(ins))]
    sends, recvs = [], []
    for r in range(1, N_DEV):
        px, py, pc = _peer(x, y, c, r)
        peer = 4 * px + 2 * py + pc
        for k in range(len(ins)):
            sems = dict(send_sem=send_sems.at[k, r - 1], recv_sem=recv_sems.at[k, r - 1],
                        device_id=(px, py, pc), device_id_type=MESH)
            sends.append(pltpu.make_async_remote_copy(src_ref=src(k, peer), dst_ref=outs[k].at[me], **sems))
            recvs.append(pltpu.make_async_remote_copy(src_ref=src(k, peer), dst_ref=outs[k].at[peer], **sems))

    def start():
        for cp in local + sends:
            cp.start()

    def wait():
        for cp in recvs:
            cp.wait_recv()
        for cp in sends:
            cp.wait_send()
        for cp in local:
            cp.wait()

    return start, wait


def _exchange_sems(n_arr):
    return [pltpu.SemaphoreType.DMA((n_arr, N_DEV - 1)), pltpu.SemaphoreType.DMA((n_arr, N_DEV - 1)),
            pltpu.SemaphoreType.DMA((n_arr,))]


def _exchange_shape(a, gather):
    return jax.ShapeDtypeStruct(((N_DEV,) + a.shape) if gather else a.shape, a.dtype)


def _gather_two_level(arrays, name):
    n_arr = len(arrays)

    def body(*refs):
        ins, outs = refs[:n_arr], refs[n_arr:2 * n_arr]
        send_sems, recv_sems, local_sems = refs[2 * n_arr:]
        x, y, c = _coords()

        def slot(k, px, py, pc):
            return outs[k].at[4 * px + 2 * py + pc]

        def routed(core):
            me, sibling = (x, y, core), (x, y, 1 - core)
            xn, yn, dg = (1 - x, y), (x, 1 - y), (1 - x, 1 - y)
            (first, s_first), (second, s_second) = ((xn, 1), (yn, 2)) if core == 0 else ((yn, 2), (xn, 1))

            def copy(k, j, block, to, own=False):
                return pltpu.make_async_remote_copy(
                    src_ref=ins[k] if own else slot(k, *block), dst_ref=slot(k, *block),
                    send_sem=send_sems.at[k, j], recv_sem=recv_sems.at[k, j],
                    device_id=to, device_id_type=MESH)

            local = [pltpu.make_async_copy(ins[k], slot(k, *me), local_sems.at[k]) for k in range(n_arr)]
            sent = []
            for k in range(n_arr):
                sent += [copy(k, 0, me, sibling, True), copy(k, 1, me, (*xn, core), True),
                         copy(k, 2, me, (*yn, core), True)]
            for cp in local + sent:
                cp.start()

            def pass_on(k, j_from, j_to, block, targets):
                copy(k, j_from, block, me).wait_recv()
                for j, to in zip(j_to, targets):
                    cp = copy(k, j, block, to)
                    cp.start()
                    sent.append(cp)

            for k in range(n_arr):
                pass_on(k, s_first, (3, 3 + s_first), (*first, core), ((*second, core), sibling))
            for k in range(n_arr):
                pass_on(k, s_second, (3 + s_second,), (*second, core), (sibling,))
            for k in range(n_arr):
                pass_on(k, 3, (6,), (*dg, core), (sibling,))
            for k in range(n_arr):
                copy(k, 0, sibling, me).wait_recv()
                for j, chip in ((4, xn), (5, yn), (6, dg)):
                    copy(k, j, (*chip, 1 - core), me).wait_recv()
            for cp in sent:
                cp.wait_send()
            for cp in local:
                cp.wait()

        for core in (0, 1):
            pl.when(c == core)(lambda core=core: routed(core))

    return pl.pallas_call(
        body, name=name,
        in_specs=[ANY] * n_arr, out_specs=[ANY] * n_arr,
        out_shape=[_exchange_shape(a, True) for a in arrays],
        scratch_shapes=_exchange_sems(n_arr),
    )(*arrays)


def _piece_columns():
    pos = {}
    for h in range(RET_HEADS):
        for k, p in enumerate((h, 4 + h, 8 + 2 * h, 9 + 2 * h, 16 + 2 * h, 17 + 2 * h)):
            pos[p] = 6 * h + k
    for p in range(FOX_PAIRS):
        for i in range(4):
            pos[24 + 8 * i + p] = 24 + 4 * p + i
    pos[D_IN_PAD // 128 - 1] = D_IN_PAD // 128 - 1
    return np.array([pos[7 * d + j] for d in range(N_DEV) for j in range(8)], np.int32)


def _dwin_pair_slabs(dz, u, core):
    def body(order_ref, cols_ref, *refs):
        pieces, u_ref, pair_ref, theirs_ref = refs[:8], refs[8], refs[9], refs[10]
        send_buf, got_buf, send_sems, recv_sems, load_sem = refs[11:]
        s = pl.program_id(0)
        x, y, c = _coords()
        cols = jnp.concatenate([p[...] for p in pieces], axis=1)
        slab = _dot_tn(cols, u_ref[...])[:W_SLAB]

        def push(q):
            return pltpu.make_async_remote_copy(
                src_ref=send_buf.at[q], dst_ref=theirs_ref.at[q],
                send_sem=send_sems.at[q], recv_sem=recv_sems.at[q],
                device_id=(x, y, 1 - c), device_id_type=MESH)

        for q in range(N_CHIP):
            @pl.when(s == q)
            def _(q=q):
                send_buf[q] = slab.astype(BF16)
                push(q).start()

            @pl.when(s == N_CHIP + q)
            def _(q=q):
                push(q).wait_recv()
                load = pltpu.make_async_copy(theirs_ref.at[q], got_buf, load_sem)
                load.start()
                load.wait()
                pair_ref[0] = (slab + got_buf[...].astype(F32)).astype(BF16)

        @pl.when(s == 2 * N_CHIP - 1)
        def _():
            for q in range(N_CHIP):
                push(q).wait_send()

    order = jnp.concatenate([2 * jnp.arange(N_CHIP) + (1 - core), 2 * jnp.arange(N_CHIP) + core]).astype(jnp.int32)
    piece = lambda j: pl.BlockSpec((T, 128), lambda s, order_ref, cols_ref: (0, cols_ref[order_ref[s] * 8 + j]))
    slabs = jax.ShapeDtypeStruct((N_CHIP, W_SLAB, D_MODEL), BF16)
    pair, _ = pl.pallas_call(
        body, name="dwin_pair_slabs",
        grid_spec=pltpu.PrefetchScalarGridSpec(
            num_scalar_prefetch=2, grid=(2 * N_CHIP,),
            in_specs=[piece(j) for j in range(8)]
            + [pl.BlockSpec((T, D_MODEL), lambda s, order_ref, cols_ref: (0, 0))],
            out_specs=[pl.BlockSpec((1, W_SLAB, D_MODEL),
                                    lambda s, order_ref, cols_ref: (jnp.maximum(s - N_CHIP, 0), 0, 0)), ANY],
            scratch_shapes=[pltpu.VMEM((N_CHIP, W_SLAB, D_MODEL), BF16), pltpu.VMEM((W_SLAB, D_MODEL), BF16),
                            pltpu.SemaphoreType.DMA((N_CHIP,)), pltpu.SemaphoreType.DMA((N_CHIP,)),
                            pltpu.SemaphoreType.DMA]),
        out_shape=[slabs, slabs],
        compiler_params=_params(dimension_semantics=("arbitrary",)),
    )(order, jnp.asarray(_piece_columns()), *([dz] * 8), u)
    return pair


def _du_rms(dz, slabs, tail, h_pad, dout, g, pair_blocks):
    tm = 272
    steps = T // tm
    merge_step = 3
    columns = _piece_columns().reshape(N_DEV, 8)

    def body(dz_ref, w_ref, t_ref, h_ref, d_ref, g_ref, p_ref, dh_ref, dg_ref, got_ref,
             via_buf, mine_buf, send_sems, recv_sems, local_sems):
        i = pl.program_id(0)
        x, y, c = _coords()

        def routed(core, step):
            first, second = ((1 - x, y), (x, 1 - y)) if core == 0 else ((x, 1 - y), (1 - x, y))
            chip = lambda px, py: 2 * px + py

            def send(j, src, dst, to):
                return pltpu.make_async_remote_copy(
                    src_ref=src, dst_ref=dst, send_sem=send_sems.at[j], recv_sem=recv_sems.at[j],
                    device_id=(*to, core), device_id_type=MESH)

            onward = send(1, p_ref.at[chip(1 - x, 1 - y)], via_buf, first)
            direct = send(0, p_ref.at[chip(*first)], got_ref.at[1], first)
            summed = send(2, mine_buf, got_ref.at[2], second)
            own = pltpu.make_async_copy(p_ref.at[chip(x, y)], got_ref.at[0], local_sems.at[0])
            load = pltpu.make_async_copy(p_ref.at[chip(*second)], mine_buf, local_sems.at[1])
            if step == "start":
                for cp in (onward, direct, own, load):
                    cp.start()
            elif step == "merge":
                onward.wait_recv()
                load.wait()
                mine_buf[...] = (mine_buf[...].astype(F32) + via_buf[...].astype(F32)).astype(BF16)
                summed.start()
            else:
                direct.wait_recv()
                summed.wait_recv()
                for cp in (onward, direct, summed):
                    cp.wait_send()
                own.wait()

        for core in (0, 1):
            pl.when(jnp.logical_and(c == core, i == 0))(lambda core=core: routed(core, "start"))

        @pl.when(i == 0)
        def _():
            dg_ref[...] = jnp.zeros_like(dg_ref)

        du = _dot(dz_ref[:, pl.ds(FF_BASE, 128)], t_ref[...])
        for d in range(N_DEV):
            cols = jnp.concatenate([dz_ref[:, pl.ds(128 * int(columns[d, j]), 128)] for j in range(7)], axis=1)
            du = du + _dot(cols, w_ref[d, pl.ds(0, W_STRIDE), :])
        h = h_ref[...]
        r = lax.rsqrt(jnp.mean(h * h, axis=-1, keepdims=True) + EPS)
        xn = h * r
        dg_ref[...] += jnp.sum(du * xn, axis=0, keepdims=True)
        dn = du * g_ref[...]
        dh_ref[...] = d_ref[...] + r * (dn - xn * jnp.mean(dn * xn, axis=-1, keepdims=True))

        for core in (0, 1):
            pl.when(jnp.logical_and(c == core, i == merge_step))(lambda core=core: routed(core, "merge"))
            pl.when(jnp.logical_and(c == core, i == steps - 1))(lambda core=core: routed(core, "finish"))

    tile = pl.BlockSpec((tm, D_MODEL), lambda i: (i, 0))
    slab = pair_blocks.shape[1:]
    return pl.pallas_call(
        body, name="du_rms", grid=(steps,),
        in_specs=[pl.BlockSpec((tm, D_IN_PAD), lambda i: (i, 0)),
                  pl.BlockSpec((N_DEV, W_SLAB, D_MODEL), lambda i: (0, 0, 0)),
                  pl.BlockSpec((128, D_MODEL), lambda i: (0, 0)),
                  tile, tile, pl.BlockSpec((1, D_MODEL), lambda i: (0, 0)), ANY],
        out_specs=[tile, pl.BlockSpec((1, D_MODEL), lambda i: (0, 0)), ANY],
        out_shape=[jax.ShapeDtypeStruct((T, D_MODEL), F32),
                   jax.ShapeDtypeStruct((1, D_MODEL), F32),
                   jax.ShapeDtypeStruct((3,) + slab, pair_blocks.dtype)],
        scratch_shapes=[pltpu.VMEM(slab, pair_blocks.dtype), pltpu.VMEM(slab, pair_blocks.dtype),
                        pltpu.SemaphoreType.DMA((3,)), pltpu.SemaphoreType.DMA((3,)),
                        pltpu.SemaphoreType.DMA((2,))],
        compiler_params=_params(dimension_semantics=("arbitrary",)),
    )(dz, slabs, tail, h_pad, dout, g, pair_blocks)


def _tri(lower):
    r = lax.broadcasted_iota(jnp.int32, (CHUNK, CHUNK), 0)
    c = lax.broadcasted_iota(jnp.int32, (CHUNK, CHUNK), 1)
    return jnp.where((r >= c) if lower else (r <= c), 1.0, 0.0).astype(F32)


def _row_valid(n):
    r = lax.broadcasted_iota(jnp.int32, (CHUNK, 128), 0) + n * CHUNK
    return r >= PAD


_FF_SPEC = pl.BlockSpec((T, 128), lambda i: (0, FF_BASE // 128))
_ZFF_SPEC = pl.BlockSpec((T, 128), lambda i: (0, 0))


def _forget_fwd(z, b_pad):
    def body(z_ref, b_ref, o_ref):
        tri = _tri(True)
        carry = jnp.zeros((1, 128), F32)
        for n in range(NCHUNK):
            rows = pl.ds(n * CHUNK, CHUNK)
            a = z_ref[rows, :] + b_ref[...]
            lf = -(jnp.maximum(-a, 0.0) + jnp.log(1.0 + jnp.exp(-jnp.abs(a))))
            lf = jnp.where(_row_valid(n), lf, 0.0)
            c = jnp.dot(tri, lf, precision=lax.Precision.HIGHEST,
                        preferred_element_type=F32) + carry
            carry = c[CHUNK - 1:CHUNK, :]
            o_ref[:, rows] = jnp.where(_row_valid(n), c * (-LOG2E), NEG_INF).T

    return pl.pallas_call(
        body, name="forget_fwd", grid=(1,),
        in_specs=[_ZFF_SPEC, pl.BlockSpec((1, 128), lambda i: (0, 0))],
        out_specs=pl.BlockSpec((128, T), lambda i: (0, 0)),
        out_shape=jax.ShapeDtypeStruct((128, T), F32),
        compiler_params=_params(dimension_semantics=("arbitrary",)),
    )(z, b_pad)


def _forget_bwd(z, b_pad, dc, dz):
    def body(z_ref, b_ref, dc_ref, dz_in, dff_ref, db_ref):
        tri = _tri(False)
        carry = jnp.zeros((1, 128), F32)
        db = jnp.zeros((1, 128), F32)
        for n in reversed(range(NCHUNK)):
            rows = pl.ds(n * CHUNK, CHUNK)
            dc_blk = jnp.concatenate([dc_ref[:, rows], jnp.zeros((128 - FOX_HEADS, CHUNK), F32)], axis=0).T
            dlf = jnp.dot(tri, dc_blk, precision=lax.Precision.HIGHEST,
                          preferred_element_type=F32) + carry
            carry = dlf[0:1, :]
            a = z_ref[rows, :] + b_ref[...]
            dff = jnp.where(_row_valid(n), dlf * jax.nn.sigmoid(-a), 0.0)
            dff_ref[rows, :] = dff.astype(BF16)
            db = db + jnp.sum(dff, axis=0, keepdims=True)
        db_ref[...] = db

    return pl.pallas_call(
        body, name="forget_bwd", grid=(1,),
        in_specs=[_ZFF_SPEC, pl.BlockSpec((1, 128), lambda i: (0, 0)),
                  pl.BlockSpec((FOX_HEADS, T), lambda i: (0, 0)), ANY],
        out_specs=[_FF_SPEC, pl.BlockSpec((1, 128), lambda i: (0, 0))],
        out_shape=[jax.ShapeDtypeStruct((T, D_IN_PAD), BF16),
                   jax.ShapeDtypeStruct((1, 128), F32)],
        input_output_aliases={3: 0},
        compiler_params=_params(dimension_semantics=("arbitrary",)),
    )(z, b_pad, dc, dz)


FOX_QB = 512
FOX_NQB = SEQ // FOX_QB


def _fox_block(b):
    lo = CHUNK + b * FOX_QB
    return pl.ds(lo, FOX_QB), lo, lo + FOX_QB


def _causal_bias():
    r = lax.broadcasted_iota(jnp.int32, (FOX_QB, FOX_QB), 0)
    c = lax.broadcasted_iota(jnp.int32, (FOX_QB, FOX_QB), 1)
    return jnp.where(c <= r, 0.0, NEG_INF).astype(F32)


def _fox_logits(q_blk, k_all, bias, causal, b):
    _, lo, hi = _fox_block(b)
    s_off = _dot_nt(q_blk, k_all[:lo]) + bias[:, :lo]
    s_dia = _dot_nt(q_blk, k_all[lo:hi]) + (bias[:, lo:hi] + causal)
    return s_off, s_dia


_FOX_Z_SPEC = pl.BlockSpec((T, FOX_W), lambda p: (0, FOX_BASE // FOX_W + p))
_FOX_BIAS_SPEC = pl.BlockSpec((2, 1, T), lambda p: (p, 0, 0))
_FOX_LSE_SPEC = pl.BlockSpec((2, T, 1), lambda p: (p, 0, 0))
_FOX_SCALE = FOX_D ** -0.5
_FOX_QSCALE = _FOX_SCALE * LOG2E


def _fox_fwd(z, bias, y, w_out_blk):
    last = FOX_PAIRS - 1

    def body(z_ref, b_ref, y_in, w_ref, a_ref, lse_ref, y_ref, wall_ref,
             send_sems, recv_sems, local_sems):
        start, wait = _direct_exchange([w_ref], [wall_ref], send_sems, recv_sems, local_sems, True)
        pl.when(pl.program_id(0) == 0)(start)

        causal = _causal_bias()
        a_ref[pl.ds(0, CHUNK), :] = jnp.zeros((CHUNK, 128), F32)
        y_ref[pl.ds(0, CHUNK), :] = jnp.zeros((CHUNK, 128), BF16)
        for j in range(2):
            lanes = pl.ds(j * FOX_D, FOX_D)
            k_all = z_ref[:, pl.ds(128 + j * FOX_D, FOX_D)]
            v_all = z_ref[:, pl.ds(256 + j * FOX_D, FOX_D)]
            bias = b_ref[j]
            lse_ref[j, pl.ds(0, CHUNK), :] = jnp.zeros((CHUNK, 1), F32)
            for b in range(FOX_NQB):
                rows, lo, hi = _fox_block(b)
                q_blk = (z_ref[rows, lanes].astype(F32) * _FOX_QSCALE).astype(BF16)
                s_off, s_dia = _fox_logits(q_blk, k_all, bias, causal, b)
                m = jnp.maximum(jnp.max(s_off, axis=-1, keepdims=True),
                                jnp.max(s_dia, axis=-1, keepdims=True))
                e_off = jnp.exp2(s_off - m)
                e_dia = jnp.exp2(s_dia - m)
                total = jnp.sum(e_off, axis=-1, keepdims=True) + jnp.sum(e_dia, axis=-1, keepdims=True)
                o = (_dot(e_off.astype(BF16), v_all[:lo]) + _dot(e_dia.astype(BF16), v_all[lo:hi])) / total
                a_ref[rows, lanes] = o
                lse_ref[j, rows, :] = m + jnp.log(total) * LOG2E
                gate = _silu_parts(z_ref[rows, pl.ds(384 + j * FOX_D, FOX_D)].astype(F32))[0]
                y_ref[rows, lanes] = (o * gate).astype(BF16)

        pl.when(pl.program_id(0) == last)(wait)

    return pl.pallas_call(
        body, name="fox_fwd", grid=(FOX_PAIRS,),
        in_specs=[_FOX_Z_SPEC, _FOX_BIAS_SPEC, ANY, ANY],
        out_specs=[pl.BlockSpec((T, 128), lambda p: (0, p)), _FOX_LSE_SPEC,
                   pl.BlockSpec((T, 128), lambda p: (0, 8 + p)), ANY],
        out_shape=[jax.ShapeDtypeStruct((T, FOX_HEADS * FOX_D), F32),
                   jax.ShapeDtypeStruct((FOX_HEADS, T, 1), F32),
                   jax.ShapeDtypeStruct((T, D_MIX), BF16),
                   _exchange_shape(w_out_blk, True)],
        input_output_aliases={2: 2},
        scratch_shapes=_exchange_sems(1),
        compiler_params=_params(dimension_semantics=("arbitrary",)),
    )(z, bias, y, w_out_blk)


def _fox_bwd(z, bias, a_f, lse, dy, dz, dwo_blocks):
    last = FOX_PAIRS - 1

    def body(z_ref, b_ref, a_ref, lse_ref, dy_ref, dz_in, dwo_ref, dz_ref, dc_ref, got_ref,
             kv_acc, dc_acc, send_sems, recv_sems, local_sems):
        start, wait = _direct_exchange([dwo_ref], [got_ref], send_sems, recv_sems, local_sems, False)
        pl.when(pl.program_id(0) == 0)(start)

        causal = _causal_bias()
        dz_ref[pl.ds(0, CHUNK), pl.ds(0, 128)] = jnp.zeros((CHUNK, 128), BF16)
        dz_ref[pl.ds(0, CHUNK), pl.ds(384, 128)] = jnp.zeros((CHUNK, 128), BF16)
        dk_rows, dv_rows = pl.ds(0, FOX_D), pl.ds(FOX_D, FOX_D)
        for j in range(2):
            lanes = pl.ds(j * FOX_D, FOX_D)
            k_all = z_ref[:, pl.ds(128 + j * FOX_D, FOX_D)]
            v_all = z_ref[:, pl.ds(256 + j * FOX_D, FOX_D)]
            bias = b_ref[j]
            kv_acc[...] = jnp.zeros_like(kv_acc)
            dc_acc[...] = jnp.zeros_like(dc_acc)
            for b in range(FOX_NQB):
                rows, lo, hi = _fox_block(b)
                off, dia = pl.ds(0, lo), pl.ds(lo, FOX_QB)
                q_blk = (z_ref[rows, lanes].astype(F32) * _FOX_QSCALE).astype(BF16)
                s_off, s_dia = _fox_logits(q_blk, k_all, bias, causal, b)
                lse_blk = lse_ref[j, rows, :]
                p_off, p_dia = jnp.exp2(s_off - lse_blk), jnp.exp2(s_dia - lse_blk)
                sg, dsg = _silu_parts(z_ref[rows, pl.ds(384 + j * FOX_D, FOX_D)].astype(F32))
                dyj = dy_ref[rows, lanes].astype(F32)
                dz_ref[rows, pl.ds(384 + j * FOX_D, FOX_D)] = (dyj * a_ref[rows, lanes] * dsg).astype(BF16)
                do_b = (dyj * sg).astype(BF16)
                dp_off = _dot_nt(do_b, v_all[:lo])
                dp_dia = _dot_nt(do_b, v_all[lo:hi])
                d = (jnp.sum(p_off * dp_off, axis=-1, keepdims=True)
                     + jnp.sum(p_dia * dp_dia, axis=-1, keepdims=True))
                ds_off = p_off * (dp_off - d)
                ds_dia = p_dia * (dp_dia - d)
                dc_acc[:, off] -= jnp.sum(ds_off, axis=0, keepdims=True)
                dc_acc[:, dia] -= jnp.sum(ds_dia, axis=0, keepdims=True)
                ds_off_b, ds_dia_b = ds_off.astype(BF16), ds_dia.astype(BF16)
                dq = _dot(ds_off_b, k_all[:lo]) + _dot(ds_dia_b, k_all[lo:hi])
                dz_ref[rows, lanes] = (dq * _FOX_SCALE).astype(BF16)
                kv_acc[dk_rows, off] += _dot_tn(q_blk, ds_off_b)
                kv_acc[dk_rows, dia] += _dot_tn(q_blk, ds_dia_b)
                kv_acc[dv_rows, off] += _dot_tn(do_b, p_off.astype(BF16))
                kv_acc[dv_rows, dia] += _dot_tn(do_b, p_dia.astype(BF16))
            for n in range(NCHUNK):
                rows = pl.ds(n * CHUNK, CHUNK)
                both = kv_acc[:, rows].T
                dz_ref[rows, pl.ds(128 + j * FOX_D, FOX_D)] = (both[:, :FOX_D] * LN2).astype(BF16)
                dz_ref[rows, pl.ds(256 + j * FOX_D, FOX_D)] = both[:, FOX_D:].astype(BF16)
            dc_ref[j] = dc_acc[...]

        pl.when(pl.program_id(0) == last)(wait)

    col = lambda base: pl.BlockSpec((T, 128), lambda p: (0, base + p))
    return pl.pallas_call(
        body, name="fox_bwd", grid=(FOX_PAIRS,),
        in_specs=[_FOX_Z_SPEC, _FOX_BIAS_SPEC, col(0), _FOX_LSE_SPEC, col(8), ANY, ANY],
        out_specs=[_FOX_Z_SPEC, _FOX_BIAS_SPEC, ANY],
        out_shape=[jax.ShapeDtypeStruct((T, D_IN_PAD), BF16),
                   jax.ShapeDtypeStruct((FOX_HEADS, 1, T), F32),
                   _exchange_shape(dwo_blocks, False)],
        input_output_aliases={5: 0},
        scratch_shapes=[pltpu.VMEM((2 * FOX_D, T), F32), pltpu.VMEM((1, T), F32)] + _exchange_sems(1),
        compiler_params=_params(dimension_semantics=("arbitrary",)),
    )(z, bias, a_f, lse, dy, dz, dwo_blocks)


def _rot(x, cosf, sins):
    return x * cosf + pltpu.roll(x, RET_DK // 2, 1) * sins


def _rot_t(d, cosf, sins):
    return d * cosf - pltpu.roll(d, RET_DK // 2, 1) * sins


_RET_Z_SPEC = pl.BlockSpec((T, RET_W), lambda h: (0, h))
_RET_TABLE_SPECS = [
    pl.BlockSpec((T, RET_DK), lambda h: (0, 0)),
    pl.BlockSpec((T, RET_DK), lambda h: (0, 0)),
    pl.BlockSpec((1, CHUNK, CHUNK), lambda h: (h, 0, 0)),
    pl.BlockSpec((1, CHUNK, 1), lambda h: (h, 0, 0)),
    pl.BlockSpec((1, CHUNK, 1), lambda h: (h, 0, 0)),
    pl.BlockSpec((1, 1, 1), lambda h: (h, 0, 0)),
]
_RQ, _RK = pl.ds(0, RET_DK), pl.ds(RET_DK, RET_DK)
_RV, _RG = pl.ds(2 * RET_DK, RET_DV), pl.ds(2 * RET_DK + RET_DV, RET_DV)
_RET_KSCALE = RET_DK ** -0.5


def _ret_fwd(z, tables):
    def body(z_ref, cos_ref, sin_ref, dm_ref, zeta_ref, xi_ref, cd_ref, raw_ref, y_ref):
        dmask, zeta, xi, cdec = dm_ref[0], zeta_ref[0], xi_ref[0], cd_ref[0]
        state = jnp.zeros((RET_DK, RET_DV), F32)
        for n in range(NCHUNK):
            rows = pl.ds(n * CHUNK, CHUNK)
            cosf, sins = cos_ref[rows, :], sin_ref[rows, :]
            qr = _rot(z_ref[rows, _RQ].astype(F32), cosf, sins)
            kr_b = (_rot(z_ref[rows, _RK].astype(F32), cosf, sins) * _RET_KSCALE).astype(BF16)
            v_b = z_ref[rows, _RV]
            a = _dot_nt(qr.astype(BF16), kr_b) * dmask
            out = _dot(a.astype(BF16), v_b) + _dot((qr * xi).astype(BF16), state.astype(BF16))
            state = state * cdec + _dot_tn(kr_b, (v_b.astype(F32) * zeta).astype(BF16))
            raw_ref[rows, :] = out
            r = lax.rsqrt(jnp.mean(out * out, axis=-1, keepdims=True) + EPS)
            y_ref[rows, :] = (out * r * _silu_parts(z_ref[rows, _RG].astype(F32))[0]).astype(BF16)

    wide = pl.BlockSpec((T, RET_DV), lambda h: (0, h))
    return pl.pallas_call(
        body, name="ret_fwd", grid=(RET_HEADS,),
        in_specs=[_RET_Z_SPEC] + _RET_TABLE_SPECS,
        out_specs=[wide, wide],
        out_shape=[jax.ShapeDtypeStruct((T, RET_HEADS * RET_DV), F32),
                   jax.ShapeDtypeStruct((T, D_MIX), BF16)],
        compiler_params=_params(dimension_semantics=("arbitrary",)),
    )(z, *tables)


def _ret_bwd(z, tables, raw, dy):
    def body(z_ref, cos_ref, sin_ref, dm_ref, zeta_ref, xi_ref, cd_ref, raw_ref, dy_ref,
             dz_ref, st_ref):
        dmask, zeta, xi, cdec = dm_ref[0], zeta_ref[0], xi_ref[0], cd_ref[0]

        def rotated(n):
            rows = pl.ds(n * CHUNK, CHUNK)
            cosf, sins = cos_ref[rows, :], sin_ref[rows, :]
            qr = _rot(z_ref[rows, _RQ].astype(F32), cosf, sins)
            kr_b = (_rot(z_ref[rows, _RK].astype(F32), cosf, sins) * _RET_KSCALE).astype(BF16)
            return rows, cosf, sins, qr, kr_b

        state = jnp.zeros((RET_DK, RET_DV), F32)
        for n in range(NCHUNK):
            st_ref[n] = state.astype(BF16)
            if n + 1 < NCHUNK:
                rows, _, _, _, kr_b = rotated(n)
                state = state * cdec + _dot_tn(kr_b, (z_ref[rows, _RV].astype(F32) * zeta).astype(BF16))

        grad_state = jnp.zeros((RET_DK, RET_DV), F32)
        for n in reversed(range(NCHUNK)):
            rows, cosf, sins, qr, kr_b = rotated(n)
            qr_b = qr.astype(BF16)
            v_b = z_ref[rows, _RV]
            gs_b = grad_state.astype(BF16)
            o = raw_ref[rows, :]
            r = lax.rsqrt(jnp.mean(o * o, axis=-1, keepdims=True) + EPS)
            hn = o * r
            sg, dsg = _silu_parts(z_ref[rows, _RG].astype(F32))
            dyn = dy_ref[rows, :].astype(F32)
            dz_ref[rows, _RG] = (dyn * hn * dsg).astype(BF16)
            dhn = dyn * sg
            do_b = (r * (dhn - hn * jnp.mean(dhn * hn, axis=-1, keepdims=True))).astype(BF16)
            a_b = (_dot_nt(qr_b, kr_b) * dmask).astype(BF16)
            da_b = (_dot_nt(do_b, v_b) * dmask).astype(BF16)
            dqr = _dot(da_b, kr_b) + xi * _dot_nt(do_b, st_ref[n])
            dkr = _dot_tn(da_b, qr_b) + zeta * _dot_nt(v_b, gs_b)
            dv = _dot_tn(a_b, do_b) + zeta * _dot(kr_b, gs_b)
            grad_state = grad_state * cdec + _dot_tn((qr * xi).astype(BF16), do_b)
            dz_ref[rows, _RQ] = _rot_t(dqr, cosf, sins).astype(BF16)
            dz_ref[rows, _RK] = (_rot_t(dkr, cosf, sins) * _RET_KSCALE).astype(BF16)
            dz_ref[rows, _RV] = dv.astype(BF16)

    wide = pl.BlockSpec((T, RET_DV), lambda h: (0, h))
    return pl.pallas_call(
        body, name="ret_bwd", grid=(RET_HEADS,),
        in_specs=[_RET_Z_SPEC] + _RET_TABLE_SPECS + [wide, wide],
        out_specs=_RET_Z_SPEC,
        out_shape=jax.ShapeDtypeStruct((T, D_IN_PAD), BF16),
        scratch_shapes=[pltpu.VMEM((NCHUNK, RET_DK, RET_DV), BF16)],
        compiler_params=_params(dimension_semantics=("arbitrary",)),
    )(z, *tables, raw, dy)


def _adamw(w, g, m, v):
    m = ADAM_B1 * m + (1.0 - ADAM_B1) * g
    v = ADAM_B2 * v + (1.0 - ADAM_B2) * (g * g)
    m_hat = m / (1.0 - ADAM_B1 ** ADAM_STEP)
    v_hat = v / (1.0 - ADAM_B2 ** ADAM_STEP)
    delta = -ADAM_LR * (m_hat / (jnp.sqrt(v_hat) + ADAM_EPS) + ADAM_WD * w)
    return delta, m, v


def _sum_adamw(parts, w, m, v, rows, name):
    _, r_tot, cols = parts.shape
    assert r_tot % rows == 0

    def body(p_ref, w_ref, m_ref, v_ref, g_ref, d_ref, nm_ref, nv_ref):
        g = p_ref[0].astype(F32)
        for d in range(1, N_DEV):
            g = g + p_ref[d].astype(F32)
        delta, nm, nv = _adamw(w_ref[...], g, m_ref[...], v_ref[...])
        g_ref[...] = g
        d_ref[...] = delta
        nm_ref[...] = nm
        nv_ref[...] = nv

    blk = pl.BlockSpec((rows, cols), lambda i: (i, 0))
    return pl.pallas_call(
        body, name=name, grid=(r_tot // rows,),
        in_specs=[pl.BlockSpec((N_DEV, rows, cols), lambda i: (0, i, 0)), blk, blk, blk],
        out_specs=[blk] * 4,
        out_shape=[jax.ShapeDtypeStruct((r_tot, cols), F32)] * 4,
        compiler_params=_params(dimension_semantics=("arbitrary",)),
    )(parts, w, m, v)


def _sum_adamw_w_in(parts, w, m, v, small):
    n_part, r, c = parts.shape
    steps = c // 128

    def body(p_ref, w_hbm, m_hbm, v_hbm, s_ref, g_hbm, d_hbm, nm_hbm, nv_hbm, got_ref,
             in_buf, out_buf, in_sems, out_sems, send_sems, recv_sems, local_sems):
        start, wait = _direct_exchange([s_ref], [got_ref], send_sems, recv_sems, local_sems, True)
        i = pl.program_id(0)
        pl.when(i == 0)(start)
        slot = i % 2

        def loads(step, into):
            cols = pl.ds(pl.multiple_of(step * 128, 128), 128)
            return [pltpu.make_async_copy(h.at[:, 0, cols], in_buf.at[into, k], in_sems.at[into, k])
                    for k, h in enumerate((w_hbm, m_hbm, v_hbm))]

        cols = pl.ds(pl.multiple_of(i * 128, 128), 128)
        stores = [pltpu.make_async_copy(out_buf.at[k], h.at[:, 0, cols], out_sems.at[k])
                  for k, h in enumerate((g_hbm, d_hbm, nm_hbm, nv_hbm))]

        @pl.when(i == 0)
        def _():
            for cp in loads(0, 0):
                cp.start()

        @pl.when(i + 1 < steps)
        def _():
            for cp in loads(i + 1, 1 - slot):
                cp.start()

        g = p_ref[0].astype(F32)
        for d in range(1, n_part):
            g = g + p_ref[d].astype(F32)
        for cp in loads(i, slot):
            cp.wait()
        delta, nm, nv = _adamw(in_buf[slot, 0], g, in_buf[slot, 1], in_buf[slot, 2])

        @pl.when(i > 0)
        def _():
            for cp in stores:
                cp.wait()

        for k, val in enumerate((g, delta, nm, nv)):
            out_buf[k] = val
        for cp in stores:
            cp.start()

        @pl.when(i == steps - 1)
        def _():
            for cp in stores:
                cp.wait()
            wait()

    return pl.pallas_call(
        body, name="adamw_w_in", grid=(steps,),
        in_specs=[pl.BlockSpec((n_part, r, 128), lambda i: (0, 0, i)), ANY, ANY, ANY, ANY],
        out_specs=[ANY] * 5,
        out_shape=[jax.ShapeDtypeStruct((r, 1, c), F32)] * 4 + [_exchange_shape(small, True)],
        scratch_shapes=[pltpu.VMEM((2, 3, r, 128), F32), pltpu.VMEM((4, r, 128), F32),
                        pltpu.SemaphoreType.DMA((2, 3)), pltpu.SemaphoreType.DMA((4,))] + _exchange_sems(1),
        compiler_params=_params(dimension_semantics=("arbitrary",)),
    )(parts, w, m, v, small)


def _adamw_small(got, me, metas, norms, finals, biases):
    def body(me_ref, gm_ref, gr_ref, *refs):
        ins, outs = refs[:12], refs[12:]
        g_meta, g_rest = gm_ref[0], gr_ref[0]
        for d in range(1, N_DEV):
            g_meta, g_rest = g_meta + gm_ref[d], g_rest + gr_ref[d]
        grads = [g_meta, g_rest[0:1], g_rest[1:2], g_rest[2:3, :FOX_HEADS]]
        for k, g in enumerate(grads):
            w_ref, m_ref, v_ref = ins[3 * k:3 * k + 3]
            delta, new_m, new_v = _adamw(w_ref[...], g, m_ref[...], v_ref[...])
            for o_ref, val in zip(outs[4 * k:4 * k + 4], (g, delta, new_m, new_v)):
                o_ref[...] = val
        outs[16][...] = g_rest[3:4, :128]

    groups = (metas, norms, finals, biases)
    full = lambda a: pl.BlockSpec(a.shape, lambda i, me_ref: (0,) * a.ndim)
    flat = [a for grp in groups for a in grp]
    res = pl.pallas_call(
        body, name="adamw_small",
        grid_spec=pltpu.PrefetchScalarGridSpec(
            num_scalar_prefetch=1, grid=(1,),
            in_specs=[pl.BlockSpec((N_DEV, N_META, META_BLK), lambda i, me_ref: (0, 0, me_ref[0])),
                      pl.BlockSpec((N_DEV, 8, D_MODEL), lambda i, me_ref: (0, N_META // 8, 0))]
            + [full(a) for a in flat],
            out_specs=[full(grp[0]) for grp in groups for _ in range(4)]
            + [pl.BlockSpec((1, 128), lambda i, me_ref: (0, 0))]),
        out_shape=[jax.ShapeDtypeStruct(grp[0].shape, F32) for grp in groups for _ in range(4)]
        + [jax.ShapeDtypeStruct((1, 128), F32)],
        compiler_params=_params(dimension_semantics=("arbitrary",)),
    )(me, got, got, *flat)
    return [res[4 * k:4 * k + 4] for k in range(4)], res[16]


def kernel(x, meta_tokens, norm_g, w_in, b_f, w_out, final_g, loss_target, m_meta_tokens, m_norm_g, m_w_in, m_b_f, m_w_out, m_final_g, v_meta_tokens, v_norm_g, v_w_in, v_b_f, v_w_out, v_final_g):
    core = lax.axis_index("c")
    me = 4 * lax.axis_index("x") + 2 * lax.axis_index("y") + core
    tables = _tables()

    wt_all, meta_all = _gather_two_level([_slab(w_in[0].T.astype(BF16), me), meta_tokens], name="gather_w_in")
    slabs, tail = _join_edges(wt_all)
    meta_full = jnp.transpose(meta_all, (1, 0, 2)).reshape(N_META, D_MODEL)
    h_pad = jnp.concatenate([jnp.zeros((PAD, D_MODEL), F32), meta_full, x[0]], axis=0)
    b_pad = jnp.pad(b_f, ((0, 0), (0, 128 - FOX_HEADS)))

    u, z = _rms_z_ret(h_pad, norm_g, slabs)
    z, zff = _z_fox(u, slabs, tail, z)
    bias = _forget_fwd(zff, b_pad)[:FOX_HEADS].reshape(FOX_HEADS, 1, T)
    raw, y = _ret_fwd(z, tables)
    a_f, lse, y, w_out_all = _fox_fwd(z, bias, y, w_out[0].astype(BF16))
    w_out_b = w_out_all.reshape(D_MIX, D_MODEL)
    dout, dout_b, dy, loss_blk, d_final_g = _out_loss_dy(y, w_out_b, h_pad, loss_target[0],
                                                         final_g.reshape(1, D_MODEL))

    d_w_out = _mm_tn(y, dout_b, tm=D_MIX, tn=256, name="mm_dwout")
    dz = _ret_bwd(z, tables, raw, dy)
    dz, dc, got_w_out = _fox_bwd(z, bias, a_f, lse, dy, dz, d_w_out.reshape(N_DEV, WO_BLK, D_MODEL))
    dz, db_f = _forget_bwd(zff, b_pad, dc.reshape(FOX_HEADS, T), dz)

    pair = _dwin_pair_slabs(dz, u, core)
    dh, d_norm_g, got_slabs = _du_rms(dz, slabs, tail, h_pad, dout, norm_g, pair)
    got_w_in = _unslab(got_slabs, me)

    small = jnp.concatenate([
        dh[PAD:CHUNK], d_norm_g, d_final_g, jnp.pad(db_f[:, :FOX_HEADS], ((0, 0), (0, D_MODEL - FOX_HEADS))),
        jnp.pad(loss_blk[0:1], ((0, 0), (0, D_MODEL - 128))),
        jnp.zeros((SMALL_ROWS - N_META - 4, D_MODEL), F32)], axis=0)
    fore = lambda a: jnp.transpose(a, (2, 0, 1))
    g_w_in, d_w_in, nm_w_in, nv_w_in, got_small = _sum_adamw_w_in(
        got_w_in, fore(w_in), fore(m_w_in), fore(v_w_in), small)
    g_w_out, d_w_out, nm_w_out, nv_w_out = _sum_adamw(got_w_out, w_out[0], m_w_out[0], v_w_out[0], 128, "adamw_w_out")

    row = lambda a: a.reshape(1, D_MODEL)
    (meta_o, norm_o, final_o, bias_o), loss_row = _adamw_small(
        got_small, me.astype(jnp.int32).reshape(1),
        (meta_tokens, m_meta_tokens, v_meta_tokens), (norm_g, m_norm_g, v_norm_g),
        (row(final_g), row(m_final_g), row(v_final_g)), (b_f, m_b_f, v_b_f))
    final_o = [a.reshape(D_MODEL) for a in final_o]

    back = lambda a: jnp.transpose(a, (1, 2, 0))
    outs = [[meta_o[k], norm_o[k], back(wk), bias_o[k], ok[None], final_o[k]]
            for k, (wk, ok) in enumerate(zip((g_w_in, d_w_in, nm_w_in, nv_w_in),
                                             (g_w_out, d_w_out, nm_w_out, nv_w_out)))]
    return (loss_row[0, 0], dh[CHUNK:][None], *outs[0], *outs[1], *outs[2], *outs[3])
```

```python
import numpy as np
import jax
import jax.numpy as jnp
from jax import lax
from jax.experimental import pallas as pl
from jax.experimental.pallas import tpu as pltpu

F32 = jnp.float32
BF16 = jnp.bfloat16

N_DEV = 8
N_CHIP = 4
D_MODEL = 1024
SEQ = 2048
N_META = 16
CHUNK = 128
PAD = CHUNK - N_META
T = SEQ + CHUNK
NCHUNK = T // CHUNK
D_MIX = 2048
RET_HEADS = 4
RET_DK = 128
RET_DV = 256
RET_W = 2 * RET_DK + 2 * RET_DV
FOX_HEADS = 16
FOX_D = 64
FOX_PAIRS = FOX_HEADS // 2
FOX_W = 4 * 128
FOX_BASE = RET_HEADS * RET_W
FF_BASE = FOX_BASE + FOX_PAIRS * FOX_W
D_IN = 7184
D_IN_PAD = 7296
W_BLK = D_IN // N_DEV
WO_BLK = D_MIX // N_DEV
META_BLK = D_MODEL // N_DEV
EPS = 1e-6
NEG_INF = -1e30
ROPE_BASE = 10000.0
LOG2E = 1.4426950408889634
LN2 = 0.6931471805599453

ADAM_LR = 0.001
ADAM_B1 = 0.9
ADAM_B2 = 0.999
ADAM_EPS = 1e-08
ADAM_WD = 0.01
ADAM_STEP = 10

SMALL_ROWS = 24
VMEM_LIMIT = 56 * 1024 * 1024
MESH = pl.DeviceIdType.MESH
ANY = pl.BlockSpec(memory_space=pl.ANY)

_NT = (((1,), (1,)), ((), ()))
_TN = (((0,), (0,)), ((), ()))


def _dot(a, b):
    return jnp.dot(a, b, preferred_element_type=F32)


def _dot_nt(a, b):
    return lax.dot_general(a, b, _NT, preferred_element_type=F32)


def _dot_tn(a, b):
    return lax.dot_general(a, b, _TN, preferred_element_type=F32)


def _params(**kw):
    return pltpu.CompilerParams(vmem_limit_bytes=VMEM_LIMIT, **kw)


def _silu_parts(g):
    sig = jax.nn.sigmoid(g)
    return g * sig, sig * (1.0 + g * (1.0 - sig))


def _tables():
    pos = np.arange(T, dtype=np.float32) - PAD
    inv = (ROPE_BASE ** (-np.arange(0, RET_DK, 2, dtype=np.float32) / RET_DK)).astype(np.float32)
    ang = pos[:, None] * inv[None, :]
    cos, sin = np.cos(ang), np.sin(ang)
    cosf = np.concatenate([cos, cos], axis=1).astype(np.float32)
    sins = np.concatenate([-sin, sin], axis=1).astype(np.float32)
    h = np.arange(RET_HEADS, dtype=np.float32)
    log_gamma = np.log1p(-np.exp2(-5.0 - h)).astype(np.float32)
    idx = np.arange(CHUNK, dtype=np.float32)
    diff = idx[:, None] - idx[None, :]
    dmask = np.where(diff[None] >= 0,
                     np.exp(log_gamma[:, None, None] * np.maximum(diff, 0.0)[None]), 0.0)
    zeta = np.exp(log_gamma[:, None] * (CHUNK - 1.0 - idx)[None, :])
    xi = np.exp(log_gamma[:, None] * (idx + 1.0)[None, :])
    cdec = np.exp(log_gamma * CHUNK)
    return (jnp.asarray(cosf), jnp.asarray(sins), jnp.asarray(dmask, F32),
            jnp.asarray(zeta[:, :, None], F32), jnp.asarray(xi[:, :, None], F32),
            jnp.asarray(cdec[:, None, None], F32))


W_STRIDE = 896
W_SLAB = 912
W_EDGE = W_SLAB - W_STRIDE
def _slab(block, me):
    shift = W_BLK - W_STRIDE
    return lax.switch(me, [lambda b, d=d: jnp.pad(b, ((shift * d, W_SLAB - W_BLK - shift * d), (0, 0)))
                           for d in range(N_DEV)], block)


def _unslab(slabs, me):
    shift = W_BLK - W_STRIDE
    return lax.switch(me, [lambda *s, d=d: jnp.stack([a[shift * d:shift * d + W_BLK] for a in s])
                           for d in range(N_DEV)], *slabs)


def _join_edges(slabs):
    last = slabs[:, W_STRIDE:]
    first = slabs[:, :W_EDGE] + jnp.concatenate([jnp.zeros_like(last[:1]), last[:-1]], axis=0)
    tail = jnp.pad(last[N_DEV - 1], ((0, 128 - W_EDGE), (0, 0)))
    return lax.dynamic_update_slice(slabs, first, (0, 0, 0)), tail


def _mm_tn(a, b, *, tm, tn, name):
    k, m = a.shape
    n = b.shape[1]
    assert m % tm == 0 and n % tn == 0

    def body(a_ref, b_ref, o_ref):
        o_ref[...] = _dot_tn(a_ref[...], b_ref[...]).astype(BF16)

    return pl.pallas_call(
        body, name=name, grid=(n // tn, m // tm),
        in_specs=[pl.BlockSpec((k, tm), lambda j, i: (0, i)),
                  pl.BlockSpec((k, tn), lambda j, i: (0, j))],
        out_specs=pl.BlockSpec((tm, tn), lambda j, i: (i, j)),
        out_shape=jax.ShapeDtypeStruct((m, n), BF16),
        compiler_params=_params(dimension_semantics=("arbitrary", "arbitrary")),
    )(a, b)


def _piece_spec(base, mult):
    def index(i):
        p = base + mult * i
        return p // 7, p % 7, 0
    return pl.BlockSpec((1, 128, D_MODEL), index)


_RET_PIECES = ((0, 1), (4, 1), (8, 2), (9, 2), (16, 2), (17, 2))
_FOX_PIECES = ((24, 1), (32, 1), (40, 1), (48, 1))


def _rms_z_ret(h_pad, g, slabs):
    def body(h_ref, g_ref, *refs):
        pieces, u_ref, z_ref = refs[:6], refs[6], refs[7]

        @pl.when(pl.program_id(0) == 0)
        def _():
            h = h_ref[...]
            r = lax.rsqrt(jnp.mean(h * h, axis=-1, keepdims=True) + EPS)
            u_ref[...] = (h * r * g_ref[...]).astype(BF16)

        w = jnp.concatenate([p[0] for p in pieces], axis=0)
        z_ref[...] = _dot_nt(u_ref[...], w).astype(BF16)

    whole = pl.BlockSpec((T, D_MODEL), lambda i: (0, 0))
    return pl.pallas_call(
        body, name="rms_z_ret", grid=(RET_HEADS,),
        in_specs=[whole, pl.BlockSpec((1, D_MODEL), lambda i: (0, 0))]
        + [_piece_spec(*bm) for bm in _RET_PIECES],
        out_specs=[whole, pl.BlockSpec((T, RET_W), lambda i: (0, i))],
        out_shape=[jax.ShapeDtypeStruct((T, D_MODEL), BF16),
                   jax.ShapeDtypeStruct((T, D_IN_PAD), BF16)],
        compiler_params=_params(dimension_semantics=("arbitrary",)),
    )(h_pad, g, *([slabs] * 6))


def _z_fox(u, slabs, tail, z):
    def body(u_ref, *refs):
        pieces, t_ref, z_ref, zff_ref = refs[:4], refs[4], refs[6], refs[7]
        u = u_ref[...]
        w = jnp.concatenate([p[0] for p in pieces], axis=0)
        z_ref[...] = _dot_nt(u, w).astype(BF16)

        @pl.when(pl.program_id(0) == 0)
        def _():
            zff_ref[...] = _dot_nt(u, t_ref[...])

    return pl.pallas_call(
        body, name="z_fox", grid=(FOX_PAIRS,),
        in_specs=[pl.BlockSpec((T, D_MODEL), lambda i: (0, 0))] + [_piece_spec(*bm) for bm in _FOX_PIECES]
        + [pl.BlockSpec((128, D_MODEL), lambda i: (0, 0)), ANY],
        out_specs=[_FOX_Z_SPEC, pl.BlockSpec((T, 128), lambda i: (0, 0))],
        out_shape=[jax.ShapeDtypeStruct((T, D_IN_PAD), BF16),
                   jax.ShapeDtypeStruct((T, 128), F32)],
        input_output_aliases={6: 0},
        compiler_params=_params(dimension_semantics=("arbitrary",)),
    )(u, *([slabs] * 4), tail, z)


def _out_loss_dy(y, w_out_b, h_pad, target, g):
    tm = T // 4

    def body(y_ref, w_ref, h_ref, t_hbm, g_ref, d_ref, db_ref, dy_ref, loss_ref, dg_ref, t_buf, t_sem):
        i = pl.program_id(0)
        head = pltpu.make_async_copy(t_hbm.at[pl.ds(0, tm - CHUNK)], t_buf.at[pl.ds(CHUNK, tm - CHUNK)], t_sem)
        rest = pltpu.make_async_copy(t_hbm.at[pl.ds(pl.multiple_of(jnp.maximum(i, 1) * tm - CHUNK, 8), tm)],
                                     t_buf, t_sem)

        @pl.when(i == 0)
        def _():
            t_buf[pl.ds(0, CHUNK), :] = jnp.zeros((CHUNK, D_MODEL), F32)
            head.start()
            loss_ref[...] = jnp.zeros_like(loss_ref)
            dg_ref[...] = jnp.zeros_like(dg_ref)

        pl.when(i > 0)(rest.start)

        w = w_ref[...]
        o = _dot(y_ref[...], w) + h_ref[...]
        pl.when(i == 0)(head.wait)
        pl.when(i > 0)(rest.wait)
        token = lax.broadcasted_iota(jnp.int32, (tm, 1), 0) + i * tm >= CHUNK
        g = g_ref[...]
        r = lax.rsqrt(jnp.mean(o * o, axis=-1, keepdims=True) + EPS)
        xn = o * r
        e = jnp.where(token, xn * g - t_buf[...], 0.0)
        loss_ref[...] += jnp.full(loss_ref.shape, 0.5 / D_MODEL * jnp.sum(e * e), F32)
        do = e * (1.0 / D_MODEL)
        dg_ref[...] += jnp.sum(do * xn, axis=0, keepdims=True)
        dn = do * g
        d = r * (dn - xn * jnp.mean(dn * xn, axis=-1, keepdims=True))
        d_b = d.astype(BF16)
        d_ref[...] = d
        db_ref[...] = d_b
        dy_ref[...] = _dot_nt(d_b, w).astype(BF16)

    tile = pl.BlockSpec((tm, D_MODEL), lambda i: (i, 0))
    wide = pl.BlockSpec((tm, D_MIX), lambda i: (i, 0))
    return pl.pallas_call(
        body, name="out_loss_dy", grid=(T // tm,),
        in_specs=[wide, pl.BlockSpec((D_MIX, D_MODEL), lambda i: (0, 0)), tile, ANY,
                  pl.BlockSpec((1, D_MODEL), lambda i: (0, 0))],
        out_specs=[tile, tile, wide,
                   pl.BlockSpec((8, 128), lambda i: (0, 0)),
                   pl.BlockSpec((1, D_MODEL), lambda i: (0, 0))],
        out_shape=[jax.ShapeDtypeStruct((T, D_MODEL), F32),
                   jax.ShapeDtypeStruct((T, D_MODEL), BF16),
                   jax.ShapeDtypeStruct((T, D_MIX), BF16),
                   jax.ShapeDtypeStruct((8, 128), F32),
                   jax.ShapeDtypeStruct((1, D_MODEL), F32)],
        scratch_shapes=[pltpu.VMEM((tm, D_MODEL), F32), pltpu.SemaphoreType.DMA],
        compiler_params=_params(dimension_semantics=("arbitrary",)),
    )(y, w_out_b, h_pad, target, g)


def _coords():
    return lax.axis_index("x"), lax.axis_index("y"), lax.axis_index("c")


def _flip(v, bit):
    return 1 - v if bit else v


def _peer(x, y, c, r):
    return _flip(x, (r >> 2) & 1), _flip(y, (r >> 1) & 1), _flip(c, r & 1)


def _direct_exchange(ins, outs, send_sems, recv_sems, local_sems, gather):
    x, y, c = _coords()
    me = 4 * x + 2 * y + c

    def src(k, to_idx):
        return ins[k] if gather else ins[k].at[to_idx]

    local = [pltpu.make_async_copy(src(k, me), outs[k].at[me], local_sems.at[k])
             for k in range(len(ins))]
    sends, recvs = [], []
    for r in range(1, N_DEV):
        px, py, pc = _peer(x, y, c, r)
        peer = 4 * px + 2 * py + pc
        for k in range(len(ins)):
            sems = dict(send_sem=send_sems.at[k, r - 1], recv_sem=recv_sems.at[k, r - 1],
                        device_id=(px, py, pc), device_id_type=MESH)
            sends.append(pltpu.make_async_remote_copy(src_ref=src(k, peer), dst_ref=outs[k].at[me], **sems))
            recvs.append(pltpu.make_async_remote_copy(src_ref=src(k, peer), dst_ref=outs[k].at[peer], **sems))

    def start():
        for cp in local + sends:
            cp.start()

    def wait():
        for cp in recvs:
            cp.wait_recv()
        for cp in sends:
            cp.wait_send()
        for cp in local:
            cp.wait()

    return start, wait


def _exchange_sems(n_arr):
    return [pltpu.SemaphoreType.DMA((n_arr, N_DEV - 1)), pltpu.SemaphoreType.DMA((n_arr, N_DEV - 1)),
            pltpu.SemaphoreType.DMA((n_arr,))]


def _exchange_shape(a, gather):
    return jax.ShapeDtypeStruct(((N_DEV,) + a.shape) if gather else a.shape, a.dtype)


def _gather_two_level(arrays, name):
    n_arr = len(arrays)

    def body(*refs):
        ins, outs = refs[:n_arr], refs[n_arr:2 * n_arr]
        send_sems, recv_sems, local_sems = refs[2 * n_arr:]
        x, y, c = _coords()

        def slot(k, px, py, pc):
            return outs[k].at[4 * px + 2 * py + pc]

        def routed(core):
            me, sibling = (x, y, core), (x, y, 1 - core)
            xn, yn, dg = (1 - x, y), (x, 1 - y), (1 - x, 1 - y)
            (first, s_first), (second, s_second) = ((xn, 1), (yn, 2)) if core == 0 else ((yn, 2), (xn, 1))

            def copy(k, j, block, to, own=False):
                return pltpu.make_async_remote_copy(
                    src_ref=ins[k] if own else slot(k, *block), dst_ref=slot(k, *block),
                    send_sem=send_sems.at[k, j], recv_sem=recv_sems.at[k, j],
                    device_id=to, device_id_type=MESH)

            local = [pltpu.make_async_copy(ins[k], slot(k, *me), local_sems.at[k]) for k in range(n_arr)]
            sent = []
            for k in range(n_arr):
                sent += [copy(k, 0, me, sibling, True), copy(k, 1, me, (*xn, core), True),
                         copy(k, 2, me, (*yn, core), True)]
            for cp in local + sent:
                cp.start()

            def pass_on(k, j_from, j_to, block, targets):
                copy(k, j_from, block, me).wait_recv()
                for j, to in zip(j_to, targets):
                    cp = copy(k, j, block, to)
                    cp.start()
                    sent.append(cp)

            for k in range(n_arr):
                pass_on(k, s_first, (3, 3 + s_first), (*first, core), ((*second, core), sibling))
            for k in range(n_arr):
                pass_on(k, s_second, (3 + s_second,), (*second, core), (sibling,))
            for k in range(n_arr):
                pass_on(k, 3, (6,), (*dg, core), (sibling,))
            for k in range(n_arr):
                copy(k, 0, sibling, me).wait_recv()
                for j, chip in ((4, xn), (5, yn), (6, dg)):
                    copy(k, j, (*chip, 1 - core), me).wait_recv()
            for cp in sent:
                cp.wait_send()
            for cp in local:
                cp.wait()

        for core in (0, 1):
            pl.when(c == core)(lambda core=core: routed(core))

    return pl.pallas_call(
        body, name=name,
        in_specs=[ANY] * n_arr, out_specs=[ANY] * n_arr,
        out_shape=[_exchange_shape(a, True) for a in arrays],
        scratch_shapes=_exchange_sems(n_arr),
    )(*arrays)


def _piece_columns():
    pos = {}
    for h in range(RET_HEADS):
        for k, p in enumerate((h, 4 + h, 8 + 2 * h, 9 + 2 * h, 16 + 2 * h, 17 + 2 * h)):
            pos[p] = 6 * h + k
    for p in range(FOX_PAIRS):
        for i in range(4):
            pos[24 + 8 * i + p] = 24 + 4 * p + i
    pos[D_IN_PAD // 128 - 1] = D_IN_PAD // 128 - 1
    return np.array([pos[7 * d + j] for d in range(N_DEV) for j in range(8)], np.int32)


def _routes(core):
    x, y, _ = _coords()
    return ((1 - x, y), (x, 1 - y)) if core == 0 else ((x, 1 - y), (1 - x, y))


def _dwin_pair_slabs(dz, u, core):
    def body(order_ref, cols_ref, *refs):
        pieces, u_ref = refs[:8], refs[8]
        kept_ref, theirs_ref, via_ref, direct_ref = refs[9:13]
        send_buf, keep_buf, got_buf, push_send, push_recv, hop_send, hop_recv, load_sem = refs[13:]
        s = pl.program_id(0)
        x, y, c = _coords()
        cols = jnp.concatenate([p[...] for p in pieces], axis=1)
        slab = _dot_tn(cols, u_ref[...])[:W_SLAB]

        def push(k):
            return pltpu.make_async_remote_copy(
                src_ref=send_buf.at[k], dst_ref=theirs_ref.at[k],
                send_sem=push_send.at[k], recv_sem=push_recv.at[k],
                device_id=(x, y, 1 - c), device_id_type=MESH)

        def hop(k, core):
            first, _ = _routes(core)
            return pltpu.make_async_remote_copy(
                src_ref=keep_buf.at[k], dst_ref=direct_ref if k == 0 else via_ref,
                send_sem=hop_send.at[k], recv_sem=hop_recv.at[k],
                device_id=(*first, core), device_id_type=MESH)

        for k in range(N_CHIP):
            @pl.when(s == 2 * k)
            def _(k=k):
                send_buf[k] = slab.astype(BF16)
                push(k).start()

            @pl.when(s == 2 * k + 1)
            def _(k=k):
                push(k).wait_recv()
                load = pltpu.make_async_copy(theirs_ref.at[k], got_buf, load_sem)
                load.start()
                load.wait()
                total = (slab + got_buf[...].astype(F32)).astype(BF16)
                if k < 2:
                    keep_buf[k] = total
                    for core in (0, 1):
                        pl.when(c == core)(lambda core=core: hop(k, core).start())
                else:
                    kept_ref[0] = total

        @pl.when(s == 2 * N_CHIP - 1)
        def _():
            for core in (0, 1):
                @pl.when(c == core)
                def _(core=core):
                    for k in range(2):
                        hop(k, core).wait_recv()
                        hop(k, core).wait_send()
            for k in range(N_CHIP):
                push(k).wait_send()

    x, y = lax.axis_index("x"), lax.axis_index("y")
    xn, yn, dg, own = 2 * (1 - x) + y, 2 * x + 1 - y, 2 * (1 - x) + 1 - y, 2 * x + y
    mine = jnp.where(core == 0, jnp.stack([xn, dg, yn, own]), jnp.stack([yn, dg, xn, own]))
    sibs = jnp.where(core == 0, jnp.stack([yn, dg, xn, own]), jnp.stack([xn, dg, yn, own]))
    order = jnp.stack([2 * sibs + (1 - core), 2 * mine + core], axis=1).reshape(2 * N_CHIP).astype(jnp.int32)
    piece = lambda j: pl.BlockSpec((T, 128), lambda s, order_ref, cols_ref: (0, cols_ref[order_ref[s] * 8 + j]))
    slab = jax.ShapeDtypeStruct((W_SLAB, D_MODEL), BF16)
    kept, _, via, direct = pl.pallas_call(
        body, name="dwin_pair_slabs",
        grid_spec=pltpu.PrefetchScalarGridSpec(
            num_scalar_prefetch=2, grid=(2 * N_CHIP,),
            in_specs=[piece(j) for j in range(8)]
            + [pl.BlockSpec((T, D_MODEL), lambda s, order_ref, cols_ref: (0, 0))],
            out_specs=[pl.BlockSpec((1, W_SLAB, D_MODEL),
                                    lambda s, order_ref, cols_ref: (jnp.where(s >= 6, 1, 0), 0, 0)),
                       ANY, ANY, ANY],
            scratch_shapes=[pltpu.VMEM((N_CHIP, W_SLAB, D_MODEL), BF16), pltpu.VMEM((2, W_SLAB, D_MODEL), BF16),
                            pltpu.VMEM((W_SLAB, D_MODEL), BF16),
                            pltpu.SemaphoreType.DMA((N_CHIP,)), pltpu.SemaphoreType.DMA((N_CHIP,)),
                            pltpu.SemaphoreType.DMA((2,)), pltpu.SemaphoreType.DMA((2,)),
                            pltpu.SemaphoreType.DMA]),
        out_shape=[jax.ShapeDtypeStruct((2, W_SLAB, D_MODEL), BF16),
                   jax.ShapeDtypeStruct((N_CHIP, W_SLAB, D_MODEL), BF16), slab, slab],
        compiler_params=_params(dimension_semantics=("arbitrary",)),
    )(order, jnp.asarray(_piece_columns()), *([dz] * 8), u)
    return kept, via, direct


def _du_rms(dz, slabs, tail, h_pad, dout, g, kept, via):
    tm = 272
    steps = T // tm
    columns = _piece_columns().reshape(N_DEV, 8)

    def body(dz_ref, w_ref, t_ref, h_ref, d_ref, g_ref, kept_ref, via_ref, dh_ref, dg_ref, got_ref,
             via_buf, mine_buf, send_sem, recv_sem, local_sems):
        i = pl.program_id(0)
        x, y, c = _coords()

        def summed(core):
            _, second = _routes(core)
            return pltpu.make_async_remote_copy(
                src_ref=mine_buf, dst_ref=got_ref, send_sem=send_sem, recv_sem=recv_sem,
                device_id=(*second, core), device_id_type=MESH)

        @pl.when(i == 0)
        def _():
            loads = [pltpu.make_async_copy(kept_ref.at[0], mine_buf, local_sems.at[0]),
                     pltpu.make_async_copy(via_ref, via_buf, local_sems.at[1])]
            for cp in loads:
                cp.start()
            for cp in loads:
                cp.wait()
            mine_buf[...] = (mine_buf[...].astype(F32) + via_buf[...].astype(F32)).astype(BF16)
            for core in (0, 1):
                pl.when(c == core)(lambda core=core: summed(core).start())
            dg_ref[...] = jnp.zeros_like(dg_ref)

        du = _dot(dz_ref[:, pl.ds(FF_BASE, 128)], t_ref[...])
        for d in range(N_DEV):
            cols = jnp.concatenate([dz_ref[:, pl.ds(128 * int(columns[d, j]), 128)] for j in range(7)], axis=1)
            du = du + _dot(cols, w_ref[d, pl.ds(0, W_STRIDE), :])
        h = h_ref[...]
        r = lax.rsqrt(jnp.mean(h * h, axis=-1, keepdims=True) + EPS)
        xn = h * r
        dg_ref[...] += jnp.sum(du * xn, axis=0, keepdims=True)
        dn = du * g_ref[...]
        dh_ref[...] = d_ref[...] + r * (dn - xn * jnp.mean(dn * xn, axis=-1, keepdims=True))

        for core in (0, 1):
            @pl.when(jnp.logical_and(c == core, i == steps - 1))
            def _(core=core):
                summed(core).wait_recv()
                summed(core).wait_send()

    tile = pl.BlockSpec((tm, D_MODEL), lambda i: (i, 0))
    return pl.pallas_call(
        body, name="du_rms", grid=(steps,),
        in_specs=[pl.BlockSpec((tm, D_IN_PAD), lambda i: (i, 0)),
                  pl.BlockSpec((N_DEV, W_SLAB, D_MODEL), lambda i: (0, 0, 0)),
                  pl.BlockSpec((128, D_MODEL), lambda i: (0, 0)),
                  tile, tile, pl.BlockSpec((1, D_MODEL), lambda i: (0, 0)), ANY, ANY],
        out_specs=[tile, pl.BlockSpec((1, D_MODEL), lambda i: (0, 0)), ANY],
        out_shape=[jax.ShapeDtypeStruct((T, D_MODEL), F32),
                   jax.ShapeDtypeStruct((1, D_MODEL), F32),
                   jax.ShapeDtypeStruct(via.shape, via.dtype)],
        scratch_shapes=[pltpu.VMEM(via.shape, via.dtype), pltpu.VMEM(via.shape, via.dtype),
                        pltpu.SemaphoreType.DMA, pltpu.SemaphoreType.DMA, pltpu.SemaphoreType.DMA((2,))],
        compiler_params=_params(dimension_semantics=("arbitrary",)),
    )(dz, slabs, tail, h_pad, dout, g, kept, via)


def _tri(lower):
    r = lax.broadcasted_iota(jnp.int32, (CHUNK, CHUNK), 0)
    c = lax.broadcasted_iota(jnp.int32, (CHUNK, CHUNK), 1)
    return jnp.where((r >= c) if lower else (r <= c), 1.0, 0.0).astype(F32)


def _row_valid(n):
    r = lax.broadcasted_iota(jnp.int32, (CHUNK, 128), 0) + n * CHUNK
    return r >= PAD


_FF_SPEC = pl.BlockSpec((T, 128), lambda i: (0, FF_BASE // 128))
_ZFF_SPEC = pl.BlockSpec((T, 128), lambda i: (0, 0))


def _forget_fwd(z, b_pad):
    def body(z_ref, b_ref, o_ref):
        tri = _tri(True)
        carry = jnp.zeros((1, 128), F32)
        for n in range(NCHUNK):
            rows = pl.ds(n * CHUNK, CHUNK)
            a = z_ref[rows, :] + b_ref[...]
            lf = -(jnp.maximum(-a, 0.0) + jnp.log(1.0 + jnp.exp(-jnp.abs(a))))
            lf = jnp.where(_row_valid(n), lf, 0.0)
            c = jnp.dot(tri, lf, precision=lax.Precision.HIGHEST,
                        preferred_element_type=F32) + carry
            carry = c[CHUNK - 1:CHUNK, :]
            o_ref[:, rows] = jnp.where(_row_valid(n), c * (-LOG2E), NEG_INF).T

    return pl.pallas_call(
        body, name="forget_fwd", grid=(1,),
        in_specs=[_ZFF_SPEC, pl.BlockSpec((1, 128), lambda i: (0, 0))],
        out_specs=pl.BlockSpec((128, T), lambda i: (0, 0)),
        out_shape=jax.ShapeDtypeStruct((128, T), F32),
        compiler_params=_params(dimension_semantics=("arbitrary",)),
    )(z, b_pad)


def _forget_bwd(z, b_pad, dc, dz):
    def body(z_ref, b_ref, dc_ref, dz_in, dff_ref, db_ref):
        tri = _tri(False)
        carry = jnp.zeros((1, 128), F32)
        db = jnp.zeros((1, 128), F32)
        for n in reversed(range(NCHUNK)):
            rows = pl.ds(n * CHUNK, CHUNK)
            dc_blk = jnp.concatenate([dc_ref[:, rows], jnp.zeros((128 - FOX_HEADS, CHUNK), F32)], axis=0).T
            dlf = jnp.dot(tri, dc_blk, precision=lax.Precision.HIGHEST,
                          preferred_element_type=F32) + carry
            carry = dlf[0:1, :]
            a = z_ref[rows, :] + b_ref[...]
            dff = jnp.where(_row_valid(n), dlf * jax.nn.sigmoid(-a), 0.0)
            dff_ref[rows, :] = dff.astype(BF16)
            db = db + jnp.sum(dff, axis=0, keepdims=True)
        db_ref[...] = db

    return pl.pallas_call(
        body, name="forget_bwd", grid=(1,),
        in_specs=[_ZFF_SPEC, pl.BlockSpec((1, 128), lambda i: (0, 0)),
                  pl.BlockSpec((FOX_HEADS, T), lambda i: (0, 0)), ANY],
        out_specs=[_FF_SPEC, pl.BlockSpec((1, 128), lambda i: (0, 0))],
        out_shape=[jax.ShapeDtypeStruct((T, D_IN_PAD), BF16),
                   jax.ShapeDtypeStruct((1, 128), F32)],
        input_output_aliases={3: 0},
        compiler_params=_params(dimension_semantics=("arbitrary",)),
    )(z, b_pad, dc, dz)


FOX_QB = 512
FOX_NQB = SEQ // FOX_QB


def _fox_block(b):
    lo = CHUNK + b * FOX_QB
    return pl.ds(lo, FOX_QB), lo, lo + FOX_QB


def _causal_bias():
    r = lax.broadcasted_iota(jnp.int32, (FOX_QB, FOX_QB), 0)
    c = lax.broadcasted_iota(jnp.int32, (FOX_QB, FOX_QB), 1)
    return jnp.where(c <= r, 0.0, NEG_INF).astype(F32)


def _fox_logits(q_blk, k_all, bias, causal, b):
    _, lo, hi = _fox_block(b)
    s_off = _dot_nt(q_blk, k_all[:lo]) + bias[:, :lo]
    s_dia = _dot_nt(q_blk, k_all[lo:hi]) + (bias[:, lo:hi] + causal)
    return s_off, s_dia


_FOX_Z_SPEC = pl.BlockSpec((T, FOX_W), lambda p: (0, FOX_BASE // FOX_W + p))
_FOX_BIAS_SPEC = pl.BlockSpec((2, 1, T), lambda p: (p, 0, 0))
_FOX_LSE_SPEC = pl.BlockSpec((2, T, 1), lambda p: (p, 0, 0))
_FOX_SCALE = FOX_D ** -0.5
_FOX_QSCALE = _FOX_SCALE * LOG2E


def _fox_fwd(z, bias, y, w_out_blk):
    last = FOX_PAIRS - 1

    def body(z_ref, b_ref, y_in, w_ref, a_ref, lse_ref, y_ref, wall_ref,
             send_sems, recv_sems, local_sems):
        start, wait = _direct_exchange([w_ref], [wall_ref], send_sems, recv_sems, local_sems, True)
        pl.when(pl.program_id(0) == 0)(start)

        causal = _causal_bias()
        a_ref[pl.ds(0, CHUNK), :] = jnp.zeros((CHUNK, 128), F32)
        y_ref[pl.ds(0, CHUNK), :] = jnp.zeros((CHUNK, 128), BF16)
        for j in range(2):
            lanes = pl.ds(j * FOX_D, FOX_D)
            k_all = z_ref[:, pl.ds(128 + j * FOX_D, FOX_D)]
            v_all = z_ref[:, pl.ds(256 + j * FOX_D, FOX_D)]
            bias = b_ref[j]
            lse_ref[j, pl.ds(0, CHUNK), :] = jnp.zeros((CHUNK, 1), F32)
            for b in range(FOX_NQB):
                rows, lo, hi = _fox_block(b)
                q_blk = (z_ref[rows, lanes].astype(F32) * _FOX_QSCALE).astype(BF16)
                s_off, s_dia = _fox_logits(q_blk, k_all, bias, causal, b)
                m = jnp.maximum(jnp.max(s_off, axis=-1, keepdims=True),
                                jnp.max(s_dia, axis=-1, keepdims=True))
                e_off = jnp.exp2(s_off - m)
                e_dia = jnp.exp2(s_dia - m)
                total = jnp.sum(e_off, axis=-1, keepdims=True) + jnp.sum(e_dia, axis=-1, keepdims=True)
                o = (_dot(e_off.astype(BF16), v_all[:lo]) + _dot(e_dia.astype(BF16), v_all[lo:hi])) / total
                a_ref[rows, lanes] = o
                lse_ref[j, rows, :] = m + jnp.log(total) * LOG2E
                gate = _silu_parts(z_ref[rows, pl.ds(384 + j * FOX_D, FOX_D)].astype(F32))[0]
                y_ref[rows, lanes] = (o * gate).astype(BF16)

        pl.when(pl.program_id(0) == last)(wait)

    return pl.pallas_call(
        body, name="fox_fwd", grid=(FOX_PAIRS,),
        in_specs=[_FOX_Z_SPEC, _FOX_BIAS_SPEC, ANY, ANY],
        out_specs=[pl.BlockSpec((T, 128), lambda p: (0, p)), _FOX_LSE_SPEC,
                   pl.BlockSpec((T, 128), lambda p: (0, 8 + p)), ANY],
        out_shape=[jax.ShapeDtypeStruct((T, FOX_HEADS * FOX_D), F32),
                   jax.ShapeDtypeStruct((FOX_HEADS, T, 1), F32),
                   jax.ShapeDtypeStruct((T, D_MIX), BF16),
                   _exchange_shape(w_out_blk, True)],
        input_output_aliases={2: 2},
        scratch_shapes=_exchange_sems(1),
        compiler_params=_params(dimension_semantics=("arbitrary",)),
    )(z, bias, y, w_out_blk)


def _fox_bwd(z, bias, a_f, lse, dy, dz, dwo_blocks):
    last = FOX_PAIRS - 1

    def body(z_ref, b_ref, a_ref, lse_ref, dy_ref, dz_in, dwo_ref, dz_ref, dc_ref, got_ref,
             kv_acc, dc_acc, send_sems, recv_sems, local_sems):
        start, wait = _direct_exchange([dwo_ref], [got_ref], send_sems, recv_sems, local_sems, False)
        pl.when(pl.program_id(0) == 0)(start)

        causal = _causal_bias()
        dz_ref[pl.ds(0, CHUNK), pl.ds(0, 128)] = jnp.zeros((CHUNK, 128), BF16)
        dz_ref[pl.ds(0, CHUNK), pl.ds(384, 128)] = jnp.zeros((CHUNK, 128), BF16)
        dk_rows, dv_rows = pl.ds(0, FOX_D), pl.ds(FOX_D, FOX_D)
        for j in range(2):
            lanes = pl.ds(j * FOX_D, FOX_D)
            k_all = z_ref[:, pl.ds(128 + j * FOX_D, FOX_D)]
            v_all = z_ref[:, pl.ds(256 + j * FOX_D, FOX_D)]
            bias = b_ref[j]
            kv_acc[...] = jnp.zeros_like(kv_acc)
            dc_acc[...] = jnp.zeros_like(dc_acc)
            for b in range(FOX_NQB):
                rows, lo, hi = _fox_block(b)
                off, dia = pl.ds(0, lo), pl.ds(lo, FOX_QB)
                q_blk = (z_ref[rows, lanes].astype(F32) * _FOX_QSCALE).astype(BF16)
                s_off, s_dia = _fox_logits(q_blk, k_all, bias, causal, b)
                lse_blk = lse_ref[j, rows, :]
                p_off, p_dia = jnp.exp2(s_off - lse_blk), jnp.exp2(s_dia - lse_blk)
                sg, dsg = _silu_parts(z_ref[rows, pl.ds(384 + j * FOX_D, FOX_D)].astype(F32))
                dyj = dy_ref[rows, lanes].astype(F32)
                dz_ref[rows, pl.ds(384 + j * FOX_D, FOX_D)] = (dyj * a_ref[rows, lanes] * dsg).astype(BF16)
                do_b = (dyj * sg).astype(BF16)
                dp_off = _dot_nt(do_b, v_all[:lo])
                dp_dia = _dot_nt(do_b, v_all[lo:hi])
                d = (jnp.sum(p_off * dp_off, axis=-1, keepdims=True)
                     + jnp.sum(p_dia * dp_dia, axis=-1, keepdims=True))
                ds_off = p_off * (dp_off - d)
                ds_dia = p_dia * (dp_dia - d)
                dc_acc[:, off] -= jnp.sum(ds_off, axis=0, keepdims=True)
                dc_acc[:, dia] -= jnp.sum(ds_dia, axis=0, keepdims=True)
                ds_off_b, ds_dia_b = ds_off.astype(BF16), ds_dia.astype(BF16)
                dq = _dot(ds_off_b, k_all[:lo]) + _dot(ds_dia_b, k_all[lo:hi])
                dz_ref[rows, lanes] = (dq * _FOX_SCALE).astype(BF16)
                kv_acc[dk_rows, off] += _dot_tn(q_blk, ds_off_b)
                kv_acc[dk_rows, dia] += _dot_tn(q_blk, ds_dia_b)
                kv_acc[dv_rows, off] += _dot_tn(do_b, p_off.astype(BF16))
                kv_acc[dv_rows, dia] += _dot_tn(do_b, p_dia.astype(BF16))
            for n in range(NCHUNK):
                rows = pl.ds(n * CHUNK, CHUNK)
                both = kv_acc[:, rows].T
                dz_ref[rows, pl.ds(128 + j * FOX_D, FOX_D)] = (both[:, :FOX_D] * LN2).astype(BF16)
                dz_ref[rows, pl.ds(256 + j * FOX_D, FOX_D)] = both[:, FOX_D:].astype(BF16)
            dc_ref[j] = dc_acc[...]

        pl.when(pl.program_id(0) == last)(wait)

    col = lambda base: pl.BlockSpec((T, 128), lambda p: (0, base + p))
    return pl.pallas_call(
        body, name="fox_bwd", grid=(FOX_PAIRS,),
        in_specs=[_FOX_Z_SPEC, _FOX_BIAS_SPEC, col(0), _FOX_LSE_SPEC, col(8), ANY, ANY],
        out_specs=[_FOX_Z_SPEC, _FOX_BIAS_SPEC, ANY],
        out_shape=[jax.ShapeDtypeStruct((T, D_IN_PAD), BF16),
                   jax.ShapeDtypeStruct((FOX_HEADS, 1, T), F32),
                   _exchange_shape(dwo_blocks, False)],
        input_output_aliases={5: 0},
        scratch_shapes=[pltpu.VMEM((2 * FOX_D, T), F32), pltpu.VMEM((1, T), F32)] + _exchange_sems(1),
        compiler_params=_params(dimension_semantics=("arbitrary",)),
    )(z, bias, a_f, lse, dy, dz, dwo_blocks)


def _rot(x, cosf, sins):
    return x * cosf + pltpu.roll(x, RET_DK // 2, 1) * sins


def _rot_t(d, cosf, sins):
    return d * cosf - pltpu.roll(d, RET_DK // 2, 1) * sins


_RET_Z_SPEC = pl.BlockSpec((T, RET_W), lambda h: (0, h))
_RET_TABLE_SPECS = [
    pl.BlockSpec((T, RET_DK), lambda h: (0, 0)),
    pl.BlockSpec((T, RET_DK), lambda h: (0, 0)),
    pl.BlockSpec((1, CHUNK, CHUNK), lambda h: (h, 0, 0)),
    pl.BlockSpec((1, CHUNK, 1), lambda h: (h, 0, 0)),
    pl.BlockSpec((1, CHUNK, 1), lambda h: (h, 0, 0)),
    pl.BlockSpec((1, 1, 1), lambda h: (h, 0, 0)),
]
_RQ, _RK = pl.ds(0, RET_DK), pl.ds(RET_DK, RET_DK)
_RV, _RG = pl.ds(2 * RET_DK, RET_DV), pl.ds(2 * RET_DK + RET_DV, RET_DV)
_RET_KSCALE = RET_DK ** -0.5


def _ret_fwd(z, tables):
    def body(z_ref, cos_ref, sin_ref, dm_ref, zeta_ref, xi_ref, cd_ref, raw_ref, y_ref):
        dmask, zeta, xi, cdec = dm_ref[0], zeta_ref[0], xi_ref[0], cd_ref[0]
        state = jnp.zeros((RET_DK, RET_DV), F32)
        for n in range(NCHUNK):
            rows = pl.ds(n * CHUNK, CHUNK)
            cosf, sins = cos_ref[rows, :], sin_ref[rows, :]
            qr = _rot(z_ref[rows, _RQ].astype(F32), cosf, sins)
            kr_b = (_rot(z_ref[rows, _RK].astype(F32), cosf, sins) * _RET_KSCALE).astype(BF16)
            v_b = z_ref[rows, _RV]
            a = _dot_nt(qr.astype(BF16), kr_b) * dmask
            out = _dot(a.astype(BF16), v_b) + _dot((qr * xi).astype(BF16), state.astype(BF16))
            state = state * cdec + _dot_tn(kr_b, (v_b.astype(F32) * zeta).astype(BF16))
            raw_ref[rows, :] = out
            r = lax.rsqrt(jnp.mean(out * out, axis=-1, keepdims=True) + EPS)
            y_ref[rows, :] = (out * r * _silu_parts(z_ref[rows, _RG].astype(F32))[0]).astype(BF16)

    wide = pl.BlockSpec((T, RET_DV), lambda h: (0, h))
    return pl.pallas_call(
        body, name="ret_fwd", grid=(RET_HEADS,),
        in_specs=[_RET_Z_SPEC] + _RET_TABLE_SPECS,
        out_specs=[wide, wide],
        out_shape=[jax.ShapeDtypeStruct((T, RET_HEADS * RET_DV), F32),
                   jax.ShapeDtypeStruct((T, D_MIX), BF16)],
        compiler_params=_params(dimension_semantics=("arbitrary",)),
    )(z, *tables)


def _ret_bwd(z, tables, raw, dy):
    def body(z_ref, cos_ref, sin_ref, dm_ref, zeta_ref, xi_ref, cd_ref, raw_ref, dy_ref,
             dz_ref, st_ref):
        dmask, zeta, xi, cdec = dm_ref[0], zeta_ref[0], xi_ref[0], cd_ref[0]

        def rotated(n):
            rows = pl.ds(n * CHUNK, CHUNK)
            cosf, sins = cos_ref[rows, :], sin_ref[rows, :]
            qr = _rot(z_ref[rows, _RQ].astype(F32), cosf, sins)
            kr_b = (_rot(z_ref[rows, _RK].astype(F32), cosf, sins) * _RET_KSCALE).astype(BF16)
            return rows, cosf, sins, qr, kr_b

        state = jnp.zeros((RET_DK, RET_DV), F32)
        for n in range(NCHUNK):
            st_ref[n] = state.astype(BF16)
            if n + 1 < NCHUNK:
                rows, _, _, _, kr_b = rotated(n)
                state = state * cdec + _dot_tn(kr_b, (z_ref[rows, _RV].astype(F32) * zeta).astype(BF16))

        grad_state = jnp.zeros((RET_DK, RET_DV), F32)
        for n in reversed(range(NCHUNK)):
            rows, cosf, sins, qr, kr_b = rotated(n)
            qr_b = qr.astype(BF16)
            v_b = z_ref[rows, _RV]
            gs_b = grad_state.astype(BF16)
            o = raw_ref[rows, :]
            r = lax.rsqrt(jnp.mean(o * o, axis=-1, keepdims=True) + EPS)
            hn = o * r
            sg, dsg = _silu_parts(z_ref[rows, _RG].astype(F32))
            dyn = dy_ref[rows, :].astype(F32)
            dz_ref[rows, _RG] = (dyn * hn * dsg).astype(BF16)
            dhn = dyn * sg
            do_b = (r * (dhn - hn * jnp.mean(dhn * hn, axis=-1, keepdims=True))).astype(BF16)
            a_b = (_dot_nt(qr_b, kr_b) * dmask).astype(BF16)
            da_b = (_dot_nt(do_b, v_b) * dmask).astype(BF16)
            dqr = _dot(da_b, kr_b) + xi * _dot_nt(do_b, st_ref[n])
            dkr = _dot_tn(da_b, qr_b) + zeta * _dot_nt(v_b, gs_b)
            dv = _dot_tn(a_b, do_b) + zeta * _dot(kr_b, gs_b)
            grad_state = grad_state * cdec + _dot_tn((qr * xi).astype(BF16), do_b)
            dz_ref[rows, _RQ] = _rot_t(dqr, cosf, sins).astype(BF16)
            dz_ref[rows, _RK] = (_rot_t(dkr, cosf, sins) * _RET_KSCALE).astype(BF16)
            dz_ref[rows, _RV] = dv.astype(BF16)

    wide = pl.BlockSpec((T, RET_DV), lambda h: (0, h))
    return pl.pallas_call(
        body, name="ret_bwd", grid=(RET_HEADS,),
        in_specs=[_RET_Z_SPEC] + _RET_TABLE_SPECS + [wide, wide],
        out_specs=_RET_Z_SPEC,
        out_shape=jax.ShapeDtypeStruct((T, D_IN_PAD), BF16),
        scratch_shapes=[pltpu.VMEM((NCHUNK, RET_DK, RET_DV), BF16)],
        compiler_params=_params(dimension_semantics=("arbitrary",)),
    )(z, *tables, raw, dy)


def _adamw(w, g, m, v):
    m = ADAM_B1 * m + (1.0 - ADAM_B1) * g
    v = ADAM_B2 * v + (1.0 - ADAM_B2) * (g * g)
    m_hat = m / (1.0 - ADAM_B1 ** ADAM_STEP)
    v_hat = v / (1.0 - ADAM_B2 ** ADAM_STEP)
    delta = -ADAM_LR * (m_hat / (jnp.sqrt(v_hat) + ADAM_EPS) + ADAM_WD * w)
    return delta, m, v


def _sum_adamw(parts, w, m, v, rows, name):
    _, r_tot, cols = parts.shape
    assert r_tot % rows == 0

    def body(p_ref, w_ref, m_ref, v_ref, g_ref, d_ref, nm_ref, nv_ref):
        g = p_ref[0].astype(F32)
        for d in range(1, N_DEV):
            g = g + p_ref[d].astype(F32)
        delta, nm, nv = _adamw(w_ref[...], g, m_ref[...], v_ref[...])
        g_ref[...] = g
        d_ref[...] = delta
        nm_ref[...] = nm
        nv_ref[...] = nv

    blk = pl.BlockSpec((rows, cols), lambda i: (i, 0))
    return pl.pallas_call(
        body, name=name, grid=(r_tot // rows,),
        in_specs=[pl.BlockSpec((N_DEV, rows, cols), lambda i: (0, i, 0)), blk, blk, blk],
        out_specs=[blk] * 4,
        out_shape=[jax.ShapeDtypeStruct((r_tot, cols), F32)] * 4,
        compiler_params=_params(dimension_semantics=("arbitrary",)),
    )(parts, w, m, v)


def _sum_adamw_w_in(parts, w, m, v, small):
    n_part, r, c = parts.shape
    steps = c // 128

    def body(p_ref, w_hbm, m_hbm, v_hbm, s_ref, g_hbm, d_hbm, nm_hbm, nv_hbm, got_ref,
             in_buf, out_buf, in_sems, out_sems, send_sems, recv_sems, local_sems):
        start, wait = _direct_exchange([s_ref], [got_ref], send_sems, recv_sems, local_sems, True)
        i = pl.program_id(0)
        pl.when(i == 0)(start)
        slot = i % 2

        def loads(step, into):
            cols = pl.ds(pl.multiple_of(step * 128, 128), 128)
            return [pltpu.make_async_copy(h.at[:, 0, cols], in_buf.at[into, k], in_sems.at[into, k])
                    for k, h in enumerate((w_hbm, m_hbm, v_hbm))]

        cols = pl.ds(pl.multiple_of(i * 128, 128), 128)
        stores = [pltpu.make_async_copy(out_buf.at[k], h.at[:, 0, cols], out_sems.at[k])
                  for k, h in enumerate((g_hbm, d_hbm, nm_hbm, nv_hbm))]

        @pl.when(i == 0)
        def _():
            for cp in loads(0, 0):
                cp.start()

        @pl.when(i + 1 < steps)
        def _():
            for cp in loads(i + 1, 1 - slot):
                cp.start()

        g = p_ref[0].astype(F32)
        for d in range(1, n_part):
            g = g + p_ref[d].astype(F32)
        for cp in loads(i, slot):
            cp.wait()
        delta, nm, nv = _adamw(in_buf[slot, 0], g, in_buf[slot, 1], in_buf[slot, 2])

        @pl.when(i > 0)
        def _():
            for cp in stores:
                cp.wait()

        for k, val in enumerate((g, delta, nm, nv)):
            out_buf[k] = val
        for cp in stores:
            cp.start()

        @pl.when(i == steps - 1)
        def _():
            for cp in stores:
                cp.wait()
            wait()

    return pl.pallas_call(
        body, name="adamw_w_in", grid=(steps,),
        in_specs=[pl.BlockSpec((n_part, r, 128), lambda i: (0, 0, i)), ANY, ANY, ANY, ANY],
        out_specs=[ANY] * 5,
        out_shape=[jax.ShapeDtypeStruct((r, 1, c), F32)] * 4 + [_exchange_shape(small, True)],
        scratch_shapes=[pltpu.VMEM((2, 3, r, 128), F32), pltpu.VMEM((4, r, 128), F32),
                        pltpu.SemaphoreType.DMA((2, 3)), pltpu.SemaphoreType.DMA((4,))] + _exchange_sems(1),
        compiler_params=_params(dimension_semantics=("arbitrary",)),
    )(parts, w, m, v, small)


def _adamw_small(got, me, metas, norms, finals, biases):
    def body(me_ref, gm_ref, gr_ref, *refs):
        ins, outs = refs[:12], refs[12:]
        g_meta, g_rest = gm_ref[0], gr_ref[0]
        for d in range(1, N_DEV):
            g_meta, g_rest = g_meta + gm_ref[d], g_rest + gr_ref[d]
        grads = [g_meta, g_rest[0:1], g_rest[1:2], g_rest[2:3, :FOX_HEADS]]
        for k, g in enumerate(grads):
            w_ref, m_ref, v_ref = ins[3 * k:3 * k + 3]
            delta, new_m, new_v = _adamw(w_ref[...], g, m_ref[...], v_ref[...])
            for o_ref, val in zip(outs[4 * k:4 * k + 4], (g, delta, new_m, new_v)):
                o_ref[...] = val
        outs[16][...] = g_rest[3:4, :128]

    groups = (metas, norms, finals, biases)
    full = lambda a: pl.BlockSpec(a.shape, lambda i, me_ref: (0,) * a.ndim)
    flat = [a for grp in groups for a in grp]
    res = pl.pallas_call(
        body, name="adamw_small",
        grid_spec=pltpu.PrefetchScalarGridSpec(
            num_scalar_prefetch=1, grid=(1,),
            in_specs=[pl.BlockSpec((N_DEV, N_META, META_BLK), lambda i, me_ref: (0, 0, me_ref[0])),
                      pl.BlockSpec((N_DEV, 8, D_MODEL), lambda i, me_ref: (0, N_META // 8, 0))]
            + [full(a) for a in flat],
            out_specs=[full(grp[0]) for grp in groups for _ in range(4)]
            + [pl.BlockSpec((1, 128), lambda i, me_ref: (0, 0))]),
        out_shape=[jax.ShapeDtypeStruct(grp[0].shape, F32) for grp in groups for _ in range(4)]
        + [jax.ShapeDtypeStruct((1, 128), F32)],
        compiler_params=_params(dimension_semantics=("arbitrary",)),
    )(me, got, got, *flat)
    return [res[4 * k:4 * k + 4] for k in range(4)], res[16]


def kernel(x, meta_tokens, norm_g, w_in, b_f, w_out, final_g, loss_target, m_meta_tokens, m_norm_g, m_w_in, m_b_f, m_w_out, m_final_g, v_meta_tokens, v_norm_g, v_w_in, v_b_f, v_w_out, v_final_g):
    core = lax.axis_index("c")
    me = 4 * lax.axis_index("x") + 2 * lax.axis_index("y") + core
    tables = _tables()

    wt_all, meta_all = _gather_two_level([_slab(w_in[0].T.astype(BF16), me), meta_tokens], name="gather_w_in")
    slabs, tail = _join_edges(wt_all)
    meta_full = jnp.transpose(meta_all, (1, 0, 2)).reshape(N_META, D_MODEL)
    h_pad = jnp.concatenate([jnp.zeros((PAD, D_MODEL), F32), meta_full, x[0]], axis=0)
    b_pad = jnp.pad(b_f, ((0, 0), (0, 128 - FOX_HEADS)))

    u, z = _rms_z_ret(h_pad, norm_g, slabs)
    z, zff = _z_fox(u, slabs, tail, z)
    bias = _forget_fwd(zff, b_pad)[:FOX_HEADS].reshape(FOX_HEADS, 1, T)
    raw, y = _ret_fwd(z, tables)
    a_f, lse, y, w_out_all = _fox_fwd(z, bias, y, w_out[0].astype(BF16))
    w_out_b = w_out_all.reshape(D_MIX, D_MODEL)
    dout, dout_b, dy, loss_blk, d_final_g = _out_loss_dy(y, w_out_b, h_pad, loss_target[0],
                                                         final_g.reshape(1, D_MODEL))

    d_w_out = _mm_tn(y, dout_b, tm=D_MIX, tn=256, name="mm_dwout")
    dz = _ret_bwd(z, tables, raw, dy)
    dz, dc, got_w_out = _fox_bwd(z, bias, a_f, lse, dy, dz, d_w_out.reshape(N_DEV, WO_BLK, D_MODEL))
    dz, db_f = _forget_bwd(zff, b_pad, dc.reshape(FOX_HEADS, T), dz)

    kept, via, direct = _dwin_pair_slabs(dz, u, core)
    dh, d_norm_g, summed = _du_rms(dz, slabs, tail, h_pad, dout, norm_g, kept, via)
    got_w_in = _unslab([kept[1], direct, summed], me)

    small = jnp.concatenate([
        dh[PAD:CHUNK], d_norm_g, d_final_g, jnp.pad(db_f[:, :FOX_HEADS], ((0, 0), (0, D_MODEL - FOX_HEADS))),
        jnp.pad(loss_blk[0:1], ((0, 0), (0, D_MODEL - 128))),
        jnp.zeros((SMALL_ROWS - N_META - 4, D_MODEL), F32)], axis=0)
    fore = lambda a: jnp.transpose(a, (2, 0, 1))
    g_w_in, d_w_in, nm_w_in, nv_w_in, got_small = _sum_adamw_w_in(
        got_w_in, fore(w_in), fore(m_w_in), fore(v_w_in), small)
    g_w_out, d_w_out, nm_w_out, nv_w_out = _sum_adamw(got_w_out, w_out[0], m_w_out[0], v_w_out[0], 128, "adamw_w_out")

    row = lambda a: a.reshape(1, D_MODEL)
    (meta_o, norm_o, final_o, bias_o), loss_row = _adamw_small(
        got_small, me.astype(jnp.int32).reshape(1),
        (meta_tokens, m_meta_tokens, v_meta_tokens), (norm_g, m_norm_g, v_norm_g),
        (row(final_g), row(m_final_g), row(v_final_g)), (b_f, m_b_f, v_b_f))
    final_o = [a.reshape(D_MODEL) for a in final_o]

    back = lambda a: jnp.transpose(a, (1, 2, 0))
    outs = [[meta_o[k], norm_o[k], back(wk), bias_o[k], ok[None], final_o[k]]
            for k, (wk, ok) in enumerate(zip((g_w_in, d_w_in, nm_w_in, nv_w_in),
                                             (g_w_out, d_w_out, nm_w_out, nv_w_out)))]
    return (loss_row[0, 0], dh[CHUNK:][None], *outs[0], *outs[1], *outs[2], *outs[3])
```

```python
import numpy as np
import jax
import jax.numpy as jnp
from jax import lax
from jax.experimental import pallas as pl
from jax.experimental.pallas import tpu as pltpu

F32 = jnp.float32
BF16 = jnp.bfloat16

N_DEV = 8
N_CHIP = 4
D_MODEL = 1024
SEQ = 2048
N_META = 16
CHUNK = 128
PAD = CHUNK - N_META
T = SEQ + CHUNK
NCHUNK = T // CHUNK
D_MIX = 2048
RET_HEADS = 4
RET_DK = 128
RET_DV = 256
RET_W = 2 * RET_DK + 2 * RET_DV
FOX_HEADS = 16
FOX_D = 64
FOX_PAIRS = FOX_HEADS // 2
FOX_W = 4 * 128
FOX_BASE = RET_HEADS * RET_W
FF_BASE = FOX_BASE + FOX_PAIRS * FOX_W
D_IN = 7184
D_IN_PAD = 7296
W_BLK = D_IN // N_DEV
WO_BLK = D_MIX // N_DEV
META_BLK = D_MODEL // N_DEV
EPS = 1e-6
NEG_INF = -1e30
ROPE_BASE = 10000.0
LOG2E = 1.4426950408889634
LN2 = 0.6931471805599453

ADAM_LR = 0.001
ADAM_B1 = 0.9
ADAM_B2 = 0.999
ADAM_EPS = 1e-08
ADAM_WD = 0.01
ADAM_STEP = 10

SMALL_ROWS = 24
VMEM_LIMIT = 56 * 1024 * 1024
MESH = pl.DeviceIdType.MESH
ANY = pl.BlockSpec(memory_space=pl.ANY)

_NT = (((1,), (1,)), ((), ()))
_TN = (((0,), (0,)), ((), ()))


def _dot(a, b):
    return jnp.dot(a, b, preferred_element_type=F32)


def _dot_nt(a, b):
    return lax.dot_general(a, b, _NT, preferred_element_type=F32)


def _dot_tn(a, b):
    return lax.dot_general(a, b, _TN, preferred_element_type=F32)


def _params(**kw):
    return pltpu.CompilerParams(vmem_limit_bytes=VMEM_LIMIT, **kw)


def _silu_parts(g):
    sig = jax.nn.sigmoid(g)
    return g * sig, sig * (1.0 + g * (1.0 - sig))


def _tables():
    pos = np.arange(T, dtype=np.float32) - PAD
    inv = (ROPE_BASE ** (-np.arange(0, RET_DK, 2, dtype=np.float32) / RET_DK)).astype(np.float32)
    ang = pos[:, None] * inv[None, :]
    cos, sin = np.cos(ang), np.sin(ang)
    cosf = np.concatenate([cos, cos], axis=1).astype(np.float32)
    sins = np.concatenate([-sin, sin], axis=1).astype(np.float32)
    h = np.arange(RET_HEADS, dtype=np.float32)
    log_gamma = np.log1p(-np.exp2(-5.0 - h)).astype(np.float32)
    idx = np.arange(CHUNK, dtype=np.float32)
    diff = idx[:, None] - idx[None, :]
    dmask = np.where(diff[None] >= 0,
                     np.exp(log_gamma[:, None, None] * np.maximum(diff, 0.0)[None]), 0.0)
    zeta = np.exp(log_gamma[:, None] * (CHUNK - 1.0 - idx)[None, :])
    xi = np.exp(log_gamma[:, None] * (idx + 1.0)[None, :])
    cdec = np.exp(log_gamma * CHUNK)
    return (jnp.asarray(cosf), jnp.asarray(sins), jnp.asarray(dmask, F32),
            jnp.asarray(zeta[:, :, None], F32), jnp.asarray(xi[:, :, None], F32),
            jnp.asarray(cdec[:, None, None], F32))


W_STRIDE = 896
W_SLAB = 912
W_EDGE = W_SLAB - W_STRIDE
def _slab(block, me):
    shift = W_BLK - W_STRIDE
    return lax.switch(me, [lambda b, d=d: jnp.pad(b, ((shift * d, W_SLAB - W_BLK - shift * d), (0, 0)))
                           for d in range(N_DEV)], block)


def _unslab(slabs, me):
    shift = W_BLK - W_STRIDE
    return lax.switch(me, [lambda *s, d=d: jnp.stack([a[shift * d:shift * d + W_BLK] for a in s])
                           for d in range(N_DEV)], *slabs)


def _join_edges(slabs):
    last = slabs[:, W_STRIDE:]
    first = slabs[:, :W_EDGE] + jnp.concatenate([jnp.zeros_like(last[:1]), last[:-1]], axis=0)
    tail = jnp.pad(last[N_DEV - 1], ((0, 128 - W_EDGE), (0, 0)))
    return lax.dynamic_update_slice(slabs, first, (0, 0, 0)), tail


def _mm_tn(a, b, *, tm, tn, name):
    k, m = a.shape
    n = b.shape[1]
    assert m % tm == 0 and n % tn == 0

    def body(a_ref, b_ref, o_ref):
        o_ref[...] = _dot_tn(a_ref[...], b_ref[...]).astype(BF16)

    return pl.pallas_call(
        body, name=name, grid=(n // tn, m // tm),
        in_specs=[pl.BlockSpec((k, tm), lambda j, i: (0, i)),
                  pl.BlockSpec((k, tn), lambda j, i: (0, j))],
        out_specs=pl.BlockSpec((tm, tn), lambda j, i: (i, j)),
        out_shape=jax.ShapeDtypeStruct((m, n), BF16),
        compiler_params=_params(dimension_semantics=("arbitrary", "arbitrary")),
    )(a, b)


def _piece_spec(base, mult):
    def index(i):
        p = base + mult * i
        return p // 7, p % 7, 0
    return pl.BlockSpec((1, 128, D_MODEL), index)


_RET_PIECES = ((0, 1), (4, 1), (8, 2), (9, 2), (16, 2), (17, 2))
_FOX_PIECES = ((24, 1), (32, 1), (40, 1), (48, 1))


def _rms_z_ret(h_pad, g, slabs):
    def body(h_ref, g_ref, *refs):
        pieces, u_ref, z_ref = refs[:6], refs[6], refs[7]

        @pl.when(pl.program_id(0) == 0)
        def _():
            h = h_ref[...]
            r = lax.rsqrt(jnp.mean(h * h, axis=-1, keepdims=True) + EPS)
            u_ref[...] = (h * r * g_ref[...]).astype(BF16)

        w = jnp.concatenate([p[0] for p in pieces], axis=0)
        z_ref[...] = _dot_nt(u_ref[...], w).astype(BF16)

    whole = pl.BlockSpec((T, D_MODEL), lambda i: (0, 0))
    return pl.pallas_call(
        body, name="rms_z_ret", grid=(RET_HEADS,),
        in_specs=[whole, pl.BlockSpec((1, D_MODEL), lambda i: (0, 0))]
        + [_piece_spec(*bm) for bm in _RET_PIECES],
        out_specs=[whole, pl.BlockSpec((T, RET_W), lambda i: (0, i))],
        out_shape=[jax.ShapeDtypeStruct((T, D_MODEL), BF16),
                   jax.ShapeDtypeStruct((T, D_IN_PAD), BF16)],
        compiler_params=_params(dimension_semantics=("arbitrary",)),
    )(h_pad, g, *([slabs] * 6))


def _z_fox(u, slabs, tail, z):
    def body(u_ref, *refs):
        pieces, t_ref, z_ref, zff_ref = refs[:4], refs[4], refs[6], refs[7]
        u = u_ref[...]
        w = jnp.concatenate([p[0] for p in pieces], axis=0)
        z_ref[...] = _dot_nt(u, w).astype(BF16)

        @pl.when(pl.program_id(0) == 0)
        def _():
            zff_ref[...] = _dot_nt(u, t_ref[...])

    return pl.pallas_call(
        body, name="z_fox", grid=(FOX_PAIRS,),
        in_specs=[pl.BlockSpec((T, D_MODEL), lambda i: (0, 0))] + [_piece_spec(*bm) for bm in _FOX_PIECES]
        + [pl.BlockSpec((128, D_MODEL), lambda i: (0, 0)), ANY],
        out_specs=[_FOX_Z_SPEC, pl.BlockSpec((T, 128), lambda i: (0, 0))],
        out_shape=[jax.ShapeDtypeStruct((T, D_IN_PAD), BF16),
                   jax.ShapeDtypeStruct((T, 128), F32)],
        input_output_aliases={6: 0},
        compiler_params=_params(dimension_semantics=("arbitrary",)),
    )(u, *([slabs] * 4), tail, z)


def _out_loss_dy(y, w_out_b, h_pad, target, g):
    tm = T // 4

    def body(y_ref, w_ref, h_ref, t_hbm, g_ref, d_ref, db_ref, dy_ref, loss_ref, dg_ref, t_buf, t_sem):
        i = pl.program_id(0)
        head = pltpu.make_async_copy(t_hbm.at[pl.ds(0, tm - CHUNK)], t_buf.at[pl.ds(CHUNK, tm - CHUNK)], t_sem)
        rest = pltpu.make_async_copy(t_hbm.at[pl.ds(pl.multiple_of(jnp.maximum(i, 1) * tm - CHUNK, 8), tm)],
                                     t_buf, t_sem)

        @pl.when(i == 0)
        def _():
            t_buf[pl.ds(0, CHUNK), :] = jnp.zeros((CHUNK, D_MODEL), F32)
            head.start()
            loss_ref[...] = jnp.zeros_like(loss_ref)
            dg_ref[...] = jnp.zeros_like(dg_ref)

        pl.when(i > 0)(rest.start)

        w = w_ref[...]
        o = _dot(y_ref[...], w) + h_ref[...]
        pl.when(i == 0)(head.wait)
        pl.when(i > 0)(rest.wait)
        token = lax.broadcasted_iota(jnp.int32, (tm, 1), 0) + i * tm >= CHUNK
        g = g_ref[...]
        r = lax.rsqrt(jnp.mean(o * o, axis=-1, keepdims=True) + EPS)
        xn = o * r
        e = jnp.where(token, xn * g - t_buf[...], 0.0)
        loss_ref[...] += jnp.full(loss_ref.shape, 0.5 / D_MODEL * jnp.sum(e * e), F32)
        do = e * (1.0 / D_MODEL)
        dg_ref[...] += jnp.sum(do * xn, axis=0, keepdims=True)
        dn = do * g
        d = r * (dn - xn * jnp.mean(dn * xn, axis=-1, keepdims=True))
        d_b = d.astype(BF16)
        d_ref[...] = d
        db_ref[...] = d_b
        dy_ref[...] = _dot_nt(d_b, w).astype(BF16)

    tile = pl.BlockSpec((tm, D_MODEL), lambda i: (i, 0))
    wide = pl.BlockSpec((tm, D_MIX), lambda i: (i, 0))
    return pl.pallas_call(
        body, name="out_loss_dy", grid=(T // tm,),
        in_specs=[wide, pl.BlockSpec((D_MIX, D_MODEL), lambda i: (0, 0)), tile, ANY,
                  pl.BlockSpec((1, D_MODEL), lambda i: (0, 0))],
        out_specs=[tile, tile, wide,
                   pl.BlockSpec((8, 128), lambda i: (0, 0)),
                   pl.BlockSpec((1, D_MODEL), lambda i: (0, 0))],
        out_shape=[jax.ShapeDtypeStruct((T, D_MODEL), F32),
                   jax.ShapeDtypeStruct((T, D_MODEL), BF16),
                   jax.ShapeDtypeStruct((T, D_MIX), BF16),
                   jax.ShapeDtypeStruct((8, 128), F32),
                   jax.ShapeDtypeStruct((1, D_MODEL), F32)],
        scratch_shapes=[pltpu.VMEM((tm, D_MODEL), F32), pltpu.SemaphoreType.DMA],
        compiler_params=_params(dimension_semantics=("arbitrary",)),
    )(y, w_out_b, h_pad, target, g)


def _coords():
    return lax.axis_index("x"), lax.axis_index("y"), lax.axis_index("c")


def _flip(v, bit):
    return 1 - v if bit else v


def _peer(x, y, c, r):
    return _flip(x, (r >> 2) & 1), _flip(y, (r >> 1) & 1), _flip(c, r & 1)


def _direct_exchange(ins, outs, send_sems, recv_sems, local_sems, gather):
    x, y, c = _coords()
    me = 4 * x + 2 * y + c

    def src(k, to_idx):
        return ins[k] if gather else ins[k].at[to_idx]

    local = [pltpu.make_async_copy(src(k, me), outs[k].at[me], local_sems.at[k])
             for k in range(len(ins))]
    sends, recvs = [], []
    for r in range(1, N_DEV):
        px, py, pc = _peer(x, y, c, r)
        peer = 4 * px + 2 * py + pc
        for k in range(len(ins)):
            sems = dict(send_sem=send_sems.at[k, r - 1], recv_sem=recv_sems.at[k, r - 1],
                        device_id=(px, py, pc), device_id_type=MESH)
            sends.append(pltpu.make_async_remote_copy(src_ref=src(k, peer), dst_ref=outs[k].at[me], **sems))
            recvs.append(pltpu.make_async_remote_copy(src_ref=src(k, peer), dst_ref=outs[k].at[peer], **sems))

    def start():
        for cp in local + sends:
            cp.start()

    def wait():
        for cp in recvs:
            cp.wait_recv()
        for cp in sends:
            cp.wait_send()
        for cp in local:
            cp.wait()

    return start, wait


def _exchange_sems(n_arr):
    return [pltpu.SemaphoreType.DMA((n_arr, N_DEV - 1)), pltpu.SemaphoreType.DMA((n_arr, N_DEV - 1)),
            pltpu.SemaphoreType.DMA((n_arr,))]


def _exchange_shape(a, gather):
    return jax.ShapeDtypeStruct(((N_DEV,) + a.shape) if gather else a.shape, a.dtype)


def _gather_two_level(arrays, name):
    n_arr = len(arrays)

    def body(*refs):
        ins, outs = refs[:n_arr], refs[n_arr:2 * n_arr]
        send_sems, recv_sems, local_sems = refs[2 * n_arr:]
        x, y, c = _coords()

        def slot(k, px, py, pc):
            return outs[k].at[4 * px + 2 * py + pc]

        def routed(core):
            me, sibling = (x, y, core), (x, y, 1 - core)
            xn, yn, dg = (1 - x, y), (x, 1 - y), (1 - x, 1 - y)
            (first, s_first), (second, s_second) = ((xn, 1), (yn, 2)) if core == 0 else ((yn, 2), (xn, 1))

            def copy(k, j, block, to, own=False):
                return pltpu.make_async_remote_copy(
                    src_ref=ins[k] if own else slot(k, *block), dst_ref=slot(k, *block),
                    send_sem=send_sems.at[k, j], recv_sem=recv_sems.at[k, j],
                    device_id=to, device_id_type=MESH)

            local = [pltpu.make_async_copy(ins[k], slot(k, *me), local_sems.at[k]) for k in range(n_arr)]
            sent = []
            for k in range(n_arr):
                sent += [copy(k, 0, me, sibling, True), copy(k, 1, me, (*xn, core), True),
                         copy(k, 2, me, (*yn, core), True)]
            for cp in local + sent:
                cp.start()

            def pass_on(k, j_from, j_to, block, targets):
                copy(k, j_from, block, me).wait_recv()
                for j, to in zip(j_to, targets):
                    cp = copy(k, j, block, to)
                    cp.start()
                    sent.append(cp)

            for k in range(n_arr):
                pass_on(k, s_first, (3, 3 + s_first), (*first, core), ((*second, core), sibling))
            for k in range(n_arr):
                pass_on(k, s_second, (3 + s_second,), (*second, core), (sibling,))
            for k in range(n_arr):
                pass_on(k, 3, (6,), (*dg, core), (sibling,))
            for k in range(n_arr):
                copy(k, 0, sibling, me).wait_recv()
                for j, chip in ((4, xn), (5, yn), (6, dg)):
                    copy(k, j, (*chip, 1 - core), me).wait_recv()
            for cp in sent:
                cp.wait_send()
            for cp in local:
                cp.wait()

        for core in (0, 1):
            pl.when(c == core)(lambda core=core: routed(core))

    return pl.pallas_call(
        body, name=name,
        in_specs=[ANY] * n_arr, out_specs=[ANY] * n_arr,
        out_shape=[_exchange_shape(a, True) for a in arrays],
        scratch_shapes=_exchange_sems(n_arr),
    )(*arrays)


def _piece_columns():
    pos = {}
    for h in range(RET_HEADS):
        for k, p in enumerate((h, 4 + h, 8 + 2 * h, 9 + 2 * h, 16 + 2 * h, 17 + 2 * h)):
            pos[p] = 6 * h + k
    for p in range(FOX_PAIRS):
        for i in range(4):
            pos[24 + 8 * i + p] = 24 + 4 * p + i
    pos[D_IN_PAD // 128 - 1] = D_IN_PAD // 128 - 1
    return np.array([pos[7 * d + j] for d in range(N_DEV) for j in range(8)], np.int32)


def _routes(core):
    x, y, _ = _coords()
    return ((1 - x, y), (x, 1 - y)) if core == 0 else ((x, 1 - y), (1 - x, y))


def _dwin_pair_slabs(dz, u, core):
    def body(order_ref, cols_ref, *refs):
        pieces, u_ref = refs[:8], refs[8]
        kept_ref, theirs_ref, via_ref, direct_ref = refs[9:13]
        send_buf, keep_buf, got_buf, push_send, push_recv, hop_send, hop_recv, load_sem = refs[13:]
        s = pl.program_id(0)
        x, y, c = _coords()
        cols = jnp.concatenate([p[...] for p in pieces[:7]] + [pieces[7][:, :W_EDGE]], axis=1)
        slab = _dot_tn(cols, u_ref[...])

        def push(k):
            return pltpu.make_async_remote_copy(
                src_ref=send_buf.at[k], dst_ref=theirs_ref.at[k],
                send_sem=push_send.at[k], recv_sem=push_recv.at[k],
                device_id=(x, y, 1 - c), device_id_type=MESH)

        def hop(k, core):
            first, _ = _routes(core)
            return pltpu.make_async_remote_copy(
                src_ref=keep_buf.at[k], dst_ref=direct_ref if k == 0 else via_ref,
                send_sem=hop_send.at[k], recv_sem=hop_recv.at[k],
                device_id=(*first, core), device_id_type=MESH)

        for k in range(N_CHIP):
            @pl.when(s == 2 * k)
            def _(k=k):
                send_buf[k] = slab.astype(BF16)
                push(k).start()

            @pl.when(s == 2 * k + 1)
            def _(k=k):
                push(k).wait_recv()
                load = pltpu.make_async_copy(theirs_ref.at[k], got_buf, load_sem)
                load.start()
                load.wait()
                total = (slab + got_buf[...].astype(F32)).astype(BF16)
                if k < 2:
                    keep_buf[k] = total
                    for core in (0, 1):
                        pl.when(c == core)(lambda core=core: hop(k, core).start())
                else:
                    kept_ref[0] = total

        @pl.when(s == 2 * N_CHIP - 1)
        def _():
            for core in (0, 1):
                @pl.when(c == core)
                def _(core=core):
                    for k in range(2):
                        hop(k, core).wait_recv()
                        hop(k, core).wait_send()
            for k in range(N_CHIP):
                push(k).wait_send()

    x, y = lax.axis_index("x"), lax.axis_index("y")
    xn, yn, dg, own = 2 * (1 - x) + y, 2 * x + 1 - y, 2 * (1 - x) + 1 - y, 2 * x + y
    mine = jnp.where(core == 0, jnp.stack([xn, dg, yn, own]), jnp.stack([yn, dg, xn, own]))
    sibs = jnp.where(core == 0, jnp.stack([yn, dg, xn, own]), jnp.stack([xn, dg, yn, own]))
    order = jnp.stack([2 * sibs + (1 - core), 2 * mine + core], axis=1).reshape(2 * N_CHIP).astype(jnp.int32)
    piece = lambda j: pl.BlockSpec((T, 128), lambda s, order_ref, cols_ref: (0, cols_ref[order_ref[s] * 8 + j]))
    slab = jax.ShapeDtypeStruct((W_SLAB, D_MODEL), BF16)
    kept, _, via, direct = pl.pallas_call(
        body, name="dwin_pair_slabs",
        grid_spec=pltpu.PrefetchScalarGridSpec(
            num_scalar_prefetch=2, grid=(2 * N_CHIP,),
            in_specs=[piece(j) for j in range(8)]
            + [pl.BlockSpec((T, D_MODEL), lambda s, order_ref, cols_ref: (0, 0))],
            out_specs=[pl.BlockSpec((1, W_SLAB, D_MODEL),
                                    lambda s, order_ref, cols_ref: (jnp.where(s >= 6, 1, 0), 0, 0)),
                       ANY, ANY, ANY],
            scratch_shapes=[pltpu.VMEM((N_CHIP, W_SLAB, D_MODEL), BF16), pltpu.VMEM((2, W_SLAB, D_MODEL), BF16),
                            pltpu.VMEM((W_SLAB, D_MODEL), BF16),
                            pltpu.SemaphoreType.DMA((N_CHIP,)), pltpu.SemaphoreType.DMA((N_CHIP,)),
                            pltpu.SemaphoreType.DMA((2,)), pltpu.SemaphoreType.DMA((2,)),
                            pltpu.SemaphoreType.DMA]),
        out_shape=[jax.ShapeDtypeStruct((2, W_SLAB, D_MODEL), BF16),
                   jax.ShapeDtypeStruct((N_CHIP, W_SLAB, D_MODEL), BF16), slab, slab],
        compiler_params=_params(dimension_semantics=("arbitrary",)),
    )(order, jnp.asarray(_piece_columns()), *([dz] * 8), u)
    return kept, via, direct


def _du_rms(dz, slabs, tail, h_pad, dout, g, kept, via):
    tm = 272
    steps = T // tm
    columns = _piece_columns().reshape(N_DEV, 8)

    def body(dz_ref, w_ref, t_ref, h_ref, d_ref, g_ref, kept_ref, via_ref, dh_ref, dg_ref, got_ref,
             via_buf, mine_buf, send_sem, recv_sem, local_sems):
        i = pl.program_id(0)
        x, y, c = _coords()

        def summed(core):
            _, second = _routes(core)
            return pltpu.make_async_remote_copy(
                src_ref=mine_buf, dst_ref=got_ref, send_sem=send_sem, recv_sem=recv_sem,
                device_id=(*second, core), device_id_type=MESH)

        @pl.when(i == 0)
        def _():
            loads = [pltpu.make_async_copy(kept_ref.at[0], mine_buf, local_sems.at[0]),
                     pltpu.make_async_copy(via_ref, via_buf, local_sems.at[1])]
            for cp in loads:
                cp.start()
            for cp in loads:
                cp.wait()
            mine_buf[...] = (mine_buf[...].astype(F32) + via_buf[...].astype(F32)).astype(BF16)
            for core in (0, 1):
                pl.when(c == core)(lambda core=core: summed(core).start())
            dg_ref[...] = jnp.zeros_like(dg_ref)

        du = _dot(dz_ref[:, pl.ds(FF_BASE, 128)], t_ref[...])
        for d in range(N_DEV):
            cols = jnp.concatenate([dz_ref[:, pl.ds(128 * int(columns[d, j]), 128)] for j in range(7)], axis=1)
            du = du + _dot(cols, w_ref[d, pl.ds(0, W_STRIDE), :])
        h = h_ref[...]
        r = lax.rsqrt(jnp.mean(h * h, axis=-1, keepdims=True) + EPS)
        xn = h * r
        dg_ref[...] += jnp.sum(du * xn, axis=0, keepdims=True)
        dn = du * g_ref[...]
        dh_ref[...] = d_ref[...] + r * (dn - xn * jnp.mean(dn * xn, axis=-1, keepdims=True))

        for core in (0, 1):
            @pl.when(jnp.logical_and(c == core, i == steps - 1))
            def _(core=core):
                summed(core).wait_recv()
                summed(core).wait_send()

    tile = pl.BlockSpec((tm, D_MODEL), lambda i: (i, 0))
    return pl.pallas_call(
        body, name="du_rms", grid=(steps,),
        in_specs=[pl.BlockSpec((tm, D_IN_PAD), lambda i: (i, 0)),
                  pl.BlockSpec((N_DEV, W_SLAB, D_MODEL), lambda i: (0, 0, 0)),
                  pl.BlockSpec((128, D_MODEL), lambda i: (0, 0)),
                  tile, tile, pl.BlockSpec((1, D_MODEL), lambda i: (0, 0)), ANY, ANY],
        out_specs=[tile, pl.BlockSpec((1, D_MODEL), lambda i: (0, 0)), ANY],
        out_shape=[jax.ShapeDtypeStruct((T, D_MODEL), F32),
                   jax.ShapeDtypeStruct((1, D_MODEL), F32),
                   jax.ShapeDtypeStruct(via.shape, via.dtype)],
        scratch_shapes=[pltpu.VMEM(via.shape, via.dtype), pltpu.VMEM(via.shape, via.dtype),
                        pltpu.SemaphoreType.DMA, pltpu.SemaphoreType.DMA, pltpu.SemaphoreType.DMA((2,))],
        compiler_params=_params(dimension_semantics=("arbitrary",)),
    )(dz, slabs, tail, h_pad, dout, g, kept, via)


def _tri(lower):
    r = lax.broadcasted_iota(jnp.int32, (CHUNK, CHUNK), 0)
    c = lax.broadcasted_iota(jnp.int32, (CHUNK, CHUNK), 1)
    return jnp.where((r >= c) if lower else (r <= c), 1.0, 0.0).astype(F32)


def _row_valid(n):
    r = lax.broadcasted_iota(jnp.int32, (CHUNK, 128), 0) + n * CHUNK
    return r >= PAD


_FF_SPEC = pl.BlockSpec((T, 128), lambda i: (0, FF_BASE // 128))
_ZFF_SPEC = pl.BlockSpec((T, 128), lambda i: (0, 0))


def _forget_fwd(z, b_pad):
    def body(z_ref, b_ref, o_ref):
        tri = _tri(True)
        carry = jnp.zeros((1, 128), F32)
        for n in range(NCHUNK):
            rows = pl.ds(n * CHUNK, CHUNK)
            a = z_ref[rows, :] + b_ref[...]
            lf = -(jnp.maximum(-a, 0.0) + jnp.log(1.0 + jnp.exp(-jnp.abs(a))))
            lf = jnp.where(_row_valid(n), lf, 0.0)
            c = jnp.dot(tri, lf, precision=lax.Precision.HIGHEST,
                        preferred_element_type=F32) + carry
            carry = c[CHUNK - 1:CHUNK, :]
            o_ref[:, rows] = jnp.where(_row_valid(n), c * (-LOG2E), NEG_INF).T

    return pl.pallas_call(
        body, name="forget_fwd", grid=(1,),
        in_specs=[_ZFF_SPEC, pl.BlockSpec((1, 128), lambda i: (0, 0))],
        out_specs=pl.BlockSpec((128, T), lambda i: (0, 0)),
        out_shape=jax.ShapeDtypeStruct((128, T), F32),
        compiler_params=_params(dimension_semantics=("arbitrary",)),
    )(z, b_pad)


def _forget_bwd(z, b_pad, dc, dz):
    def body(z_ref, b_ref, dc_ref, dz_in, dff_ref, db_ref):
        tri = _tri(False)
        carry = jnp.zeros((1, 128), F32)
        db = jnp.zeros((1, 128), F32)
        for n in reversed(range(NCHUNK)):
            rows = pl.ds(n * CHUNK, CHUNK)
            dc_blk = jnp.concatenate([dc_ref[:, rows], jnp.zeros((128 - FOX_HEADS, CHUNK), F32)], axis=0).T
            dlf = jnp.dot(tri, dc_blk, precision=lax.Precision.HIGHEST,
                          preferred_element_type=F32) + carry
            carry = dlf[0:1, :]
            a = z_ref[rows, :] + b_ref[...]
            dff = jnp.where(_row_valid(n), dlf * jax.nn.sigmoid(-a), 0.0)
            dff_ref[rows, :] = dff.astype(BF16)
            db = db + jnp.sum(dff, axis=0, keepdims=True)
        db_ref[...] = db

    return pl.pallas_call(
        body, name="forget_bwd", grid=(1,),
        in_specs=[_ZFF_SPEC, pl.BlockSpec((1, 128), lambda i: (0, 0)),
                  pl.BlockSpec((FOX_HEADS, T), lambda i: (0, 0)), ANY],
        out_specs=[_FF_SPEC, pl.BlockSpec((1, 128), lambda i: (0, 0))],
        out_shape=[jax.ShapeDtypeStruct((T, D_IN_PAD), BF16),
                   jax.ShapeDtypeStruct((1, 128), F32)],
        input_output_aliases={3: 0},
        compiler_params=_params(dimension_semantics=("arbitrary",)),
    )(z, b_pad, dc, dz)


FOX_QB = 512
FOX_NQB = SEQ // FOX_QB


def _fox_block(b):
    lo = CHUNK + b * FOX_QB
    return pl.ds(lo, FOX_QB), lo, lo + FOX_QB


def _causal_bias():
    r = lax.broadcasted_iota(jnp.int32, (FOX_QB, FOX_QB), 0)
    c = lax.broadcasted_iota(jnp.int32, (FOX_QB, FOX_QB), 1)
    return jnp.where(c <= r, 0.0, NEG_INF).astype(F32)


def _fox_logits(q_blk, k_all, bias, causal, b):
    _, lo, hi = _fox_block(b)
    s_off = _dot_nt(q_blk, k_all[:lo]) + bias[:, :lo]
    s_dia = _dot_nt(q_blk, k_all[lo:hi]) + (bias[:, lo:hi] + causal)
    return s_off, s_dia


_FOX_Z_SPEC = pl.BlockSpec((T, FOX_W), lambda p: (0, FOX_BASE // FOX_W + p))
_FOX_BIAS_SPEC = pl.BlockSpec((2, 1, T), lambda p: (p, 0, 0))
_FOX_LSE_SPEC = pl.BlockSpec((2, T, 1), lambda p: (p, 0, 0))
_FOX_SCALE = FOX_D ** -0.5
_FOX_QSCALE = _FOX_SCALE * LOG2E


def _fox_fwd(z, bias, y, w_out_blk):
    last = FOX_PAIRS - 1

    def body(z_ref, b_ref, y_in, w_ref, a_ref, lse_ref, y_ref, wall_ref,
             send_sems, recv_sems, local_sems):
        start, wait = _direct_exchange([w_ref], [wall_ref], send_sems, recv_sems, local_sems, True)
        pl.when(pl.program_id(0) == 0)(start)

        causal = _causal_bias()
        a_ref[pl.ds(0, CHUNK), :] = jnp.zeros((CHUNK, 128), F32)
        y_ref[pl.ds(0, CHUNK), :] = jnp.zeros((CHUNK, 128), BF16)
        for j in range(2):
            lanes = pl.ds(j * FOX_D, FOX_D)
            k_all = z_ref[:, pl.ds(128 + j * FOX_D, FOX_D)]
            v_all = z_ref[:, pl.ds(256 + j * FOX_D, FOX_D)]
            bias = b_ref[j]
            lse_ref[j, pl.ds(0, CHUNK), :] = jnp.zeros((CHUNK, 1), F32)
            for b in range(FOX_NQB):
                rows, lo, hi = _fox_block(b)
                q_blk = (z_ref[rows, lanes].astype(F32) * _FOX_QSCALE).astype(BF16)
                s_off, s_dia = _fox_logits(q_blk, k_all, bias, causal, b)
                m = jnp.maximum(jnp.max(s_off, axis=-1, keepdims=True),
                                jnp.max(s_dia, axis=-1, keepdims=True))
                e_off = jnp.exp2(s_off - m)
                e_dia = jnp.exp2(s_dia - m)
                total = jnp.sum(e_off, axis=-1, keepdims=True) + jnp.sum(e_dia, axis=-1, keepdims=True)
                o = (_dot(e_off.astype(BF16), v_all[:lo]) + _dot(e_dia.astype(BF16), v_all[lo:hi])) / total
                a_ref[rows, lanes] = o
                lse_ref[j, rows, :] = m + jnp.log(total) * LOG2E
                gate = _silu_parts(z_ref[rows, pl.ds(384 + j * FOX_D, FOX_D)].astype(F32))[0]
                y_ref[rows, lanes] = (o * gate).astype(BF16)

        pl.when(pl.program_id(0) == last)(wait)

    return pl.pallas_call(
        body, name="fox_fwd", grid=(FOX_PAIRS,),
        in_specs=[_FOX_Z_SPEC, _FOX_BIAS_SPEC, ANY, ANY],
        out_specs=[pl.BlockSpec((T, 128), lambda p: (0, p)), _FOX_LSE_SPEC,
                   pl.BlockSpec((T, 128), lambda p: (0, 8 + p)), ANY],
        out_shape=[jax.ShapeDtypeStruct((T, FOX_HEADS * FOX_D), F32),
                   jax.ShapeDtypeStruct((FOX_HEADS, T, 1), F32),
                   jax.ShapeDtypeStruct((T, D_MIX), BF16),
                   _exchange_shape(w_out_blk, True)],
        input_output_aliases={2: 2},
        scratch_shapes=_exchange_sems(1),
        compiler_params=_params(dimension_semantics=("arbitrary",)),
    )(z, bias, y, w_out_blk)


def _fox_bwd(z, bias, a_f, lse, dy, dz, dwo_blocks):
    last = FOX_PAIRS - 1

    def body(z_ref, b_ref, a_ref, lse_ref, dy_ref, dz_in, dwo_ref, dz_ref, dc_ref, got_ref,
             kv_acc, dc_acc, send_sems, recv_sems, local_sems):
        start, wait = _direct_exchange([dwo_ref], [got_ref], send_sems, recv_sems, local_sems, False)
        pl.when(pl.program_id(0) == 0)(start)

        causal = _causal_bias()
        dz_ref[pl.ds(0, CHUNK), pl.ds(0, 128)] = jnp.zeros((CHUNK, 128), BF16)
        dz_ref[pl.ds(0, CHUNK), pl.ds(384, 128)] = jnp.zeros((CHUNK, 128), BF16)
        dk_rows, dv_rows = pl.ds(0, FOX_D), pl.ds(FOX_D, FOX_D)
        for j in range(2):
            lanes = pl.ds(j * FOX_D, FOX_D)
            k_all = z_ref[:, pl.ds(128 + j * FOX_D, FOX_D)]
            v_all = z_ref[:, pl.ds(256 + j * FOX_D, FOX_D)]
            bias = b_ref[j]
            kv_acc[...] = jnp.zeros_like(kv_acc)
            dc_acc[...] = jnp.zeros_like(dc_acc)
            for b in range(FOX_NQB):
                rows, lo, hi = _fox_block(b)
                off, dia = pl.ds(0, lo), pl.ds(lo, FOX_QB)
                q_blk = (z_ref[rows, lanes].astype(F32) * _FOX_QSCALE).astype(BF16)
                s_off, s_dia = _fox_logits(q_blk, k_all, bias, causal, b)
                lse_blk = lse_ref[j, rows, :]
                p_off, p_dia = jnp.exp2(s_off - lse_blk), jnp.exp2(s_dia - lse_blk)
                sg, dsg = _silu_parts(z_ref[rows, pl.ds(384 + j * FOX_D, FOX_D)].astype(F32))
                dyj = dy_ref[rows, lanes].astype(F32)
                dz_ref[rows, pl.ds(384 + j * FOX_D, FOX_D)] = (dyj * a_ref[rows, lanes] * dsg).astype(BF16)
                do_b = (dyj * sg).astype(BF16)
                dp_off = _dot_nt(do_b, v_all[:lo])
                dp_dia = _dot_nt(do_b, v_all[lo:hi])
                d = (jnp.sum(p_off * dp_off, axis=-1, keepdims=True)
                     + jnp.sum(p_dia * dp_dia, axis=-1, keepdims=True))
                ds_off = p_off * (dp_off - d)
                ds_dia = p_dia * (dp_dia - d)
                dc_acc[:, off] -= jnp.sum(ds_off, axis=0, keepdims=True)
                dc_acc[:, dia] -= jnp.sum(ds_dia, axis=0, keepdims=True)
                ds_off_b, ds_dia_b = ds_off.astype(BF16), ds_dia.astype(BF16)
                dq = _dot(ds_off_b, k_all[:lo]) + _dot(ds_dia_b, k_all[lo:hi])
                dz_ref[rows, lanes] = (dq * _FOX_SCALE).astype(BF16)
                kv_acc[dk_rows, off] += _dot_tn(q_blk, ds_off_b)
                kv_acc[dk_rows, dia] += _dot_tn(q_blk, ds_dia_b)
                kv_acc[dv_rows, off] += _dot_tn(do_b, p_off.astype(BF16))
                kv_acc[dv_rows, dia] += _dot_tn(do_b, p_dia.astype(BF16))
            for n in range(NCHUNK):
                rows = pl.ds(n * CHUNK, CHUNK)
                both = kv_acc[:, rows].T
                dz_ref[rows, pl.ds(128 + j * FOX_D, FOX_D)] = (both[:, :FOX_D] * LN2).astype(BF16)
                dz_ref[rows, pl.ds(256 + j * FOX_D, FOX_D)] = both[:, FOX_D:].astype(BF16)
            dc_ref[j] = dc_acc[...]

        pl.when(pl.program_id(0) == last)(wait)

    col = lambda base: pl.BlockSpec((T, 128), lambda p: (0, base + p))
    return pl.pallas_call(
        body, name="fox_bwd", grid=(FOX_PAIRS,),
        in_specs=[_FOX_Z_SPEC, _FOX_BIAS_SPEC, col(0), _FOX_LSE_SPEC, col(8), ANY, ANY],
        out_specs=[_FOX_Z_SPEC, _FOX_BIAS_SPEC, ANY],
        out_shape=[jax.ShapeDtypeStruct((T, D_IN_PAD), BF16),
                   jax.ShapeDtypeStruct((FOX_HEADS, 1, T), F32),
                   _exchange_shape(dwo_blocks, False)],
        input_output_aliases={5: 0},
        scratch_shapes=[pltpu.VMEM((2 * FOX_D, T), F32), pltpu.VMEM((1, T), F32)] + _exchange_sems(1),
        compiler_params=_params(dimension_semantics=("arbitrary",)),
    )(z, bias, a_f, lse, dy, dz, dwo_blocks)


def _rot(x, cosf, sins):
    return x * cosf + pltpu.roll(x, RET_DK // 2, 1) * sins


def _rot_t(d, cosf, sins):
    return d * cosf - pltpu.roll(d, RET_DK // 2, 1) * sins


_RET_Z_SPEC = pl.BlockSpec((T, RET_W), lambda h: (0, h))
_RET_TABLE_SPECS = [
    pl.BlockSpec((T, RET_DK), lambda h: (0, 0)),
    pl.BlockSpec((T, RET_DK), lambda h: (0, 0)),
    pl.BlockSpec((1, CHUNK, CHUNK), lambda h: (h, 0, 0)),
    pl.BlockSpec((1, CHUNK, 1), lambda h: (h, 0, 0)),
    pl.BlockSpec((1, CHUNK, 1), lambda h: (h, 0, 0)),
    pl.BlockSpec((1, 1, 1), lambda h: (h, 0, 0)),
]
_RQ, _RK = pl.ds(0, RET_DK), pl.ds(RET_DK, RET_DK)
_RV, _RG = pl.ds(2 * RET_DK, RET_DV), pl.ds(2 * RET_DK + RET_DV, RET_DV)
_RET_KSCALE = RET_DK ** -0.5


def _ret_fwd(z, tables):
    def body(z_ref, cos_ref, sin_ref, dm_ref, zeta_ref, xi_ref, cd_ref, raw_ref, y_ref):
        dmask, zeta, xi, cdec = dm_ref[0], zeta_ref[0], xi_ref[0], cd_ref[0]
        state = jnp.zeros((RET_DK, RET_DV), F32)
        for n in range(NCHUNK):
            rows = pl.ds(n * CHUNK, CHUNK)
            cosf, sins = cos_ref[rows, :], sin_ref[rows, :]
            qr = _rot(z_ref[rows, _RQ].astype(F32), cosf, sins)
            kr_b = (_rot(z_ref[rows, _RK].astype(F32), cosf, sins) * _RET_KSCALE).astype(BF16)
            v_b = z_ref[rows, _RV]
            a = _dot_nt(qr.astype(BF16), kr_b) * dmask
            out = _dot(a.astype(BF16), v_b) + _dot((qr * xi).astype(BF16), state.astype(BF16))
            state = state * cdec + _dot_tn(kr_b, (v_b.astype(F32) * zeta).astype(BF16))
            raw_ref[rows, :] = out
            r = lax.rsqrt(jnp.mean(out * out, axis=-1, keepdims=True) + EPS)
            y_ref[rows, :] = (out * r * _silu_parts(z_ref[rows, _RG].astype(F32))[0]).astype(BF16)

    wide = pl.BlockSpec((T, RET_DV), lambda h: (0, h))
    return pl.pallas_call(
        body, name="ret_fwd", grid=(RET_HEADS,),
        in_specs=[_RET_Z_SPEC] + _RET_TABLE_SPECS,
        out_specs=[wide, wide],
        out_shape=[jax.ShapeDtypeStruct((T, RET_HEADS * RET_DV), F32),
                   jax.ShapeDtypeStruct((T, D_MIX), BF16)],
        compiler_params=_params(dimension_semantics=("arbitrary",)),
    )(z, *tables)


def _ret_bwd(z, tables, raw, dy):
    def body(z_ref, cos_ref, sin_ref, dm_ref, zeta_ref, xi_ref, cd_ref, raw_ref, dy_ref,
             dz_ref, st_ref):
        dmask, zeta, xi, cdec = dm_ref[0], zeta_ref[0], xi_ref[0], cd_ref[0]

        def rotated(n):
            rows = pl.ds(n * CHUNK, CHUNK)
            cosf, sins = cos_ref[rows, :], sin_ref[rows, :]
            qr = _rot(z_ref[rows, _RQ].astype(F32), cosf, sins)
            kr_b = (_rot(z_ref[rows, _RK].astype(F32), cosf, sins) * _RET_KSCALE).astype(BF16)
            return rows, cosf, sins, qr, kr_b

        state = jnp.zeros((RET_DK, RET_DV), F32)
        for n in range(NCHUNK):
            st_ref[n] = state.astype(BF16)
            if n + 1 < NCHUNK:
                rows, _, _, _, kr_b = rotated(n)
                state = state * cdec + _dot_tn(kr_b, (z_ref[rows, _RV].astype(F32) * zeta).astype(BF16))

        grad_state = jnp.zeros((RET_DK, RET_DV), F32)
        for n in reversed(range(NCHUNK)):
            rows, cosf, sins, qr, kr_b = rotated(n)
            qr_b = qr.astype(BF16)
            v_b = z_ref[rows, _RV]
            gs_b = grad_state.astype(BF16)
            o = raw_ref[rows, :]
            r = lax.rsqrt(jnp.mean(o * o, axis=-1, keepdims=True) + EPS)
            hn = o * r
            sg, dsg = _silu_parts(z_ref[rows, _RG].astype(F32))
            dyn = dy_ref[rows, :].astype(F32)
            dz_ref[rows, _RG] = (dyn * hn * dsg).astype(BF16)
            dhn = dyn * sg
            do_b = (r * (dhn - hn * jnp.mean(dhn * hn, axis=-1, keepdims=True))).astype(BF16)
            a_b = (_dot_nt(qr_b, kr_b) * dmask).astype(BF16)
            da_b = (_dot_nt(do_b, v_b) * dmask).astype(BF16)
            dqr = _dot(da_b, kr_b) + xi * _dot_nt(do_b, st_ref[n])
            dkr = _dot_tn(da_b, qr_b) + zeta * _dot_nt(v_b, gs_b)
            dv = _dot_tn(a_b, do_b) + zeta * _dot(kr_b, gs_b)
            grad_state = grad_state * cdec + _dot_tn((qr * xi).astype(BF16), do_b)
            dz_ref[rows, _RQ] = _rot_t(dqr, cosf, sins).astype(BF16)
            dz_ref[rows, _RK] = (_rot_t(dkr, cosf, sins) * _RET_KSCALE).astype(BF16)
            dz_ref[rows, _RV] = dv.astype(BF16)

    wide = pl.BlockSpec((T, RET_DV), lambda h: (0, h))
    return pl.pallas_call(
        body, name="ret_bwd", grid=(RET_HEADS,),
        in_specs=[_RET_Z_SPEC] + _RET_TABLE_SPECS + [wide, wide],
        out_specs=_RET_Z_SPEC,
        out_shape=jax.ShapeDtypeStruct((T, D_IN_PAD), BF16),
        scratch_shapes=[pltpu.VMEM((NCHUNK, RET_DK, RET_DV), BF16)],
        compiler_params=_params(dimension_semantics=("arbitrary",)),
    )(z, *tables, raw, dy)


def _adamw(w, g, m, v):
    m = ADAM_B1 * m + (1.0 - ADAM_B1) * g
    v = ADAM_B2 * v + (1.0 - ADAM_B2) * (g * g)
    m_hat = m / (1.0 - ADAM_B1 ** ADAM_STEP)
    v_hat = v / (1.0 - ADAM_B2 ** ADAM_STEP)
    delta = -ADAM_LR * (m_hat / (jnp.sqrt(v_hat) + ADAM_EPS) + ADAM_WD * w)
    return delta, m, v


def _sum_adamw(parts, w, m, v, rows, name):
    _, r_tot, cols = parts.shape
    assert r_tot % rows == 0

    def body(p_ref, w_ref, m_ref, v_ref, g_ref, d_ref, nm_ref, nv_ref):
        g = p_ref[0].astype(F32)
        for d in range(1, N_DEV):
            g = g + p_ref[d].astype(F32)
        delta, nm, nv = _adamw(w_ref[...], g, m_ref[...], v_ref[...])
        g_ref[...] = g
        d_ref[...] = delta
        nm_ref[...] = nm
        nv_ref[...] = nv

    blk = pl.BlockSpec((rows, cols), lambda i: (i, 0))
    return pl.pallas_call(
        body, name=name, grid=(r_tot // rows,),
        in_specs=[pl.BlockSpec((N_DEV, rows, cols), lambda i: (0, i, 0)), blk, blk, blk],
        out_specs=[blk] * 4,
        out_shape=[jax.ShapeDtypeStruct((r_tot, cols), F32)] * 4,
        compiler_params=_params(dimension_semantics=("arbitrary",)),
    )(parts, w, m, v)


def _sum_adamw_w_in(parts, w, m, v, small):
    n_part, r, c = parts.shape
    steps = c // 128

    def body(p_ref, w_hbm, m_hbm, v_hbm, s_ref, g_hbm, d_hbm, nm_hbm, nv_hbm, got_ref,
             in_buf, out_buf, in_sems, out_sems, send_sems, recv_sems, local_sems):
        start, wait = _direct_exchange([s_ref], [got_ref], send_sems, recv_sems, local_sems, True)
        i = pl.program_id(0)
        pl.when(i == 0)(start)
        slot = i % 2

        def loads(step, into):
            cols = pl.ds(pl.multiple_of(step * 128, 128), 128)
            return [pltpu.make_async_copy(h.at[:, 0, cols], in_buf.at[into, k], in_sems.at[into, k])
                    for k, h in enumerate((w_hbm, m_hbm, v_hbm))]

        cols = pl.ds(pl.multiple_of(i * 128, 128), 128)
        stores = [pltpu.make_async_copy(out_buf.at[k], h.at[:, 0, cols], out_sems.at[k])
                  for k, h in enumerate((g_hbm, d_hbm, nm_hbm, nv_hbm))]

        @pl.when(i == 0)
        def _():
            for cp in loads(0, 0):
                cp.start()

        @pl.when(i + 1 < steps)
        def _():
            for cp in loads(i + 1, 1 - slot):
                cp.start()

        g = p_ref[0].astype(F32)
        for d in range(1, n_part):
            g = g + p_ref[d].astype(F32)
        for cp in loads(i, slot):
            cp.wait()
        delta, nm, nv = _adamw(in_buf[slot, 0], g, in_buf[slot, 1], in_buf[slot, 2])

        @pl.when(i > 0)
        def _():
            for cp in stores:
                cp.wait()

        for k, val in enumerate((g, delta, nm, nv)):
            out_buf[k] = val
        for cp in stores:
            cp.start()

        @pl.when(i == steps - 1)
        def _():
            for cp in stores:
                cp.wait()
            wait()

    return pl.pallas_call(
        body, name="adamw_w_in", grid=(steps,),
        in_specs=[pl.BlockSpec((n_part, r, 128), lambda i: (0, 0, i)), ANY, ANY, ANY, ANY],
        out_specs=[ANY] * 5,
        out_shape=[jax.ShapeDtypeStruct((r, 1, c), F32)] * 4 + [_exchange_shape(small, True)],
        scratch_shapes=[pltpu.VMEM((2, 3, r, 128), F32), pltpu.VMEM((4, r, 128), F32),
                        pltpu.SemaphoreType.DMA((2, 3)), pltpu.SemaphoreType.DMA((4,))] + _exchange_sems(1),
        compiler_params=_params(dimension_semantics=("arbitrary",)),
    )(parts, w, m, v, small)


def _adamw_small(got, me, metas, norms, finals, biases):
    def body(me_ref, gm_ref, gr_ref, *refs):
        ins, outs = refs[:12], refs[12:]
        g_meta, g_rest = gm_ref[0], gr_ref[0]
        for d in range(1, N_DEV):
            g_meta, g_rest = g_meta + gm_ref[d], g_rest + gr_ref[d]
        grads = [g_meta, g_rest[0:1], g_rest[1:2], g_rest[2:3, :FOX_HEADS]]
        for k, g in enumerate(grads):
            w_ref, m_ref, v_ref = ins[3 * k:3 * k + 3]
            delta, new_m, new_v = _adamw(w_ref[...], g, m_ref[...], v_ref[...])
            for o_ref, val in zip(outs[4 * k:4 * k + 4], (g, delta, new_m, new_v)):
                o_ref[...] = val
        outs[16][...] = g_rest[3:4, :128]

    groups = (metas, norms, finals, biases)
    full = lambda a: pl.BlockSpec(a.shape, lambda i, me_ref: (0,) * a.ndim)
    flat = [a for grp in groups for a in grp]
    res = pl.pallas_call(
        body, name="adamw_small",
        grid_spec=pltpu.PrefetchScalarGridSpec(
            num_scalar_prefetch=1, grid=(1,),
            in_specs=[pl.BlockSpec((N_DEV, N_META, META_BLK), lambda i, me_ref: (0, 0, me_ref[0])),
                      pl.BlockSpec((N_DEV, 8, D_MODEL), lambda i, me_ref: (0, N_META // 8, 0))]
            + [full(a) for a in flat],
            out_specs=[full(grp[0]) for grp in groups for _ in range(4)]
            + [pl.BlockSpec((1, 128), lambda i, me_ref: (0, 0))]),
        out_shape=[jax.ShapeDtypeStruct(grp[0].shape, F32) for grp in groups for _ in range(4)]
        + [jax.ShapeDtypeStruct((1, 128), F32)],
        compiler_params=_params(dimension_semantics=("arbitrary",)),
    )(me, got, got, *flat)
    return [res[4 * k:4 * k + 4] for k in range(4)], res[16]


def kernel(x, meta_tokens, norm_g, w_in, b_f, w_out, final_g, loss_target, m_meta_tokens, m_norm_g, m_w_in, m_b_f, m_w_out, m_final_g, v_meta_tokens, v_norm_g, v_w_in, v_b_f, v_w_out, v_final_g):
    core = lax.axis_index("c")
    me = 4 * lax.axis_index("x") + 2 * lax.axis_index("y") + core
    tables = _tables()

    wt_all, meta_all = _gather_two_level([_slab(w_in[0].T.astype(BF16), me), meta_tokens], name="gather_w_in")
    slabs, tail = _join_edges(wt_all)
    meta_full = jnp.transpose(meta_all, (1, 0, 2)).reshape(N_META, D_MODEL)
    h_pad = jnp.concatenate([jnp.zeros((PAD, D_MODEL), F32), meta_full, x[0]], axis=0)
    b_pad = jnp.pad(b_f, ((0, 0), (0, 128 - FOX_HEADS)))

    u, z = _rms_z_ret(h_pad, norm_g, slabs)
    z, zff = _z_fox(u, slabs, tail, z)
    bias = _forget_fwd(zff, b_pad)[:FOX_HEADS].reshape(FOX_HEADS, 1, T)
    raw, y = _ret_fwd(z, tables)
    a_f, lse, y, w_out_all = _fox_fwd(z, bias, y, w_out[0].astype(BF16))
    w_out_b = w_out_all.reshape(D_MIX, D_MODEL)
    dout, dout_b, dy, loss_blk, d_final_g = _out_loss_dy(y, w_out_b, h_pad, loss_target[0],
                                                         final_g.reshape(1, D_MODEL))

    d_w_out = _mm_tn(y, dout_b, tm=D_MIX, tn=256, name="mm_dwout")
    dz = _ret_bwd(z, tables, raw, dy)
    dz, dc, got_w_out = _fox_bwd(z, bias, a_f, lse, dy, dz, d_w_out.reshape(N_DEV, WO_BLK, D_MODEL))
    dz, db_f = _forget_bwd(zff, b_pad, dc.reshape(FOX_HEADS, T), dz)

    kept, via, direct = _dwin_pair_slabs(dz, u, core)
    dh, d_norm_g, summed = _du_rms(dz, slabs, tail, h_pad, dout, norm_g, kept, via)
    got_w_in = _unslab([kept[1], direct, summed], me)

    small = jnp.concatenate([
        dh[PAD:CHUNK], d_norm_g, d_final_g, jnp.pad(db_f[:, :FOX_HEADS], ((0, 0), (0, D_MODEL - FOX_HEADS))),
        jnp.pad(loss_blk[0:1], ((0, 0), (0, D_MODEL - 128))),
        jnp.zeros((SMALL_ROWS - N_META - 4, D_MODEL), F32)], axis=0)
    fore = lambda a: jnp.transpose(a, (2, 0, 1))
    g_w_in, d_w_in, nm_w_in, nv_w_in, got_small = _sum_adamw_w_in(
        got_w_in, fore(w_in), fore(m_w_in), fore(v_w_in), small)
    g_w_out, d_w_out, nm_w_out, nv_w_out = _sum_adamw(got_w_out, w_out[0], m_w_out[0], v_w_out[0], 128, "adamw_w_out")

    row = lambda a: a.reshape(1, D_MODEL)
    (meta_o, norm_o, final_o, bias_o), loss_row = _adamw_small(
        got_small, me.astype(jnp.int32).reshape(1),
        (meta_tokens, m_meta_tokens, v_meta_tokens), (norm_g, m_norm_g, v_norm_g),
        (row(final_g), row(m_final_g), row(v_final_g)), (b_f, m_b_f, v_b_f))
    final_o = [a.reshape(D_MODEL) for a in final_o]

    back = lambda a: jnp.transpose(a, (1, 2, 0))
    outs = [[meta_o[k], norm_o[k], back(wk), bias_o[k], ok[None], final_o[k]]
            for k, (wk, ok) in enumerate(zip((g_w_in, d_w_in, nm_w_in, nv_w_in),
                                             (g_w_out, d_w_out, nm_w_out, nv_w_out)))]
    return (loss_row[0, 0], dh[CHUNK:][None], *outs[0], *outs[1], *outs[2], *outs[3])
```

```python
import numpy as np
import jax
import jax.numpy as jnp
from jax import lax
from jax.experimental import pallas as pl
from jax.experimental.pallas import tpu as pltpu

F32 = jnp.float32
BF16 = jnp.bfloat16

N_DEV = 8
N_CHIP = 4
D_MODEL = 1024
SEQ = 2048
N_META = 16
CHUNK = 128
PAD = CHUNK - N_META
T = SEQ + CHUNK
NCHUNK = T // CHUNK
D_MIX = 2048
RET_HEADS = 4
RET_DK = 128
RET_DV = 256
RET_W = 2 * RET_DK + 2 * RET_DV
FOX_HEADS = 16
FOX_D = 64
FOX_PAIRS = FOX_HEADS // 2
FOX_W = 4 * 128
FOX_BASE = RET_HEADS * RET_W
FF_BASE = FOX_BASE + FOX_PAIRS * FOX_W
D_IN = 7184
D_IN_PAD = 7296
W_BLK = D_IN // N_DEV
WO_BLK = D_MIX // N_DEV
META_BLK = D_MODEL // N_DEV
EPS = 1e-6
NEG_INF = -1e30
ROPE_BASE = 10000.0
LOG2E = 1.4426950408889634
LN2 = 0.6931471805599453

ADAM_LR = 0.001
ADAM_B1 = 0.9
ADAM_B2 = 0.999
ADAM_EPS = 1e-08
ADAM_WD = 0.01
ADAM_STEP = 10

SMALL_ROWS = 24
VMEM_LIMIT = 56 * 1024 * 1024
MESH = pl.DeviceIdType.MESH
ANY = pl.BlockSpec(memory_space=pl.ANY)

_NT = (((1,), (1,)), ((), ()))
_TN = (((0,), (0,)), ((), ()))


def _dot(a, b):
    return jnp.dot(a, b, preferred_element_type=F32)


def _dot_nt(a, b):
    return lax.dot_general(a, b, _NT, preferred_element_type=F32)


def _dot_tn(a, b):
    return lax.dot_general(a, b, _TN, preferred_element_type=F32)


def _params(**kw):
    return pltpu.CompilerParams(vmem_limit_bytes=VMEM_LIMIT, **kw)


def _silu_parts(g):
    sig = jax.nn.sigmoid(g)
    return g * sig, sig * (1.0 + g * (1.0 - sig))


def _tables():
    pos = np.arange(T, dtype=np.float32) - PAD
    inv = (ROPE_BASE ** (-np.arange(0, RET_DK, 2, dtype=np.float32) / RET_DK)).astype(np.float32)
    ang = pos[:, None] * inv[None, :]
    cos, sin = np.cos(ang), np.sin(ang)
    cosf = np.concatenate([cos, cos], axis=1).astype(np.float32)
    sins = np.concatenate([-sin, sin], axis=1).astype(np.float32)
    h = np.arange(RET_HEADS, dtype=np.float32)
    log_gamma = np.log1p(-np.exp2(-5.0 - h)).astype(np.float32)
    idx = np.arange(CHUNK, dtype=np.float32)
    diff = idx[:, None] - idx[None, :]
    dmask = np.where(diff[None] >= 0,
                     np.exp(log_gamma[:, None, None] * np.maximum(diff, 0.0)[None]), 0.0)
    zeta = np.exp(log_gamma[:, None] * (CHUNK - 1.0 - idx)[None, :])
    xi = np.exp(log_gamma[:, None] * (idx + 1.0)[None, :])
    cdec = np.exp(log_gamma * CHUNK)
    return (jnp.asarray(cosf), jnp.asarray(sins), jnp.asarray(dmask, F32),
            jnp.asarray(zeta[:, :, None], F32), jnp.asarray(xi[:, :, None], F32),
            jnp.asarray(cdec[:, None, None], F32))


W_STRIDE = 896
W_SLAB = 912
W_EDGE = W_SLAB - W_STRIDE
def _slab(block, me):
    shift = W_BLK - W_STRIDE
    return lax.switch(me, [lambda b, d=d: jnp.pad(b, ((shift * d, W_SLAB - W_BLK - shift * d), (0, 0)))
                           for d in range(N_DEV)], block)


def _unslab(slabs, me):
    shift = W_BLK - W_STRIDE
    return lax.switch(me, [lambda *s, d=d: jnp.stack([a[shift * d:shift * d + W_BLK] for a in s])
                           for d in range(N_DEV)], *slabs)


def _join_edges(slabs):
    last = slabs[:, W_STRIDE:]
    first = slabs[:, :W_EDGE] + jnp.concatenate([jnp.zeros_like(last[:1]), last[:-1]], axis=0)
    tail = jnp.pad(last[N_DEV - 1], ((0, 128 - W_EDGE), (0, 0)))
    return lax.dynamic_update_slice(slabs, first, (0, 0, 0)), tail


def _mm_tn(a, b, *, tm, tn, name):
    k, m = a.shape
    n = b.shape[1]
    assert m % tm == 0 and n % tn == 0

    def body(a_ref, b_ref, o_ref):
        o_ref[...] = _dot_tn(a_ref[...], b_ref[...]).astype(BF16)

    return pl.pallas_call(
        body, name=name, grid=(n // tn, m // tm),
        in_specs=[pl.BlockSpec((k, tm), lambda j, i: (0, i)),
                  pl.BlockSpec((k, tn), lambda j, i: (0, j))],
        out_specs=pl.BlockSpec((tm, tn), lambda j, i: (i, j)),
        out_shape=jax.ShapeDtypeStruct((m, n), BF16),
        compiler_params=_params(dimension_semantics=("arbitrary", "arbitrary")),
    )(a, b)


def _piece_spec(base, mult):
    def index(i):
        p = base + mult * i
        return p // 7, p % 7, 0
    return pl.BlockSpec((1, 128, D_MODEL), index)


_RET_PIECES = ((0, 1), (4, 1), (8, 2), (9, 2), (16, 2), (17, 2))
_FOX_PIECES = ((24, 1), (32, 1), (40, 1), (48, 1))


def _rms_z_ret(x, meta, g, slabs):
    def body(x_hbm, m_ref, g_ref, *refs):
        pieces, h_ref, u_ref, z_ref, x_sem = refs[:6], refs[6], refs[7], refs[8], refs[9]

        @pl.when(pl.program_id(0) == 0)
        def _():
            tokens = pltpu.make_async_copy(x_hbm, h_ref.at[pl.ds(CHUNK, SEQ)], x_sem)
            tokens.start()
            h_ref[pl.ds(0, PAD), :] = jnp.zeros((PAD, D_MODEL), F32)
            h_ref[pl.ds(PAD, N_META), :] = m_ref[...]
            tokens.wait()
            h = h_ref[...]
            r = lax.rsqrt(jnp.mean(h * h, axis=-1, keepdims=True) + EPS)
            u_ref[...] = (h * r * g_ref[...]).astype(BF16)

        w = jnp.concatenate([p[0] for p in pieces], axis=0)
        z_ref[...] = _dot_nt(u_ref[...], w).astype(BF16)

    whole = pl.BlockSpec((T, D_MODEL), lambda i: (0, 0))
    return pl.pallas_call(
        body, name="rms_z_ret", grid=(RET_HEADS,),
        in_specs=[ANY, pl.BlockSpec((N_META, D_MODEL), lambda i: (0, 0)),
                  pl.BlockSpec((1, D_MODEL), lambda i: (0, 0))]
        + [_piece_spec(*bm) for bm in _RET_PIECES],
        out_specs=[whole, whole, pl.BlockSpec((T, RET_W), lambda i: (0, i))],
        out_shape=[jax.ShapeDtypeStruct((T, D_MODEL), F32),
                   jax.ShapeDtypeStruct((T, D_MODEL), BF16),
                   jax.ShapeDtypeStruct((T, D_IN_PAD), BF16)],
        scratch_shapes=[pltpu.SemaphoreType.DMA],
        compiler_params=_params(dimension_semantics=("arbitrary",)),
    )(x, meta, g, *([slabs] * 6))


def _z_fox(u, slabs, tail, z):
    def body(u_ref, *refs):
        pieces, t_ref, z_ref, zff_ref = refs[:4], refs[4], refs[6], refs[7]
        u = u_ref[...]
        w = jnp.concatenate([p[0] for p in pieces], axis=0)
        z_ref[...] = _dot_nt(u, w).astype(BF16)

        @pl.when(pl.program_id(0) == 0)
        def _():
            zff_ref[...] = _dot_nt(u, t_ref[...])

    return pl.pallas_call(
        body, name="z_fox", grid=(FOX_PAIRS,),
        in_specs=[pl.BlockSpec((T, D_MODEL), lambda i: (0, 0))] + [_piece_spec(*bm) for bm in _FOX_PIECES]
        + [pl.BlockSpec((128, D_MODEL), lambda i: (0, 0)), ANY],
        out_specs=[_FOX_Z_SPEC, pl.BlockSpec((T, 128), lambda i: (0, 0))],
        out_shape=[jax.ShapeDtypeStruct((T, D_IN_PAD), BF16),
                   jax.ShapeDtypeStruct((T, 128), F32)],
        input_output_aliases={6: 0},
        compiler_params=_params(dimension_semantics=("arbitrary",)),
    )(u, *([slabs] * 4), tail, z)


def _out_loss_dy(y, w_out_b, h_pad, target, g):
    tm = T // 4

    def body(y_ref, w_ref, h_ref, t_hbm, g_ref, d_ref, db_ref, dy_ref, loss_ref, dg_ref, t_buf, t_sem):
        i = pl.program_id(0)
        head = pltpu.make_async_copy(t_hbm.at[pl.ds(0, tm - CHUNK)], t_buf.at[pl.ds(CHUNK, tm - CHUNK)], t_sem)
        rest = pltpu.make_async_copy(t_hbm.at[pl.ds(pl.multiple_of(jnp.maximum(i, 1) * tm - CHUNK, 8), tm)],
                                     t_buf, t_sem)

        @pl.when(i == 0)
        def _():
            t_buf[pl.ds(0, CHUNK), :] = jnp.zeros((CHUNK, D_MODEL), F32)
            head.start()
            loss_ref[...] = jnp.zeros_like(loss_ref)
            dg_ref[...] = jnp.zeros_like(dg_ref)

        pl.when(i > 0)(rest.start)

        w = w_ref[...]
        o = _dot(y_ref[...], w) + h_ref[...]
        pl.when(i == 0)(head.wait)
        pl.when(i > 0)(rest.wait)
        token = lax.broadcasted_iota(jnp.int32, (tm, 1), 0) + i * tm >= CHUNK
        g = g_ref[...]
        r = lax.rsqrt(jnp.mean(o * o, axis=-1, keepdims=True) + EPS)
        xn = o * r
        e = jnp.where(token, xn * g - t_buf[...], 0.0)
        loss_ref[...] += jnp.full(loss_ref.shape, 0.5 / D_MODEL * jnp.sum(e * e), F32)
        do = e * (1.0 / D_MODEL)
        dg_ref[...] += jnp.sum(do * xn, axis=0, keepdims=True)
        dn = do * g
        d = r * (dn - xn * jnp.mean(dn * xn, axis=-1, keepdims=True))
        d_b = d.astype(BF16)
        d_ref[...] = d
        db_ref[...] = d_b
        dy_ref[...] = _dot_nt(d_b, w).astype(BF16)

    tile = pl.BlockSpec((tm, D_MODEL), lambda i: (i, 0))
    wide = pl.BlockSpec((tm, D_MIX), lambda i: (i, 0))
    return pl.pallas_call(
        body, name="out_loss_dy", grid=(T // tm,),
        in_specs=[wide, pl.BlockSpec((D_MIX, D_MODEL), lambda i: (0, 0)), tile, ANY,
                  pl.BlockSpec((1, D_MODEL), lambda i: (0, 0))],
        out_specs=[tile, tile, wide,
                   pl.BlockSpec((8, 128), lambda i: (0, 0)),
                   pl.BlockSpec((1, D_MODEL), lambda i: (0, 0))],
        out_shape=[jax.ShapeDtypeStruct((T, D_MODEL), F32),
                   jax.ShapeDtypeStruct((T, D_MODEL), BF16),
                   jax.ShapeDtypeStruct((T, D_MIX), BF16),
                   jax.ShapeDtypeStruct((8, 128), F32),
                   jax.ShapeDtypeStruct((1, D_MODEL), F32)],
        scratch_shapes=[pltpu.VMEM((tm, D_MODEL), F32), pltpu.SemaphoreType.DMA],
        compiler_params=_params(dimension_semantics=("arbitrary",)),
    )(y, w_out_b, h_pad, target, g)


def _coords():
    return lax.axis_index("x"), lax.axis_index("y"), lax.axis_index("c")


def _flip(v, bit):
    return 1 - v if bit else v


def _peer(x, y, c, r):
    return _flip(x, (r >> 2) & 1), _flip(y, (r >> 1) & 1), _flip(c, r & 1)


def _direct_exchange(ins, outs, send_sems, recv_sems, local_sems, gather):
    x, y, c = _coords()
    me = 4 * x + 2 * y + c

    def src(k, to_idx):
        return ins[k] if gather else ins[k].at[to_idx]

    local = [pltpu.make_async_copy(src(k, me), outs[k].at[me], local_sems.at[k])
             for k in range(len(ins))]
    sends, recvs = [], []
    for r in range(1, N_DEV):
        px, py, pc = _peer(x, y, c, r)
        peer = 4 * px + 2 * py + pc
        for k in range(len(ins)):
            sems = dict(send_sem=send_sems.at[k, r - 1], recv_sem=recv_sems.at[k, r - 1],
                        device_id=(px, py, pc), device_id_type=MESH)
            sends.append(pltpu.make_async_remote_copy(src_ref=src(k, peer), dst_ref=outs[k].at[me], **sems))
            recvs.append(pltpu.make_async_remote_copy(src_ref=src(k, peer), dst_ref=outs[k].at[peer], **sems))

    def start():
        for cp in local + sends:
            cp.start()

    def wait():
        for cp in recvs:
            cp.wait_recv()
        for cp in sends:
            cp.wait_send()
        for cp in local:
            cp.wait()

    return start, wait


def _exchange_sems(n_arr):
    return [pltpu.SemaphoreType.DMA((n_arr, N_DEV - 1)), pltpu.SemaphoreType.DMA((n_arr, N_DEV - 1)),
            pltpu.SemaphoreType.DMA((n_arr,))]


def _exchange_shape(a, gather):
    return jax.ShapeDtypeStruct(((N_DEV,) + a.shape) if gather else a.shape, a.dtype)


def _gather_two_level(arrays, name):
    n_arr = len(arrays)

    def body(*refs):
        ins, outs = refs[:n_arr], refs[n_arr:2 * n_arr]
        send_sems, recv_sems, local_sems = refs[2 * n_arr:]
        x, y, c = _coords()

        def slot(k, px, py, pc):
            return outs[k].at[4 * px + 2 * py + pc]

        def routed(core):
            me, sibling = (x, y, core), (x, y, 1 - core)
            xn, yn, dg = (1 - x, y), (x, 1 - y), (1 - x, 1 - y)
            (first, s_first), (second, s_second) = ((xn, 1), (yn, 2)) if core == 0 else ((yn, 2), (xn, 1))

            def copy(k, j, block, to, own=False):
                return pltpu.make_async_remote_copy(
                    src_ref=ins[k] if own else slot(k, *block), dst_ref=slot(k, *block),
                    send_sem=send_sems.at[k, j], recv_sem=recv_sems.at[k, j],
                    device_id=to, device_id_type=MESH)

            local = [pltpu.make_async_copy(ins[k], slot(k, *me), local_sems.at[k]) for k in range(n_arr)]
            sent = []
            for k in range(n_arr):
                sent += [copy(k, 0, me, sibling, True), copy(k, 1, me, (*xn, core), True),
                         copy(k, 2, me, (*yn, core), True)]
            for cp in local + sent:
                cp.start()

            def pass_on(k, j_from, j_to, block, targets):
                copy(k, j_from, block, me).wait_recv()
                for j, to in zip(j_to, targets):
                    cp = copy(k, j, block, to)
                    cp.start()
                    sent.append(cp)

            for k in range(n_arr):
                pass_on(k, s_first, (3, 3 + s_first), (*first, core), ((*second, core), sibling))
            for k in range(n_arr):
                pass_on(k, s_second, (3 + s_second,), (*second, core), (sibling,))
            for k in range(n_arr):
                pass_on(k, 3, (6,), (*dg, core), (sibling,))
            for k in range(n_arr):
                copy(k, 0, sibling, me).wait_recv()
                for j, chip in ((4, xn), (5, yn), (6, dg)):
                    copy(k, j, (*chip, 1 - core), me).wait_recv()
            for cp in sent:
                cp.wait_send()
            for cp in local:
                cp.wait()

        for core in (0, 1):
            pl.when(c == core)(lambda core=core: routed(core))

    return pl.pallas_call(
        body, name=name,
        in_specs=[ANY] * n_arr, out_specs=[ANY] * n_arr,
        out_shape=[_exchange_shape(a, True) for a in arrays],
        scratch_shapes=_exchange_sems(n_arr),
    )(*arrays)


def _piece_columns():
    pos = {}
    for h in range(RET_HEADS):
        for k, p in enumerate((h, 4 + h, 8 + 2 * h, 9 + 2 * h, 16 + 2 * h, 17 + 2 * h)):
            pos[p] = 6 * h + k
    for p in range(FOX_PAIRS):
        for i in range(4):
            pos[24 + 8 * i + p] = 24 + 4 * p + i
    pos[D_IN_PAD // 128 - 1] = D_IN_PAD // 128 - 1
    return np.array([pos[7 * d + j] for d in range(N_DEV) for j in range(8)], np.int32)


def _routes(core):
    x, y, _ = _coords()
    return ((1 - x, y), (x, 1 - y)) if core == 0 else ((x, 1 - y), (1 - x, y))


def _dwin_pair_slabs(dz, u, core):
    def body(order_ref, cols_ref, *refs):
        pieces, u_ref = refs[:8], refs[8]
        kept_ref, theirs_ref, via_ref, direct_ref = refs[9:13]
        send_buf, keep_buf, got_buf, push_send, push_recv, hop_send, hop_recv, load_sem = refs[13:]
        s = pl.program_id(0)
        x, y, c = _coords()
        cols = jnp.concatenate([p[...] for p in pieces[:7]] + [pieces[7][:, :W_EDGE]], axis=1)
        slab = _dot_tn(cols, u_ref[...])

        def push(k):
            return pltpu.make_async_remote_copy(
                src_ref=send_buf.at[k], dst_ref=theirs_ref.at[k],
                send_sem=push_send.at[k], recv_sem=push_recv.at[k],
                device_id=(x, y, 1 - c), device_id_type=MESH)

        def hop(k, core):
            first, _ = _routes(core)
            return pltpu.make_async_remote_copy(
                src_ref=keep_buf.at[k], dst_ref=direct_ref if k == 0 else via_ref,
                send_sem=hop_send.at[k], recv_sem=hop_recv.at[k],
                device_id=(*first, core), device_id_type=MESH)

        for k in range(N_CHIP):
            @pl.when(s == 2 * k)
            def _(k=k):
                send_buf[k] = slab.astype(BF16)
                push(k).start()

            @pl.when(s == 2 * k + 1)
            def _(k=k):
                push(k).wait_recv()
                load = pltpu.make_async_copy(theirs_ref.at[k], got_buf, load_sem)
                load.start()
                load.wait()
                total = (slab + got_buf[...].astype(F32)).astype(BF16)
                if k < 2:
                    keep_buf[k] = total
                    for core in (0, 1):
                        pl.when(c == core)(lambda core=core: hop(k, core).start())
                else:
                    kept_ref[0] = total

        @pl.when(s == 2 * N_CHIP - 1)
        def _():
            for core in (0, 1):
                @pl.when(c == core)
                def _(core=core):
                    for k in range(2):
                        hop(k, core).wait_recv()
                        hop(k, core).wait_send()
            for k in range(N_CHIP):
                push(k).wait_send()

    x, y = lax.axis_index("x"), lax.axis_index("y")
    xn, yn, dg, own = 2 * (1 - x) + y, 2 * x + 1 - y, 2 * (1 - x) + 1 - y, 2 * x + y
    mine = jnp.where(core == 0, jnp.stack([xn, dg, yn, own]), jnp.stack([yn, dg, xn, own]))
    sibs = jnp.where(core == 0, jnp.stack([yn, dg, xn, own]), jnp.stack([xn, dg, yn, own]))
    order = jnp.stack([2 * sibs + (1 - core), 2 * mine + core], axis=1).reshape(2 * N_CHIP).astype(jnp.int32)
    piece = lambda j: pl.BlockSpec((T, 128), lambda s, order_ref, cols_ref: (0, cols_ref[order_ref[s] * 8 + j]))
    slab = jax.ShapeDtypeStruct((W_SLAB, D_MODEL), BF16)
    kept, _, via, direct = pl.pallas_call(
        body, name="dwin_pair_slabs",
        grid_spec=pltpu.PrefetchScalarGridSpec(
            num_scalar_prefetch=2, grid=(2 * N_CHIP,),
            in_specs=[piece(j) for j in range(8)]
            + [pl.BlockSpec((T, D_MODEL), lambda s, order_ref, cols_ref: (0, 0))],
            out_specs=[pl.BlockSpec((1, W_SLAB, D_MODEL),
                                    lambda s, order_ref, cols_ref: (jnp.where(s >= 6, 1, 0), 0, 0)),
                       ANY, ANY, ANY],
            scratch_shapes=[pltpu.VMEM((N_CHIP, W_SLAB, D_MODEL), BF16), pltpu.VMEM((2, W_SLAB, D_MODEL), BF16),
                            pltpu.VMEM((W_SLAB, D_MODEL), BF16),
                            pltpu.SemaphoreType.DMA((N_CHIP,)), pltpu.SemaphoreType.DMA((N_CHIP,)),
                            pltpu.SemaphoreType.DMA((2,)), pltpu.SemaphoreType.DMA((2,)),
                            pltpu.SemaphoreType.DMA]),
        out_shape=[jax.ShapeDtypeStruct((2, W_SLAB, D_MODEL), BF16),
                   jax.ShapeDtypeStruct((N_CHIP, W_SLAB, D_MODEL), BF16), slab, slab],
        compiler_params=_params(dimension_semantics=("arbitrary",)),
    )(order, jnp.asarray(_piece_columns()), *([dz] * 8), u)
    return kept, via, direct


def _du_rms(dz, slabs, tail, h_pad, dout, g, kept, via):
    tm = 272
    steps = T // tm
    columns = _piece_columns().reshape(N_DEV, 8)

    def body(dz_ref, w_ref, t_ref, h_ref, d_ref, g_ref, kept_ref, via_ref, dh_ref, dg_ref, got_ref,
             via_buf, mine_buf, send_sem, recv_sem, local_sems):
        i = pl.program_id(0)
        x, y, c = _coords()

        def summed(core):
            _, second = _routes(core)
            return pltpu.make_async_remote_copy(
                src_ref=mine_buf, dst_ref=got_ref, send_sem=send_sem, recv_sem=recv_sem,
                device_id=(*second, core), device_id_type=MESH)

        @pl.when(i == 0)
        def _():
            loads = [pltpu.make_async_copy(kept_ref.at[0], mine_buf, local_sems.at[0]),
                     pltpu.make_async_copy(via_ref, via_buf, local_sems.at[1])]
            for cp in loads:
                cp.start()
            for cp in loads:
                cp.wait()
            mine_buf[...] = (mine_buf[...].astype(F32) + via_buf[...].astype(F32)).astype(BF16)
            for core in (0, 1):
                pl.when(c == core)(lambda core=core: summed(core).start())
            dg_ref[...] = jnp.zeros_like(dg_ref)

        du = _dot(dz_ref[:, pl.ds(FF_BASE, 128)], t_ref[...])
        for d in range(N_DEV):
            cols = jnp.concatenate([dz_ref[:, pl.ds(128 * int(columns[d, j]), 128)] for j in range(7)], axis=1)
            du = du + _dot(cols, w_ref[d, pl.ds(0, W_STRIDE), :])
        h = h_ref[...]
        r = lax.rsqrt(jnp.mean(h * h, axis=-1, keepdims=True) + EPS)
        xn = h * r
        dg_ref[...] += jnp.sum(du * xn, axis=0, keepdims=True)
        dn = du * g_ref[...]
        dh_ref[...] = d_ref[...] + r * (dn - xn * jnp.mean(dn * xn, axis=-1, keepdims=True))

        for core in (0, 1):
            @pl.when(jnp.logical_and(c == core, i == steps - 1))
            def _(core=core):
                summed(core).wait_recv()
                summed(core).wait_send()

    tile = pl.BlockSpec((tm, D_MODEL), lambda i: (i, 0))
    return pl.pallas_call(
        body, name="du_rms", grid=(steps,),
        in_specs=[pl.BlockSpec((tm, D_IN_PAD), lambda i: (i, 0)),
                  pl.BlockSpec((N_DEV, W_SLAB, D_MODEL), lambda i: (0, 0, 0)),
                  pl.BlockSpec((128, D_MODEL), lambda i: (0, 0)),
                  tile, tile, pl.BlockSpec((1, D_MODEL), lambda i: (0, 0)), ANY, ANY],
        out_specs=[tile, pl.BlockSpec((1, D_MODEL), lambda i: (0, 0)), ANY],
        out_shape=[jax.ShapeDtypeStruct((T, D_MODEL), F32),
                   jax.ShapeDtypeStruct((1, D_MODEL), F32),
                   jax.ShapeDtypeStruct(via.shape, via.dtype)],
        scratch_shapes=[pltpu.VMEM(via.shape, via.dtype), pltpu.VMEM(via.shape, via.dtype),
                        pltpu.SemaphoreType.DMA, pltpu.SemaphoreType.DMA, pltpu.SemaphoreType.DMA((2,))],
        compiler_params=_params(dimension_semantics=("arbitrary",)),
    )(dz, slabs, tail, h_pad, dout, g, kept, via)


def _tri(lower):
    r = lax.broadcasted_iota(jnp.int32, (CHUNK, CHUNK), 0)
    c = lax.broadcasted_iota(jnp.int32, (CHUNK, CHUNK), 1)
    return jnp.where((r >= c) if lower else (r <= c), 1.0, 0.0).astype(F32)


def _row_valid(n):
    r = lax.broadcasted_iota(jnp.int32, (CHUNK, 128), 0) + n * CHUNK
    return r >= PAD


_FF_SPEC = pl.BlockSpec((T, 128), lambda i: (0, FF_BASE // 128))
_ZFF_SPEC = pl.BlockSpec((T, 128), lambda i: (0, 0))


def _forget_fwd(z, b_pad):
    def body(z_ref, b_ref, o_ref):
        tri = _tri(True)
        carry = jnp.zeros((1, 128), F32)
        for n in range(NCHUNK):
            rows = pl.ds(n * CHUNK, CHUNK)
            a = z_ref[rows, :] + b_ref[...]
            lf = -(jnp.maximum(-a, 0.0) + jnp.log(1.0 + jnp.exp(-jnp.abs(a))))
            lf = jnp.where(_row_valid(n), lf, 0.0)
            c = jnp.dot(tri, lf, precision=lax.Precision.HIGHEST,
                        preferred_element_type=F32) + carry
            carry = c[CHUNK - 1:CHUNK, :]
            o_ref[:, rows] = jnp.where(_row_valid(n), c * (-LOG2E), NEG_INF).T

    return pl.pallas_call(
        body, name="forget_fwd", grid=(1,),
        in_specs=[_ZFF_SPEC, pl.BlockSpec((1, 128), lambda i: (0, 0))],
        out_specs=pl.BlockSpec((128, T), lambda i: (0, 0)),
        out_shape=jax.ShapeDtypeStruct((128, T), F32),
        compiler_params=_params(dimension_semantics=("arbitrary",)),
    )(z, b_pad)


def _forget_bwd(z, b_pad, dc, dz):
    def body(z_ref, b_ref, dc_ref, dz_in, dff_ref, db_ref):
        tri = _tri(False)
        carry = jnp.zeros((1, 128), F32)
        db = jnp.zeros((1, 128), F32)
        for n in reversed(range(NCHUNK)):
            rows = pl.ds(n * CHUNK, CHUNK)
            dc_blk = jnp.concatenate([dc_ref[:, rows], jnp.zeros((128 - FOX_HEADS, CHUNK), F32)], axis=0).T
            dlf = jnp.dot(tri, dc_blk, precision=lax.Precision.HIGHEST,
                          preferred_element_type=F32) + carry
            carry = dlf[0:1, :]
            a = z_ref[rows, :] + b_ref[...]
            dff = jnp.where(_row_valid(n), dlf * jax.nn.sigmoid(-a), 0.0)
            dff_ref[rows, :] = dff.astype(BF16)
            db = db + jnp.sum(dff, axis=0, keepdims=True)
        db_ref[...] = db

    return pl.pallas_call(
        body, name="forget_bwd", grid=(1,),
        in_specs=[_ZFF_SPEC, pl.BlockSpec((1, 128), lambda i: (0, 0)),
                  pl.BlockSpec((FOX_HEADS, T), lambda i: (0, 0)), ANY],
        out_specs=[_FF_SPEC, pl.BlockSpec((1, 128), lambda i: (0, 0))],
        out_shape=[jax.ShapeDtypeStruct((T, D_IN_PAD), BF16),
                   jax.ShapeDtypeStruct((1, 128), F32)],
        input_output_aliases={3: 0},
        compiler_params=_params(dimension_semantics=("arbitrary",)),
    )(z, b_pad, dc, dz)


FOX_QB = 512
FOX_NQB = SEQ // FOX_QB


def _fox_block(b):
    lo = CHUNK + b * FOX_QB
    return pl.ds(lo, FOX_QB), lo, lo + FOX_QB


def _causal_bias():
    r = lax.broadcasted_iota(jnp.int32, (FOX_QB, FOX_QB), 0)
    c = lax.broadcasted_iota(jnp.int32, (FOX_QB, FOX_QB), 1)
    return jnp.where(c <= r, 0.0, NEG_INF).astype(F32)


def _fox_logits(q_blk, k_all, bias, causal, b):
    _, lo, hi = _fox_block(b)
    s_off = _dot_nt(q_blk, k_all[:lo]) + bias[:, :lo]
    s_dia = _dot_nt(q_blk, k_all[lo:hi]) + (bias[:, lo:hi] + causal)
    return s_off, s_dia


_FOX_Z_SPEC = pl.BlockSpec((T, FOX_W), lambda p: (0, FOX_BASE // FOX_W + p))
_FOX_BIAS_SPEC = pl.BlockSpec((2, 1, T), lambda p: (p, 0, 0))
_FOX_LSE_SPEC = pl.BlockSpec((2, T, 1), lambda p: (p, 0, 0))
_FOX_SCALE = FOX_D ** -0.5
_FOX_QSCALE = _FOX_SCALE * LOG2E


def _fox_fwd(z, bias, y, w_out_blk):
    last = FOX_PAIRS - 1

    def body(z_ref, b_ref, y_in, w_ref, a_ref, lse_ref, y_ref, wall_ref,
             send_sems, recv_sems, local_sems):
        start, wait = _direct_exchange([w_ref], [wall_ref], send_sems, recv_sems, local_sems, True)
        pl.when(pl.program_id(0) == 0)(start)

        causal = _causal_bias()
        a_ref[pl.ds(0, CHUNK), :] = jnp.zeros((CHUNK, 128), F32)
        y_ref[pl.ds(0, CHUNK), :] = jnp.zeros((CHUNK, 128), BF16)
        for j in range(2):
            lanes = pl.ds(j * FOX_D, FOX_D)
            k_all = z_ref[:, pl.ds(128 + j * FOX_D, FOX_D)]
            v_all = z_ref[:, pl.ds(256 + j * FOX_D, FOX_D)]
            bias = b_ref[j]
            lse_ref[j, pl.ds(0, CHUNK), :] = jnp.zeros((CHUNK, 1), F32)
            for b in range(FOX_NQB):
                rows, lo, hi = _fox_block(b)
                q_blk = (z_ref[rows, lanes].astype(F32) * _FOX_QSCALE).astype(BF16)
                s_off, s_dia = _fox_logits(q_blk, k_all, bias, causal, b)
                m = jnp.maximum(jnp.max(s_off, axis=-1, keepdims=True),
                                jnp.max(s_dia, axis=-1, keepdims=True))
                e_off = jnp.exp2(s_off - m)
                e_dia = jnp.exp2(s_dia - m)
                total = jnp.sum(e_off, axis=-1, keepdims=True) + jnp.sum(e_dia, axis=-1, keepdims=True)
                o = (_dot(e_off.astype(BF16), v_all[:lo]) + _dot(e_dia.astype(BF16), v_all[lo:hi])) / total
                a_ref[rows, lanes] = o
                lse_ref[j, rows, :] = m + jnp.log(total) * LOG2E
                gate = _silu_parts(z_ref[rows, pl.ds(384 + j * FOX_D, FOX_D)].astype(F32))[0]
                y_ref[rows, lanes] = (o * gate).astype(BF16)

        pl.when(pl.program_id(0) == last)(wait)

    return pl.pallas_call(
        body, name="fox_fwd", grid=(FOX_PAIRS,),
        in_specs=[_FOX_Z_SPEC, _FOX_BIAS_SPEC, ANY, ANY],
        out_specs=[pl.BlockSpec((T, 128), lambda p: (0, p)), _FOX_LSE_SPEC,
                   pl.BlockSpec((T, 128), lambda p: (0, 8 + p)), ANY],
        out_shape=[jax.ShapeDtypeStruct((T, FOX_HEADS * FOX_D), F32),
                   jax.ShapeDtypeStruct((FOX_HEADS, T, 1), F32),
                   jax.ShapeDtypeStruct((T, D_MIX), BF16),
                   _exchange_shape(w_out_blk, True)],
        input_output_aliases={2: 2},
        scratch_shapes=_exchange_sems(1),
        compiler_params=_params(dimension_semantics=("arbitrary",)),
    )(z, bias, y, w_out_blk)


def _fox_bwd(z, bias, a_f, lse, dy, dz, dwo_blocks):
    last = FOX_PAIRS - 1

    def body(z_ref, b_ref, a_ref, lse_ref, dy_ref, dz_in, dwo_ref, dz_ref, dc_ref, got_ref,
             kv_acc, dc_acc, send_sems, recv_sems, local_sems):
        start, wait = _direct_exchange([dwo_ref], [got_ref], send_sems, recv_sems, local_sems, False)
        pl.when(pl.program_id(0) == 0)(start)

        causal = _causal_bias()
        dz_ref[pl.ds(0, CHUNK), pl.ds(0, 128)] = jnp.zeros((CHUNK, 128), BF16)
        dz_ref[pl.ds(0, CHUNK), pl.ds(384, 128)] = jnp.zeros((CHUNK, 128), BF16)
        dk_rows, dv_rows = pl.ds(0, FOX_D), pl.ds(FOX_D, FOX_D)
        for j in range(2):
            lanes = pl.ds(j * FOX_D, FOX_D)
            k_all = z_ref[:, pl.ds(128 + j * FOX_D, FOX_D)]
            v_all = z_ref[:, pl.ds(256 + j * FOX_D, FOX_D)]
            bias = b_ref[j]
            kv_acc[...] = jnp.zeros_like(kv_acc)
            dc_acc[...] = jnp.zeros_like(dc_acc)
            for b in range(FOX_NQB):
                rows, lo, hi = _fox_block(b)
                off, dia = pl.ds(0, lo), pl.ds(lo, FOX_QB)
                q_blk = (z_ref[rows, lanes].astype(F32) * _FOX_QSCALE).astype(BF16)
                s_off, s_dia = _fox_logits(q_blk, k_all, bias, causal, b)
                lse_blk = lse_ref[j, rows, :]
                p_off, p_dia = jnp.exp2(s_off - lse_blk), jnp.exp2(s_dia - lse_blk)
                sg, dsg = _silu_parts(z_ref[rows, pl.ds(384 + j * FOX_D, FOX_D)].astype(F32))
                dyj = dy_ref[rows, lanes].astype(F32)
                dz_ref[rows, pl.ds(384 + j * FOX_D, FOX_D)] = (dyj * a_ref[rows, lanes] * dsg).astype(BF16)
                do_b = (dyj * sg).astype(BF16)
                dp_off = _dot_nt(do_b, v_all[:lo])
                dp_dia = _dot_nt(do_b, v_all[lo:hi])
                d = (jnp.sum(p_off * dp_off, axis=-1, keepdims=True)
                     + jnp.sum(p_dia * dp_dia, axis=-1, keepdims=True))
                ds_off = p_off * (dp_off - d)
                ds_dia = p_dia * (dp_dia - d)
                dc_acc[:, off] -= jnp.sum(ds_off, axis=0, keepdims=True)
                dc_acc[:, dia] -= jnp.sum(ds_dia, axis=0, keepdims=True)
                ds_off_b, ds_dia_b = ds_off.astype(BF16), ds_dia.astype(BF16)
                dq = _dot(ds_off_b, k_all[:lo]) + _dot(ds_dia_b, k_all[lo:hi])
                dz_ref[rows, lanes] = (dq * _FOX_SCALE).astype(BF16)
                kv_acc[dk_rows, off] += _dot_tn(q_blk, ds_off_b)
                kv_acc[dk_rows, dia] += _dot_tn(q_blk, ds_dia_b)
                kv_acc[dv_rows, off] += _dot_tn(do_b, p_off.astype(BF16))
                kv_acc[dv_rows, dia] += _dot_tn(do_b, p_dia.astype(BF16))
            for n in range(NCHUNK):
                rows = pl.ds(n * CHUNK, CHUNK)
                both = kv_acc[:, rows].T
                dz_ref[rows, pl.ds(128 + j * FOX_D, FOX_D)] = (both[:, :FOX_D] * LN2).astype(BF16)
                dz_ref[rows, pl.ds(256 + j * FOX_D, FOX_D)] = both[:, FOX_D:].astype(BF16)
            dc_ref[j] = dc_acc[...]

        pl.when(pl.program_id(0) == last)(wait)

    col = lambda base: pl.BlockSpec((T, 128), lambda p: (0, base + p))
    return pl.pallas_call(
        body, name="fox_bwd", grid=(FOX_PAIRS,),
        in_specs=[_FOX_Z_SPEC, _FOX_BIAS_SPEC, col(0), _FOX_LSE_SPEC, col(8), ANY, ANY],
        out_specs=[_FOX_Z_SPEC, _FOX_BIAS_SPEC, ANY],
        out_shape=[jax.ShapeDtypeStruct((T, D_IN_PAD), BF16),
                   jax.ShapeDtypeStruct((FOX_HEADS, 1, T), F32),
                   _exchange_shape(dwo_blocks, False)],
        input_output_aliases={5: 0},
        scratch_shapes=[pltpu.VMEM((2 * FOX_D, T), F32), pltpu.VMEM((1, T), F32)] + _exchange_sems(1),
        compiler_params=_params(dimension_semantics=("arbitrary",)),
    )(z, bias, a_f, lse, dy, dz, dwo_blocks)


def _rot(x, cosf, sins):
    return x * cosf + pltpu.roll(x, RET_DK // 2, 1) * sins


def _rot_t(d, cosf, sins):
    return d * cosf - pltpu.roll(d, RET_DK // 2, 1) * sins


_RET_Z_SPEC = pl.BlockSpec((T, RET_W), lambda h: (0, h))
_RET_TABLE_SPECS = [
    pl.BlockSpec((T, RET_DK), lambda h: (0, 0)),
    pl.BlockSpec((T, RET_DK), lambda h: (0, 0)),
    pl.BlockSpec((1, CHUNK, CHUNK), lambda h: (h, 0, 0)),
    pl.BlockSpec((1, CHUNK, 1), lambda h: (h, 0, 0)),
    pl.BlockSpec((1, CHUNK, 1), lambda h: (h, 0, 0)),
    pl.BlockSpec((1, 1, 1), lambda h: (h, 0, 0)),
]
_RQ, _RK = pl.ds(0, RET_DK), pl.ds(RET_DK, RET_DK)
_RV, _RG = pl.ds(2 * RET_DK, RET_DV), pl.ds(2 * RET_DK + RET_DV, RET_DV)
_RET_KSCALE = RET_DK ** -0.5


def _ret_fwd(z, tables):
    def body(z_ref, cos_ref, sin_ref, dm_ref, zeta_ref, xi_ref, cd_ref, raw_ref, y_ref):
        dmask, zeta, xi, cdec = dm_ref[0], zeta_ref[0], xi_ref[0], cd_ref[0]
        state = jnp.zeros((RET_DK, RET_DV), F32)
        for n in range(NCHUNK):
            rows = pl.ds(n * CHUNK, CHUNK)
            cosf, sins = cos_ref[rows, :], sin_ref[rows, :]
            qr = _rot(z_ref[rows, _RQ].astype(F32), cosf, sins)
            kr_b = (_rot(z_ref[rows, _RK].astype(F32), cosf, sins) * _RET_KSCALE).astype(BF16)
            v_b = z_ref[rows, _RV]
            a = _dot_nt(qr.astype(BF16), kr_b) * dmask
            out = _dot(a.astype(BF16), v_b) + _dot((qr * xi).astype(BF16), state.astype(BF16))
            state = state * cdec + _dot_tn(kr_b, (v_b.astype(F32) * zeta).astype(BF16))
            raw_ref[rows, :] = out
            r = lax.rsqrt(jnp.mean(out * out, axis=-1, keepdims=True) + EPS)
            y_ref[rows, :] = (out * r * _silu_parts(z_ref[rows, _RG].astype(F32))[0]).astype(BF16)

    wide = pl.BlockSpec((T, RET_DV), lambda h: (0, h))
    return pl.pallas_call(
        body, name="ret_fwd", grid=(RET_HEADS,),
        in_specs=[_RET_Z_SPEC] + _RET_TABLE_SPECS,
        out_specs=[wide, wide],
        out_shape=[jax.ShapeDtypeStruct((T, RET_HEADS * RET_DV), F32),
                   jax.ShapeDtypeStruct((T, D_MIX), BF16)],
        compiler_params=_params(dimension_semantics=("arbitrary",)),
    )(z, *tables)


def _ret_bwd(z, tables, raw, dy):
    def body(z_ref, cos_ref, sin_ref, dm_ref, zeta_ref, xi_ref, cd_ref, raw_ref, dy_ref,
             dz_ref, st_ref):
        dmask, zeta, xi, cdec = dm_ref[0], zeta_ref[0], xi_ref[0], cd_ref[0]

        def rotated(n):
            rows = pl.ds(n * CHUNK, CHUNK)
            cosf, sins = cos_ref[rows, :], sin_ref[rows, :]
            qr = _rot(z_ref[rows, _RQ].astype(F32), cosf, sins)
            kr_b = (_rot(z_ref[rows, _RK].astype(F32), cosf, sins) * _RET_KSCALE).astype(BF16)
            return rows, cosf, sins, qr, kr_b

        state = jnp.zeros((RET_DK, RET_DV), F32)
        for n in range(NCHUNK):
            st_ref[n] = state.astype(BF16)
            if n + 1 < NCHUNK:
                rows, _, _, _, kr_b = rotated(n)
                state = state * cdec + _dot_tn(kr_b, (z_ref[rows, _RV].astype(F32) * zeta).astype(BF16))

        grad_state = jnp.zeros((RET_DK, RET_DV), F32)
        for n in reversed(range(NCHUNK)):
            rows, cosf, sins, qr, kr_b = rotated(n)
            qr_b = qr.astype(BF16)
            v_b = z_ref[rows, _RV]
            gs_b = grad_state.astype(BF16)
            o = raw_ref[rows, :]
            r = lax.rsqrt(jnp.mean(o * o, axis=-1, keepdims=True) + EPS)
            hn = o * r
            sg, dsg = _silu_parts(z_ref[rows, _RG].astype(F32))
            dyn = dy_ref[rows, :].astype(F32)
            dz_ref[rows, _RG] = (dyn * hn * dsg).astype(BF16)
            dhn = dyn * sg
            do_b = (r * (dhn - hn * jnp.mean(dhn * hn, axis=-1, keepdims=True))).astype(BF16)
            a_b = (_dot_nt(qr_b, kr_b) * dmask).astype(BF16)
            da_b = (_dot_nt(do_b, v_b) * dmask).astype(BF16)
            dqr = _dot(da_b, kr_b) + xi * _dot_nt(do_b, st_ref[n])
            dkr = _dot_tn(da_b, qr_b) + zeta * _dot_nt(v_b, gs_b)
            dv = _dot_tn(a_b, do_b) + zeta * _dot(kr_b, gs_b)
            grad_state = grad_state * cdec + _dot_tn((qr * xi).astype(BF16), do_b)
            dz_ref[rows, _RQ] = _rot_t(dqr, cosf, sins).astype(BF16)
            dz_ref[rows, _RK] = (_rot_t(dkr, cosf, sins) * _RET_KSCALE).astype(BF16)
            dz_ref[rows, _RV] = dv.astype(BF16)

    wide = pl.BlockSpec((T, RET_DV), lambda h: (0, h))
    return pl.pallas_call(
        body, name="ret_bwd", grid=(RET_HEADS,),
        in_specs=[_RET_Z_SPEC] + _RET_TABLE_SPECS + [wide, wide],
        out_specs=_RET_Z_SPEC,
        out_shape=jax.ShapeDtypeStruct((T, D_IN_PAD), BF16),
        scratch_shapes=[pltpu.VMEM((NCHUNK, RET_DK, RET_DV), BF16)],
        compiler_params=_params(dimension_semantics=("arbitrary",)),
    )(z, *tables, raw, dy)


def _adamw(w, g, m, v):
    m = ADAM_B1 * m + (1.0 - ADAM_B1) * g
    v = ADAM_B2 * v + (1.0 - ADAM_B2) * (g * g)
    m_hat = m / (1.0 - ADAM_B1 ** ADAM_STEP)
    v_hat = v / (1.0 - ADAM_B2 ** ADAM_STEP)
    delta = -ADAM_LR * (m_hat / (jnp.sqrt(v_hat) + ADAM_EPS) + ADAM_WD * w)
    return delta, m, v


def _sum_adamw(parts, w, m, v, rows, name):
    _, r_tot, cols = parts.shape
    assert r_tot % rows == 0

    def body(p_ref, w_ref, m_ref, v_ref, g_ref, d_ref, nm_ref, nv_ref):
        g = p_ref[0].astype(F32)
        for d in range(1, N_DEV):
            g = g + p_ref[d].astype(F32)
        delta, nm, nv = _adamw(w_ref[...], g, m_ref[...], v_ref[...])
        g_ref[...] = g
        d_ref[...] = delta
        nm_ref[...] = nm
        nv_ref[...] = nv

    blk = pl.BlockSpec((rows, cols), lambda i: (i, 0))
    return pl.pallas_call(
        body, name=name, grid=(r_tot // rows,),
        in_specs=[pl.BlockSpec((N_DEV, rows, cols), lambda i: (0, i, 0)), blk, blk, blk],
        out_specs=[blk] * 4,
        out_shape=[jax.ShapeDtypeStruct((r_tot, cols), F32)] * 4,
        compiler_params=_params(dimension_semantics=("arbitrary",)),
    )(parts, w, m, v)


def _sum_adamw_w_in(parts, w, m, v, small):
    n_part, r, c = parts.shape
    steps = c // 128

    def body(p_ref, w_hbm, m_hbm, v_hbm, s_ref, g_hbm, d_hbm, nm_hbm, nv_hbm, got_ref,
             in_buf, out_buf, in_sems, out_sems, send_sems, recv_sems, local_sems):
        start, wait = _direct_exchange([s_ref], [got_ref], send_sems, recv_sems, local_sems, True)
        i = pl.program_id(0)
        pl.when(i == 0)(start)
        slot = i % 2

        def loads(step, into):
            cols = pl.ds(pl.multiple_of(step * 128, 128), 128)
            return [pltpu.make_async_copy(h.at[:, 0, cols], in_buf.at[into, k], in_sems.at[into, k])
                    for k, h in enumerate((w_hbm, m_hbm, v_hbm))]

        cols = pl.ds(pl.multiple_of(i * 128, 128), 128)
        stores = [pltpu.make_async_copy(out_buf.at[k], h.at[:, 0, cols], out_sems.at[k])
                  for k, h in enumerate((g_hbm, d_hbm, nm_hbm, nv_hbm))]

        @pl.when(i == 0)
        def _():
            for cp in loads(0, 0):
                cp.start()

        @pl.when(i + 1 < steps)
        def _():
            for cp in loads(i + 1, 1 - slot):
                cp.start()

        g = p_ref[0].astype(F32)
        for d in range(1, n_part):
            g = g + p_ref[d].astype(F32)
        for cp in loads(i, slot):
            cp.wait()
        delta, nm, nv = _adamw(in_buf[slot, 0], g, in_buf[slot, 1], in_buf[slot, 2])

        @pl.when(i > 0)
        def _():
            for cp in stores:
                cp.wait()

        for k, val in enumerate((g, delta, nm, nv)):
            out_buf[k] = val
        for cp in stores:
            cp.start()

        @pl.when(i == steps - 1)
        def _():
            for cp in stores:
                cp.wait()
            wait()

    return pl.pallas_call(
        body, name="adamw_w_in", grid=(steps,),
        in_specs=[pl.BlockSpec((n_part, r, 128), lambda i: (0, 0, i)), ANY, ANY, ANY, ANY],
        out_specs=[ANY] * 5,
        out_shape=[jax.ShapeDtypeStruct((r, 1, c), F32)] * 4 + [_exchange_shape(small, True)],
        scratch_shapes=[pltpu.VMEM((2, 3, r, 128), F32), pltpu.VMEM((4, r, 128), F32),
                        pltpu.SemaphoreType.DMA((2, 3)), pltpu.SemaphoreType.DMA((4,))] + _exchange_sems(1),
        compiler_params=_params(dimension_semantics=("arbitrary",)),
    )(parts, w, m, v, small)


def _adamw_small(got, me, metas, norms, finals, biases):
    def body(me_ref, gm_ref, gr_ref, *refs):
        ins, outs = refs[:12], refs[12:]
        g_meta, g_rest = gm_ref[0], gr_ref[0]
        for d in range(1, N_DEV):
            g_meta, g_rest = g_meta + gm_ref[d], g_rest + gr_ref[d]
        grads = [g_meta, g_rest[0:1], g_rest[1:2], g_rest[2:3, :FOX_HEADS]]
        for k, g in enumerate(grads):
            w_ref, m_ref, v_ref = ins[3 * k:3 * k + 3]
            delta, new_m, new_v = _adamw(w_ref[...], g, m_ref[...], v_ref[...])
            for o_ref, val in zip(outs[4 * k:4 * k + 4], (g, delta, new_m, new_v)):
                o_ref[...] = val
        outs[16][...] = g_rest[3:4, :128]

    groups = (metas, norms, finals, biases)
    full = lambda a: pl.BlockSpec(a.shape, lambda i, me_ref: (0,) * a.ndim)
    flat = [a for grp in groups for a in grp]
    res = pl.pallas_call(
        body, name="adamw_small",
        grid_spec=pltpu.PrefetchScalarGridSpec(
            num_scalar_prefetch=1, grid=(1,),
            in_specs=[pl.BlockSpec((N_DEV, N_META, META_BLK), lambda i, me_ref: (0, 0, me_ref[0])),
                      pl.BlockSpec((N_DEV, 8, D_MODEL), lambda i, me_ref: (0, N_META // 8, 0))]
            + [full(a) for a in flat],
            out_specs=[full(grp[0]) for grp in groups for _ in range(4)]
            + [pl.BlockSpec((1, 128), lambda i, me_ref: (0, 0))]),
        out_shape=[jax.ShapeDtypeStruct(grp[0].shape, F32) for grp in groups for _ in range(4)]
        + [jax.ShapeDtypeStruct((1, 128), F32)],
        compiler_params=_params(dimension_semantics=("arbitrary",)),
    )(me, got, got, *flat)
    return [res[4 * k:4 * k + 4] for k in range(4)], res[16]


def kernel(x, meta_tokens, norm_g, w_in, b_f, w_out, final_g, loss_target, m_meta_tokens, m_norm_g, m_w_in, m_b_f, m_w_out, m_final_g, v_meta_tokens, v_norm_g, v_w_in, v_b_f, v_w_out, v_final_g):
    core = lax.axis_index("c")
    me = 4 * lax.axis_index("x") + 2 * lax.axis_index("y") + core
    tables = _tables()

    wt_all, meta_all = _gather_two_level([_slab(w_in[0].T.astype(BF16), me), meta_tokens], name="gather_w_in")
    slabs, tail = _join_edges(wt_all)
    meta_full = jnp.transpose(meta_all, (1, 0, 2)).reshape(N_META, D_MODEL)
    b_pad = jnp.pad(b_f, ((0, 0), (0, 128 - FOX_HEADS)))

    h_pad, u, z = _rms_z_ret(x[0], meta_full, norm_g, slabs)
    z, zff = _z_fox(u, slabs, tail, z)
    bias = _forget_fwd(zff, b_pad)[:FOX_HEADS].reshape(FOX_HEADS, 1, T)
    raw, y = _ret_fwd(z, tables)
    a_f, lse, y, w_out_all = _fox_fwd(z, bias, y, w_out[0].astype(BF16))
    w_out_b = w_out_all.reshape(D_MIX, D_MODEL)
    dout, dout_b, dy, loss_blk, d_final_g = _out_loss_dy(y, w_out_b, h_pad, loss_target[0],
                                                         final_g.reshape(1, D_MODEL))

    d_w_out = _mm_tn(y, dout_b, tm=D_MIX, tn=256, name="mm_dwout")
    dz = _ret_bwd(z, tables, raw, dy)
    dz, dc, got_w_out = _fox_bwd(z, bias, a_f, lse, dy, dz, d_w_out.reshape(N_DEV, WO_BLK, D_MODEL))
    dz, db_f = _forget_bwd(zff, b_pad, dc.reshape(FOX_HEADS, T), dz)

    kept, via, direct = _dwin_pair_slabs(dz, u, core)
    dh, d_norm_g, summed = _du_rms(dz, slabs, tail, h_pad, dout, norm_g, kept, via)
    got_w_in = _unslab([kept[1], direct, summed], me)

    small = jnp.concatenate([
        dh[PAD:CHUNK], d_norm_g, d_final_g, jnp.pad(db_f[:, :FOX_HEADS], ((0, 0), (0, D_MODEL - FOX_HEADS))),
        jnp.pad(loss_blk[0:1], ((0, 0), (0, D_MODEL - 128))),
        jnp.zeros((SMALL_ROWS - N_META - 4, D_MODEL), F32)], axis=0)
    fore = lambda a: jnp.transpose(a, (2, 0, 1))
    g_w_in, d_w_in, nm_w_in, nv_w_in, got_small = _sum_adamw_w_in(
        got_w_in, fore(w_in), fore(m_w_in), fore(v_w_in), small)
    g_w_out, d_w_out, nm_w_out, nv_w_out = _sum_adamw(got_w_out, w_out[0], m_w_out[0], v_w_out[0], 128, "adamw_w_out")

    row = lambda a: a.reshape(1, D_MODEL)
    (meta_o, norm_o, final_o, bias_o), loss_row = _adamw_small(
        got_small, me.astype(jnp.int32).reshape(1),
        (meta_tokens, m_meta_tokens, v_meta_tokens), (norm_g, m_norm_g, v_norm_g),
        (row(final_g), row(m_final_g), row(v_final_g)), (b_f, m_b_f, v_b_f))
    final_o = [a.reshape(D_MODEL) for a in final_o]

    back = lambda a: jnp.transpose(a, (1, 2, 0))
    outs = [[meta_o[k], norm_o[k], back(wk), bias_o[k], ok[None], final_o[k]]
            for k, (wk, ok) in enumerate(zip((g_w_in, d_w_in, nm_w_in, nv_w_in),
                                             (g_w_out, d_w_out, nm_w_out, nv_w_out)))]
    return (loss_row[0, 0], dh[CHUNK:][None], *outs[0], *outs[1], *outs[2], *outs[3])
```

```python
import numpy as np
import jax
import jax.numpy as jnp
from jax import lax
from jax.experimental import pallas as pl
from jax.experimental.pallas import tpu as pltpu

F32 = jnp.float32
BF16 = jnp.bfloat16

N_DEV = 8
N_CHIP = 4
D_MODEL = 1024
SEQ = 2048
N_META = 16
CHUNK = 128
PAD = CHUNK - N_META
T = SEQ + CHUNK
NCHUNK = T // CHUNK
D_MIX = 2048
RET_HEADS = 4
RET_DK = 128
RET_DV = 256
RET_W = 2 * RET_DK + 2 * RET_DV
FOX_HEADS = 16
FOX_D = 64
FOX_PAIRS = FOX_HEADS // 2
FOX_W = 4 * 128
FOX_BASE = RET_HEADS * RET_W
FF_BASE = FOX_BASE + FOX_PAIRS * FOX_W
D_IN = 7184
D_IN_PAD = 7296
W_BLK = D_IN // N_DEV
WO_BLK = D_MIX // N_DEV
META_BLK = D_MODEL // N_DEV
EPS = 1e-6
NEG_INF = -1e30
ROPE_BASE = 10000.0
LOG2E = 1.4426950408889634
LN2 = 0.6931471805599453

ADAM_LR = 0.001
ADAM_B1 = 0.9
ADAM_B2 = 0.999
ADAM_EPS = 1e-08
ADAM_WD = 0.01
ADAM_STEP = 10

SMALL_ROWS = 24
VMEM_LIMIT = 56 * 1024 * 1024
MESH = pl.DeviceIdType.MESH
ANY = pl.BlockSpec(memory_space=pl.ANY)

_NT = (((1,), (1,)), ((), ()))
_TN = (((0,), (0,)), ((), ()))


def _dot(a, b):
    return jnp.dot(a, b, preferred_element_type=F32)


def _dot_nt(a, b):
    return lax.dot_general(a, b, _NT, preferred_element_type=F32)


def _dot_tn(a, b):
    return lax.dot_general(a, b, _TN, preferred_element_type=F32)


def _params(**kw):
    return pltpu.CompilerParams(vmem_limit_bytes=VMEM_LIMIT, **kw)


def _silu_parts(g):
    sig = jax.nn.sigmoid(g)
    return g * sig, sig * (1.0 + g * (1.0 - sig))


def _tables():
    pos = np.arange(T, dtype=np.float32) - PAD
    inv = (ROPE_BASE ** (-np.arange(0, RET_DK, 2, dtype=np.float32) / RET_DK)).astype(np.float32)
    ang = pos[:, None] * inv[None, :]
    cos, sin = np.cos(ang), np.sin(ang)
    cosf = np.concatenate([cos, cos], axis=1).astype(np.float32)
    sins = np.concatenate([-sin, sin], axis=1).astype(np.float32)
    h = np.arange(RET_HEADS, dtype=np.float32)
    log_gamma = np.log1p(-np.exp2(-5.0 - h)).astype(np.float32)
    idx = np.arange(CHUNK, dtype=np.float32)
    diff = idx[:, None] - idx[None, :]
    dmask = np.where(diff[None] >= 0,
                     np.exp(log_gamma[:, None, None] * np.maximum(diff, 0.0)[None]), 0.0)
    zeta = np.exp(log_gamma[:, None] * (CHUNK - 1.0 - idx)[None, :])
    xi = np.exp(log_gamma[:, None] * (idx + 1.0)[None, :])
    cdec = np.exp(log_gamma * CHUNK)
    return (jnp.asarray(cosf), jnp.asarray(sins), jnp.asarray(dmask, F32),
            jnp.asarray(zeta[:, :, None], F32), jnp.asarray(xi[:, :, None], F32),
            jnp.asarray(cdec[:, None, None], F32))


W_STRIDE = 896
W_SLAB = 912
W_EDGE = W_SLAB - W_STRIDE
def _slab(block, me):
    shift = W_BLK - W_STRIDE
    return lax.switch(me, [lambda b, d=d: jnp.pad(b, ((shift * d, W_SLAB - W_BLK - shift * d), (0, 0)))
                           for d in range(N_DEV)], block)


def _unslab(slabs, me):
    shift = W_BLK - W_STRIDE
    return lax.switch(me, [lambda *s, d=d: jnp.stack([a[shift * d:shift * d + W_BLK] for a in s])
                           for d in range(N_DEV)], *slabs)


def _join_edges(slabs):
    last = slabs[:, W_STRIDE:]
    first = slabs[:, :W_EDGE] + jnp.concatenate([jnp.zeros_like(last[:1]), last[:-1]], axis=0)
    tail = jnp.pad(last[N_DEV - 1], ((0, 128 - W_EDGE), (0, 0)))
    return lax.dynamic_update_slice(slabs, first, (0, 0, 0)), tail


def _mm_tn(a, b, *, tm, tn, name):
    k, m = a.shape
    n = b.shape[1]
    assert m % tm == 0 and n % tn == 0

    def body(a_ref, b_ref, o_ref):
        o_ref[...] = _dot_tn(a_ref[...], b_ref[...]).astype(BF16)

    return pl.pallas_call(
        body, name=name, grid=(n // tn, m // tm),
        in_specs=[pl.BlockSpec((k, tm), lambda j, i: (0, i)),
                  pl.BlockSpec((k, tn), lambda j, i: (0, j))],
        out_specs=pl.BlockSpec((tm, tn), lambda j, i: (i, j)),
        out_shape=jax.ShapeDtypeStruct((m, n), BF16),
        compiler_params=_params(dimension_semantics=("arbitrary", "arbitrary")),
    )(a, b)


def _piece_spec(base, mult):
    def index(i):
        p = base + mult * i
        return p // 7, p % 7, 0
    return pl.BlockSpec((1, 128, D_MODEL), index)


_RET_PIECES = ((0, 1), (4, 1), (8, 2), (9, 2), (16, 2), (17, 2))
_FOX_PIECES = ((24, 1), (32, 1), (40, 1), (48, 1))


def _rms_z_ret(x, meta, g, slabs):
    def body(x_hbm, m_ref, g_ref, *refs):
        pieces, h_ref, u_ref, z_ref, x_sem = refs[:6], refs[6], refs[7], refs[8], refs[9]

        @pl.when(pl.program_id(0) == 0)
        def _():
            tokens = pltpu.make_async_copy(x_hbm, h_ref.at[pl.ds(CHUNK, SEQ)], x_sem)
            tokens.start()
            h_ref[pl.ds(0, PAD), :] = jnp.zeros((PAD, D_MODEL), F32)
            h_ref[pl.ds(PAD, N_META), :] = m_ref[...]
            tokens.wait()
            h = h_ref[...]
            r = lax.rsqrt(jnp.mean(h * h, axis=-1, keepdims=True) + EPS)
            u_ref[...] = (h * r * g_ref[...]).astype(BF16)

        w = jnp.concatenate([p[0] for p in pieces], axis=0)
        z_ref[...] = _dot_nt(u_ref[...], w).astype(BF16)

    whole = pl.BlockSpec((T, D_MODEL), lambda i: (0, 0))
    return pl.pallas_call(
        body, name="rms_z_ret", grid=(RET_HEADS,),
        in_specs=[ANY, pl.BlockSpec((N_META, D_MODEL), lambda i: (0, 0)),
                  pl.BlockSpec((1, D_MODEL), lambda i: (0, 0))]
        + [_piece_spec(*bm) for bm in _RET_PIECES],
        out_specs=[whole, whole, pl.BlockSpec((T, RET_W), lambda i: (0, i))],
        out_shape=[jax.ShapeDtypeStruct((T, D_MODEL), F32),
                   jax.ShapeDtypeStruct((T, D_MODEL), BF16),
                   jax.ShapeDtypeStruct((T, D_IN_PAD), BF16)],
        scratch_shapes=[pltpu.SemaphoreType.DMA],
        compiler_params=_params(dimension_semantics=("arbitrary",)),
    )(x, meta, g, *([slabs] * 6))


def _z_fox(u, slabs, tail, z):
    def body(u_ref, *refs):
        pieces, t_ref, z_ref, zff_ref = refs[:4], refs[4], refs[6], refs[7]
        u = u_ref[...]
        w = jnp.concatenate([p[0] for p in pieces], axis=0)
        z_ref[...] = _dot_nt(u, w).astype(BF16)

        @pl.when(pl.program_id(0) == 0)
        def _():
            zff_ref[...] = _dot_nt(u, t_ref[...])

    return pl.pallas_call(
        body, name="z_fox", grid=(FOX_PAIRS,),
        in_specs=[pl.BlockSpec((T, D_MODEL), lambda i: (0, 0))] + [_piece_spec(*bm) for bm in _FOX_PIECES]
        + [pl.BlockSpec((128, D_MODEL), lambda i: (0, 0)), ANY],
        out_specs=[_FOX_Z_SPEC, pl.BlockSpec((T, 128), lambda i: (0, 0))],
        out_shape=[jax.ShapeDtypeStruct((T, D_IN_PAD), BF16),
                   jax.ShapeDtypeStruct((T, 128), F32)],
        input_output_aliases={6: 0},
        compiler_params=_params(dimension_semantics=("arbitrary",)),
    )(u, *([slabs] * 4), tail, z)


def _out_loss_dy(y, w_out_b, h_pad, target, g):
    tm = T // 4

    def body(y_ref, w_ref, h_ref, t_hbm, g_ref, d_ref, db_ref, dy_ref, loss_ref, dg_ref, t_buf, t_sem):
        i = pl.program_id(0)
        head = pltpu.make_async_copy(t_hbm.at[pl.ds(0, tm - CHUNK)], t_buf.at[pl.ds(CHUNK, tm - CHUNK)], t_sem)
        rest = pltpu.make_async_copy(t_hbm.at[pl.ds(pl.multiple_of(jnp.maximum(i, 1) * tm - CHUNK, 8), tm)],
                                     t_buf, t_sem)

        @pl.when(i == 0)
        def _():
            t_buf[pl.ds(0, CHUNK), :] = jnp.zeros((CHUNK, D_MODEL), F32)
            head.start()
            loss_ref[...] = jnp.zeros_like(loss_ref)
            dg_ref[...] = jnp.zeros_like(dg_ref)

        pl.when(i > 0)(rest.start)

        w = w_ref[...]
        o = _dot(y_ref[...], w) + h_ref[...]
        pl.when(i == 0)(head.wait)
        pl.when(i > 0)(rest.wait)
        token = lax.broadcasted_iota(jnp.int32, (tm, 1), 0) + i * tm >= CHUNK
        g = g_ref[...]
        r = lax.rsqrt(jnp.mean(o * o, axis=-1, keepdims=True) + EPS)
        xn = o * r
        e = jnp.where(token, xn * g - t_buf[...], 0.0)
        loss_ref[...] += jnp.full(loss_ref.shape, 0.5 / D_MODEL * jnp.sum(e * e), F32)
        do = e * (1.0 / D_MODEL)
        dg_ref[...] += jnp.sum(do * xn, axis=0, keepdims=True)
        dn = do * g
        d = r * (dn - xn * jnp.mean(dn * xn, axis=-1, keepdims=True))
        d_b = d.astype(BF16)
        d_ref[...] = d
        db_ref[...] = d_b
        dy_ref[...] = _dot_nt(d_b, w).astype(BF16)

    tile = pl.BlockSpec((tm, D_MODEL), lambda i: (i, 0))
    wide = pl.BlockSpec((tm, D_MIX), lambda i: (i, 0))
    return pl.pallas_call(
        body, name="out_loss_dy", grid=(T // tm,),
        in_specs=[wide, pl.BlockSpec((D_MIX, D_MODEL), lambda i: (0, 0)), tile, ANY,
                  pl.BlockSpec((1, D_MODEL), lambda i: (0, 0))],
        out_specs=[tile, tile, wide,
                   pl.BlockSpec((8, 128), lambda i: (0, 0)),
                   pl.BlockSpec((1, D_MODEL), lambda i: (0, 0))],
        out_shape=[jax.ShapeDtypeStruct((T, D_MODEL), F32),
                   jax.ShapeDtypeStruct((T, D_MODEL), BF16),
                   jax.ShapeDtypeStruct((T, D_MIX), BF16),
                   jax.ShapeDtypeStruct((8, 128), F32),
                   jax.ShapeDtypeStruct((1, D_MODEL), F32)],
        scratch_shapes=[pltpu.VMEM((tm, D_MODEL), F32), pltpu.SemaphoreType.DMA],
        compiler_params=_params(dimension_semantics=("arbitrary",)),
    )(y, w_out_b, h_pad, target, g)


def _coords():
    return lax.axis_index("x"), lax.axis_index("y"), lax.axis_index("c")


def _flip(v, bit):
    return 1 - v if bit else v


def _peer(x, y, c, r):
    return _flip(x, (r >> 2) & 1), _flip(y, (r >> 1) & 1), _flip(c, r & 1)


def _direct_exchange(ins, outs, send_sems, recv_sems, local_sems, gather):
    x, y, c = _coords()
    me = 4 * x + 2 * y + c

    def src(k, to_idx):
        return ins[k] if gather else ins[k].at[to_idx]

    local = [pltpu.make_async_copy(src(k, me), outs[k].at[me], local_sems.at[k])
             for k in range(len(ins))]
    sends, recvs = [], []
    for r in range(1, N_DEV):
        px, py, pc = _peer(x, y, c, r)
        peer = 4 * px + 2 * py + pc
        for k in range(len(ins)):
            sems = dict(send_sem=send_sems.at[k, r - 1], recv_sem=recv_sems.at[k, r - 1],
                        device_id=(px, py, pc), device_id_type=MESH)
            sends.append(pltpu.make_async_remote_copy(src_ref=src(k, peer), dst_ref=outs[k].at[me], **sems))
            recvs.append(pltpu.make_async_remote_copy(src_ref=src(k, peer), dst_ref=outs[k].at[peer], **sems))

    def start():
        for cp in local + sends:
            cp.start()

    def wait():
        for cp in recvs:
            cp.wait_recv()
        for cp in sends:
            cp.wait_send()
        for cp in local:
            cp.wait()

    return start, wait


def _exchange_sems(n_arr):
    return [pltpu.SemaphoreType.DMA((n_arr, N_DEV - 1)), pltpu.SemaphoreType.DMA((n_arr, N_DEV - 1)),
            pltpu.SemaphoreType.DMA((n_arr,))]


def _exchange_shape(a, gather):
    return jax.ShapeDtypeStruct(((N_DEV,) + a.shape) if gather else a.shape, a.dtype)


def _gather_two_level(arrays, name):
    n_arr = len(arrays)

    def body(*refs):
        ins, outs = refs[:n_arr], refs[n_arr:2 * n_arr]
        send_sems, recv_sems, local_sems = refs[2 * n_arr:]
        x, y, c = _coords()

        def slot(k, px, py, pc):
            return outs[k].at[4 * px + 2 * py + pc]

        def routed(core):
            me, sibling = (x, y, core), (x, y, 1 - core)
            xn, yn, dg = (1 - x, y), (x, 1 - y), (1 - x, 1 - y)
            (first, s_first), (second, s_second) = ((xn, 1), (yn, 2)) if core == 0 else ((yn, 2), (xn, 1))

            def copy(k, j, block, to, own=False):
                return pltpu.make_async_remote_copy(
                    src_ref=ins[k] if own else slot(k, *block), dst_ref=slot(k, *block),
                    send_sem=send_sems.at[k, j], recv_sem=recv_sems.at[k, j],
                    device_id=to, device_id_type=MESH)

            local = [pltpu.make_async_copy(ins[k], slot(k, *me), local_sems.at[k]) for k in range(n_arr)]
            sent = []
            for k in range(n_arr):
                sent += [copy(k, 0, me, sibling, True), copy(k, 1, me, (*xn, core), True),
                         copy(k, 2, me, (*yn, core), True)]
            for cp in local + sent:
                cp.start()

            def pass_on(k, j_from, j_to, block, targets):
                copy(k, j_from, block, me).wait_recv()
                for j, to in zip(j_to, targets):
                    cp = copy(k, j, block, to)
                    cp.start()
                    sent.append(cp)

            for k in range(n_arr):
                pass_on(k, s_first, (3, 3 + s_first), (*first, core), ((*second, core), sibling))
            for k in range(n_arr):
                pass_on(k, s_second, (3 + s_second,), (*second, core), (sibling,))
            for k in range(n_arr):
                pass_on(k, 3, (6,), (*dg, core), (sibling,))
            for k in range(n_arr):
                copy(k, 0, sibling, me).wait_recv()
                for j, chip in ((4, xn), (5, yn), (6, dg)):
                    copy(k, j, (*chip, 1 - core), me).wait_recv()
            for cp in sent:
                cp.wait_send()
            for cp in local:
                cp.wait()

        for core in (0, 1):
            pl.when(c == core)(lambda core=core: routed(core))

    return pl.pallas_call(
        body, name=name,
        in_specs=[ANY] * n_arr, out_specs=[ANY] * n_arr,
        out_shape=[_exchange_shape(a, True) for a in arrays],
        scratch_shapes=_exchange_sems(n_arr),
    )(*arrays)


def _piece_columns():
    pos = {}
    for h in range(RET_HEADS):
        for k, p in enumerate((h, 4 + h, 8 + 2 * h, 9 + 2 * h, 16 + 2 * h, 17 + 2 * h)):
            pos[p] = 6 * h + k
    for p in range(FOX_PAIRS):
        for i in range(4):
            pos[24 + 8 * i + p] = 24 + 4 * p + i
    pos[D_IN_PAD // 128 - 1] = D_IN_PAD // 128 - 1
    return np.array([pos[7 * d + j] for d in range(N_DEV) for j in range(8)], np.int32)


def _routes(core):
    x, y, _ = _coords()
    return ((1 - x, y), (x, 1 - y)) if core == 0 else ((x, 1 - y), (1 - x, y))


def _dwin_pair_slabs(dz, u, core):
    def body(order_ref, cols_ref, *refs):
        pieces, u_ref = refs[:8], refs[8]
        kept_ref, theirs_ref, via_ref, direct_ref = refs[9:13]
        send_buf, keep_buf, got_buf, push_send, push_recv, hop_send, hop_recv, load_sem = refs[13:]
        s = pl.program_id(0)
        x, y, c = _coords()
        cols = jnp.concatenate([p[...] for p in pieces[:7]] + [pieces[7][:, :W_EDGE]], axis=1)
        slab = _dot_tn(cols, u_ref[...])

        def push(k):
            return pltpu.make_async_remote_copy(
                src_ref=send_buf.at[k], dst_ref=theirs_ref.at[k],
                send_sem=push_send.at[k], recv_sem=push_recv.at[k],
                device_id=(x, y, 1 - c), device_id_type=MESH)

        def hop(k, core):
            first, _ = _routes(core)
            return pltpu.make_async_remote_copy(
                src_ref=keep_buf.at[k], dst_ref=direct_ref if k == 0 else via_ref,
                send_sem=hop_send.at[k], recv_sem=hop_recv.at[k],
                device_id=(*first, core), device_id_type=MESH)

        for k in range(N_CHIP):
            @pl.when(s == 2 * k)
            def _(k=k):
                send_buf[k] = slab.astype(BF16)
                push(k).start()

            @pl.when(s == 2 * k + 1)
            def _(k=k):
                push(k).wait_recv()
                load = pltpu.make_async_copy(theirs_ref.at[k], got_buf, load_sem)
                load.start()
                load.wait()
                total = (slab + got_buf[...].astype(F32)).astype(BF16)
                if k < 2:
                    keep_buf[k] = total
                    for core in (0, 1):
                        pl.when(c == core)(lambda core=core: hop(k, core).start())
                else:
                    kept_ref[0] = total

        @pl.when(s == 2 * N_CHIP - 1)
        def _():
            for core in (0, 1):
                @pl.when(c == core)
                def _(core=core):
                    for k in range(2):
                        hop(k, core).wait_recv()
                        hop(k, core).wait_send()
            for k in range(N_CHIP):
                push(k).wait_send()

    x, y = lax.axis_index("x"), lax.axis_index("y")
    xn, yn, dg, own = 2 * (1 - x) + y, 2 * x + 1 - y, 2 * (1 - x) + 1 - y, 2 * x + y
    mine = jnp.where(core == 0, jnp.stack([xn, dg, yn, own]), jnp.stack([yn, dg, xn, own]))
    sibs = jnp.where(core == 0, jnp.stack([yn, dg, xn, own]), jnp.stack([xn, dg, yn, own]))
    order = jnp.stack([2 * sibs + (1 - core), 2 * mine + core], axis=1).reshape(2 * N_CHIP).astype(jnp.int32)
    piece = lambda j: pl.BlockSpec((T, 128), lambda s, order_ref, cols_ref: (0, cols_ref[order_ref[s] * 8 + j]))
    slab = jax.ShapeDtypeStruct((W_SLAB, D_MODEL), BF16)
    kept, _, via, direct = pl.pallas_call(
        body, name="dwin_pair_slabs",
        grid_spec=pltpu.PrefetchScalarGridSpec(
            num_scalar_prefetch=2, grid=(2 * N_CHIP,),
            in_specs=[piece(j) for j in range(8)]
            + [pl.BlockSpec((T, D_MODEL), lambda s, order_ref, cols_ref: (0, 0))],
            out_specs=[pl.BlockSpec((1, W_SLAB, D_MODEL),
                                    lambda s, order_ref, cols_ref: (jnp.where(s >= 6, 1, 0), 0, 0)),
                       ANY, ANY, ANY],
            scratch_shapes=[pltpu.VMEM((N_CHIP, W_SLAB, D_MODEL), BF16), pltpu.VMEM((2, W_SLAB, D_MODEL), BF16),
                            pltpu.VMEM((W_SLAB, D_MODEL), BF16),
                            pltpu.SemaphoreType.DMA((N_CHIP,)), pltpu.SemaphoreType.DMA((N_CHIP,)),
                            pltpu.SemaphoreType.DMA((2,)), pltpu.SemaphoreType.DMA((2,)),
                            pltpu.SemaphoreType.DMA]),
        out_shape=[jax.ShapeDtypeStruct((2, W_SLAB, D_MODEL), BF16),
                   jax.ShapeDtypeStruct((N_CHIP, W_SLAB, D_MODEL), BF16), slab, slab],
        compiler_params=_params(dimension_semantics=("arbitrary",)),
    )(order, jnp.asarray(_piece_columns()), *([dz] * 8), u)
    return kept, via, direct


def _du_rms(dz, slabs, tail, h_pad, dout, g, kept, via):
    tm = 272
    steps = T // tm
    columns = _piece_columns().reshape(N_DEV, 8)

    def body(dz_ref, w_ref, t_ref, h_ref, d_ref, g_ref, kept_ref, via_ref, dh_ref, dg_ref, got_ref,
             via_buf, mine_buf, send_sem, recv_sem, local_sems):
        i = pl.program_id(0)
        x, y, c = _coords()

        def summed(core):
            _, second = _routes(core)
            return pltpu.make_async_remote_copy(
                src_ref=mine_buf, dst_ref=got_ref, send_sem=send_sem, recv_sem=recv_sem,
                device_id=(*second, core), device_id_type=MESH)

        @pl.when(i == 0)
        def _():
            loads = [pltpu.make_async_copy(kept_ref.at[0], mine_buf, local_sems.at[0]),
                     pltpu.make_async_copy(via_ref, via_buf, local_sems.at[1])]
            for cp in loads:
                cp.start()
            for cp in loads:
                cp.wait()
            mine_buf[...] = (mine_buf[...].astype(F32) + via_buf[...].astype(F32)).astype(BF16)
            for core in (0, 1):
                pl.when(c == core)(lambda core=core: summed(core).start())
            dg_ref[...] = jnp.zeros_like(dg_ref)

        du = _dot(dz_ref[:, pl.ds(FF_BASE, 128)], t_ref[...])
        for d in range(N_DEV):
            cols = jnp.concatenate([dz_ref[:, pl.ds(128 * int(columns[d, j]), 128)] for j in range(7)], axis=1)
            du = du + _dot(cols, w_ref[d, pl.ds(0, W_STRIDE), :])
        h = h_ref[...]
        r = lax.rsqrt(jnp.mean(h * h, axis=-1, keepdims=True) + EPS)
        xn = h * r
        dg_ref[...] += jnp.sum(du * xn, axis=0, keepdims=True)
        dn = du * g_ref[...]
        dh_ref[...] = d_ref[...] + r * (dn - xn * jnp.mean(dn * xn, axis=-1, keepdims=True))

        for core in (0, 1):
            @pl.when(jnp.logical_and(c == core, i == steps - 1))
            def _(core=core):
                summed(core).wait_recv()
                summed(core).wait_send()

    tile = pl.BlockSpec((tm, D_MODEL), lambda i: (i, 0))
    return pl.pallas_call(
        body, name="du_rms", grid=(steps,),
        in_specs=[pl.BlockSpec((tm, D_IN_PAD), lambda i: (i, 0)),
                  pl.BlockSpec((N_DEV, W_SLAB, D_MODEL), lambda i: (0, 0, 0)),
                  pl.BlockSpec((128, D_MODEL), lambda i: (0, 0)),
                  tile, tile, pl.BlockSpec((1, D_MODEL), lambda i: (0, 0)), ANY, ANY],
        out_specs=[tile, pl.BlockSpec((1, D_MODEL), lambda i: (0, 0)), ANY],
        out_shape=[jax.ShapeDtypeStruct((T, D_MODEL), F32),
                   jax.ShapeDtypeStruct((1, D_MODEL), F32),
                   jax.ShapeDtypeStruct(via.shape, via.dtype)],
        scratch_shapes=[pltpu.VMEM(via.shape, via.dtype), pltpu.VMEM(via.shape, via.dtype),
                        pltpu.SemaphoreType.DMA, pltpu.SemaphoreType.DMA, pltpu.SemaphoreType.DMA((2,))],
        compiler_params=_params(dimension_semantics=("arbitrary",)),
    )(dz, slabs, tail, h_pad, dout, g, kept, via)


def _tri(lower):
    r = lax.broadcasted_iota(jnp.int32, (CHUNK, CHUNK), 0)
    c = lax.broadcasted_iota(jnp.int32, (CHUNK, CHUNK), 1)
    return jnp.where((r >= c) if lower else (r <= c), 1.0, 0.0).astype(F32)


def _row_valid(n):
    r = lax.broadcasted_iota(jnp.int32, (CHUNK, 128), 0) + n * CHUNK
    return r >= PAD


_FF_SPEC = pl.BlockSpec((T, 128), lambda i: (0, FF_BASE // 128))
_ZFF_SPEC = pl.BlockSpec((T, 128), lambda i: (0, 0))


def _forget_fwd(z, b_pad):
    def body(z_ref, b_ref, o_ref):
        tri = _tri(True)
        carry = jnp.zeros((1, 128), F32)
        for n in range(NCHUNK):
            rows = pl.ds(n * CHUNK, CHUNK)
            a = z_ref[rows, :] + b_ref[...]
            lf = -(jnp.maximum(-a, 0.0) + jnp.log(1.0 + jnp.exp(-jnp.abs(a))))
            lf = jnp.where(_row_valid(n), lf, 0.0)
            c = jnp.dot(tri, lf, precision=lax.Precision.HIGHEST,
                        preferred_element_type=F32) + carry
            carry = c[CHUNK - 1:CHUNK, :]
            o_ref[:, rows] = jnp.where(_row_valid(n), c * (-LOG2E), NEG_INF).T

    return pl.pallas_call(
        body, name="forget_fwd", grid=(1,),
        in_specs=[_ZFF_SPEC, pl.BlockSpec((1, 128), lambda i: (0, 0))],
        out_specs=pl.BlockSpec((128, T), lambda i: (0, 0)),
        out_shape=jax.ShapeDtypeStruct((128, T), F32),
        compiler_params=_params(dimension_semantics=("arbitrary",)),
    )(z, b_pad)


def _forget_bwd(z, b_pad, dc, dz):
    def body(z_ref, b_ref, dc_ref, dz_in, dff_ref, db_ref):
        tri = _tri(False)
        carry = jnp.zeros((1, 128), F32)
        db = jnp.zeros((1, 128), F32)
        for n in reversed(range(NCHUNK)):
            rows = pl.ds(n * CHUNK, CHUNK)
            dc_blk = jnp.concatenate([dc_ref[:, rows], jnp.zeros((128 - FOX_HEADS, CHUNK), F32)], axis=0).T
            dlf = jnp.dot(tri, dc_blk, precision=lax.Precision.HIGHEST,
                          preferred_element_type=F32) + carry
            carry = dlf[0:1, :]
            a = z_ref[rows, :] + b_ref[...]
            dff = jnp.where(_row_valid(n), dlf * jax.nn.sigmoid(-a), 0.0)
            dff_ref[rows, :] = dff.astype(BF16)
            db = db + jnp.sum(dff, axis=0, keepdims=True)
        db_ref[...] = db

    return pl.pallas_call(
        body, name="forget_bwd", grid=(1,),
        in_specs=[_ZFF_SPEC, pl.BlockSpec((1, 128), lambda i: (0, 0)),
                  pl.BlockSpec((FOX_HEADS, T), lambda i: (0, 0)), ANY],
        out_specs=[_FF_SPEC, pl.BlockSpec((1, 128), lambda i: (0, 0))],
        out_shape=[jax.ShapeDtypeStruct((T, D_IN_PAD), BF16),
                   jax.ShapeDtypeStruct((1, 128), F32)],
        input_output_aliases={3: 0},
        compiler_params=_params(dimension_semantics=("arbitrary",)),
    )(z, b_pad, dc, dz)


FOX_QB = 512
FOX_NQB = SEQ // FOX_QB


def _fox_block(b):
    lo = CHUNK + b * FOX_QB
    return pl.ds(lo, FOX_QB), lo, lo + FOX_QB


def _causal_bias():
    r = lax.broadcasted_iota(jnp.int32, (FOX_QB, FOX_QB), 0)
    c = lax.broadcasted_iota(jnp.int32, (FOX_QB, FOX_QB), 1)
    return jnp.where(c <= r, 0.0, NEG_INF).astype(F32)


def _fox_logits(q_blk, k_all, bias, causal, b):
    _, lo, hi = _fox_block(b)
    here = bias[:, lo:lo + 1]
    s_off = _dot_nt(q_blk, k_all[:lo]) + (bias[:, :lo] - here)
    s_dia = _dot_nt(q_blk, k_all[lo:hi]) + ((bias[:, lo:hi] - here) + causal)
    return s_off, s_dia


_FOX_Z_SPEC = pl.BlockSpec((T, FOX_W), lambda p: (0, FOX_BASE // FOX_W + p))
_FOX_BIAS_SPEC = pl.BlockSpec((2, 1, T), lambda p: (p, 0, 0))
_FOX_LSE_SPEC = pl.BlockSpec((2, T, 1), lambda p: (p, 0, 0))
_FOX_SCALE = FOX_D ** -0.5
_FOX_QSCALE = _FOX_SCALE * LOG2E


def _fox_fwd(z, bias, y, w_out_blk):
    last = FOX_PAIRS - 1

    def body(z_ref, b_ref, y_in, w_ref, a_ref, lse_ref, y_ref, wall_ref,
             send_sems, recv_sems, local_sems):
        start, wait = _direct_exchange([w_ref], [wall_ref], send_sems, recv_sems, local_sems, True)
        pl.when(pl.program_id(0) == 0)(start)

        causal = _causal_bias()
        a_ref[pl.ds(0, CHUNK), :] = jnp.zeros((CHUNK, 128), F32)
        y_ref[pl.ds(0, CHUNK), :] = jnp.zeros((CHUNK, 128), BF16)
        for j in range(2):
            lanes = pl.ds(j * FOX_D, FOX_D)
            k_all = z_ref[:, pl.ds(128 + j * FOX_D, FOX_D)]
            v_all = z_ref[:, pl.ds(256 + j * FOX_D, FOX_D)]
            bias = b_ref[j]
            lse_ref[j, pl.ds(0, CHUNK), :] = jnp.zeros((CHUNK, 1), F32)
            for b in range(FOX_NQB):
                rows, lo, hi = _fox_block(b)
                q_blk = (z_ref[rows, lanes].astype(F32) * _FOX_QSCALE).astype(BF16)
                s_off, s_dia = _fox_logits(q_blk, k_all, bias, causal, b)
                m = jnp.maximum(jnp.max(s_off, axis=-1, keepdims=True),
                                jnp.max(s_dia, axis=-1, keepdims=True))
                e_off = jnp.exp2(s_off - m)
                e_dia = jnp.exp2(s_dia - m)
                total = jnp.sum(e_off, axis=-1, keepdims=True) + jnp.sum(e_dia, axis=-1, keepdims=True)
                o = (_dot(e_off.astype(BF16), v_all[:lo]) + _dot(e_dia.astype(BF16), v_all[lo:hi])) / total
                a_ref[rows, lanes] = o
                lse_ref[j, rows, :] = m + jnp.log(total) * LOG2E
                gate = _silu_parts(z_ref[rows, pl.ds(384 + j * FOX_D, FOX_D)].astype(F32))[0]
                y_ref[rows, lanes] = (o * gate).astype(BF16)

        pl.when(pl.program_id(0) == last)(wait)

    return pl.pallas_call(
        body, name="fox_fwd", grid=(FOX_PAIRS,),
        in_specs=[_FOX_Z_SPEC, _FOX_BIAS_SPEC, ANY, ANY],
        out_specs=[pl.BlockSpec((T, 128), lambda p: (0, p)), _FOX_LSE_SPEC,
                   pl.BlockSpec((T, 128), lambda p: (0, 8 + p)), ANY],
        out_shape=[jax.ShapeDtypeStruct((T, FOX_HEADS * FOX_D), F32),
                   jax.ShapeDtypeStruct((FOX_HEADS, T, 1), F32),
                   jax.ShapeDtypeStruct((T, D_MIX), BF16),
                   _exchange_shape(w_out_blk, True)],
        input_output_aliases={2: 2},
        scratch_shapes=_exchange_sems(1),
        compiler_params=_params(dimension_semantics=("arbitrary",)),
    )(z, bias, y, w_out_blk)


def _fox_bwd(z, bias, a_f, lse, dy, dz, dwo_blocks):
    last = FOX_PAIRS - 1

    def body(z_ref, b_ref, a_ref, lse_ref, dy_ref, dz_in, dwo_ref, dz_ref, dc_ref, got_ref,
             kv_acc, dc_acc, send_sems, recv_sems, local_sems):
        start, wait = _direct_exchange([dwo_ref], [got_ref], send_sems, recv_sems, local_sems, False)
        pl.when(pl.program_id(0) == 0)(start)

        causal = _causal_bias()
        dz_ref[pl.ds(0, CHUNK), pl.ds(0, 128)] = jnp.zeros((CHUNK, 128), BF16)
        dz_ref[pl.ds(0, CHUNK), pl.ds(384, 128)] = jnp.zeros((CHUNK, 128), BF16)
        dk_rows, dv_rows = pl.ds(0, FOX_D), pl.ds(FOX_D, FOX_D)
        for j in range(2):
            lanes = pl.ds(j * FOX_D, FOX_D)
            k_all = z_ref[:, pl.ds(128 + j * FOX_D, FOX_D)]
            v_all = z_ref[:, pl.ds(256 + j * FOX_D, FOX_D)]
            bias = b_ref[j]
            kv_acc[...] = jnp.zeros_like(kv_acc)
            dc_acc[...] = jnp.zeros_like(dc_acc)
            for b in range(FOX_NQB):
                rows, lo, hi = _fox_block(b)
                off, dia = pl.ds(0, lo), pl.ds(lo, FOX_QB)
                q_blk = (z_ref[rows, lanes].astype(F32) * _FOX_QSCALE).astype(BF16)
                s_off, s_dia = _fox_logits(q_blk, k_all, bias, causal, b)
                lse_blk = lse_ref[j, rows, :]
                p_off, p_dia = jnp.exp2(s_off - lse_blk), jnp.exp2(s_dia - lse_blk)
                sg, dsg = _silu_parts(z_ref[rows, pl.ds(384 + j * FOX_D, FOX_D)].astype(F32))
                dyj = dy_ref[rows, lanes].astype(F32)
                dz_ref[rows, pl.ds(384 + j * FOX_D, FOX_D)] = (dyj * a_ref[rows, lanes] * dsg).astype(BF16)
                do_b = (dyj * sg).astype(BF16)
                dp_off = _dot_nt(do_b, v_all[:lo])
                dp_dia = _dot_nt(do_b, v_all[lo:hi])
                d = (jnp.sum(p_off * dp_off, axis=-1, keepdims=True)
                     + jnp.sum(p_dia * dp_dia, axis=-1, keepdims=True))
                ds_off = p_off * (dp_off - d)
                ds_dia = p_dia * (dp_dia - d)
                dc_acc[:, off] -= jnp.sum(ds_off, axis=0, keepdims=True)
                dc_acc[:, dia] -= jnp.sum(ds_dia, axis=0, keepdims=True)
                ds_off_b, ds_dia_b = ds_off.astype(BF16), ds_dia.astype(BF16)
                dq = _dot(ds_off_b, k_all[:lo]) + _dot(ds_dia_b, k_all[lo:hi])
                dz_ref[rows, lanes] = (dq * _FOX_SCALE).astype(BF16)
                kv_acc[dk_rows, off] += _dot_tn(q_blk, ds_off_b)
                kv_acc[dk_rows, dia] += _dot_tn(q_blk, ds_dia_b)
                kv_acc[dv_rows, off] += _dot_tn(do_b, p_off.astype(BF16))
                kv_acc[dv_rows, dia] += _dot_tn(do_b, p_dia.astype(BF16))
            for n in range(NCHUNK):
                rows = pl.ds(n * CHUNK, CHUNK)
                both = kv_acc[:, rows].T
                dz_ref[rows, pl.ds(128 + j * FOX_D, FOX_D)] = (both[:, :FOX_D] * LN2).astype(BF16)
                dz_ref[rows, pl.ds(256 + j * FOX_D, FOX_D)] = both[:, FOX_D:].astype(BF16)
            dc_ref[j] = dc_acc[...]

        pl.when(pl.program_id(0) == last)(wait)

    col = lambda base: pl.BlockSpec((T, 128), lambda p: (0, base + p))
    return pl.pallas_call(
        body, name="fox_bwd", grid=(FOX_PAIRS,),
        in_specs=[_FOX_Z_SPEC, _FOX_BIAS_SPEC, col(0), _FOX_LSE_SPEC, col(8), ANY, ANY],
        out_specs=[_FOX_Z_SPEC, _FOX_BIAS_SPEC, ANY],
        out_shape=[jax.ShapeDtypeStruct((T, D_IN_PAD), BF16),
                   jax.ShapeDtypeStruct((FOX_HEADS, 1, T), F32),
                   _exchange_shape(dwo_blocks, False)],
        input_output_aliases={5: 0},
        scratch_shapes=[pltpu.VMEM((2 * FOX_D, T), F32), pltpu.VMEM((1, T), F32)] + _exchange_sems(1),
        compiler_params=_params(dimension_semantics=("arbitrary",)),
    )(z, bias, a_f, lse, dy, dz, dwo_blocks)


def _rot(x, cosf, sins):
    return x * cosf + pltpu.roll(x, RET_DK // 2, 1) * sins


def _rot_t(d, cosf, sins):
    return d * cosf - pltpu.roll(d, RET_DK // 2, 1) * sins


_RET_Z_SPEC = pl.BlockSpec((T, RET_W), lambda h: (0, h))
_RET_TABLE_SPECS = [
    pl.BlockSpec((T, RET_DK), lambda h: (0, 0)),
    pl.BlockSpec((T, RET_DK), lambda h: (0, 0)),
    pl.BlockSpec((1, CHUNK, CHUNK), lambda h: (h, 0, 0)),
    pl.BlockSpec((1, CHUNK, 1), lambda h: (h, 0, 0)),
    pl.BlockSpec((1, CHUNK, 1), lambda h: (h, 0, 0)),
    pl.BlockSpec((1, 1, 1), lambda h: (h, 0, 0)),
]
_RQ, _RK = pl.ds(0, RET_DK), pl.ds(RET_DK, RET_DK)
_RV, _RG = pl.ds(2 * RET_DK, RET_DV), pl.ds(2 * RET_DK + RET_DV, RET_DV)
_RET_KSCALE = RET_DK ** -0.5


def _ret_fwd(z, tables):
    def body(z_ref, cos_ref, sin_ref, dm_ref, zeta_ref, xi_ref, cd_ref, raw_ref, y_ref):
        dmask, zeta, xi, cdec = dm_ref[0], zeta_ref[0], xi_ref[0], cd_ref[0]
        state = jnp.zeros((RET_DK, RET_DV), F32)
        for n in range(NCHUNK):
            rows = pl.ds(n * CHUNK, CHUNK)
            cosf, sins = cos_ref[rows, :], sin_ref[rows, :]
            qr = _rot(z_ref[rows, _RQ].astype(F32), cosf, sins)
            kr_b = (_rot(z_ref[rows, _RK].astype(F32), cosf, sins) * _RET_KSCALE).astype(BF16)
            v_b = z_ref[rows, _RV]
            a = _dot_nt(qr.astype(BF16), kr_b) * dmask
            out = _dot(a.astype(BF16), v_b) + _dot((qr * xi).astype(BF16), state.astype(BF16))
            state = state * cdec + _dot_tn(kr_b, (v_b.astype(F32) * zeta).astype(BF16))
            raw_ref[rows, :] = out
            r = lax.rsqrt(jnp.mean(out * out, axis=-1, keepdims=True) + EPS)
            y_ref[rows, :] = (out * r * _silu_parts(z_ref[rows, _RG].astype(F32))[0]).astype(BF16)

    wide = pl.BlockSpec((T, RET_DV), lambda h: (0, h))
    return pl.pallas_call(
        body, name="ret_fwd", grid=(RET_HEADS,),
        in_specs=[_RET_Z_SPEC] + _RET_TABLE_SPECS,
        out_specs=[wide, wide],
        out_shape=[jax.ShapeDtypeStruct((T, RET_HEADS * RET_DV), F32),
                   jax.ShapeDtypeStruct((T, D_MIX), BF16)],
        compiler_params=_params(dimension_semantics=("arbitrary",)),
    )(z, *tables)


def _ret_bwd(z, tables, raw, dy):
    def body(z_ref, cos_ref, sin_ref, dm_ref, zeta_ref, xi_ref, cd_ref, raw_ref, dy_ref,
             dz_ref, st_ref):
        dmask, zeta, xi, cdec = dm_ref[0], zeta_ref[0], xi_ref[0], cd_ref[0]

        def rotated(n):
            rows = pl.ds(n * CHUNK, CHUNK)
            cosf, sins = cos_ref[rows, :], sin_ref[rows, :]
            qr = _rot(z_ref[rows, _RQ].astype(F32), cosf, sins)
            kr_b = (_rot(z_ref[rows, _RK].astype(F32), cosf, sins) * _RET_KSCALE).astype(BF16)
            return rows, cosf, sins, qr, kr_b

        state = jnp.zeros((RET_DK, RET_DV), F32)
        for n in range(NCHUNK):
            st_ref[n] = state.astype(BF16)
            if n + 1 < NCHUNK:
                rows, _, _, _, kr_b = rotated(n)
                state = state * cdec + _dot_tn(kr_b, (z_ref[rows, _RV].astype(F32) * zeta).astype(BF16))

        grad_state = jnp.zeros((RET_DK, RET_DV), F32)
        for n in reversed(range(NCHUNK)):
            rows, cosf, sins, qr, kr_b = rotated(n)
            qr_b = qr.astype(BF16)
            v_b = z_ref[rows, _RV]
            gs_b = grad_state.astype(BF16)
            o = raw_ref[rows, :]
            r = lax.rsqrt(jnp.mean(o * o, axis=-1, keepdims=True) + EPS)
            hn = o * r
            sg, dsg = _silu_parts(z_ref[rows, _RG].astype(F32))
            dyn = dy_ref[rows, :].astype(F32)
            dz_ref[rows, _RG] = (dyn * hn * dsg).astype(BF16)
            dhn = dyn * sg
            do_b = (r * (dhn - hn * jnp.mean(dhn * hn, axis=-1, keepdims=True))).astype(BF16)
            a_b = (_dot_nt(qr_b, kr_b) * dmask).astype(BF16)
            da_b = (_dot_nt(do_b, v_b) * dmask).astype(BF16)
            dqr = _dot(da_b, kr_b) + xi * _dot_nt(do_b, st_ref[n])
            dkr = _dot_tn(da_b, qr_b) + zeta * _dot_nt(v_b, gs_b)
            dv = _dot_tn(a_b, do_b) + zeta * _dot(kr_b, gs_b)
            grad_state = grad_state * cdec + _dot_tn((qr * xi).astype(BF16), do_b)
            dz_ref[rows, _RQ] = _rot_t(dqr, cosf, sins).astype(BF16)
            dz_ref[rows, _RK] = (_rot_t(dkr, cosf, sins) * _RET_KSCALE).astype(BF16)
            dz_ref[rows, _RV] = dv.astype(BF16)

    wide = pl.BlockSpec((T, RET_DV), lambda h: (0, h))
    return pl.pallas_call(
        body, name="ret_bwd", grid=(RET_HEADS,),
        in_specs=[_RET_Z_SPEC] + _RET_TABLE_SPECS + [wide, wide],
        out_specs=_RET_Z_SPEC,
        out_shape=jax.ShapeDtypeStruct((T, D_IN_PAD), BF16),
        scratch_shapes=[pltpu.VMEM((NCHUNK, RET_DK, RET_DV), BF16)],
        compiler_params=_params(dimension_semantics=("arbitrary",)),
    )(z, *tables, raw, dy)


def _adamw(w, g, m, v):
    m = ADAM_B1 * m + (1.0 - ADAM_B1) * g
    v = ADAM_B2 * v + (1.0 - ADAM_B2) * (g * g)
    m_hat = m / (1.0 - ADAM_B1 ** ADAM_STEP)
    v_hat = v / (1.0 - ADAM_B2 ** ADAM_STEP)
    delta = -ADAM_LR * (m_hat / (jnp.sqrt(v_hat) + ADAM_EPS) + ADAM_WD * w)
    return delta, m, v


def _sum_adamw(parts, w, m, v, rows, name):
    _, r_tot, cols = parts.shape
    assert r_tot % rows == 0

    def body(p_ref, w_ref, m_ref, v_ref, g_ref, d_ref, nm_ref, nv_ref):
        g = p_ref[0].astype(F32)
        for d in range(1, N_DEV):
            g = g + p_ref[d].astype(F32)
        delta, nm, nv = _adamw(w_ref[...], g, m_ref[...], v_ref[...])
        g_ref[...] = g
        d_ref[...] = delta
        nm_ref[...] = nm
        nv_ref[...] = nv

    blk = pl.BlockSpec((rows, cols), lambda i: (i, 0))
    return pl.pallas_call(
        body, name=name, grid=(r_tot // rows,),
        in_specs=[pl.BlockSpec((N_DEV, rows, cols), lambda i: (0, i, 0)), blk, blk, blk],
        out_specs=[blk] * 4,
        out_shape=[jax.ShapeDtypeStruct((r_tot, cols), F32)] * 4,
        compiler_params=_params(dimension_semantics=("arbitrary",)),
    )(parts, w, m, v)


def _sum_adamw_w_in(parts, w, m, v, small):
    n_part, r, c = parts.shape
    steps = c // 128

    def body(p_ref, w_hbm, m_hbm, v_hbm, s_ref, g_hbm, d_hbm, nm_hbm, nv_hbm, got_ref,
             in_buf, out_buf, in_sems, out_sems, send_sems, recv_sems, local_sems):
        start, wait = _direct_exchange([s_ref], [got_ref], send_sems, recv_sems, local_sems, True)
        i = pl.program_id(0)
        pl.when(i == 0)(start)
        slot = i % 2

        def loads(step, into):
            cols = pl.ds(pl.multiple_of(step * 128, 128), 128)
            return [pltpu.make_async_copy(h.at[:, 0, cols], in_buf.at[into, k], in_sems.at[into, k])
                    for k, h in enumerate((w_hbm, m_hbm, v_hbm))]

        cols = pl.ds(pl.multiple_of(i * 128, 128), 128)
        stores = [pltpu.make_async_copy(out_buf.at[k], h.at[:, 0, cols], out_sems.at[k])
                  for k, h in enumerate((g_hbm, d_hbm, nm_hbm, nv_hbm))]

        @pl.when(i == 0)
        def _():
            for cp in loads(0, 0):
                cp.start()

        @pl.when(i + 1 < steps)
        def _():
            for cp in loads(i + 1, 1 - slot):
                cp.start()

        g = p_ref[0].astype(F32)
        for d in range(1, n_part):
            g = g + p_ref[d].astype(F32)
        for cp in loads(i, slot):
            cp.wait()
        delta, nm, nv = _adamw(in_buf[slot, 0], g, in_buf[slot, 1], in_buf[slot, 2])

        @pl.when(i > 0)
        def _():
            for cp in stores:
                cp.wait()

        for k, val in enumerate((g, delta, nm, nv)):
            out_buf[k] = val
        for cp in stores:
            cp.start()

        @pl.when(i == steps - 1)
        def _():
            for cp in stores:
                cp.wait()
            wait()

    return pl.pallas_call(
        body, name="adamw_w_in", grid=(steps,),
        in_specs=[pl.BlockSpec((n_part, r, 128), lambda i: (0, 0, i)), ANY, ANY, ANY, ANY],
        out_specs=[ANY] * 5,
        out_shape=[jax.ShapeDtypeStruct((r, 1, c), F32)] * 4 + [_exchange_shape(small, True)],
        scratch_shapes=[pltpu.VMEM((2, 3, r, 128), F32), pltpu.VMEM((4, r, 128), F32),
                        pltpu.SemaphoreType.DMA((2, 3)), pltpu.SemaphoreType.DMA((4,))] + _exchange_sems(1),
        compiler_params=_params(dimension_semantics=("arbitrary",)),
    )(parts, w, m, v, small)


def _adamw_small(got, me, metas, norms, finals, biases):
    def body(me_ref, gm_ref, gr_ref, *refs):
        ins, outs = refs[:12], refs[12:]
        g_meta, g_rest = gm_ref[0], gr_ref[0]
        for d in range(1, N_DEV):
            g_meta, g_rest = g_meta + gm_ref[d], g_rest + gr_ref[d]
        grads = [g_meta, g_rest[0:1], g_rest[1:2], g_rest[2:3, :FOX_HEADS]]
        for k, g in enumerate(grads):
            w_ref, m_ref, v_ref = ins[3 * k:3 * k + 3]
            delta, new_m, new_v = _adamw(w_ref[...], g, m_ref[...], v_ref[...])
            for o_ref, val in zip(outs[4 * k:4 * k + 4], (g, delta, new_m, new_v)):
                o_ref[...] = val
        outs[16][...] = g_rest[3:4, :128]

    groups = (metas, norms, finals, biases)
    full = lambda a: pl.BlockSpec(a.shape, lambda i, me_ref: (0,) * a.ndim)
    flat = [a for grp in groups for a in grp]
    res = pl.pallas_call(
        body, name="adamw_small",
        grid_spec=pltpu.PrefetchScalarGridSpec(
            num_scalar_prefetch=1, grid=(1,),
            in_specs=[pl.BlockSpec((N_DEV, N_META, META_BLK), lambda i, me_ref: (0, 0, me_ref[0])),
                      pl.BlockSpec((N_DEV, 8, D_MODEL), lambda i, me_ref: (0, N_META // 8, 0))]
            + [full(a) for a in flat],
            out_specs=[full(grp[0]) for grp in groups for _ in range(4)]
            + [pl.BlockSpec((1, 128), lambda i, me_ref: (0, 0))]),
        out_shape=[jax.ShapeDtypeStruct(grp[0].shape, F32) for grp in groups for _ in range(4)]
        + [jax.ShapeDtypeStruct((1, 128), F32)],
        compiler_params=_params(dimension_semantics=("arbitrary",)),
    )(me, got, got, *flat)
    return [res[4 * k:4 * k + 4] for k in range(4)], res[16]


def kernel(x, meta_tokens, norm_g, w_in, b_f, w_out, final_g, loss_target, m_meta_tokens, m_norm_g, m_w_in, m_b_f, m_w_out, m_final_g, v_meta_tokens, v_norm_g, v_w_in, v_b_f, v_w_out, v_final_g):
    core = lax.axis_index("c")
    me = 4 * lax.axis_index("x") + 2 * lax.axis_index("y") + core
    tables = _tables()

    wt_all, meta_all = _gather_two_level([_slab(w_in[0].T.astype(BF16), me), meta_tokens], name="gather_w_in")
    slabs, tail = _join_edges(wt_all)
    meta_full = jnp.transpose(meta_all, (1, 0, 2)).reshape(N_META, D_MODEL)
    b_pad = jnp.pad(b_f, ((0, 0), (0, 128 - FOX_HEADS)))

    h_pad, u, z = _rms_z_ret(x[0], meta_full, norm_g, slabs)
    z, zff = _z_fox(u, slabs, tail, z)
    bias = _forget_fwd(zff, b_pad)[:FOX_HEADS].reshape(FOX_HEADS, 1, T)
    raw, y = _ret_fwd(z, tables)
    a_f, lse, y, w_out_all = _fox_fwd(z, bias, y, w_out[0].astype(BF16))
    w_out_b = w_out_all.reshape(D_MIX, D_MODEL)
    dout, dout_b, dy, loss_blk, d_final_g = _out_loss_dy(y, w_out_b, h_pad, loss_target[0],
                                                         final_g.reshape(1, D_MODEL))

    d_w_out = _mm_tn(y, dout_b, tm=D_MIX, tn=256, name="mm_dwout")
    dz = _ret_bwd(z, tables, raw, dy)
    dz, dc, got_w_out = _fox_bwd(z, bias, a_f, lse, dy, dz, d_w_out.reshape(N_DEV, WO_BLK, D_MODEL))
    dz, db_f = _forget_bwd(zff, b_pad, dc.reshape(FOX_HEADS, T), dz)

    kept, via, direct = _dwin_pair_slabs(dz, u, core)
    dh, d_norm_g, summed = _du_rms(dz, slabs, tail, h_pad, dout, norm_g, kept, via)
    got_w_in = _unslab([kept[1], direct, summed], me)

    small = jnp.concatenate([
        dh[PAD:CHUNK], d_norm_g, d_final_g, jnp.pad(db_f[:, :FOX_HEADS], ((0, 0), (0, D_MODEL - FOX_HEADS))),
        jnp.pad(loss_blk[0:1], ((0, 0), (0, D_MODEL - 128))),
        jnp.zeros((SMALL_ROWS - N_META - 4, D_MODEL), F32)], axis=0)
    fore = lambda a: jnp.transpose(a, (2, 0, 1))
    g_w_in, d_w_in, nm_w_in, nv_w_in, got_small = _sum_adamw_w_in(
        got_w_in, fore(w_in), fore(m_w_in), fore(v_w_in), small)
    g_w_out, d_w_out, nm_w_out, nv_w_out = _sum_adamw(got_w_out, w_out[0], m_w_out[0], v_w_out[0], 128, "adamw_w_out")

    row = lambda a: a.reshape(1, D_MODEL)
    (meta_o, norm_o, final_o, bias_o), loss_row = _adamw_small(
        got_small, me.astype(jnp.int32).reshape(1),
        (meta_tokens, m_meta_tokens, v_meta_tokens), (norm_g, m_norm_g, v_norm_g),
        (row(final_g), row(m_final_g), row(v_final_g)), (b_f, m_b_f, v_b_f))
    final_o = [a.reshape(D_MODEL) for a in final_o]

    back = lambda a: jnp.transpose(a, (1, 2, 0))
    outs = [[meta_o[k], norm_o[k], back(wk), bias_o[k], ok[None], final_o[k]]
            for k, (wk, ok) in enumerate(zip((g_w_in, d_w_in, nm_w_in, nv_w_in),
                                             (g_w_out, d_w_out, nm_w_out, nv_w_out)))]
    return (loss_row[0, 0], dh[CHUNK:][None], *outs[0], *outs[1], *outs[2], *outs[3])
```

```python
import numpy as np
import jax
import jax.numpy as jnp
from jax import lax
from jax.experimental import pallas as pl
from jax.experimental.pallas import tpu as pltpu

F32 = jnp.float32
BF16 = jnp.bfloat16

N_DEV = 8
N_CHIP = 4
D_MODEL = 1024
SEQ = 2048
N_META = 16
CHUNK = 128
PAD = CHUNK - N_META
T = SEQ + CHUNK
NCHUNK = T // CHUNK
D_MIX = 2048
RET_HEADS = 4
RET_DK = 128
RET_DV = 256
RET_W = 2 * RET_DK + 2 * RET_DV
FOX_HEADS = 16
FOX_D = 64
FOX_PAIRS = FOX_HEADS // 2
FOX_W = 4 * 128
FOX_BASE = RET_HEADS * RET_W
FF_BASE = FOX_BASE + FOX_PAIRS * FOX_W
D_IN = 7184
D_IN_PAD = 7296
W_BLK = D_IN // N_DEV
WO_BLK = D_MIX // N_DEV
META_BLK = D_MODEL // N_DEV
EPS = 1e-6
NEG_INF = -1e30
ROPE_BASE = 10000.0
LOG2E = 1.4426950408889634
LN2 = 0.6931471805599453

ADAM_LR = 0.001
ADAM_B1 = 0.9
ADAM_B2 = 0.999
ADAM_EPS = 1e-08
ADAM_WD = 0.01
ADAM_STEP = 10

SMALL_ROWS = 24
VMEM_LIMIT = 56 * 1024 * 1024
MESH = pl.DeviceIdType.MESH
ANY = pl.BlockSpec(memory_space=pl.ANY)

_NT = (((1,), (1,)), ((), ()))
_TN = (((0,), (0,)), ((), ()))


def _dot(a, b):
    return jnp.dot(a, b, preferred_element_type=F32)


def _dot_nt(a, b):
    return lax.dot_general(a, b, _NT, preferred_element_type=F32)


def _dot_tn(a, b):
    return lax.dot_general(a, b, _TN, preferred_element_type=F32)


def _params(**kw):
    return pltpu.CompilerParams(vmem_limit_bytes=VMEM_LIMIT, **kw)


def _silu_parts(g):
    sig = jax.nn.sigmoid(g)
    return g * sig, sig * (1.0 + g * (1.0 - sig))


def _tables():
    pos = np.arange(T, dtype=np.float32) - PAD
    inv = (ROPE_BASE ** (-np.arange(0, RET_DK, 2, dtype=np.float32) / RET_DK)).astype(np.float32)
    ang = pos[:, None] * inv[None, :]
    cos, sin = np.cos(ang), np.sin(ang)
    cosf = np.concatenate([cos, cos], axis=1).astype(np.float32)
    sins = np.concatenate([-sin, sin], axis=1).astype(np.float32)
    h = np.arange(RET_HEADS, dtype=np.float32)
    log_gamma = np.log1p(-np.exp2(-5.0 - h)).astype(np.float32)
    idx = np.arange(CHUNK, dtype=np.float32)
    diff = idx[:, None] - idx[None, :]
    dmask = np.where(diff[None] >= 0,
                     np.exp(log_gamma[:, None, None] * np.maximum(diff, 0.0)[None]), 0.0)
    zeta = np.exp(log_gamma[:, None] * (CHUNK - 1.0 - idx)[None, :])
    xi = np.exp(log_gamma[:, None] * (idx + 1.0)[None, :])
    cdec = np.exp(log_gamma * CHUNK)
    return (jnp.asarray(cosf), jnp.asarray(sins), jnp.asarray(dmask, F32),
            jnp.asarray(zeta[:, :, None], F32), jnp.asarray(xi[:, :, None], F32),
            jnp.asarray(cdec[:, None, None], F32))


W_STRIDE = 896
W_SLAB = 912
W_EDGE = W_SLAB - W_STRIDE
def _slab(block, me):
    shift = W_BLK - W_STRIDE
    return lax.switch(me, [lambda b, d=d: jnp.pad(b, ((shift * d, W_SLAB - W_BLK - shift * d), (0, 0)))
                           for d in range(N_DEV)], block)


def _unslab(slabs, me):
    shift = W_BLK - W_STRIDE
    return lax.switch(me, [lambda *s, d=d: jnp.stack([a[shift * d:shift * d + W_BLK] for a in s])
                           for d in range(N_DEV)], *slabs)


def _join_edges(slabs):
    last = slabs[:, W_STRIDE:]
    first = slabs[:, :W_EDGE] + jnp.concatenate([jnp.zeros_like(last[:1]), last[:-1]], axis=0)
    tail = jnp.pad(last[N_DEV - 1], ((0, 128 - W_EDGE), (0, 0)))
    return lax.dynamic_update_slice(slabs, first, (0, 0, 0)), tail


def _mm_tn(a, b, *, tm, tn, name):
    k, m = a.shape
    n = b.shape[1]
    assert m % tm == 0 and n % tn == 0

    def body(a_ref, b_ref, o_ref):
        o_ref[...] = _dot_tn(a_ref[...], b_ref[...]).astype(BF16)

    return pl.pallas_call(
        body, name=name, grid=(n // tn, m // tm),
        in_specs=[pl.BlockSpec((k, tm), lambda j, i: (0, i)),
                  pl.BlockSpec((k, tn), lambda j, i: (0, j))],
        out_specs=pl.BlockSpec((tm, tn), lambda j, i: (i, j)),
        out_shape=jax.ShapeDtypeStruct((m, n), BF16),
        compiler_params=_params(dimension_semantics=("arbitrary", "arbitrary")),
    )(a, b)


def _piece_spec(base, mult):
    def index(i):
        p = base + mult * i
        return p // 7, p % 7, 0
    return pl.BlockSpec((1, 128, D_MODEL), index)


_RET_PIECES = ((0, 1), (4, 1), (8, 2), (9, 2), (16, 2), (17, 2))
_FOX_PIECES = ((24, 1), (32, 1), (40, 1), (48, 1))


def _rms_z_ret(x, meta, g, slabs, tables):
    def body(x_hbm, m_ref, g_ref, *refs):
        pieces, tabs = refs[:6], refs[6:12]
        h_ref, u_ref, z_ref, raw_ref, y_ref, x_sem = refs[12:]

        @pl.when(pl.program_id(0) == 0)
        def _():
            tokens = pltpu.make_async_copy(x_hbm, h_ref.at[pl.ds(CHUNK, SEQ)], x_sem)
            tokens.start()
            h_ref[pl.ds(0, PAD), :] = jnp.zeros((PAD, D_MODEL), F32)
            h_ref[pl.ds(PAD, N_META), :] = m_ref[...]
            tokens.wait()
            h = h_ref[...]
            r = lax.rsqrt(jnp.mean(h * h, axis=-1, keepdims=True) + EPS)
            u_ref[...] = (h * r * g_ref[...]).astype(BF16)

        w = jnp.concatenate([p[0] for p in pieces], axis=0)
        z_ref[...] = _dot_nt(u_ref[...], w).astype(BF16)
        _ret_fwd_head(z_ref, *tabs, raw_ref, y_ref)

    whole = pl.BlockSpec((T, D_MODEL), lambda i: (0, 0))
    wide = pl.BlockSpec((T, RET_DV), lambda i: (0, i))
    return pl.pallas_call(
        body, name="rms_z_ret", grid=(RET_HEADS,),
        in_specs=[ANY, pl.BlockSpec((N_META, D_MODEL), lambda i: (0, 0)),
                  pl.BlockSpec((1, D_MODEL), lambda i: (0, 0))]
        + [_piece_spec(*bm) for bm in _RET_PIECES] + _RET_TABLE_SPECS,
        out_specs=[whole, whole, pl.BlockSpec((T, RET_W), lambda i: (0, i)), wide, wide],
        out_shape=[jax.ShapeDtypeStruct((T, D_MODEL), F32),
                   jax.ShapeDtypeStruct((T, D_MODEL), BF16),
                   jax.ShapeDtypeStruct((T, D_IN_PAD), BF16),
                   jax.ShapeDtypeStruct((T, RET_HEADS * RET_DV), F32),
                   jax.ShapeDtypeStruct((T, D_MIX), BF16)],
        scratch_shapes=[pltpu.SemaphoreType.DMA],
        compiler_params=_params(dimension_semantics=("arbitrary",)),
    )(x, meta, g, *([slabs] * 6), *tables)


def _z_fox(u, slabs, tail, z):
    def body(u_ref, *refs):
        pieces, t_ref, z_ref, zff_ref = refs[:4], refs[4], refs[6], refs[7]
        u = u_ref[...]
        w = jnp.concatenate([p[0] for p in pieces], axis=0)
        z_ref[...] = _dot_nt(u, w).astype(BF16)

        @pl.when(pl.program_id(0) == 0)
        def _():
            zff_ref[...] = _dot_nt(u, t_ref[...])

    return pl.pallas_call(
        body, name="z_fox", grid=(FOX_PAIRS,),
        in_specs=[pl.BlockSpec((T, D_MODEL), lambda i: (0, 0))] + [_piece_spec(*bm) for bm in _FOX_PIECES]
        + [pl.BlockSpec((128, D_MODEL), lambda i: (0, 0)), ANY],
        out_specs=[_FOX_Z_SPEC, pl.BlockSpec((T, 128), lambda i: (0, 0))],
        out_shape=[jax.ShapeDtypeStruct((T, D_IN_PAD), BF16),
                   jax.ShapeDtypeStruct((T, 128), F32)],
        input_output_aliases={6: 0},
        compiler_params=_params(dimension_semantics=("arbitrary",)),
    )(u, *([slabs] * 4), tail, z)


def _out_loss_dy(y, w_out_b, h_pad, target, g):
    tm = T // 4

    def body(y_ref, w_ref, h_ref, t_hbm, g_ref, d_ref, db_ref, dy_ref, loss_ref, dg_ref, t_buf, t_sem):
        i = pl.program_id(0)
        head = pltpu.make_async_copy(t_hbm.at[pl.ds(0, tm - CHUNK)], t_buf.at[pl.ds(CHUNK, tm - CHUNK)], t_sem)
        rest = pltpu.make_async_copy(t_hbm.at[pl.ds(pl.multiple_of(jnp.maximum(i, 1) * tm - CHUNK, 8), tm)],
                                     t_buf, t_sem)

        @pl.when(i == 0)
        def _():
            t_buf[pl.ds(0, CHUNK), :] = jnp.zeros((CHUNK, D_MODEL), F32)
            head.start()
            loss_ref[...] = jnp.zeros_like(loss_ref)
            dg_ref[...] = jnp.zeros_like(dg_ref)

        pl.when(i > 0)(rest.start)

        w = w_ref[...]
        o = _dot(y_ref[...], w) + h_ref[...]
        pl.when(i == 0)(head.wait)
        pl.when(i > 0)(rest.wait)
        token = lax.broadcasted_iota(jnp.int32, (tm, 1), 0) + i * tm >= CHUNK
        g = g_ref[...]
        r = lax.rsqrt(jnp.mean(o * o, axis=-1, keepdims=True) + EPS)
        xn = o * r
        e = jnp.where(token, xn * g - t_buf[...], 0.0)
        loss_ref[...] += jnp.full(loss_ref.shape, 0.5 / D_MODEL * jnp.sum(e * e), F32)
        do = e * (1.0 / D_MODEL)
        dg_ref[...] += jnp.sum(do * xn, axis=0, keepdims=True)
        dn = do * g
        d = r * (dn - xn * jnp.mean(dn * xn, axis=-1, keepdims=True))
        d_b = d.astype(BF16)
        d_ref[...] = d
        db_ref[...] = d_b
        dy_ref[...] = _dot_nt(d_b, w).astype(BF16)

    tile = pl.BlockSpec((tm, D_MODEL), lambda i: (i, 0))
    wide = pl.BlockSpec((tm, D_MIX), lambda i: (i, 0))
    return pl.pallas_call(
        body, name="out_loss_dy", grid=(T // tm,),
        in_specs=[wide, pl.BlockSpec((D_MIX, D_MODEL), lambda i: (0, 0)), tile, ANY,
                  pl.BlockSpec((1, D_MODEL), lambda i: (0, 0))],
        out_specs=[tile, tile, wide,
                   pl.BlockSpec((8, 128), lambda i: (0, 0)),
                   pl.BlockSpec((1, D_MODEL), lambda i: (0, 0))],
        out_shape=[jax.ShapeDtypeStruct((T, D_MODEL), F32),
                   jax.ShapeDtypeStruct((T, D_MODEL), BF16),
                   jax.ShapeDtypeStruct((T, D_MIX), BF16),
                   jax.ShapeDtypeStruct((8, 128), F32),
                   jax.ShapeDtypeStruct((1, D_MODEL), F32)],
        scratch_shapes=[pltpu.VMEM((tm, D_MODEL), F32), pltpu.SemaphoreType.DMA],
        compiler_params=_params(dimension_semantics=("arbitrary",)),
    )(y, w_out_b, h_pad, target, g)


def _coords():
    return lax.axis_index("x"), lax.axis_index("y"), lax.axis_index("c")


def _flip(v, bit):
    return 1 - v if bit else v


def _peer(x, y, c, r):
    return _flip(x, (r >> 2) & 1), _flip(y, (r >> 1) & 1), _flip(c, r & 1)


def _direct_exchange(ins, outs, send_sems, recv_sems, local_sems, gather):
    x, y, c = _coords()
    me = 4 * x + 2 * y + c

    def src(k, to_idx):
        return ins[k] if gather else ins[k].at[to_idx]

    local = [pltpu.make_async_copy(src(k, me), outs[k].at[me], local_sems.at[k])
             for k in range(len(ins))]
    sends, recvs = [], []
    for r in range(1, N_DEV):
        px, py, pc = _peer(x, y, c, r)
        peer = 4 * px + 2 * py + pc
        for k in range(len(ins)):
            sems = dict(send_sem=send_sems.at[k, r - 1], recv_sem=recv_sems.at[k, r - 1],
                        device_id=(px, py, pc), device_id_type=MESH)
            sends.append(pltpu.make_async_remote_copy(src_ref=src(k, peer), dst_ref=outs[k].at[me], **sems))
            recvs.append(pltpu.make_async_remote_copy(src_ref=src(k, peer), dst_ref=outs[k].at[peer], **sems))

    def start():
        for cp in local + sends:
            cp.start()

    def wait():
        for cp in recvs:
            cp.wait_recv()
        for cp in sends:
            cp.wait_send()
        for cp in local:
            cp.wait()

    return start, wait


def _exchange_sems(n_arr):
    return [pltpu.SemaphoreType.DMA((n_arr, N_DEV - 1)), pltpu.SemaphoreType.DMA((n_arr, N_DEV - 1)),
            pltpu.SemaphoreType.DMA((n_arr,))]


def _exchange_shape(a, gather):
    return jax.ShapeDtypeStruct(((N_DEV,) + a.shape) if gather else a.shape, a.dtype)


def _gather_two_level(arrays, name):
    n_arr = len(arrays)

    def body(*refs):
        ins, outs = refs[:n_arr], refs[n_arr:2 * n_arr]
        send_sems, recv_sems, local_sems = refs[2 * n_arr:]
        x, y, c = _coords()

        def slot(k, px, py, pc):
            return outs[k].at[4 * px + 2 * py + pc]

        def routed(core):
            me, sibling = (x, y, core), (x, y, 1 - core)
            xn, yn, dg = (1 - x, y), (x, 1 - y), (1 - x, 1 - y)
            (first, s_first), (second, s_second) = ((xn, 1), (yn, 2)) if core == 0 else ((yn, 2), (xn, 1))

            def copy(k, j, block, to, own=False):
                return pltpu.make_async_remote_copy(
                    src_ref=ins[k] if own else slot(k, *block), dst_ref=slot(k, *block),
                    send_sem=send_sems.at[k, j], recv_sem=recv_sems.at[k, j],
                    device_id=to, device_id_type=MESH)

            local = [pltpu.make_async_copy(ins[k], slot(k, *me), local_sems.at[k]) for k in range(n_arr)]
            sent = []
            for k in range(n_arr):
                sent += [copy(k, 0, me, sibling, True), copy(k, 1, me, (*xn, core), True),
                         copy(k, 2, me, (*yn, core), True)]
            for cp in local + sent:
                cp.start()

            def pass_on(k, j_from, j_to, block, targets):
                copy(k, j_from, block, me).wait_recv()
                for j, to in zip(j_to, targets):
                    cp = copy(k, j, block, to)
                    cp.start()
                    sent.append(cp)

            for k in range(n_arr):
                pass_on(k, s_first, (3, 3 + s_first), (*first, core), ((*second, core), sibling))
            for k in range(n_arr):
                pass_on(k, s_second, (3 + s_second,), (*second, core), (sibling,))
            for k in range(n_arr):
                pass_on(k, 3, (6,), (*dg, core), (sibling,))
            for k in range(n_arr):
                copy(k, 0, sibling, me).wait_recv()
                for j, chip in ((4, xn), (5, yn), (6, dg)):
                    copy(k, j, (*chip, 1 - core), me).wait_recv()
            for cp in sent:
                cp.wait_send()
            for cp in local:
                cp.wait()

        for core in (0, 1):
            pl.when(c == core)(lambda core=core: routed(core))

    return pl.pallas_call(
        body, name=name,
        in_specs=[ANY] * n_arr, out_specs=[ANY] * n_arr,
        out_shape=[_exchange_shape(a, True) for a in arrays],
        scratch_shapes=_exchange_sems(n_arr),
    )(*arrays)


def _piece_columns():
    pos = {}
    for h in range(RET_HEADS):
        for k, p in enumerate((h, 4 + h, 8 + 2 * h, 9 + 2 * h, 16 + 2 * h, 17 + 2 * h)):
            pos[p] = 6 * h + k
    for p in range(FOX_PAIRS):
        for i in range(4):
            pos[24 + 8 * i + p] = 24 + 4 * p + i
    pos[D_IN_PAD // 128 - 1] = D_IN_PAD // 128 - 1
    return np.array([pos[7 * d + j] for d in range(N_DEV) for j in range(8)], np.int32)


def _routes(core):
    x, y, _ = _coords()
    return ((1 - x, y), (x, 1 - y)) if core == 0 else ((x, 1 - y), (1 - x, y))


def _dwin_pair_slabs(dz, u, core):
    def body(order_ref, cols_ref, *refs):
        pieces, u_ref = refs[:8], refs[8]
        kept_ref, theirs_ref, via_ref, direct_ref = refs[9:13]
        send_buf, keep_buf, got_buf, push_send, push_recv, hop_send, hop_recv, load_sem = refs[13:]
        s = pl.program_id(0)
        x, y, c = _coords()
        cols = jnp.concatenate([p[...] for p in pieces[:7]] + [pieces[7][:, :W_EDGE]], axis=1)
        slab = _dot_tn(cols, u_ref[...])

        def push(k):
            return pltpu.make_async_remote_copy(
                src_ref=send_buf.at[k], dst_ref=theirs_ref.at[k],
                send_sem=push_send.at[k], recv_sem=push_recv.at[k],
                device_id=(x, y, 1 - c), device_id_type=MESH)

        def hop(k, core):
            first, _ = _routes(core)
            return pltpu.make_async_remote_copy(
                src_ref=keep_buf.at[k], dst_ref=direct_ref if k == 0 else via_ref,
                send_sem=hop_send.at[k], recv_sem=hop_recv.at[k],
                device_id=(*first, core), device_id_type=MESH)

        for k in range(N_CHIP):
            @pl.when(s == 2 * k)
            def _(k=k):
                send_buf[k] = slab.astype(BF16)
                push(k).start()

            @pl.when(s == 2 * k + 1)
            def _(k=k):
                push(k).wait_recv()
                load = pltpu.make_async_copy(theirs_ref.at[k], got_buf, load_sem)
                load.start()
                load.wait()
                total = (slab + got_buf[...].astype(F32)).astype(BF16)
                if k < 2:
                    keep_buf[k] = total
                    for core in (0, 1):
                        pl.when(c == core)(lambda core=core: hop(k, core).start())
                else:
                    kept_ref[0] = total

        @pl.when(s == 2 * N_CHIP - 1)
        def _():
            for core in (0, 1):
                @pl.when(c == core)
                def _(core=core):
                    for k in range(2):
                        hop(k, core).wait_recv()
                        hop(k, core).wait_send()
            for k in range(N_CHIP):
                push(k).wait_send()

    x, y = lax.axis_index("x"), lax.axis_index("y")
    xn, yn, dg, own = 2 * (1 - x) + y, 2 * x + 1 - y, 2 * (1 - x) + 1 - y, 2 * x + y
    mine = jnp.where(core == 0, jnp.stack([xn, dg, yn, own]), jnp.stack([yn, dg, xn, own]))
    sibs = jnp.where(core == 0, jnp.stack([yn, dg, xn, own]), jnp.stack([xn, dg, yn, own]))
    order = jnp.stack([2 * sibs + (1 - core), 2 * mine + core], axis=1).reshape(2 * N_CHIP).astype(jnp.int32)
    piece = lambda j: pl.BlockSpec((T, 128), lambda s, order_ref, cols_ref: (0, cols_ref[order_ref[s] * 8 + j]))
    slab = jax.ShapeDtypeStruct((W_SLAB, D_MODEL), BF16)
    kept, _, via, direct = pl.pallas_call(
        body, name="dwin_pair_slabs",
        grid_spec=pltpu.PrefetchScalarGridSpec(
            num_scalar_prefetch=2, grid=(2 * N_CHIP,),
            in_specs=[piece(j) for j in range(8)]
            + [pl.BlockSpec((T, D_MODEL), lambda s, order_ref, cols_ref: (0, 0))],
            out_specs=[pl.BlockSpec((1, W_SLAB, D_MODEL),
                                    lambda s, order_ref, cols_ref: (jnp.where(s >= 6, 1, 0), 0, 0)),
                       ANY, ANY, ANY],
            scratch_shapes=[pltpu.VMEM((N_CHIP, W_SLAB, D_MODEL), BF16), pltpu.VMEM((2, W_SLAB, D_MODEL), BF16),
                            pltpu.VMEM((W_SLAB, D_MODEL), BF16),
                            pltpu.SemaphoreType.DMA((N_CHIP,)), pltpu.SemaphoreType.DMA((N_CHIP,)),
                            pltpu.SemaphoreType.DMA((2,)), pltpu.SemaphoreType.DMA((2,)),
                            pltpu.SemaphoreType.DMA]),
        out_shape=[jax.ShapeDtypeStruct((2, W_SLAB, D_MODEL), BF16),
                   jax.ShapeDtypeStruct((N_CHIP, W_SLAB, D_MODEL), BF16), slab, slab],
        compiler_params=_params(dimension_semantics=("arbitrary",)),
    )(order, jnp.asarray(_piece_columns()), *([dz] * 8), u)
    return kept, via, direct


def _du_rms(dz, slabs, tail, h_pad, dout, g, kept, via):
    tm = 272
    steps = T // tm
    columns = _piece_columns().reshape(N_DEV, 8)

    def body(dz_ref, w_ref, t_ref, h_ref, d_ref, g_ref, kept_ref, via_ref, dh_ref, dg_ref, got_ref,
             via_buf, mine_buf, send_sem, recv_sem, local_sems):
        i = pl.program_id(0)
        x, y, c = _coords()

        def summed(core):
            _, second = _routes(core)
            return pltpu.make_async_remote_copy(
                src_ref=mine_buf, dst_ref=got_ref, send_sem=send_sem, recv_sem=recv_sem,
                device_id=(*second, core), device_id_type=MESH)

        @pl.when(i == 0)
        def _():
            loads = [pltpu.make_async_copy(kept_ref.at[0], mine_buf, local_sems.at[0]),
                     pltpu.make_async_copy(via_ref, via_buf, local_sems.at[1])]
            for cp in loads:
                cp.start()
            for cp in loads:
                cp.wait()
            mine_buf[...] = (mine_buf[...].astype(F32) + via_buf[...].astype(F32)).astype(BF16)
            for core in (0, 1):
                pl.when(c == core)(lambda core=core: summed(core).start())
            dg_ref[...] = jnp.zeros_like(dg_ref)

        du = _dot(dz_ref[:, pl.ds(FF_BASE, 128)], t_ref[...])
        for d in range(N_DEV):
            cols = jnp.concatenate([dz_ref[:, pl.ds(128 * int(columns[d, j]), 128)] for j in range(7)], axis=1)
            du = du + _dot(cols, w_ref[d, pl.ds(0, W_STRIDE), :])
        h = h_ref[...]
        r = lax.rsqrt(jnp.mean(h * h, axis=-1, keepdims=True) + EPS)
        xn = h * r
        dg_ref[...] += jnp.sum(du * xn, axis=0, keepdims=True)
        dn = du * g_ref[...]
        dh_ref[...] = d_ref[...] + r * (dn - xn * jnp.mean(dn * xn, axis=-1, keepdims=True))

        for core in (0, 1):
            @pl.when(jnp.logical_and(c == core, i == steps - 1))
            def _(core=core):
                summed(core).wait_recv()
                summed(core).wait_send()

    tile = pl.BlockSpec((tm, D_MODEL), lambda i: (i, 0))
    return pl.pallas_call(
        body, name="du_rms", grid=(steps,),
        in_specs=[pl.BlockSpec((tm, D_IN_PAD), lambda i: (i, 0)),
                  pl.BlockSpec((N_DEV, W_SLAB, D_MODEL), lambda i: (0, 0, 0)),
                  pl.BlockSpec((128, D_MODEL), lambda i: (0, 0)),
                  tile, tile, pl.BlockSpec((1, D_MODEL), lambda i: (0, 0)), ANY, ANY],
        out_specs=[tile, pl.BlockSpec((1, D_MODEL), lambda i: (0, 0)), ANY],
        out_shape=[jax.ShapeDtypeStruct((T, D_MODEL), F32),
                   jax.ShapeDtypeStruct((1, D_MODEL), F32),
                   jax.ShapeDtypeStruct(via.shape, via.dtype)],
        scratch_shapes=[pltpu.VMEM(via.shape, via.dtype), pltpu.VMEM(via.shape, via.dtype),
                        pltpu.SemaphoreType.DMA, pltpu.SemaphoreType.DMA, pltpu.SemaphoreType.DMA((2,))],
        compiler_params=_params(dimension_semantics=("arbitrary",)),
    )(dz, slabs, tail, h_pad, dout, g, kept, via)


def _tri(lower):
    r = lax.broadcasted_iota(jnp.int32, (CHUNK, CHUNK), 0)
    c = lax.broadcasted_iota(jnp.int32, (CHUNK, CHUNK), 1)
    return jnp.where((r >= c) if lower else (r <= c), 1.0, 0.0).astype(F32)


def _row_valid(n):
    r = lax.broadcasted_iota(jnp.int32, (CHUNK, 128), 0) + n * CHUNK
    return r >= PAD


_FF_SPEC = pl.BlockSpec((T, 128), lambda i: (0, FF_BASE // 128))
_ZFF_SPEC = pl.BlockSpec((T, 128), lambda i: (0, 0))


def _forget_fwd(z, b_pad):
    def body(z_ref, b_ref, o_ref):
        tri = _tri(True)
        carry = jnp.zeros((1, 128), F32)
        for n in range(NCHUNK):
            rows = pl.ds(n * CHUNK, CHUNK)
            a = z_ref[rows, :] + b_ref[...]
            lf = -(jnp.maximum(-a, 0.0) + jnp.log(1.0 + jnp.exp(-jnp.abs(a))))
            lf = jnp.where(_row_valid(n), lf, 0.0)
            c = jnp.dot(tri, lf, precision=lax.Precision.HIGHEST,
                        preferred_element_type=F32) + carry
            carry = c[CHUNK - 1:CHUNK, :]
            o_ref[:, rows] = jnp.where(_row_valid(n), c * (-LOG2E), NEG_INF).T

    return pl.pallas_call(
        body, name="forget_fwd", grid=(1,),
        in_specs=[_ZFF_SPEC, pl.BlockSpec((1, 128), lambda i: (0, 0))],
        out_specs=pl.BlockSpec((128, T), lambda i: (0, 0)),
        out_shape=jax.ShapeDtypeStruct((128, T), F32),
        compiler_params=_params(dimension_semantics=("arbitrary",)),
    )(z, b_pad)


def _forget_bwd(z, b_pad, dc, dz):
    def body(z_ref, b_ref, dc_ref, dz_in, dff_ref, db_ref):
        tri = _tri(False)
        carry = jnp.zeros((1, 128), F32)
        db = jnp.zeros((1, 128), F32)
        for n in reversed(range(NCHUNK)):
            rows = pl.ds(n * CHUNK, CHUNK)
            dc_blk = jnp.concatenate([dc_ref[:, rows], jnp.zeros((128 - FOX_HEADS, CHUNK), F32)], axis=0).T
            dlf = jnp.dot(tri, dc_blk, precision=lax.Precision.HIGHEST,
                          preferred_element_type=F32) + carry
            carry = dlf[0:1, :]
            a = z_ref[rows, :] + b_ref[...]
            dff = jnp.where(_row_valid(n), dlf * jax.nn.sigmoid(-a), 0.0)
            dff_ref[rows, :] = dff.astype(BF16)
            db = db + jnp.sum(dff, axis=0, keepdims=True)
        db_ref[...] = db

    return pl.pallas_call(
        body, name="forget_bwd", grid=(1,),
        in_specs=[_ZFF_SPEC, pl.BlockSpec((1, 128), lambda i: (0, 0)),
                  pl.BlockSpec((FOX_HEADS, T), lambda i: (0, 0)), ANY],
        out_specs=[_FF_SPEC, pl.BlockSpec((1, 128), lambda i: (0, 0))],
        out_shape=[jax.ShapeDtypeStruct((T, D_IN_PAD), BF16),
                   jax.ShapeDtypeStruct((1, 128), F32)],
        input_output_aliases={3: 0},
        compiler_params=_params(dimension_semantics=("arbitrary",)),
    )(z, b_pad, dc, dz)


FOX_QB = 512
FOX_NQB = SEQ // FOX_QB


def _fox_block(b):
    lo = CHUNK + b * FOX_QB
    return pl.ds(lo, FOX_QB), lo, lo + FOX_QB


def _causal_bias():
    r = lax.broadcasted_iota(jnp.int32, (FOX_QB, FOX_QB), 0)
    c = lax.broadcasted_iota(jnp.int32, (FOX_QB, FOX_QB), 1)
    return jnp.where(c <= r, 0.0, NEG_INF).astype(F32)


def _fox_logits(q_blk, k_all, bias, causal, b):
    _, lo, hi = _fox_block(b)
    here = bias[:, lo:lo + 1]
    s_off = _dot_nt(q_blk, k_all[:lo]) + (bias[:, :lo] - here)
    s_dia = _dot_nt(q_blk, k_all[lo:hi]) + ((bias[:, lo:hi] - here) + causal)
    return s_off, s_dia


_FOX_Z_SPEC = pl.BlockSpec((T, FOX_W), lambda p: (0, FOX_BASE // FOX_W + p))
_FOX_BIAS_SPEC = pl.BlockSpec((2, 1, T), lambda p: (p, 0, 0))
_FOX_LSE_SPEC = pl.BlockSpec((2, T, 1), lambda p: (p, 0, 0))
_FOX_SCALE = FOX_D ** -0.5
_FOX_QSCALE = _FOX_SCALE * LOG2E


def _fox_fwd(z, bias, y, w_out_blk):
    last = FOX_PAIRS - 1

    def body(z_ref, b_ref, y_in, w_ref, a_ref, lse_ref, y_ref, wall_ref,
             send_sems, recv_sems, local_sems):
        start, wait = _direct_exchange([w_ref], [wall_ref], send_sems, recv_sems, local_sems, True)
        pl.when(pl.program_id(0) == 0)(start)

        causal = _causal_bias()
        a_ref[pl.ds(0, CHUNK), :] = jnp.zeros((CHUNK, 128), F32)
        y_ref[pl.ds(0, CHUNK), :] = jnp.zeros((CHUNK, 128), BF16)
        for j in range(2):
            lanes = pl.ds(j * FOX_D, FOX_D)
            k_all = z_ref[:, pl.ds(128 + j * FOX_D, FOX_D)]
            v_all = z_ref[:, pl.ds(256 + j * FOX_D, FOX_D)]
            bias = b_ref[j]
            lse_ref[j, pl.ds(0, CHUNK), :] = jnp.zeros((CHUNK, 1), F32)
            for b in range(FOX_NQB):
                rows, lo, hi = _fox_block(b)
                q_blk = (z_ref[rows, lanes].astype(F32) * _FOX_QSCALE).astype(BF16)
                s_off, s_dia = _fox_logits(q_blk, k_all, bias, causal, b)
                m = jnp.maximum(jnp.max(s_off, axis=-1, keepdims=True),
                                jnp.max(s_dia, axis=-1, keepdims=True))
                e_off = jnp.exp2(s_off - m)
                e_dia = jnp.exp2(s_dia - m)
                total = jnp.sum(e_off, axis=-1, keepdims=True) + jnp.sum(e_dia, axis=-1, keepdims=True)
                o = (_dot(e_off.astype(BF16), v_all[:lo]) + _dot(e_dia.astype(BF16), v_all[lo:hi])) / total
                a_ref[rows, lanes] = o
                lse_ref[j, rows, :] = m + jnp.log(total) * LOG2E
                gate = _silu_parts(z_ref[rows, pl.ds(384 + j * FOX_D, FOX_D)].astype(F32))[0]
                y_ref[rows, lanes] = (o * gate).astype(BF16)

        pl.when(pl.program_id(0) == last)(wait)

    return pl.pallas_call(
        body, name="fox_fwd", grid=(FOX_PAIRS,),
        in_specs=[_FOX_Z_SPEC, _FOX_BIAS_SPEC, ANY, ANY],
        out_specs=[pl.BlockSpec((T, 128), lambda p: (0, p)), _FOX_LSE_SPEC,
                   pl.BlockSpec((T, 128), lambda p: (0, 8 + p)), ANY],
        out_shape=[jax.ShapeDtypeStruct((T, FOX_HEADS * FOX_D), F32),
                   jax.ShapeDtypeStruct((FOX_HEADS, T, 1), F32),
                   jax.ShapeDtypeStruct((T, D_MIX), BF16),
                   _exchange_shape(w_out_blk, True)],
        input_output_aliases={2: 2},
        scratch_shapes=_exchange_sems(1),
        compiler_params=_params(dimension_semantics=("arbitrary",)),
    )(z, bias, y, w_out_blk)


def _fox_bwd(z, bias, a_f, lse, dy, dz, dwo_blocks):
    last = FOX_PAIRS - 1

    def body(z_ref, b_ref, a_ref, lse_ref, dy_ref, dz_in, dwo_ref, dz_ref, dc_ref, got_ref,
             kv_acc, dc_acc, send_sems, recv_sems, local_sems):
        start, wait = _direct_exchange([dwo_ref], [got_ref], send_sems, recv_sems, local_sems, False)
        pl.when(pl.program_id(0) == 0)(start)

        causal = _causal_bias()
        dz_ref[pl.ds(0, CHUNK), pl.ds(0, 128)] = jnp.zeros((CHUNK, 128), BF16)
        dz_ref[pl.ds(0, CHUNK), pl.ds(384, 128)] = jnp.zeros((CHUNK, 128), BF16)
        dk_rows, dv_rows = pl.ds(0, FOX_D), pl.ds(FOX_D, FOX_D)
        for j in range(2):
            lanes = pl.ds(j * FOX_D, FOX_D)
            k_all = z_ref[:, pl.ds(128 + j * FOX_D, FOX_D)]
            v_all = z_ref[:, pl.ds(256 + j * FOX_D, FOX_D)]
            bias = b_ref[j]
            kv_acc[...] = jnp.zeros_like(kv_acc)
            dc_acc[...] = jnp.zeros_like(dc_acc)
            for b in range(FOX_NQB):
                rows, lo, hi = _fox_block(b)
                off, dia = pl.ds(0, lo), pl.ds(lo, FOX_QB)
                q_blk = (z_ref[rows, lanes].astype(F32) * _FOX_QSCALE).astype(BF16)
                s_off, s_dia = _fox_logits(q_blk, k_all, bias, causal, b)
                lse_blk = lse_ref[j, rows, :]
                p_off, p_dia = jnp.exp2(s_off - lse_blk), jnp.exp2(s_dia - lse_blk)
                sg, dsg = _silu_parts(z_ref[rows, pl.ds(384 + j * FOX_D, FOX_D)].astype(F32))
                dyj = dy_ref[rows, lanes].astype(F32)
                dz_ref[rows, pl.ds(384 + j * FOX_D, FOX_D)] = (dyj * a_ref[rows, lanes] * dsg).astype(BF16)
                do_b = (dyj * sg).astype(BF16)
                dp_off = _dot_nt(do_b, v_all[:lo])
                dp_dia = _dot_nt(do_b, v_all[lo:hi])
                d = (jnp.sum(p_off * dp_off, axis=-1, keepdims=True)
                     + jnp.sum(p_dia * dp_dia, axis=-1, keepdims=True))
                ds_off = p_off * (dp_off - d)
                ds_dia = p_dia * (dp_dia - d)
                dc_acc[:, off] -= jnp.sum(ds_off, axis=0, keepdims=True)
                dc_acc[:, dia] -= jnp.sum(ds_dia, axis=0, keepdims=True)
                ds_off_b, ds_dia_b = ds_off.astype(BF16), ds_dia.astype(BF16)
                dq = _dot(ds_off_b, k_all[:lo]) + _dot(ds_dia_b, k_all[lo:hi])
                dz_ref[rows, lanes] = (dq * _FOX_SCALE).astype(BF16)
                kv_acc[dk_rows, off] += _dot_tn(q_blk, ds_off_b)
                kv_acc[dk_rows, dia] += _dot_tn(q_blk, ds_dia_b)
                kv_acc[dv_rows, off] += _dot_tn(do_b, p_off.astype(BF16))
                kv_acc[dv_rows, dia] += _dot_tn(do_b, p_dia.astype(BF16))
            for n in range(NCHUNK):
                rows = pl.ds(n * CHUNK, CHUNK)
                both = kv_acc[:, rows].T
                dz_ref[rows, pl.ds(128 + j * FOX_D, FOX_D)] = (both[:, :FOX_D] * LN2).astype(BF16)
                dz_ref[rows, pl.ds(256 + j * FOX_D, FOX_D)] = both[:, FOX_D:].astype(BF16)
            dc_ref[j] = dc_acc[...]

        pl.when(pl.program_id(0) == last)(wait)

    col = lambda base: pl.BlockSpec((T, 128), lambda p: (0, base + p))
    return pl.pallas_call(
        body, name="fox_bwd", grid=(FOX_PAIRS,),
        in_specs=[_FOX_Z_SPEC, _FOX_BIAS_SPEC, col(0), _FOX_LSE_SPEC, col(8), ANY, ANY],
        out_specs=[_FOX_Z_SPEC, _FOX_BIAS_SPEC, ANY],
        out_shape=[jax.ShapeDtypeStruct((T, D_IN_PAD), BF16),
                   jax.ShapeDtypeStruct((FOX_HEADS, 1, T), F32),
                   _exchange_shape(dwo_blocks, False)],
        input_output_aliases={5: 0},
        scratch_shapes=[pltpu.VMEM((2 * FOX_D, T), F32), pltpu.VMEM((1, T), F32)] + _exchange_sems(1),
        compiler_params=_params(dimension_semantics=("arbitrary",)),
    )(z, bias, a_f, lse, dy, dz, dwo_blocks)


def _rot(x, cosf, sins):
    return x * cosf + pltpu.roll(x, RET_DK // 2, 1) * sins


def _rot_t(d, cosf, sins):
    return d * cosf - pltpu.roll(d, RET_DK // 2, 1) * sins


_RET_Z_SPEC = pl.BlockSpec((T, RET_W), lambda h: (0, h))
_RET_TABLE_SPECS = [
    pl.BlockSpec((T, RET_DK), lambda h: (0, 0)),
    pl.BlockSpec((T, RET_DK), lambda h: (0, 0)),
    pl.BlockSpec((1, CHUNK, CHUNK), lambda h: (h, 0, 0)),
    pl.BlockSpec((1, CHUNK, 1), lambda h: (h, 0, 0)),
    pl.BlockSpec((1, CHUNK, 1), lambda h: (h, 0, 0)),
    pl.BlockSpec((1, 1, 1), lambda h: (h, 0, 0)),
]
_RQ, _RK = pl.ds(0, RET_DK), pl.ds(RET_DK, RET_DK)
_RV, _RG = pl.ds(2 * RET_DK, RET_DV), pl.ds(2 * RET_DK + RET_DV, RET_DV)
_RET_KSCALE = RET_DK ** -0.5


def _ret_fwd_head(z_ref, cos_ref, sin_ref, dm_ref, zeta_ref, xi_ref, cd_ref, raw_ref, y_ref):
    dmask, zeta, xi, cdec = dm_ref[0], zeta_ref[0], xi_ref[0], cd_ref[0]
    state = jnp.zeros((RET_DK, RET_DV), F32)
    for n in range(NCHUNK):
        rows = pl.ds(n * CHUNK, CHUNK)
        cosf, sins = cos_ref[rows, :], sin_ref[rows, :]
        qr = _rot(z_ref[rows, _RQ].astype(F32), cosf, sins)
        kr_b = (_rot(z_ref[rows, _RK].astype(F32), cosf, sins) * _RET_KSCALE).astype(BF16)
        v_b = z_ref[rows, _RV]
        a = _dot_nt(qr.astype(BF16), kr_b) * dmask
        out = _dot(a.astype(BF16), v_b) + _dot((qr * xi).astype(BF16), state.astype(BF16))
        state = state * cdec + _dot_tn(kr_b, (v_b.astype(F32) * zeta).astype(BF16))
        raw_ref[rows, :] = out
        r = lax.rsqrt(jnp.mean(out * out, axis=-1, keepdims=True) + EPS)
        y_ref[rows, :] = (out * r * _silu_parts(z_ref[rows, _RG].astype(F32))[0]).astype(BF16)


def _ret_bwd(z, tables, raw, dy):
    def body(z_ref, cos_ref, sin_ref, dm_ref, zeta_ref, xi_ref, cd_ref, raw_ref, dy_ref,
             dz_ref, st_ref):
        dmask, zeta, xi, cdec = dm_ref[0], zeta_ref[0], xi_ref[0], cd_ref[0]

        def rotated(n):
            rows = pl.ds(n * CHUNK, CHUNK)
            cosf, sins = cos_ref[rows, :], sin_ref[rows, :]
            qr = _rot(z_ref[rows, _RQ].astype(F32), cosf, sins)
            kr_b = (_rot(z_ref[rows, _RK].astype(F32), cosf, sins) * _RET_KSCALE).astype(BF16)
            return rows, cosf, sins, qr, kr_b

        state = jnp.zeros((RET_DK, RET_DV), F32)
        for n in range(NCHUNK):
            st_ref[n] = state.astype(BF16)
            if n + 1 < NCHUNK:
                rows, _, _, _, kr_b = rotated(n)
                state = state * cdec + _dot_tn(kr_b, (z_ref[rows, _RV].astype(F32) * zeta).astype(BF16))

        grad_state = jnp.zeros((RET_DK, RET_DV), F32)
        for n in reversed(range(NCHUNK)):
            rows, cosf, sins, qr, kr_b = rotated(n)
            qr_b = qr.astype(BF16)
            v_b = z_ref[rows, _RV]
            gs_b = grad_state.astype(BF16)
            o = raw_ref[rows, :]
            r = lax.rsqrt(jnp.mean(o * o, axis=-1, keepdims=True) + EPS)
            hn = o * r
            sg, dsg = _silu_parts(z_ref[rows, _RG].astype(F32))
            dyn = dy_ref[rows, :].astype(F32)
            dz_ref[rows, _RG] = (dyn * hn * dsg).astype(BF16)
            dhn = dyn * sg
            do_b = (r * (dhn - hn * jnp.mean(dhn * hn, axis=-1, keepdims=True))).astype(BF16)
            a_b = (_dot_nt(qr_b, kr_b) * dmask).astype(BF16)
            da_b = (_dot_nt(do_b, v_b) * dmask).astype(BF16)
            dqr = _dot(da_b, kr_b) + xi * _dot_nt(do_b, st_ref[n])
            dkr = _dot_tn(da_b, qr_b) + zeta * _dot_nt(v_b, gs_b)
            dv = _dot_tn(a_b, do_b) + zeta * _dot(kr_b, gs_b)
            grad_state = grad_state * cdec + _dot_tn((qr * xi).astype(BF16), do_b)
            dz_ref[rows, _RQ] = _rot_t(dqr, cosf, sins).astype(BF16)
            dz_ref[rows, _RK] = (_rot_t(dkr, cosf, sins) * _RET_KSCALE).astype(BF16)
            dz_ref[rows, _RV] = dv.astype(BF16)

    wide = pl.BlockSpec((T, RET_DV), lambda h: (0, h))
    return pl.pallas_call(
        body, name="ret_bwd", grid=(RET_HEADS,),
        in_specs=[_RET_Z_SPEC] + _RET_TABLE_SPECS + [wide, wide],
        out_specs=_RET_Z_SPEC,
        out_shape=jax.ShapeDtypeStruct((T, D_IN_PAD), BF16),
        scratch_shapes=[pltpu.VMEM((NCHUNK, RET_DK, RET_DV), BF16)],
        compiler_params=_params(dimension_semantics=("arbitrary",)),
    )(z, *tables, raw, dy)


def _adamw(w, g, m, v):
    m = ADAM_B1 * m + (1.0 - ADAM_B1) * g
    v = ADAM_B2 * v + (1.0 - ADAM_B2) * (g * g)
    m_hat = m / (1.0 - ADAM_B1 ** ADAM_STEP)
    v_hat = v / (1.0 - ADAM_B2 ** ADAM_STEP)
    delta = -ADAM_LR * (m_hat / (jnp.sqrt(v_hat) + ADAM_EPS) + ADAM_WD * w)
    return delta, m, v


def _sum_adamw(parts, w, m, v, rows, name):
    _, r_tot, cols = parts.shape
    assert r_tot % rows == 0

    def body(p_ref, w_ref, m_ref, v_ref, g_ref, d_ref, nm_ref, nv_ref):
        g = p_ref[0].astype(F32)
        for d in range(1, N_DEV):
            g = g + p_ref[d].astype(F32)
        delta, nm, nv = _adamw(w_ref[...], g, m_ref[...], v_ref[...])
        g_ref[...] = g
        d_ref[...] = delta
        nm_ref[...] = nm
        nv_ref[...] = nv

    blk = pl.BlockSpec((rows, cols), lambda i: (i, 0))
    return pl.pallas_call(
        body, name=name, grid=(r_tot // rows,),
        in_specs=[pl.BlockSpec((N_DEV, rows, cols), lambda i: (0, i, 0)), blk, blk, blk],
        out_specs=[blk] * 4,
        out_shape=[jax.ShapeDtypeStruct((r_tot, cols), F32)] * 4,
        compiler_params=_params(dimension_semantics=("arbitrary",)),
    )(parts, w, m, v)


def _sum_adamw_w_in(parts, w, m, v, small):
    n_part, r, c = parts.shape
    steps = c // 128

    def body(p_ref, w_hbm, m_hbm, v_hbm, s_ref, g_hbm, d_hbm, nm_hbm, nv_hbm, got_ref,
             in_buf, out_buf, in_sems, out_sems, send_sems, recv_sems, local_sems):
        start, wait = _direct_exchange([s_ref], [got_ref], send_sems, recv_sems, local_sems, True)
        i = pl.program_id(0)
        pl.when(i == 0)(start)
        slot = i % 2

        def loads(step, into):
            cols = pl.ds(pl.multiple_of(step * 128, 128), 128)
            return [pltpu.make_async_copy(h.at[:, 0, cols], in_buf.at[into, k], in_sems.at[into, k])
                    for k, h in enumerate((w_hbm, m_hbm, v_hbm))]

        cols = pl.ds(pl.multiple_of(i * 128, 128), 128)
        stores = [pltpu.make_async_copy(out_buf.at[k], h.at[:, 0, cols], out_sems.at[k])
                  for k, h in enumerate((g_hbm, d_hbm, nm_hbm, nv_hbm))]

        @pl.when(i == 0)
        def _():
            for cp in loads(0, 0):
                cp.start()

        @pl.when(i + 1 < steps)
        def _():
            for cp in loads(i + 1, 1 - slot):
                cp.start()

        g = p_ref[0].astype(F32)
        for d in range(1, n_part):
            g = g + p_ref[d].astype(F32)
        for cp in loads(i, slot):
            cp.wait()
        delta, nm, nv = _adamw(in_buf[slot, 0], g, in_buf[slot, 1], in_buf[slot, 2])

        @pl.when(i > 0)
        def _():
            for cp in stores:
                cp.wait()

        for k, val in enumerate((g, delta, nm, nv)):
            out_buf[k] = val
        for cp in stores:
            cp.start()

        @pl.when(i == steps - 1)
        def _():
            for cp in stores:
                cp.wait()
            wait()

    return pl.pallas_call(
        body, name="adamw_w_in", grid=(steps,),
        in_specs=[pl.BlockSpec((n_part, r, 128), lambda i: (0, 0, i)), ANY, ANY, ANY, ANY],
        out_specs=[ANY] * 5,
        out_shape=[jax.ShapeDtypeStruct((r, 1, c), F32)] * 4 + [_exchange_shape(small, True)],
        scratch_shapes=[pltpu.VMEM((2, 3, r, 128), F32), pltpu.VMEM((4, r, 128), F32),
                        pltpu.SemaphoreType.DMA((2, 3)), pltpu.SemaphoreType.DMA((4,))] + _exchange_sems(1),
        compiler_params=_params(dimension_semantics=("arbitrary",)),
    )(parts, w, m, v, small)


def _adamw_small(got, me, metas, norms, finals, biases):
    def body(me_ref, gm_ref, gr_ref, *refs):
        ins, outs = refs[:12], refs[12:]
        g_meta, g_rest = gm_ref[0], gr_ref[0]
        for d in range(1, N_DEV):
            g_meta, g_rest = g_meta + gm_ref[d], g_rest + gr_ref[d]
        grads = [g_meta, g_rest[0:1], g_rest[1:2], g_rest[2:3, :FOX_HEADS]]
        for k, g in enumerate(grads):
            w_ref, m_ref, v_ref = ins[3 * k:3 * k + 3]
            delta, new_m, new_v = _adamw(w_ref[...], g, m_ref[...], v_ref[...])
            for o_ref, val in zip(outs[4 * k:4 * k + 4], (g, delta, new_m, new_v)):
                o_ref[...] = val
        outs[16][...] = g_rest[3:4, :128]

    groups = (metas, norms, finals, biases)
    full = lambda a: pl.BlockSpec(a.shape, lambda i, me_ref: (0,) * a.ndim)
    flat = [a for grp in groups for a in grp]
    res = pl.pallas_call(
        body, name="adamw_small",
        grid_spec=pltpu.PrefetchScalarGridSpec(
            num_scalar_prefetch=1, grid=(1,),
            in_specs=[pl.BlockSpec((N_DEV, N_META, META_BLK), lambda i, me_ref: (0, 0, me_ref[0])),
                      pl.BlockSpec((N_DEV, 8, D_MODEL), lambda i, me_ref: (0, N_META // 8, 0))]
            + [full(a) for a in flat],
            out_specs=[full(grp[0]) for grp in groups for _ in range(4)]
            + [pl.BlockSpec((1, 128), lambda i, me_ref: (0, 0))]),
        out_shape=[jax.ShapeDtypeStruct(grp[0].shape, F32) for grp in groups for _ in range(4)]
        + [jax.ShapeDtypeStruct((1, 128), F32)],
        compiler_params=_params(dimension_semantics=("arbitrary",)),
    )(me, got, got, *flat)
    return [res[4 * k:4 * k + 4] for k in range(4)], res[16]


def kernel(x, meta_tokens, norm_g, w_in, b_f, w_out, final_g, loss_target, m_meta_tokens, m_norm_g, m_w_in, m_b_f, m_w_out, m_final_g, v_meta_tokens, v_norm_g, v_w_in, v_b_f, v_w_out, v_final_g):
    core = lax.axis_index("c")
    me = 4 * lax.axis_index("x") + 2 * lax.axis_index("y") + core
    tables = _tables()

    wt_all, meta_all = _gather_two_level([_slab(w_in[0].T.astype(BF16), me), meta_tokens], name="gather_w_in")
    slabs, tail = _join_edges(wt_all)
    meta_full = jnp.transpose(meta_all, (1, 0, 2)).reshape(N_META, D_MODEL)
    b_pad = jnp.pad(b_f, ((0, 0), (0, 128 - FOX_HEADS)))

    h_pad, u, z, raw, y = _rms_z_ret(x[0], meta_full, norm_g, slabs, tables)
    z, zff = _z_fox(u, slabs, tail, z)
    bias = _forget_fwd(zff, b_pad)[:FOX_HEADS].reshape(FOX_HEADS, 1, T)
    a_f, lse, y, w_out_all = _fox_fwd(z, bias, y, w_out[0].astype(BF16))
    w_out_b = w_out_all.reshape(D_MIX, D_MODEL)
    dout, dout_b, dy, loss_blk, d_final_g = _out_loss_dy(y, w_out_b, h_pad, loss_target[0],
                                                         final_g.reshape(1, D_MODEL))

    d_w_out = _mm_tn(y, dout_b, tm=D_MIX, tn=256, name="mm_dwout")
    dz = _ret_bwd(z, tables, raw, dy)
    dz, dc, got_w_out = _fox_bwd(z, bias, a_f, lse, dy, dz, d_w_out.reshape(N_DEV, WO_BLK, D_MODEL))
    dz, db_f = _forget_bwd(zff, b_pad, dc.reshape(FOX_HEADS, T), dz)

    kept, via, direct = _dwin_pair_slabs(dz, u, core)
    dh, d_norm_g, summed = _du_rms(dz, slabs, tail, h_pad, dout, norm_g, kept, via)
    got_w_in = _unslab([kept[1], direct, summed], me)

    small = jnp.concatenate([
        dh[PAD:CHUNK], d_norm_g, d_final_g, jnp.pad(db_f[:, :FOX_HEADS], ((0, 0), (0, D_MODEL - FOX_HEADS))),
        jnp.pad(loss_blk[0:1], ((0, 0), (0, D_MODEL - 128))),
        jnp.zeros((SMALL_ROWS - N_META - 4, D_MODEL), F32)], axis=0)
    fore = lambda a: jnp.transpose(a, (2, 0, 1))
    g_w_in, d_w_in, nm_w_in, nv_w_in, got_small = _sum_adamw_w_in(
        got_w_in, fore(w_in), fore(m_w_in), fore(v_w_in), small)
    g_w_out, d_w_out, nm_w_out, nv_w_out = _sum_adamw(got_w_out, w_out[0], m_w_out[0], v_w_out[0], 128, "adamw_w_out")

    row = lambda a: a.reshape(1, D_MODEL)
    (meta_o, norm_o, final_o, bias_o), loss_row = _adamw_small(
        got_small, me.astype(jnp.int32).reshape(1),
        (meta_tokens, m_meta_tokens, v_meta_tokens), (norm_g, m_norm_g, v_norm_g),
        (row(final_g), row(m_final_g), row(v_final_g)), (b_f, m_b_f, v_b_f))
    final_o = [a.reshape(D_MODEL) for a in final_o]

    back = lambda a: jnp.transpose(a, (1, 2, 0))
    outs = [[meta_o[k], norm_o[k], back(wk), bias_o[k], ok[None], final_o[k]]
            for k, (wk, ok) in enumerate(zip((g_w_in, d_w_in, nm_w_in, nv_w_in),
                                             (g_w_out, d_w_out, nm_w_out, nv_w_out)))]
    return (loss_row[0, 0], dh[CHUNK:][None], *outs[0], *outs[1], *outs[2], *outs[3])
```

```python
import numpy as np
import jax
import jax.numpy as jnp
from jax import lax
from jax.experimental import pallas as pl
from jax.experimental.pallas import tpu as pltpu

F32 = jnp.float32
BF16 = jnp.bfloat16

N_DEV = 8
N_CHIP = 4
D_MODEL = 1024
SEQ = 2048
N_META = 16
CHUNK = 128
PAD = CHUNK - N_META
T = SEQ + CHUNK
NCHUNK = T // CHUNK
D_MIX = 2048
RET_HEADS = 4
RET_DK = 128
RET_DV = 256
RET_W = 2 * RET_DK + 2 * RET_DV
FOX_HEADS = 16
FOX_D = 64
FOX_PAIRS = FOX_HEADS // 2
FOX_W = 4 * 128
FOX_BASE = RET_HEADS * RET_W
FF_BASE = FOX_BASE + FOX_PAIRS * FOX_W
D_IN = 7184
D_IN_PAD = 7296
W_BLK = D_IN // N_DEV
WO_BLK = D_MIX // N_DEV
META_BLK = D_MODEL // N_DEV
EPS = 1e-6
NEG_INF = -1e30
ROPE_BASE = 10000.0
LOG2E = 1.4426950408889634
LN2 = 0.6931471805599453

ADAM_LR = 0.001
ADAM_B1 = 0.9
ADAM_B2 = 0.999
ADAM_EPS = 1e-08
ADAM_WD = 0.01
ADAM_STEP = 10

SMALL_ROWS = 24
VMEM_LIMIT = 56 * 1024 * 1024
MESH = pl.DeviceIdType.MESH
ANY = pl.BlockSpec(memory_space=pl.ANY)

_NT = (((1,), (1,)), ((), ()))
_TN = (((0,), (0,)), ((), ()))


def _dot(a, b):
    return jnp.dot(a, b, preferred_element_type=F32)


def _dot_nt(a, b):
    return lax.dot_general(a, b, _NT, preferred_element_type=F32)


def _dot_tn(a, b):
    return lax.dot_general(a, b, _TN, preferred_element_type=F32)


def _params(**kw):
    return pltpu.CompilerParams(vmem_limit_bytes=VMEM_LIMIT, **kw)


def _silu_parts(g):
    sig = jax.nn.sigmoid(g)
    return g * sig, sig * (1.0 + g * (1.0 - sig))


def _tables():
    pos = np.arange(T, dtype=np.float32) - PAD
    inv = (ROPE_BASE ** (-np.arange(0, RET_DK, 2, dtype=np.float32) / RET_DK)).astype(np.float32)
    ang = pos[:, None] * inv[None, :]
    cos, sin = np.cos(ang), np.sin(ang)
    cosf = np.concatenate([cos, cos], axis=1).astype(np.float32)
    sins = np.concatenate([-sin, sin], axis=1).astype(np.float32)
    h = np.arange(RET_HEADS, dtype=np.float32)
    log_gamma = np.log1p(-np.exp2(-5.0 - h)).astype(np.float32)
    idx = np.arange(CHUNK, dtype=np.float32)
    diff = idx[:, None] - idx[None, :]
    dmask = np.where(diff[None] >= 0,
                     np.exp(log_gamma[:, None, None] * np.maximum(diff, 0.0)[None]), 0.0)
    zeta = np.exp(log_gamma[:, None] * (CHUNK - 1.0 - idx)[None, :])
    xi = np.exp(log_gamma[:, None] * (idx + 1.0)[None, :])
    cdec = np.exp(log_gamma * CHUNK)
    return (jnp.asarray(cosf), jnp.asarray(sins), jnp.asarray(dmask, F32),
            jnp.asarray(zeta[:, :, None], F32), jnp.asarray(xi[:, :, None], F32),
            jnp.asarray(cdec[:, None, None], F32))


W_STRIDE = 896
W_SLAB = 912
W_EDGE = W_SLAB - W_STRIDE
def _slab(block, me):
    shift = W_BLK - W_STRIDE
    return lax.switch(me, [lambda b, d=d: jnp.pad(b, ((shift * d, W_SLAB - W_BLK - shift * d), (0, 0)))
                           for d in range(N_DEV)], block)


def _unslab(slabs, me):
    shift = W_BLK - W_STRIDE
    return lax.switch(me, [lambda *s, d=d: jnp.stack([a[shift * d:shift * d + W_BLK] for a in s])
                           for d in range(N_DEV)], *slabs)


def _join_edges(slabs):
    last = slabs[:, W_STRIDE:]
    first = slabs[:, :W_EDGE] + jnp.concatenate([jnp.zeros_like(last[:1]), last[:-1]], axis=0)
    tail = jnp.pad(last[N_DEV - 1], ((0, 128 - W_EDGE), (0, 0)))
    return lax.dynamic_update_slice(slabs, first, (0, 0, 0)), tail


def _mm_tn(a, b, *, tm, tn, name):
    k, m = a.shape
    n = b.shape[1]
    assert m % tm == 0 and n % tn == 0

    def body(a_ref, b_ref, o_ref):
        o_ref[...] = _dot_tn(a_ref[...], b_ref[...]).astype(BF16)

    return pl.pallas_call(
        body, name=name, grid=(n // tn, m // tm),
        in_specs=[pl.BlockSpec((k, tm), lambda j, i: (0, i)),
                  pl.BlockSpec((k, tn), lambda j, i: (0, j))],
        out_specs=pl.BlockSpec((tm, tn), lambda j, i: (i, j)),
        out_shape=jax.ShapeDtypeStruct((m, n), BF16),
        compiler_params=_params(dimension_semantics=("arbitrary", "arbitrary")),
    )(a, b)


def _piece_spec(base, mult):
    def index(i):
        p = base + mult * i
        return p // 7, p % 7, 0
    return pl.BlockSpec((1, 128, D_MODEL), index)


_RET_PIECES = ((0, 1), (4, 1), (8, 2), (9, 2), (16, 2), (17, 2))
_FOX_PIECES = ((24, 1), (32, 1), (40, 1), (48, 1))


def _rms_z_ret(x, meta, g, slabs, tables):
    def body(x_hbm, m_ref, g_ref, *refs):
        pieces, tabs = refs[:6], refs[6:12]
        h_ref, u_ref, z_ref, raw_ref, y_ref, x_sem = refs[12:]

        @pl.when(pl.program_id(0) == 0)
        def _():
            tokens = pltpu.make_async_copy(x_hbm, h_ref.at[pl.ds(CHUNK, SEQ)], x_sem)
            tokens.start()
            h_ref[pl.ds(0, PAD), :] = jnp.zeros((PAD, D_MODEL), F32)
            h_ref[pl.ds(PAD, N_META), :] = m_ref[...]
            tokens.wait()
            h = h_ref[...]
            r = lax.rsqrt(jnp.mean(h * h, axis=-1, keepdims=True) + EPS)
            u_ref[...] = (h * r * g_ref[...]).astype(BF16)

        w = jnp.concatenate([p[0] for p in pieces], axis=0)
        z_ref[...] = _dot_nt(u_ref[...], w).astype(BF16)
        _ret_fwd_head(z_ref, *tabs, raw_ref, y_ref)

    whole = pl.BlockSpec((T, D_MODEL), lambda i: (0, 0))
    wide = pl.BlockSpec((T, RET_DV), lambda i: (0, i))
    return pl.pallas_call(
        body, name="rms_z_ret", grid=(RET_HEADS,),
        in_specs=[ANY, pl.BlockSpec((N_META, D_MODEL), lambda i: (0, 0)),
                  pl.BlockSpec((1, D_MODEL), lambda i: (0, 0))]
        + [_piece_spec(*bm) for bm in _RET_PIECES] + _RET_TABLE_SPECS,
        out_specs=[whole, whole, pl.BlockSpec((T, RET_W), lambda i: (0, i)), wide, wide],
        out_shape=[jax.ShapeDtypeStruct((T, D_MODEL), F32),
                   jax.ShapeDtypeStruct((T, D_MODEL), BF16),
                   jax.ShapeDtypeStruct((T, D_IN_PAD), BF16),
                   jax.ShapeDtypeStruct((T, RET_HEADS * RET_DV), F32),
                   jax.ShapeDtypeStruct((T, D_MIX), BF16)],
        scratch_shapes=[pltpu.SemaphoreType.DMA],
        compiler_params=_params(dimension_semantics=("arbitrary",)),
    )(x, meta, g, *([slabs] * 6), *tables)


def _z_fox(u, slabs, tail, z):
    def body(u_ref, *refs):
        pieces, t_ref, z_ref, zff_ref = refs[:4], refs[4], refs[6], refs[7]
        u = u_ref[...]
        w = jnp.concatenate([p[0] for p in pieces], axis=0)
        z_ref[...] = _dot_nt(u, w).astype(BF16)

        @pl.when(pl.program_id(0) == 0)
        def _():
            zff_ref[...] = _dot_nt(u, t_ref[...])

    return pl.pallas_call(
        body, name="z_fox", grid=(FOX_PAIRS,),
        in_specs=[pl.BlockSpec((T, D_MODEL), lambda i: (0, 0))] + [_piece_spec(*bm) for bm in _FOX_PIECES]
        + [pl.BlockSpec((128, D_MODEL), lambda i: (0, 0)), ANY],
        out_specs=[_FOX_Z_SPEC, pl.BlockSpec((T, 128), lambda i: (0, 0))],
        out_shape=[jax.ShapeDtypeStruct((T, D_IN_PAD), BF16),
                   jax.ShapeDtypeStruct((T, 128), F32)],
        input_output_aliases={6: 0},
        compiler_params=_params(dimension_semantics=("arbitrary",)),
    )(u, *([slabs] * 4), tail, z)


def _out_loss_dy(y, w_out_b, h_pad, target, g):
    tm = T // 4

    def body(y_ref, w_ref, h_ref, t_hbm, g_ref, d_ref, db_ref, dy_ref, loss_ref, dg_ref, t_buf, t_sem):
        i = pl.program_id(0)
        head = pltpu.make_async_copy(t_hbm.at[pl.ds(0, tm - CHUNK)], t_buf.at[pl.ds(CHUNK, tm - CHUNK)], t_sem)
        rest = pltpu.make_async_copy(t_hbm.at[pl.ds(pl.multiple_of(jnp.maximum(i, 1) * tm - CHUNK, 8), tm)],
                                     t_buf, t_sem)

        @pl.when(i == 0)
        def _():
            t_buf[pl.ds(0, CHUNK), :] = jnp.zeros((CHUNK, D_MODEL), F32)
            head.start()
            loss_ref[...] = jnp.zeros_like(loss_ref)
            dg_ref[...] = jnp.zeros_like(dg_ref)

        pl.when(i > 0)(rest.start)

        w = w_ref[...]
        o = _dot(y_ref[...], w) + h_ref[...]
        pl.when(i == 0)(head.wait)
        pl.when(i > 0)(rest.wait)
        token = lax.broadcasted_iota(jnp.int32, (tm, 1), 0) + i * tm >= CHUNK
        g = g_ref[...]
        r = lax.rsqrt(jnp.mean(o * o, axis=-1, keepdims=True) + EPS)
        xn = o * r
        e = jnp.where(token, xn * g - t_buf[...], 0.0)
        loss_ref[...] += jnp.full(loss_ref.shape, 0.5 / D_MODEL * jnp.sum(e * e), F32)
        do = e * (1.0 / D_MODEL)
        dg_ref[...] += jnp.sum(do * xn, axis=0, keepdims=True)
        dn = do * g
        d = r * (dn - xn * jnp.mean(dn * xn, axis=-1, keepdims=True))
        d_b = d.astype(BF16)
        d_ref[...] = d
        db_ref[...] = d_b
        dy_ref[...] = _dot_nt(d_b, w).astype(BF16)

    tile = pl.BlockSpec((tm, D_MODEL), lambda i: (i, 0))
    wide = pl.BlockSpec((tm, D_MIX), lambda i: (i, 0))
    return pl.pallas_call(
        body, name="out_loss_dy", grid=(T // tm,),
        in_specs=[wide, pl.BlockSpec((D_MIX, D_MODEL), lambda i: (0, 0)), tile, ANY,
                  pl.BlockSpec((1, D_MODEL), lambda i: (0, 0))],
        out_specs=[tile, tile, wide,
                   pl.BlockSpec((8, 128), lambda i: (0, 0)),
                   pl.BlockSpec((1, D_MODEL), lambda i: (0, 0))],
        out_shape=[jax.ShapeDtypeStruct((T, D_MODEL), F32),
                   jax.ShapeDtypeStruct((T, D_MODEL), BF16),
                   jax.ShapeDtypeStruct((T, D_MIX), BF16),
                   jax.ShapeDtypeStruct((8, 128), F32),
                   jax.ShapeDtypeStruct((1, D_MODEL), F32)],
        scratch_shapes=[pltpu.VMEM((tm, D_MODEL), F32), pltpu.SemaphoreType.DMA],
        compiler_params=_params(dimension_semantics=("arbitrary",)),
    )(y, w_out_b, h_pad, target, g)


def _coords():
    return lax.axis_index("x"), lax.axis_index("y"), lax.axis_index("c")


def _flip(v, bit):
    return 1 - v if bit else v


def _peer(x, y, c, r):
    return _flip(x, (r >> 2) & 1), _flip(y, (r >> 1) & 1), _flip(c, r & 1)


def _direct_exchange(ins, outs, send_sems, recv_sems, local_sems, gather):
    x, y, c = _coords()
    me = 4 * x + 2 * y + c

    def src(k, to_idx):
        return ins[k] if gather else ins[k].at[to_idx]

    local = [pltpu.make_async_copy(src(k, me), outs[k].at[me], local_sems.at[k])
             for k in range(len(ins))]
    sends, recvs = [], []
    for r in range(1, N_DEV):
        px, py, pc = _peer(x, y, c, r)
        peer = 4 * px + 2 * py + pc
        for k in range(len(ins)):
            sems = dict(send_sem=send_sems.at[k, r - 1], recv_sem=recv_sems.at[k, r - 1],
                        device_id=(px, py, pc), device_id_type=MESH)
            sends.append(pltpu.make_async_remote_copy(src_ref=src(k, peer), dst_ref=outs[k].at[me], **sems))
            recvs.append(pltpu.make_async_remote_copy(src_ref=src(k, peer), dst_ref=outs[k].at[peer], **sems))

    def start():
        for cp in local + sends:
            cp.start()

    def wait():
        for cp in recvs:
            cp.wait_recv()
        for cp in sends:
            cp.wait_send()
        for cp in local:
            cp.wait()

    return start, wait


def _exchange_sems(n_arr):
    return [pltpu.SemaphoreType.DMA((n_arr, N_DEV - 1)), pltpu.SemaphoreType.DMA((n_arr, N_DEV - 1)),
            pltpu.SemaphoreType.DMA((n_arr,))]


def _exchange_shape(a, gather):
    return jax.ShapeDtypeStruct(((N_DEV,) + a.shape) if gather else a.shape, a.dtype)


def _gather_two_level(arrays, name):
    n_arr = len(arrays)

    def body(*refs):
        ins, outs = refs[:n_arr], refs[n_arr:2 * n_arr]
        send_sems, recv_sems, local_sems = refs[2 * n_arr:]
        x, y, c = _coords()

        def slot(k, px, py, pc):
            return outs[k].at[4 * px + 2 * py + pc]

        def routed(core):
            me, sibling = (x, y, core), (x, y, 1 - core)
            xn, yn, dg = (1 - x, y), (x, 1 - y), (1 - x, 1 - y)
            (first, s_first), (second, s_second) = ((xn, 1), (yn, 2)) if core == 0 else ((yn, 2), (xn, 1))

            def copy(k, j, block, to, own=False):
                return pltpu.make_async_remote_copy(
                    src_ref=ins[k] if own else slot(k, *block), dst_ref=slot(k, *block),
                    send_sem=send_sems.at[k, j], recv_sem=recv_sems.at[k, j],
                    device_id=to, device_id_type=MESH)

            local = [pltpu.make_async_copy(ins[k], slot(k, *me), local_sems.at[k]) for k in range(n_arr)]
            sent = []
            for k in range(n_arr):
                sent += [copy(k, 0, me, sibling, True), copy(k, 1, me, (*xn, core), True),
                         copy(k, 2, me, (*yn, core), True)]
            for cp in local + sent:
                cp.start()

            def pass_on(k, j_from, j_to, block, targets):
                copy(k, j_from, block, me).wait_recv()
                for j, to in zip(j_to, targets):
                    cp = copy(k, j, block, to)
                    cp.start()
                    sent.append(cp)

            for k in range(n_arr):
                pass_on(k, s_first, (3, 3 + s_first), (*first, core), ((*second, core), sibling))
            for k in range(n_arr):
                pass_on(k, s_second, (3 + s_second,), (*second, core), (sibling,))
            for k in range(n_arr):
                pass_on(k, 3, (6,), (*dg, core), (sibling,))
            for k in range(n_arr):
                copy(k, 0, sibling, me).wait_recv()
                for j, chip in ((4, xn), (5, yn), (6, dg)):
                    copy(k, j, (*chip, 1 - core), me).wait_recv()
            for cp in sent:
                cp.wait_send()
            for cp in local:
                cp.wait()

        for core in (0, 1):
            pl.when(c == core)(lambda core=core: routed(core))

    return pl.pallas_call(
        body, name=name,
        in_specs=[ANY] * n_arr, out_specs=[ANY] * n_arr,
        out_shape=[_exchange_shape(a, True) for a in arrays],
        scratch_shapes=_exchange_sems(n_arr),
    )(*arrays)


def _piece_columns():
    pos = {}
    for h in range(RET_HEADS):
        for k, p in enumerate((h, 4 + h, 8 + 2 * h, 9 + 2 * h, 16 + 2 * h, 17 + 2 * h)):
            pos[p] = 6 * h + k
    for p in range(FOX_PAIRS):
        for i in range(4):
            pos[24 + 8 * i + p] = 24 + 4 * p + i
    pos[D_IN_PAD // 128 - 1] = D_IN_PAD // 128 - 1
    return np.array([pos[7 * d + j] for d in range(N_DEV) for j in range(8)], np.int32)


def _routes(core):
    x, y, _ = _coords()
    return ((1 - x, y), (x, 1 - y)) if core == 0 else ((x, 1 - y), (1 - x, y))


def _dwin_pair_slabs(dz, u, core):
    def body(order_ref, cols_ref, *refs):
        pieces, u_ref = refs[:8], refs[8]
        kept_ref, theirs_ref, via_ref, direct_ref = refs[9:13]
        send_buf, keep_buf, got_buf, push_send, push_recv, hop_send, hop_recv, load_sem = refs[13:]
        s = pl.program_id(0)
        x, y, c = _coords()
        cols = jnp.concatenate([p[...] for p in pieces[:7]] + [pieces[7][:, :W_EDGE]], axis=1)
        slab = _dot_tn(cols, u_ref[...])

        def push(k):
            return pltpu.make_async_remote_copy(
                src_ref=send_buf.at[k], dst_ref=theirs_ref.at[k],
                send_sem=push_send.at[k], recv_sem=push_recv.at[k],
                device_id=(x, y, 1 - c), device_id_type=MESH)

        def hop(k, core):
            first, _ = _routes(core)
            return pltpu.make_async_remote_copy(
                src_ref=keep_buf.at[k], dst_ref=direct_ref if k == 0 else via_ref,
                send_sem=hop_send.at[k], recv_sem=hop_recv.at[k],
                device_id=(*first, core), device_id_type=MESH)

        for k in range(N_CHIP):
            @pl.when(s == 2 * k)
            def _(k=k):
                send_buf[k] = slab.astype(BF16)
                push(k).start()

            @pl.when(s == 2 * k + 1)
            def _(k=k):
                push(k).wait_recv()
                load = pltpu.make_async_copy(theirs_ref.at[k], got_buf, load_sem)
                load.start()
                load.wait()
                total = (slab + got_buf[...].astype(F32)).astype(BF16)
                if k < 2:
                    keep_buf[k] = total
                    for core in (0, 1):
                        pl.when(c == core)(lambda core=core: hop(k, core).start())
                else:
                    kept_ref[0] = total

        @pl.when(s == 2 * N_CHIP - 1)
        def _():
            for core in (0, 1):
                @pl.when(c == core)
                def _(core=core):
                    for k in range(2):
                        hop(k, core).wait_recv()
                        hop(k, core).wait_send()
            for k in range(N_CHIP):
                push(k).wait_send()

    x, y = lax.axis_index("x"), lax.axis_index("y")
    xn, yn, dg, own = 2 * (1 - x) + y, 2 * x + 1 - y, 2 * (1 - x) + 1 - y, 2 * x + y
    mine = jnp.where(core == 0, jnp.stack([xn, dg, yn, own]), jnp.stack([yn, dg, xn, own]))
    sibs = jnp.where(core == 0, jnp.stack([yn, dg, xn, own]), jnp.stack([xn, dg, yn, own]))
    order = jnp.stack([2 * sibs + (1 - core), 2 * mine + core], axis=1).reshape(2 * N_CHIP).astype(jnp.int32)
    piece = lambda j: pl.BlockSpec((T, 128), lambda s, order_ref, cols_ref: (0, cols_ref[order_ref[s] * 8 + j]))
    slab = jax.ShapeDtypeStruct((W_SLAB, D_MODEL), BF16)
    kept, _, via, direct = pl.pallas_call(
        body, name="dwin_pair_slabs",
        grid_spec=pltpu.PrefetchScalarGridSpec(
            num_scalar_prefetch=2, grid=(2 * N_CHIP,),
            in_specs=[piece(j) for j in range(8)]
            + [pl.BlockSpec((T, D_MODEL), lambda s, order_ref, cols_ref: (0, 0))],
            out_specs=[pl.BlockSpec((1, W_SLAB, D_MODEL),
                                    lambda s, order_ref, cols_ref: (jnp.where(s >= 6, 1, 0), 0, 0)),
                       ANY, ANY, ANY],
            scratch_shapes=[pltpu.VMEM((N_CHIP, W_SLAB, D_MODEL), BF16), pltpu.VMEM((2, W_SLAB, D_MODEL), BF16),
                            pltpu.VMEM((W_SLAB, D_MODEL), BF16),
                            pltpu.SemaphoreType.DMA((N_CHIP,)), pltpu.SemaphoreType.DMA((N_CHIP,)),
                            pltpu.SemaphoreType.DMA((2,)), pltpu.SemaphoreType.DMA((2,)),
                            pltpu.SemaphoreType.DMA]),
        out_shape=[jax.ShapeDtypeStruct((2, W_SLAB, D_MODEL), BF16),
                   jax.ShapeDtypeStruct((N_CHIP, W_SLAB, D_MODEL), BF16), slab, slab],
        compiler_params=_params(dimension_semantics=("arbitrary",)),
    )(order, jnp.asarray(_piece_columns()), *([dz] * 8), u)
    return kept, via, direct


def _du_rms(dz, slabs, tail, h_pad, dout, g, kept, via):
    tm = 272
    steps = T // tm
    columns = _piece_columns().reshape(N_DEV, 8)

    def body(dz_ref, w_hbm, t_ref, h_ref, d_ref, g_ref, kept_ref, via_ref, dh_ref, dg_ref, got_ref,
             w_ref, via_buf, mine_buf, w_sems, send_sem, recv_sem, local_sems):
        i = pl.program_id(0)
        x, y, c = _coords()
        w_loads = [pltpu.make_async_copy(w_hbm.at[d], w_ref.at[d], w_sems.at[d]) for d in range(N_DEV)]

        @pl.when(i == 0)
        def _():
            for cp in w_loads:
                cp.start()

        def summed(core):
            _, second = _routes(core)
            return pltpu.make_async_remote_copy(
                src_ref=mine_buf, dst_ref=got_ref, send_sem=send_sem, recv_sem=recv_sem,
                device_id=(*second, core), device_id_type=MESH)

        @pl.when(i == 0)
        def _():
            loads = [pltpu.make_async_copy(kept_ref.at[0], mine_buf, local_sems.at[0]),
                     pltpu.make_async_copy(via_ref, via_buf, local_sems.at[1])]
            for cp in loads:
                cp.start()
            for cp in loads:
                cp.wait()
            mine_buf[...] = (mine_buf[...].astype(F32) + via_buf[...].astype(F32)).astype(BF16)
            for core in (0, 1):
                pl.when(c == core)(lambda core=core: summed(core).start())
            dg_ref[...] = jnp.zeros_like(dg_ref)

        du = _dot(dz_ref[:, pl.ds(FF_BASE, 128)], t_ref[...])
        for d in range(N_DEV):
            cols = jnp.concatenate([dz_ref[:, pl.ds(128 * int(columns[d, j]), 128)] for j in range(7)], axis=1)
            pl.when(i == 0)(w_loads[d].wait)
            du = du + _dot(cols, w_ref[d, pl.ds(0, W_STRIDE), :])
        h = h_ref[...]
        r = lax.rsqrt(jnp.mean(h * h, axis=-1, keepdims=True) + EPS)
        xn = h * r
        dg_ref[...] += jnp.sum(du * xn, axis=0, keepdims=True)
        dn = du * g_ref[...]
        dh_ref[...] = d_ref[...] + r * (dn - xn * jnp.mean(dn * xn, axis=-1, keepdims=True))

        for core in (0, 1):
            @pl.when(jnp.logical_and(c == core, i == steps - 1))
            def _(core=core):
                summed(core).wait_recv()
                summed(core).wait_send()

    tile = pl.BlockSpec((tm, D_MODEL), lambda i: (i, 0))
    return pl.pallas_call(
        body, name="du_rms", grid=(steps,),
        in_specs=[pl.BlockSpec((tm, D_IN_PAD), lambda i: (i, 0)), ANY,
                  pl.BlockSpec((128, D_MODEL), lambda i: (0, 0)),
                  tile, tile, pl.BlockSpec((1, D_MODEL), lambda i: (0, 0)), ANY, ANY],
        out_specs=[tile, pl.BlockSpec((1, D_MODEL), lambda i: (0, 0)), ANY],
        out_shape=[jax.ShapeDtypeStruct((T, D_MODEL), F32),
                   jax.ShapeDtypeStruct((1, D_MODEL), F32),
                   jax.ShapeDtypeStruct(via.shape, via.dtype)],
        scratch_shapes=[pltpu.VMEM(slabs.shape, slabs.dtype),
                        pltpu.VMEM(via.shape, via.dtype), pltpu.VMEM(via.shape, via.dtype),
                        pltpu.SemaphoreType.DMA((N_DEV,)),
                        pltpu.SemaphoreType.DMA, pltpu.SemaphoreType.DMA, pltpu.SemaphoreType.DMA((2,))],
        compiler_params=_params(dimension_semantics=("arbitrary",)),
    )(dz, slabs, tail, h_pad, dout, g, kept, via)


def _tri(lower):
    r = lax.broadcasted_iota(jnp.int32, (CHUNK, CHUNK), 0)
    c = lax.broadcasted_iota(jnp.int32, (CHUNK, CHUNK), 1)
    return jnp.where((r >= c) if lower else (r <= c), 1.0, 0.0).astype(F32)


def _row_valid(n):
    r = lax.broadcasted_iota(jnp.int32, (CHUNK, 128), 0) + n * CHUNK
    return r >= PAD


_FF_SPEC = pl.BlockSpec((T, 128), lambda i: (0, FF_BASE // 128))
_ZFF_SPEC = pl.BlockSpec((T, 128), lambda i: (0, 0))


def _forget_fwd(z, b_pad):
    def body(z_ref, b_ref, o_ref):
        tri = _tri(True)
        carry = jnp.zeros((1, 128), F32)
        for n in range(NCHUNK):
            rows = pl.ds(n * CHUNK, CHUNK)
            a = z_ref[rows, :] + b_ref[...]
            lf = -(jnp.maximum(-a, 0.0) + jnp.log(1.0 + jnp.exp(-jnp.abs(a))))
            lf = jnp.where(_row_valid(n), lf, 0.0)
            c = jnp.dot(tri, lf, precision=lax.Precision.HIGHEST,
                        preferred_element_type=F32) + carry
            carry = c[CHUNK - 1:CHUNK, :]
            o_ref[:, rows] = jnp.where(_row_valid(n), c * (-LOG2E), NEG_INF).T

    return pl.pallas_call(
        body, name="forget_fwd", grid=(1,),
        in_specs=[_ZFF_SPEC, pl.BlockSpec((1, 128), lambda i: (0, 0))],
        out_specs=pl.BlockSpec((128, T), lambda i: (0, 0)),
        out_shape=jax.ShapeDtypeStruct((128, T), F32),
        compiler_params=_params(dimension_semantics=("arbitrary",)),
    )(z, b_pad)


def _forget_bwd(z, b_pad, dc, dz):
    def body(z_ref, b_ref, dc_ref, dz_in, dff_ref, db_ref):
        tri = _tri(False)
        carry = jnp.zeros((1, 128), F32)
        db = jnp.zeros((1, 128), F32)
        for n in reversed(range(NCHUNK)):
            rows = pl.ds(n * CHUNK, CHUNK)
            dc_blk = jnp.concatenate([dc_ref[:, rows], jnp.zeros((128 - FOX_HEADS, CHUNK), F32)], axis=0).T
            dlf = jnp.dot(tri, dc_blk, precision=lax.Precision.HIGHEST,
                          preferred_element_type=F32) + carry
            carry = dlf[0:1, :]
            a = z_ref[rows, :] + b_ref[...]
            dff = jnp.where(_row_valid(n), dlf * jax.nn.sigmoid(-a), 0.0)
            dff_ref[rows, :] = dff.astype(BF16)
            db = db + jnp.sum(dff, axis=0, keepdims=True)
        db_ref[...] = db

    return pl.pallas_call(
        body, name="forget_bwd", grid=(1,),
        in_specs=[_ZFF_SPEC, pl.BlockSpec((1, 128), lambda i: (0, 0)),
                  pl.BlockSpec((FOX_HEADS, T), lambda i: (0, 0)), ANY],
        out_specs=[_FF_SPEC, pl.BlockSpec((1, 128), lambda i: (0, 0))],
        out_shape=[jax.ShapeDtypeStruct((T, D_IN_PAD), BF16),
                   jax.ShapeDtypeStruct((1, 128), F32)],
        input_output_aliases={3: 0},
        compiler_params=_params(dimension_semantics=("arbitrary",)),
    )(z, b_pad, dc, dz)


FOX_QB = 512
FOX_NQB = SEQ // FOX_QB


def _fox_block(b):
    lo = CHUNK + b * FOX_QB
    return pl.ds(lo, FOX_QB), lo, lo + FOX_QB


def _causal_bias():
    r = lax.broadcasted_iota(jnp.int32, (FOX_QB, FOX_QB), 0)
    c = lax.broadcasted_iota(jnp.int32, (FOX_QB, FOX_QB), 1)
    return jnp.where(c <= r, 0.0, NEG_INF).astype(F32)


def _fox_logits(q_blk, k_all, bias, causal, b):
    _, lo, hi = _fox_block(b)
    here = bias[:, lo:lo + 1]
    s_off = _dot_nt(q_blk, k_all[:lo]) + (bias[:, :lo] - here)
    s_dia = _dot_nt(q_blk, k_all[lo:hi]) + ((bias[:, lo:hi] - here) + causal)
    return s_off, s_dia


_FOX_Z_SPEC = pl.BlockSpec((T, FOX_W), lambda p: (0, FOX_BASE // FOX_W + p))
_FOX_BIAS_SPEC = pl.BlockSpec((2, 1, T), lambda p: (p, 0, 0))
_FOX_LSE_SPEC = pl.BlockSpec((2, T, 1), lambda p: (p, 0, 0))
_FOX_SCALE = FOX_D ** -0.5
_FOX_QSCALE = _FOX_SCALE * LOG2E


def _fox_fwd(z, bias, y, w_out_blk):
    last = FOX_PAIRS - 1

    def body(z_ref, b_ref, y_in, w_ref, a_ref, lse_ref, y_ref, wall_ref,
             send_sems, recv_sems, local_sems):
        start, wait = _direct_exchange([w_ref], [wall_ref], send_sems, recv_sems, local_sems, True)
        pl.when(pl.program_id(0) == 0)(start)

        causal = _causal_bias()
        a_ref[pl.ds(0, CHUNK), :] = jnp.zeros((CHUNK, 128), BF16)
        y_ref[pl.ds(0, CHUNK), :] = jnp.zeros((CHUNK, 128), BF16)
        for j in range(2):
            lanes = pl.ds(j * FOX_D, FOX_D)
            k_all = z_ref[:, pl.ds(128 + j * FOX_D, FOX_D)]
            v_all = z_ref[:, pl.ds(256 + j * FOX_D, FOX_D)]
            bias = b_ref[j]
            lse_ref[j, pl.ds(0, CHUNK), :] = jnp.zeros((CHUNK, 1), F32)
            for b in range(FOX_NQB):
                rows, lo, hi = _fox_block(b)
                q_blk = (z_ref[rows, lanes].astype(F32) * _FOX_QSCALE).astype(BF16)
                s_off, s_dia = _fox_logits(q_blk, k_all, bias, causal, b)
                m = jnp.maximum(jnp.max(s_off, axis=-1, keepdims=True),
                                jnp.max(s_dia, axis=-1, keepdims=True))
                e_off = jnp.exp2(s_off - m)
                e_dia = jnp.exp2(s_dia - m)
                total = jnp.sum(e_off, axis=-1, keepdims=True) + jnp.sum(e_dia, axis=-1, keepdims=True)
                o = (_dot(e_off.astype(BF16), v_all[:lo]) + _dot(e_dia.astype(BF16), v_all[lo:hi])) / total
                a_ref[rows, lanes] = o.astype(BF16)
                lse_ref[j, rows, :] = m + jnp.log(total) * LOG2E
                gate = _silu_parts(z_ref[rows, pl.ds(384 + j * FOX_D, FOX_D)].astype(F32))[0]
                y_ref[rows, lanes] = (o * gate).astype(BF16)

        pl.when(pl.program_id(0) == last)(wait)

    return pl.pallas_call(
        body, name="fox_fwd", grid=(FOX_PAIRS,),
        in_specs=[_FOX_Z_SPEC, _FOX_BIAS_SPEC, ANY, ANY],
        out_specs=[pl.BlockSpec((T, 128), lambda p: (0, p)), _FOX_LSE_SPEC,
                   pl.BlockSpec((T, 128), lambda p: (0, 8 + p)), ANY],
        out_shape=[jax.ShapeDtypeStruct((T, FOX_HEADS * FOX_D), BF16),
                   jax.ShapeDtypeStruct((FOX_HEADS, T, 1), F32),
                   jax.ShapeDtypeStruct((T, D_MIX), BF16),
                   _exchange_shape(w_out_blk, True)],
        input_output_aliases={2: 2},
        scratch_shapes=_exchange_sems(1),
        compiler_params=_params(dimension_semantics=("arbitrary",)),
    )(z, bias, y, w_out_blk)


def _fox_bwd(z, bias, a_f, lse, dy, dz, dwo_blocks):
    last = FOX_PAIRS - 1

    def body(z_ref, b_ref, a_ref, lse_ref, dy_ref, dz_in, dwo_ref, dz_ref, dc_ref, got_ref,
             kv_acc, dc_acc, send_sems, recv_sems, local_sems):
        start, wait = _direct_exchange([dwo_ref], [got_ref], send_sems, recv_sems, local_sems, False)
        pl.when(pl.program_id(0) == 0)(start)

        causal = _causal_bias()
        dz_ref[pl.ds(0, CHUNK), pl.ds(0, 128)] = jnp.zeros((CHUNK, 128), BF16)
        dz_ref[pl.ds(0, CHUNK), pl.ds(384, 128)] = jnp.zeros((CHUNK, 128), BF16)
        dk_rows, dv_rows = pl.ds(0, FOX_D), pl.ds(FOX_D, FOX_D)
        for j in range(2):
            lanes = pl.ds(j * FOX_D, FOX_D)
            k_all = z_ref[:, pl.ds(128 + j * FOX_D, FOX_D)]
            v_all = z_ref[:, pl.ds(256 + j * FOX_D, FOX_D)]
            bias = b_ref[j]
            kv_acc[...] = jnp.zeros_like(kv_acc)
            dc_acc[...] = jnp.zeros_like(dc_acc)
            for b in range(FOX_NQB):
                rows, lo, hi = _fox_block(b)
                off, dia = pl.ds(0, lo), pl.ds(lo, FOX_QB)
                q_blk = (z_ref[rows, lanes].astype(F32) * _FOX_QSCALE).astype(BF16)
                s_off, s_dia = _fox_logits(q_blk, k_all, bias, causal, b)
                lse_blk = lse_ref[j, rows, :]
                p_off, p_dia = jnp.exp2(s_off - lse_blk), jnp.exp2(s_dia - lse_blk)
                sg, dsg = _silu_parts(z_ref[rows, pl.ds(384 + j * FOX_D, FOX_D)].astype(F32))
                dyj = dy_ref[rows, lanes].astype(F32)
                dz_ref[rows, pl.ds(384 + j * FOX_D, FOX_D)] = (dyj * a_ref[rows, lanes].astype(F32) * dsg).astype(BF16)
                do_b = (dyj * sg).astype(BF16)
                dp_off = _dot_nt(do_b, v_all[:lo])
                dp_dia = _dot_nt(do_b, v_all[lo:hi])
                d = (jnp.sum(p_off * dp_off, axis=-1, keepdims=True)
                     + jnp.sum(p_dia * dp_dia, axis=-1, keepdims=True))
                ds_off = p_off * (dp_off - d)
                ds_dia = p_dia * (dp_dia - d)
                dc_acc[:, off] -= jnp.sum(ds_off, axis=0, keepdims=True)
                dc_acc[:, dia] -= jnp.sum(ds_dia, axis=0, keepdims=True)
                ds_off_b, ds_dia_b = ds_off.astype(BF16), ds_dia.astype(BF16)
                dq = _dot(ds_off_b, k_all[:lo]) + _dot(ds_dia_b, k_all[lo:hi])
                dz_ref[rows, lanes] = (dq * _FOX_SCALE).astype(BF16)
                kv_acc[dk_rows, off] += _dot_tn(q_blk, ds_off_b)
                kv_acc[dk_rows, dia] += _dot_tn(q_blk, ds_dia_b)
                kv_acc[dv_rows, off] += _dot_tn(do_b, p_off.astype(BF16))
                kv_acc[dv_rows, dia] += _dot_tn(do_b, p_dia.astype(BF16))
            for n in range(NCHUNK):
                rows = pl.ds(n * CHUNK, CHUNK)
                both = kv_acc[:, rows].T
                dz_ref[rows, pl.ds(128 + j * FOX_D, FOX_D)] = (both[:, :FOX_D] * LN2).astype(BF16)
                dz_ref[rows, pl.ds(256 + j * FOX_D, FOX_D)] = both[:, FOX_D:].astype(BF16)
            dc_ref[j] = dc_acc[...]

        pl.when(pl.program_id(0) == last)(wait)

    col = lambda base: pl.BlockSpec((T, 128), lambda p: (0, base + p))
    return pl.pallas_call(
        body, name="fox_bwd", grid=(FOX_PAIRS,),
        in_specs=[_FOX_Z_SPEC, _FOX_BIAS_SPEC, col(0), _FOX_LSE_SPEC, col(8), ANY, ANY],
        out_specs=[_FOX_Z_SPEC, _FOX_BIAS_SPEC, ANY],
        out_shape=[jax.ShapeDtypeStruct((T, D_IN_PAD), BF16),
                   jax.ShapeDtypeStruct((FOX_HEADS, 1, T), F32),
                   _exchange_shape(dwo_blocks, False)],
        input_output_aliases={5: 0},
        scratch_shapes=[pltpu.VMEM((2 * FOX_D, T), F32), pltpu.VMEM((1, T), F32)] + _exchange_sems(1),
        compiler_params=_params(dimension_semantics=("arbitrary",)),
    )(z, bias, a_f, lse, dy, dz, dwo_blocks)


def _rot(x, cosf, sins):
    return x * cosf + pltpu.roll(x, RET_DK // 2, 1) * sins


def _rot_t(d, cosf, sins):
    return d * cosf - pltpu.roll(d, RET_DK // 2, 1) * sins


_RET_Z_SPEC = pl.BlockSpec((T, RET_W), lambda h: (0, h))
_RET_TABLE_SPECS = [
    pl.BlockSpec((T, RET_DK), lambda h: (0, 0)),
    pl.BlockSpec((T, RET_DK), lambda h: (0, 0)),
    pl.BlockSpec((1, CHUNK, CHUNK), lambda h: (h, 0, 0)),
    pl.BlockSpec((1, CHUNK, 1), lambda h: (h, 0, 0)),
    pl.BlockSpec((1, CHUNK, 1), lambda h: (h, 0, 0)),
    pl.BlockSpec((1, 1, 1), lambda h: (h, 0, 0)),
]
_RQ, _RK = pl.ds(0, RET_DK), pl.ds(RET_DK, RET_DK)
_RV, _RG = pl.ds(2 * RET_DK, RET_DV), pl.ds(2 * RET_DK + RET_DV, RET_DV)
_RET_KSCALE = RET_DK ** -0.5


def _ret_fwd_head(z_ref, cos_ref, sin_ref, dm_ref, zeta_ref, xi_ref, cd_ref, raw_ref, y_ref):
    dmask, zeta, xi, cdec = dm_ref[0], zeta_ref[0], xi_ref[0], cd_ref[0]
    state = jnp.zeros((RET_DK, RET_DV), F32)
    for n in range(NCHUNK):
        rows = pl.ds(n * CHUNK, CHUNK)
        cosf, sins = cos_ref[rows, :], sin_ref[rows, :]
        qr = _rot(z_ref[rows, _RQ].astype(F32), cosf, sins)
        kr_b = (_rot(z_ref[rows, _RK].astype(F32), cosf, sins) * _RET_KSCALE).astype(BF16)
        v_b = z_ref[rows, _RV]
        a = _dot_nt(qr.astype(BF16), kr_b) * dmask
        out = _dot(a.astype(BF16), v_b) + _dot((qr * xi).astype(BF16), state.astype(BF16))
        state = state * cdec + _dot_tn(kr_b, (v_b.astype(F32) * zeta).astype(BF16))
        raw_ref[rows, :] = out
        r = lax.rsqrt(jnp.mean(out * out, axis=-1, keepdims=True) + EPS)
        y_ref[rows, :] = (out * r * _silu_parts(z_ref[rows, _RG].astype(F32))[0]).astype(BF16)


def _ret_bwd(z, tables, raw, dy):
    def body(z_ref, cos_ref, sin_ref, dm_ref, zeta_ref, xi_ref, cd_ref, raw_ref, dy_ref,
             dz_ref, st_ref):
        dmask, zeta, xi, cdec = dm_ref[0], zeta_ref[0], xi_ref[0], cd_ref[0]

        def rotated(n):
            rows = pl.ds(n * CHUNK, CHUNK)
            cosf, sins = cos_ref[rows, :], sin_ref[rows, :]
            qr = _rot(z_ref[rows, _RQ].astype(F32), cosf, sins)
            kr_b = (_rot(z_ref[rows, _RK].astype(F32), cosf, sins) * _RET_KSCALE).astype(BF16)
            return rows, cosf, sins, qr, kr_b

        state = jnp.zeros((RET_DK, RET_DV), F32)
        for n in range(NCHUNK):
            st_ref[n] = state.astype(BF16)
            if n + 1 < NCHUNK:
                rows, _, _, _, kr_b = rotated(n)
                state = state * cdec + _dot_tn(kr_b, (z_ref[rows, _RV].astype(F32) * zeta).astype(BF16))

        grad_state = jnp.zeros((RET_DK, RET_DV), F32)
        for n in reversed(range(NCHUNK)):
            rows, cosf, sins, qr, kr_b = rotated(n)
            qr_b = qr.astype(BF16)
            v_b = z_ref[rows, _RV]
            gs_b = grad_state.astype(BF16)
            o = raw_ref[rows, :]
            r = lax.rsqrt(jnp.mean(o * o, axis=-1, keepdims=True) + EPS)
            hn = o * r
            sg, dsg = _silu_parts(z_ref[rows, _RG].astype(F32))
            dyn = dy_ref[rows, :].astype(F32)
            dz_ref[rows, _RG] = (dyn * hn * dsg).astype(BF16)
            dhn = dyn * sg
            do_b = (r * (dhn - hn * jnp.mean(dhn * hn, axis=-1, keepdims=True))).astype(BF16)
            a_b = (_dot_nt(qr_b, kr_b) * dmask).astype(BF16)
            da_b = (_dot_nt(do_b, v_b) * dmask).astype(BF16)
            dqr = _dot(da_b, kr_b) + xi * _dot_nt(do_b, st_ref[n])
            dkr = _dot_tn(da_b, qr_b) + zeta * _dot_nt(v_b, gs_b)
            dv = _dot_tn(a_b, do_b) + zeta * _dot(kr_b, gs_b)
            grad_state = grad_state * cdec + _dot_tn((qr * xi).astype(BF16), do_b)
            dz_ref[rows, _RQ] = _rot_t(dqr, cosf, sins).astype(BF16)
            dz_ref[rows, _RK] = (_rot_t(dkr, cosf, sins) * _RET_KSCALE).astype(BF16)
            dz_ref[rows, _RV] = dv.astype(BF16)

    wide = pl.BlockSpec((T, RET_DV), lambda h: (0, h))
    return pl.pallas_call(
        body, name="ret_bwd", grid=(RET_HEADS,),
        in_specs=[_RET_Z_SPEC] + _RET_TABLE_SPECS + [wide, wide],
        out_specs=_RET_Z_SPEC,
        out_shape=jax.ShapeDtypeStruct((T, D_IN_PAD), BF16),
        scratch_shapes=[pltpu.VMEM((NCHUNK, RET_DK, RET_DV), BF16)],
        compiler_params=_params(dimension_semantics=("arbitrary",)),
    )(z, *tables, raw, dy)


def _adamw(w, g, m, v):
    m = ADAM_B1 * m + (1.0 - ADAM_B1) * g
    v = ADAM_B2 * v + (1.0 - ADAM_B2) * (g * g)
    m_hat = m / (1.0 - ADAM_B1 ** ADAM_STEP)
    v_hat = v / (1.0 - ADAM_B2 ** ADAM_STEP)
    delta = -ADAM_LR * (m_hat / (jnp.sqrt(v_hat) + ADAM_EPS) + ADAM_WD * w)
    return delta, m, v


def _sum_adamw(parts, w, m, v, rows, name):
    _, r_tot, cols = parts.shape
    assert r_tot % rows == 0

    def body(p_ref, w_ref, m_ref, v_ref, g_ref, d_ref, nm_ref, nv_ref):
        g = p_ref[0].astype(F32)
        for d in range(1, N_DEV):
            g = g + p_ref[d].astype(F32)
        delta, nm, nv = _adamw(w_ref[...], g, m_ref[...], v_ref[...])
        g_ref[...] = g
        d_ref[...] = delta
        nm_ref[...] = nm
        nv_ref[...] = nv

    blk = pl.BlockSpec((rows, cols), lambda i: (i, 0))
    return pl.pallas_call(
        body, name=name, grid=(r_tot // rows,),
        in_specs=[pl.BlockSpec((N_DEV, rows, cols), lambda i: (0, i, 0)), blk, blk, blk],
        out_specs=[blk] * 4,
        out_shape=[jax.ShapeDtypeStruct((r_tot, cols), F32)] * 4,
        compiler_params=_params(dimension_semantics=("arbitrary",)),
    )(parts, w, m, v)


def _sum_adamw_w_in(parts, w, m, v, small):
    n_part, r, c = parts.shape
    steps = c // 128

    def body(p_ref, w_hbm, m_hbm, v_hbm, s_ref, g_hbm, d_hbm, nm_hbm, nv_hbm, got_ref,
             in_buf, out_buf, in_sems, out_sems, send_sems, recv_sems, local_sems):
        start, wait = _direct_exchange([s_ref], [got_ref], send_sems, recv_sems, local_sems, True)
        i = pl.program_id(0)
        pl.when(i == 0)(start)
        slot = i % 2

        def loads(step, into):
            cols = pl.ds(pl.multiple_of(step * 128, 128), 128)
            return [pltpu.make_async_copy(h.at[:, 0, cols], in_buf.at[into, k], in_sems.at[into, k])
                    for k, h in enumerate((w_hbm, m_hbm, v_hbm))]

        cols = pl.ds(pl.multiple_of(i * 128, 128), 128)
        stores = [pltpu.make_async_copy(out_buf.at[k], h.at[:, 0, cols], out_sems.at[k])
                  for k, h in enumerate((g_hbm, d_hbm, nm_hbm, nv_hbm))]

        @pl.when(i == 0)
        def _():
            for cp in loads(0, 0):
                cp.start()

        @pl.when(i + 1 < steps)
        def _():
            for cp in loads(i + 1, 1 - slot):
                cp.start()

        g = p_ref[0].astype(F32)
        for d in range(1, n_part):
            g = g + p_ref[d].astype(F32)
        for cp in loads(i, slot):
            cp.wait()
        delta, nm, nv = _adamw(in_buf[slot, 0], g, in_buf[slot, 1], in_buf[slot, 2])

        @pl.when(i > 0)
        def _():
            for cp in stores:
                cp.wait()

        for k, val in enumerate((g, delta, nm, nv)):
            out_buf[k] = val
        for cp in stores:
            cp.start()

        @pl.when(i == steps - 1)
        def _():
            for cp in stores:
                cp.wait()
            wait()

    return pl.pallas_call(
        body, name="adamw_w_in", grid=(steps,),
        in_specs=[pl.BlockSpec((n_part, r, 128), lambda i: (0, 0, i)), ANY, ANY, ANY, ANY],
        out_specs=[ANY] * 5,
        out_shape=[jax.ShapeDtypeStruct((r, 1, c), F32)] * 4 + [_exchange_shape(small, True)],
        scratch_shapes=[pltpu.VMEM((2, 3, r, 128), F32), pltpu.VMEM((4, r, 128), F32),
                        pltpu.SemaphoreType.DMA((2, 3)), pltpu.SemaphoreType.DMA((4,))] + _exchange_sems(1),
        compiler_params=_params(dimension_semantics=("arbitrary",)),
    )(parts, w, m, v, small)


def _adamw_small(got, me, metas, norms, finals, biases):
    def body(me_ref, gm_ref, gr_ref, *refs):
        ins, outs = refs[:12], refs[12:]
        g_meta, g_rest = gm_ref[0], gr_ref[0]
        for d in range(1, N_DEV):
            g_meta, g_rest = g_meta + gm_ref[d], g_rest + gr_ref[d]
        grads = [g_meta, g_rest[0:1], g_rest[1:2], g_rest[2:3, :FOX_HEADS]]
        for k, g in enumerate(grads):
            w_ref, m_ref, v_ref = ins[3 * k:3 * k + 3]
            delta, new_m, new_v = _adamw(w_ref[...], g, m_ref[...], v_ref[...])
            for o_ref, val in zip(outs[4 * k:4 * k + 4], (g, delta, new_m, new_v)):
                o_ref[...] = val
        outs[16][...] = g_rest[3:4, :128]

    groups = (metas, norms, finals, biases)
    full = lambda a: pl.BlockSpec(a.shape, lambda i, me_ref: (0,) * a.ndim)
    flat = [a for grp in groups for a in grp]
    res = pl.pallas_call(
        body, name="adamw_small",
        grid_spec=pltpu.PrefetchScalarGridSpec(
            num_scalar_prefetch=1, grid=(1,),
            in_specs=[pl.BlockSpec((N_DEV, N_META, META_BLK), lambda i, me_ref: (0, 0, me_ref[0])),
                      pl.BlockSpec((N_DEV, 8, D_MODEL), lambda i, me_ref: (0, N_META // 8, 0))]
            + [full(a) for a in flat],
            out_specs=[full(grp[0]) for grp in groups for _ in range(4)]
            + [pl.BlockSpec((1, 128), lambda i, me_ref: (0, 0))]),
        out_shape=[jax.ShapeDtypeStruct(grp[0].shape, F32) for grp in groups for _ in range(4)]
        + [jax.ShapeDtypeStruct((1, 128), F32)],
        compiler_params=_params(dimension_semantics=("arbitrary",)),
    )(me, got, got, *flat)
    return [res[4 * k:4 * k + 4] for k in range(4)], res[16]


def kernel(x, meta_tokens, norm_g, w_in, b_f, w_out, final_g, loss_target, m_meta_tokens, m_norm_g, m_w_in, m_b_f, m_w_out, m_final_g, v_meta_tokens, v_norm_g, v_w_in, v_b_f, v_w_out, v_final_g):
    core = lax.axis_index("c")
    me = 4 * lax.axis_index("x") + 2 * lax.axis_index("y") + core
    tables = _tables()

    wt_all, meta_all = _gather_two_level([_slab(w_in[0].T.astype(BF16), me), meta_tokens], name="gather_w_in")
    slabs, tail = _join_edges(wt_all)
    meta_full = jnp.transpose(meta_all, (1, 0, 2)).reshape(N_META, D_MODEL)
    b_pad = jnp.pad(b_f, ((0, 0), (0, 128 - FOX_HEADS)))

    h_pad, u, z, raw, y = _rms_z_ret(x[0], meta_full, norm_g, slabs, tables)
    z, zff = _z_fox(u, slabs, tail, z)
    bias = _forget_fwd(zff, b_pad)[:FOX_HEADS].reshape(FOX_HEADS, 1, T)
    a_f, lse, y, w_out_all = _fox_fwd(z, bias, y, w_out[0].astype(BF16))
    w_out_b = w_out_all.reshape(D_MIX, D_MODEL)
    dout, dout_b, dy, loss_blk, d_final_g = _out_loss_dy(y, w_out_b, h_pad, loss_target[0],
                                                         final_g.reshape(1, D_MODEL))

    d_w_out = _mm_tn(y, dout_b, tm=D_MIX, tn=256, name="mm_dwout")
    dz = _ret_bwd(z, tables, raw, dy)
    dz, dc, got_w_out = _fox_bwd(z, bias, a_f, lse, dy, dz, d_w_out.reshape(N_DEV, WO_BLK, D_MODEL))
    dz, db_f = _forget_bwd(zff, b_pad, dc.reshape(FOX_HEADS, T), dz)

    kept, via, direct = _dwin_pair_slabs(dz, u, core)
    dh, d_norm_g, summed = _du_rms(dz, slabs, tail, h_pad, dout, norm_g, kept, via)
    got_w_in = _unslab([kept[1], direct, summed], me)

    small = jnp.concatenate([
        dh[PAD:CHUNK], d_norm_g, d_final_g, jnp.pad(db_f[:, :FOX_HEADS], ((0, 0), (0, D_MODEL - FOX_HEADS))),
        jnp.pad(loss_blk[0:1], ((0, 0), (0, D_MODEL - 128))),
        jnp.zeros((SMALL_ROWS - N_META - 4, D_MODEL), F32)], axis=0)
    fore = lambda a: jnp.transpose(a, (2, 0, 1))
    g_w_in, d_w_in, nm_w_in, nv_w_in, got_small = _sum_adamw_w_in(
        got_w_in, fore(w_in), fore(m_w_in), fore(v_w_in), small)
    g_w_out, d_w_out, nm_w_out, nv_w_out = _sum_adamw(got_w_out, w_out[0], m_w_out[0], v_w_out[0], 128, "adamw_w_out")

    row = lambda a: a.reshape(1, D_MODEL)
    (meta_o, norm_o, final_o, bias_o), loss_row = _adamw_small(
        got_small, me.astype(jnp.int32).reshape(1),
        (meta_tokens, m_meta_tokens, v_meta_tokens), (norm_g, m_norm_g, v_norm_g),
        (row(final_g), row(m_final_g), row(v_final_g)), (b_f, m_b_f, v_b_f))
    final_o = [a.reshape(D_MODEL) for a in final_o]

    back = lambda a: jnp.transpose(a, (1, 2, 0))
    outs = [[meta_o[k], norm_o[k], back(wk), bias_o[k], ok[None], final_o[k]]
            for k, (wk, ok) in enumerate(zip((g_w_in, d_w_in, nm_w_in, nv_w_in),
                                             (g_w_out, d_w_out, nm_w_out, nv_w_out)))]
    return (loss_row[0, 0], dh[CHUNK:][None], *outs[0], *outs[1], *outs[2], *outs[3])
```

```python
import numpy as np
import jax
import jax.numpy as jnp
from jax import lax
from jax.experimental import pallas as pl
from jax.experimental.pallas import tpu as pltpu

F32 = jnp.float32
BF16 = jnp.bfloat16

N_DEV = 8
N_CHIP = 4
D_MODEL = 1024
SEQ = 2048
N_META = 16
CHUNK = 128
PAD = CHUNK - N_META
T = SEQ + CHUNK
NCHUNK = T // CHUNK
D_MIX = 2048
RET_HEADS = 4
RET_DK = 128
RET_DV = 256
RET_W = 2 * RET_DK + 2 * RET_DV
FOX_HEADS = 16
FOX_D = 64
FOX_PAIRS = FOX_HEADS // 2
FOX_W = 4 * 128
FOX_BASE = RET_HEADS * RET_W
FF_BASE = FOX_BASE + FOX_PAIRS * FOX_W
D_IN = 7184
D_IN_PAD = 7296
W_BLK = D_IN // N_DEV
WO_BLK = D_MIX // N_DEV
META_BLK = D_MODEL // N_DEV
EPS = 1e-6
NEG_INF = -1e30
ROPE_BASE = 10000.0
LOG2E = 1.4426950408889634
LN2 = 0.6931471805599453

ADAM_LR = 0.001
ADAM_B1 = 0.9
ADAM_B2 = 0.999
ADAM_EPS = 1e-08
ADAM_WD = 0.01
ADAM_STEP = 10

SMALL_ROWS = 24
VMEM_LIMIT = 56 * 1024 * 1024
MESH = pl.DeviceIdType.MESH
ANY = pl.BlockSpec(memory_space=pl.ANY)

_NT = (((1,), (1,)), ((), ()))
_TN = (((0,), (0,)), ((), ()))


def _dot(a, b):
    return jnp.dot(a, b, preferred_element_type=F32)


def _dot_nt(a, b):
    return lax.dot_general(a, b, _NT, preferred_element_type=F32)


def _dot_tn(a, b):
    return lax.dot_general(a, b, _TN, preferred_element_type=F32)


def _params(**kw):
    return pltpu.CompilerParams(vmem_limit_bytes=VMEM_LIMIT, **kw)


def _silu_parts(g):
    sig = jax.nn.sigmoid(g)
    return g * sig, sig * (1.0 + g * (1.0 - sig))


def _tables():
    pos = np.arange(T, dtype=np.float32) - PAD
    inv = (ROPE_BASE ** (-np.arange(0, RET_DK, 2, dtype=np.float32) / RET_DK)).astype(np.float32)
    ang = pos[:, None] * inv[None, :]
    cos, sin = np.cos(ang), np.sin(ang)
    cosf = np.concatenate([cos, cos], axis=1).astype(np.float32)
    sins = np.concatenate([-sin, sin], axis=1).astype(np.float32)
    h = np.arange(RET_HEADS, dtype=np.float32)
    log_gamma = np.log1p(-np.exp2(-5.0 - h)).astype(np.float32)
    idx = np.arange(CHUNK, dtype=np.float32)
    diff = idx[:, None] - idx[None, :]
    dmask = np.where(diff[None] >= 0,
                     np.exp(log_gamma[:, None, None] * np.maximum(diff, 0.0)[None]), 0.0)
    zeta = np.exp(log_gamma[:, None] * (CHUNK - 1.0 - idx)[None, :])
    xi = np.exp(log_gamma[:, None] * (idx + 1.0)[None, :])
    cdec = np.exp(log_gamma * CHUNK)
    return (jnp.asarray(cosf), jnp.asarray(sins), jnp.asarray(dmask, F32),
            jnp.asarray(zeta[:, :, None], F32), jnp.asarray(xi[:, :, None], F32),
            jnp.asarray(cdec[:, None, None], F32))


W_STRIDE = 896
W_SLAB = 912
W_EDGE = W_SLAB - W_STRIDE
def _slab(block, me):
    shift = W_BLK - W_STRIDE
    return lax.switch(me, [lambda b, d=d: jnp.pad(b, ((shift * d, W_SLAB - W_BLK - shift * d), (0, 0)))
                           for d in range(N_DEV)], block)


def _unslab(slabs, me):
    shift = W_BLK - W_STRIDE
    return lax.switch(me, [lambda *s, d=d: jnp.stack([a[shift * d:shift * d + W_BLK] for a in s])
                           for d in range(N_DEV)], *slabs)


def _join_edges(slabs):
    last = slabs[:, W_STRIDE:]
    first = slabs[:, :W_EDGE] + jnp.concatenate([jnp.zeros_like(last[:1]), last[:-1]], axis=0)
    tail = jnp.pad(last[N_DEV - 1], ((0, 128 - W_EDGE), (0, 0)))
    return lax.dynamic_update_slice(slabs, first, (0, 0, 0)), tail


def _mm_tn(a, b, *, tm, tn, name):
    k, m = a.shape
    n = b.shape[1]
    assert m % tm == 0 and n % tn == 0

    def body(a_ref, b_ref, o_ref):
        o_ref[...] = _dot_tn(a_ref[...], b_ref[...]).astype(BF16)

    return pl.pallas_call(
        body, name=name, grid=(n // tn, m // tm),
        in_specs=[pl.BlockSpec((k, tm), lambda j, i: (0, i)),
                  pl.BlockSpec((k, tn), lambda j, i: (0, j))],
        out_specs=pl.BlockSpec((tm, tn), lambda j, i: (i, j)),
        out_shape=jax.ShapeDtypeStruct((m, n), BF16),
        compiler_params=_params(dimension_semantics=("arbitrary", "arbitrary")),
    )(a, b)


def _piece_spec(base, mult):
    def index(i):
        p = base + mult * i
        return p // 7, p % 7, 0
    return pl.BlockSpec((1, 128, D_MODEL), index)


_RET_PIECES = ((0, 1), (4, 1), (8, 2), (9, 2), (16, 2), (17, 2))
_FOX_PIECES = ((24, 1), (32, 1), (40, 1), (48, 1))


def _rms_z_ret(x, meta, g, slabs, tables):
    def body(x_hbm, m_ref, g_ref, *refs):
        pieces, tabs = refs[:6], refs[6:12]
        h_ref, u_ref, z_ref, raw_ref, y_ref, st_ref, x_sem = refs[12:]

        @pl.when(pl.program_id(0) == 0)
        def _():
            tokens = pltpu.make_async_copy(x_hbm, h_ref.at[pl.ds(CHUNK, SEQ)], x_sem)
            tokens.start()
            h_ref[pl.ds(0, PAD), :] = jnp.zeros((PAD, D_MODEL), F32)
            h_ref[pl.ds(PAD, N_META), :] = m_ref[...]
            tokens.wait()
            h = h_ref[...]
            r = lax.rsqrt(jnp.mean(h * h, axis=-1, keepdims=True) + EPS)
            u_ref[...] = (h * r * g_ref[...]).astype(BF16)

        w = jnp.concatenate([p[0] for p in pieces], axis=0)
        z_ref[...] = _dot_nt(u_ref[...], w).astype(BF16)
        _ret_fwd_head(z_ref, *tabs, raw_ref, y_ref, st_ref)

    whole = pl.BlockSpec((T, D_MODEL), lambda i: (0, 0))
    wide = pl.BlockSpec((T, RET_DV), lambda i: (0, i))
    return pl.pallas_call(
        body, name="rms_z_ret", grid=(RET_HEADS,),
        in_specs=[ANY, pl.BlockSpec((N_META, D_MODEL), lambda i: (0, 0)),
                  pl.BlockSpec((1, D_MODEL), lambda i: (0, 0))]
        + [_piece_spec(*bm) for bm in _RET_PIECES] + _RET_TABLE_SPECS,
        out_specs=[whole, whole, pl.BlockSpec((T, RET_W), lambda i: (0, i)), wide, wide, _RET_STATE_SPEC],
        out_shape=[jax.ShapeDtypeStruct((T, D_MODEL), F32),
                   jax.ShapeDtypeStruct((T, D_MODEL), BF16),
                   jax.ShapeDtypeStruct((T, D_IN_PAD), BF16),
                   jax.ShapeDtypeStruct((T, RET_HEADS * RET_DV), F32),
                   jax.ShapeDtypeStruct((T, D_MIX), BF16),
                   jax.ShapeDtypeStruct((RET_HEADS, NCHUNK, RET_DK, RET_DV), BF16)],
        scratch_shapes=[pltpu.SemaphoreType.DMA],
        compiler_params=_params(dimension_semantics=("arbitrary",)),
    )(x, meta, g, *([slabs] * 6), *tables)


def _z_fox(u, slabs, tail, z):
    def body(u_ref, *refs):
        pieces, t_ref, z_ref, zff_ref = refs[:4], refs[4], refs[6], refs[7]
        u = u_ref[...]
        w = jnp.concatenate([p[0] for p in pieces], axis=0)
        z_ref[...] = _dot_nt(u, w).astype(BF16)

        @pl.when(pl.program_id(0) == 0)
        def _():
            zff_ref[...] = _dot_nt(u, t_ref[...])

    return pl.pallas_call(
        body, name="z_fox", grid=(FOX_PAIRS,),
        in_specs=[pl.BlockSpec((T, D_MODEL), lambda i: (0, 0))] + [_piece_spec(*bm) for bm in _FOX_PIECES]
        + [pl.BlockSpec((128, D_MODEL), lambda i: (0, 0)), ANY],
        out_specs=[_FOX_Z_SPEC, pl.BlockSpec((T, 128), lambda i: (0, 0))],
        out_shape=[jax.ShapeDtypeStruct((T, D_IN_PAD), BF16),
                   jax.ShapeDtypeStruct((T, 128), F32)],
        input_output_aliases={6: 0},
        compiler_params=_params(dimension_semantics=("arbitrary",)),
    )(u, *([slabs] * 4), tail, z)


def _out_loss_dy(y, w_out_b, h_pad, target, g):
    tm = T // 4

    def body(y_ref, w_ref, h_ref, t_hbm, g_ref, d_ref, db_ref, dy_ref, loss_ref, dg_ref, t_buf, t_sem):
        i = pl.program_id(0)
        head = pltpu.make_async_copy(t_hbm.at[pl.ds(0, tm - CHUNK)], t_buf.at[pl.ds(CHUNK, tm - CHUNK)], t_sem)
        rest = pltpu.make_async_copy(t_hbm.at[pl.ds(pl.multiple_of(jnp.maximum(i, 1) * tm - CHUNK, 8), tm)],
                                     t_buf, t_sem)

        @pl.when(i == 0)
        def _():
            t_buf[pl.ds(0, CHUNK), :] = jnp.zeros((CHUNK, D_MODEL), F32)
            head.start()
            loss_ref[...] = jnp.zeros_like(loss_ref)
            dg_ref[...] = jnp.zeros_like(dg_ref)

        pl.when(i > 0)(rest.start)

        w = w_ref[...]
        o = _dot(y_ref[...], w) + h_ref[...]
        pl.when(i == 0)(head.wait)
        pl.when(i > 0)(rest.wait)
        token = lax.broadcasted_iota(jnp.int32, (tm, 1), 0) + i * tm >= CHUNK
        g = g_ref[...]
        r = lax.rsqrt(jnp.mean(o * o, axis=-1, keepdims=True) + EPS)
        xn = o * r
        e = jnp.where(token, xn * g - t_buf[...], 0.0)
        loss_ref[...] += jnp.full(loss_ref.shape, 0.5 / D_MODEL * jnp.sum(e * e), F32)
        do = e * (1.0 / D_MODEL)
        dg_ref[...] += jnp.sum(do * xn, axis=0, keepdims=True)
        dn = do * g
        d = r * (dn - xn * jnp.mean(dn * xn, axis=-1, keepdims=True))
        d_b = d.astype(BF16)
        d_ref[...] = d
        db_ref[...] = d_b
        dy_ref[...] = _dot_nt(d_b, w).astype(BF16)

    tile = pl.BlockSpec((tm, D_MODEL), lambda i: (i, 0))
    wide = pl.BlockSpec((tm, D_MIX), lambda i: (i, 0))
    return pl.pallas_call(
        body, name="out_loss_dy", grid=(T // tm,),
        in_specs=[wide, pl.BlockSpec((D_MIX, D_MODEL), lambda i: (0, 0)), tile, ANY,
                  pl.BlockSpec((1, D_MODEL), lambda i: (0, 0))],
        out_specs=[tile, tile, wide,
                   pl.BlockSpec((8, 128), lambda i: (0, 0)),
                   pl.BlockSpec((1, D_MODEL), lambda i: (0, 0))],
        out_shape=[jax.ShapeDtypeStruct((T, D_MODEL), F32),
                   jax.ShapeDtypeStruct((T, D_MODEL), BF16),
                   jax.ShapeDtypeStruct((T, D_MIX), BF16),
                   jax.ShapeDtypeStruct((8, 128), F32),
                   jax.ShapeDtypeStruct((1, D_MODEL), F32)],
        scratch_shapes=[pltpu.VMEM((tm, D_MODEL), F32), pltpu.SemaphoreType.DMA],
        compiler_params=_params(dimension_semantics=("arbitrary",)),
    )(y, w_out_b, h_pad, target, g)


def _coords():
    return lax.axis_index("x"), lax.axis_index("y"), lax.axis_index("c")


def _flip(v, bit):
    return 1 - v if bit else v


def _peer(x, y, c, r):
    return _flip(x, (r >> 2) & 1), _flip(y, (r >> 1) & 1), _flip(c, r & 1)


def _direct_exchange(ins, outs, send_sems, recv_sems, local_sems, gather):
    x, y, c = _coords()
    me = 4 * x + 2 * y + c

    def src(k, to_idx):
        return ins[k] if gather else ins[k].at[to_idx]

    local = [pltpu.make_async_copy(src(k, me), outs[k].at[me], local_sems.at[k])
             for k in range(len(ins))]
    sends, recvs = [], []
    for r in range(1, N_DEV):
        px, py, pc = _peer(x, y, c, r)
        peer = 4 * px + 2 * py + pc
        for k in range(len(ins)):
            sems = dict(send_sem=send_sems.at[k, r - 1], recv_sem=recv_sems.at[k, r - 1],
                        device_id=(px, py, pc), device_id_type=MESH)
            sends.append(pltpu.make_async_remote_copy(src_ref=src(k, peer), dst_ref=outs[k].at[me], **sems))
            recvs.append(pltpu.make_async_remote_copy(src_ref=src(k, peer), dst_ref=outs[k].at[peer], **sems))

    def start():
        for cp in local + sends:
            cp.start()

    def wait():
        for cp in recvs:
            cp.wait_recv()
        for cp in sends:
            cp.wait_send()
        for cp in local:
            cp.wait()

    return start, wait


def _exchange_sems(n_arr):
    return [pltpu.SemaphoreType.DMA((n_arr, N_DEV - 1)), pltpu.SemaphoreType.DMA((n_arr, N_DEV - 1)),
            pltpu.SemaphoreType.DMA((n_arr,))]


def _exchange_shape(a, gather):
    return jax.ShapeDtypeStruct(((N_DEV,) + a.shape) if gather else a.shape, a.dtype)


def _gather_two_level(arrays, name):
    n_arr = len(arrays)

    def body(*refs):
        ins, outs = refs[:n_arr], refs[n_arr:2 * n_arr]
        send_sems, recv_sems, local_sems = refs[2 * n_arr:]
        x, y, c = _coords()

        def slot(k, px, py, pc):
            return outs[k].at[4 * px + 2 * py + pc]

        def routed(core):
            me, sibling = (x, y, core), (x, y, 1 - core)
            xn, yn, dg = (1 - x, y), (x, 1 - y), (1 - x, 1 - y)
            (first, s_first), (second, s_second) = ((xn, 1), (yn, 2)) if core == 0 else ((yn, 2), (xn, 1))

            def copy(k, j, block, to, own=False):
                return pltpu.make_async_remote_copy(
                    src_ref=ins[k] if own else slot(k, *block), dst_ref=slot(k, *block),
                    send_sem=send_sems.at[k, j], recv_sem=recv_sems.at[k, j],
                    device_id=to, device_id_type=MESH)

            local = [pltpu.make_async_copy(ins[k], slot(k, *me), local_sems.at[k]) for k in range(n_arr)]
            sent = []
            for k in range(n_arr):
                sent += [copy(k, 0, me, sibling, True), copy(k, 1, me, (*xn, core), True),
                         copy(k, 2, me, (*yn, core), True)]
            for cp in local + sent:
                cp.start()

            def pass_on(k, j_from, j_to, block, targets):
                copy(k, j_from, block, me).wait_recv()
                for j, to in zip(j_to, targets):
                    cp = copy(k, j, block, to)
                    cp.start()
                    sent.append(cp)

            for k in range(n_arr):
                pass_on(k, s_first, (3, 3 + s_first), (*first, core), ((*second, core), sibling))
            for k in range(n_arr):
                pass_on(k, s_second, (3 + s_second,), (*second, core), (sibling,))
            for k in range(n_arr):
                pass_on(k, 3, (6,), (*dg, core), (sibling,))
            for k in range(n_arr):
                copy(k, 0, sibling, me).wait_recv()
                for j, chip in ((4, xn), (5, yn), (6, dg)):
                    copy(k, j, (*chip, 1 - core), me).wait_recv()
            for cp in sent:
                cp.wait_send()
            for cp in local:
                cp.wait()

        for core in (0, 1):
            pl.when(c == core)(lambda core=core: routed(core))

    return pl.pallas_call(
        body, name=name,
        in_specs=[ANY] * n_arr, out_specs=[ANY] * n_arr,
        out_shape=[_exchange_shape(a, True) for a in arrays],
        scratch_shapes=_exchange_sems(n_arr),
    )(*arrays)


def _piece_columns():
    pos = {}
    for h in range(RET_HEADS):
        for k, p in enumerate((h, 4 + h, 8 + 2 * h, 9 + 2 * h, 16 + 2 * h, 17 + 2 * h)):
            pos[p] = 6 * h + k
    for p in range(FOX_PAIRS):
        for i in range(4):
            pos[24 + 8 * i + p] = 24 + 4 * p + i
    pos[D_IN_PAD // 128 - 1] = D_IN_PAD // 128 - 1
    return np.array([pos[7 * d + j] for d in range(N_DEV) for j in range(8)], np.int32)


def _routes(core):
    x, y, _ = _coords()
    return ((1 - x, y), (x, 1 - y)) if core == 0 else ((x, 1 - y), (1 - x, y))


def _dwin_pair_slabs(dz, u, core):
    def body(order_ref, cols_ref, *refs):
        pieces, u_ref = refs[:8], refs[8]
        kept_ref, theirs_ref, via_ref, direct_ref = refs[9:13]
        send_buf, keep_buf, got_buf, push_send, push_recv, hop_send, hop_recv, load_sem = refs[13:]
        s = pl.program_id(0)
        x, y, c = _coords()
        cols = jnp.concatenate([p[...] for p in pieces[:7]] + [pieces[7][:, :W_EDGE]], axis=1)
        slab = _dot_tn(cols, u_ref[...])

        def push(k):
            return pltpu.make_async_remote_copy(
                src_ref=send_buf.at[k], dst_ref=theirs_ref.at[k],
                send_sem=push_send.at[k], recv_sem=push_recv.at[k],
                device_id=(x, y, 1 - c), device_id_type=MESH)

        def hop(k, core):
            first, _ = _routes(core)
            return pltpu.make_async_remote_copy(
                src_ref=keep_buf.at[k], dst_ref=direct_ref if k == 0 else via_ref,
                send_sem=hop_send.at[k], recv_sem=hop_recv.at[k],
                device_id=(*first, core), device_id_type=MESH)

        for k in range(N_CHIP):
            @pl.when(s == 2 * k)
            def _(k=k):
                send_buf[k] = slab.astype(BF16)
                push(k).start()

            @pl.when(s == 2 * k + 1)
            def _(k=k):
                push(k).wait_recv()
                load = pltpu.make_async_copy(theirs_ref.at[k], got_buf, load_sem)
                load.start()
                load.wait()
                total = (slab + got_buf[...].astype(F32)).astype(BF16)
                if k < 2:
                    keep_buf[k] = total
                    for core in (0, 1):
                        pl.when(c == core)(lambda core=core: hop(k, core).start())
                else:
                    kept_ref[0] = total

        @pl.when(s == 2 * N_CHIP - 1)
        def _():
            for core in (0, 1):
                @pl.when(c == core)
                def _(core=core):
                    for k in range(2):
                        hop(k, core).wait_recv()
                        hop(k, core).wait_send()
            for k in range(N_CHIP):
                push(k).wait_send()

    x, y = lax.axis_index("x"), lax.axis_index("y")
    xn, yn, dg, own = 2 * (1 - x) + y, 2 * x + 1 - y, 2 * (1 - x) + 1 - y, 2 * x + y
    mine = jnp.where(core == 0, jnp.stack([xn, dg, yn, own]), jnp.stack([yn, dg, xn, own]))
    sibs = jnp.where(core == 0, jnp.stack([yn, dg, xn, own]), jnp.stack([xn, dg, yn, own]))
    order = jnp.stack([2 * sibs + (1 - core), 2 * mine + core], axis=1).reshape(2 * N_CHIP).astype(jnp.int32)
    piece = lambda j: pl.BlockSpec((T, 128), lambda s, order_ref, cols_ref: (0, cols_ref[order_ref[s] * 8 + j]))
    slab = jax.ShapeDtypeStruct((W_SLAB, D_MODEL), BF16)
    kept, _, via, direct = pl.pallas_call(
        body, name="dwin_pair_slabs",
        grid_spec=pltpu.PrefetchScalarGridSpec(
            num_scalar_prefetch=2, grid=(2 * N_CHIP,),
            in_specs=[piece(j) for j in range(8)]
            + [pl.BlockSpec((T, D_MODEL), lambda s, order_ref, cols_ref: (0, 0))],
            out_specs=[pl.BlockSpec((1, W_SLAB, D_MODEL),
                                    lambda s, order_ref, cols_ref: (jnp.where(s >= 6, 1, 0), 0, 0)),
                       ANY, ANY, ANY],
            scratch_shapes=[pltpu.VMEM((N_CHIP, W_SLAB, D_MODEL), BF16), pltpu.VMEM((2, W_SLAB, D_MODEL), BF16),
                            pltpu.VMEM((W_SLAB, D_MODEL), BF16),
                            pltpu.SemaphoreType.DMA((N_CHIP,)), pltpu.SemaphoreType.DMA((N_CHIP,)),
                            pltpu.SemaphoreType.DMA((2,)), pltpu.SemaphoreType.DMA((2,)),
                            pltpu.SemaphoreType.DMA]),
        out_shape=[jax.ShapeDtypeStruct((2, W_SLAB, D_MODEL), BF16),
                   jax.ShapeDtypeStruct((N_CHIP, W_SLAB, D_MODEL), BF16), slab, slab],
        compiler_params=_params(dimension_semantics=("arbitrary",)),
    )(order, jnp.asarray(_piece_columns()), *([dz] * 8), u)
    return kept, via, direct


def _du_rms(dz, slabs, tail, h_pad, dout, g, kept, via):
    tm = 272
    steps = T // tm
    columns = _piece_columns().reshape(N_DEV, 8)

    def body(dz_ref, w_ref, t_ref, h_ref, d_ref, g_ref, kept_ref, via_ref, dh_ref, dg_ref, got_ref,
             via_buf, mine_buf, send_sem, recv_sem, local_sems):
        i = pl.program_id(0)
        x, y, c = _coords()

        def summed(core):
            _, second = _routes(core)
            return pltpu.make_async_remote_copy(
                src_ref=mine_buf, dst_ref=got_ref, send_sem=send_sem, recv_sem=recv_sem,
                device_id=(*second, core), device_id_type=MESH)

        @pl.when(i == 0)
        def _():
            loads = [pltpu.make_async_copy(kept_ref.at[0], mine_buf, local_sems.at[0]),
                     pltpu.make_async_copy(via_ref, via_buf, local_sems.at[1])]
            for cp in loads:
                cp.start()
            for cp in loads:
                cp.wait()
            mine_buf[...] = (mine_buf[...].astype(F32) + via_buf[...].astype(F32)).astype(BF16)
            for core in (0, 1):
                pl.when(c == core)(lambda core=core: summed(core).start())
            dg_ref[...] = jnp.zeros_like(dg_ref)

        du = _dot(dz_ref[:, pl.ds(FF_BASE, 128)], t_ref[...])
        for d in range(N_DEV):
            cols = jnp.concatenate([dz_ref[:, pl.ds(128 * int(columns[d, j]), 128)] for j in range(7)], axis=1)
            du = du + _dot(cols, w_ref[d, pl.ds(0, W_STRIDE), :])
        h = h_ref[...]
        r = lax.rsqrt(jnp.mean(h * h, axis=-1, keepdims=True) + EPS)
        xn = h * r
        dg_ref[...] += jnp.sum(du * xn, axis=0, keepdims=True)
        dn = du * g_ref[...]
        dh_ref[...] = d_ref[...] + r * (dn - xn * jnp.mean(dn * xn, axis=-1, keepdims=True))

        for core in (0, 1):
            @pl.when(jnp.logical_and(c == core, i == steps - 1))
            def _(core=core):
                summed(core).wait_recv()
                summed(core).wait_send()

    tile = pl.BlockSpec((tm, D_MODEL), lambda i: (i, 0))
    return pl.pallas_call(
        body, name="du_rms", grid=(steps,),
        in_specs=[pl.BlockSpec((tm, D_IN_PAD), lambda i: (i, 0)),
                  pl.BlockSpec((N_DEV, W_SLAB, D_MODEL), lambda i: (0, 0, 0)),
                  pl.BlockSpec((128, D_MODEL), lambda i: (0, 0)),
                  tile, tile, pl.BlockSpec((1, D_MODEL), lambda i: (0, 0)), ANY, ANY],
        out_specs=[tile, pl.BlockSpec((1, D_MODEL), lambda i: (0, 0)), ANY],
        out_shape=[jax.ShapeDtypeStruct((T, D_MODEL), F32),
                   jax.ShapeDtypeStruct((1, D_MODEL), F32),
                   jax.ShapeDtypeStruct(via.shape, via.dtype)],
        scratch_shapes=[pltpu.VMEM(via.shape, via.dtype), pltpu.VMEM(via.shape, via.dtype),
                        pltpu.SemaphoreType.DMA, pltpu.SemaphoreType.DMA, pltpu.SemaphoreType.DMA((2,))],
        compiler_params=_params(dimension_semantics=("arbitrary",)),
    )(dz, slabs, tail, h_pad, dout, g, kept, via)


def _tri(lower):
    r = lax.broadcasted_iota(jnp.int32, (CHUNK, CHUNK), 0)
    c = lax.broadcasted_iota(jnp.int32, (CHUNK, CHUNK), 1)
    return jnp.where((r >= c) if lower else (r <= c), 1.0, 0.0).astype(F32)


def _row_valid(n):
    r = lax.broadcasted_iota(jnp.int32, (CHUNK, 128), 0) + n * CHUNK
    return r >= PAD


_FF_SPEC = pl.BlockSpec((T, 128), lambda i: (0, FF_BASE // 128))
_ZFF_SPEC = pl.BlockSpec((T, 128), lambda i: (0, 0))


def _forget_fwd(z, b_pad):
    def body(z_ref, b_ref, o_ref):
        tri = _tri(True)
        carry = jnp.zeros((1, 128), F32)
        for n in range(NCHUNK):
            rows = pl.ds(n * CHUNK, CHUNK)
            a = z_ref[rows, :] + b_ref[...]
            lf = -(jnp.maximum(-a, 0.0) + jnp.log(1.0 + jnp.exp(-jnp.abs(a))))
            lf = jnp.where(_row_valid(n), lf, 0.0)
            c = jnp.dot(tri, lf, precision=lax.Precision.HIGHEST,
                        preferred_element_type=F32) + carry
            carry = c[CHUNK - 1:CHUNK, :]
            o_ref[:, rows] = jnp.where(_row_valid(n), c * (-LOG2E), NEG_INF).T

    return pl.pallas_call(
        body, name="forget_fwd", grid=(1,),
        in_specs=[_ZFF_SPEC, pl.BlockSpec((1, 128), lambda i: (0, 0))],
        out_specs=pl.BlockSpec((128, T), lambda i: (0, 0)),
        out_shape=jax.ShapeDtypeStruct((128, T), F32),
        compiler_params=_params(dimension_semantics=("arbitrary",)),
    )(z, b_pad)


def _forget_bwd(z, b_pad, dc, dz):
    def body(z_ref, b_ref, dc_ref, dz_in, dff_ref, db_ref):
        tri = _tri(False)
        carry = jnp.zeros((1, 128), F32)
        db = jnp.zeros((1, 128), F32)
        for n in reversed(range(NCHUNK)):
            rows = pl.ds(n * CHUNK, CHUNK)
            dc_blk = jnp.concatenate([dc_ref[:, rows], jnp.zeros((128 - FOX_HEADS, CHUNK), F32)], axis=0).T
            dlf = jnp.dot(tri, dc_blk, precision=lax.Precision.HIGHEST,
                          preferred_element_type=F32) + carry
            carry = dlf[0:1, :]
            a = z_ref[rows, :] + b_ref[...]
            dff = jnp.where(_row_valid(n), dlf * jax.nn.sigmoid(-a), 0.0)
            dff_ref[rows, :] = dff.astype(BF16)
            db = db + jnp.sum(dff, axis=0, keepdims=True)
        db_ref[...] = db

    return pl.pallas_call(
        body, name="forget_bwd", grid=(1,),
        in_specs=[_ZFF_SPEC, pl.BlockSpec((1, 128), lambda i: (0, 0)),
                  pl.BlockSpec((FOX_HEADS, T), lambda i: (0, 0)), ANY],
        out_specs=[_FF_SPEC, pl.BlockSpec((1, 128), lambda i: (0, 0))],
        out_shape=[jax.ShapeDtypeStruct((T, D_IN_PAD), BF16),
                   jax.ShapeDtypeStruct((1, 128), F32)],
        input_output_aliases={3: 0},
        compiler_params=_params(dimension_semantics=("arbitrary",)),
    )(z, b_pad, dc, dz)


FOX_QB = 512
FOX_NQB = SEQ // FOX_QB


def _fox_block(b):
    lo = CHUNK + b * FOX_QB
    return pl.ds(lo, FOX_QB), lo, lo + FOX_QB


def _causal_bias():
    r = lax.broadcasted_iota(jnp.int32, (FOX_QB, FOX_QB), 0)
    c = lax.broadcasted_iota(jnp.int32, (FOX_QB, FOX_QB), 1)
    return jnp.where(c <= r, 0.0, NEG_INF).astype(F32)


def _fox_logits(q_blk, k_all, bias, causal, b):
    _, lo, hi = _fox_block(b)
    here = bias[:, lo:lo + 1]
    s_off = _dot_nt(q_blk, k_all[:lo]) + (bias[:, :lo] - here)
    s_dia = _dot_nt(q_blk, k_all[lo:hi]) + ((bias[:, lo:hi] - here) + causal)
    return s_off, s_dia


_FOX_Z_SPEC = pl.BlockSpec((T, FOX_W), lambda p: (0, FOX_BASE // FOX_W + p))
_FOX_BIAS_SPEC = pl.BlockSpec((2, 1, T), lambda p: (p, 0, 0))
_FOX_LSE_SPEC = pl.BlockSpec((2, T, 1), lambda p: (p, 0, 0))
_FOX_SCALE = FOX_D ** -0.5
_FOX_QSCALE = _FOX_SCALE * LOG2E


def _fox_fwd(z, bias, y, w_out_blk):
    last = FOX_PAIRS - 1

    def body(z_ref, b_ref, y_in, w_ref, a_ref, lse_ref, y_ref, wall_ref,
             send_sems, recv_sems, local_sems):
        start, wait = _direct_exchange([w_ref], [wall_ref], send_sems, recv_sems, local_sems, True)
        pl.when(pl.program_id(0) == 0)(start)

        causal = _causal_bias()
        a_ref[pl.ds(0, CHUNK), :] = jnp.zeros((CHUNK, 128), F32)
        y_ref[pl.ds(0, CHUNK), :] = jnp.zeros((CHUNK, 128), BF16)
        for j in range(2):
            lanes = pl.ds(j * FOX_D, FOX_D)
            k_all = z_ref[:, pl.ds(128 + j * FOX_D, FOX_D)]
            v_all = z_ref[:, pl.ds(256 + j * FOX_D, FOX_D)]
            bias = b_ref[j]
            lse_ref[j, pl.ds(0, CHUNK), :] = jnp.zeros((CHUNK, 1), F32)
            for b in range(FOX_NQB):
                rows, lo, hi = _fox_block(b)
                q_blk = (z_ref[rows, lanes].astype(F32) * _FOX_QSCALE).astype(BF16)
                s_off, s_dia = _fox_logits(q_blk, k_all, bias, causal, b)
                m = jnp.maximum(jnp.max(s_off, axis=-1, keepdims=True),
                                jnp.max(s_dia, axis=-1, keepdims=True))
                e_off = jnp.exp2(s_off - m)
                e_dia = jnp.exp2(s_dia - m)
                total = jnp.sum(e_off, axis=-1, keepdims=True) + jnp.sum(e_dia, axis=-1, keepdims=True)
                o = (_dot(e_off.astype(BF16), v_all[:lo]) + _dot(e_dia.astype(BF16), v_all[lo:hi])) / total
                a_ref[rows, lanes] = o
                lse_ref[j, rows, :] = m + jnp.log(total) * LOG2E
                gate = _silu_parts(z_ref[rows, pl.ds(384 + j * FOX_D, FOX_D)].astype(F32))[0]
                y_ref[rows, lanes] = (o * gate).astype(BF16)

        pl.when(pl.program_id(0) == last)(wait)

    return pl.pallas_call(
        body, name="fox_fwd", grid=(FOX_PAIRS,),
        in_specs=[_FOX_Z_SPEC, _FOX_BIAS_SPEC, ANY, ANY],
        out_specs=[pl.BlockSpec((T, 128), lambda p: (0, p)), _FOX_LSE_SPEC,
                   pl.BlockSpec((T, 128), lambda p: (0, 8 + p)), ANY],
        out_shape=[jax.ShapeDtypeStruct((T, FOX_HEADS * FOX_D), F32),
                   jax.ShapeDtypeStruct((FOX_HEADS, T, 1), F32),
                   jax.ShapeDtypeStruct((T, D_MIX), BF16),
                   _exchange_shape(w_out_blk, True)],
        input_output_aliases={2: 2},
        scratch_shapes=_exchange_sems(1),
        compiler_params=_params(dimension_semantics=("arbitrary",)),
    )(z, bias, y, w_out_blk)


def _fox_bwd(z, bias, a_f, lse, dy, dz, dwo_blocks):
    last = FOX_PAIRS - 1

    def body(z_ref, b_ref, a_ref, lse_ref, dy_ref, dz_in, dwo_ref, dz_ref, dc_ref, got_ref,
             kv_acc, dc_acc, send_sems, recv_sems, local_sems):
        start, wait = _direct_exchange([dwo_ref], [got_ref], send_sems, recv_sems, local_sems, False)
        pl.when(pl.program_id(0) == 0)(start)

        causal = _causal_bias()
        dz_ref[pl.ds(0, CHUNK), pl.ds(0, 128)] = jnp.zeros((CHUNK, 128), BF16)
        dz_ref[pl.ds(0, CHUNK), pl.ds(384, 128)] = jnp.zeros((CHUNK, 128), BF16)
        dk_rows, dv_rows = pl.ds(0, FOX_D), pl.ds(FOX_D, FOX_D)
        for j in range(2):
            lanes = pl.ds(j * FOX_D, FOX_D)
            k_all = z_ref[:, pl.ds(128 + j * FOX_D, FOX_D)]
            v_all = z_ref[:, pl.ds(256 + j * FOX_D, FOX_D)]
            bias = b_ref[j]
            kv_acc[...] = jnp.zeros_like(kv_acc)
            dc_acc[...] = jnp.zeros_like(dc_acc)
            for b in range(FOX_NQB):
                rows, lo, hi = _fox_block(b)
                off, dia = pl.ds(0, lo), pl.ds(lo, FOX_QB)
                q_blk = (z_ref[rows, lanes].astype(F32) * _FOX_QSCALE).astype(BF16)
                s_off, s_dia = _fox_logits(q_blk, k_all, bias, causal, b)
                lse_blk = lse_ref[j, rows, :]
                p_off, p_dia = jnp.exp2(s_off - lse_blk), jnp.exp2(s_dia - lse_blk)
                sg, dsg = _silu_parts(z_ref[rows, pl.ds(384 + j * FOX_D, FOX_D)].astype(F32))
                dyj = dy_ref[rows, lanes].astype(F32)
                dz_ref[rows, pl.ds(384 + j * FOX_D, FOX_D)] = (dyj * a_ref[rows, lanes] * dsg).astype(BF16)
                do_b = (dyj * sg).astype(BF16)
                dp_off = _dot_nt(do_b, v_all[:lo])
                dp_dia = _dot_nt(do_b, v_all[lo:hi])
                d = (jnp.sum(p_off * dp_off, axis=-1, keepdims=True)
                     + jnp.sum(p_dia * dp_dia, axis=-1, keepdims=True))
                ds_off = p_off * (dp_off - d)
                ds_dia = p_dia * (dp_dia - d)
                dc_acc[:, off] -= jnp.sum(ds_off, axis=0, keepdims=True)
                dc_acc[:, dia] -= jnp.sum(ds_dia, axis=0, keepdims=True)
                ds_off_b, ds_dia_b = ds_off.astype(BF16), ds_dia.astype(BF16)
                dq = _dot(ds_off_b, k_all[:lo]) + _dot(ds_dia_b, k_all[lo:hi])
                dz_ref[rows, lanes] = (dq * _FOX_SCALE).astype(BF16)
                kv_acc[dk_rows, off] += _dot_tn(q_blk, ds_off_b)
                kv_acc[dk_rows, dia] += _dot_tn(q_blk, ds_dia_b)
                kv_acc[dv_rows, off] += _dot_tn(do_b, p_off.astype(BF16))
                kv_acc[dv_rows, dia] += _dot_tn(do_b, p_dia.astype(BF16))
            for n in range(NCHUNK):
                rows = pl.ds(n * CHUNK, CHUNK)
                both = kv_acc[:, rows].T
                dz_ref[rows, pl.ds(128 + j * FOX_D, FOX_D)] = (both[:, :FOX_D] * LN2).astype(BF16)
                dz_ref[rows, pl.ds(256 + j * FOX_D, FOX_D)] = both[:, FOX_D:].astype(BF16)
            dc_ref[j] = dc_acc[...]

        pl.when(pl.program_id(0) == last)(wait)

    col = lambda base: pl.BlockSpec((T, 128), lambda p: (0, base + p))
    return pl.pallas_call(
        body, name="fox_bwd", grid=(FOX_PAIRS,),
        in_specs=[_FOX_Z_SPEC, _FOX_BIAS_SPEC, col(0), _FOX_LSE_SPEC, col(8), ANY, ANY],
        out_specs=[_FOX_Z_SPEC, _FOX_BIAS_SPEC, ANY],
        out_shape=[jax.ShapeDtypeStruct((T, D_IN_PAD), BF16),
                   jax.ShapeDtypeStruct((FOX_HEADS, 1, T), F32),
                   _exchange_shape(dwo_blocks, False)],
        input_output_aliases={5: 0},
        scratch_shapes=[pltpu.VMEM((2 * FOX_D, T), F32), pltpu.VMEM((1, T), F32)] + _exchange_sems(1),
        compiler_params=_params(dimension_semantics=("arbitrary",)),
    )(z, bias, a_f, lse, dy, dz, dwo_blocks)


def _rot(x, cosf, sins):
    return x * cosf + pltpu.roll(x, RET_DK // 2, 1) * sins


def _rot_t(d, cosf, sins):
    return d * cosf - pltpu.roll(d, RET_DK // 2, 1) * sins


_RET_Z_SPEC = pl.BlockSpec((T, RET_W), lambda h: (0, h))
_RET_TABLE_SPECS = [
    pl.BlockSpec((T, RET_DK), lambda h: (0, 0)),
    pl.BlockSpec((T, RET_DK), lambda h: (0, 0)),
    pl.BlockSpec((1, CHUNK, CHUNK), lambda h: (h, 0, 0)),
    pl.BlockSpec((1, CHUNK, 1), lambda h: (h, 0, 0)),
    pl.BlockSpec((1, CHUNK, 1), lambda h: (h, 0, 0)),
    pl.BlockSpec((1, 1, 1), lambda h: (h, 0, 0)),
]
_RQ, _RK = pl.ds(0, RET_DK), pl.ds(RET_DK, RET_DK)
_RV, _RG = pl.ds(2 * RET_DK, RET_DV), pl.ds(2 * RET_DK + RET_DV, RET_DV)
_RET_KSCALE = RET_DK ** -0.5


_RET_STATE_SPEC = pl.BlockSpec((1, NCHUNK, RET_DK, RET_DV), lambda h: (h, 0, 0, 0))


def _ret_fwd_head(z_ref, cos_ref, sin_ref, dm_ref, zeta_ref, xi_ref, cd_ref, raw_ref, y_ref, st_ref):
    dmask, zeta, xi, cdec = dm_ref[0], zeta_ref[0], xi_ref[0], cd_ref[0]
    state = jnp.zeros((RET_DK, RET_DV), F32)
    for n in range(NCHUNK):
        rows = pl.ds(n * CHUNK, CHUNK)
        cosf, sins = cos_ref[rows, :], sin_ref[rows, :]
        qr = _rot(z_ref[rows, _RQ].astype(F32), cosf, sins)
        kr_b = (_rot(z_ref[rows, _RK].astype(F32), cosf, sins) * _RET_KSCALE).astype(BF16)
        v_b = z_ref[rows, _RV]
        a = _dot_nt(qr.astype(BF16), kr_b) * dmask
        state_b = state.astype(BF16)
        st_ref[0, n] = state_b
        out = _dot(a.astype(BF16), v_b) + _dot((qr * xi).astype(BF16), state_b)
        state = state * cdec + _dot_tn(kr_b, (v_b.astype(F32) * zeta).astype(BF16))
        raw_ref[rows, :] = out
        r = lax.rsqrt(jnp.mean(out * out, axis=-1, keepdims=True) + EPS)
        y_ref[rows, :] = (out * r * _silu_parts(z_ref[rows, _RG].astype(F32))[0]).astype(BF16)


def _ret_bwd(z, tables, raw, states, dy):
    def body(z_ref, cos_ref, sin_ref, dm_ref, zeta_ref, xi_ref, cd_ref, raw_ref, st_all, dy_ref, dz_ref):
        dmask, zeta, xi, cdec = dm_ref[0], zeta_ref[0], xi_ref[0], cd_ref[0]
        st_ref = st_all.at[0]

        grad_state = jnp.zeros((RET_DK, RET_DV), F32)
        for n in reversed(range(NCHUNK)):
            rows = pl.ds(n * CHUNK, CHUNK)
            cosf, sins = cos_ref[rows, :], sin_ref[rows, :]
            qr = _rot(z_ref[rows, _RQ].astype(F32), cosf, sins)
            kr_b = (_rot(z_ref[rows, _RK].astype(F32), cosf, sins) * _RET_KSCALE).astype(BF16)
            qr_b = qr.astype(BF16)
            v_b = z_ref[rows, _RV]
            gs_b = grad_state.astype(BF16)
            o = raw_ref[rows, :]
            r = lax.rsqrt(jnp.mean(o * o, axis=-1, keepdims=True) + EPS)
            hn = o * r
            sg, dsg = _silu_parts(z_ref[rows, _RG].astype(F32))
            dyn = dy_ref[rows, :].astype(F32)
            dz_ref[rows, _RG] = (dyn * hn * dsg).astype(BF16)
            dhn = dyn * sg
            do_b = (r * (dhn - hn * jnp.mean(dhn * hn, axis=-1, keepdims=True))).astype(BF16)
            a_b = (_dot_nt(qr_b, kr_b) * dmask).astype(BF16)
            da_b = (_dot_nt(do_b, v_b) * dmask).astype(BF16)
            dqr = _dot(da_b, kr_b) + xi * _dot_nt(do_b, st_ref[n])
            dkr = _dot_tn(da_b, qr_b) + zeta * _dot_nt(v_b, gs_b)
            dv = _dot_tn(a_b, do_b) + zeta * _dot(kr_b, gs_b)
            grad_state = grad_state * cdec + _dot_tn((qr * xi).astype(BF16), do_b)
            dz_ref[rows, _RQ] = _rot_t(dqr, cosf, sins).astype(BF16)
            dz_ref[rows, _RK] = (_rot_t(dkr, cosf, sins) * _RET_KSCALE).astype(BF16)
            dz_ref[rows, _RV] = dv.astype(BF16)

    wide = pl.BlockSpec((T, RET_DV), lambda h: (0, h))
    return pl.pallas_call(
        body, name="ret_bwd", grid=(RET_HEADS,),
        in_specs=[_RET_Z_SPEC] + _RET_TABLE_SPECS + [wide, _RET_STATE_SPEC, wide],
        out_specs=_RET_Z_SPEC,
        out_shape=jax.ShapeDtypeStruct((T, D_IN_PAD), BF16),
        compiler_params=_params(dimension_semantics=("arbitrary",)),
    )(z, *tables, raw, states, dy)


def _adamw(w, g, m, v):
    m = ADAM_B1 * m + (1.0 - ADAM_B1) * g
    v = ADAM_B2 * v + (1.0 - ADAM_B2) * (g * g)
    m_hat = m / (1.0 - ADAM_B1 ** ADAM_STEP)
    v_hat = v / (1.0 - ADAM_B2 ** ADAM_STEP)
    delta = -ADAM_LR * (m_hat / (jnp.sqrt(v_hat) + ADAM_EPS) + ADAM_WD * w)
    return delta, m, v


def _sum_adamw(parts, w, m, v, rows, name):
    _, r_tot, cols = parts.shape
    assert r_tot % rows == 0

    def body(p_ref, w_ref, m_ref, v_ref, g_ref, d_ref, nm_ref, nv_ref):
        g = p_ref[0].astype(F32)
        for d in range(1, N_DEV):
            g = g + p_ref[d].astype(F32)
        delta, nm, nv = _adamw(w_ref[...], g, m_ref[...], v_ref[...])
        g_ref[...] = g
        d_ref[...] = delta
        nm_ref[...] = nm
        nv_ref[...] = nv

    blk = pl.BlockSpec((rows, cols), lambda i: (i, 0))
    return pl.pallas_call(
        body, name=name, grid=(r_tot // rows,),
        in_specs=[pl.BlockSpec((N_DEV, rows, cols), lambda i: (0, i, 0)), blk, blk, blk],
        out_specs=[blk] * 4,
        out_shape=[jax.ShapeDtypeStruct((r_tot, cols), F32)] * 4,
        compiler_params=_params(dimension_semantics=("arbitrary",)),
    )(parts, w, m, v)


def _sum_adamw_w_in(parts, w, m, v, small):
    n_part, r, c = parts.shape
    steps = c // 128

    def body(p_ref, w_hbm, m_hbm, v_hbm, s_ref, g_hbm, d_hbm, nm_hbm, nv_hbm, got_ref,
             in_buf, out_buf, in_sems, out_sems, send_sems, recv_sems, local_sems):
        start, wait = _direct_exchange([s_ref], [got_ref], send_sems, recv_sems, local_sems, True)
        i = pl.program_id(0)
        pl.when(i == 0)(start)
        slot = i % 2

        def loads(step, into):
            cols = pl.ds(pl.multiple_of(step * 128, 128), 128)
            return [pltpu.make_async_copy(h.at[:, 0, cols], in_buf.at[into, k], in_sems.at[into, k])
                    for k, h in enumerate((w_hbm, m_hbm, v_hbm))]

        cols = pl.ds(pl.multiple_of(i * 128, 128), 128)
        stores = [pltpu.make_async_copy(out_buf.at[k], h.at[:, 0, cols], out_sems.at[k])
                  for k, h in enumerate((g_hbm, d_hbm, nm_hbm, nv_hbm))]

        @pl.when(i == 0)
        def _():
            for cp in loads(0, 0):
                cp.start()

        @pl.when(i + 1 < steps)
        def _():
            for cp in loads(i + 1, 1 - slot):
                cp.start()

        g = p_ref[0].astype(F32)
        for d in range(1, n_part):
            g = g + p_ref[d].astype(F32)
        for cp in loads(i, slot):
            cp.wait()
        delta, nm, nv = _adamw(in_buf[slot, 0], g, in_buf[slot, 1], in_buf[slot, 2])

        @pl.when(i > 0)
        def _():
            for cp in stores:
                cp.wait()

        for k, val in enumerate((g, delta, nm, nv)):
            out_buf[k] = val
        for cp in stores:
            cp.start()

        @pl.when(i == steps - 1)
        def _():
            for cp in stores:
                cp.wait()
            wait()

    return pl.pallas_call(
        body, name="adamw_w_in", grid=(steps,),
        in_specs=[pl.BlockSpec((n_part, r, 128), lambda i: (0, 0, i)), ANY, ANY, ANY, ANY],
        out_specs=[ANY] * 5,
        out_shape=[jax.ShapeDtypeStruct((r, 1, c), F32)] * 4 + [_exchange_shape(small, True)],
        scratch_shapes=[pltpu.VMEM((2, 3, r, 128), F32), pltpu.VMEM((4, r, 128), F32),
                        pltpu.SemaphoreType.DMA((2, 3)), pltpu.SemaphoreType.DMA((4,))] + _exchange_sems(1),
        compiler_params=_params(dimension_semantics=("arbitrary",)),
    )(parts, w, m, v, small)


def _adamw_small(got, me, metas, norms, finals, biases):
    def body(me_ref, gm_ref, gr_ref, *refs):
        ins, outs = refs[:12], refs[12:]
        g_meta, g_rest = gm_ref[0], gr_ref[0]
        for d in range(1, N_DEV):
            g_meta, g_rest = g_meta + gm_ref[d], g_rest + gr_ref[d]
        grads = [g_meta, g_rest[0:1], g_rest[1:2], g_rest[2:3, :FOX_HEADS]]
        for k, g in enumerate(grads):
            w_ref, m_ref, v_ref = ins[3 * k:3 * k + 3]
            delta, new_m, new_v = _adamw(w_ref[...], g, m_ref[...], v_ref[...])
            for o_ref, val in zip(outs[4 * k:4 * k + 4], (g, delta, new_m, new_v)):
                o_ref[...] = val
        outs[16][...] = g_rest[3:4, :128]

    groups = (metas, norms, finals, biases)
    full = lambda a: pl.BlockSpec(a.shape, lambda i, me_ref: (0,) * a.ndim)
    flat = [a for grp in groups for a in grp]
    res = pl.pallas_call(
        body, name="adamw_small",
        grid_spec=pltpu.PrefetchScalarGridSpec(
            num_scalar_prefetch=1, grid=(1,),
            in_specs=[pl.BlockSpec((N_DEV, N_META, META_BLK), lambda i, me_ref: (0, 0, me_ref[0])),
                      pl.BlockSpec((N_DEV, 8, D_MODEL), lambda i, me_ref: (0, N_META // 8, 0))]
            + [full(a) for a in flat],
            out_specs=[full(grp[0]) for grp in groups for _ in range(4)]
            + [pl.BlockSpec((1, 128), lambda i, me_ref: (0, 0))]),
        out_shape=[jax.ShapeDtypeStruct(grp[0].shape, F32) for grp in groups for _ in range(4)]
        + [jax.ShapeDtypeStruct((1, 128), F32)],
        compiler_params=_params(dimension_semantics=("arbitrary",)),
    )(me, got, got, *flat)
    return [res[4 * k:4 * k + 4] for k in range(4)], res[16]


def kernel(x, meta_tokens, norm_g, w_in, b_f, w_out, final_g, loss_target, m_meta_tokens, m_norm_g, m_w_in, m_b_f, m_w_out, m_final_g, v_meta_tokens, v_norm_g, v_w_in, v_b_f, v_w_out, v_final_g):
    core = lax.axis_index("c")
    me = 4 * lax.axis_index("x") + 2 * lax.axis_index("y") + core
    tables = _tables()

    wt_all, meta_all = _gather_two_level([_slab(w_in[0].T.astype(BF16), me), meta_tokens], name="gather_w_in")
    slabs, tail = _join_edges(wt_all)
    meta_full = jnp.transpose(meta_all, (1, 0, 2)).reshape(N_META, D_MODEL)
    b_pad = jnp.pad(b_f, ((0, 0), (0, 128 - FOX_HEADS)))

    h_pad, u, z, raw, y, states = _rms_z_ret(x[0], meta_full, norm_g, slabs, tables)
    z, zff = _z_fox(u, slabs, tail, z)
    bias = _forget_fwd(zff, b_pad)[:FOX_HEADS].reshape(FOX_HEADS, 1, T)
    a_f, lse, y, w_out_all = _fox_fwd(z, bias, y, w_out[0].astype(BF16))
    w_out_b = w_out_all.reshape(D_MIX, D_MODEL)
    dout, dout_b, dy, loss_blk, d_final_g = _out_loss_dy(y, w_out_b, h_pad, loss_target[0],
                                                         final_g.reshape(1, D_MODEL))

    d_w_out = _mm_tn(y, dout_b, tm=D_MIX, tn=256, name="mm_dwout")
    dz = _ret_bwd(z, tables, raw, states, dy)
    dz, dc, got_w_out = _fox_bwd(z, bias, a_f, lse, dy, dz, d_w_out.reshape(N_DEV, WO_BLK, D_MODEL))
    dz, db_f = _forget_bwd(zff, b_pad, dc.reshape(FOX_HEADS, T), dz)

    kept, via, direct = _dwin_pair_slabs(dz, u, core)
    dh, d_norm_g, summed = _du_rms(dz, slabs, tail, h_pad, dout, norm_g, kept, via)
    got_w_in = _unslab([kept[1], direct, summed], me)

    small = jnp.concatenate([
        dh[PAD:CHUNK], d_norm_g, d_final_g, jnp.pad(db_f[:, :FOX_HEADS], ((0, 0), (0, D_MODEL - FOX_HEADS))),
        jnp.pad(loss_blk[0:1], ((0, 0), (0, D_MODEL - 128))),
        jnp.zeros((SMALL_ROWS - N_META - 4, D_MODEL), F32)], axis=0)
    fore = lambda a: jnp.transpose(a, (2, 0, 1))
    g_w_in, d_w_in, nm_w_in, nv_w_in, got_small = _sum_adamw_w_in(
        got_w_in, fore(w_in), fore(m_w_in), fore(v_w_in), small)
    g_w_out, d_w_out, nm_w_out, nv_w_out = _sum_adamw(got_w_out, w_out[0], m_w_out[0], v_w_out[0], 128, "adamw_w_out")

    row = lambda a: a.reshape(1, D_MODEL)
    (meta_o, norm_o, final_o, bias_o), loss_row = _adamw_small(
        got_small, me.astype(jnp.int32).reshape(1),
        (meta_tokens, m_meta_tokens, v_meta_tokens), (norm_g, m_norm_g, v_norm_g),
        (row(final_g), row(m_final_g), row(v_final_g)), (b_f, m_b_f, v_b_f))
    final_o = [a.reshape(D_MODEL) for a in final_o]

    back = lambda a: jnp.transpose(a, (1, 2, 0))
    outs = [[meta_o[k], norm_o[k], back(wk), bias_o[k], ok[None], final_o[k]]
            for k, (wk, ok) in enumerate(zip((g_w_in, d_w_in, nm_w_in, nv_w_in),
                                             (g_w_out, d_w_out, nm_w_out, nv_w_out)))]
    return (loss_row[0, 0], dh[CHUNK:][None], *outs[0], *outs[1], *outs[2], *outs[3])
```

```python
import numpy as np
import jax
import jax.numpy as jnp
from jax import lax
from jax.experimental import pallas as pl
from jax.experimental.pallas import tpu as pltpu

F32 = jnp.float32
BF16 = jnp.bfloat16

N_DEV = 8
N_CHIP = 4
D_MODEL = 1024
SEQ = 2048
N_META = 16
CHUNK = 128
PAD = CHUNK - N_META
T = SEQ + CHUNK
NCHUNK = T // CHUNK
D_MIX = 2048
RET_HEADS = 4
RET_DK = 128
RET_DV = 256
RET_W = 2 * RET_DK + 2 * RET_DV
FOX_HEADS = 16
FOX_D = 64
FOX_PAIRS = FOX_HEADS // 2
FOX_W = 4 * 128
FOX_BASE = RET_HEADS * RET_W
FF_BASE = FOX_BASE + FOX_PAIRS * FOX_W
D_IN = 7184
D_IN_PAD = 7296
W_BLK = D_IN // N_DEV
WO_BLK = D_MIX // N_DEV
META_BLK = D_MODEL // N_DEV
EPS = 1e-6
NEG_INF = -1e30
ROPE_BASE = 10000.0
LOG2E = 1.4426950408889634
LN2 = 0.6931471805599453

ADAM_LR = 0.001
ADAM_B1 = 0.9
ADAM_B2 = 0.999
ADAM_EPS = 1e-08
ADAM_WD = 0.01
ADAM_STEP = 10

SMALL_ROWS = 24
VMEM_LIMIT = 56 * 1024 * 1024
MESH = pl.DeviceIdType.MESH
ANY = pl.BlockSpec(memory_space=pl.ANY)

_NT = (((1,), (1,)), ((), ()))
_TN = (((0,), (0,)), ((), ()))


def _dot(a, b):
    return jnp.dot(a, b, preferred_element_type=F32)


def _dot_nt(a, b):
    return lax.dot_general(a, b, _NT, preferred_element_type=F32)


def _dot_tn(a, b):
    return lax.dot_general(a, b, _TN, preferred_element_type=F32)


def _params(**kw):
    return pltpu.CompilerParams(vmem_limit_bytes=VMEM_LIMIT, **kw)


def _silu_parts(g):
    sig = jax.nn.sigmoid(g)
    return g * sig, sig * (1.0 + g * (1.0 - sig))


def _tables():
    pos = np.arange(T, dtype=np.float32) - PAD
    inv = (ROPE_BASE ** (-np.arange(0, RET_DK, 2, dtype=np.float32) / RET_DK)).astype(np.float32)
    ang = pos[:, None] * inv[None, :]
    cos, sin = np.cos(ang), np.sin(ang)
    cosf = np.concatenate([cos, cos], axis=1).astype(np.float32)
    sins = np.concatenate([-sin, sin], axis=1).astype(np.float32)
    h = np.arange(RET_HEADS, dtype=np.float32)
    log_gamma = np.log1p(-np.exp2(-5.0 - h)).astype(np.float32)
    idx = np.arange(CHUNK, dtype=np.float32)
    diff = idx[:, None] - idx[None, :]
    dmask = np.where(diff[None] >= 0,
                     np.exp(log_gamma[:, None, None] * np.maximum(diff, 0.0)[None]), 0.0)
    zeta = np.exp(log_gamma[:, None] * (CHUNK - 1.0 - idx)[None, :])
    xi = np.exp(log_gamma[:, None] * (idx + 1.0)[None, :])
    cdec = np.exp(log_gamma * CHUNK)
    return (jnp.asarray(cosf), jnp.asarray(sins), jnp.asarray(dmask, F32),
            jnp.asarray(zeta[:, :, None], F32), jnp.asarray(xi[:, :, None], F32),
            jnp.asarray(cdec[:, None, None], F32))


W_STRIDE = 896
W_SLAB = 912
W_EDGE = W_SLAB - W_STRIDE
def _slab(block, me):
    shift = W_BLK - W_STRIDE
    return lax.switch(me, [lambda b, d=d: jnp.pad(b, ((shift * d, W_SLAB - W_BLK - shift * d), (0, 0)))
                           for d in range(N_DEV)], block)


def _unslab(slabs, me):
    shift = W_BLK - W_STRIDE
    return lax.switch(me, [lambda *s, d=d: jnp.stack([a[shift * d:shift * d + W_BLK] for a in s])
                           for d in range(N_DEV)], *slabs)


def _join_edges(slabs):
    last = slabs[:, W_STRIDE:]
    first = slabs[:, :W_EDGE] + jnp.concatenate([jnp.zeros_like(last[:1]), last[:-1]], axis=0)
    tail = jnp.pad(last[N_DEV - 1], ((0, 128 - W_EDGE), (0, 0)))
    return lax.dynamic_update_slice(slabs, first, (0, 0, 0)), tail


def _mm_tn(a, b, *, tm, tn, name):
    k, m = a.shape
    n = b.shape[1]
    assert m % tm == 0 and n % tn == 0

    def body(a_ref, b_ref, o_ref):
        o_ref[...] = _dot_tn(a_ref[...], b_ref[...]).astype(BF16)

    return pl.pallas_call(
        body, name=name, grid=(n // tn, m // tm),
        in_specs=[pl.BlockSpec((k, tm), lambda j, i: (0, i)),
                  pl.BlockSpec((k, tn), lambda j, i: (0, j))],
        out_specs=pl.BlockSpec((tm, tn), lambda j, i: (i, j)),
        out_shape=jax.ShapeDtypeStruct((m, n), BF16),
        compiler_params=_params(dimension_semantics=("arbitrary", "arbitrary")),
    )(a, b)


def _piece_spec(base, mult):
    def index(i):
        p = base + mult * i
        return p // 7, p % 7, 0
    return pl.BlockSpec((1, 128, D_MODEL), index)


_RET_PIECES = ((0, 1), (4, 1), (8, 2), (9, 2), (16, 2), (17, 2))
_FOX_PIECES = ((24, 1), (32, 1), (40, 1), (48, 1))


def _rms_z_ret(x, meta, g, slabs, tables):
    def body(x_hbm, m_ref, g_ref, *refs):
        pieces, tabs = refs[:6], refs[6:12]
        h_ref, u_ref, z_ref, raw_ref, y_ref, st_ref, x_sem = refs[12:]

        @pl.when(pl.program_id(0) == 0)
        def _():
            tokens = pltpu.make_async_copy(x_hbm, h_ref.at[pl.ds(CHUNK, SEQ)], x_sem)
            tokens.start()
            h_ref[pl.ds(0, PAD), :] = jnp.zeros((PAD, D_MODEL), F32)
            h_ref[pl.ds(PAD, N_META), :] = m_ref[...]
            tokens.wait()
            h = h_ref[...]
            r = lax.rsqrt(jnp.mean(h * h, axis=-1, keepdims=True) + EPS)
            u_ref[...] = (h * r * g_ref[...]).astype(BF16)

        w = jnp.concatenate([p[0] for p in pieces], axis=0)
        z_ref[...] = _dot_nt(u_ref[...], w).astype(BF16)
        _ret_fwd_head(z_ref, *tabs, raw_ref, y_ref, st_ref)

    whole = pl.BlockSpec((T, D_MODEL), lambda i: (0, 0))
    wide = pl.BlockSpec((T, RET_DV), lambda i: (0, i))
    return pl.pallas_call(
        body, name="rms_z_ret", grid=(RET_HEADS,),
        in_specs=[ANY, pl.BlockSpec((N_META, D_MODEL), lambda i: (0, 0)),
                  pl.BlockSpec((1, D_MODEL), lambda i: (0, 0))]
        + [_piece_spec(*bm) for bm in _RET_PIECES] + _RET_TABLE_SPECS,
        out_specs=[whole, whole, pl.BlockSpec((T, RET_W), lambda i: (0, i)), wide, wide, _RET_STATE_SPEC],
        out_shape=[jax.ShapeDtypeStruct((T, D_MODEL), F32),
                   jax.ShapeDtypeStruct((T, D_MODEL), BF16),
                   jax.ShapeDtypeStruct((T, D_IN_PAD), BF16),
                   jax.ShapeDtypeStruct((T, RET_HEADS * RET_DV), F32),
                   jax.ShapeDtypeStruct((T, D_MIX), BF16),
                   jax.ShapeDtypeStruct((RET_HEADS, NCHUNK, RET_DK, RET_DV), BF16)],
        scratch_shapes=[pltpu.SemaphoreType.DMA],
        compiler_params=_params(dimension_semantics=("arbitrary",)),
    )(x, meta, g, *([slabs] * 6), *tables)


def _z_fox(u, slabs, tail, z):
    def body(u_ref, *refs):
        pieces, t_ref, z_ref, zff_ref = refs[:4], refs[4], refs[6], refs[7]
        u = u_ref[...]
        w = jnp.concatenate([p[0] for p in pieces], axis=0)
        z_ref[...] = _dot_nt(u, w).astype(BF16)

        @pl.when(pl.program_id(0) == 0)
        def _():
            zff_ref[...] = _dot_nt(u, t_ref[...])

    return pl.pallas_call(
        body, name="z_fox", grid=(FOX_PAIRS,),
        in_specs=[pl.BlockSpec((T, D_MODEL), lambda i: (0, 0))] + [_piece_spec(*bm) for bm in _FOX_PIECES]
        + [pl.BlockSpec((128, D_MODEL), lambda i: (0, 0)), ANY],
        out_specs=[_FOX_Z_SPEC, pl.BlockSpec((T, 128), lambda i: (0, 0))],
        out_shape=[jax.ShapeDtypeStruct((T, D_IN_PAD), BF16),
                   jax.ShapeDtypeStruct((T, 128), F32)],
        input_output_aliases={6: 0},
        compiler_params=_params(dimension_semantics=("arbitrary",)),
    )(u, *([slabs] * 4), tail, z)


def _out_loss_dy(y, w_out_b, h_pad, target, g):
    tm = T // 4

    def body(y_ref, w_ref, h_ref, t_hbm, g_ref, d_ref, db_ref, dy_ref, loss_ref, dg_ref, t_buf, t_sem):
        i = pl.program_id(0)
        head = pltpu.make_async_copy(t_hbm.at[pl.ds(0, tm - CHUNK)], t_buf.at[pl.ds(CHUNK, tm - CHUNK)], t_sem)
        rest = pltpu.make_async_copy(t_hbm.at[pl.ds(pl.multiple_of(jnp.maximum(i, 1) * tm - CHUNK, 8), tm)],
                                     t_buf, t_sem)

        @pl.when(i == 0)
        def _():
            t_buf[pl.ds(0, CHUNK), :] = jnp.zeros((CHUNK, D_MODEL), F32)
            head.start()
            loss_ref[...] = jnp.zeros_like(loss_ref)
            dg_ref[...] = jnp.zeros_like(dg_ref)

        pl.when(i > 0)(rest.start)

        w = w_ref[...]
        o = _dot(y_ref[...], w) + h_ref[...]
        pl.when(i == 0)(head.wait)
        pl.when(i > 0)(rest.wait)
        token = lax.broadcasted_iota(jnp.int32, (tm, 1), 0) + i * tm >= CHUNK
        g = g_ref[...]
        r = lax.rsqrt(jnp.mean(o * o, axis=-1, keepdims=True) + EPS)
        xn = o * r
        e = jnp.where(token, xn * g - t_buf[...], 0.0)
        loss_ref[...] += jnp.full(loss_ref.shape, 0.5 / D_MODEL * jnp.sum(e * e), F32)
        do = e * (1.0 / D_MODEL)
        dg_ref[...] += jnp.sum(do * xn, axis=0, keepdims=True)
        dn = do * g
        d = r * (dn - xn * jnp.mean(dn * xn, axis=-1, keepdims=True))
        d_b = d.astype(BF16)
        d_ref[...] = d
        db_ref[...] = d_b
        dy_ref[...] = _dot_nt(d_b, w).astype(BF16)

    tile = pl.BlockSpec((tm, D_MODEL), lambda i: (i, 0))
    wide = pl.BlockSpec((tm, D_MIX), lambda i: (i, 0))
    return pl.pallas_call(
        body, name="out_loss_dy", grid=(T // tm,),
        in_specs=[wide, pl.BlockSpec((D_MIX, D_MODEL), lambda i: (0, 0)), tile, ANY,
                  pl.BlockSpec((1, D_MODEL), lambda i: (0, 0))],
        out_specs=[tile, tile, wide,
                   pl.BlockSpec((8, 128), lambda i: (0, 0)),
                   pl.BlockSpec((1, D_MODEL), lambda i: (0, 0))],
        out_shape=[jax.ShapeDtypeStruct((T, D_MODEL), F32),
                   jax.ShapeDtypeStruct((T, D_MODEL), BF16),
                   jax.ShapeDtypeStruct((T, D_MIX), BF16),
                   jax.ShapeDtypeStruct((8, 128), F32),
                   jax.ShapeDtypeStruct((1, D_MODEL), F32)],
        scratch_shapes=[pltpu.VMEM((tm, D_MODEL), F32), pltpu.SemaphoreType.DMA],
        compiler_params=_params(dimension_semantics=("arbitrary",)),
    )(y, w_out_b, h_pad, target, g)


def _coords():
    return lax.axis_index("x"), lax.axis_index("y"), lax.axis_index("c")


def _flip(v, bit):
    return 1 - v if bit else v


def _peer(x, y, c, r):
    return _flip(x, (r >> 2) & 1), _flip(y, (r >> 1) & 1), _flip(c, r & 1)


def _direct_exchange(ins, outs, send_sems, recv_sems, local_sems, gather):
    x, y, c = _coords()
    me = 4 * x + 2 * y + c

    def src(k, to_idx):
        return ins[k] if gather else ins[k].at[to_idx]

    local = [pltpu.make_async_copy(src(k, me), outs[k].at[me], local_sems.at[k])
             for k in range(len(ins))]
    sends, recvs = [], []
    for r in range(1, N_DEV):
        px, py, pc = _peer(x, y, c, r)
        peer = 4 * px + 2 * py + pc
        for k in range(len(ins)):
            sems = dict(send_sem=send_sems.at[k, r - 1], recv_sem=recv_sems.at[k, r - 1],
                        device_id=(px, py, pc), device_id_type=MESH)
            sends.append(pltpu.make_async_remote_copy(src_ref=src(k, peer), dst_ref=outs[k].at[me], **sems))
            recvs.append(pltpu.make_async_remote_copy(src_ref=src(k, peer), dst_ref=outs[k].at[peer], **sems))

    def start():
        for cp in local + sends:
            cp.start()

    def wait():
        for cp in recvs:
            cp.wait_recv()
        for cp in sends:
            cp.wait_send()
        for cp in local:
            cp.wait()

    return start, wait


def _exchange_sems(n_arr):
    return [pltpu.SemaphoreType.DMA((n_arr, N_DEV - 1)), pltpu.SemaphoreType.DMA((n_arr, N_DEV - 1)),
            pltpu.SemaphoreType.DMA((n_arr,))]


def _exchange_shape(a, gather):
    return jax.ShapeDtypeStruct(((N_DEV,) + a.shape) if gather else a.shape, a.dtype)


def _gather_two_level(arrays, name):
    n_arr = len(arrays)

    def body(*refs):
        ins, outs = refs[:n_arr], refs[n_arr:2 * n_arr]
        send_sems, recv_sems, local_sems = refs[2 * n_arr:]
        x, y, c = _coords()

        def slot(k, px, py, pc):
            return outs[k].at[4 * px + 2 * py + pc]

        def routed(core):
            me, sibling = (x, y, core), (x, y, 1 - core)
            xn, yn, dg = (1 - x, y), (x, 1 - y), (1 - x, 1 - y)
            (first, s_first), (second, s_second) = ((xn, 1), (yn, 2)) if core == 0 else ((yn, 2), (xn, 1))

            def copy(k, j, block, to, own=False):
                return pltpu.make_async_remote_copy(
                    src_ref=ins[k] if own else slot(k, *block), dst_ref=slot(k, *block),
                    send_sem=send_sems.at[k, j], recv_sem=recv_sems.at[k, j],
                    device_id=to, device_id_type=MESH)

            local = [pltpu.make_async_copy(ins[k], slot(k, *me), local_sems.at[k]) for k in range(n_arr)]
            sent = []
            for k in range(n_arr):
                sent += [copy(k, 0, me, sibling, True), copy(k, 1, me, (*xn, core), True),
                         copy(k, 2, me, (*yn, core), True)]
            for cp in local + sent:
                cp.start()

            def pass_on(k, j_from, j_to, block, targets):
                copy(k, j_from, block, me).wait_recv()
                for j, to in zip(j_to, targets):
                    cp = copy(k, j, block, to)
                    cp.start()
                    sent.append(cp)

            for k in range(n_arr):
                pass_on(k, s_first, (3, 3 + s_first), (*first, core), ((*second, core), sibling))
            for k in range(n_arr):
                pass_on(k, s_second, (3 + s_second,), (*second, core), (sibling,))
            for k in range(n_arr):
                pass_on(k, 3, (6,), (*dg, core), (sibling,))
            for k in range(n_arr):
                copy(k, 0, sibling, me).wait_recv()
                for j, chip in ((4, xn), (5, yn), (6, dg)):
                    copy(k, j, (*chip, 1 - core), me).wait_recv()
            for cp in sent:
                cp.wait_send()
            for cp in local:
                cp.wait()

        for core in (0, 1):
            pl.when(c == core)(lambda core=core: routed(core))

    return pl.pallas_call(
        body, name=name,
        in_specs=[ANY] * n_arr, out_specs=[ANY] * n_arr,
        out_shape=[_exchange_shape(a, True) for a in arrays],
        scratch_shapes=_exchange_sems(n_arr),
    )(*arrays)


def _piece_columns():
    pos = {}
    for h in range(RET_HEADS):
        for k, p in enumerate((h, 4 + h, 8 + 2 * h, 9 + 2 * h, 16 + 2 * h, 17 + 2 * h)):
            pos[p] = 6 * h + k
    for p in range(FOX_PAIRS):
        for i in range(4):
            pos[24 + 8 * i + p] = 24 + 4 * p + i
    pos[D_IN_PAD // 128 - 1] = D_IN_PAD // 128 - 1
    return np.array([pos[7 * d + j] for d in range(N_DEV) for j in range(8)], np.int32)


def _routes(core):
    x, y, _ = _coords()
    return ((1 - x, y), (x, 1 - y)) if core == 0 else ((x, 1 - y), (1 - x, y))


def _dwin_pair_slabs(dz, u, core):
    def body(order_ref, cols_ref, *refs):
        pieces, u_ref = refs[:8], refs[8]
        kept_ref, theirs_ref, via_ref = refs[9:12]
        send_buf, keep_buf, got_buf, push_send, push_recv, hop_send, hop_recv, load_sem = refs[12:]
        s = pl.program_id(0)
        x, y, c = _coords()
        cols = jnp.concatenate([p[...] for p in pieces[:7]] + [pieces[7][:, :W_EDGE]], axis=1)
        slab = _dot_tn(cols, u_ref[...])

        def push(k):
            return pltpu.make_async_remote_copy(
                src_ref=send_buf.at[k], dst_ref=theirs_ref.at[k],
                send_sem=push_send.at[k], recv_sem=push_recv.at[k],
                device_id=(x, y, 1 - c), device_id_type=MESH)

        def hop(core):
            first, _ = _routes(core)
            return pltpu.make_async_remote_copy(
                src_ref=keep_buf, dst_ref=via_ref, send_sem=hop_send, recv_sem=hop_recv,
                device_id=(*first, core), device_id_type=MESH)

        for k in range(N_CHIP):
            @pl.when(s == 2 * k)
            def _(k=k):
                send_buf[k] = slab.astype(BF16)
                push(k).start()

            @pl.when(s == 2 * k + 1)
            def _(k=k):
                push(k).wait_recv()
                load = pltpu.make_async_copy(theirs_ref.at[k], got_buf, load_sem)
                load.start()
                load.wait()
                total = (slab + got_buf[...].astype(F32)).astype(BF16)
                if k == 0:
                    keep_buf[...] = total
                    for core in (0, 1):
                        pl.when(c == core)(lambda core=core: hop(core).start())
                else:
                    kept_ref[0] = total

        @pl.when(s == 2 * N_CHIP - 1)
        def _():
            for core in (0, 1):
                @pl.when(c == core)
                def _(core=core):
                    hop(core).wait_recv()
                    hop(core).wait_send()
            for k in range(N_CHIP):
                push(k).wait_send()

    x, y = lax.axis_index("x"), lax.axis_index("y")
    xn, yn, dg, own = 2 * (1 - x) + y, 2 * x + 1 - y, 2 * (1 - x) + 1 - y, 2 * x + y
    mine = jnp.where(core == 0, jnp.stack([dg, xn, yn, own]), jnp.stack([dg, yn, xn, own]))
    sibs = jnp.where(core == 0, jnp.stack([dg, yn, xn, own]), jnp.stack([dg, xn, yn, own]))
    order = jnp.stack([2 * sibs + (1 - core), 2 * mine + core], axis=1).reshape(2 * N_CHIP).astype(jnp.int32)
    piece = lambda j: pl.BlockSpec((T, 128), lambda s, order_ref, cols_ref: (0, cols_ref[order_ref[s] * 8 + j]))
    slab = jax.ShapeDtypeStruct((W_SLAB, D_MODEL), BF16)
    kept, _, via = pl.pallas_call(
        body, name="dwin_pair_slabs",
        grid_spec=pltpu.PrefetchScalarGridSpec(
            num_scalar_prefetch=2, grid=(2 * N_CHIP,),
            in_specs=[piece(j) for j in range(8)]
            + [pl.BlockSpec((T, D_MODEL), lambda s, order_ref, cols_ref: (0, 0))],
            out_specs=[pl.BlockSpec((1, W_SLAB, D_MODEL),
                                    lambda s, order_ref, cols_ref: (jnp.maximum(s - 2, 0) // 2, 0, 0)),
                       ANY, ANY],
            scratch_shapes=[pltpu.VMEM((N_CHIP, W_SLAB, D_MODEL), BF16), pltpu.VMEM((W_SLAB, D_MODEL), BF16),
                            pltpu.VMEM((W_SLAB, D_MODEL), BF16),
                            pltpu.SemaphoreType.DMA((N_CHIP,)), pltpu.SemaphoreType.DMA((N_CHIP,)),
                            pltpu.SemaphoreType.DMA, pltpu.SemaphoreType.DMA,
                            pltpu.SemaphoreType.DMA]),
        out_shape=[jax.ShapeDtypeStruct((3, W_SLAB, D_MODEL), BF16),
                   jax.ShapeDtypeStruct((N_CHIP, W_SLAB, D_MODEL), BF16), slab],
        compiler_params=_params(dimension_semantics=("arbitrary",)),
    )(order, jnp.asarray(_piece_columns()), *([dz] * 8), u)
    return kept, via


def _du_rms(dz, slabs, tail, h_pad, dout, g, kept, via):
    tm = 272
    steps = T // tm
    columns = _piece_columns().reshape(N_DEV, 8)

    def body(dz_ref, w_ref, t_ref, h_ref, d_ref, g_ref, kept_ref, via_ref, dh_ref, dg_ref,
             direct_ref, got_ref, via_buf, mine_buf, send_sems, recv_sems, local_sems):
        i = pl.program_id(0)
        x, y, c = _coords()

        def sends(core):
            first, second = _routes(core)
            return [pltpu.make_async_remote_copy(
                        src_ref=kept_ref.at[0], dst_ref=direct_ref, send_sem=send_sems.at[0],
                        recv_sem=recv_sems.at[0], device_id=(*first, core), device_id_type=MESH),
                    pltpu.make_async_remote_copy(
                        src_ref=mine_buf, dst_ref=got_ref, send_sem=send_sems.at[1],
                        recv_sem=recv_sems.at[1], device_id=(*second, core), device_id_type=MESH)]

        @pl.when(i == 0)
        def _():
            loads = [pltpu.make_async_copy(kept_ref.at[1], mine_buf, local_sems.at[0]),
                     pltpu.make_async_copy(via_ref, via_buf, local_sems.at[1])]
            for cp in loads:
                cp.start()
            for cp in loads:
                cp.wait()
            mine_buf[...] = (mine_buf[...].astype(F32) + via_buf[...].astype(F32)).astype(BF16)
            for core in (0, 1):
                @pl.when(c == core)
                def _(core=core):
                    for cp in sends(core):
                        cp.start()
            dg_ref[...] = jnp.zeros_like(dg_ref)

        du = _dot(dz_ref[:, pl.ds(FF_BASE, 128)], t_ref[...])
        for d in range(N_DEV):
            cols = jnp.concatenate([dz_ref[:, pl.ds(128 * int(columns[d, j]), 128)] for j in range(7)], axis=1)
            du = du + _dot(cols, w_ref[d, pl.ds(0, W_STRIDE), :])
        h = h_ref[...]
        r = lax.rsqrt(jnp.mean(h * h, axis=-1, keepdims=True) + EPS)
        xn = h * r
        dg_ref[...] += jnp.sum(du * xn, axis=0, keepdims=True)
        dn = du * g_ref[...]
        dh_ref[...] = d_ref[...] + r * (dn - xn * jnp.mean(dn * xn, axis=-1, keepdims=True))

        for core in (0, 1):
            @pl.when(jnp.logical_and(c == core, i == steps - 1))
            def _(core=core):
                for cp in sends(core):
                    cp.wait_recv()
                    cp.wait_send()

    tile = pl.BlockSpec((tm, D_MODEL), lambda i: (i, 0))
    slab = jax.ShapeDtypeStruct(via.shape, via.dtype)
    return pl.pallas_call(
        body, name="du_rms", grid=(steps,),
        in_specs=[pl.BlockSpec((tm, D_IN_PAD), lambda i: (i, 0)),
                  pl.BlockSpec((N_DEV, W_SLAB, D_MODEL), lambda i: (0, 0, 0)),
                  pl.BlockSpec((128, D_MODEL), lambda i: (0, 0)),
                  tile, tile, pl.BlockSpec((1, D_MODEL), lambda i: (0, 0)), ANY, ANY],
        out_specs=[tile, pl.BlockSpec((1, D_MODEL), lambda i: (0, 0)), ANY, ANY],
        out_shape=[jax.ShapeDtypeStruct((T, D_MODEL), F32),
                   jax.ShapeDtypeStruct((1, D_MODEL), F32), slab, slab],
        scratch_shapes=[pltpu.VMEM(via.shape, via.dtype), pltpu.VMEM(via.shape, via.dtype),
                        pltpu.SemaphoreType.DMA((2,)), pltpu.SemaphoreType.DMA((2,)),
                        pltpu.SemaphoreType.DMA((2,))],
        compiler_params=_params(dimension_semantics=("arbitrary",)),
    )(dz, slabs, tail, h_pad, dout, g, kept, via)


def _tri(lower):
    r = lax.broadcasted_iota(jnp.int32, (CHUNK, CHUNK), 0)
    c = lax.broadcasted_iota(jnp.int32, (CHUNK, CHUNK), 1)
    return jnp.where((r >= c) if lower else (r <= c), 1.0, 0.0).astype(F32)


def _row_valid(n):
    r = lax.broadcasted_iota(jnp.int32, (CHUNK, 128), 0) + n * CHUNK
    return r >= PAD


_FF_SPEC = pl.BlockSpec((T, 128), lambda i: (0, FF_BASE // 128))
_ZFF_SPEC = pl.BlockSpec((T, 128), lambda i: (0, 0))


def _forget_fwd(z, b_pad):
    def body(z_ref, b_ref, o_ref):
        tri = _tri(True)
        carry = jnp.zeros((1, 128), F32)
        for n in range(NCHUNK):
            rows = pl.ds(n * CHUNK, CHUNK)
            a = z_ref[rows, :] + b_ref[...]
            lf = -(jnp.maximum(-a, 0.0) + jnp.log(1.0 + jnp.exp(-jnp.abs(a))))
            lf = jnp.where(_row_valid(n), lf, 0.0)
            c = jnp.dot(tri, lf, precision=lax.Precision.HIGHEST,
                        preferred_element_type=F32) + carry
            carry = c[CHUNK - 1:CHUNK, :]
            o_ref[:, rows] = jnp.where(_row_valid(n), c * (-LOG2E), NEG_INF).T

    return pl.pallas_call(
        body, name="forget_fwd", grid=(1,),
        in_specs=[_ZFF_SPEC, pl.BlockSpec((1, 128), lambda i: (0, 0))],
        out_specs=pl.BlockSpec((128, T), lambda i: (0, 0)),
        out_shape=jax.ShapeDtypeStruct((128, T), F32),
        compiler_params=_params(dimension_semantics=("arbitrary",)),
    )(z, b_pad)


def _forget_bwd(z, b_pad, dc, dz):
    def body(z_ref, b_ref, dc_ref, dz_in, dff_ref, db_ref):
        tri = _tri(False)
        carry = jnp.zeros((1, 128), F32)
        db = jnp.zeros((1, 128), F32)
        for n in reversed(range(NCHUNK)):
            rows = pl.ds(n * CHUNK, CHUNK)
            dc_blk = jnp.concatenate([dc_ref[:, rows], jnp.zeros((128 - FOX_HEADS, CHUNK), F32)], axis=0).T
            dlf = jnp.dot(tri, dc_blk, precision=lax.Precision.HIGHEST,
                          preferred_element_type=F32) + carry
            carry = dlf[0:1, :]
            a = z_ref[rows, :] + b_ref[...]
            dff = jnp.where(_row_valid(n), dlf * jax.nn.sigmoid(-a), 0.0)
            dff_ref[rows, :] = dff.astype(BF16)
            db = db + jnp.sum(dff, axis=0, keepdims=True)
        db_ref[...] = db

    return pl.pallas_call(
        body, name="forget_bwd", grid=(1,),
        in_specs=[_ZFF_SPEC, pl.BlockSpec((1, 128), lambda i: (0, 0)),
                  pl.BlockSpec((FOX_HEADS, T), lambda i: (0, 0)), ANY],
        out_specs=[_FF_SPEC, pl.BlockSpec((1, 128), lambda i: (0, 0))],
        out_shape=[jax.ShapeDtypeStruct((T, D_IN_PAD), BF16),
                   jax.ShapeDtypeStruct((1, 128), F32)],
        input_output_aliases={3: 0},
        compiler_params=_params(dimension_semantics=("arbitrary",)),
    )(z, b_pad, dc, dz)


FOX_QB = 512
FOX_NQB = SEQ // FOX_QB


def _fox_block(b):
    lo = CHUNK + b * FOX_QB
    return pl.ds(lo, FOX_QB), lo, lo + FOX_QB


def _causal_bias():
    r = lax.broadcasted_iota(jnp.int32, (FOX_QB, FOX_QB), 0)
    c = lax.broadcasted_iota(jnp.int32, (FOX_QB, FOX_QB), 1)
    return jnp.where(c <= r, 0.0, NEG_INF).astype(F32)


def _fox_logits(q_blk, k_all, bias, causal, b):
    _, lo, hi = _fox_block(b)
    here = bias[:, lo:lo + 1]
    s_off = _dot_nt(q_blk, k_all[:lo]) + (bias[:, :lo] - here)
    s_dia = _dot_nt(q_blk, k_all[lo:hi]) + ((bias[:, lo:hi] - here) + causal)
    return s_off, s_dia


_FOX_Z_SPEC = pl.BlockSpec((T, FOX_W), lambda p: (0, FOX_BASE // FOX_W + p))
_FOX_BIAS_SPEC = pl.BlockSpec((2, 1, T), lambda p: (p, 0, 0))
_FOX_LSE_SPEC = pl.BlockSpec((2, T, 1), lambda p: (p, 0, 0))
_FOX_SCALE = FOX_D ** -0.5
_FOX_QSCALE = _FOX_SCALE * LOG2E


def _fox_fwd(z, bias, y, w_out_blk):
    last = FOX_PAIRS - 1

    def body(z_ref, b_ref, y_in, w_ref, a_ref, lse_ref, y_ref, wall_ref,
             send_sems, recv_sems, local_sems):
        start, wait = _direct_exchange([w_ref], [wall_ref], send_sems, recv_sems, local_sems, True)
        pl.when(pl.program_id(0) == 0)(start)

        causal = _causal_bias()
        a_ref[pl.ds(0, CHUNK), :] = jnp.zeros((CHUNK, 128), F32)
        y_ref[pl.ds(0, CHUNK), :] = jnp.zeros((CHUNK, 128), BF16)
        for j in range(2):
            lanes = pl.ds(j * FOX_D, FOX_D)
            k_all = z_ref[:, pl.ds(128 + j * FOX_D, FOX_D)]
            v_all = z_ref[:, pl.ds(256 + j * FOX_D, FOX_D)]
            bias = b_ref[j]
            lse_ref[j, pl.ds(0, CHUNK), :] = jnp.zeros((CHUNK, 1), F32)
            for b in range(FOX_NQB):
                rows, lo, hi = _fox_block(b)
                q_blk = (z_ref[rows, lanes].astype(F32) * _FOX_QSCALE).astype(BF16)
                s_off, s_dia = _fox_logits(q_blk, k_all, bias, causal, b)
                m = jnp.maximum(jnp.max(s_off, axis=-1, keepdims=True),
                                jnp.max(s_dia, axis=-1, keepdims=True))
                e_off = jnp.exp2(s_off - m)
                e_dia = jnp.exp2(s_dia - m)
                total = jnp.sum(e_off, axis=-1, keepdims=True) + jnp.sum(e_dia, axis=-1, keepdims=True)
                o = (_dot(e_off.astype(BF16), v_all[:lo]) + _dot(e_dia.astype(BF16), v_all[lo:hi])) / total
                a_ref[rows, lanes] = o
                lse_ref[j, rows, :] = m + jnp.log(total) * LOG2E
                gate = _silu_parts(z_ref[rows, pl.ds(384 + j * FOX_D, FOX_D)].astype(F32))[0]
                y_ref[rows, lanes] = (o * gate).astype(BF16)

        pl.when(pl.program_id(0) == last)(wait)

    return pl.pallas_call(
        body, name="fox_fwd", grid=(FOX_PAIRS,),
        in_specs=[_FOX_Z_SPEC, _FOX_BIAS_SPEC, ANY, ANY],
        out_specs=[pl.BlockSpec((T, 128), lambda p: (0, p)), _FOX_LSE_SPEC,
                   pl.BlockSpec((T, 128), lambda p: (0, 8 + p)), ANY],
        out_shape=[jax.ShapeDtypeStruct((T, FOX_HEADS * FOX_D), F32),
                   jax.ShapeDtypeStruct((FOX_HEADS, T, 1), F32),
                   jax.ShapeDtypeStruct((T, D_MIX), BF16),
                   _exchange_shape(w_out_blk, True)],
        input_output_aliases={2: 2},
        scratch_shapes=_exchange_sems(1),
        compiler_params=_params(dimension_semantics=("arbitrary",)),
    )(z, bias, y, w_out_blk)


def _fox_bwd(z, bias, a_f, lse, dy, dz, dwo_blocks):
    last = FOX_PAIRS - 1

    def body(z_ref, b_ref, a_ref, lse_ref, dy_ref, dz_in, dwo_ref, dz_ref, dc_ref, got_ref,
             kv_acc, dc_acc, send_sems, recv_sems, local_sems):
        start, wait = _direct_exchange([dwo_ref], [got_ref], send_sems, recv_sems, local_sems, False)
        pl.when(pl.program_id(0) == 0)(start)

        causal = _causal_bias()
        dz_ref[pl.ds(0, CHUNK), pl.ds(0, 128)] = jnp.zeros((CHUNK, 128), BF16)
        dz_ref[pl.ds(0, CHUNK), pl.ds(384, 128)] = jnp.zeros((CHUNK, 128), BF16)
        dk_rows, dv_rows = pl.ds(0, FOX_D), pl.ds(FOX_D, FOX_D)
        for j in range(2):
            lanes = pl.ds(j * FOX_D, FOX_D)
            k_all = z_ref[:, pl.ds(128 + j * FOX_D, FOX_D)]
            v_all = z_ref[:, pl.ds(256 + j * FOX_D, FOX_D)]
            bias = b_ref[j]
            kv_acc[...] = jnp.zeros_like(kv_acc)
            dc_acc[...] = jnp.zeros_like(dc_acc)
            for b in range(FOX_NQB):
                rows, lo, hi = _fox_block(b)
                off, dia = pl.ds(0, lo), pl.ds(lo, FOX_QB)
                q_blk = (z_ref[rows, lanes].astype(F32) * _FOX_QSCALE).astype(BF16)
                s_off, s_dia = _fox_logits(q_blk, k_all, bias, causal, b)
                lse_blk = lse_ref[j, rows, :]
                p_off, p_dia = jnp.exp2(s_off - lse_blk), jnp.exp2(s_dia - lse_blk)
                sg, dsg = _silu_parts(z_ref[rows, pl.ds(384 + j * FOX_D, FOX_D)].astype(F32))
                dyj = dy_ref[rows, lanes].astype(F32)
                dz_ref[rows, pl.ds(384 + j * FOX_D, FOX_D)] = (dyj * a_ref[rows, lanes] * dsg).astype(BF16)
                do_b = (dyj * sg).astype(BF16)
                dp_off = _dot_nt(do_b, v_all[:lo])
                dp_dia = _dot_nt(do_b, v_all[lo:hi])
                d = (jnp.sum(p_off * dp_off, axis=-1, keepdims=True)
                     + jnp.sum(p_dia * dp_dia, axis=-1, keepdims=True))
                ds_off = p_off * (dp_off - d)
                ds_dia = p_dia * (dp_dia - d)
                dc_acc[:, off] -= jnp.sum(ds_off, axis=0, keepdims=True)
                dc_acc[:, dia] -= jnp.sum(ds_dia, axis=0, keepdims=True)
                ds_off_b, ds_dia_b = ds_off.astype(BF16), ds_dia.astype(BF16)
                dq = _dot(ds_off_b, k_all[:lo]) + _dot(ds_dia_b, k_all[lo:hi])
                dz_ref[rows, lanes] = (dq * _FOX_SCALE).astype(BF16)
                kv_acc[dk_rows, off] += _dot_tn(q_blk, ds_off_b)
                kv_acc[dk_rows, dia] += _dot_tn(q_blk, ds_dia_b)
                kv_acc[dv_rows, off] += _dot_tn(do_b, p_off.astype(BF16))
                kv_acc[dv_rows, dia] += _dot_tn(do_b, p_dia.astype(BF16))
            for n in range(NCHUNK):
                rows = pl.ds(n * CHUNK, CHUNK)
                both = kv_acc[:, rows].T
                dz_ref[rows, pl.ds(128 + j * FOX_D, FOX_D)] = (both[:, :FOX_D] * LN2).astype(BF16)
                dz_ref[rows, pl.ds(256 + j * FOX_D, FOX_D)] = both[:, FOX_D:].astype(BF16)
            dc_ref[j] = dc_acc[...]

        pl.when(pl.program_id(0) == last)(wait)

    col = lambda base: pl.BlockSpec((T, 128), lambda p: (0, base + p))
    return pl.pallas_call(
        body, name="fox_bwd", grid=(FOX_PAIRS,),
        in_specs=[_FOX_Z_SPEC, _FOX_BIAS_SPEC, col(0), _FOX_LSE_SPEC, col(8), ANY, ANY],
        out_specs=[_FOX_Z_SPEC, _FOX_BIAS_SPEC, ANY],
        out_shape=[jax.ShapeDtypeStruct((T, D_IN_PAD), BF16),
                   jax.ShapeDtypeStruct((FOX_HEADS, 1, T), F32),
                   _exchange_shape(dwo_blocks, False)],
        input_output_aliases={5: 0},
        scratch_shapes=[pltpu.VMEM((2 * FOX_D, T), F32), pltpu.VMEM((1, T), F32)] + _exchange_sems(1),
        compiler_params=_params(dimension_semantics=("arbitrary",)),
    )(z, bias, a_f, lse, dy, dz, dwo_blocks)


def _rot(x, cosf, sins):
    return x * cosf + pltpu.roll(x, RET_DK // 2, 1) * sins


def _rot_t(d, cosf, sins):
    return d * cosf - pltpu.roll(d, RET_DK // 2, 1) * sins


_RET_Z_SPEC = pl.BlockSpec((T, RET_W), lambda h: (0, h))
_RET_TABLE_SPECS = [
    pl.BlockSpec((T, RET_DK), lambda h: (0, 0)),
    pl.BlockSpec((T, RET_DK), lambda h: (0, 0)),
    pl.BlockSpec((1, CHUNK, CHUNK), lambda h: (h, 0, 0)),
    pl.BlockSpec((1, CHUNK, 1), lambda h: (h, 0, 0)),
    pl.BlockSpec((1, CHUNK, 1), lambda h: (h, 0, 0)),
    pl.BlockSpec((1, 1, 1), lambda h: (h, 0, 0)),
]
_RQ, _RK = pl.ds(0, RET_DK), pl.ds(RET_DK, RET_DK)
_RV, _RG = pl.ds(2 * RET_DK, RET_DV), pl.ds(2 * RET_DK + RET_DV, RET_DV)
_RET_KSCALE = RET_DK ** -0.5


_RET_STATE_SPEC = pl.BlockSpec((1, NCHUNK, RET_DK, RET_DV), lambda h: (h, 0, 0, 0))


def _ret_fwd_head(z_ref, cos_ref, sin_ref, dm_ref, zeta_ref, xi_ref, cd_ref, raw_ref, y_ref, st_ref):
    dmask, zeta, xi, cdec = dm_ref[0], zeta_ref[0], xi_ref[0], cd_ref[0]
    state = jnp.zeros((RET_DK, RET_DV), F32)
    for n in range(NCHUNK):
        rows = pl.ds(n * CHUNK, CHUNK)
        cosf, sins = cos_ref[rows, :], sin_ref[rows, :]
        qr = _rot(z_ref[rows, _RQ].astype(F32), cosf, sins)
        kr_b = (_rot(z_ref[rows, _RK].astype(F32), cosf, sins) * _RET_KSCALE).astype(BF16)
        v_b = z_ref[rows, _RV]
        a = _dot_nt(qr.astype(BF16), kr_b) * dmask
        state_b = state.astype(BF16)
        st_ref[0, n] = state_b
        out = _dot(a.astype(BF16), v_b) + _dot((qr * xi).astype(BF16), state_b)
        state = state * cdec + _dot_tn(kr_b, (v_b.astype(F32) * zeta).astype(BF16))
        raw_ref[rows, :] = out
        r = lax.rsqrt(jnp.mean(out * out, axis=-1, keepdims=True) + EPS)
        y_ref[rows, :] = (out * r * _silu_parts(z_ref[rows, _RG].astype(F32))[0]).astype(BF16)


def _ret_bwd(z, tables, raw, states, dy):
    def body(z_ref, cos_ref, sin_ref, dm_ref, zeta_ref, xi_ref, cd_ref, raw_ref, st_all, dy_ref, dz_ref):
        dmask, zeta, xi, cdec = dm_ref[0], zeta_ref[0], xi_ref[0], cd_ref[0]
        st_ref = st_all.at[0]

        grad_state = jnp.zeros((RET_DK, RET_DV), F32)
        for n in reversed(range(NCHUNK)):
            rows = pl.ds(n * CHUNK, CHUNK)
            cosf, sins = cos_ref[rows, :], sin_ref[rows, :]
            qr = _rot(z_ref[rows, _RQ].astype(F32), cosf, sins)
            kr_b = (_rot(z_ref[rows, _RK].astype(F32), cosf, sins) * _RET_KSCALE).astype(BF16)
            qr_b = qr.astype(BF16)
            v_b = z_ref[rows, _RV]
            gs_b = grad_state.astype(BF16)
            o = raw_ref[rows, :]
            r = lax.rsqrt(jnp.mean(o * o, axis=-1, keepdims=True) + EPS)
            hn = o * r
            sg, dsg = _silu_parts(z_ref[rows, _RG].astype(F32))
            dyn = dy_ref[rows, :].astype(F32)
            dz_ref[rows, _RG] = (dyn * hn * dsg).astype(BF16)
            dhn = dyn * sg
            do_b = (r * (dhn - hn * jnp.mean(dhn * hn, axis=-1, keepdims=True))).astype(BF16)
            a_b = (_dot_nt(qr_b, kr_b) * dmask).astype(BF16)
            da_b = (_dot_nt(do_b, v_b) * dmask).astype(BF16)
            dqr = _dot(da_b, kr_b) + xi * _dot_nt(do_b, st_ref[n])
            dkr = _dot_tn(da_b, qr_b) + zeta * _dot_nt(v_b, gs_b)
            dv = _dot_tn(a_b, do_b) + zeta * _dot(kr_b, gs_b)
            grad_state = grad_state * cdec + _dot_tn((qr * xi).astype(BF16), do_b)
            dz_ref[rows, _RQ] = _rot_t(dqr, cosf, sins).astype(BF16)
            dz_ref[rows, _RK] = (_rot_t(dkr, cosf, sins) * _RET_KSCALE).astype(BF16)
            dz_ref[rows, _RV] = dv.astype(BF16)

    wide = pl.BlockSpec((T, RET_DV), lambda h: (0, h))
    return pl.pallas_call(
        body, name="ret_bwd", grid=(RET_HEADS,),
        in_specs=[_RET_Z_SPEC] + _RET_TABLE_SPECS + [wide, _RET_STATE_SPEC, wide],
        out_specs=_RET_Z_SPEC,
        out_shape=jax.ShapeDtypeStruct((T, D_IN_PAD), BF16),
        compiler_params=_params(dimension_semantics=("arbitrary",)),
    )(z, *tables, raw, states, dy)


def _adamw(w, g, m, v):
    m = ADAM_B1 * m + (1.0 - ADAM_B1) * g
    v = ADAM_B2 * v + (1.0 - ADAM_B2) * (g * g)
    m_hat = m / (1.0 - ADAM_B1 ** ADAM_STEP)
    v_hat = v / (1.0 - ADAM_B2 ** ADAM_STEP)
    delta = -ADAM_LR * (m_hat / (jnp.sqrt(v_hat) + ADAM_EPS) + ADAM_WD * w)
    return delta, m, v


def _sum_adamw(parts, w, m, v, rows, name):
    _, r_tot, cols = parts.shape
    assert r_tot % rows == 0

    def body(p_ref, w_ref, m_ref, v_ref, g_ref, d_ref, nm_ref, nv_ref):
        g = p_ref[0].astype(F32)
        for d in range(1, N_DEV):
            g = g + p_ref[d].astype(F32)
        delta, nm, nv = _adamw(w_ref[...], g, m_ref[...], v_ref[...])
        g_ref[...] = g
        d_ref[...] = delta
        nm_ref[...] = nm
        nv_ref[...] = nv

    blk = pl.BlockSpec((rows, cols), lambda i: (i, 0))
    return pl.pallas_call(
        body, name=name, grid=(r_tot // rows,),
        in_specs=[pl.BlockSpec((N_DEV, rows, cols), lambda i: (0, i, 0)), blk, blk, blk],
        out_specs=[blk] * 4,
        out_shape=[jax.ShapeDtypeStruct((r_tot, cols), F32)] * 4,
        compiler_params=_params(dimension_semantics=("arbitrary",)),
    )(parts, w, m, v)


def _sum_adamw_w_in(parts, w, m, v, small):
    n_part, r, c = parts.shape
    steps = c // 128

    def body(p_ref, w_hbm, m_hbm, v_hbm, s_ref, g_hbm, d_hbm, nm_hbm, nv_hbm, got_ref,
             in_buf, out_buf, in_sems, out_sems, send_sems, recv_sems, local_sems):
        start, wait = _direct_exchange([s_ref], [got_ref], send_sems, recv_sems, local_sems, True)
        i = pl.program_id(0)
        pl.when(i == 0)(start)
        slot = i % 2

        def loads(step, into):
            cols = pl.ds(pl.multiple_of(step * 128, 128), 128)
            return [pltpu.make_async_copy(h.at[:, 0, cols], in_buf.at[into, k], in_sems.at[into, k])
                    for k, h in enumerate((w_hbm, m_hbm, v_hbm))]

        cols = pl.ds(pl.multiple_of(i * 128, 128), 128)
        stores = [pltpu.make_async_copy(out_buf.at[k], h.at[:, 0, cols], out_sems.at[k])
                  for k, h in enumerate((g_hbm, d_hbm, nm_hbm, nv_hbm))]

        @pl.when(i == 0)
        def _():
            for cp in loads(0, 0):
                cp.start()

        @pl.when(i + 1 < steps)
        def _():
            for cp in loads(i + 1, 1 - slot):
                cp.start()

        g = p_ref[0].astype(F32)
        for d in range(1, n_part):
            g = g + p_ref[d].astype(F32)
        for cp in loads(i, slot):
            cp.wait()
        delta, nm, nv = _adamw(in_buf[slot, 0], g, in_buf[slot, 1], in_buf[slot, 2])

        @pl.when(i > 0)
        def _():
            for cp in stores:
                cp.wait()

        for k, val in enumerate((g, delta, nm, nv)):
            out_buf[k] = val
        for cp in stores:
            cp.start()

        @pl.when(i == steps - 1)
        def _():
            for cp in stores:
                cp.wait()
            wait()

    return pl.pallas_call(
        body, name="adamw_w_in", grid=(steps,),
        in_specs=[pl.BlockSpec((n_part, r, 128), lambda i: (0, 0, i)), ANY, ANY, ANY, ANY],
        out_specs=[ANY] * 5,
        out_shape=[jax.ShapeDtypeStruct((r, 1, c), F32)] * 4 + [_exchange_shape(small, True)],
        scratch_shapes=[pltpu.VMEM((2, 3, r, 128), F32), pltpu.VMEM((4, r, 128), F32),
                        pltpu.SemaphoreType.DMA((2, 3)), pltpu.SemaphoreType.DMA((4,))] + _exchange_sems(1),
        compiler_params=_params(dimension_semantics=("arbitrary",)),
    )(parts, w, m, v, small)


def _adamw_small(got, me, metas, norms, finals, biases):
    def body(me_ref, gm_ref, gr_ref, *refs):
        ins, outs = refs[:12], refs[12:]
        g_meta, g_rest = gm_ref[0], gr_ref[0]
        for d in range(1, N_DEV):
            g_meta, g_rest = g_meta + gm_ref[d], g_rest + gr_ref[d]
        grads = [g_meta, g_rest[0:1], g_rest[1:2], g_rest[2:3, :FOX_HEADS]]
        for k, g in enumerate(grads):
            w_ref, m_ref, v_ref = ins[3 * k:3 * k + 3]
            delta, new_m, new_v = _adamw(w_ref[...], g, m_ref[...], v_ref[...])
            for o_ref, val in zip(outs[4 * k:4 * k + 4], (g, delta, new_m, new_v)):
                o_ref[...] = val
        outs[16][...] = g_rest[3:4, :128]

    groups = (metas, norms, finals, biases)
    full = lambda a: pl.BlockSpec(a.shape, lambda i, me_ref: (0,) * a.ndim)
    flat = [a for grp in groups for a in grp]
    res = pl.pallas_call(
        body, name="adamw_small",
        grid_spec=pltpu.PrefetchScalarGridSpec(
            num_scalar_prefetch=1, grid=(1,),
            in_specs=[pl.BlockSpec((N_DEV, N_META, META_BLK), lambda i, me_ref: (0, 0, me_ref[0])),
                      pl.BlockSpec((N_DEV, 8, D_MODEL), lambda i, me_ref: (0, N_META // 8, 0))]
            + [full(a) for a in flat],
            out_specs=[full(grp[0]) for grp in groups for _ in range(4)]
            + [pl.BlockSpec((1, 128), lambda i, me_ref: (0, 0))]),
        out_shape=[jax.ShapeDtypeStruct(grp[0].shape, F32) for grp in groups for _ in range(4)]
        + [jax.ShapeDtypeStruct((1, 128), F32)],
        compiler_params=_params(dimension_semantics=("arbitrary",)),
    )(me, got, got, *flat)
    return [res[4 * k:4 * k + 4] for k in range(4)], res[16]


def kernel(x, meta_tokens, norm_g, w_in, b_f, w_out, final_g, loss_target, m_meta_tokens, m_norm_g, m_w_in, m_b_f, m_w_out, m_final_g, v_meta_tokens, v_norm_g, v_w_in, v_b_f, v_w_out, v_final_g):
    core = lax.axis_index("c")
    me = 4 * lax.axis_index("x") + 2 * lax.axis_index("y") + core
    tables = _tables()

    wt_all, meta_all = _gather_two_level([_slab(w_in[0].T.astype(BF16), me), meta_tokens], name="gather_w_in")
    slabs, tail = _join_edges(wt_all)
    meta_full = jnp.transpose(meta_all, (1, 0, 2)).reshape(N_META, D_MODEL)
    b_pad = jnp.pad(b_f, ((0, 0), (0, 128 - FOX_HEADS)))

    h_pad, u, z, raw, y, states = _rms_z_ret(x[0], meta_full, norm_g, slabs, tables)
    z, zff = _z_fox(u, slabs, tail, z)
    bias = _forget_fwd(zff, b_pad)[:FOX_HEADS].reshape(FOX_HEADS, 1, T)
    a_f, lse, y, w_out_all = _fox_fwd(z, bias, y, w_out[0].astype(BF16))
    w_out_b = w_out_all.reshape(D_MIX, D_MODEL)
    dout, dout_b, dy, loss_blk, d_final_g = _out_loss_dy(y, w_out_b, h_pad, loss_target[0],
                                                         final_g.reshape(1, D_MODEL))

    d_w_out = _mm_tn(y, dout_b, tm=D_MIX, tn=256, name="mm_dwout")
    dz = _ret_bwd(z, tables, raw, states, dy)
    dz, dc, got_w_out = _fox_bwd(z, bias, a_f, lse, dy, dz, d_w_out.reshape(N_DEV, WO_BLK, D_MODEL))
    dz, db_f = _forget_bwd(zff, b_pad, dc.reshape(FOX_HEADS, T), dz)

    kept, via = _dwin_pair_slabs(dz, u, core)
    dh, d_norm_g, direct, summed = _du_rms(dz, slabs, tail, h_pad, dout, norm_g, kept, via)
    got_w_in = _unslab([kept[2], direct, summed], me)

    small = jnp.concatenate([
        dh[PAD:CHUNK], d_norm_g, d_final_g, jnp.pad(db_f[:, :FOX_HEADS], ((0, 0), (0, D_MODEL - FOX_HEADS))),
        jnp.pad(loss_blk[0:1], ((0, 0), (0, D_MODEL - 128))),
        jnp.zeros((SMALL_ROWS - N_META - 4, D_MODEL), F32)], axis=0)
    fore = lambda a: jnp.transpose(a, (2, 0, 1))
    g_w_in, d_w_in, nm_w_in, nv_w_in, got_small = _sum_adamw_w_in(
        got_w_in, fore(w_in), fore(m_w_in), fore(v_w_in), small)
    g_w_out, d_w_out, nm_w_out, nv_w_out = _sum_adamw(got_w_out, w_out[0], m_w_out[0], v_w_out[0], 128, "adamw_w_out")

    row = lambda a: a.reshape(1, D_MODEL)
    (meta_o, norm_o, final_o, bias_o), loss_row = _adamw_small(
        got_small, me.astype(jnp.int32).reshape(1),
        (meta_tokens, m_meta_tokens, v_meta_tokens), (norm_g, m_norm_g, v_norm_g),
        (row(final_g), row(m_final_g), row(v_final_g)), (b_f, m_b_f, v_b_f))
    final_o = [a.reshape(D_MODEL) for a in final_o]

    back = lambda a: jnp.transpose(a, (1, 2, 0))
    outs = [[meta_o[k], norm_o[k], back(wk), bias_o[k], ok[None], final_o[k]]
            for k, (wk, ok) in enumerate(zip((g_w_in, d_w_in, nm_w_in, nv_w_in),
                                             (g_w_out, d_w_out, nm_w_out, nv_w_out)))]
    return (loss_row[0, 0], dh[CHUNK:][None], *outs[0], *outs[1], *outs[2], *outs[3])
```

```python
import numpy as np
import jax
import jax.numpy as jnp
from jax import lax
from jax.experimental import pallas as pl
from jax.experimental.pallas import tpu as pltpu

F32 = jnp.float32
BF16 = jnp.bfloat16

N_DEV = 8
N_CHIP = 4
D_MODEL = 1024
SEQ = 2048
N_META = 16
CHUNK = 128
PAD = CHUNK - N_META
T = SEQ + CHUNK
NCHUNK = T // CHUNK
D_MIX = 2048
RET_HEADS = 4
RET_DK = 128
RET_DV = 256
RET_W = 2 * RET_DK + 2 * RET_DV
FOX_HEADS = 16
FOX_D = 64
FOX_PAIRS = FOX_HEADS // 2
FOX_W = 4 * 128
FOX_BASE = RET_HEADS * RET_W
FF_BASE = FOX_BASE + FOX_PAIRS * FOX_W
D_IN = 7184
D_IN_PAD = 7296
W_BLK = D_IN // N_DEV
WO_BLK = D_MIX // N_DEV
META_BLK = D_MODEL // N_DEV
EPS = 1e-6
NEG_INF = -1e30
ROPE_BASE = 10000.0
LOG2E = 1.4426950408889634
LN2 = 0.6931471805599453

ADAM_LR = 0.001
ADAM_B1 = 0.9
ADAM_B2 = 0.999
ADAM_EPS = 1e-08
ADAM_WD = 0.01
ADAM_STEP = 10

SMALL_ROWS = 24
VMEM_LIMIT = 56 * 1024 * 1024
MESH = pl.DeviceIdType.MESH
ANY = pl.BlockSpec(memory_space=pl.ANY)

_NT = (((1,), (1,)), ((), ()))
_TN = (((0,), (0,)), ((), ()))


def _dot(a, b):
    return jnp.dot(a, b, preferred_element_type=F32)


def _dot_nt(a, b):
    return lax.dot_general(a, b, _NT, preferred_element_type=F32)


def _dot_tn(a, b):
    return lax.dot_general(a, b, _TN, preferred_element_type=F32)


def _params(**kw):
    return pltpu.CompilerParams(vmem_limit_bytes=VMEM_LIMIT, **kw)


def _silu_parts(g):
    sig = jax.nn.sigmoid(g)
    return g * sig, sig * (1.0 + g * (1.0 - sig))


def _tables():
    pos = np.arange(T, dtype=np.float32) - PAD
    inv = (ROPE_BASE ** (-np.arange(0, RET_DK, 2, dtype=np.float32) / RET_DK)).astype(np.float32)
    ang = pos[:, None] * inv[None, :]
    cos, sin = np.cos(ang), np.sin(ang)
    cosf = np.concatenate([cos, cos], axis=1).astype(np.float32)
    sins = np.concatenate([-sin, sin], axis=1).astype(np.float32)
    h = np.arange(RET_HEADS, dtype=np.float32)
    log_gamma = np.log1p(-np.exp2(-5.0 - h)).astype(np.float32)
    idx = np.arange(CHUNK, dtype=np.float32)
    diff = idx[:, None] - idx[None, :]
    dmask = np.where(diff[None] >= 0,
                     np.exp(log_gamma[:, None, None] * np.maximum(diff, 0.0)[None]), 0.0)
    zeta = np.exp(log_gamma[:, None] * (CHUNK - 1.0 - idx)[None, :])
    xi = np.exp(log_gamma[:, None] * (idx + 1.0)[None, :])
    cdec = np.exp(log_gamma * CHUNK)
    return (jnp.asarray(cosf), jnp.asarray(sins), jnp.asarray(dmask, F32),
            jnp.asarray(zeta[:, :, None], F32), jnp.asarray(xi[:, :, None], F32),
            jnp.asarray(cdec[:, None, None], F32))


W_STRIDE = 896
W_SLAB = 912
W_EDGE = W_SLAB - W_STRIDE
def _slab(block, me):
    shift = W_BLK - W_STRIDE
    return lax.switch(me, [lambda b, d=d: jnp.pad(b, ((shift * d, W_SLAB - W_BLK - shift * d), (0, 0)))
                           for d in range(N_DEV)], block)


def _unslab(slabs, me):
    shift = W_BLK - W_STRIDE
    return lax.switch(me, [lambda *s, d=d: jnp.stack([a[shift * d:shift * d + W_BLK] for a in s])
                           for d in range(N_DEV)], *slabs)


def _join_edges(slabs):
    last = slabs[:, W_STRIDE:]
    first = slabs[:, :W_EDGE] + jnp.concatenate([jnp.zeros_like(last[:1]), last[:-1]], axis=0)
    tail = jnp.pad(last[N_DEV - 1], ((0, 128 - W_EDGE), (0, 0)))
    return lax.dynamic_update_slice(slabs, first, (0, 0, 0)), tail


def _mm_tn(a, b, *, tm, tn, name):
    k, m = a.shape
    n = b.shape[1]
    assert m % tm == 0 and n % tn == 0

    def body(a_ref, b_ref, o_ref):
        o_ref[...] = _dot_tn(a_ref[...], b_ref[...]).astype(BF16)

    return pl.pallas_call(
        body, name=name, grid=(n // tn, m // tm),
        in_specs=[pl.BlockSpec((k, tm), lambda j, i: (0, i)),
                  pl.BlockSpec((k, tn), lambda j, i: (0, j))],
        out_specs=pl.BlockSpec((tm, tn), lambda j, i: (i, j)),
        out_shape=jax.ShapeDtypeStruct((m, n), BF16),
        compiler_params=_params(dimension_semantics=("arbitrary", "arbitrary")),
    )(a, b)


def _piece_spec(base, mult):
    def index(i):
        p = base + mult * i
        return p // 7, p % 7, 0
    return pl.BlockSpec((1, 128, D_MODEL), index)


_RET_PIECES = ((0, 1), (4, 1), (8, 2), (9, 2), (16, 2), (17, 2))
_FOX_PIECES = ((24, 1), (32, 1), (40, 1), (48, 1))


def _rms_z_ret(x, meta, g, slabs, tables):
    def body(x_hbm, m_ref, g_ref, *refs):
        pieces, tabs = refs[:6], refs[6:12]
        h_ref, u_ref, z_ref, raw_ref, y_ref, st_ref, x_sem = refs[12:]

        @pl.when(pl.program_id(0) == 0)
        def _():
            tokens = pltpu.make_async_copy(x_hbm, h_ref.at[pl.ds(CHUNK, SEQ)], x_sem)
            tokens.start()
            h_ref[pl.ds(0, PAD), :] = jnp.zeros((PAD, D_MODEL), F32)
            h_ref[pl.ds(PAD, N_META), :] = m_ref[...]
            tokens.wait()
            h = h_ref[...]
            r = lax.rsqrt(jnp.mean(h * h, axis=-1, keepdims=True) + EPS)
            u_ref[...] = (h * r * g_ref[...]).astype(BF16)

        w = jnp.concatenate([p[0] for p in pieces], axis=0)
        z_ref[...] = _dot_nt(u_ref[...], w).astype(BF16)
        _ret_fwd_head(z_ref, *tabs, raw_ref, y_ref, st_ref)

    whole = pl.BlockSpec((T, D_MODEL), lambda i: (0, 0))
    wide = pl.BlockSpec((T, RET_DV), lambda i: (0, i))
    return pl.pallas_call(
        body, name="rms_z_ret", grid=(RET_HEADS,),
        in_specs=[ANY, pl.BlockSpec((N_META, D_MODEL), lambda i: (0, 0)),
                  pl.BlockSpec((1, D_MODEL), lambda i: (0, 0))]
        + [_piece_spec(*bm) for bm in _RET_PIECES] + _RET_TABLE_SPECS,
        out_specs=[whole, whole, pl.BlockSpec((T, RET_W), lambda i: (0, i)), wide, wide, _RET_STATE_SPEC],
        out_shape=[jax.ShapeDtypeStruct((T, D_MODEL), F32),
                   jax.ShapeDtypeStruct((T, D_MODEL), BF16),
                   jax.ShapeDtypeStruct((T, D_IN_PAD), BF16),
                   jax.ShapeDtypeStruct((T, RET_HEADS * RET_DV), F32),
                   jax.ShapeDtypeStruct((T, D_MIX), BF16),
                   jax.ShapeDtypeStruct((RET_HEADS, NCHUNK, RET_DK, RET_DV), BF16)],
        scratch_shapes=[pltpu.SemaphoreType.DMA],
        compiler_params=_params(dimension_semantics=("arbitrary",)),
    )(x, meta, g, *([slabs] * 6), *tables)


def _z_fox(u, slabs, tail, z):
    def body(u_ref, *refs):
        pieces, t_ref, z_ref, zff_ref = refs[:4], refs[4], refs[6], refs[7]
        u = u_ref[...]
        w = jnp.concatenate([p[0] for p in pieces], axis=0)
        z_ref[...] = _dot_nt(u, w).astype(BF16)

        @pl.when(pl.program_id(0) == 0)
        def _():
            zff_ref[...] = _dot_nt(u, t_ref[...])

    return pl.pallas_call(
        body, name="z_fox", grid=(FOX_PAIRS,),
        in_specs=[pl.BlockSpec((T, D_MODEL), lambda i: (0, 0))] + [_piece_spec(*bm) for bm in _FOX_PIECES]
        + [pl.BlockSpec((128, D_MODEL), lambda i: (0, 0)), ANY],
        out_specs=[_FOX_Z_SPEC, pl.BlockSpec((T, 128), lambda i: (0, 0))],
        out_shape=[jax.ShapeDtypeStruct((T, D_IN_PAD), BF16),
                   jax.ShapeDtypeStruct((T, 128), F32)],
        input_output_aliases={6: 0},
        compiler_params=_params(dimension_semantics=("arbitrary",)),
    )(u, *([slabs] * 4), tail, z)


def _out_loss_dy(y, w_out_b, h_pad, target, g):
    tm = T // 4

    def body(y_ref, w_ref, h_ref, t_hbm, g_ref, d_ref, db_ref, dy_ref, loss_ref, dg_ref, t_buf, t_sem):
        i = pl.program_id(0)
        head = pltpu.make_async_copy(t_hbm.at[pl.ds(0, tm - CHUNK)], t_buf.at[pl.ds(CHUNK, tm - CHUNK)], t_sem)
        rest = pltpu.make_async_copy(t_hbm.at[pl.ds(pl.multiple_of(jnp.maximum(i, 1) * tm - CHUNK, 8), tm)],
                                     t_buf, t_sem)

        @pl.when(i == 0)
        def _():
            t_buf[pl.ds(0, CHUNK), :] = jnp.zeros((CHUNK, D_MODEL), F32)
            head.start()
            loss_ref[...] = jnp.zeros_like(loss_ref)
            dg_ref[...] = jnp.zeros_like(dg_ref)

        pl.when(i > 0)(rest.start)

        w = w_ref[...]
        o = _dot(y_ref[...], w) + h_ref[...]
        pl.when(i == 0)(head.wait)
        pl.when(i > 0)(rest.wait)
        token = lax.broadcasted_iota(jnp.int32, (tm, 1), 0) + i * tm >= CHUNK
        g = g_ref[...]
        r = lax.rsqrt(jnp.mean(o * o, axis=-1, keepdims=True) + EPS)
        xn = o * r
        e = jnp.where(token, xn * g - t_buf[...], 0.0)
        loss_ref[...] += jnp.full(loss_ref.shape, 0.5 / D_MODEL * jnp.sum(e * e), F32)
        do = e * (1.0 / D_MODEL)
        dg_ref[...] += jnp.sum(do * xn, axis=0, keepdims=True)
        dn = do * g
        d = r * (dn - xn * jnp.mean(dn * xn, axis=-1, keepdims=True))
        d_b = d.astype(BF16)
        d_ref[...] = d
        db_ref[...] = d_b
        dy_ref[...] = _dot_nt(d_b, w).astype(BF16)

    tile = pl.BlockSpec((tm, D_MODEL), lambda i: (i, 0))
    wide = pl.BlockSpec((tm, D_MIX), lambda i: (i, 0))
    return pl.pallas_call(
        body, name="out_loss_dy", grid=(T // tm,),
        in_specs=[wide, pl.BlockSpec((D_MIX, D_MODEL), lambda i: (0, 0)), tile, ANY,
                  pl.BlockSpec((1, D_MODEL), lambda i: (0, 0))],
        out_specs=[tile, tile, wide,
                   pl.BlockSpec((8, 128), lambda i: (0, 0)),
                   pl.BlockSpec((1, D_MODEL), lambda i: (0, 0))],
        out_shape=[jax.ShapeDtypeStruct((T, D_MODEL), F32),
                   jax.ShapeDtypeStruct((T, D_MODEL), BF16),
                   jax.ShapeDtypeStruct((T, D_MIX), BF16),
                   jax.ShapeDtypeStruct((8, 128), F32),
                   jax.ShapeDtypeStruct((1, D_MODEL), F32)],
        scratch_shapes=[pltpu.VMEM((tm, D_MODEL), F32), pltpu.SemaphoreType.DMA],
        compiler_params=_params(dimension_semantics=("arbitrary",)),
    )(y, w_out_b, h_pad, target, g)


def _coords():
    return lax.axis_index("x"), lax.axis_index("y"), lax.axis_index("c")


def _flip(v, bit):
    return 1 - v if bit else v


def _peer(x, y, c, r):
    return _flip(x, (r >> 2) & 1), _flip(y, (r >> 1) & 1), _flip(c, r & 1)


def _direct_exchange(ins, outs, send_sems, recv_sems, local_sems, gather):
    x, y, c = _coords()
    me = 4 * x + 2 * y + c

    def src(k, to_idx):
        return ins[k] if gather else ins[k].at[to_idx]

    local = [pltpu.make_async_copy(src(k, me), outs[k].at[me], local_sems.at[k])
             for k in range(len(ins))]
    sends, recvs = [], []
    for r in range(1, N_DEV):
        px, py, pc = _peer(x, y, c, r)
        peer = 4 * px + 2 * py + pc
        for k in range(len(ins)):
            sems = dict(send_sem=send_sems.at[k, r - 1], recv_sem=recv_sems.at[k, r - 1],
                        device_id=(px, py, pc), device_id_type=MESH)
            sends.append(pltpu.make_async_remote_copy(src_ref=src(k, peer), dst_ref=outs[k].at[me], **sems))
            recvs.append(pltpu.make_async_remote_copy(src_ref=src(k, peer), dst_ref=outs[k].at[peer], **sems))

    def start():
        for cp in local + sends:
            cp.start()

    def wait():
        for cp in recvs:
            cp.wait_recv()
        for cp in sends:
            cp.wait_send()
        for cp in local:
            cp.wait()

    return start, wait


def _exchange_sems(n_arr):
    return [pltpu.SemaphoreType.DMA((n_arr, N_DEV - 1)), pltpu.SemaphoreType.DMA((n_arr, N_DEV - 1)),
            pltpu.SemaphoreType.DMA((n_arr,))]


def _exchange_shape(a, gather):
    return jax.ShapeDtypeStruct(((N_DEV,) + a.shape) if gather else a.shape, a.dtype)


def _gather_two_level(arrays, name):
    n_arr = len(arrays)

    def body(*refs):
        ins, outs = refs[:n_arr], refs[n_arr:2 * n_arr]
        send_sems, recv_sems, local_sems = refs[2 * n_arr:]
        x, y, c = _coords()

        def slot(k, px, py, pc):
            return outs[k].at[4 * px + 2 * py + pc]

        def routed(core):
            me, sibling = (x, y, core), (x, y, 1 - core)
            xn, yn, dg = (1 - x, y), (x, 1 - y), (1 - x, 1 - y)
            (first, s_first), (second, s_second) = ((xn, 1), (yn, 2)) if core == 0 else ((yn, 2), (xn, 1))

            def copy(k, j, block, to, own=False):
                return pltpu.make_async_remote_copy(
                    src_ref=ins[k] if own else slot(k, *block), dst_ref=slot(k, *block),
                    send_sem=send_sems.at[k, j], recv_sem=recv_sems.at[k, j],
                    device_id=to, device_id_type=MESH)

            local = [pltpu.make_async_copy(ins[k], slot(k, *me), local_sems.at[k]) for k in range(n_arr)]
            sent = []
            for k in range(n_arr):
                sent += [copy(k, 0, me, sibling, True), copy(k, 1, me, (*xn, core), True),
                         copy(k, 2, me, (*yn, core), True)]
            for cp in local + sent:
                cp.start()

            def pass_on(k, j_from, j_to, block, targets):
                copy(k, j_from, block, me).wait_recv()
                for j, to in zip(j_to, targets):
                    cp = copy(k, j, block, to)
                    cp.start()
                    sent.append(cp)

            for k in range(n_arr):
                pass_on(k, s_first, (3, 3 + s_first), (*first, core), ((*second, core), sibling))
            for k in range(n_arr):
                pass_on(k, s_second, (3 + s_second,), (*second, core), (sibling,))
            for k in range(n_arr):
                pass_on(k, 3, (6,), (*dg, core), (sibling,))
            for k in range(n_arr):
                copy(k, 0, sibling, me).wait_recv()
                for j, chip in ((4, xn), (5, yn), (6, dg)):
                    copy(k, j, (*chip, 1 - core), me).wait_recv()
            for cp in sent:
                cp.wait_send()
            for cp in local:
                cp.wait()

        for core in (0, 1):
            pl.when(c == core)(lambda core=core: routed(core))

    return pl.pallas_call(
        body, name=name,
        in_specs=[ANY] * n_arr, out_specs=[ANY] * n_arr,
        out_shape=[_exchange_shape(a, True) for a in arrays],
        scratch_shapes=_exchange_sems(n_arr),
    )(*arrays)


def _piece_columns():
    pos = {}
    for h in range(RET_HEADS):
        for k, p in enumerate((h, 4 + h, 8 + 2 * h, 9 + 2 * h, 16 + 2 * h, 17 + 2 * h)):
            pos[p] = 6 * h + k
    for p in range(FOX_PAIRS):
        for i in range(4):
            pos[24 + 8 * i + p] = 24 + 4 * p + i
    pos[D_IN_PAD // 128 - 1] = D_IN_PAD // 128 - 1
    return np.array([pos[7 * d + j] for d in range(N_DEV) for j in range(8)], np.int32)


def _routes(core):
    x, y, _ = _coords()
    return ((1 - x, y), (x, 1 - y)) if core == 0 else ((x, 1 - y), (1 - x, y))


def _dwin_pair_slabs(dz, u, core):
    def body(order_ref, cols_ref, *refs):
        pieces, u_ref = refs[:8], refs[8]
        kept_ref, theirs_ref, via_ref = refs[9:12]
        send_buf, keep_buf, got_buf, push_send, push_recv, hop_send, hop_recv, load_sem = refs[12:]
        s = pl.program_id(0)
        x, y, c = _coords()
        cols = jnp.concatenate([p[...] for p in pieces[:7]] + [pieces[7][:, :W_EDGE]], axis=1)
        slab = _dot_tn(cols, u_ref[...])

        def push(k):
            return pltpu.make_async_remote_copy(
                src_ref=send_buf.at[k], dst_ref=theirs_ref.at[k],
                send_sem=push_send.at[k], recv_sem=push_recv.at[k],
                device_id=(x, y, 1 - c), device_id_type=MESH)

        def hop(core):
            first, _ = _routes(core)
            return pltpu.make_async_remote_copy(
                src_ref=keep_buf, dst_ref=via_ref, send_sem=hop_send, recv_sem=hop_recv,
                device_id=(*first, core), device_id_type=MESH)

        for k in range(N_CHIP):
            @pl.when(s == 2 * k)
            def _(k=k):
                send_buf[k] = slab.astype(BF16)
                push(k).start()

            @pl.when(s == 2 * k + 1)
            def _(k=k):
                push(k).wait_recv()
                load = pltpu.make_async_copy(theirs_ref.at[k], got_buf, load_sem)
                load.start()
                load.wait()
                total = (slab + got_buf[...].astype(F32)).astype(BF16)
                if k == 0:
                    keep_buf[...] = total
                    for core in (0, 1):
                        pl.when(c == core)(lambda core=core: hop(core).start())
                else:
                    kept_ref[0] = total

        @pl.when(s == 2 * N_CHIP - 1)
        def _():
            for core in (0, 1):
                @pl.when(c == core)
                def _(core=core):
                    hop(core).wait_recv()
                    hop(core).wait_send()
            for k in range(N_CHIP):
                push(k).wait_send()

    x, y = lax.axis_index("x"), lax.axis_index("y")
    xn, yn, dg, own = 2 * (1 - x) + y, 2 * x + 1 - y, 2 * (1 - x) + 1 - y, 2 * x + y
    mine = jnp.where(core == 0, jnp.stack([dg, xn, yn, own]), jnp.stack([dg, yn, xn, own]))
    sibs = jnp.where(core == 0, jnp.stack([dg, yn, xn, own]), jnp.stack([dg, xn, yn, own]))
    order = jnp.stack([2 * sibs + (1 - core), 2 * mine + core], axis=1).reshape(2 * N_CHIP).astype(jnp.int32)
    piece = lambda j: pl.BlockSpec((T, 128), lambda s, order_ref, cols_ref: (0, cols_ref[order_ref[s] * 8 + j]))
    slab = jax.ShapeDtypeStruct((W_SLAB, D_MODEL), BF16)
    kept, _, via = pl.pallas_call(
        body, name="dwin_pair_slabs",
        grid_spec=pltpu.PrefetchScalarGridSpec(
            num_scalar_prefetch=2, grid=(2 * N_CHIP,),
            in_specs=[piece(j) for j in range(8)]
            + [pl.BlockSpec((T, D_MODEL), lambda s, order_ref, cols_ref: (0, 0))],
            out_specs=[pl.BlockSpec((1, W_SLAB, D_MODEL),
                                    lambda s, order_ref, cols_ref: (jnp.maximum(s - 2, 0) // 2, 0, 0)),
                       ANY, ANY],
            scratch_shapes=[pltpu.VMEM((N_CHIP, W_SLAB, D_MODEL), BF16), pltpu.VMEM((W_SLAB, D_MODEL), BF16),
                            pltpu.VMEM((W_SLAB, D_MODEL), BF16),
                            pltpu.SemaphoreType.DMA((N_CHIP,)), pltpu.SemaphoreType.DMA((N_CHIP,)),
                            pltpu.SemaphoreType.DMA, pltpu.SemaphoreType.DMA,
                            pltpu.SemaphoreType.DMA]),
        out_shape=[jax.ShapeDtypeStruct((3, W_SLAB, D_MODEL), BF16),
                   jax.ShapeDtypeStruct((N_CHIP, W_SLAB, D_MODEL), BF16), slab],
        compiler_params=_params(dimension_semantics=("arbitrary",)),
    )(order, jnp.asarray(_piece_columns()), *([dz] * 8), u)
    return kept, via


def _du_rms(dz, slabs, tail, h_pad, dout, g, kept, via):
    tm = 272
    steps = T // tm
    columns = _piece_columns().reshape(N_DEV, 8)

    def body(dz_ref, w_hbm, t_ref, h_ref, d_ref, g_ref, kept_ref, via_ref, dh_ref, dg_ref,
             direct_ref, got_ref, w_ref, via_buf, mine_buf, w_sem, send_sems, recv_sems, local_sems):
        i = pl.program_id(0)
        x, y, c = _coords()
        w_load = pltpu.make_async_copy(w_hbm, w_ref, w_sem)
        pl.when(i == 0)(w_load.start)

        def sends(core):
            first, second = _routes(core)
            return [pltpu.make_async_remote_copy(
                        src_ref=kept_ref.at[0], dst_ref=direct_ref, send_sem=send_sems.at[0],
                        recv_sem=recv_sems.at[0], device_id=(*first, core), device_id_type=MESH),
                    pltpu.make_async_remote_copy(
                        src_ref=mine_buf, dst_ref=got_ref, send_sem=send_sems.at[1],
                        recv_sem=recv_sems.at[1], device_id=(*second, core), device_id_type=MESH)]

        @pl.when(i == 0)
        def _():
            loads = [pltpu.make_async_copy(kept_ref.at[1], mine_buf, local_sems.at[0]),
                     pltpu.make_async_copy(via_ref, via_buf, local_sems.at[1])]
            for cp in loads:
                cp.start()
            for cp in loads:
                cp.wait()
            mine_buf[...] = (mine_buf[...].astype(F32) + via_buf[...].astype(F32)).astype(BF16)
            for core in (0, 1):
                @pl.when(c == core)
                def _(core=core):
                    for cp in sends(core):
                        cp.start()
            dg_ref[...] = jnp.zeros_like(dg_ref)
            w_load.wait()

        du = _dot(dz_ref[:, pl.ds(FF_BASE, 128)], t_ref[...])
        for d in range(N_DEV):
            cols = jnp.concatenate([dz_ref[:, pl.ds(128 * int(columns[d, j]), 128)] for j in range(7)], axis=1)
            du = du + _dot(cols, w_ref[d, pl.ds(0, W_STRIDE), :])
        h = h_ref[...]
        r = lax.rsqrt(jnp.mean(h * h, axis=-1, keepdims=True) + EPS)
        xn = h * r
        dg_ref[...] += jnp.sum(du * xn, axis=0, keepdims=True)
        dn = du * g_ref[...]
        dh_ref[...] = d_ref[...] + r * (dn - xn * jnp.mean(dn * xn, axis=-1, keepdims=True))

        for core in (0, 1):
            @pl.when(jnp.logical_and(c == core, i == steps - 1))
            def _(core=core):
                for cp in sends(core):
                    cp.wait_recv()
                    cp.wait_send()

    tile = pl.BlockSpec((tm, D_MODEL), lambda i: (i, 0))
    slab = jax.ShapeDtypeStruct(via.shape, via.dtype)
    return pl.pallas_call(
        body, name="du_rms", grid=(steps,),
        in_specs=[pl.BlockSpec((tm, D_IN_PAD), lambda i: (i, 0)), ANY,
                  pl.BlockSpec((128, D_MODEL), lambda i: (0, 0)),
                  tile, tile, pl.BlockSpec((1, D_MODEL), lambda i: (0, 0)), ANY, ANY],
        out_specs=[tile, pl.BlockSpec((1, D_MODEL), lambda i: (0, 0)), ANY, ANY],
        out_shape=[jax.ShapeDtypeStruct((T, D_MODEL), F32),
                   jax.ShapeDtypeStruct((1, D_MODEL), F32), slab, slab],
        scratch_shapes=[pltpu.VMEM(slabs.shape, slabs.dtype),
                        pltpu.VMEM(via.shape, via.dtype), pltpu.VMEM(via.shape, via.dtype),
                        pltpu.SemaphoreType.DMA,
                        pltpu.SemaphoreType.DMA((2,)), pltpu.SemaphoreType.DMA((2,)),
                        pltpu.SemaphoreType.DMA((2,))],
        compiler_params=_params(dimension_semantics=("arbitrary",)),
    )(dz, slabs, tail, h_pad, dout, g, kept, via)


def _tri(lower):
    r = lax.broadcasted_iota(jnp.int32, (CHUNK, CHUNK), 0)
    c = lax.broadcasted_iota(jnp.int32, (CHUNK, CHUNK), 1)
    return jnp.where((r >= c) if lower else (r <= c), 1.0, 0.0).astype(F32)


def _row_valid(n):
    r = lax.broadcasted_iota(jnp.int32, (CHUNK, 128), 0) + n * CHUNK
    return r >= PAD


_FF_SPEC = pl.BlockSpec((T, 128), lambda i: (0, FF_BASE // 128))
_ZFF_SPEC = pl.BlockSpec((T, 128), lambda i: (0, 0))


def _forget_fwd(z, b_pad):
    def body(z_ref, b_ref, o_ref):
        tri = _tri(True)
        carry = jnp.zeros((1, 128), F32)
        for n in range(NCHUNK):
            rows = pl.ds(n * CHUNK, CHUNK)
            a = z_ref[rows, :] + b_ref[...]
            lf = -(jnp.maximum(-a, 0.0) + jnp.log(1.0 + jnp.exp(-jnp.abs(a))))
            lf = jnp.where(_row_valid(n), lf, 0.0)
            c = jnp.dot(tri, lf, precision=lax.Precision.HIGHEST,
                        preferred_element_type=F32) + carry
            carry = c[CHUNK - 1:CHUNK, :]
            o_ref[:, rows] = jnp.where(_row_valid(n), c * (-LOG2E), NEG_INF).T

    return pl.pallas_call(
        body, name="forget_fwd", grid=(1,),
        in_specs=[_ZFF_SPEC, pl.BlockSpec((1, 128), lambda i: (0, 0))],
        out_specs=pl.BlockSpec((128, T), lambda i: (0, 0)),
        out_shape=jax.ShapeDtypeStruct((128, T), F32),
        compiler_params=_params(dimension_semantics=("arbitrary",)),
    )(z, b_pad)


def _forget_bwd(z, b_pad, dc, dz):
    def body(z_ref, b_ref, dc_ref, dz_in, dff_ref, db_ref):
        tri = _tri(False)
        carry = jnp.zeros((1, 128), F32)
        db = jnp.zeros((1, 128), F32)
        for n in reversed(range(NCHUNK)):
            rows = pl.ds(n * CHUNK, CHUNK)
            dc_blk = jnp.concatenate([dc_ref[:, rows], jnp.zeros((128 - FOX_HEADS, CHUNK), F32)], axis=0).T
            dlf = jnp.dot(tri, dc_blk, precision=lax.Precision.HIGHEST,
                          preferred_element_type=F32) + carry
            carry = dlf[0:1, :]
            a = z_ref[rows, :] + b_ref[...]
            dff = jnp.where(_row_valid(n), dlf * jax.nn.sigmoid(-a), 0.0)
            dff_ref[rows, :] = dff.astype(BF16)
            db = db + jnp.sum(dff, axis=0, keepdims=True)
        db_ref[...] = db

    return pl.pallas_call(
        body, name="forget_bwd", grid=(1,),
        in_specs=[_ZFF_SPEC, pl.BlockSpec((1, 128), lambda i: (0, 0)),
                  pl.BlockSpec((FOX_HEADS, T), lambda i: (0, 0)), ANY],
        out_specs=[_FF_SPEC, pl.BlockSpec((1, 128), lambda i: (0, 0))],
        out_shape=[jax.ShapeDtypeStruct((T, D_IN_PAD), BF16),
                   jax.ShapeDtypeStruct((1, 128), F32)],
        input_output_aliases={3: 0},
        compiler_params=_params(dimension_semantics=("arbitrary",)),
    )(z, b_pad, dc, dz)


FOX_QB = 512
FOX_NQB = SEQ // FOX_QB


def _fox_block(b):
    lo = CHUNK + b * FOX_QB
    return pl.ds(lo, FOX_QB), lo, lo + FOX_QB


def _causal_bias():
    r = lax.broadcasted_iota(jnp.int32, (FOX_QB, FOX_QB), 0)
    c = lax.broadcasted_iota(jnp.int32, (FOX_QB, FOX_QB), 1)
    return jnp.where(c <= r, 0.0, NEG_INF).astype(F32)


def _fox_logits(q_blk, k_all, bias, causal, b):
    _, lo, hi = _fox_block(b)
    here = bias[:, lo:lo + 1]
    s_off = _dot_nt(q_blk, k_all[:lo]) + (bias[:, :lo] - here)
    s_dia = _dot_nt(q_blk, k_all[lo:hi]) + ((bias[:, lo:hi] - here) + causal)
    return s_off, s_dia


_FOX_Z_SPEC = pl.BlockSpec((T, FOX_W), lambda p: (0, FOX_BASE // FOX_W + p))
_FOX_BIAS_SPEC = pl.BlockSpec((2, 1, T), lambda p: (p, 0, 0))
_FOX_LSE_SPEC = pl.BlockSpec((2, T, 1), lambda p: (p, 0, 0))
_FOX_SCALE = FOX_D ** -0.5
_FOX_QSCALE = _FOX_SCALE * LOG2E


def _fox_fwd(z, bias, y, w_out_blk):
    last = FOX_PAIRS - 1

    def body(z_ref, b_ref, y_in, w_ref, a_ref, lse_ref, y_ref, wall_ref,
             send_sems, recv_sems, local_sems):
        start, wait = _direct_exchange([w_ref], [wall_ref], send_sems, recv_sems, local_sems, True)
        pl.when(pl.program_id(0) == 0)(start)

        causal = _causal_bias()
        a_ref[pl.ds(0, CHUNK), :] = jnp.zeros((CHUNK, 128), F32)
        y_ref[pl.ds(0, CHUNK), :] = jnp.zeros((CHUNK, 128), BF16)
        for j in range(2):
            lanes = pl.ds(j * FOX_D, FOX_D)
            k_all = z_ref[:, pl.ds(128 + j * FOX_D, FOX_D)]
            v_all = z_ref[:, pl.ds(256 + j * FOX_D, FOX_D)]
            bias = b_ref[j]
            lse_ref[j, pl.ds(0, CHUNK), :] = jnp.zeros((CHUNK, 1), F32)
            for b in range(FOX_NQB):
                rows, lo, hi = _fox_block(b)
                q_blk = (z_ref[rows, lanes].astype(F32) * _FOX_QSCALE).astype(BF16)
                s_off, s_dia = _fox_logits(q_blk, k_all, bias, causal, b)
                m = jnp.maximum(jnp.max(s_off, axis=-1, keepdims=True),
                                jnp.max(s_dia, axis=-1, keepdims=True))
                e_off = jnp.exp2(s_off - m)
                e_dia = jnp.exp2(s_dia - m)
                total = jnp.sum(e_off, axis=-1, keepdims=True) + jnp.sum(e_dia, axis=-1, keepdims=True)
                o = (_dot(e_off.astype(BF16), v_all[:lo]) + _dot(e_dia.astype(BF16), v_all[lo:hi])) / total
                a_ref[rows, lanes] = o
                lse_ref[j, rows, :] = m + jnp.log(total) * LOG2E
                gate = _silu_parts(z_ref[rows, pl.ds(384 + j * FOX_D, FOX_D)].astype(F32))[0]
                y_ref[rows, lanes] = (o * gate).astype(BF16)

        pl.when(pl.program_id(0) == last)(wait)

    return pl.pallas_call(
        body, name="fox_fwd", grid=(FOX_PAIRS,),
        in_specs=[_FOX_Z_SPEC, _FOX_BIAS_SPEC, ANY, ANY],
        out_specs=[pl.BlockSpec((T, 128), lambda p: (0, p)), _FOX_LSE_SPEC,
                   pl.BlockSpec((T, 128), lambda p: (0, 8 + p)), ANY],
        out_shape=[jax.ShapeDtypeStruct((T, FOX_HEADS * FOX_D), F32),
                   jax.ShapeDtypeStruct((FOX_HEADS, T, 1), F32),
                   jax.ShapeDtypeStruct((T, D_MIX), BF16),
                   _exchange_shape(w_out_blk, True)],
        input_output_aliases={2: 2},
        scratch_shapes=_exchange_sems(1),
        compiler_params=_params(dimension_semantics=("arbitrary",)),
    )(z, bias, y, w_out_blk)


def _fox_bwd(z, bias, a_f, lse, dy, dz, dwo_blocks):
    last = FOX_PAIRS - 1

    def body(z_ref, b_ref, a_ref, lse_ref, dy_ref, dz_in, dwo_ref, dz_ref, dc_ref, got_ref,
             kv_acc, dc_acc, send_sems, recv_sems, local_sems):
        start, wait = _direct_exchange([dwo_ref], [got_ref], send_sems, recv_sems, local_sems, False)
        pl.when(pl.program_id(0) == 0)(start)

        causal = _causal_bias()
        dz_ref[pl.ds(0, CHUNK), pl.ds(0, 128)] = jnp.zeros((CHUNK, 128), BF16)
        dz_ref[pl.ds(0, CHUNK), pl.ds(384, 128)] = jnp.zeros((CHUNK, 128), BF16)
        dk_rows, dv_rows = pl.ds(0, FOX_D), pl.ds(FOX_D, FOX_D)
        for j in range(2):
            lanes = pl.ds(j * FOX_D, FOX_D)
            k_all = z_ref[:, pl.ds(128 + j * FOX_D, FOX_D)]
            v_all = z_ref[:, pl.ds(256 + j * FOX_D, FOX_D)]
            bias = b_ref[j]
            kv_acc[...] = jnp.zeros_like(kv_acc)
            dc_acc[...] = jnp.zeros_like(dc_acc)
            for b in range(FOX_NQB):
                rows, lo, hi = _fox_block(b)
                off, dia = pl.ds(0, lo), pl.ds(lo, FOX_QB)
                q_blk = (z_ref[rows, lanes].astype(F32) * _FOX_QSCALE).astype(BF16)
                s_off, s_dia = _fox_logits(q_blk, k_all, bias, causal, b)
                lse_blk = lse_ref[j, rows, :]
                p_off, p_dia = jnp.exp2(s_off - lse_blk), jnp.exp2(s_dia - lse_blk)
                sg, dsg = _silu_parts(z_ref[rows, pl.ds(384 + j * FOX_D, FOX_D)].astype(F32))
                dyj = dy_ref[rows, lanes].astype(F32)
                dz_ref[rows, pl.ds(384 + j * FOX_D, FOX_D)] = (dyj * a_ref[rows, lanes] * dsg).astype(BF16)
                do_b = (dyj * sg).astype(BF16)
                dp_off = _dot_nt(do_b, v_all[:lo])
                dp_dia = _dot_nt(do_b, v_all[lo:hi])
                d = (jnp.sum(p_off * dp_off, axis=-1, keepdims=True)
                     + jnp.sum(p_dia * dp_dia, axis=-1, keepdims=True))
                ds_off = p_off * (dp_off - d)
                ds_dia = p_dia * (dp_dia - d)
                dc_acc[:, off] -= jnp.sum(ds_off, axis=0, keepdims=True)
                dc_acc[:, dia] -= jnp.sum(ds_dia, axis=0, keepdims=True)
                ds_off_b, ds_dia_b = ds_off.astype(BF16), ds_dia.astype(BF16)
                dq = _dot(ds_off_b, k_all[:lo]) + _dot(ds_dia_b, k_all[lo:hi])
                dz_ref[rows, lanes] = (dq * _FOX_SCALE).astype(BF16)
                kv_acc[dk_rows, off] += _dot_tn(q_blk, ds_off_b)
                kv_acc[dk_rows, dia] += _dot_tn(q_blk, ds_dia_b)
                kv_acc[dv_rows, off] += _dot_tn(do_b, p_off.astype(BF16))
                kv_acc[dv_rows, dia] += _dot_tn(do_b, p_dia.astype(BF16))
            for n in range(NCHUNK):
                rows = pl.ds(n * CHUNK, CHUNK)
                both = kv_acc[:, rows].T
                dz_ref[rows, pl.ds(128 + j * FOX_D, FOX_D)] = (both[:, :FOX_D] * LN2).astype(BF16)
                dz_ref[rows, pl.ds(256 + j * FOX_D, FOX_D)] = both[:, FOX_D:].astype(BF16)
            dc_ref[j] = dc_acc[...]

        pl.when(pl.program_id(0) == last)(wait)

    col = lambda base: pl.BlockSpec((T, 128), lambda p: (0, base + p))
    return pl.pallas_call(
        body, name="fox_bwd", grid=(FOX_PAIRS,),
        in_specs=[_FOX_Z_SPEC, _FOX_BIAS_SPEC, col(0), _FOX_LSE_SPEC, col(8), ANY, ANY],
        out_specs=[_FOX_Z_SPEC, _FOX_BIAS_SPEC, ANY],
        out_shape=[jax.ShapeDtypeStruct((T, D_IN_PAD), BF16),
                   jax.ShapeDtypeStruct((FOX_HEADS, 1, T), F32),
                   _exchange_shape(dwo_blocks, False)],
        input_output_aliases={5: 0},
        scratch_shapes=[pltpu.VMEM((2 * FOX_D, T), F32), pltpu.VMEM((1, T), F32)] + _exchange_sems(1),
        compiler_params=_params(dimension_semantics=("arbitrary",)),
    )(z, bias, a_f, lse, dy, dz, dwo_blocks)


def _rot(x, cosf, sins):
    return x * cosf + pltpu.roll(x, RET_DK // 2, 1) * sins


def _rot_t(d, cosf, sins):
    return d * cosf - pltpu.roll(d, RET_DK // 2, 1) * sins


_RET_Z_SPEC = pl.BlockSpec((T, RET_W), lambda h: (0, h))
_RET_TABLE_SPECS = [
    pl.BlockSpec((T, RET_DK), lambda h: (0, 0)),
    pl.BlockSpec((T, RET_DK), lambda h: (0, 0)),
    pl.BlockSpec((1, CHUNK, CHUNK), lambda h: (h, 0, 0)),
    pl.BlockSpec((1, CHUNK, 1), lambda h: (h, 0, 0)),
    pl.BlockSpec((1, CHUNK, 1), lambda h: (h, 0, 0)),
    pl.BlockSpec((1, 1, 1), lambda h: (h, 0, 0)),
]
_RQ, _RK = pl.ds(0, RET_DK), pl.ds(RET_DK, RET_DK)
_RV, _RG = pl.ds(2 * RET_DK, RET_DV), pl.ds(2 * RET_DK + RET_DV, RET_DV)
_RET_KSCALE = RET_DK ** -0.5


_RET_STATE_SPEC = pl.BlockSpec((1, NCHUNK, RET_DK, RET_DV), lambda h: (h, 0, 0, 0))


def _ret_fwd_head(z_ref, cos_ref, sin_ref, dm_ref, zeta_ref, xi_ref, cd_ref, raw_ref, y_ref, st_ref):
    dmask, zeta, xi, cdec = dm_ref[0], zeta_ref[0], xi_ref[0], cd_ref[0]
    state = jnp.zeros((RET_DK, RET_DV), F32)
    for n in range(NCHUNK):
        rows = pl.ds(n * CHUNK, CHUNK)
        cosf, sins = cos_ref[rows, :], sin_ref[rows, :]
        qr = _rot(z_ref[rows, _RQ].astype(F32), cosf, sins)
        kr_b = (_rot(z_ref[rows, _RK].astype(F32), cosf, sins) * _RET_KSCALE).astype(BF16)
        v_b = z_ref[rows, _RV]
        a = _dot_nt(qr.astype(BF16), kr_b) * dmask
        state_b = state.astype(BF16)
        st_ref[0, n] = state_b
        out = _dot(a.astype(BF16), v_b) + _dot((qr * xi).astype(BF16), state_b)
        state = state * cdec + _dot_tn(kr_b, (v_b.astype(F32) * zeta).astype(BF16))
        raw_ref[rows, :] = out
        r = lax.rsqrt(jnp.mean(out * out, axis=-1, keepdims=True) + EPS)
        y_ref[rows, :] = (out * r * _silu_parts(z_ref[rows, _RG].astype(F32))[0]).astype(BF16)


def _ret_bwd(z, tables, raw, states, dy):
    def body(z_ref, cos_ref, sin_ref, dm_ref, zeta_ref, xi_ref, cd_ref, raw_ref, st_all, dy_ref, dz_ref):
        dmask, zeta, xi, cdec = dm_ref[0], zeta_ref[0], xi_ref[0], cd_ref[0]
        st_ref = st_all.at[0]

        grad_state = jnp.zeros((RET_DK, RET_DV), F32)
        for n in reversed(range(NCHUNK)):
            rows = pl.ds(n * CHUNK, CHUNK)
            cosf, sins = cos_ref[rows, :], sin_ref[rows, :]
            qr = _rot(z_ref[rows, _RQ].astype(F32), cosf, sins)
            kr_b = (_rot(z_ref[rows, _RK].astype(F32), cosf, sins) * _RET_KSCALE).astype(BF16)
            qr_b = qr.astype(BF16)
            v_b = z_ref[rows, _RV]
            gs_b = grad_state.astype(BF16)
            o = raw_ref[rows, :]
            r = lax.rsqrt(jnp.mean(o * o, axis=-1, keepdims=True) + EPS)
            hn = o * r
            sg, dsg = _silu_parts(z_ref[rows, _RG].astype(F32))
            dyn = dy_ref[rows, :].astype(F32)
            dz_ref[rows, _RG] = (dyn * hn * dsg).astype(BF16)
            dhn = dyn * sg
            do_b = (r * (dhn - hn * jnp.mean(dhn * hn, axis=-1, keepdims=True))).astype(BF16)
            a_b = (_dot_nt(qr_b, kr_b) * dmask).astype(BF16)
            da_b = (_dot_nt(do_b, v_b) * dmask).astype(BF16)
            dqr = _dot(da_b, kr_b) + xi * _dot_nt(do_b, st_ref[n])
            dkr = _dot_tn(da_b, qr_b) + zeta * _dot_nt(v_b, gs_b)
            dv = _dot_tn(a_b, do_b) + zeta * _dot(kr_b, gs_b)
            grad_state = grad_state * cdec + _dot_tn((qr * xi).astype(BF16), do_b)
            dz_ref[rows, _RQ] = _rot_t(dqr, cosf, sins).astype(BF16)
            dz_ref[rows, _RK] = (_rot_t(dkr, cosf, sins) * _RET_KSCALE).astype(BF16)
            dz_ref[rows, _RV] = dv.astype(BF16)

    wide = pl.BlockSpec((T, RET_DV), lambda h: (0, h))
    return pl.pallas_call(
        body, name="ret_bwd", grid=(RET_HEADS,),
        in_specs=[_RET_Z_SPEC] + _RET_TABLE_SPECS + [wide, _RET_STATE_SPEC, wide],
        out_specs=_RET_Z_SPEC,
        out_shape=jax.ShapeDtypeStruct((T, D_IN_PAD), BF16),
        compiler_params=_params(dimension_semantics=("arbitrary",)),
    )(z, *tables, raw, states, dy)


def _adamw(w, g, m, v):
    m = ADAM_B1 * m + (1.0 - ADAM_B1) * g
    v = ADAM_B2 * v + (1.0 - ADAM_B2) * (g * g)
    m_hat = m / (1.0 - ADAM_B1 ** ADAM_STEP)
    v_hat = v / (1.0 - ADAM_B2 ** ADAM_STEP)
    delta = -ADAM_LR * (m_hat / (jnp.sqrt(v_hat) + ADAM_EPS) + ADAM_WD * w)
    return delta, m, v


def _sum_adamw(parts, w, m, v, rows, name):
    _, r_tot, cols = parts.shape
    assert r_tot % rows == 0

    def body(p_ref, w_ref, m_ref, v_ref, g_ref, d_ref, nm_ref, nv_ref):
        g = p_ref[0].astype(F32)
        for d in range(1, N_DEV):
            g = g + p_ref[d].astype(F32)
        delta, nm, nv = _adamw(w_ref[...], g, m_ref[...], v_ref[...])
        g_ref[...] = g
        d_ref[...] = delta
        nm_ref[...] = nm
        nv_ref[...] = nv

    blk = pl.BlockSpec((rows, cols), lambda i: (i, 0))
    return pl.pallas_call(
        body, name=name, grid=(r_tot // rows,),
        in_specs=[pl.BlockSpec((N_DEV, rows, cols), lambda i: (0, i, 0)), blk, blk, blk],
        out_specs=[blk] * 4,
        out_shape=[jax.ShapeDtypeStruct((r_tot, cols), F32)] * 4,
        compiler_params=_params(dimension_semantics=("arbitrary",)),
    )(parts, w, m, v)


def _sum_adamw_w_in(parts, w, m, v, small):
    n_part, r, c = parts.shape
    steps = c // 128

    def body(p_ref, w_hbm, m_hbm, v_hbm, s_ref, g_hbm, d_hbm, nm_hbm, nv_hbm, got_ref,
             in_buf, out_buf, in_sems, out_sems, send_sems, recv_sems, local_sems):
        start, wait = _direct_exchange([s_ref], [got_ref], send_sems, recv_sems, local_sems, True)
        i = pl.program_id(0)
        pl.when(i == 0)(start)
        slot = i % 2

        def loads(step, into):
            cols = pl.ds(pl.multiple_of(step * 128, 128), 128)
            return [pltpu.make_async_copy(h.at[:, 0, cols], in_buf.at[into, k], in_sems.at[into, k])
                    for k, h in enumerate((w_hbm, m_hbm, v_hbm))]

        cols = pl.ds(pl.multiple_of(i * 128, 128), 128)
        stores = [pltpu.make_async_copy(out_buf.at[k], h.at[:, 0, cols], out_sems.at[k])
                  for k, h in enumerate((g_hbm, d_hbm, nm_hbm, nv_hbm))]

        @pl.when(i == 0)
        def _():
            for cp in loads(0, 0):
                cp.start()

        @pl.when(i + 1 < steps)
        def _():
            for cp in loads(i + 1, 1 - slot):
                cp.start()

        g = p_ref[0].astype(F32)
        for d in range(1, n_part):
            g = g + p_ref[d].astype(F32)
        for cp in loads(i, slot):
            cp.wait()
        delta, nm, nv = _adamw(in_buf[slot, 0], g, in_buf[slot, 1], in_buf[slot, 2])

        @pl.when(i > 0)
        def _():
            for cp in stores:
                cp.wait()

        for k, val in enumerate((g, delta, nm, nv)):
            out_buf[k] = val
        for cp in stores:
            cp.start()

        @pl.when(i == steps - 1)
        def _():
            for cp in stores:
                cp.wait()
            wait()

    return pl.pallas_call(
        body, name="adamw_w_in", grid=(steps,),
        in_specs=[pl.BlockSpec((n_part, r, 128), lambda i: (0, 0, i)), ANY, ANY, ANY, ANY],
        out_specs=[ANY] * 5,
        out_shape=[jax.ShapeDtypeStruct((r, 1, c), F32)] * 4 + [_exchange_shape(small, True)],
        scratch_shapes=[pltpu.VMEM((2, 3, r, 128), F32), pltpu.VMEM((4, r, 128), F32),
                        pltpu.SemaphoreType.DMA((2, 3)), pltpu.SemaphoreType.DMA((4,))] + _exchange_sems(1),
        compiler_params=_params(dimension_semantics=("arbitrary",)),
    )(parts, w, m, v, small)


def _adamw_small(got, me, metas, norms, finals, biases):
    def body(me_ref, gm_ref, gr_ref, *refs):
        ins, outs = refs[:12], refs[12:]
        g_meta, g_rest = gm_ref[0], gr_ref[0]
        for d in range(1, N_DEV):
            g_meta, g_rest = g_meta + gm_ref[d], g_rest + gr_ref[d]
        grads = [g_meta, g_rest[0:1], g_rest[1:2], g_rest[2:3, :FOX_HEADS]]
        for k, g in enumerate(grads):
            w_ref, m_ref, v_ref = ins[3 * k:3 * k + 3]
            delta, new_m, new_v = _adamw(w_ref[...], g, m_ref[...], v_ref[...])
            for o_ref, val in zip(outs[4 * k:4 * k + 4], (g, delta, new_m, new_v)):
                o_ref[...] = val
        outs[16][...] = g_rest[3:4, :128]

    groups = (metas, norms, finals, biases)
    full = lambda a: pl.BlockSpec(a.shape, lambda i, me_ref: (0,) * a.ndim)
    flat = [a for grp in groups for a in grp]
    res = pl.pallas_call(
        body, name="adamw_small",
        grid_spec=pltpu.PrefetchScalarGridSpec(
            num_scalar_prefetch=1, grid=(1,),
            in_specs=[pl.BlockSpec((N_DEV, N_META, META_BLK), lambda i, me_ref: (0, 0, me_ref[0])),
                      pl.BlockSpec((N_DEV, 8, D_MODEL), lambda i, me_ref: (0, N_META // 8, 0))]
            + [full(a) for a in flat],
            out_specs=[full(grp[0]) for grp in groups for _ in range(4)]
            + [pl.BlockSpec((1, 128), lambda i, me_ref: (0, 0))]),
        out_shape=[jax.ShapeDtypeStruct(grp[0].shape, F32) for grp in groups for _ in range(4)]
        + [jax.ShapeDtypeStruct((1, 128), F32)],
        compiler_params=_params(dimension_semantics=("arbitrary",)),
    )(me, got, got, *flat)
    return [res[4 * k:4 * k + 4] for k in range(4)], res[16]


def kernel(x, meta_tokens, norm_g, w_in, b_f, w_out, final_g, loss_target, m_meta_tokens, m_norm_g, m_w_in, m_b_f, m_w_out, m_final_g, v_meta_tokens, v_norm_g, v_w_in, v_b_f, v_w_out, v_final_g):
    core = lax.axis_index("c")
    me = 4 * lax.axis_index("x") + 2 * lax.axis_index("y") + core
    tables = _tables()

    wt_all, meta_all = _gather_two_level([_slab(w_in[0].T.astype(BF16), me), meta_tokens], name="gather_w_in")
    slabs, tail = _join_edges(wt_all)
    meta_full = jnp.transpose(meta_all, (1, 0, 2)).reshape(N_META, D_MODEL)
    b_pad = jnp.pad(b_f, ((0, 0), (0, 128 - FOX_HEADS)))

    h_pad, u, z, raw, y, states = _rms_z_ret(x[0], meta_full, norm_g, slabs, tables)
    z, zff = _z_fox(u, slabs, tail, z)
    bias = _forget_fwd(zff, b_pad)[:FOX_HEADS].reshape(FOX_HEADS, 1, T)
    a_f, lse, y, w_out_all = _fox_fwd(z, bias, y, w_out[0].astype(BF16))
    w_out_b = w_out_all.reshape(D_MIX, D_MODEL)
    dout, dout_b, dy, loss_blk, d_final_g = _out_loss_dy(y, w_out_b, h_pad, loss_target[0],
                                                         final_g.reshape(1, D_MODEL))

    d_w_out = _mm_tn(y, dout_b, tm=D_MIX, tn=256, name="mm_dwout")
    dz = _ret_bwd(z, tables, raw, states, dy)
    dz, dc, got_w_out = _fox_bwd(z, bias, a_f, lse, dy, dz, d_w_out.reshape(N_DEV, WO_BLK, D_MODEL))
    dz, db_f = _forget_bwd(zff, b_pad, dc.reshape(FOX_HEADS, T), dz)

    kept, via = _dwin_pair_slabs(dz, u, core)
    dh, d_norm_g, direct, summed = _du_rms(dz, slabs, tail, h_pad, dout, norm_g, kept, via)
    got_w_in = _unslab([kept[2], direct, summed], me)

    small = jnp.concatenate([
        dh[PAD:CHUNK], d_norm_g, d_final_g, jnp.pad(db_f[:, :FOX_HEADS], ((0, 0), (0, D_MODEL - FOX_HEADS))),
        jnp.pad(loss_blk[0:1], ((0, 0), (0, D_MODEL - 128))),
        jnp.zeros((SMALL_ROWS - N_META - 4, D_MODEL), F32)], axis=0)
    fore = lambda a: jnp.transpose(a, (2, 0, 1))
    g_w_in, d_w_in, nm_w_in, nv_w_in, got_small = _sum_adamw_w_in(
        got_w_in, fore(w_in), fore(m_w_in), fore(v_w_in), small)
    g_w_out, d_w_out, nm_w_out, nv_w_out = _sum_adamw(got_w_out, w_out[0], m_w_out[0], v_w_out[0], 128, "adamw_w_out")

    row = lambda a: a.reshape(1, D_MODEL)
    (meta_o, norm_o, final_o, bias_o), loss_row = _adamw_small(
        got_small, me.astype(jnp.int32).reshape(1),
        (meta_tokens, m_meta_tokens, v_meta_tokens), (norm_g, m_norm_g, v_norm_g),
        (row(final_g), row(m_final_g), row(v_final_g)), (b_f, m_b_f, v_b_f))
    final_o = [a.reshape(D_MODEL) for a in final_o]

    back = lambda a: jnp.transpose(a, (1, 2, 0))
    outs = [[meta_o[k], norm_o[k], back(wk), bias_o[k], ok[None], final_o[k]]
            for k, (wk, ok) in enumerate(zip((g_w_in, d_w_in, nm_w_in, nv_w_in),
                                             (g_w_out, d_w_out, nm_w_out, nv_w_out)))]
    return (loss_row[0, 0], dh[CHUNK:][None], *outs[0], *outs[1], *outs[2], *outs[3])
```

```python
import numpy as np
import jax
import jax.numpy as jnp
from jax import lax
from jax.experimental import pallas as pl
from jax.experimental.pallas import tpu as pltpu

F32 = jnp.float32
BF16 = jnp.bfloat16

N_DEV = 8
N_CHIP = 4
D_MODEL = 1024
SEQ = 2048
N_META = 16
CHUNK = 128
PAD = CHUNK - N_META
T = SEQ + CHUNK
NCHUNK = T // CHUNK
D_MIX = 2048
RET_HEADS = 4
RET_DK = 128
RET_DV = 256
RET_W = 2 * RET_DK + 2 * RET_DV
FOX_HEADS = 16
FOX_D = 64
FOX_PAIRS = FOX_HEADS // 2
FOX_W = 4 * 128
FOX_BASE = RET_HEADS * RET_W
FF_BASE = FOX_BASE + FOX_PAIRS * FOX_W
D_IN = 7184
D_IN_PAD = 7296
W_BLK = D_IN // N_DEV
WO_BLK = D_MIX // N_DEV
META_BLK = D_MODEL // N_DEV
EPS = 1e-6
NEG_INF = -1e30
ROPE_BASE = 10000.0
LOG2E = 1.4426950408889634
LN2 = 0.6931471805599453

ADAM_LR = 0.001
ADAM_B1 = 0.9
ADAM_B2 = 0.999
ADAM_EPS = 1e-08
ADAM_WD = 0.01
ADAM_STEP = 10

SMALL_ROWS = 24
VMEM_LIMIT = 56 * 1024 * 1024
MESH = pl.DeviceIdType.MESH
ANY = pl.BlockSpec(memory_space=pl.ANY)

_NT = (((1,), (1,)), ((), ()))
_TN = (((0,), (0,)), ((), ()))


def _dot(a, b):
    return jnp.dot(a, b, preferred_element_type=F32)


def _dot_nt(a, b):
    return lax.dot_general(a, b, _NT, preferred_element_type=F32)


def _dot_tn(a, b):
    return lax.dot_general(a, b, _TN, preferred_element_type=F32)


def _params(**kw):
    return pltpu.CompilerParams(vmem_limit_bytes=VMEM_LIMIT, **kw)


def _silu_parts(g):
    sig = jax.nn.sigmoid(g)
    return g * sig, sig * (1.0 + g * (1.0 - sig))


def _tables():
    pos = np.arange(T, dtype=np.float32) - PAD
    inv = (ROPE_BASE ** (-np.arange(0, RET_DK, 2, dtype=np.float32) / RET_DK)).astype(np.float32)
    ang = pos[:, None] * inv[None, :]
    cos, sin = np.cos(ang), np.sin(ang)
    cosf = np.concatenate([cos, cos], axis=1).astype(np.float32)
    sins = np.concatenate([-sin, sin], axis=1).astype(np.float32)
    h = np.arange(RET_HEADS, dtype=np.float32)
    log_gamma = np.log1p(-np.exp2(-5.0 - h)).astype(np.float32)
    idx = np.arange(CHUNK, dtype=np.float32)
    diff = idx[:, None] - idx[None, :]
    dmask = np.where(diff[None] >= 0,
                     np.exp(log_gamma[:, None, None] * np.maximum(diff, 0.0)[None]), 0.0)
    zeta = np.exp(log_gamma[:, None] * (CHUNK - 1.0 - idx)[None, :])
    xi = np.exp(log_gamma[:, None] * (idx + 1.0)[None, :])
    cdec = np.exp(log_gamma * CHUNK)
    return (jnp.asarray(cosf), jnp.asarray(sins), jnp.asarray(dmask, F32),
            jnp.asarray(zeta[:, :, None], F32), jnp.asarray(xi[:, :, None], F32),
            jnp.asarray(cdec[:, None, None], F32))


W_STRIDE = 896
W_SLAB = 912
W_EDGE = W_SLAB - W_STRIDE
def _slab(block, me):
    shift = W_BLK - W_STRIDE
    return lax.switch(me, [lambda b, d=d: jnp.pad(b, ((shift * d, W_SLAB - W_BLK - shift * d), (0, 0)))
                           for d in range(N_DEV)], block)


def _unslab(slabs, me):
    shift = W_BLK - W_STRIDE
    return lax.switch(me, [lambda *s, d=d: jnp.stack([a[shift * d:shift * d + W_BLK] for a in s])
                           for d in range(N_DEV)], *slabs)


def _join_edges(slabs):
    last = slabs[:, W_STRIDE:]
    first = slabs[:, :W_EDGE] + jnp.concatenate([jnp.zeros_like(last[:1]), last[:-1]], axis=0)
    tail = jnp.pad(last[N_DEV - 1], ((0, 128 - W_EDGE), (0, 0)))
    return lax.dynamic_update_slice(slabs, first, (0, 0, 0)), tail


def _mm_tn(a, b, *, tm, tn, name):
    k, m = a.shape
    n = b.shape[1]
    assert m % tm == 0 and n % tn == 0

    def body(a_ref, b_ref, o_ref):
        o_ref[...] = _dot_tn(a_ref[...], b_ref[...]).astype(BF16)

    return pl.pallas_call(
        body, name=name, grid=(n // tn, m // tm),
        in_specs=[pl.BlockSpec((k, tm), lambda j, i: (0, i)),
                  pl.BlockSpec((k, tn), lambda j, i: (0, j))],
        out_specs=pl.BlockSpec((tm, tn), lambda j, i: (i, j)),
        out_shape=jax.ShapeDtypeStruct((m, n), BF16),
        compiler_params=_params(dimension_semantics=("arbitrary", "arbitrary")),
    )(a, b)


def _piece_spec(base, mult):
    def index(i):
        p = base + mult * i
        return p // 7, p % 7, 0
    return pl.BlockSpec((1, 128, D_MODEL), index)


_RET_PIECES = ((0, 1), (4, 1), (8, 2), (9, 2), (16, 2), (17, 2))
_FOX_PIECES = ((24, 1), (32, 1), (40, 1), (48, 1))


def _rms_z_ret(x, meta, g, slabs, tables):
    def body(x_hbm, m_ref, g_ref, *refs):
        pieces, tabs = refs[:6], refs[6:12]
        h_ref, u_ref, z_ref, raw_ref, y_ref, st_ref, x_sem = refs[12:]

        @pl.when(pl.program_id(0) == 0)
        def _():
            tokens = pltpu.make_async_copy(x_hbm, h_ref.at[pl.ds(CHUNK, SEQ)], x_sem)
            tokens.start()
            h_ref[pl.ds(0, PAD), :] = jnp.zeros((PAD, D_MODEL), F32)
            h_ref[pl.ds(PAD, N_META), :] = m_ref[...]
            tokens.wait()
            h = h_ref[...]
            r = lax.rsqrt(jnp.mean(h * h, axis=-1, keepdims=True) + EPS)
            u_ref[...] = (h * r * g_ref[...]).astype(BF16)

        w = jnp.concatenate([p[0] for p in pieces], axis=0)
        z_ref[...] = _dot_nt(u_ref[...], w).astype(BF16)
        _ret_fwd_head(z_ref, *tabs, raw_ref, y_ref, st_ref)

    whole = pl.BlockSpec((T, D_MODEL), lambda i: (0, 0))
    wide = pl.BlockSpec((T, RET_DV), lambda i: (0, i))
    return pl.pallas_call(
        body, name="rms_z_ret", grid=(RET_HEADS,),
        in_specs=[ANY, pl.BlockSpec((N_META, D_MODEL), lambda i: (0, 0)),
                  pl.BlockSpec((1, D_MODEL), lambda i: (0, 0))]
        + [_piece_spec(*bm) for bm in _RET_PIECES] + _RET_TABLE_SPECS,
        out_specs=[whole, whole, pl.BlockSpec((T, RET_W), lambda i: (0, i)), wide, wide, _RET_STATE_SPEC],
        out_shape=[jax.ShapeDtypeStruct((T, D_MODEL), F32),
                   jax.ShapeDtypeStruct((T, D_MODEL), BF16),
                   jax.ShapeDtypeStruct((T, D_IN_PAD), BF16),
                   jax.ShapeDtypeStruct((T, RET_HEADS * RET_DV), F32),
                   jax.ShapeDtypeStruct((T, D_MIX), BF16),
                   jax.ShapeDtypeStruct((RET_HEADS, NCHUNK, RET_DK, RET_DV), BF16)],
        scratch_shapes=[pltpu.SemaphoreType.DMA],
        compiler_params=_params(dimension_semantics=("arbitrary",)),
    )(x, meta, g, *([slabs] * 6), *tables)


def _z_fox(u, slabs, tail, z):
    def body(u_ref, *refs):
        pieces, t_ref, z_ref, zff_ref = refs[:4], refs[4], refs[6], refs[7]
        u = u_ref[...]
        w = jnp.concatenate([p[0] for p in pieces], axis=0)
        z_ref[...] = _dot_nt(u, w).astype(BF16)

        @pl.when(pl.program_id(0) == 0)
        def _():
            zff_ref[...] = _dot_nt(u, t_ref[...])

    return pl.pallas_call(
        body, name="z_fox", grid=(FOX_PAIRS,),
        in_specs=[pl.BlockSpec((T, D_MODEL), lambda i: (0, 0))] + [_piece_spec(*bm) for bm in _FOX_PIECES]
        + [pl.BlockSpec((128, D_MODEL), lambda i: (0, 0)), ANY],
        out_specs=[_FOX_Z_SPEC, pl.BlockSpec((T, 128), lambda i: (0, 0))],
        out_shape=[jax.ShapeDtypeStruct((T, D_IN_PAD), BF16),
                   jax.ShapeDtypeStruct((T, 128), F32)],
        input_output_aliases={6: 0},
        compiler_params=_params(dimension_semantics=("arbitrary",)),
    )(u, *([slabs] * 4), tail, z)


def _out_loss_dy(y, w_out_b, h_pad, target, g):
    tm = T // 4

    def body(y_ref, w_ref, h_ref, t_hbm, g_ref, d_ref, db_ref, dy_ref, loss_ref, dg_ref, t_buf, t_sem):
        i = pl.program_id(0)
        head = pltpu.make_async_copy(t_hbm.at[pl.ds(0, tm - CHUNK)], t_buf.at[pl.ds(CHUNK, tm - CHUNK)], t_sem)
        rest = pltpu.make_async_copy(t_hbm.at[pl.ds(pl.multiple_of(jnp.maximum(i, 1) * tm - CHUNK, 8), tm)],
                                     t_buf, t_sem)

        @pl.when(i == 0)
        def _():
            t_buf[pl.ds(0, CHUNK), :] = jnp.zeros((CHUNK, D_MODEL), F32)
            head.start()
            loss_ref[...] = jnp.zeros_like(loss_ref)
            dg_ref[...] = jnp.zeros_like(dg_ref)

        pl.when(i > 0)(rest.start)

        w = w_ref[...]
        o = _dot(y_ref[...], w) + h_ref[...]
        pl.when(i == 0)(head.wait)
        pl.when(i > 0)(rest.wait)
        token = lax.broadcasted_iota(jnp.int32, (tm, 1), 0) + i * tm >= CHUNK
        g = g_ref[...]
        r = lax.rsqrt(jnp.mean(o * o, axis=-1, keepdims=True) + EPS)
        xn = o * r
        e = jnp.where(token, xn * g - t_buf[...], 0.0)
        loss_ref[...] += jnp.full(loss_ref.shape, 0.5 / D_MODEL * jnp.sum(e * e), F32)
        do = e * (1.0 / D_MODEL)
        dg_ref[...] += jnp.sum(do * xn, axis=0, keepdims=True)
        dn = do * g
        d = r * (dn - xn * jnp.mean(dn * xn, axis=-1, keepdims=True))
        d_b = d.astype(BF16)
        d_ref[...] = d
        db_ref[...] = d_b
        dy_ref[...] = _dot_nt(d_b, w).astype(BF16)

    tile = pl.BlockSpec((tm, D_MODEL), lambda i: (i, 0))
    wide = pl.BlockSpec((tm, D_MIX), lambda i: (i, 0))
    return pl.pallas_call(
        body, name="out_loss_dy", grid=(T // tm,),
        in_specs=[wide, pl.BlockSpec((D_MIX, D_MODEL), lambda i: (0, 0)), tile, ANY,
                  pl.BlockSpec((1, D_MODEL), lambda i: (0, 0))],
        out_specs=[tile, tile, wide,
                   pl.BlockSpec((8, 128), lambda i: (0, 0)),
                   pl.BlockSpec((1, D_MODEL), lambda i: (0, 0))],
        out_shape=[jax.ShapeDtypeStruct((T, D_MODEL), F32),
                   jax.ShapeDtypeStruct((T, D_MODEL), BF16),
                   jax.ShapeDtypeStruct((T, D_MIX), BF16),
                   jax.ShapeDtypeStruct((8, 128), F32),
                   jax.ShapeDtypeStruct((1, D_MODEL), F32)],
        scratch_shapes=[pltpu.VMEM((tm, D_MODEL), F32), pltpu.SemaphoreType.DMA],
        compiler_params=_params(dimension_semantics=("arbitrary",)),
    )(y, w_out_b, h_pad, target, g)


def _coords():
    return lax.axis_index("x"), lax.axis_index("y"), lax.axis_index("c")


def _flip(v, bit):
    return 1 - v if bit else v


def _peer(x, y, c, r):
    return _flip(x, (r >> 2) & 1), _flip(y, (r >> 1) & 1), _flip(c, r & 1)


def _direct_exchange(ins, outs, send_sems, recv_sems, local_sems, gather):
    x, y, c = _coords()
    me = 4 * x + 2 * y + c

    def src(k, to_idx):
        return ins[k] if gather else ins[k].at[to_idx]

    local = [pltpu.make_async_copy(src(k, me), outs[k].at[me], local_sems.at[k])
             for k in range(len(ins))]
    sends, recvs = [], []
    for r in range(1, N_DEV):
        px, py, pc = _peer(x, y, c, r)
        peer = 4 * px + 2 * py + pc
        for k in range(len(ins)):
            sems = dict(send_sem=send_sems.at[k, r - 1], recv_sem=recv_sems.at[k, r - 1],
                        device_id=(px, py, pc), device_id_type=MESH)
            sends.append(pltpu.make_async_remote_copy(src_ref=src(k, peer), dst_ref=outs[k].at[me], **sems))
            recvs.append(pltpu.make_async_remote_copy(src_ref=src(k, peer), dst_ref=outs[k].at[peer], **sems))

    def start():
        for cp in local + sends:
            cp.start()

    def wait():
        for cp in recvs:
            cp.wait_recv()
        for cp in sends:
            cp.wait_send()
        for cp in local:
            cp.wait()

    return start, wait


def _exchange_sems(n_arr):
    return [pltpu.SemaphoreType.DMA((n_arr, N_DEV - 1)), pltpu.SemaphoreType.DMA((n_arr, N_DEV - 1)),
            pltpu.SemaphoreType.DMA((n_arr,))]


def _exchange_shape(a, gather):
    return jax.ShapeDtypeStruct(((N_DEV,) + a.shape) if gather else a.shape, a.dtype)


def _gather_two_level(arrays, name):
    n_arr = len(arrays)

    def body(*refs):
        ins, outs = refs[:n_arr], refs[n_arr:2 * n_arr]
        send_sems, recv_sems, local_sems = refs[2 * n_arr:]
        x, y, c = _coords()

        def slot(k, px, py, pc):
            return outs[k].at[4 * px + 2 * py + pc]

        def routed(core):
            me, sibling = (x, y, core), (x, y, 1 - core)
            xn, yn, dg = (1 - x, y), (x, 1 - y), (1 - x, 1 - y)
            (first, s_first), (second, s_second) = ((xn, 1), (yn, 2)) if core == 0 else ((yn, 2), (xn, 1))

            def copy(k, j, block, to, own=False):
                return pltpu.make_async_remote_copy(
                    src_ref=ins[k] if own else slot(k, *block), dst_ref=slot(k, *block),
                    send_sem=send_sems.at[k, j], recv_sem=recv_sems.at[k, j],
                    device_id=to, device_id_type=MESH)

            local = [pltpu.make_async_copy(ins[k], slot(k, *me), local_sems.at[k]) for k in range(n_arr)]
            sent = []
            for k in range(n_arr):
                sent += [copy(k, 0, me, sibling, True), copy(k, 1, me, (*xn, core), True),
                         copy(k, 2, me, (*yn, core), True)]
            for cp in local + sent:
                cp.start()

            def pass_on(k, j_from, j_to, block, targets):
                copy(k, j_from, block, me).wait_recv()
                for j, to in zip(j_to, targets):
                    cp = copy(k, j, block, to)
                    cp.start()
                    sent.append(cp)

            for k in range(n_arr):
                pass_on(k, s_first, (3, 3 + s_first), (*first, core), ((*second, core), sibling))
            for k in range(n_arr):
                pass_on(k, s_second, (3 + s_second,), (*second, core), (sibling,))
            for k in range(n_arr):
                pass_on(k, 3, (6,), (*dg, core), (sibling,))
            for k in range(n_arr):
                copy(k, 0, sibling, me).wait_recv()
                for j, chip in ((4, xn), (5, yn), (6, dg)):
                    copy(k, j, (*chip, 1 - core), me).wait_recv()
            for cp in sent:
                cp.wait_send()
            for cp in local:
                cp.wait()

        for core in (0, 1):
            pl.when(c == core)(lambda core=core: routed(core))

    return pl.pallas_call(
        body, name=name,
        in_specs=[ANY] * n_arr, out_specs=[ANY] * n_arr,
        out_shape=[_exchange_shape(a, True) for a in arrays],
        scratch_shapes=_exchange_sems(n_arr),
    )(*arrays)


def _piece_columns():
    pos = {}
    for h in range(RET_HEADS):
        for k, p in enumerate((h, 4 + h, 8 + 2 * h, 9 + 2 * h, 16 + 2 * h, 17 + 2 * h)):
            pos[p] = 6 * h + k
    for p in range(FOX_PAIRS):
        for i in range(4):
            pos[24 + 8 * i + p] = 24 + 4 * p + i
    pos[D_IN_PAD // 128 - 1] = D_IN_PAD // 128 - 1
    return np.array([pos[7 * d + j] for d in range(N_DEV) for j in range(8)], np.int32)


def _routes(core):
    x, y, _ = _coords()
    return ((1 - x, y), (x, 1 - y)) if core == 0 else ((x, 1 - y), (1 - x, y))


def _dwin_pair_slabs(dz, u, core):
    def body(order_ref, cols_ref, *refs):
        pieces, u_ref = refs[:8], refs[8]
        kept_ref, theirs_ref, via_ref = refs[9:12]
        send_buf, keep_buf, got_buf, push_send, push_recv, hop_send, hop_recv, load_sem = refs[12:]
        s = pl.program_id(0)
        x, y, c = _coords()
        cols = jnp.concatenate([p[...] for p in pieces[:7]] + [pieces[7][:, :W_EDGE]], axis=1)
        slab = _dot_tn(cols, u_ref[...])

        def push(k):
            return pltpu.make_async_remote_copy(
                src_ref=send_buf.at[k], dst_ref=theirs_ref.at[k],
                send_sem=push_send.at[k], recv_sem=push_recv.at[k],
                device_id=(x, y, 1 - c), device_id_type=MESH)

        def hop(core):
            first, _ = _routes(core)
            return pltpu.make_async_remote_copy(
                src_ref=keep_buf, dst_ref=via_ref, send_sem=hop_send, recv_sem=hop_recv,
                device_id=(*first, core), device_id_type=MESH)

        for k in range(N_CHIP):
            @pl.when(s == 2 * k)
            def _(k=k):
                send_buf[k] = slab.astype(BF16)
                push(k).start()

            @pl.when(s == 2 * k + 1)
            def _(k=k):
                push(k).wait_recv()
                load = pltpu.make_async_copy(theirs_ref.at[k], got_buf, load_sem)
                load.start()
                load.wait()
                total = (slab + got_buf[...].astype(F32)).astype(BF16)
                if k == 0:
                    keep_buf[...] = total
                    for core in (0, 1):
                        pl.when(c == core)(lambda core=core: hop(core).start())
                else:
                    kept_ref[0] = total

        @pl.when(s == 2 * N_CHIP - 1)
        def _():
            for core in (0, 1):
                @pl.when(c == core)
                def _(core=core):
                    hop(core).wait_recv()
                    hop(core).wait_send()
            for k in range(N_CHIP):
                push(k).wait_send()

    x, y = lax.axis_index("x"), lax.axis_index("y")
    xn, yn, dg, own = 2 * (1 - x) + y, 2 * x + 1 - y, 2 * (1 - x) + 1 - y, 2 * x + y
    mine = jnp.where(core == 0, jnp.stack([dg, xn, yn, own]), jnp.stack([dg, yn, xn, own]))
    sibs = jnp.where(core == 0, jnp.stack([dg, yn, xn, own]), jnp.stack([dg, xn, yn, own]))
    order = jnp.stack([2 * sibs + (1 - core), 2 * mine + core], axis=1).reshape(2 * N_CHIP).astype(jnp.int32)
    piece = lambda j: pl.BlockSpec((T, 128), lambda s, order_ref, cols_ref: (0, cols_ref[order_ref[s] * 8 + j]))
    slab = jax.ShapeDtypeStruct((W_SLAB, D_MODEL), BF16)
    kept, _, via = pl.pallas_call(
        body, name="dwin_pair_slabs",
        grid_spec=pltpu.PrefetchScalarGridSpec(
            num_scalar_prefetch=2, grid=(2 * N_CHIP,),
            in_specs=[piece(j) for j in range(8)]
            + [pl.BlockSpec((T, D_MODEL), lambda s, order_ref, cols_ref: (0, 0))],
            out_specs=[pl.BlockSpec((1, W_SLAB, D_MODEL),
                                    lambda s, order_ref, cols_ref: (jnp.maximum(s - 2, 0) // 2, 0, 0)),
                       ANY, ANY],
            scratch_shapes=[pltpu.VMEM((N_CHIP, W_SLAB, D_MODEL), BF16), pltpu.VMEM((W_SLAB, D_MODEL), BF16),
                            pltpu.VMEM((W_SLAB, D_MODEL), BF16),
                            pltpu.SemaphoreType.DMA((N_CHIP,)), pltpu.SemaphoreType.DMA((N_CHIP,)),
                            pltpu.SemaphoreType.DMA, pltpu.SemaphoreType.DMA,
                            pltpu.SemaphoreType.DMA]),
        out_shape=[jax.ShapeDtypeStruct((3, W_SLAB, D_MODEL), BF16),
                   jax.ShapeDtypeStruct((N_CHIP, W_SLAB, D_MODEL), BF16), slab],
        compiler_params=_params(dimension_semantics=("arbitrary",)),
    )(order, jnp.asarray(_piece_columns()), *([dz] * 8), u)
    return kept, via


def _du_rms(dz, slabs, tail, h_pad, dout, g, kept, via):
    tm = 272
    steps = T // tm
    columns = _piece_columns().reshape(N_DEV, 8)

    def body(dz_ref, w_ref, t_ref, h_ref, d_ref, g_ref, kept_ref, via_ref, dh_ref, dg_ref,
             direct_ref, got_ref, via_buf, mine_buf, send_sems, recv_sems, local_sems):
        i = pl.program_id(0)
        x, y, c = _coords()

        def sends(core):
            first, second = _routes(core)
            return [pltpu.make_async_remote_copy(
                        src_ref=kept_ref.at[0], dst_ref=direct_ref, send_sem=send_sems.at[0],
                        recv_sem=recv_sems.at[0], device_id=(*first, core), device_id_type=MESH),
                    pltpu.make_async_remote_copy(
                        src_ref=mine_buf, dst_ref=got_ref, send_sem=send_sems.at[1],
                        recv_sem=recv_sems.at[1], device_id=(*second, core), device_id_type=MESH)]

        @pl.when(i == 0)
        def _():
            loads = [pltpu.make_async_copy(kept_ref.at[1], mine_buf, local_sems.at[0]),
                     pltpu.make_async_copy(via_ref, via_buf, local_sems.at[1])]
            for cp in loads:
                cp.start()
            for cp in loads:
                cp.wait()
            mine_buf[...] = (mine_buf[...].astype(F32) + via_buf[...].astype(F32)).astype(BF16)
            for core in (0, 1):
                @pl.when(c == core)
                def _(core=core):
                    for cp in sends(core):
                        cp.start()
            dg_ref[...] = jnp.zeros_like(dg_ref)

        du = _dot(dz_ref[:, pl.ds(FF_BASE, 128)], t_ref[...])
        for d in range(N_DEV):
            cols = jnp.concatenate([dz_ref[:, pl.ds(128 * int(columns[d, j]), 128)] for j in range(7)], axis=1)
            du = du + _dot(cols, w_ref[d, pl.ds(0, W_STRIDE), :])
        h = h_ref[...]
        r = lax.rsqrt(jnp.mean(h * h, axis=-1, keepdims=True) + EPS)
        xn = h * r
        dg_ref[...] += jnp.sum(du * xn, axis=0, keepdims=True)
        dn = du * g_ref[...]
        dh_ref[...] = d_ref[...] + r * (dn - xn * jnp.mean(dn * xn, axis=-1, keepdims=True))

        for core in (0, 1):
            @pl.when(jnp.logical_and(c == core, i == steps - 1))
            def _(core=core):
                for cp in sends(core):
                    cp.wait_recv()
                    cp.wait_send()

    tile = pl.BlockSpec((tm, D_MODEL), lambda i: (i, 0))
    slab = jax.ShapeDtypeStruct(via.shape, via.dtype)
    return pl.pallas_call(
        body, name="du_rms", grid=(steps,),
        in_specs=[pl.BlockSpec((tm, D_IN_PAD), lambda i: (i, 0)),
                  pl.BlockSpec((N_DEV, W_SLAB, D_MODEL), lambda i: (0, 0, 0)),
                  pl.BlockSpec((128, D_MODEL), lambda i: (0, 0)),
                  tile, tile, pl.BlockSpec((1, D_MODEL), lambda i: (0, 0)), ANY, ANY],
        out_specs=[tile, pl.BlockSpec((1, D_MODEL), lambda i: (0, 0)), ANY, ANY],
        out_shape=[jax.ShapeDtypeStruct((T, D_MODEL), F32),
                   jax.ShapeDtypeStruct((1, D_MODEL), F32), slab, slab],
        scratch_shapes=[pltpu.VMEM(via.shape, via.dtype), pltpu.VMEM(via.shape, via.dtype),
                        pltpu.SemaphoreType.DMA((2,)), pltpu.SemaphoreType.DMA((2,)),
                        pltpu.SemaphoreType.DMA((2,))],
        compiler_params=_params(dimension_semantics=("arbitrary",)),
    )(dz, slabs, tail, h_pad, dout, g, kept, via)


def _tri(lower):
    r = lax.broadcasted_iota(jnp.int32, (CHUNK, CHUNK), 0)
    c = lax.broadcasted_iota(jnp.int32, (CHUNK, CHUNK), 1)
    return jnp.where((r >= c) if lower else (r <= c), 1.0, 0.0).astype(F32)


def _row_valid(n):
    r = lax.broadcasted_iota(jnp.int32, (CHUNK, 128), 0) + n * CHUNK
    return r >= PAD


_FF_SPEC = pl.BlockSpec((T, 128), lambda i: (0, FF_BASE // 128))
_ZFF_SPEC = pl.BlockSpec((T, 128), lambda i: (0, 0))


def _forget_fwd(z, b_pad):
    def body(z_ref, b_ref, o_ref):
        tri = _tri(True)
        carry = jnp.zeros((1, 128), F32)
        for n in range(NCHUNK):
            rows = pl.ds(n * CHUNK, CHUNK)
            a = z_ref[rows, :] + b_ref[...]
            lf = -(jnp.maximum(-a, 0.0) + jnp.log(1.0 + jnp.exp(-jnp.abs(a))))
            lf = jnp.where(_row_valid(n), lf, 0.0)
            c = jnp.dot(tri, lf, precision=lax.Precision.HIGHEST,
                        preferred_element_type=F32) + carry
            carry = c[CHUNK - 1:CHUNK, :]
            o_ref[:, rows] = jnp.where(_row_valid(n), c * (-LOG2E), NEG_INF).T

    return pl.pallas_call(
        body, name="forget_fwd", grid=(1,),
        in_specs=[_ZFF_SPEC, pl.BlockSpec((1, 128), lambda i: (0, 0))],
        out_specs=pl.BlockSpec((128, T), lambda i: (0, 0)),
        out_shape=jax.ShapeDtypeStruct((128, T), F32),
        compiler_params=_params(dimension_semantics=("arbitrary",)),
    )(z, b_pad)


def _forget_bwd(z, b_pad, dc, dz):
    def body(z_ref, b_ref, dc_ref, dz_in, dff_ref, db_ref):
        tri = _tri(False)
        carry = jnp.zeros((1, 128), F32)
        db = jnp.zeros((1, 128), F32)
        for n in reversed(range(NCHUNK)):
            rows = pl.ds(n * CHUNK, CHUNK)
            dc_blk = jnp.concatenate([dc_ref[:, rows], jnp.zeros((128 - FOX_HEADS, CHUNK), F32)], axis=0).T
            dlf = jnp.dot(tri, dc_blk, precision=lax.Precision.HIGHEST,
                          preferred_element_type=F32) + carry
            carry = dlf[0:1, :]
            a = z_ref[rows, :] + b_ref[...]
            dff = jnp.where(_row_valid(n), dlf * jax.nn.sigmoid(-a), 0.0)
            dff_ref[rows, :] = dff.astype(BF16)
            db = db + jnp.sum(dff, axis=0, keepdims=True)
        db_ref[...] = db

    return pl.pallas_call(
        body, name="forget_bwd", grid=(1,),
        in_specs=[_ZFF_SPEC, pl.BlockSpec((1, 128), lambda i: (0, 0)),
                  pl.BlockSpec((FOX_HEADS, T), lambda i: (0, 0)), ANY],
        out_specs=[_FF_SPEC, pl.BlockSpec((1, 128), lambda i: (0, 0))],
        out_shape=[jax.ShapeDtypeStruct((T, D_IN_PAD), BF16),
                   jax.ShapeDtypeStruct((1, 128), F32)],
        input_output_aliases={3: 0},
        compiler_params=_params(dimension_semantics=("arbitrary",)),
    )(z, b_pad, dc, dz)


FOX_QB = 512
FOX_NQB = SEQ // FOX_QB


def _fox_block(b):
    lo = CHUNK + b * FOX_QB
    return pl.ds(lo, FOX_QB), lo, lo + FOX_QB


def _causal_bias():
    r = lax.broadcasted_iota(jnp.int32, (FOX_QB, FOX_QB), 0)
    c = lax.broadcasted_iota(jnp.int32, (FOX_QB, FOX_QB), 1)
    return jnp.where(c <= r, 0.0, NEG_INF).astype(F32)


def _fox_logits(q_blk, k_all, bias, causal, b):
    _, lo, hi = _fox_block(b)
    here = bias[:, lo:lo + 1]
    s_off = _dot_nt(q_blk, k_all(0, lo)) + (bias[:, :lo] - here)
    s_dia = _dot_nt(q_blk, k_all(lo, hi)) + ((bias[:, lo:hi] - here) + causal)
    return s_off, s_dia


_FOX_Z_SPEC = pl.BlockSpec((T, FOX_W), lambda p: (0, FOX_BASE // FOX_W + p))
_FOX_BIAS_SPEC = pl.BlockSpec((2, 1, T), lambda p: (p, 0, 0))
_FOX_LSE_SPEC = pl.BlockSpec((2, T, 1), lambda p: (p, 0, 0))
_FOX_SCALE = FOX_D ** -0.5
_FOX_QSCALE = _FOX_SCALE * LOG2E


def _fox_fwd(z, bias, y, w_out_blk):
    last = FOX_PAIRS - 1

    def body(z_ref, b_ref, y_in, w_ref, a_ref, lse_ref, y_ref, wall_ref,
             send_sems, recv_sems, local_sems):
        start, wait = _direct_exchange([w_ref], [wall_ref], send_sems, recv_sems, local_sems, True)
        pl.when(pl.program_id(0) == 0)(start)

        causal = _causal_bias()
        a_ref[pl.ds(0, CHUNK), :] = jnp.zeros((CHUNK, 128), F32)
        y_ref[pl.ds(0, CHUNK), :] = jnp.zeros((CHUNK, 128), BF16)
        for j in range(2):
            lanes = pl.ds(j * FOX_D, FOX_D)
            k_all = lambda a, b, j=j: z_ref[pl.ds(a, b - a), pl.ds(128 + j * FOX_D, FOX_D)]
            v_all = lambda a, b, j=j: z_ref[pl.ds(a, b - a), pl.ds(256 + j * FOX_D, FOX_D)]
            bias = b_ref[j]
            lse_ref[j, pl.ds(0, CHUNK), :] = jnp.zeros((CHUNK, 1), F32)
            for b in range(FOX_NQB):
                rows, lo, hi = _fox_block(b)
                q_blk = (z_ref[rows, lanes].astype(F32) * _FOX_QSCALE).astype(BF16)
                s_off, s_dia = _fox_logits(q_blk, k_all, bias, causal, b)
                m = jnp.maximum(jnp.max(s_off, axis=-1, keepdims=True),
                                jnp.max(s_dia, axis=-1, keepdims=True))
                e_off = jnp.exp2(s_off - m)
                e_dia = jnp.exp2(s_dia - m)
                total = jnp.sum(e_off, axis=-1, keepdims=True) + jnp.sum(e_dia, axis=-1, keepdims=True)
                o = (_dot(e_off.astype(BF16), v_all(0, lo)) + _dot(e_dia.astype(BF16), v_all(lo, hi))) / total
                a_ref[rows, lanes] = o
                lse_ref[j, rows, :] = m + jnp.log(total) * LOG2E
                gate = _silu_parts(z_ref[rows, pl.ds(384 + j * FOX_D, FOX_D)].astype(F32))[0]
                y_ref[rows, lanes] = (o * gate).astype(BF16)

        pl.when(pl.program_id(0) == last)(wait)

    return pl.pallas_call(
        body, name="fox_fwd", grid=(FOX_PAIRS,),
        in_specs=[_FOX_Z_SPEC, _FOX_BIAS_SPEC, ANY, ANY],
        out_specs=[pl.BlockSpec((T, 128), lambda p: (0, p)), _FOX_LSE_SPEC,
                   pl.BlockSpec((T, 128), lambda p: (0, 8 + p)), ANY],
        out_shape=[jax.ShapeDtypeStruct((T, FOX_HEADS * FOX_D), F32),
                   jax.ShapeDtypeStruct((FOX_HEADS, T, 1), F32),
                   jax.ShapeDtypeStruct((T, D_MIX), BF16),
                   _exchange_shape(w_out_blk, True)],
        input_output_aliases={2: 2},
        scratch_shapes=_exchange_sems(1),
        compiler_params=_params(dimension_semantics=("arbitrary",)),
    )(z, bias, y, w_out_blk)


def _fox_bwd(z, bias, a_f, lse, dy, dz, dwo_blocks):
    last = FOX_PAIRS - 1

    def body(z_ref, b_ref, a_ref, lse_ref, dy_ref, dz_in, dwo_ref, dz_ref, dc_ref, got_ref,
             kv_acc, dc_acc, send_sems, recv_sems, local_sems):
        start, wait = _direct_exchange([dwo_ref], [got_ref], send_sems, recv_sems, local_sems, False)
        pl.when(pl.program_id(0) == 0)(start)

        causal = _causal_bias()
        dz_ref[pl.ds(0, CHUNK), pl.ds(0, 128)] = jnp.zeros((CHUNK, 128), BF16)
        dz_ref[pl.ds(0, CHUNK), pl.ds(384, 128)] = jnp.zeros((CHUNK, 128), BF16)
        dk_rows, dv_rows = pl.ds(0, FOX_D), pl.ds(FOX_D, FOX_D)
        for j in range(2):
            lanes = pl.ds(j * FOX_D, FOX_D)
            k_all = lambda a, b, j=j: z_ref[pl.ds(a, b - a), pl.ds(128 + j * FOX_D, FOX_D)]
            v_all = lambda a, b, j=j: z_ref[pl.ds(a, b - a), pl.ds(256 + j * FOX_D, FOX_D)]
            bias = b_ref[j]
            kv_acc[...] = jnp.zeros_like(kv_acc)
            dc_acc[...] = jnp.zeros_like(dc_acc)
            for b in range(FOX_NQB):
                rows, lo, hi = _fox_block(b)
                off, dia = pl.ds(0, lo), pl.ds(lo, FOX_QB)
                q_blk = (z_ref[rows, lanes].astype(F32) * _FOX_QSCALE).astype(BF16)
                s_off, s_dia = _fox_logits(q_blk, k_all, bias, causal, b)
                lse_blk = lse_ref[j, rows, :]
                p_off, p_dia = jnp.exp2(s_off - lse_blk), jnp.exp2(s_dia - lse_blk)
                sg, dsg = _silu_parts(z_ref[rows, pl.ds(384 + j * FOX_D, FOX_D)].astype(F32))
                dyj = dy_ref[rows, lanes].astype(F32)
                dz_ref[rows, pl.ds(384 + j * FOX_D, FOX_D)] = (dyj * a_ref[rows, lanes] * dsg).astype(BF16)
                do_b = (dyj * sg).astype(BF16)
                dp_off = _dot_nt(do_b, v_all(0, lo))
                dp_dia = _dot_nt(do_b, v_all(lo, hi))
                d = (jnp.sum(p_off * dp_off, axis=-1, keepdims=True)
                     + jnp.sum(p_dia * dp_dia, axis=-1, keepdims=True))
                ds_off = p_off * (dp_off - d)
                ds_dia = p_dia * (dp_dia - d)
                dc_acc[:, off] -= jnp.sum(ds_off, axis=0, keepdims=True)
                dc_acc[:, dia] -= jnp.sum(ds_dia, axis=0, keepdims=True)
                ds_off_b, ds_dia_b = ds_off.astype(BF16), ds_dia.astype(BF16)
                dq = _dot(ds_off_b, k_all(0, lo)) + _dot(ds_dia_b, k_all(lo, hi))
                dz_ref[rows, lanes] = (dq * _FOX_SCALE).astype(BF16)
                kv_acc[dk_rows, off] += _dot_tn(q_blk, ds_off_b)
                kv_acc[dk_rows, dia] += _dot_tn(q_blk, ds_dia_b)
                kv_acc[dv_rows, off] += _dot_tn(do_b, p_off.astype(BF16))
                kv_acc[dv_rows, dia] += _dot_tn(do_b, p_dia.astype(BF16))
            for n in range(NCHUNK):
                rows = pl.ds(n * CHUNK, CHUNK)
                both = kv_acc[:, rows].T
                dz_ref[rows, pl.ds(128 + j * FOX_D, FOX_D)] = (both[:, :FOX_D] * LN2).astype(BF16)
                dz_ref[rows, pl.ds(256 + j * FOX_D, FOX_D)] = both[:, FOX_D:].astype(BF16)
            dc_ref[j] = dc_acc[...]

        pl.when(pl.program_id(0) == last)(wait)

    col = lambda base: pl.BlockSpec((T, 128), lambda p: (0, base + p))
    return pl.pallas_call(
        body, name="fox_bwd", grid=(FOX_PAIRS,),
        in_specs=[_FOX_Z_SPEC, _FOX_BIAS_SPEC, col(0), _FOX_LSE_SPEC, col(8), ANY, ANY],
        out_specs=[_FOX_Z_SPEC, _FOX_BIAS_SPEC, ANY],
        out_shape=[jax.ShapeDtypeStruct((T, D_IN_PAD), BF16),
                   jax.ShapeDtypeStruct((FOX_HEADS, 1, T), F32),
                   _exchange_shape(dwo_blocks, False)],
        input_output_aliases={5: 0},
        scratch_shapes=[pltpu.VMEM((2 * FOX_D, T), F32), pltpu.VMEM((1, T), F32)] + _exchange_sems(1),
        compiler_params=_params(dimension_semantics=("arbitrary",)),
    )(z, bias, a_f, lse, dy, dz, dwo_blocks)


def _rot(x, cosf, sins):
    return x * cosf + pltpu.roll(x, RET_DK // 2, 1) * sins


def _rot_t(d, cosf, sins):
    return d * cosf - pltpu.roll(d, RET_DK // 2, 1) * sins


_RET_Z_SPEC = pl.BlockSpec((T, RET_W), lambda h: (0, h))
_RET_TABLE_SPECS = [
    pl.BlockSpec((T, RET_DK), lambda h: (0, 0)),
    pl.BlockSpec((T, RET_DK), lambda h: (0, 0)),
    pl.BlockSpec((1, CHUNK, CHUNK), lambda h: (h, 0, 0)),
    pl.BlockSpec((1, CHUNK, 1), lambda h: (h, 0, 0)),
    pl.BlockSpec((1, CHUNK, 1), lambda h: (h, 0, 0)),
    pl.BlockSpec((1, 1, 1), lambda h: (h, 0, 0)),
]
_RQ, _RK = pl.ds(0, RET_DK), pl.ds(RET_DK, RET_DK)
_RV, _RG = pl.ds(2 * RET_DK, RET_DV), pl.ds(2 * RET_DK + RET_DV, RET_DV)
_RET_KSCALE = RET_DK ** -0.5


_RET_STATE_SPEC = pl.BlockSpec((1, NCHUNK, RET_DK, RET_DV), lambda h: (h, 0, 0, 0))


def _ret_fwd_head(z_ref, cos_ref, sin_ref, dm_ref, zeta_ref, xi_ref, cd_ref, raw_ref, y_ref, st_ref):
    dmask, zeta, xi, cdec = dm_ref[0], zeta_ref[0], xi_ref[0], cd_ref[0]
    state = jnp.zeros((RET_DK, RET_DV), F32)
    for n in range(NCHUNK):
        rows = pl.ds(n * CHUNK, CHUNK)
        cosf, sins = cos_ref[rows, :], sin_ref[rows, :]
        qr = _rot(z_ref[rows, _RQ].astype(F32), cosf, sins)
        kr_b = (_rot(z_ref[rows, _RK].astype(F32), cosf, sins) * _RET_KSCALE).astype(BF16)
        v_b = z_ref[rows, _RV]
        a = _dot_nt(qr.astype(BF16), kr_b) * dmask
        state_b = state.astype(BF16)
        st_ref[0, n] = state_b
        out = _dot(a.astype(BF16), v_b) + _dot((qr * xi).astype(BF16), state_b)
        state = state * cdec + _dot_tn(kr_b, (v_b.astype(F32) * zeta).astype(BF16))
        raw_ref[rows, :] = out
        r = lax.rsqrt(jnp.mean(out * out, axis=-1, keepdims=True) + EPS)
        y_ref[rows, :] = (out * r * _silu_parts(z_ref[rows, _RG].astype(F32))[0]).astype(BF16)


def _ret_bwd(z, tables, raw, states, dy):
    def body(z_ref, cos_ref, sin_ref, dm_ref, zeta_ref, xi_ref, cd_ref, raw_ref, st_all, dy_ref, dz_ref):
        dmask, zeta, xi, cdec = dm_ref[0], zeta_ref[0], xi_ref[0], cd_ref[0]
        st_ref = st_all.at[0]

        grad_state = jnp.zeros((RET_DK, RET_DV), F32)
        for n in reversed(range(NCHUNK)):
            rows = pl.ds(n * CHUNK, CHUNK)
            cosf, sins = cos_ref[rows, :], sin_ref[rows, :]
            qr = _rot(z_ref[rows, _RQ].astype(F32), cosf, sins)
            kr_b = (_rot(z_ref[rows, _RK].astype(F32), cosf, sins) * _RET_KSCALE).astype(BF16)
            qr_b = qr.astype(BF16)
            v_b = z_ref[rows, _RV]
            gs_b = grad_state.astype(BF16)
            o = raw_ref[rows, :]
            r = lax.rsqrt(jnp.mean(o * o, axis=-1, keepdims=True) + EPS)
            hn = o * r
            sg, dsg = _silu_parts(z_ref[rows, _RG].astype(F32))
            dyn = dy_ref[rows, :].astype(F32)
            dz_ref[rows, _RG] = (dyn * hn * dsg).astype(BF16)
            dhn = dyn * sg
            do_b = (r * (dhn - hn * jnp.mean(dhn * hn, axis=-1, keepdims=True))).astype(BF16)
            a_b = (_dot_nt(qr_b, kr_b) * dmask).astype(BF16)
            da_b = (_dot_nt(do_b, v_b) * dmask).astype(BF16)
            dqr = _dot(da_b, kr_b) + xi * _dot_nt(do_b, st_ref[n])
            dkr = _dot_tn(da_b, qr_b) + zeta * _dot_nt(v_b, gs_b)
            dv = _dot_tn(a_b, do_b) + zeta * _dot(kr_b, gs_b)
            grad_state = grad_state * cdec + _dot_tn((qr * xi).astype(BF16), do_b)
            dz_ref[rows, _RQ] = _rot_t(dqr, cosf, sins).astype(BF16)
            dz_ref[rows, _RK] = (_rot_t(dkr, cosf, sins) * _RET_KSCALE).astype(BF16)
            dz_ref[rows, _RV] = dv.astype(BF16)

    wide = pl.BlockSpec((T, RET_DV), lambda h: (0, h))
    return pl.pallas_call(
        body, name="ret_bwd", grid=(RET_HEADS,),
        in_specs=[_RET_Z_SPEC] + _RET_TABLE_SPECS + [wide, _RET_STATE_SPEC, wide],
        out_specs=_RET_Z_SPEC,
        out_shape=jax.ShapeDtypeStruct((T, D_IN_PAD), BF16),
        compiler_params=_params(dimension_semantics=("arbitrary",)),
    )(z, *tables, raw, states, dy)


def _adamw(w, g, m, v):
    m = ADAM_B1 * m + (1.0 - ADAM_B1) * g
    v = ADAM_B2 * v + (1.0 - ADAM_B2) * (g * g)
    m_hat = m / (1.0 - ADAM_B1 ** ADAM_STEP)
    v_hat = v / (1.0 - ADAM_B2 ** ADAM_STEP)
    delta = -ADAM_LR * (m_hat / (jnp.sqrt(v_hat) + ADAM_EPS) + ADAM_WD * w)
    return delta, m, v


def _sum_adamw(parts, w, m, v, rows, name):
    _, r_tot, cols = parts.shape
    assert r_tot % rows == 0

    def body(p_ref, w_ref, m_ref, v_ref, g_ref, d_ref, nm_ref, nv_ref):
        g = p_ref[0].astype(F32)
        for d in range(1, N_DEV):
            g = g + p_ref[d].astype(F32)
        delta, nm, nv = _adamw(w_ref[...], g, m_ref[...], v_ref[...])
        g_ref[...] = g
        d_ref[...] = delta
        nm_ref[...] = nm
        nv_ref[...] = nv

    blk = pl.BlockSpec((rows, cols), lambda i: (i, 0))
    return pl.pallas_call(
        body, name=name, grid=(r_tot // rows,),
        in_specs=[pl.BlockSpec((N_DEV, rows, cols), lambda i: (0, i, 0)), blk, blk, blk],
        out_specs=[blk] * 4,
        out_shape=[jax.ShapeDtypeStruct((r_tot, cols), F32)] * 4,
        compiler_params=_params(dimension_semantics=("arbitrary",)),
    )(parts, w, m, v)


def _sum_adamw_w_in(parts, w, m, v, small):
    n_part, r, c = parts.shape
    steps = c // 128

    def body(p_ref, w_hbm, m_hbm, v_hbm, s_ref, g_hbm, d_hbm, nm_hbm, nv_hbm, got_ref,
             in_buf, out_buf, in_sems, out_sems, send_sems, recv_sems, local_sems):
        start, wait = _direct_exchange([s_ref], [got_ref], send_sems, recv_sems, local_sems, True)
        i = pl.program_id(0)
        pl.when(i == 0)(start)
        slot = i % 2

        def loads(step, into):
            cols = pl.ds(pl.multiple_of(step * 128, 128), 128)
            return [pltpu.make_async_copy(h.at[:, 0, cols], in_buf.at[into, k], in_sems.at[into, k])
                    for k, h in enumerate((w_hbm, m_hbm, v_hbm))]

        cols = pl.ds(pl.multiple_of(i * 128, 128), 128)
        stores = [pltpu.make_async_copy(out_buf.at[k], h.at[:, 0, cols], out_sems.at[k])
                  for k, h in enumerate((g_hbm, d_hbm, nm_hbm, nv_hbm))]

        @pl.when(i == 0)
        def _():
            for cp in loads(0, 0):
                cp.start()

        @pl.when(i + 1 < steps)
        def _():
            for cp in loads(i + 1, 1 - slot):
                cp.start()

        g = p_ref[0].astype(F32)
        for d in range(1, n_part):
            g = g + p_ref[d].astype(F32)
        for cp in loads(i, slot):
            cp.wait()
        delta, nm, nv = _adamw(in_buf[slot, 0], g, in_buf[slot, 1], in_buf[slot, 2])

        @pl.when(i > 0)
        def _():
            for cp in stores:
                cp.wait()

        for k, val in enumerate((g, delta, nm, nv)):
            out_buf[k] = val
        for cp in stores:
            cp.start()

        @pl.when(i == steps - 1)
        def _():
            for cp in stores:
                cp.wait()
            wait()

    return pl.pallas_call(
        body, name="adamw_w_in", grid=(steps,),
        in_specs=[pl.BlockSpec((n_part, r, 128), lambda i: (0, 0, i)), ANY, ANY, ANY, ANY],
        out_specs=[ANY] * 5,
        out_shape=[jax.ShapeDtypeStruct((r, 1, c), F32)] * 4 + [_exchange_shape(small, True)],
        scratch_shapes=[pltpu.VMEM((2, 3, r, 128), F32), pltpu.VMEM((4, r, 128), F32),
                        pltpu.SemaphoreType.DMA((2, 3)), pltpu.SemaphoreType.DMA((4,))] + _exchange_sems(1),
        compiler_params=_params(dimension_semantics=("arbitrary",)),
    )(parts, w, m, v, small)


def _adamw_small(got, me, metas, norms, finals, biases):
    def body(me_ref, gm_ref, gr_ref, *refs):
        ins, outs = refs[:12], refs[12:]
        g_meta, g_rest = gm_ref[0], gr_ref[0]
        for d in range(1, N_DEV):
            g_meta, g_rest = g_meta + gm_ref[d], g_rest + gr_ref[d]
        grads = [g_meta, g_rest[0:1], g_rest[1:2], g_rest[2:3, :FOX_HEADS]]
        for k, g in enumerate(grads):
            w_ref, m_ref, v_ref = ins[3 * k:3 * k + 3]
            delta, new_m, new_v = _adamw(w_ref[...], g, m_ref[...], v_ref[...])
            for o_ref, val in zip(outs[4 * k:4 * k + 4], (g, delta, new_m, new_v)):
                o_ref[...] = val
        outs[16][...] = g_rest[3:4, :128]

    groups = (metas, norms, finals, biases)
    full = lambda a: pl.BlockSpec(a.shape, lambda i, me_ref: (0,) * a.ndim)
    flat = [a for grp in groups for a in grp]
    res = pl.pallas_call(
        body, name="adamw_small",
        grid_spec=pltpu.PrefetchScalarGridSpec(
            num_scalar_prefetch=1, grid=(1,),
            in_specs=[pl.BlockSpec((N_DEV, N_META, META_BLK), lambda i, me_ref: (0, 0, me_ref[0])),
                      pl.BlockSpec((N_DEV, 8, D_MODEL), lambda i, me_ref: (0, N_META // 8, 0))]
            + [full(a) for a in flat],
            out_specs=[full(grp[0]) for grp in groups for _ in range(4)]
            + [pl.BlockSpec((1, 128), lambda i, me_ref: (0, 0))]),
        out_shape=[jax.ShapeDtypeStruct(grp[0].shape, F32) for grp in groups for _ in range(4)]
        + [jax.ShapeDtypeStruct((1, 128), F32)],
        compiler_params=_params(dimension_semantics=("arbitrary",)),
    )(me, got, got, *flat)
    return [res[4 * k:4 * k + 4] for k in range(4)], res[16]


def kernel(x, meta_tokens, norm_g, w_in, b_f, w_out, final_g, loss_target, m_meta_tokens, m_norm_g, m_w_in, m_b_f, m_w_out, m_final_g, v_meta_tokens, v_norm_g, v_w_in, v_b_f, v_w_out, v_final_g):
    core = lax.axis_index("c")
    me = 4 * lax.axis_index("x") + 2 * lax.axis_index("y") + core
    tables = _tables()

    wt_all, meta_all = _gather_two_level([_slab(w_in[0].T.astype(BF16), me), meta_tokens], name="gather_w_in")
    slabs, tail = _join_edges(wt_all)
    meta_full = jnp.transpose(meta_all, (1, 0, 2)).reshape(N_META, D_MODEL)
    b_pad = jnp.pad(b_f, ((0, 0), (0, 128 - FOX_HEADS)))

    h_pad, u, z, raw, y, states = _rms_z_ret(x[0], meta_full, norm_g, slabs, tables)
    z, zff = _z_fox(u, slabs, tail, z)
    bias = _forget_fwd(zff, b_pad)[:FOX_HEADS].reshape(FOX_HEADS, 1, T)
    a_f, lse, y, w_out_all = _fox_fwd(z, bias, y, w_out[0].astype(BF16))
    w_out_b = w_out_all.reshape(D_MIX, D_MODEL)
    dout, dout_b, dy, loss_blk, d_final_g = _out_loss_dy(y, w_out_b, h_pad, loss_target[0],
                                                         final_g.reshape(1, D_MODEL))

    d_w_out = _mm_tn(y, dout_b, tm=D_MIX, tn=256, name="mm_dwout")
    dz = _ret_bwd(z, tables, raw, states, dy)
    dz, dc, got_w_out = _fox_bwd(z, bias, a_f, lse, dy, dz, d_w_out.reshape(N_DEV, WO_BLK, D_MODEL))
    dz, db_f = _forget_bwd(zff, b_pad, dc.reshape(FOX_HEADS, T), dz)

    kept, via = _dwin_pair_slabs(dz, u, core)
    dh, d_norm_g, direct, summed = _du_rms(dz, slabs, tail, h_pad, dout, norm_g, kept, via)
    got_w_in = _unslab([kept[2], direct, summed], me)

    small = jnp.concatenate([
        dh[PAD:CHUNK], d_norm_g, d_final_g, jnp.pad(db_f[:, :FOX_HEADS], ((0, 0), (0, D_MODEL - FOX_HEADS))),
        jnp.pad(loss_blk[0:1], ((0, 0), (0, D_MODEL - 128))),
        jnp.zeros((SMALL_ROWS - N_META - 4, D_MODEL), F32)], axis=0)
    fore = lambda a: jnp.transpose(a, (2, 0, 1))
    g_w_in, d_w_in, nm_w_in, nv_w_in, got_small = _sum_adamw_w_in(
        got_w_in, fore(w_in), fore(m_w_in), fore(v_w_in), small)
    g_w_out, d_w_out, nm_w_out, nv_w_out = _sum_adamw(got_w_out, w_out[0], m_w_out[0], v_w_out[0], 128, "adamw_w_out")

    row = lambda a: a.reshape(1, D_MODEL)
    (meta_o, norm_o, final_o, bias_o), loss_row = _adamw_small(
        got_small, me.astype(jnp.int32).reshape(1),
        (meta_tokens, m_meta_tokens, v_meta_tokens), (norm_g, m_norm_g, v_norm_g),
        (row(final_g), row(m_final_g), row(v_final_g)), (b_f, m_b_f, v_b_f))
    final_o = [a.reshape(D_MODEL) for a in final_o]

    back = lambda a: jnp.transpose(a, (1, 2, 0))
    outs = [[meta_o[k], norm_o[k], back(wk), bias_o[k], ok[None], final_o[k]]
            for k, (wk, ok) in enumerate(zip((g_w_in, d_w_in, nm_w_in, nv_w_in),
                                             (g_w_out, d_w_out, nm_w_out, nv_w_out)))]
    return (loss_row[0, 0], dh[CHUNK:][None], *outs[0], *outs[1], *outs[2], *outs[3])
```
